```python
import jax, jax.numpy as jnp
from jax import lax
import numpy as np

D_MODEL = 1024
BATCH = 8
SEQ = 2048
DEPTH = 4

N_MIXERS = 3
MEM_LEN = 256
EPS = 1e-6
ROPE_THETA = 10000.0
MAX_POS_OFFSET = 4096

MLA_HEADS = 8
MLA_NOPE = 128
MLA_ROPE = 64
MLA_V = 128
MLA_Q_RANK = 384
MLA_KV_RANK = 256
Q_BLOCK = 128

GDN_HEADS = 8
GDN_DK = 128
GDN_DV = 128
GDN_CONV = 4
GDN_CHUNK = 64
GDN_QKV = GDN_HEADS * (2 * GDN_DK + GDN_DV)
GDN_PROJ = GDN_QKV + GDN_HEADS * GDN_DV + 2 * GDN_HEADS

SC_WIDTH = D_MODEL
SC_CONV = 3

X_HEADS = 4
X_HEAD_DIM = D_MODEL // X_HEADS

D_FF = 4 * D_MODEL

N_A = (DEPTH + 2) // N_MIXERS
N_B = (DEPTH + 1) // N_MIXERS
N_C = DEPTH // N_MIXERS

kernel_name = "hybrid_mla_gdn_shortconv_memxattn"


def rms_norm(x, g):
    xf = x.astype(jnp.float32)
    y = xf * lax.rsqrt(jnp.mean(xf * xf, axis=-1, keepdims=True) + EPS)
    return (y * g.astype(jnp.float32)).astype(x.dtype)


def rope_tables(positions):
    inv_freq = ROPE_THETA ** (-jnp.arange(0, MLA_ROPE, 2, dtype=jnp.float32) / MLA_ROPE)
    ang = positions.astype(jnp.float32)[..., None] * inv_freq
    return jnp.cos(ang), jnp.sin(ang)


def apply_rope(x, cos, sin):
    c = cos[:, :, None, :]
    s = sin[:, :, None, :]
    x1, x2 = jnp.split(x.astype(jnp.float32), 2, axis=-1)
    return jnp.concatenate([x1 * c - x2 * s, x2 * c + x1 * s], axis=-1).astype(x.dtype)


def causal_depthwise_conv(x, w):
    k, c = w.shape
    return lax.conv_general_dilated(
        x, w[:, None, :].astype(x.dtype), window_strides=(1,), padding=[(k - 1, 0)],
        dimension_numbers=("NWC", "WIO", "NWC"), feature_group_count=c)


def mla_mixer(h, cos, sin, w_in, q_norm, kv_norm, w_uq, w_ukv, w_o):
    b, s, _ = h.shape
    z = h @ w_in
    c_q, c_kv, k_rope = jnp.split(z, [MLA_Q_RANK, MLA_Q_RANK + MLA_KV_RANK], axis=-1)
    q = (rms_norm(c_q, q_norm) @ w_uq).reshape(b, s, MLA_HEADS, MLA_NOPE + MLA_ROPE)
    q_nope = q[..., :MLA_NOPE]
    q_rope = apply_rope(q[..., MLA_NOPE:], cos, sin)
    kv = (rms_norm(c_kv, kv_norm) @ w_ukv).reshape(b, s, MLA_HEADS, MLA_NOPE + MLA_V)
    k_nope, v = kv[..., :MLA_NOPE], kv[..., MLA_NOPE:]
    k_rope = apply_rope(k_rope[:, :, None, :], cos, sin)[:, :, 0, :]
    scale = (MLA_NOPE + MLA_ROPE) ** -0.5
    outs = []
    for start in range(0, s, Q_BLOCK):
        end = start + Q_BLOCK
        sc = (jnp.einsum("bqhd,bkhd->bhqk", q_nope[:, start:end], k_nope[:, :end])
              + jnp.einsum("bqhr,bkr->bhqk", q_rope[:, start:end], k_rope[:, :end]))
        sc = sc.astype(jnp.float32) * scale
        mask = (start + jnp.arange(Q_BLOCK))[:, None] >= jnp.arange(end)[None, :]
        sc = jnp.where(mask, sc, -jnp.inf)
        p = jax.nn.softmax(sc, axis=-1).astype(v.dtype)
        outs.append(jnp.einsum("bhqk,bkhd->bqhd", p, v[:, :end]))
    o = jnp.concatenate(outs, axis=1).reshape(b, s, MLA_HEADS * MLA_V)
    return o @ w_o


def chunk_gated_delta_rule(q, k, v, g, beta):
    b, s, h, dk = q.shape
    dv = v.shape[-1]
    c = GDN_CHUNK
    n = s // c

    def to_chunks(t):
        t = t.astype(jnp.float32).reshape((b, n, c, h) + t.shape[3:])
        return jnp.moveaxis(t, (1, 3), (0, 2))

    qc, kc, vc = to_chunks(q), to_chunks(k), to_chunks(v)
    gc = lax.cumsum(to_chunks(g), axis=3)
    bc = to_chunks(beta)
    tri = jnp.tril(jnp.ones((c, c), dtype=bool))
    strict = jnp.tril(jnp.ones((c, c), dtype=bool), -1)
    decay = jnp.exp(jnp.where(tri, gc[..., :, None] - gc[..., None, :], -jnp.inf))
    k_beta = kc * bc[..., None]
    m = jnp.where(strict, jnp.einsum("nbhid,nbhjd->nbhij", k_beta, kc) * decay, 0.0)
    eye = jnp.eye(c, dtype=jnp.float32)
    t_inv = lax.linalg.triangular_solve(eye + m, jnp.broadcast_to(eye, m.shape),
                                        left_side=True, lower=True, unit_diagonal=True)
    u = t_inv @ (vc * bc[..., None])
    w = t_inv @ (k_beta * jnp.exp(gc)[..., None])
    attn_intra = jnp.einsum("nbhid,nbhjd->nbhij", qc, kc) * decay

    def step(state, xs):
        q_i, k_i, u_i, w_i, g_i, a_i = xs
        v_new = u_i - w_i @ state
        o_i = (q_i * jnp.exp(g_i)[..., None]) @ state + a_i @ v_new
        g_last = g_i[..., -1:]
        state = (state * jnp.exp(g_last)[..., None]
                 + jnp.einsum("bhcd,bhce->bhde", k_i * jnp.exp(g_last - g_i)[..., None], v_new))
        return state, o_i

    s0 = jnp.zeros((b, h, dk, dv), jnp.float32)
    _, o = lax.scan(step, s0, (qc, kc, u, w, gc, attn_intra))
    return jnp.moveaxis(o, (0, 2), (1, 3)).reshape(b, s, h, dv)


def gdn_mixer(h, w_in, conv_w, a_log, dt_bias, o_norm, w_o):
    b, s, _ = h.shape
    z = h @ w_in
    qkv, gate, beta_logit, a_logit = jnp.split(
        z, [GDN_QKV, GDN_QKV + GDN_HEADS * GDN_DV, GDN_QKV + GDN_HEADS * GDN_DV + GDN_HEADS], axis=-1)
    qkv = jax.nn.silu(causal_depthwise_conv(qkv, conv_w))
    q, k, v = jnp.split(qkv, [GDN_HEADS * GDN_DK, 2 * GDN_HEADS * GDN_DK], axis=-1)
    q = q.reshape(b, s, GDN_HEADS, GDN_DK).astype(jnp.float32)
    k = k.reshape(b, s, GDN_HEADS, GDN_DK).astype(jnp.float32)
    v = v.reshape(b, s, GDN_HEADS, GDN_DV)
    q = q * lax.rsqrt(jnp.sum(q * q, -1, keepdims=True) + EPS) * (GDN_DK ** -0.5)
    k = k * lax.rsqrt(jnp.sum(k * k, -1, keepdims=True) + EPS)
    beta = jax.nn.sigmoid(beta_logit.astype(jnp.float32))
    g = -jnp.exp(a_log.astype(jnp.float32)) * jax.nn.softplus(
        a_logit.astype(jnp.float32) + dt_bias.astype(jnp.float32))
    o = chunk_gated_delta_rule(q, k, v, g, beta)
    o = rms_norm(o, o_norm) * jax.nn.silu(gate.reshape(b, s, GDN_HEADS, GDN_DV).astype(jnp.float32))
    return o.reshape(b, s, GDN_HEADS * GDN_DV).astype(h.dtype) @ w_o


def short_conv_mixer(h, w_in, conv_w, w_o):
    z = h @ w_in
    b_gate, c_gate, u = jnp.split(z, 3, axis=-1)
    y = b_gate * causal_depthwise_conv(c_gate * u, conv_w)
    return y @ w_o


def memory_cross_attention(h, mem_n, w_q, w_kv, w_o):
    b, s, _ = h.shape
    m = mem_n.shape[1]
    q = (h @ w_q).reshape(b, s, X_HEADS, X_HEAD_DIM)
    k, v = jnp.split(mem_n @ w_kv, 2, axis=-1)
    k = k.reshape(b, m, X_HEADS, X_HEAD_DIM)
    v = v.reshape(b, m, X_HEADS, X_HEAD_DIM)
    sc = jnp.einsum("bqhd,bkhd->bhqk", q, k).astype(jnp.float32) * (X_HEAD_DIM ** -0.5)
    p = jax.nn.softmax(sc, axis=-1).astype(v.dtype)
    o = jnp.einsum("bhqk,bkhd->bqhd", p, v).reshape(b, s, X_HEADS * X_HEAD_DIM)
    return o @ w_o


def relu2_mlp(h, w1, w2):
    return jnp.square(jax.nn.relu(h @ w1)) @ w2


def _fwd_setup_inputs(seed: int = 0) -> dict:
    key = jax.random.key(seed)
    ks = iter(jax.random.split(key, 40))

    def w(shape, fan_in):
        return jax.random.normal(next(ks), shape, jnp.float32) * (fan_in ** -0.5)

    def gain(shape):
        return 1.0 + 0.02 * jax.random.normal(next(ks), shape, jnp.float32)

    x = jax.random.normal(next(ks), (BATCH, SEQ, D_MODEL), jnp.float32)
    mem = jax.random.normal(next(ks), (BATCH, MEM_LEN, D_MODEL), jnp.float32)
    offsets = jax.random.randint(next(ks), (BATCH, 1), 0, MAX_POS_OFFSET, dtype=jnp.int32)
    positions = offsets + jnp.arange(SEQ, dtype=jnp.int32)[None, :]

    mla_w_in = w((N_A, D_MODEL, MLA_Q_RANK + MLA_KV_RANK + MLA_ROPE), D_MODEL)
    mla_q_norm = gain((N_A, MLA_Q_RANK))
    mla_kv_norm = gain((N_A, MLA_KV_RANK))
    mla_w_uq = w((N_A, MLA_Q_RANK, MLA_HEADS * (MLA_NOPE + MLA_ROPE)), MLA_Q_RANK)
    mla_w_ukv = w((N_A, MLA_KV_RANK, MLA_HEADS * (MLA_NOPE + MLA_V)), MLA_KV_RANK)
    mla_w_o = w((N_A, MLA_HEADS * MLA_V, D_MODEL), MLA_HEADS * MLA_V)

    gdn_w_in = w((N_B, D_MODEL, GDN_PROJ), D_MODEL)
    gdn_conv_w = w((N_B, GDN_CONV, GDN_QKV), GDN_CONV)
    gdn_a_log = jnp.log(jax.random.uniform(next(ks), (N_B, GDN_HEADS), jnp.float32, 1.0, 16.0))
    dt = jnp.exp(jax.random.uniform(next(ks), (N_B, GDN_HEADS), jnp.float32,
                                    float(np.log(1e-3)), float(np.log(1e-1))))
    gdn_dt_bias = dt + jnp.log(-jnp.expm1(-dt))
    gdn_o_norm = gain((N_B, GDN_DV))
    gdn_w_o = w((N_B, GDN_HEADS * GDN_DV, D_MODEL), GDN_HEADS * GDN_DV)

    sc_w_in = w((N_C, D_MODEL, 3 * SC_WIDTH), D_MODEL)
    sc_conv_w = w((N_C, SC_CONV, SC_WIDTH), SC_CONV)
    sc_w_o = w((N_C, SC_WIDTH, D_MODEL), SC_WIDTH)

    norm_mix = gain((DEPTH, D_MODEL))
    norm_mem = gain((DEPTH, D_MODEL))
    norm_mlp = gain((DEPTH, D_MODEL))
    xa_w_q = w((DEPTH, D_MODEL, X_HEADS * X_HEAD_DIM), D_MODEL)
    xa_w_kv = w((DEPTH, D_MODEL, 2 * X_HEADS * X_HEAD_DIM), D_MODEL)
    xa_w_o = w((DEPTH, X_HEADS * X_HEAD_DIM, D_MODEL), X_HEADS * X_HEAD_DIM)
    mlp_w1 = w((DEPTH, D_MODEL, D_FF), D_MODEL)
    mlp_w2 = w((DEPTH, D_FF, D_MODEL), D_FF)
    mem_norm = gain((D_MODEL,))
    final_norm = gain((D_MODEL,))

    return {
        "x": x, "mem": mem, "positions": positions,
        "mla_w_in": mla_w_in, "mla_q_norm": mla_q_norm, "mla_kv_norm": mla_kv_norm,
        "mla_w_uq": mla_w_uq, "mla_w_ukv": mla_w_ukv, "mla_w_o": mla_w_o,
        "gdn_w_in": gdn_w_in, "gdn_conv_w": gdn_conv_w, "gdn_a_log": gdn_a_log,
        "gdn_dt_bias": gdn_dt_bias, "gdn_o_norm": gdn_o_norm, "gdn_w_o": gdn_w_o,
        "sc_w_in": sc_w_in, "sc_conv_w": sc_conv_w, "sc_w_o": sc_w_o,
        "norm_mix": norm_mix, "norm_mem": norm_mem, "norm_mlp": norm_mlp,
        "xa_w_q": xa_w_q, "xa_w_kv": xa_w_kv, "xa_w_o": xa_w_o,
        "mlp_w1": mlp_w1, "mlp_w2": mlp_w2,
        "mem_norm": mem_norm, "final_norm": final_norm,
    }


def _fwd_reference(x, mem, positions,
              mla_w_in, mla_q_norm, mla_kv_norm, mla_w_uq, mla_w_ukv, mla_w_o,
              gdn_w_in, gdn_conv_w, gdn_a_log, gdn_dt_bias, gdn_o_norm, gdn_w_o,
              sc_w_in, sc_conv_w, sc_w_o,
              norm_mix, norm_mem, norm_mlp,
              xa_w_q, xa_w_kv, xa_w_o,
              mlp_w1, mlp_w2,
              mem_norm, final_norm):
    cos, sin = rope_tables(positions)
    mem_n = rms_norm(mem, mem_norm)
    for i in range(DEPTH):
        j = i // N_MIXERS
        kind = i % N_MIXERS
        h = rms_norm(x, norm_mix[i])
        if kind == 0:
            y = mla_mixer(h, cos, sin, mla_w_in[j], mla_q_norm[j], mla_kv_norm[j],
                          mla_w_uq[j], mla_w_ukv[j], mla_w_o[j])
        elif kind == 1:
            y = gdn_mixer(h, gdn_w_in[j], gdn_conv_w[j], gdn_a_log[j], gdn_dt_bias[j],
                          gdn_o_norm[j], gdn_w_o[j])
        else:
            y = short_conv_mixer(h, sc_w_in[j], sc_conv_w[j], sc_w_o[j])
        x = x + y
        x = x + memory_cross_attention(rms_norm(x, norm_mem[i]), mem_n,
                                       xa_w_q[i], xa_w_kv[i], xa_w_o[i])
        x = x + relu2_mlp(rms_norm(x, norm_mlp[i]), mlp_w1[i], mlp_w2[i])
    return rms_norm(x, final_norm)


import jax as _jax
import jax.numpy as _jnp

TWIN_FORMAT = 'train_step'
FWD_PARAMS = ['x', 'mem', 'positions', 'mla_w_in', 'mla_q_norm', 'mla_kv_norm', 'mla_w_uq', 'mla_w_ukv', 'mla_w_o', 'gdn_w_in', 'gdn_conv_w', 'gdn_a_log', 'gdn_dt_bias', 'gdn_o_norm', 'gdn_w_o', 'sc_w_in', 'sc_conv_w', 'sc_w_o', 'norm_mix', 'norm_mem', 'norm_mlp', 'xa_w_q', 'xa_w_kv', 'xa_w_o', 'mlp_w1', 'mlp_w2', 'mem_norm', 'final_norm']
TWIN_WEIGHTS = ['mla_w_in', 'mla_q_norm', 'mla_kv_norm', 'mla_w_uq', 'mla_w_ukv', 'mla_w_o', 'gdn_w_in', 'gdn_conv_w', 'gdn_a_log', 'gdn_dt_bias', 'gdn_o_norm', 'gdn_w_o', 'sc_w_in', 'sc_conv_w', 'sc_w_o', 'norm_mix', 'norm_mem', 'norm_mlp', 'xa_w_q', 'xa_w_kv', 'xa_w_o', 'mlp_w1', 'mlp_w2', 'mem_norm', 'final_norm']
TWIN_DIFF_INPUT = 'x'
TWIN_INPUTS = ['x', 'mem', 'positions', 'mla_w_in', 'mla_q_norm', 'mla_kv_norm', 'mla_w_uq', 'mla_w_ukv', 'mla_w_o', 'gdn_w_in', 'gdn_conv_w', 'gdn_a_log', 'gdn_dt_bias', 'gdn_o_norm', 'gdn_w_o', 'sc_w_in', 'sc_conv_w', 'sc_w_o', 'norm_mix', 'norm_mem', 'norm_mlp', 'xa_w_q', 'xa_w_kv', 'xa_w_o', 'mlp_w1', 'mlp_w2', 'mem_norm', 'final_norm', 'loss_target', 'm_mla_w_in', 'm_mla_q_norm', 'm_mla_kv_norm', 'm_mla_w_uq', 'm_mla_w_ukv', 'm_mla_w_o', 'm_gdn_w_in', 'm_gdn_conv_w', 'm_gdn_a_log', 'm_gdn_dt_bias', 'm_gdn_o_norm', 'm_gdn_w_o', 'm_sc_w_in', 'm_sc_conv_w', 'm_sc_w_o', 'm_norm_mix', 'm_norm_mem', 'm_norm_mlp', 'm_xa_w_q', 'm_xa_w_kv', 'm_xa_w_o', 'm_mlp_w1', 'm_mlp_w2', 'm_mem_norm', 'm_final_norm', 'v_mla_w_in', 'v_mla_q_norm', 'v_mla_kv_norm', 'v_mla_w_uq', 'v_mla_w_ukv', 'v_mla_w_o', 'v_gdn_w_in', 'v_gdn_conv_w', 'v_gdn_a_log', 'v_gdn_dt_bias', 'v_gdn_o_norm', 'v_gdn_w_o', 'v_sc_w_in', 'v_sc_conv_w', 'v_sc_w_o', 'v_norm_mix', 'v_norm_mem', 'v_norm_mlp', 'v_xa_w_q', 'v_xa_w_kv', 'v_xa_w_o', 'v_mlp_w1', 'v_mlp_w2', 'v_mem_norm', 'v_final_norm']
TWIN_OUTPUTS = ['loss', 'grad_x', 'grad_mla_w_in', 'grad_mla_q_norm', 'grad_mla_kv_norm', 'grad_mla_w_uq', 'grad_mla_w_ukv', 'grad_mla_w_o', 'grad_gdn_w_in', 'grad_gdn_conv_w', 'grad_gdn_a_log', 'grad_gdn_dt_bias', 'grad_gdn_o_norm', 'grad_gdn_w_o', 'grad_sc_w_in', 'grad_sc_conv_w', 'grad_sc_w_o', 'grad_norm_mix', 'grad_norm_mem', 'grad_norm_mlp', 'grad_xa_w_q', 'grad_xa_w_kv', 'grad_xa_w_o', 'grad_mlp_w1', 'grad_mlp_w2', 'grad_mem_norm', 'grad_final_norm', 'delta_mla_w_in', 'delta_mla_q_norm', 'delta_mla_kv_norm', 'delta_mla_w_uq', 'delta_mla_w_ukv', 'delta_mla_w_o', 'delta_gdn_w_in', 'delta_gdn_conv_w', 'delta_gdn_a_log', 'delta_gdn_dt_bias', 'delta_gdn_o_norm', 'delta_gdn_w_o', 'delta_sc_w_in', 'delta_sc_conv_w', 'delta_sc_w_o', 'delta_norm_mix', 'delta_norm_mem', 'delta_norm_mlp', 'delta_xa_w_q', 'delta_xa_w_kv', 'delta_xa_w_o', 'delta_mlp_w1', 'delta_mlp_w2', 'delta_mem_norm', 'delta_final_norm', 'new_m_mla_w_in', 'new_m_mla_q_norm', 'new_m_mla_kv_norm', 'new_m_mla_w_uq', 'new_m_mla_w_ukv', 'new_m_mla_w_o', 'new_m_gdn_w_in', 'new_m_gdn_conv_w', 'new_m_gdn_a_log', 'new_m_gdn_dt_bias', 'new_m_gdn_o_norm', 'new_m_gdn_w_o', 'new_m_sc_w_in', 'new_m_sc_conv_w', 'new_m_sc_w_o', 'new_m_norm_mix', 'new_m_norm_mem', 'new_m_norm_mlp', 'new_m_xa_w_q', 'new_m_xa_w_kv', 'new_m_xa_w_o', 'new_m_mlp_w1', 'new_m_mlp_w2', 'new_m_mem_norm', 'new_m_final_norm', 'new_v_mla_w_in', 'new_v_mla_q_norm', 'new_v_mla_kv_norm', 'new_v_mla_w_uq', 'new_v_mla_w_ukv', 'new_v_mla_w_o', 'new_v_gdn_w_in', 'new_v_gdn_conv_w', 'new_v_gdn_a_log', 'new_v_gdn_dt_bias', 'new_v_gdn_o_norm', 'new_v_gdn_w_o', 'new_v_sc_w_in', 'new_v_sc_conv_w', 'new_v_sc_w_o', 'new_v_norm_mix', 'new_v_norm_mem', 'new_v_norm_mlp', 'new_v_xa_w_q', 'new_v_xa_w_kv', 'new_v_xa_w_o', 'new_v_mlp_w1', 'new_v_mlp_w2', 'new_v_mem_norm', 'new_v_final_norm']
TWIN_LEAF_KINDS = {'loss': 'loss', 'grad_x': 'grad_x', 'grad_mla_w_in': 'grad_w', 'grad_mla_q_norm': 'grad_w', 'grad_mla_kv_norm': 'grad_w', 'grad_mla_w_uq': 'grad_w', 'grad_mla_w_ukv': 'grad_w', 'grad_mla_w_o': 'grad_w', 'grad_gdn_w_in': 'grad_w', 'grad_gdn_conv_w': 'grad_w', 'grad_gdn_a_log': 'grad_w', 'grad_gdn_dt_bias': 'grad_w', 'grad_gdn_o_norm': 'grad_w', 'grad_gdn_w_o': 'grad_w', 'grad_sc_w_in': 'grad_w', 'grad_sc_conv_w': 'grad_w', 'grad_sc_w_o': 'grad_w', 'grad_norm_mix': 'grad_w', 'grad_norm_mem': 'grad_w', 'grad_norm_mlp': 'grad_w', 'grad_xa_w_q': 'grad_w', 'grad_xa_w_kv': 'grad_w', 'grad_xa_w_o': 'grad_w', 'grad_mlp_w1': 'grad_w', 'grad_mlp_w2': 'grad_w', 'grad_mem_norm': 'grad_w', 'grad_final_norm': 'grad_w', 'delta_mla_w_in': 'delta_w', 'delta_mla_q_norm': 'delta_w', 'delta_mla_kv_norm': 'delta_w', 'delta_mla_w_uq': 'delta_w', 'delta_mla_w_ukv': 'delta_w', 'delta_mla_w_o': 'delta_w', 'delta_gdn_w_in': 'delta_w', 'delta_gdn_conv_w': 'delta_w', 'delta_gdn_a_log': 'delta_w', 'delta_gdn_dt_bias': 'delta_w', 'delta_gdn_o_norm': 'delta_w', 'delta_gdn_w_o': 'delta_w', 'delta_sc_w_in': 'delta_w', 'delta_sc_conv_w': 'delta_w', 'delta_sc_w_o': 'delta_w', 'delta_norm_mix': 'delta_w', 'delta_norm_mem': 'delta_w', 'delta_norm_mlp': 'delta_w', 'delta_xa_w_q': 'delta_w', 'delta_xa_w_kv': 'delta_w', 'delta_xa_w_o': 'delta_w', 'delta_mlp_w1': 'delta_w', 'delta_mlp_w2': 'delta_w', 'delta_mem_norm': 'delta_w', 'delta_final_norm': 'delta_w', 'new_m_mla_w_in': 'new_m', 'new_m_mla_q_norm': 'new_m', 'new_m_mla_kv_norm': 'new_m', 'new_m_mla_w_uq': 'new_m', 'new_m_mla_w_ukv': 'new_m', 'new_m_mla_w_o': 'new_m', 'new_m_gdn_w_in': 'new_m', 'new_m_gdn_conv_w': 'new_m', 'new_m_gdn_a_log': 'new_m', 'new_m_gdn_dt_bias': 'new_m', 'new_m_gdn_o_norm': 'new_m', 'new_m_gdn_w_o': 'new_m', 'new_m_sc_w_in': 'new_m', 'new_m_sc_conv_w': 'new_m', 'new_m_sc_w_o': 'new_m', 'new_m_norm_mix': 'new_m', 'new_m_norm_mem': 'new_m', 'new_m_norm_mlp': 'new_m', 'new_m_xa_w_q': 'new_m', 'new_m_xa_w_kv': 'new_m', 'new_m_xa_w_o': 'new_m', 'new_m_mlp_w1': 'new_m', 'new_m_mlp_w2': 'new_m', 'new_m_mem_norm': 'new_m', 'new_m_final_norm': 'new_m', 'new_v_mla_w_in': 'new_v', 'new_v_mla_q_norm': 'new_v', 'new_v_mla_kv_norm': 'new_v', 'new_v_mla_w_uq': 'new_v', 'new_v_mla_w_ukv': 'new_v', 'new_v_mla_w_o': 'new_v', 'new_v_gdn_w_in': 'new_v', 'new_v_gdn_conv_w': 'new_v', 'new_v_gdn_a_log': 'new_v', 'new_v_gdn_dt_bias': 'new_v', 'new_v_gdn_o_norm': 'new_v', 'new_v_gdn_w_o': 'new_v', 'new_v_sc_w_in': 'new_v', 'new_v_sc_conv_w': 'new_v', 'new_v_sc_w_o': 'new_v', 'new_v_norm_mix': 'new_v', 'new_v_norm_mem': 'new_v', 'new_v_norm_mlp': 'new_v', 'new_v_xa_w_q': 'new_v', 'new_v_xa_w_kv': 'new_v', 'new_v_xa_w_o': 'new_v', 'new_v_mlp_w1': 'new_v', 'new_v_mlp_w2': 'new_v', 'new_v_mem_norm': 'new_v', 'new_v_final_norm': 'new_v'}


def _forward(args):
    return _fwd_reference(*[args[k] for k in FWD_PARAMS])


def _output_shape():
    out = _jax.eval_shape(lambda: _forward(_fwd_setup_inputs(0)))
    return out.shape, out.dtype

N_MICROBATCH = 1
ADAM_LR = 0.001
ADAM_B1 = 0.9
ADAM_B2 = 0.999
ADAM_EPS = 1e-08
ADAM_WD = 0.01
ADAM_STEP = 10
PER_EXAMPLE_BATCH_AXIS = {'x': 0, 'mem': 0, 'positions': 0, 'loss_target': 0}
SHARED_INPUTS = []
_WEIGHT_DTYPES = {'mla_w_in': _jnp.float32, 'mla_q_norm': _jnp.float32, 'mla_kv_norm': _jnp.float32, 'mla_w_uq': _jnp.float32, 'mla_w_ukv': _jnp.float32, 'mla_w_o': _jnp.float32, 'gdn_w_in': _jnp.float32, 'gdn_conv_w': _jnp.float32, 'gdn_a_log': _jnp.float32, 'gdn_dt_bias': _jnp.float32, 'gdn_o_norm': _jnp.float32, 'gdn_w_o': _jnp.float32, 'sc_w_in': _jnp.float32, 'sc_conv_w': _jnp.float32, 'sc_w_o': _jnp.float32, 'norm_mix': _jnp.float32, 'norm_mem': _jnp.float32, 'norm_mlp': _jnp.float32, 'xa_w_q': _jnp.float32, 'xa_w_kv': _jnp.float32, 'xa_w_o': _jnp.float32, 'mlp_w1': _jnp.float32, 'mlp_w2': _jnp.float32, 'mem_norm': _jnp.float32, 'final_norm': _jnp.float32}
MOMENT_SCALE = {'mla_w_in': 5.702422e-02, 'mla_q_norm': 4.144706e-02, 'mla_kv_norm': 8.102973e-02, 'mla_w_uq': 2.026663e-02, 'mla_w_ukv': 2.771107e-02, 'mla_w_o': 3.338840e-02, 'gdn_w_in': 4.824624e-02, 'gdn_conv_w': 4.478358e-02, 'gdn_a_log': 2.538836e-01, 'gdn_dt_bias': 2.505917e-01, 'gdn_o_norm': 1.628294e-01, 'gdn_w_o': 5.772921e-02, 'sc_w_in': 5.621663e-02, 'sc_conv_w': 5.875355e-02, 'sc_w_o': 5.643659e-02, 'norm_mix': 7.512202e-02, 'norm_mem': 1.202192e-02, 'norm_mlp': 1.092732e-01, 'xa_w_q': 1.200517e-02, 'xa_w_kv': 1.219391e-02, 'xa_w_o': 1.238247e-02, 'mlp_w1': 5.365935e-02, 'mlp_w2': 9.940869e-02, 'mem_norm': 3.608285e-02, 'final_norm': 1.648009e+01}


def _to_microbatches(a, axis):
    t = _jnp.moveaxis(a, axis, 0)
    t = t.reshape((N_MICROBATCH, t.shape[0] // N_MICROBATCH) + t.shape[1:])
    return _jnp.moveaxis(t, 1, axis + 1)


def setup_inputs(seed: int = 0) -> dict:
    inp = _fwd_setup_inputs(seed)
    key = _jax.random.fold_in(_jax.random.key(seed), 7919)
    shape, _ = _output_shape()
    out = dict(inp)
    out["loss_target"] = _jax.random.normal(_jax.random.fold_in(key, 0), shape, _jnp.float32)
    for i, name in enumerate(TWIN_WEIGHTS):
        w = inp[name].astype(_jnp.float32)
        if MOMENT_SCALE is None:
            s = _jnp.sqrt(_jnp.mean(_jnp.square(w)) + 1e-30)
        else:
            s = MOMENT_SCALE[name]
        km, kv = _jax.random.split(_jax.random.fold_in(key, i + 1))
        out[name] = w
        out["m_" + name] = s * _jax.random.normal(km, w.shape, _jnp.float32)
        out["v_" + name] = (s * s) * _jax.random.uniform(kv, w.shape, _jnp.float32, 0.5, 1.5)
    if N_MICROBATCH > 1:
        for name, axis in PER_EXAMPLE_BATCH_AXIS.items():
            out[name] = _to_microbatches(out[name], axis)
    return {'x': out['x'], 'mem': out['mem'], 'positions': out['positions'], 'mla_w_in': out['mla_w_in'], 'mla_q_norm': out['mla_q_norm'], 'mla_kv_norm': out['mla_kv_norm'], 'mla_w_uq': out['mla_w_uq'], 'mla_w_ukv': out['mla_w_ukv'], 'mla_w_o': out['mla_w_o'], 'gdn_w_in': out['gdn_w_in'], 'gdn_conv_w': out['gdn_conv_w'], 'gdn_a_log': out['gdn_a_log'], 'gdn_dt_bias': out['gdn_dt_bias'], 'gdn_o_norm': out['gdn_o_norm'], 'gdn_w_o': out['gdn_w_o'], 'sc_w_in': out['sc_w_in'], 'sc_conv_w': out['sc_conv_w'], 'sc_w_o': out['sc_w_o'], 'norm_mix': out['norm_mix'], 'norm_mem': out['norm_mem'], 'norm_mlp': out['norm_mlp'], 'xa_w_q': out['xa_w_q'], 'xa_w_kv': out['xa_w_kv'], 'xa_w_o': out['xa_w_o'], 'mlp_w1': out['mlp_w1'], 'mlp_w2': out['mlp_w2'], 'mem_norm': out['mem_norm'], 'final_norm': out['final_norm'], 'loss_target': out['loss_target'], 'm_mla_w_in': out['m_mla_w_in'], 'm_mla_q_norm': out['m_mla_q_norm'], 'm_mla_kv_norm': out['m_mla_kv_norm'], 'm_mla_w_uq': out['m_mla_w_uq'], 'm_mla_w_ukv': out['m_mla_w_ukv'], 'm_mla_w_o': out['m_mla_w_o'], 'm_gdn_w_in': out['m_gdn_w_in'], 'm_gdn_conv_w': out['m_gdn_conv_w'], 'm_gdn_a_log': out['m_gdn_a_log'], 'm_gdn_dt_bias': out['m_gdn_dt_bias'], 'm_gdn_o_norm': out['m_gdn_o_norm'], 'm_gdn_w_o': out['m_gdn_w_o'], 'm_sc_w_in': out['m_sc_w_in'], 'm_sc_conv_w': out['m_sc_conv_w'], 'm_sc_w_o': out['m_sc_w_o'], 'm_norm_mix': out['m_norm_mix'], 'm_norm_mem': out['m_norm_mem'], 'm_norm_mlp': out['m_norm_mlp'], 'm_xa_w_q': out['m_xa_w_q'], 'm_xa_w_kv': out['m_xa_w_kv'], 'm_xa_w_o': out['m_xa_w_o'], 'm_mlp_w1': out['m_mlp_w1'], 'm_mlp_w2': out['m_mlp_w2'], 'm_mem_norm': out['m_mem_norm'], 'm_final_norm': out['m_final_norm'], 'v_mla_w_in': out['v_mla_w_in'], 'v_mla_q_norm': out['v_mla_q_norm'], 'v_mla_kv_norm': out['v_mla_kv_norm'], 'v_mla_w_uq': out['v_mla_w_uq'], 'v_mla_w_ukv': out['v_mla_w_ukv'], 'v_mla_w_o': out['v_mla_w_o'], 'v_gdn_w_in': out['v_gdn_w_in'], 'v_gdn_conv_w': out['v_gdn_conv_w'], 'v_gdn_a_log': out['v_gdn_a_log'], 'v_gdn_dt_bias': out['v_gdn_dt_bias'], 'v_gdn_o_norm': out['v_gdn_o_norm'], 'v_gdn_w_o': out['v_gdn_w_o'], 'v_sc_w_in': out['v_sc_w_in'], 'v_sc_conv_w': out['v_sc_conv_w'], 'v_sc_w_o': out['v_sc_w_o'], 'v_norm_mix': out['v_norm_mix'], 'v_norm_mem': out['v_norm_mem'], 'v_norm_mlp': out['v_norm_mlp'], 'v_xa_w_q': out['v_xa_w_q'], 'v_xa_w_kv': out['v_xa_w_kv'], 'v_xa_w_o': out['v_xa_w_o'], 'v_mlp_w1': out['v_mlp_w1'], 'v_mlp_w2': out['v_mlp_w2'], 'v_mem_norm': out['v_mem_norm'], 'v_final_norm': out['v_final_norm']}


def _loss(weights, diff, rest, loss_target):
    with _jax.named_scope("forward"):
        args = {**rest, TWIN_DIFF_INPUT: diff, **{k: w.astype(_WEIGHT_DTYPES[k]) for k, w in weights.items()}}
        y = _forward(args)
    with _jax.named_scope("loss_head"):
        err = _jnp.square(y.astype(_jnp.float32) - loss_target)
        return 0.5 * _jnp.sum(_jnp.mean(err, axis=-1)) if err.ndim else 0.5 * err


def _adamw(w, g, m, v):
    m = ADAM_B1 * m + (1.0 - ADAM_B1) * g
    v = ADAM_B2 * v + (1.0 - ADAM_B2) * _jnp.square(g)
    m_hat = m / (1.0 - ADAM_B1 ** ADAM_STEP)
    v_hat = v / (1.0 - ADAM_B2 ** ADAM_STEP)
    delta = -ADAM_LR * (m_hat / (_jnp.sqrt(v_hat) + ADAM_EPS) + ADAM_WD * w)
    return delta, m, v


def reference(x, mem, positions, mla_w_in, mla_q_norm, mla_kv_norm, mla_w_uq, mla_w_ukv, mla_w_o, gdn_w_in, gdn_conv_w, gdn_a_log, gdn_dt_bias, gdn_o_norm, gdn_w_o, sc_w_in, sc_conv_w, sc_w_o, norm_mix, norm_mem, norm_mlp, xa_w_q, xa_w_kv, xa_w_o, mlp_w1, mlp_w2, mem_norm, final_norm, loss_target, m_mla_w_in, m_mla_q_norm, m_mla_kv_norm, m_mla_w_uq, m_mla_w_ukv, m_mla_w_o, m_gdn_w_in, m_gdn_conv_w, m_gdn_a_log, m_gdn_dt_bias, m_gdn_o_norm, m_gdn_w_o, m_sc_w_in, m_sc_conv_w, m_sc_w_o, m_norm_mix, m_norm_mem, m_norm_mlp, m_xa_w_q, m_xa_w_kv, m_xa_w_o, m_mlp_w1, m_mlp_w2, m_mem_norm, m_final_norm, v_mla_w_in, v_mla_q_norm, v_mla_kv_norm, v_mla_w_uq, v_mla_w_ukv, v_mla_w_o, v_gdn_w_in, v_gdn_conv_w, v_gdn_a_log, v_gdn_dt_bias, v_gdn_o_norm, v_gdn_w_o, v_sc_w_in, v_sc_conv_w, v_sc_w_o, v_norm_mix, v_norm_mem, v_norm_mlp, v_xa_w_q, v_xa_w_kv, v_xa_w_o, v_mlp_w1, v_mlp_w2, v_mem_norm, v_final_norm):
    given = dict(x=x, mem=mem, positions=positions, mla_w_in=mla_w_in, mla_q_norm=mla_q_norm, mla_kv_norm=mla_kv_norm, mla_w_uq=mla_w_uq, mla_w_ukv=mla_w_ukv, mla_w_o=mla_w_o, gdn_w_in=gdn_w_in, gdn_conv_w=gdn_conv_w, gdn_a_log=gdn_a_log, gdn_dt_bias=gdn_dt_bias, gdn_o_norm=gdn_o_norm, gdn_w_o=gdn_w_o, sc_w_in=sc_w_in, sc_conv_w=sc_conv_w, sc_w_o=sc_w_o, norm_mix=norm_mix, norm_mem=norm_mem, norm_mlp=norm_mlp, xa_w_q=xa_w_q, xa_w_kv=xa_w_kv, xa_w_o=xa_w_o, mlp_w1=mlp_w1, mlp_w2=mlp_w2, mem_norm=mem_norm, final_norm=final_norm, loss_target=loss_target, m_mla_w_in=m_mla_w_in, m_mla_q_norm=m_mla_q_norm, m_mla_kv_norm=m_mla_kv_norm, m_mla_w_uq=m_mla_w_uq, m_mla_w_ukv=m_mla_w_ukv, m_mla_w_o=m_mla_w_o, m_gdn_w_in=m_gdn_w_in, m_gdn_conv_w=m_gdn_conv_w, m_gdn_a_log=m_gdn_a_log, m_gdn_dt_bias=m_gdn_dt_bias, m_gdn_o_norm=m_gdn_o_norm, m_gdn_w_o=m_gdn_w_o, m_sc_w_in=m_sc_w_in, m_sc_conv_w=m_sc_conv_w, m_sc_w_o=m_sc_w_o, m_norm_mix=m_norm_mix, m_norm_mem=m_norm_mem, m_norm_mlp=m_norm_mlp, m_xa_w_q=m_xa_w_q, m_xa_w_kv=m_xa_w_kv, m_xa_w_o=m_xa_w_o, m_mlp_w1=m_mlp_w1, m_mlp_w2=m_mlp_w2, m_mem_norm=m_mem_norm, m_final_norm=m_final_norm, v_mla_w_in=v_mla_w_in, v_mla_q_norm=v_mla_q_norm, v_mla_kv_norm=v_mla_kv_norm, v_mla_w_uq=v_mla_w_uq, v_mla_w_ukv=v_mla_w_ukv, v_mla_w_o=v_mla_w_o, v_gdn_w_in=v_gdn_w_in, v_gdn_conv_w=v_gdn_conv_w, v_gdn_a_log=v_gdn_a_log, v_gdn_dt_bias=v_gdn_dt_bias, v_gdn_o_norm=v_gdn_o_norm, v_gdn_w_o=v_gdn_w_o, v_sc_w_in=v_sc_w_in, v_sc_conv_w=v_sc_conv_w, v_sc_w_o=v_sc_w_o, v_norm_mix=v_norm_mix, v_norm_mem=v_norm_mem, v_norm_mlp=v_norm_mlp, v_xa_w_q=v_xa_w_q, v_xa_w_kv=v_xa_w_kv, v_xa_w_o=v_xa_w_o, v_mlp_w1=v_mlp_w1, v_mlp_w2=v_mlp_w2, v_mem_norm=v_mem_norm, v_final_norm=v_final_norm)
    weights = {n: given[n] for n in TWIN_WEIGHTS}
    shared = {n: given[n] for n in SHARED_INPUTS}
    per_example = {n: given[n] for n in ['x', 'mem', 'positions']}
    grad_fn = _jax.value_and_grad(_loss, argnums=(0, 1))

    def one_microbatch(ex, loss_target):
        ex = dict(ex)
        diff = ex.pop(TWIN_DIFF_INPUT)
        return grad_fn(weights, diff, {**shared, **ex}, loss_target)

    if N_MICROBATCH == 1:
        loss, (grad_w, grad_x) = one_microbatch(per_example, given["loss_target"])
    else:
        def body(carry, xs):
            loss_sum, grad_sum = carry
            l_k, (gw_k, gx_k) = one_microbatch(xs[0], xs[1])
            with _jax.named_scope("update"):
                return (loss_sum + l_k, _jax.tree.map(_jnp.add, grad_sum, gw_k)), gx_k

        init = (_jnp.zeros((), _jnp.float32), _jax.tree.map(_jnp.zeros_like, weights))
        (loss, grad_w), grad_x = _jax.lax.scan(body, init, (per_example, given["loss_target"]))
    with _jax.named_scope("update"):
        delta_w, new_m, new_v = {}, {}, {}
        for n in TWIN_WEIGHTS:
            delta_w[n], new_m[n], new_v[n] = _adamw(weights[n], grad_w[n], given["m_" + n], given["v_" + n])
    return (loss, grad_x, *[grad_w[n] for n in TWIN_WEIGHTS], *[delta_w[n] for n in TWIN_WEIGHTS],
            *[new_m[n] for n in TWIN_WEIGHTS], *[new_v[n] for n in TWIN_WEIGHTS])
```

```python
import functools

import jax
import jax.numpy as jnp
from jax import lax
from jax.experimental import pallas as pl
from jax.experimental.pallas import tpu as pltpu

F32 = jnp.float32
BF16 = jnp.bfloat16
HI = lax.Precision.HIGHEST
MESH = pl.DeviceIdType.MESH

EPS = 1e-6
ROPE_THETA = 10000.0
N_CHIPS = 4
LANES = 128
VMEM_LIMIT = 48 * 1024 * 1024
NEG = -1e30

MLA_H, MLA_NOPE, MLA_ROPE, MLA_V = 8, 128, 64, 128
MLA_QR, MLA_KVR = 384, 256
MLA_ZPAD = 768
GDN_H, GDN_D, GDN_C = 8, 128, 64
XA_H, XA_D = 4, 256

ADAM_LR, ADAM_B1, ADAM_B2, ADAM_EPS, ADAM_WD, ADAM_STEP = 0.001, 0.9, 0.999, 1e-08, 0.01, 10

FLAT_ROWS = 16128
FLAT_COLS = 1024
HALF_ROWS = FLAT_ROWS // 2


def _cparams(sem=None):
    return pltpu.CompilerParams(dimension_semantics=sem, vmem_limit_bytes=VMEM_LIMIT)


def _pick(dim, pref):
    t = (min(pref, dim) // LANES) * LANES
    while t >= LANES:
        if dim % t == 0:
            return t
        t -= LANES
    return dim


_DIMS = {"nn": ((1,), (0,)), "nt": ((1,), (1,)), "tn": ((0,), (0,))}


def mm(a, b, mode, name, outs=(F32,), epi=None, extras=(), tm=512, tn=512, tk=512):
    if mode == "nn":
        (M, K), (K2, N) = a.shape, b.shape
    elif mode == "nt":
        (M, K), (N, K2) = a.shape, b.shape
    else:
        (K, M), (K2, N) = a.shape, b.shape
    assert K == K2, (name, a.shape, b.shape)
    tm, tn, tk = _pick(M, tm), _pick(N, tn), _pick(K, tk)
    nk = K // tk
    n_ex, n_out = len(extras), len(outs)
    dims = (_DIMS[mode], ((), ()))

    def body(*refs):
        a_ref, b_ref = refs[0], refs[1]
        ex_refs = refs[2:2 + n_ex]
        o_refs = refs[2 + n_ex:2 + n_ex + n_out]
        acc = refs[-1]
        k = pl.program_id(2)

        @pl.when(k == 0)
        def _():
            acc[...] = jnp.zeros_like(acc)

        acc[...] += lax.dot_general(a_ref[...].astype(BF16), b_ref[...].astype(BF16), dims,
                                    preferred_element_type=F32)

        @pl.when(k == nk - 1)
        def _():
            r = acc[...]
            res = epi(r, *[e[...] for e in ex_refs]) if epi is not None else (r,)
            for o_ref, v in zip(o_refs, res):
                o_ref[...] = v.astype(o_ref.dtype)

    if mode == "tn":
        a_spec = pl.BlockSpec((tk, tm), lambda i, j, k: (k, i))
    else:
        a_spec = pl.BlockSpec((tm, tk), lambda i, j, k: (i, k))
    if mode == "nt":
        b_spec = pl.BlockSpec((tn, tk), lambda i, j, k: (j, k))
    else:
        b_spec = pl.BlockSpec((tk, tn), lambda i, j, k: (k, j))
    mn_spec = pl.BlockSpec((tm, tn), lambda i, j, k: (i, j))
    res = pl.pallas_call(
        body, name=name, grid=(M // tm, N // tn, nk),
        in_specs=[a_spec, b_spec] + [mn_spec] * n_ex,
        out_specs=[mn_spec] * n_out,
        out_shape=[jax.ShapeDtypeStruct((M, N), d) for d in outs],
        scratch_shapes=[pltpu.VMEM((tm, tn), F32)],
        compiler_params=_cparams(("parallel", "parallel", "arbitrary")),
    )(a, b, *extras)
    return res[0] if n_out == 1 else tuple(res)


def _epi_add(acc, r):
    return (acc + r,)


def _epi_relu2(acc):
    r = jnp.maximum(acc, 0.0)
    return acc, r * r


def _epi_relu2_bwd(acc, h1):
    return (acc * (2.0 * jnp.maximum(h1.astype(F32), 0.0)),)


def _rms(x, g):
    return x * lax.rsqrt(jnp.mean(x * x, axis=-1, keepdims=True) + EPS) * g


def _row_spec(ts, cols):
    return pl.BlockSpec((ts, cols), lambda i: (i, 0))


def _par_spec(cols):
    return pl.BlockSpec((1, cols), lambda i: (0, 0))


def rmsnorm_fwd(x, g, name, ts=256):
    T, D = x.shape
    ts = min(ts, T)

    def body(x_ref, g_ref, o_ref):
        o_ref[...] = _rms(x_ref[...], g_ref[...]).astype(o_ref.dtype)

    return pl.pallas_call(
        body, name=name, grid=(T // ts,),
        in_specs=[_row_spec(ts, D), _par_spec(D)], out_specs=_row_spec(ts, D),
        out_shape=jax.ShapeDtypeStruct((T, D), BF16), compiler_params=_cparams(("parallel",)),
    )(x, g.reshape(1, D))


def rmsnorm_bwd(x, g, dy, dx_in, name, ts=256):
    T, D = x.shape
    ts = min(ts, T)

    def body(x_ref, g_ref, dy_ref, dxi_ref, dx_ref, dg_ref):
        xv = x_ref[...]
        r = lax.rsqrt(jnp.mean(xv * xv, axis=-1, keepdims=True) + EPS)
        xh = xv * r
        dyv = dy_ref[...].astype(F32)
        dxh = dyv * g_ref[...]
        dx_ref[...] = dxi_ref[...] + r * (dxh - xh * jnp.mean(dxh * xh, axis=-1, keepdims=True))
        dg = jnp.sum(dyv * xh, axis=0, keepdims=True)

        @pl.when(pl.program_id(0) == 0)
        def _():
            dg_ref[...] = jnp.zeros_like(dg_ref)

        dg_ref[...] += dg

    dx, dg = pl.pallas_call(
        body, name=name, grid=(T // ts,),
        in_specs=[_row_spec(ts, D), _par_spec(D), _row_spec(ts, D), _row_spec(ts, D)],
        out_specs=[_row_spec(ts, D), _par_spec(D)],
        out_shape=[jax.ShapeDtypeStruct((T, D), F32), jax.ShapeDtypeStruct((1, D), F32)],
        compiler_params=_cparams(("arbitrary",)),
    )(x, g.reshape(1, D), dy, dx_in)
    return dx, dg.reshape(D)


def rope_tables(pos, name="rope_tables"):
    T = pos.shape[0]
    half = MLA_ROPE // 2
    inv = ROPE_THETA ** (-jnp.arange(0, MLA_ROPE, 2, dtype=F32) / MLA_ROPE)
    inv_row = jnp.concatenate([inv, inv, jnp.zeros((LANES - MLA_ROPE,), F32)]).reshape(1, LANES)

    def body(p_ref, f_ref, c_ref, a_ref, b_ref):
        ang = p_ref[...].astype(F32) * f_ref[...]
        lane = lax.broadcasted_iota(jnp.int32, ang.shape, 1)
        c, s = jnp.cos(ang), jnp.sin(ang)
        c_ref[...] = jnp.where(lane < MLA_ROPE, c, 0.0)
        a_ref[...] = jnp.where(lane < half, -s, 0.0)
        b_ref[...] = jnp.where((lane >= half) & (lane < MLA_ROPE), s, 0.0)

    sh = jax.ShapeDtypeStruct((T, LANES), F32)
    return pl.pallas_call(body, name=name, out_shape=[sh, sh, sh], compiler_params=_cparams())(pos, inv_row)


def _roll_l(x):
    return pltpu.roll(x, LANES - MLA_ROPE // 2, 1)


def _roll_r(x):
    return pltpu.roll(x, MLA_ROPE // 2, 1)


def _rope(r, c, sa, sb):
    return r * c + _roll_l(r) * sa + _roll_r(r) * sb


def _rope_t(d, c, sa, sb):
    return d * c + _roll_r(d * sa) + _roll_l(d * sb)


def mla_mid_fwd(z, qn, kvn, tabs, name, ts=256):
    T = z.shape[0]
    ts = min(ts, T)
    a0, a1 = MLA_QR, MLA_QR + MLA_KVR

    def body(z_ref, qn_ref, kvn_ref, c_ref, sa_ref, sb_ref, cq_ref, ckv_ref, kr_ref):
        cq_ref[...] = _rms(z_ref[:, 0:a0], qn_ref[...]).astype(BF16)
        ckv_ref[...] = _rms(z_ref[:, a0:a1], kvn_ref[...]).astype(BF16)
        kr_ref[...] = _rope(z_ref[:, a1:MLA_ZPAD], c_ref[...], sa_ref[...], sb_ref[...]).astype(BF16)

    return pl.pallas_call(
        body, name=name, grid=(T // ts,),
        in_specs=[_row_spec(ts, MLA_ZPAD), _par_spec(MLA_QR), _par_spec(MLA_KVR)] + [_row_spec(ts, LANES)] * 3,
        out_specs=[_row_spec(ts, MLA_QR), _row_spec(ts, MLA_KVR), _row_spec(ts, LANES)],
        out_shape=[jax.ShapeDtypeStruct((T, MLA_QR), BF16), jax.ShapeDtypeStruct((T, MLA_KVR), BF16),
                   jax.ShapeDtypeStruct((T, LANES), BF16)],
        compiler_params=_cparams(("parallel",)),
    )(z, qn.reshape(1, -1), kvn.reshape(1, -1), *tabs)


def mla_mid_bwd(z, qn, kvn, tabs, dcq, dckv, dkr, name, ts=256):
    T = z.shape[0]
    ts = min(ts, T)
    a0, a1 = MLA_QR, MLA_QR + MLA_KVR

    def body(z_ref, qn_ref, kvn_ref, c_ref, sa_ref, sb_ref, dcq_ref, dckv_ref, dkr_ref, dz_ref, dqn_ref, dkvn_ref):
        _, vq = jax.vjp(_rms, z_ref[:, 0:a0], qn_ref[...])
        dzq, dqn = vq(dcq_ref[...].astype(F32))
        _, vk = jax.vjp(_rms, z_ref[:, a0:a1], kvn_ref[...])
        dzk, dkvn = vk(dckv_ref[...].astype(F32))
        dz_ref[:, 0:a0] = dzq.astype(dz_ref.dtype)
        dz_ref[:, a0:a1] = dzk.astype(dz_ref.dtype)
        dz_ref[:, a1:MLA_ZPAD] = _rope_t(dkr_ref[...].astype(F32), c_ref[...], sa_ref[...], sb_ref[...]).astype(dz_ref.dtype)

        @pl.when(pl.program_id(0) == 0)
        def _():
            dqn_ref[...] = jnp.zeros_like(dqn_ref)
            dkvn_ref[...] = jnp.zeros_like(dkvn_ref)

        dqn_ref[...] += dqn
        dkvn_ref[...] += dkvn

    dz, dqn, dkvn = pl.pallas_call(
        body, name=name, grid=(T // ts,),
        in_specs=[_row_spec(ts, MLA_ZPAD), _par_spec(MLA_QR), _par_spec(MLA_KVR)] + [_row_spec(ts, LANES)] * 3
        + [_row_spec(ts, MLA_QR), _row_spec(ts, MLA_KVR), _row_spec(ts, LANES)],
        out_specs=[_row_spec(ts, MLA_ZPAD), _par_spec(MLA_QR), _par_spec(MLA_KVR)],
        out_shape=[jax.ShapeDtypeStruct((T, MLA_ZPAD), BF16), jax.ShapeDtypeStruct((1, MLA_QR), F32),
                   jax.ShapeDtypeStruct((1, MLA_KVR), F32)],
        compiler_params=_cparams(("arbitrary",)),
    )(z, qn.reshape(1, -1), kvn.reshape(1, -1), *tabs, dcq, dckv, dkr)
    return dz, dqn.reshape(-1), dkvn.reshape(-1)


def rope_q(q, tabs, transpose, name, ts=256):
    T, W = q.shape
    ts = min(ts, T)
    fn = _rope_t if transpose else _rope
    hw = 2 * LANES

    def body(q_ref, c_ref, sa_ref, sb_ref, o_ref):
        c, sa, sb = c_ref[...], sa_ref[...], sb_ref[...]
        for h in range(W // hw):
            o_ref[:, h * hw:h * hw + LANES] = q_ref[:, h * hw:h * hw + LANES].astype(o_ref.dtype)
            o_ref[:, h * hw + LANES:(h + 1) * hw] = fn(q_ref[:, h * hw + LANES:(h + 1) * hw].astype(F32), c, sa, sb).astype(o_ref.dtype)

    return pl.pallas_call(
        body, name=name, grid=(T // ts,),
        in_specs=[_row_spec(ts, W)] + [_row_spec(ts, LANES)] * 3, out_specs=_row_spec(ts, W),
        out_shape=jax.ShapeDtypeStruct((T, W), BF16), compiler_params=_cparams(("parallel",)),
    )(q, *tabs)


def loss_head(x, g, target, name="loss_head", ts=256):
    T, D = x.shape
    ts = min(ts, T)

    def body(x_ref, g_ref, t_ref, se_ref, dx_ref, dg_ref):
        xv = x_ref[...]
        r = lax.rsqrt(jnp.mean(xv * xv, axis=-1, keepdims=True) + EPS)
        xh = xv * r
        err = xh * g_ref[...] - t_ref[...]
        dy = err * (1.0 / D)
        dxh = dy * g_ref[...]
        dx_ref[...] = r * (dxh - xh * jnp.mean(dxh * xh, axis=-1, keepdims=True))

        @pl.when(pl.program_id(0) == 0)
        def _():
            se_ref[...] = jnp.zeros_like(se_ref)
            dg_ref[...] = jnp.zeros_like(dg_ref)

        se_ref[...] += jnp.sum(err * err, axis=0, keepdims=True)
        dg_ref[...] += jnp.sum(dy * xh, axis=0, keepdims=True)

    se, dx, dg = pl.pallas_call(
        body, name=name, grid=(T // ts,),
        in_specs=[_row_spec(ts, D), _par_spec(D), _row_spec(ts, D)],
        out_specs=[_par_spec(D), _row_spec(ts, D), _par_spec(D)],
        out_shape=[jax.ShapeDtypeStruct((1, D), F32), jax.ShapeDtypeStruct((T, D), F32), jax.ShapeDtypeStruct((1, D), F32)],
        compiler_params=_cparams(("arbitrary",)),
    )(x, g.reshape(1, D), target)
    return se, dx, dg.reshape(D)


def _dot_nt(a, b):
    return lax.dot_general(a, b, (((1,), (1,)), ((), ())), preferred_element_type=F32)


def _dot_tn(a, b):
    return lax.dot_general(a, b, (((0,), (0,)), ((), ())), preferred_element_type=F32)


def _dot_nn(a, b):
    return lax.dot_general(a, b, (((1,), (0,)), ((), ())), preferred_element_type=F32)


class _Attn:
    def __init__(self, H, dq, dk1, dv, k1_col, v_col, causal, scale, blk=256):
        self.H, self.dq, self.dk1, self.dv = H, dq, dk1, dv
        self.k1_col, self.v_col, self.causal, self.scale, self.blk = k1_col, v_col, causal, scale, blk


def _keys(k1_ref, k2_ref, rows):
    ks = k1_ref[rows, :]
    if k2_ref is not None:
        ks = jnp.concatenate([ks, k2_ref[rows, :]], axis=1)
    return ks


def _mask(s, cfg, i, j, t):
    if not cfg.causal:
        return s
    row = i * t + lax.broadcasted_iota(jnp.int32, s.shape, 0)
    col = j * t + lax.broadcasted_iota(jnp.int32, s.shape, 1)
    return jnp.where(row >= col, s, NEG)


def flash_fwd(cfg, q, k1, v, k2, name):
    Tq, Tk = q.shape[0], k1.shape[0]
    t = min(cfg.blk, Tq, Tk)
    nkb = Tk // t
    has_k2 = k2 is not None

    def body(*refs):
        q_ref, k1_ref, v_ref = refs[:3]
        k2_ref = refs[3] if has_k2 else None
        o_ref, lse_ref = refs[-2], refs[-1]
        i = pl.program_id(1)
        qv = q_ref[...]

        def step(j, carry):
            m, l, acc = carry
            rows = pl.ds(pl.multiple_of(j * t, t), t)
            s = _mask(_dot_nt(qv, _keys(k1_ref, k2_ref, rows)) * cfg.scale, cfg, i, j, t)
            m2 = jnp.maximum(m, jnp.max(s, axis=-1, keepdims=True))
            p = jnp.exp(s - m2)
            alpha = jnp.exp(m - m2)
            l2 = alpha * l + jnp.sum(p, axis=-1, keepdims=True)
            acc2 = alpha * acc + _dot_nn(p.astype(BF16), v_ref[rows, :])
            return m2, l2, acc2

        init = (jnp.full((t, 1), NEG, F32), jnp.zeros((t, 1), F32), jnp.zeros((t, cfg.dv), F32))
        m, l, acc = lax.fori_loop(0, (i + 1) if cfg.causal else nkb, step, init)
        o_ref[...] = (acc / l).astype(o_ref.dtype)
        lse_ref[...] = m + jnp.log(l)

    in_specs = [pl.BlockSpec((t, cfg.dq), lambda h, i: (i, h)),
                pl.BlockSpec((Tk, cfg.dk1), lambda h, i: (0, cfg.k1_col + h)),
                pl.BlockSpec((Tk, cfg.dv), lambda h, i: (0, cfg.v_col + h))]
    args = [q, k1, v]
    if has_k2:
        in_specs.append(pl.BlockSpec((Tk, LANES), lambda h, i: (0, 0)))
        args.append(k2)
    return pl.pallas_call(
        body, name=name, grid=(cfg.H, Tq // t), in_specs=in_specs,
        out_specs=[pl.BlockSpec((t, cfg.dv), lambda h, i: (i, h)), pl.BlockSpec((None, t, 1), lambda h, i: (h, i, 0))],
        out_shape=[jax.ShapeDtypeStruct((Tq, cfg.H * cfg.dv), BF16), jax.ShapeDtypeStruct((cfg.H, Tq, 1), F32)],
        compiler_params=_cparams(("parallel", "parallel")),
    )(*args)


def flash_dq(cfg, q, k1, v, k2, o, do, lse, out_dtype, name):
    Tq, Tk = q.shape[0], k1.shape[0]
    t = min(cfg.blk, Tq, Tk)
    nkb = Tk // t
    has_k2 = k2 is not None

    def body(*refs):
        q_ref, k1_ref, v_ref = refs[:3]
        k2_ref = refs[3] if has_k2 else None
        o_ref, do_ref, lse_ref, dq_ref, dl_ref = refs[-5:]
        i = pl.program_id(1)
        qv, dov, lsev = q_ref[...], do_ref[...], lse_ref[...]
        delta = jnp.sum(dov.astype(F32) * o_ref[...].astype(F32), axis=-1, keepdims=True)
        dl_ref[...] = delta

        def step(j, dq):
            rows = pl.ds(pl.multiple_of(j * t, t), t)
            ks = _keys(k1_ref, k2_ref, rows)
            s = _mask(_dot_nt(qv, ks) * cfg.scale, cfg, i, j, t)
            p = jnp.exp(s - lsev)
            dp = _dot_nt(dov, v_ref[rows, :])
            ds = p * (dp - delta) * cfg.scale
            return dq + _dot_nn(ds.astype(BF16), ks)

        dq = lax.fori_loop(0, (i + 1) if cfg.causal else nkb, step, jnp.zeros((t, cfg.dq), F32))
        dq_ref[...] = dq.astype(dq_ref.dtype)

    in_specs = [pl.BlockSpec((t, cfg.dq), lambda h, i: (i, h)),
                pl.BlockSpec((Tk, cfg.dk1), lambda h, i: (0, cfg.k1_col + h)),
                pl.BlockSpec((Tk, cfg.dv), lambda h, i: (0, cfg.v_col + h))]
    args = [q, k1, v]
    if has_k2:
        in_specs.append(pl.BlockSpec((Tk, LANES), lambda h, i: (0, 0)))
        args.append(k2)
    in_specs += [pl.BlockSpec((t, cfg.dv), lambda h, i: (i, h)), pl.BlockSpec((t, cfg.dv), lambda h, i: (i, h)),
                 pl.BlockSpec((None, t, 1), lambda h, i: (h, i, 0))]
    args += [o, do, lse]
    return pl.pallas_call(
        body, name=name, grid=(cfg.H, Tq // t), in_specs=in_specs,
        out_specs=[pl.BlockSpec((t, cfg.dq), lambda h, i: (i, h)), pl.BlockSpec((None, t, 1), lambda h, i: (h, i, 0))],
        out_shape=[jax.ShapeDtypeStruct((Tq, cfg.H * cfg.dq), out_dtype), jax.ShapeDtypeStruct((cfg.H, Tq, 1), F32)],
        compiler_params=_cparams(("parallel", "parallel")),
    )(*args)


def flash_dkv(cfg, q, k1, v, k2, do, lse, delta, out_dtype, name):
    Tq, Tk = q.shape[0], k1.shape[0]
    t = min(cfg.blk, Tq, Tk)
    nqb = Tq // t
    has_k2 = k2 is not None

    def body(*refs):
        q_ref, k1_ref, v_ref = refs[:3]
        k2_ref = refs[3] if has_k2 else None
        n_in = 4 if has_k2 else 3
        do_ref, lse_ref, dl_ref = refs[n_in:n_in + 3]
        dk1_ref, dv_ref = refs[n_in + 3], refs[n_in + 4]
        j, h = pl.program_id(0), pl.program_id(1)
        ks = _keys(k1_ref, k2_ref, slice(None))
        vs = v_ref[...]

        def step(i, carry):
            dk, dv = carry
            rows = pl.ds(pl.multiple_of(i * t, t), t)
            qi, doi = q_ref[rows, :], do_ref[rows, :]
            s = _mask(_dot_nt(qi, ks) * cfg.scale, cfg, i, j, t)
            p = jnp.exp(s - lse_ref[rows, :])
            dv = dv + _dot_tn(p.astype(BF16), doi)
            ds = p * (_dot_nt(doi, vs) - dl_ref[rows, :]) * cfg.scale
            dk = dk + _dot_tn(ds.astype(BF16), qi)
            return dk, dv

        init = (jnp.zeros((t, cfg.dq), F32), jnp.zeros((t, cfg.dv), F32))
        dk, dv = lax.fori_loop(j if cfg.causal else 0, nqb, step, init)
        dv_ref[...] = dv.astype(dv_ref.dtype)
        dk1_ref[...] = dk[:, 0:cfg.dk1].astype(dk1_ref.dtype)
        if has_k2:
            dk2_ref = refs[n_in + 5]

            @pl.when(h == 0)
            def _():
                dk2_ref[...] = jnp.zeros_like(dk2_ref)

            dk2_ref[...] += dk[:, cfg.dk1:]

    in_specs = [pl.BlockSpec((Tq, cfg.dq), lambda j, h: (0, h)),
                pl.BlockSpec((t, cfg.dk1), lambda j, h: (j, cfg.k1_col + h)),
                pl.BlockSpec((t, cfg.dv), lambda j, h: (j, cfg.v_col + h))]
    args = [q, k1, v]
    if has_k2:
        in_specs.append(pl.BlockSpec((t, LANES), lambda j, h: (j, 0)))
        args.append(k2)
    in_specs += [pl.BlockSpec((Tq, cfg.dv), lambda j, h: (0, h)), pl.BlockSpec((None, Tq, 1), lambda j, h: (h, 0, 0)),
                 pl.BlockSpec((None, Tq, 1), lambda j, h: (h, 0, 0))]
    args += [do, lse, delta]
    out_specs = [pl.BlockSpec((t, cfg.dk1), lambda j, h: (j, h)), pl.BlockSpec((t, cfg.dv), lambda j, h: (j, h))]
    out_shape = [jax.ShapeDtypeStruct((Tk, cfg.H * cfg.dk1), out_dtype), jax.ShapeDtypeStruct((Tk, cfg.H * cfg.dv), out_dtype)]
    if has_k2:
        out_specs.append(pl.BlockSpec((t, LANES), lambda j, h: (j, 0)))
        out_shape.append(jax.ShapeDtypeStruct((Tk, LANES), F32))
    return pl.pallas_call(
        body, name=name, grid=(Tk // t, cfg.H), in_specs=in_specs, out_specs=out_specs, out_shape=out_shape,
        compiler_params=_cparams(("parallel", "arbitrary")),
    )(*args)


def _shift_down(x, s):
    if s == 0:
        return x
    t = lax.broadcasted_iota(jnp.int32, x.shape, 0)
    return jnp.where(t >= s, pltpu.roll(x, s, 0), 0.0)


def _shift_up(x, s):
    if s == 0:
        return x
    n = x.shape[0]
    t = lax.broadcasted_iota(jnp.int32, x.shape, 0)
    return jnp.where(t < n - s, pltpu.roll(x, n - s, 0), 0.0)


def _conv(x, w_ref, kw):
    y = x * w_ref[kw - 1:kw, :]
    for j in range(kw - 1):
        y = y + _shift_down(x, kw - 1 - j) * w_ref[j:j + 1, :]
    return y


def _conv_t(d, w_ref, kw):
    y = d * w_ref[kw - 1:kw, :]
    for j in range(kw - 1):
        y = y + _shift_up(d, kw - 1 - j) * w_ref[j:j + 1, :]
    return y


def _conv_dw(d, x, kw):
    rows = lax.broadcasted_iota(jnp.int32, (kw, d.shape[1]), 0)
    dw = jnp.zeros((kw, d.shape[1]), F32)
    for j in range(kw):
        r = jnp.sum(d * _shift_down(x, kw - 1 - j), axis=0, keepdims=True)
        dw = jnp.where(rows == j, r, dw)
    return dw


def _silu(x):
    return x * jax.nn.sigmoid(x)


def _silu_grad(x):
    s = jax.nn.sigmoid(x)
    return s * (1.0 + x * (1.0 - s))


def gdn_conv_fwd(z, w, name, tc=256):
    T, C = z.shape[0], w.shape[1]
    kw = w.shape[0]

    def body(x_ref, w_ref, o_ref):
        o_ref[...] = _silu(_conv(x_ref[...], w_ref, kw))

    return pl.pallas_call(
        body, name=name, grid=(C // tc,),
        in_specs=[pl.BlockSpec((T, tc), lambda j: (0, j)), pl.BlockSpec((kw, tc), lambda j: (0, j))],
        out_specs=pl.BlockSpec((T, tc), lambda j: (0, j)),
        out_shape=jax.ShapeDtypeStruct((T, C), F32), compiler_params=_cparams(("parallel",)),
    )(z, w)


def gdn_conv_bwd(z, w, dy, name, tc=256):
    T, C = z.shape[0], w.shape[1]
    kw = w.shape[0]

    def body(x_ref, w_ref, dy_ref, dx_ref, dw_ref):
        xv = x_ref[...]
        dc = dy_ref[...] * _silu_grad(_conv(xv, w_ref, kw))
        dx_ref[...] = _conv_t(dc, w_ref, kw).astype(dx_ref.dtype)
        dw_ref[...] = _conv_dw(dc, xv, kw)

    col = lambda j: (0, j)
    return pl.pallas_call(
        body, name=name, grid=(C // tc,),
        in_specs=[pl.BlockSpec((T, tc), col), pl.BlockSpec((kw, tc), col), pl.BlockSpec((T, tc), col)],
        out_specs=[pl.BlockSpec((T, tc), col), pl.BlockSpec((kw, tc), col)],
        out_shape=[jax.ShapeDtypeStruct((T, C), BF16), jax.ShapeDtypeStruct((kw, C), F32)],
        compiler_params=_cparams(("parallel",)),
    )(z, w, dy)


def sc_fwd(z, w, name, tc=256):
    T, C = z.shape[0], w.shape[1]
    kw, nb = w.shape[0], C // tc

    def body(b_ref, c_ref, u_ref, w_ref, o_ref):
        o_ref[...] = (b_ref[...] * _conv(c_ref[...] * u_ref[...], w_ref, kw)).astype(o_ref.dtype)

    return pl.pallas_call(
        body, name=name, grid=(nb,),
        in_specs=[pl.BlockSpec((T, tc), lambda j: (0, j)), pl.BlockSpec((T, tc), lambda j: (0, nb + j)),
                  pl.BlockSpec((T, tc), lambda j: (0, 2 * nb + j)), pl.BlockSpec((kw, tc), lambda j: (0, j))],
        out_specs=pl.BlockSpec((T, tc), lambda j: (0, j)),
        out_shape=jax.ShapeDtypeStruct((T, C), BF16), compiler_params=_cparams(("parallel",)),
    )(z, z, z, w)


def sc_bwd(z, w, dy, name, tc=256):
    T, C = z.shape[0], w.shape[1]
    kw, nb = w.shape[0], C // tc

    def body(b_ref, c_ref, u_ref, w_ref, dy_ref, db_ref, dc_ref, du_ref, dw_ref):
        cv, uv, dyv = c_ref[...], u_ref[...], dy_ref[...]
        cu = cv * uv
        db_ref[...] = (dyv * _conv(cu, w_ref, kw)).astype(db_ref.dtype)
        dcv = dyv * b_ref[...]
        dcu = _conv_t(dcv, w_ref, kw)
        dc_ref[...] = (dcu * uv).astype(dc_ref.dtype)
        du_ref[...] = (dcu * cv).astype(du_ref.dtype)
        dw_ref[...] = _conv_dw(dcv, cu, kw)

    col = lambda j: (0, j)
    act = jax.ShapeDtypeStruct((T, C), BF16)
    return pl.pallas_call(
        body, name=name, grid=(nb,),
        in_specs=[pl.BlockSpec((T, tc), col), pl.BlockSpec((T, tc), lambda j: (0, nb + j)),
                  pl.BlockSpec((T, tc), lambda j: (0, 2 * nb + j)), pl.BlockSpec((kw, tc), col), pl.BlockSpec((T, tc), col)],
        out_specs=[pl.BlockSpec((T, tc), col)] * 3 + [pl.BlockSpec((kw, tc), col)],
        out_shape=[act, act, act, jax.ShapeDtypeStruct((kw, C), F32)],
        compiler_params=_cparams(("parallel",)),
    )(z, z, z, w, dy)


def _hdot(a, b, dims):
    return lax.dot_general(a, b, (dims, ((), ())), precision=HI, preferred_element_type=F32)


def _hnn(a, b):
    return _hdot(a, b, ((1,), (0,)))


def _hnt(a, b):
    return _hdot(a, b, ((1,), (1,)))


def _htn(a, b):
    return _hdot(a, b, ((0,), (0,)))


@jax.custom_vjp
def _unit_lower_inverse(m):
    c = m.shape[0]
    eye = (lax.broadcasted_iota(jnp.int32, (c, c), 0) == lax.broadcasted_iota(jnp.int32, (c, c), 1)).astype(F32)
    t = eye - m
    p = _hnn(m, m)
    n = 2
    while n < c:
        t = t + _hnn(t, p)
        n *= 2
        if n < c:
            p = _hnn(p, p)
    return t


def _uli_fwd(m):
    t = _unit_lower_inverse(m)
    return t, t


def _uli_bwd(t, dt):
    return (-_htn(t, _hnt(dt, t)),)


_unit_lower_inverse.defvjp(_uli_fwd, _uli_bwd)


def _gdn_chunk(q, k, v, gate, bl, al, a_log, dt_bias, o_norm, st):
    c = q.shape[0]
    ii = lax.broadcasted_iota(jnp.int32, (c, c), 0)
    jj = lax.broadcasted_iota(jnp.int32, (c, c), 1)
    tri, strict = ii >= jj, ii > jj
    q = q * lax.rsqrt(jnp.sum(q * q, -1, keepdims=True) + EPS) * (GDN_D ** -0.5)
    k = k * lax.rsqrt(jnp.sum(k * k, -1, keepdims=True) + EPS)
    beta = jax.nn.sigmoid(bl)
    g = -jnp.exp(a_log) * jax.nn.softplus(al + dt_bias)
    gc = _hnn(tri.astype(F32), g)
    gcol = _hnn(gc, jnp.full((LANES, c), 1.0 / LANES, F32))
    grow = _hnt(jnp.full((c, LANES), 1.0 / LANES, F32), gc)
    decay = jnp.where(tri, jnp.exp(jnp.where(tri, gcol - grow, 0.0)), 0.0)
    kb = k * beta
    m = jnp.where(strict, _hnt(kb, k) * decay, 0.0)
    t_inv = _unit_lower_inverse(m)
    eg = jnp.exp(gc)
    u = _hnn(t_inv, v * beta)
    w = _hnn(t_inv, kb * eg)
    attn = _hnt(q, k) * decay
    v_new = u - _hnn(w, st)
    o = _hnn(q * eg, st) + _hnn(attn, v_new)
    g_last = jnp.sum(g, axis=0, keepdims=True)
    st_new = st * jnp.exp(g_last) + _htn(k * jnp.exp(g_last - gc), v_new)
    o = o * lax.rsqrt(jnp.mean(o * o, -1, keepdims=True) + EPS) * o_norm
    return o * _silu(gate), st_new


def _gdn_specs(n_chunks, rev):
    def tok(col):
        if rev:
            return pl.BlockSpec((GDN_C, GDN_D), lambda h, n: (n_chunks - 1 - n, col + h))
        return pl.BlockSpec((GDN_C, GDN_D), lambda h, n: (n, col + h))
    par = pl.BlockSpec((1, GDN_D), lambda h, n: (0, h))
    shared = pl.BlockSpec((1, GDN_D), lambda h, n: (0, 0))
    if rev:
        st = pl.BlockSpec((None, None, GDN_D, GDN_D), lambda h, n: (h, n_chunks - 1 - n, 0, 0))
    else:
        st = pl.BlockSpec((None, None, GDN_D, GDN_D), lambda h, n: (h, n, 0, 0))
    return tok, par, shared, st


def gdn_chunk_fwd(qkv, z, a_log_x, dt_bias_x, o_norm, name):
    T = qkv.shape[0]
    n_chunks = T // GDN_C
    H = GDN_H
    tok, par, shared, st_spec = _gdn_specs(n_chunks, False)

    def body(q_ref, k_ref, v_ref, g_ref, bl_ref, al_ref, a_ref, dt_ref, on_ref, o_ref, st_ref, state):
        @pl.when(pl.program_id(1) == 0)
        def _():
            state[...] = jnp.zeros_like(state)

        st = state[...]
        st_ref[...] = st
        o, st_new = _gdn_chunk(q_ref[...], k_ref[...], v_ref[...], g_ref[...], bl_ref[...], al_ref[...],
                               a_ref[...], dt_ref[...], on_ref[...], st)
        o_ref[...] = o.astype(o_ref.dtype)
        state[...] = st_new

    return pl.pallas_call(
        body, name=name, grid=(H, n_chunks),
        in_specs=[tok(0), tok(H), tok(2 * H), tok(3 * H), tok(4 * H), tok(5 * H), par, par, shared],
        out_specs=[tok(0), st_spec],
        out_shape=[jax.ShapeDtypeStruct((T, H * GDN_D), BF16), jax.ShapeDtypeStruct((H, n_chunks, GDN_D, GDN_D), F32)],
        scratch_shapes=[pltpu.VMEM((GDN_D, GDN_D), F32)],
        compiler_params=_cparams(("parallel", "arbitrary")),
    )(qkv, qkv, qkv, z, z, z, a_log_x, dt_bias_x, o_norm)


def gdn_chunk_bwd(qkv, z, a_log_x, dt_bias_x, o_norm, states, do, name):
    T = qkv.shape[0]
    n_chunks = T // GDN_C
    H = GDN_H
    tok, par, shared, st_spec = _gdn_specs(n_chunks, True)

    def body(q_ref, k_ref, v_ref, g_ref, bl_ref, al_ref, a_ref, dt_ref, on_ref, st_ref, do_ref,
             dq_ref, dk_ref, dv_ref, dg_ref, dbl_ref, dal_ref, da_ref, ddt_ref, don_ref, dstate):
        h, n = pl.program_id(0), pl.program_id(1)

        @pl.when(n == 0)
        def _():
            dstate[...] = jnp.zeros_like(dstate)
            da_ref[...] = jnp.zeros_like(da_ref)
            ddt_ref[...] = jnp.zeros_like(ddt_ref)

        @pl.when((n == 0) & (h == 0))
        def _():
            don_ref[...] = jnp.zeros_like(don_ref)

        _, vjp = jax.vjp(_gdn_chunk, q_ref[...], k_ref[...], v_ref[...], g_ref[...], bl_ref[...], al_ref[...],
                         a_ref[...], dt_ref[...], on_ref[...], st_ref[...])
        dq, dk, dv, dg, dbl, dal, da, ddt, don, dst = vjp((do_ref[...].astype(F32), dstate[...]))
        dq_ref[...] = dq
        dk_ref[...] = dk
        dv_ref[...] = dv
        dg_ref[...] = dg.astype(dg_ref.dtype)
        dbl_ref[...] = dbl.astype(dbl_ref.dtype)
        dal_ref[...] = dal.astype(dal_ref.dtype)
        da_ref[...] += da
        ddt_ref[...] += ddt
        don_ref[...] += don
        dstate[...] = dst

    tok0 = tok(0)
    f32_tok = jax.ShapeDtypeStruct((T, H * GDN_D), F32)
    bf_tok = jax.ShapeDtypeStruct((T, H * GDN_D), BF16)
    par_sh = jax.ShapeDtypeStruct((1, H * GDN_D), F32)
    return pl.pallas_call(
        body, name=name, grid=(H, n_chunks),
        in_specs=[tok(0), tok(H), tok(2 * H), tok(3 * H), tok(4 * H), tok(5 * H), par, par, shared, st_spec, tok0],
        out_specs=[tok0] * 6 + [par, par, shared],
        out_shape=[f32_tok, f32_tok, f32_tok, bf_tok, bf_tok, bf_tok, par_sh, par_sh, jax.ShapeDtypeStruct((1, GDN_D), F32)],
        scratch_shapes=[pltpu.VMEM((GDN_D, GDN_D), F32)],
        compiler_params=_cparams(("arbitrary", "arbitrary")),
    )(qkv, qkv, qkv, z, z, z, a_log_x, dt_bias_x, o_norm, states, do)


def pair_add(g, b, c_idx, name="grad_pair_add", tr=128):
    n, rh, w = b.shape
    nb = rh // tr

    def body(c_ref, g_ref, b_ref, o_ref):
        o_ref[...] = (g_ref[...].astype(F32) + b_ref[...].astype(F32)).astype(o_ref.dtype)

    return pl.pallas_call(
        body, name=name,
        grid_spec=pltpu.PrefetchScalarGridSpec(
            num_scalar_prefetch=1, grid=(n, nb),
            in_specs=[pl.BlockSpec((None, tr, w), lambda k, r, c: (k, c[0] * nb + r, 0)),
                      pl.BlockSpec((None, tr, w), lambda k, r, c: (k, r, 0))],
            out_specs=pl.BlockSpec((None, tr, w), lambda k, r, c: (k, r, 0))),
        out_shape=jax.ShapeDtypeStruct(b.shape, BF16), compiler_params=_cparams(("parallel", "parallel")),
    )(c_idx, g, b)


def chip_sum(parts, name="grad_chip_sum", tr=128):
    n, rh, w = parts.shape

    def body(p_ref, o_ref):
        acc = p_ref[0].astype(F32)
        for k in range(1, n):
            acc = acc + p_ref[k].astype(F32)
        o_ref[...] = acc

    return pl.pallas_call(
        body, name=name, grid=(rh // tr,),
        in_specs=[pl.BlockSpec((n, tr, w), lambda r: (0, r, 0))], out_specs=pl.BlockSpec((tr, w), lambda r: (r, 0)),
        out_shape=jax.ShapeDtypeStruct((rh, w), F32), compiler_params=_cparams(("parallel",)),
    )(parts)


def adamw(g, w, m, v, name="adamw", tr=256):
    rows, cols = g.shape

    def body(g_ref, w_ref, m_ref, v_ref, d_ref, nm_ref, nv_ref):
        gv = g_ref[...]
        nm = ADAM_B1 * m_ref[...] + (1.0 - ADAM_B1) * gv
        nv = ADAM_B2 * v_ref[...] + (1.0 - ADAM_B2) * (gv * gv)
        m_hat = nm / (1.0 - ADAM_B1 ** ADAM_STEP)
        v_hat = nv / (1.0 - ADAM_B2 ** ADAM_STEP)
        d_ref[...] = -ADAM_LR * (m_hat / (jnp.sqrt(v_hat) + ADAM_EPS) + ADAM_WD * w_ref[...])
        nm_ref[...] = nm
        nv_ref[...] = nv

    spec = pl.BlockSpec((tr, cols), lambda r: (r, 0))
    sh = jax.ShapeDtypeStruct((rows, cols), F32)
    return pl.pallas_call(
        body, name=name, grid=(rows // tr,), in_specs=[spec] * 4, out_specs=[spec] * 3, out_shape=[sh, sh, sh],
        compiler_params=_cparams(("parallel",)),
    )(g, w, m, v)


_ANY = pl.BlockSpec(memory_space=pl.ANY)


def _place():
    x, y, c = lax.axis_index("x"), lax.axis_index("y"), lax.axis_index("c")
    chips = [(1 - x, y), (x, 1 - y), (1 - x, 1 - y)]
    return x, y, c, chips


def _chip_index(cx, cy):
    return 2 * cx + cy


def _remote(src, dst, send_sem, recv_sem, to):
    return pltpu.make_async_remote_copy(src_ref=src, dst_ref=dst, send_sem=send_sem, recv_sem=recv_sem,
                                        device_id=to, device_id_type=MESH)


def gather_weights(wf, name="weight_all_gather"):
    rows, cols = wf.shape
    rh = rows // 2

    def body(w_ref, out_ref, send_sems, recv_sems, local_sem):
        x, y, c, chips = _place()
        me = _chip_index(x, y)
        mine = pl.ds(c * rh, rh)
        own = pltpu.make_async_copy(w_ref, out_ref.at[me], local_sem)
        own.start()
        first = [_remote(w_ref.at[mine], out_ref.at[me, mine], send_sems.at[j], recv_sems.at[j], (*chip, c))
                 for j, chip in enumerate(chips)]
        for cp in first:
            cp.start()
        passed = []
        for j, chip in enumerate(chips):
            landed = out_ref.at[_chip_index(*chip), mine]
            _remote(landed, landed, send_sems.at[j], recv_sems.at[j], (*chip, c)).wait_recv()
            cp = _remote(landed, landed, send_sems.at[3 + j], recv_sems.at[3 + j], (x, y, 1 - c))
            cp.start()
            passed.append(cp)
        theirs = pl.ds((1 - c) * rh, rh)
        for j, chip in enumerate(chips):
            sib = out_ref.at[_chip_index(*chip), theirs]
            _remote(sib, sib, send_sems.at[3 + j], recv_sems.at[3 + j], (x, y, 1 - c)).wait_recv()
        for cp in first + passed:
            cp.wait_send()
        own.wait()

    return pl.pallas_call(
        body, name=name, in_specs=[_ANY], out_specs=_ANY,
        out_shape=jax.ShapeDtypeStruct((N_CHIPS, rows, cols), wf.dtype),
        scratch_shapes=[pltpu.SemaphoreType.DMA((6,)), pltpu.SemaphoreType.DMA((6,)), pltpu.SemaphoreType.DMA],
        compiler_params=pltpu.CompilerParams(has_side_effects=True),
    )(wf)


def pair_swap_halves(g, name="grad_pair_swap"):
    n, rows, cols = g.shape
    rh = rows // 2

    def body(g_ref, out_ref, send_sem, recv_sem):
        x, y, c, _ = _place()
        cp = _remote(g_ref.at[:, pl.ds((1 - c) * rh, rh), :], out_ref, send_sem, recv_sem, (x, y, 1 - c))
        cp.start()
        cp.wait()

    return pl.pallas_call(
        body, name=name, in_specs=[_ANY], out_specs=_ANY,
        out_shape=jax.ShapeDtypeStruct((n, rh, cols), g.dtype),
        scratch_shapes=[pltpu.SemaphoreType.DMA, pltpu.SemaphoreType.DMA],
        compiler_params=pltpu.CompilerParams(has_side_effects=True),
    )(g)


def chip_exchange(p, name="grad_chip_exchange"):
    def body(p_ref, out_ref, send_sems, recv_sems, local_sem):
        x, y, c, chips = _place()
        me = _chip_index(x, y)
        own = pltpu.make_async_copy(p_ref.at[me], out_ref.at[me], local_sem)
        own.start()
        sends = [_remote(p_ref.at[_chip_index(*chip)], out_ref.at[me], send_sems.at[j], recv_sems.at[j], (*chip, c))
                 for j, chip in enumerate(chips)]
        for cp in sends:
            cp.start()
        for j, chip in enumerate(chips):
            got = out_ref.at[_chip_index(*chip)]
            _remote(got, got, send_sems.at[j], recv_sems.at[j], (*chip, c)).wait_recv()
        for cp in sends:
            cp.wait_send()
        own.wait()

    return pl.pallas_call(
        body, name=name, in_specs=[_ANY], out_specs=_ANY, out_shape=jax.ShapeDtypeStruct(p.shape, p.dtype),
        scratch_shapes=[pltpu.SemaphoreType.DMA((3,)), pltpu.SemaphoreType.DMA((3,)), pltpu.SemaphoreType.DMA],
        compiler_params=pltpu.CompilerParams(has_side_effects=True),
    )(p)


def pair_join_halves(r, name="grad_pair_join"):
    rh, cols = r.shape

    def body(r_ref, out_ref, send_sem, recv_sem, local_sem):
        x, y, c, _ = _place()
        own = pltpu.make_async_copy(r_ref, out_ref.at[c], local_sem)
        own.start()
        cp = _remote(r_ref, out_ref.at[c], send_sem, recv_sem, (x, y, 1 - c))
        cp.start()
        theirs = out_ref.at[1 - c]
        _remote(theirs, theirs, send_sem, recv_sem, (x, y, 1 - c)).wait_recv()
        cp.wait_send()
        own.wait()

    return pl.pallas_call(
        body, name=name, in_specs=[_ANY], out_specs=_ANY, out_shape=jax.ShapeDtypeStruct((2, rh, cols), r.dtype),
        scratch_shapes=[pltpu.SemaphoreType.DMA, pltpu.SemaphoreType.DMA, pltpu.SemaphoreType.DMA],
        compiler_params=pltpu.CompilerParams(has_side_effects=True),
    )(r)


_BIG = [("mla_w_in", 1), ("mla_w_uq", 2), ("mla_w_ukv", 2), ("mla_w_o", 1), ("gdn_w_in", 2), ("gdn_w_o", 1),
        ("sc_w_in", 2), ("sc_w_o", 1), ("xa_w_q", 1), ("xa_w_kv", 2), ("xa_w_o", 1), ("mlp_w1", 2), ("mlp_w2", 1)]
_SMALL = [("mla_q_norm", 1), ("mla_kv_norm", 1), ("gdn_conv_w", 2), ("sc_conv_w", 2)]
_REPL = ["gdn_a_log", "gdn_dt_bias", "gdn_o_norm", "norm_mix", "norm_mem", "norm_mlp", "mem_norm", "final_norm"]
_FLAT_ORDER = [n for n, _ in _BIG] + [n for n, _ in _SMALL] + _REPL
_WEIGHTS = ['mla_w_in', 'mla_q_norm', 'mla_kv_norm', 'mla_w_uq', 'mla_w_ukv', 'mla_w_o', 'gdn_w_in', 'gdn_conv_w',
            'gdn_a_log', 'gdn_dt_bias', 'gdn_o_norm', 'gdn_w_o', 'sc_w_in', 'sc_conv_w', 'sc_w_o', 'norm_mix',
            'norm_mem', 'norm_mlp', 'xa_w_q', 'xa_w_kv', 'xa_w_o', 'mlp_w1', 'mlp_w2', 'mem_norm', 'final_norm']


def _pad_rows(flat, rows):
    return jnp.pad(flat, (0, rows * FLAT_COLS - flat.shape[0])).reshape(rows, FLAT_COLS)


def _pack_shard(vals, dtype=F32):
    return _pad_rows(jnp.concatenate([vals[n].astype(dtype).reshape(-1) for n in _FLAT_ORDER]), FLAT_ROWS)


def _unpack_shard(flat, like):
    out, off = {}, 0
    flat = flat.reshape(-1)
    for n in _FLAT_ORDER:
        size = like[n].size
        out[n] = flat[off:off + size].reshape(like[n].shape)
        off += size
    return out


def _pack_for_gather(p):
    big = jnp.concatenate([p[n].astype(BF16).reshape(-1) for n, _ in _BIG])
    small = jnp.concatenate([p[n].reshape(-1) for n, _ in _SMALL])
    small = lax.bitcast_convert_type(small, BF16).reshape(-1)
    return _pad_rows(jnp.concatenate([big, small]), FLAT_ROWS)


def _unpack_gathered(g, p):
    g = g.reshape(N_CHIPS, -1)
    out, off = {}, 0
    for n, ax in _BIG:
        size = p[n].size
        blocks = g[:, off:off + size].reshape((N_CHIPS,) + p[n].shape)
        out[n] = jnp.concatenate([blocks[s] for s in range(N_CHIPS)], axis=ax)
        off += size
    for n, ax in _SMALL:
        size = p[n].size
        words = g[:, off:off + 2 * size].reshape(N_CHIPS, size, 2)
        blocks = lax.bitcast_convert_type(words, F32).reshape((N_CHIPS,) + p[n].shape)
        out[n] = jnp.concatenate([blocks[s] for s in range(N_CHIPS)], axis=ax)
        off += 2 * size
    return out


def _pack_grads(grads, p, dtype):
    axes = dict(_BIG + _SMALL)
    slabs = []
    for s in range(N_CHIPS):
        vals = {}
        for n in _FLAT_ORDER:
            gr = grads[n]
            if n in axes:
                width = p[n].shape[axes[n]]
                gr = lax.slice_in_dim(gr, s * width, (s + 1) * width, axis=axes[n])
            vals[n] = gr
        slabs.append(_pack_shard(vals, dtype))
    return jnp.stack(slabs)


_MLA_CFG = _Attn(MLA_H, 2 * LANES, MLA_NOPE, MLA_V, 0, MLA_H, True, (MLA_NOPE + MLA_ROPE) ** -0.5)
_XA_CFG = _Attn(XA_H, XA_D, XA_D, XA_D, 0, XA_H, False, XA_D ** -0.5)


def _mla_weights(w_in, w_uq, w_ukv):
    k = w_in.shape[0]
    w_in_p = jnp.pad(w_in, ((0, 0), (0, MLA_ZPAD - w_in.shape[1])))
    w_uq_p = jnp.pad(w_uq.reshape(MLA_QR, MLA_H, MLA_NOPE + MLA_ROPE), ((0, 0), (0, 0), (0, 2 * LANES - MLA_NOPE - MLA_ROPE)))
    w_uq_p = w_uq_p.reshape(MLA_QR, MLA_H * 2 * LANES)
    kv = w_ukv.reshape(MLA_KVR, MLA_H, MLA_NOPE + MLA_V)
    w_ukv_p = jnp.concatenate([kv[:, :, :MLA_NOPE].reshape(MLA_KVR, -1), kv[:, :, MLA_NOPE:].reshape(MLA_KVR, -1)], axis=1)
    del k
    return w_in_p, w_uq_p, w_ukv_p


def _mla_weight_grads(d_in_p, d_uq_p, d_ukv_p):
    d_in = d_in_p[:, :MLA_QR + MLA_KVR + MLA_ROPE]
    d_uq = d_uq_p.reshape(MLA_QR, MLA_H, 2 * LANES)[:, :, :MLA_NOPE + MLA_ROPE].reshape(MLA_QR, -1)
    half = MLA_H * MLA_NOPE
    d_ukv = jnp.concatenate([d_ukv_p[:, :half].reshape(MLA_KVR, MLA_H, MLA_NOPE),
                             d_ukv_p[:, half:].reshape(MLA_KVR, MLA_H, MLA_V)], axis=2).reshape(MLA_KVR, -1)
    return d_in, d_uq, d_ukv


def _mla_fwd(xs, h, wts, qn, kvn, tabs, tag):
    w_in_p, w_uq_p, w_ukv_p, w_o = wts
    z = mm(h, w_in_p, "nn", f"{tag}_in")
    cq, ckv, kr = mla_mid_fwd(z, qn, kvn, tabs, f"{tag}_mid")
    q = rope_q(mm(cq, w_uq_p, "nn", f"{tag}_uq"), tabs, False, f"{tag}_ropeq")
    kv = mm(ckv, w_ukv_p, "nn", f"{tag}_ukv", outs=(BF16,))
    o, lse = flash_fwd(_MLA_CFG, q, kv, kv, kr, f"{tag}_attn")
    xs = mm(o, w_o, "nn", f"{tag}_out", epi=_epi_add, extras=(xs,))
    return xs, (z, cq, ckv, kr, q, kv, o, lse)


def _mla_bwd(dx, h, wts, qn, kvn, tabs, saved, tag):
    w_in_p, w_uq_p, w_ukv_p, w_o = wts
    z, cq, ckv, kr, q, kv, o, lse = saved
    d_wo = mm(o, dx, "tn", f"{tag}_dwo")
    do = mm(dx, w_o, "nt", f"{tag}_do", outs=(BF16,))
    dq, delta = flash_dq(_MLA_CFG, q, kv, kv, kr, o, do, lse, F32, f"{tag}_attn_dq")
    dk1, dv, dkr = flash_dkv(_MLA_CFG, q, kv, kv, kr, do, lse, delta, BF16, f"{tag}_attn_dkv")
    dqp = rope_q(dq, tabs, True, f"{tag}_ropeq_t")
    d_uq_p = mm(cq, dqp, "tn", f"{tag}_duq")
    dcq = mm(dqp, w_uq_p, "nt", f"{tag}_dcq")
    dkv = jnp.concatenate([dk1, dv], axis=1)
    d_ukv_p = mm(ckv, dkv, "tn", f"{tag}_dukv")
    dckv = mm(dkv, w_ukv_p, "nt", f"{tag}_dckv")
    dz, dqn, dkvn = mla_mid_bwd(z, qn, kvn, tabs, dcq, dckv, dkr, f"{tag}_mid_bwd")
    d_in_p = mm(h, dz, "tn", f"{tag}_din")
    dh = mm(dz, w_in_p, "nt", f"{tag}_dh")
    d_in, d_uq, d_ukv = _mla_weight_grads(d_in_p, d_uq_p, d_ukv_p)
    return dh, dict(mla_w_in=d_in, mla_w_uq=d_uq, mla_w_ukv=d_ukv, mla_w_o=d_wo, mla_q_norm=dqn, mla_kv_norm=dkvn)


_GDN_QKV = 3 * GDN_H * GDN_D
_GDN_GATE_END = _GDN_QKV + GDN_H * GDN_D


def _gdn_weights(w_in):
    rep = lambda cols: jnp.repeat(cols, GDN_D, axis=1)
    return jnp.concatenate([w_in[:, :_GDN_GATE_END], rep(w_in[:, _GDN_GATE_END:_GDN_GATE_END + GDN_H]),
                            rep(w_in[:, _GDN_GATE_END + GDN_H:])], axis=1)


def _fold(x):
    return x.reshape(x.shape[0], -1, GDN_D).sum(-1)


def _gdn_fwd(xs, h, w_in_x, conv_w, a_log, dt_bias, o_norm, w_o, tag):
    z = mm(h, w_in_x, "nn", f"{tag}_in")
    qkv = gdn_conv_fwd(z, conv_w, f"{tag}_conv")
    a_x, dt_x = jnp.repeat(a_log.reshape(1, -1), GDN_D, axis=1), jnp.repeat(dt_bias.reshape(1, -1), GDN_D, axis=1)
    og, states = gdn_chunk_fwd(qkv, z, a_x, dt_x, o_norm.reshape(1, -1), f"{tag}_chunks")
    xs = mm(og, w_o, "nn", f"{tag}_out", epi=_epi_add, extras=(xs,))
    return xs, (z, qkv, a_x, dt_x, og, states)


def _gdn_bwd(dx, h, w_in_x, conv_w, o_norm, w_o, saved, tag):
    z, qkv, a_x, dt_x, og, states = saved
    d_wo = mm(og, dx, "tn", f"{tag}_dwo")
    dog = mm(dx, w_o, "nt", f"{tag}_dog")
    dq, dk, dv, dgate, dbl, dal, da_x, ddt_x, don = gdn_chunk_bwd(qkv, z, a_x, dt_x, o_norm.reshape(1, -1), states, dog,
                                                                  f"{tag}_chunks_bwd")
    dpre, dconv = gdn_conv_bwd(z, conv_w, jnp.concatenate([dq, dk, dv], axis=1), f"{tag}_conv_bwd")
    dz = jnp.concatenate([dpre, dgate, dbl, dal], axis=1)
    d_in_x = mm(h, dz, "tn", f"{tag}_din")
    dh = mm(dz, w_in_x, "nt", f"{tag}_dh")
    ge = _GDN_GATE_END
    d_in = jnp.concatenate([d_in_x[:, :ge], _fold(d_in_x[:, ge:ge + GDN_H * GDN_D]), _fold(d_in_x[:, ge + GDN_H * GDN_D:])], axis=1)
    return dh, dict(gdn_w_in=d_in, gdn_conv_w=dconv, gdn_a_log=_fold(da_x).reshape(-1), gdn_dt_bias=_fold(ddt_x).reshape(-1),
                    gdn_o_norm=don.reshape(-1), gdn_w_o=d_wo)


def _sc_fwd(xs, h, w_in, conv_w, w_o, tag):
    z = mm(h, w_in, "nn", f"{tag}_in")
    y = sc_fwd(z, conv_w, f"{tag}_conv")
    xs = mm(y, w_o, "nn", f"{tag}_out", epi=_epi_add, extras=(xs,))
    return xs, (z, y)


def _sc_bwd(dx, h, w_in, conv_w, w_o, saved, tag):
    z, y = saved
    d_wo = mm(y, dx, "tn", f"{tag}_dwo")
    dy = mm(dx, w_o, "nt", f"{tag}_dy")
    db, dc, du, dconv = sc_bwd(z, conv_w, dy, f"{tag}_conv_bwd")
    dz = jnp.concatenate([db, dc, du], axis=1)
    d_in = mm(h, dz, "tn", f"{tag}_din")
    dh = mm(dz, w_in, "nt", f"{tag}_dh")
    return dh, dict(sc_w_in=d_in, sc_conv_w=dconv, sc_w_o=d_wo)


def local_step(x, mem, pos, target, w):
    depth = w["norm_mix"].shape[0]
    tabs = rope_tables(pos)
    mem_n = rmsnorm_fwd(mem, w["mem_norm"], "mem_norm")
    mla_w = [_mla_weights(w["mla_w_in"][j], w["mla_w_uq"][j], w["mla_w_ukv"][j]) + (w["mla_w_o"][j],)
             for j in range(w["mla_w_in"].shape[0])]
    gdn_in_x = [_gdn_weights(w["gdn_w_in"][j]) for j in range(w["gdn_w_in"].shape[0])]

    xs = x
    saved = []
    for i in range(depth):
        j, kind = i // 3, i % 3
        tag = f"l{i}"
        x_a = xs
        h = rmsnorm_fwd(xs, w["norm_mix"][i], f"{tag}_norm_mix")
        if kind == 0:
            xs, mix = _mla_fwd(xs, h, mla_w[j], w["mla_q_norm"][j], w["mla_kv_norm"][j], tabs, f"{tag}_mla")
        elif kind == 1:
            xs, mix = _gdn_fwd(xs, h, gdn_in_x[j], w["gdn_conv_w"][j], w["gdn_a_log"][j], w["gdn_dt_bias"][j],
                               w["gdn_o_norm"][j], w["gdn_w_o"][j], f"{tag}_gdn")
        else:
            xs, mix = _sc_fwd(xs, h, w["sc_w_in"][j], w["sc_conv_w"][j], w["sc_w_o"][j], f"{tag}_sc")
        x_b = xs
        hn = rmsnorm_fwd(xs, w["norm_mem"][i], f"{tag}_norm_mem")
        xq = mm(hn, w["xa_w_q"][i], "nn", f"{tag}_xa_q", outs=(BF16,))
        xkv = mm(mem_n, w["xa_w_kv"][i], "nn", f"{tag}_xa_kv", outs=(BF16,))
        xo, xlse = flash_fwd(_XA_CFG, xq, xkv, xkv, None, f"{tag}_xa_attn")
        xs = mm(xo, w["xa_w_o"][i], "nn", f"{tag}_xa_out", epi=_epi_add, extras=(xs,))
        x_c = xs
        hm = rmsnorm_fwd(xs, w["norm_mlp"][i], f"{tag}_norm_mlp")
        h1, act = mm(hm, w["mlp_w1"][i], "nn", f"{tag}_mlp_up", outs=(BF16, BF16), epi=_epi_relu2)
        xs = mm(act, w["mlp_w2"][i], "nn", f"{tag}_mlp_down", epi=_epi_add, extras=(xs,))
        saved.append((x_a, h, mix, x_b, hn, xq, xkv, xo, xlse, x_c, hm, h1, act))

    se, dx, d_final = loss_head(xs, w["final_norm"], target)

    per_layer = {n: [None] * depth for n in ("norm_mix", "norm_mem", "norm_mlp", "xa_w_q", "xa_w_kv", "xa_w_o", "mlp_w1", "mlp_w2")}
    mixer = {}
    dmem_n = jnp.zeros(mem.shape, F32)
    for i in reversed(range(depth)):
        j, kind = i // 3, i % 3
        tag = f"l{i}"
        x_a, h, mix, x_b, hn, xq, xkv, xo, xlse, x_c, hm, h1, act = saved[i]
        per_layer["mlp_w2"][i] = mm(act, dx, "tn", f"{tag}_mlp_dw2")
        dh1 = mm(dx, w["mlp_w2"][i], "nt", f"{tag}_mlp_dh1", outs=(BF16,), epi=_epi_relu2_bwd, extras=(h1,))
        per_layer["mlp_w1"][i] = mm(hm, dh1, "tn", f"{tag}_mlp_dw1")
        dhm = mm(dh1, w["mlp_w1"][i], "nt", f"{tag}_mlp_dhm")
        dx, per_layer["norm_mlp"][i] = rmsnorm_bwd(x_c, w["norm_mlp"][i], dhm, dx, f"{tag}_norm_mlp_bwd")
        per_layer["xa_w_o"][i] = mm(xo, dx, "tn", f"{tag}_xa_dwo")
        dxo = mm(dx, w["xa_w_o"][i], "nt", f"{tag}_xa_do", outs=(BF16,))
        dxq, xdelta = flash_dq(_XA_CFG, xq, xkv, xkv, None, xo, dxo, xlse, BF16, f"{tag}_xa_attn_dq")
        dxk, dxv = flash_dkv(_XA_CFG, xq, xkv, xkv, None, dxo, xlse, xdelta, BF16, f"{tag}_xa_attn_dkv")
        dxkv = jnp.concatenate([dxk, dxv], axis=1)
        per_layer["xa_w_q"][i] = mm(hn, dxq, "tn", f"{tag}_xa_dwq")
        dhn = mm(dxq, w["xa_w_q"][i], "nt", f"{tag}_xa_dhn")
        per_layer["xa_w_kv"][i] = mm(mem_n, dxkv, "tn", f"{tag}_xa_dwkv")
        dmem_n = mm(dxkv, w["xa_w_kv"][i], "nt", f"{tag}_xa_dmem", epi=_epi_add, extras=(dmem_n,))
        dx, per_layer["norm_mem"][i] = rmsnorm_bwd(x_b, w["norm_mem"][i], dhn, dx, f"{tag}_norm_mem_bwd")
        if kind == 0:
            dh, gr = _mla_bwd(dx, h, mla_w[j], w["mla_q_norm"][j], w["mla_kv_norm"][j], tabs, mix, f"{tag}_mla")
        elif kind == 1:
            dh, gr = _gdn_bwd(dx, h, gdn_in_x[j], w["gdn_conv_w"][j], w["gdn_o_norm"][j], w["gdn_w_o"][j], mix, f"{tag}_gdn")
        else:
            dh, gr = _sc_bwd(dx, h, w["sc_w_in"][j], w["sc_conv_w"][j], w["sc_w_o"][j], mix, f"{tag}_sc")
        for n, g in gr.items():
            mixer.setdefault(n, {})[j] = g
        dx, per_layer["norm_mix"][i] = rmsnorm_bwd(x_a, w["norm_mix"][i], dh, dx, f"{tag}_norm_mix_bwd")

    _, d_mem_norm = rmsnorm_bwd(mem, w["mem_norm"], dmem_n, jnp.zeros(mem.shape, F32), "mem_norm_bwd")
    grads = {n: jnp.stack(v) for n, v in per_layer.items()}
    for n, by_j in mixer.items():
        grads[n] = jnp.stack([by_j[j] for j in sorted(by_j)])
    grads["mem_norm"] = d_mem_norm
    grads["final_norm"] = d_final
    return se, dx, grads


def kernel(x, mem, positions, mla_w_in, mla_q_norm, mla_kv_norm, mla_w_uq, mla_w_ukv, mla_w_o, gdn_w_in, gdn_conv_w, gdn_a_log, gdn_dt_bias, gdn_o_norm, gdn_w_o, sc_w_in, sc_conv_w, sc_w_o, norm_mix, norm_mem, norm_mlp, xa_w_q, xa_w_kv, xa_w_o, mlp_w1, mlp_w2, mem_norm, final_norm, loss_target, m_mla_w_in, m_mla_q_norm, m_mla_kv_norm, m_mla_w_uq, m_mla_w_ukv, m_mla_w_o, m_gdn_w_in, m_gdn_conv_w, m_gdn_a_log, m_gdn_dt_bias, m_gdn_o_norm, m_gdn_w_o, m_sc_w_in, m_sc_conv_w, m_sc_w_o, m_norm_mix, m_norm_mem, m_norm_mlp, m_xa_w_q, m_xa_w_kv, m_xa_w_o, m_mlp_w1, m_mlp_w2, m_mem_norm, m_final_norm, v_mla_w_in, v_mla_q_norm, v_mla_kv_norm, v_mla_w_uq, v_mla_w_ukv, v_mla_w_o, v_gdn_w_in, v_gdn_conv_w, v_gdn_a_log, v_gdn_dt_bias, v_gdn_o_norm, v_gdn_w_o, v_sc_w_in, v_sc_conv_w, v_sc_w_o, v_norm_mix, v_norm_mem, v_norm_mlp, v_xa_w_q, v_xa_w_kv, v_xa_w_o, v_mlp_w1, v_mlp_w2, v_mem_norm, v_final_norm):
    given = dict(locals())
    p = {n: given[n] for n in _WEIGHTS}
    mom = {n: given["m_" + n] for n in _WEIGHTS}
    var = {n: given["v_" + n] for n in _WEIGHTS}

    full = _unpack_gathered(gather_weights(_pack_for_gather(p)), p)
    for n in _REPL:
        full[n] = p[n]

    se, dx, grads = local_step(x[0], mem[0], positions.reshape(-1, 1), loss_target[0], full)
    loss = lax.psum(0.5 * jnp.sum(se) / x.shape[-1], ("x", "y", "c"))

    c_idx = lax.axis_index("c").astype(jnp.int32).reshape(1)
    g_all = _pack_grads(grads, p, BF16)
    partial = pair_add(g_all, pair_swap_halves(g_all), c_idx)
    reduced = pair_join_halves(chip_sum(chip_exchange(partial))).reshape(FLAT_ROWS, FLAT_COLS)
    delta, new_m, new_v = adamw(reduced, _pack_shard(p), _pack_shard(mom), _pack_shard(var))

    outs = [_unpack_shard(a, p) for a in (reduced, delta, new_m, new_v)]
    return (loss, dx[None], *[o[n] for o in outs for n in _WEIGHTS])
```

```python
import jax
import jax.numpy as jnp
from jax import lax
from jax.experimental import pallas as pl
from jax.experimental.pallas import tpu as pltpu

F32 = jnp.float32
BF16 = jnp.bfloat16
HI = lax.Precision.HIGHEST
MESH = pl.DeviceIdType.MESH

EPS = 1e-6
ROPE_THETA = 10000.0
N_CHIPS = 4
LANES = 128
VMEM_LIMIT = 56 * 1024 * 1024
NEG = -1e30

MLA_H, MLA_NOPE, MLA_ROPE, MLA_V = 8, 128, 64, 128
MLA_QR, MLA_KVR = 384, 256
MLA_ZPAD = 768
GDN_H, GDN_D, GDN_C = 8, 128, 64
XA_H, XA_D = 4, 256

ADAM_LR, ADAM_B1, ADAM_B2, ADAM_EPS, ADAM_WD, ADAM_STEP = 0.001, 0.9, 0.999, 1e-08, 0.01, 10

SMALL_ROWS, SMALL_COLS = 32, 1024


def _cparams(sem=None):
    return pltpu.CompilerParams(dimension_semantics=sem, vmem_limit_bytes=VMEM_LIMIT)


def _pick(dim, pref):
    t = (min(pref, dim) // LANES) * LANES
    while t >= LANES:
        if dim % t == 0:
            return t
        t -= LANES
    return dim


def _pick_rows(rows, pref=256):
    t = pref
    while rows % t:
        t //= 2
    return t


class Slab:
    def __init__(self, rows, width, dtype, arr=None):
        self.shape, self.dtype, self.arr = (N_CHIPS, rows, width), dtype, arr


class Loc:
    def __init__(self, slab, row0, K, N, axis):
        self.slab, self.row0, self.K, self.N, self.axis = slab, row0, K, N, axis
        self.Ks = K // N_CHIPS if axis == 0 else K
        self.Ns = N // N_CHIPS if axis == 1 else N

    def tile_spec(self, tr, tc, rc):
        assert self.row0 % tr == 0 and self.Ks % tr == 0 and self.Ns % tc == 0, (self.row0, self.Ks, self.Ns, tr, tc)
        r0, rb, cb = self.row0 // tr, self.Ks // tr, self.Ns // tc
        if self.axis == 0:
            return pl.BlockSpec((None, tr, tc), lambda i, j: (rc(i, j)[0] // rb, r0 + rc(i, j)[0] % rb, rc(i, j)[1]))
        return pl.BlockSpec((None, tr, tc), lambda i, j: (rc(i, j)[1] // cb, r0 + rc(i, j)[0], rc(i, j)[1] % cb))

    def slot_spec(self, slot, tr, tc, rc):
        assert self.row0 % tr == 0, (self.row0, tr)
        r0 = self.row0 // tr
        return pl.BlockSpec((None, tr, tc), lambda i, j: (slot, r0 + rc(i, j)[0], rc(i, j)[1]))


_DIMS = {"nn": ((1,), (0,)), "nt": ((1,), (1,)), "tn": ((0,), (0,))}
_ANY = pl.BlockSpec(memory_space=pl.ANY)


def mm(a, b, mode, name, outs=(F32,), epi=None, extras=(), tm=512, tn=1024, out_loc=None):
    b_loc = b if isinstance(b, Loc) else None
    if mode == "nn":
        M, K = a.shape
        K2, N = (b_loc.K, b_loc.N) if b_loc else b.shape
    elif mode == "nt":
        M, K = a.shape
        N, K2 = (b_loc.K, b_loc.N) if b_loc else b.shape
    else:
        K, M = a.shape
        K2, N = b.shape
    assert K == K2, (name, a.shape, K2, N)
    tm = _pick(out_loc.Ks if (out_loc and out_loc.axis == 0) else M, tm)
    if out_loc is not None and out_loc.axis == 1:
        tn = _pick(out_loc.Ns, tn)
    elif b_loc is not None and ((mode == "nn" and b_loc.axis == 1) or (mode == "nt" and b_loc.axis == 0)):
        tn = _pick(b_loc.Ns if mode == "nn" else b_loc.Ks, tn)
    else:
        tn = _pick(N, tn)

    parts = 1
    if mode == "tn":
        a_spec = pl.BlockSpec((K, tm), lambda i, j: (0, i))
        b_specs, b_args = [pl.BlockSpec((K, tn), lambda i, j: (0, j))], [b]
    else:
        a_spec = pl.BlockSpec((tm, K), lambda i, j: (i, 0))
        if b_loc is None:
            b_specs = [pl.BlockSpec((K, tn), lambda i, j: (0, j)) if mode == "nn" else pl.BlockSpec((tn, K), lambda i, j: (j, 0))]
            b_args = [b]
        elif mode == "nn" and b_loc.axis == 1:
            b_specs, b_args = [b_loc.tile_spec(K, tn, lambda i, j: (0, j))], [b_loc.slab.arr]
        elif mode == "nt" and b_loc.axis == 0:
            b_specs, b_args = [b_loc.tile_spec(tn, K, lambda i, j: (j, 0))], [b_loc.slab.arr]
        elif mode == "nn":
            parts = N_CHIPS
            b_specs = [b_loc.slot_spec(s, b_loc.Ks, tn, lambda i, j: (0, j)) for s in range(parts)]
            b_args = [b_loc.slab.arr] * parts
        else:
            parts = N_CHIPS
            b_specs = [b_loc.slot_spec(s, tn, b_loc.Ns, lambda i, j: (j, 0)) for s in range(parts)]
            b_args = [b_loc.slab.arr] * parts
    kp = K // parts
    n_ex, n_out = len(extras), len(outs)
    dims = (_DIMS[mode], ((), ()))

    def body(*refs):
        a_ref = refs[0]
        b_refs = refs[1:1 + parts]
        ex_refs = refs[1 + parts:1 + parts + n_ex]
        o_refs = refs[-n_out:]
        acc = None
        for s in range(parts):
            av = a_ref[...] if parts == 1 else a_ref[:, s * kp:(s + 1) * kp]
            d = lax.dot_general(av.astype(BF16), b_refs[s][...].astype(BF16), dims, preferred_element_type=F32)
            acc = d if acc is None else acc + d
        res = epi(acc, *[e[...] for e in ex_refs]) if epi is not None else (acc,)
        for o_ref, v in zip(o_refs, res):
            o_ref[...] = v.astype(o_ref.dtype)

    mn_spec = pl.BlockSpec((tm, tn), lambda i, j: (i, j))
    in_specs = [a_spec] + b_specs + [mn_spec] * n_ex
    args = [a] + b_args + list(extras)
    aliases = {}
    if out_loc is None:
        out_specs = [mn_spec] * n_out
        out_shape = [jax.ShapeDtypeStruct((M, N), d) for d in outs]
    else:
        assert n_out == 1 and mode == "tn"
        out_specs = [out_loc.tile_spec(tm, tn, lambda i, j: (i, j))]
        out_shape = [jax.ShapeDtypeStruct(out_loc.slab.shape, out_loc.slab.dtype)]
        if out_loc.slab.arr is not None:
            in_specs.append(_ANY)
            args.append(out_loc.slab.arr)
            aliases = {len(args) - 1: 0}

    res = pl.pallas_call(
        body, name=name, grid=(M // tm, N // tn), in_specs=in_specs, out_specs=out_specs, out_shape=out_shape,
        input_output_aliases=aliases, compiler_params=_cparams(("parallel", "parallel")),
    )(*args)
    if out_loc is not None:
        out_loc.slab.arr = res[0]
        return None
    return res[0] if n_out == 1 else tuple(res)


def _epi_add(acc, r):
    return (acc + r,)


def _epi_relu2(acc):
    r = jnp.maximum(acc, 0.0)
    return acc, r * r


def _epi_relu2_bwd(acc, h1):
    return (acc * (2.0 * jnp.maximum(h1.astype(F32), 0.0)),)


def _rms(x, g):
    return x * lax.rsqrt(jnp.mean(x * x, axis=-1, keepdims=True) + EPS) * g


def _row_spec(ts, cols):
    return pl.BlockSpec((ts, cols), lambda i: (i, 0))


def _par_spec(cols):
    return pl.BlockSpec((1, cols), lambda i: (0, 0))


def rmsnorm_fwd(x, g, name, ts=256):
    T, D = x.shape
    ts = min(ts, T)

    def body(x_ref, g_ref, o_ref):
        o_ref[...] = _rms(x_ref[...], g_ref[...]).astype(o_ref.dtype)

    return pl.pallas_call(
        body, name=name, grid=(T // ts,),
        in_specs=[_row_spec(ts, D), _par_spec(D)], out_specs=_row_spec(ts, D),
        out_shape=jax.ShapeDtypeStruct((T, D), BF16), compiler_params=_cparams(("parallel",)),
    )(x, g.reshape(1, D))


def rmsnorm_bwd(x, g, dy, dx_in, name, ts=256):
    T, D = x.shape
    ts = min(ts, T)

    def body(x_ref, g_ref, dy_ref, dxi_ref, dx_ref, dg_ref):
        xv = x_ref[...]
        r = lax.rsqrt(jnp.mean(xv * xv, axis=-1, keepdims=True) + EPS)
        xh = xv * r
        dyv = dy_ref[...].astype(F32)
        dxh = dyv * g_ref[...]
        dx_ref[...] = dxi_ref[...] + r * (dxh - xh * jnp.mean(dxh * xh, axis=-1, keepdims=True))
        dg = jnp.sum(dyv * xh, axis=0, keepdims=True)

        @pl.when(pl.program_id(0) == 0)
        def _():
            dg_ref[...] = jnp.zeros_like(dg_ref)

        dg_ref[...] += dg

    dx, dg = pl.pallas_call(
        body, name=name, grid=(T // ts,),
        in_specs=[_row_spec(ts, D), _par_spec(D), _row_spec(ts, D), _row_spec(ts, D)],
        out_specs=[_row_spec(ts, D), _par_spec(D)],
        out_shape=[jax.ShapeDtypeStruct((T, D), F32), jax.ShapeDtypeStruct((1, D), F32)],
        compiler_params=_cparams(("arbitrary",)),
    )(x, g.reshape(1, D), dy, dx_in)
    return dx, dg.reshape(D)


def rope_tables(pos, name="rope_tables"):
    T = pos.shape[0]
    half = MLA_ROPE // 2
    inv = ROPE_THETA ** (-jnp.arange(0, MLA_ROPE, 2, dtype=F32) / MLA_ROPE)
    inv_row = jnp.concatenate([inv, inv, jnp.zeros((LANES - MLA_ROPE,), F32)]).reshape(1, LANES)

    def body(p_ref, f_ref, c_ref, a_ref, b_ref):
        ang = p_ref[...].astype(F32) * f_ref[...]
        lane = lax.broadcasted_iota(jnp.int32, ang.shape, 1)
        c, s = jnp.cos(ang), jnp.sin(ang)
        c_ref[...] = jnp.where(lane < MLA_ROPE, c, 0.0)
        a_ref[...] = jnp.where(lane < half, -s, 0.0)
        b_ref[...] = jnp.where((lane >= half) & (lane < MLA_ROPE), s, 0.0)

    sh = jax.ShapeDtypeStruct((T, LANES), F32)
    return pl.pallas_call(body, name=name, out_shape=[sh, sh, sh], compiler_params=_cparams())(pos, inv_row)


def _roll_l(x):
    return pltpu.roll(x, LANES - MLA_ROPE // 2, 1)


def _roll_r(x):
    return pltpu.roll(x, MLA_ROPE // 2, 1)


def _rope(r, c, sa, sb):
    return r * c + _roll_l(r) * sa + _roll_r(r) * sb


def _rope_t(d, c, sa, sb):
    return d * c + _roll_r(d * sa) + _roll_l(d * sb)


def mla_mid_fwd(z, qn, kvn, tabs, name, ts=256):
    T = z.shape[0]
    ts = min(ts, T)
    a0, a1 = MLA_QR, MLA_QR + MLA_KVR

    def body(z_ref, qn_ref, kvn_ref, c_ref, sa_ref, sb_ref, cq_ref, ckv_ref, kr_ref):
        cq_ref[...] = _rms(z_ref[:, 0:a0], qn_ref[...]).astype(BF16)
        ckv_ref[...] = _rms(z_ref[:, a0:a1], kvn_ref[...]).astype(BF16)
        kr_ref[...] = _rope(z_ref[:, a1:MLA_ZPAD], c_ref[...], sa_ref[...], sb_ref[...]).astype(BF16)

    return pl.pallas_call(
        body, name=name, grid=(T // ts,),
        in_specs=[_row_spec(ts, MLA_ZPAD), _par_spec(MLA_QR), _par_spec(MLA_KVR)] + [_row_spec(ts, LANES)] * 3,
        out_specs=[_row_spec(ts, MLA_QR), _row_spec(ts, MLA_KVR), _row_spec(ts, LANES)],
        out_shape=[jax.ShapeDtypeStruct((T, MLA_QR), BF16), jax.ShapeDtypeStruct((T, MLA_KVR), BF16),
                   jax.ShapeDtypeStruct((T, LANES), BF16)],
        compiler_params=_cparams(("parallel",)),
    )(z, qn.reshape(1, -1), kvn.reshape(1, -1), *tabs)


def mla_mid_bwd(z, qn, kvn, tabs, dcq, dckv, dkr, name, ts=256):
    T = z.shape[0]
    ts = min(ts, T)
    a0, a1 = MLA_QR, MLA_QR + MLA_KVR

    def body(z_ref, qn_ref, kvn_ref, c_ref, sa_ref, sb_ref, dcq_ref, dckv_ref, dkr_ref, dz_ref, dqn_ref, dkvn_ref):
        _, vq = jax.vjp(_rms, z_ref[:, 0:a0], qn_ref[...])
        dzq, dqn = vq(dcq_ref[...].astype(F32))
        _, vk = jax.vjp(_rms, z_ref[:, a0:a1], kvn_ref[...])
        dzk, dkvn = vk(dckv_ref[...].astype(F32))
        dz_ref[:, 0:a0] = dzq.astype(dz_ref.dtype)
        dz_ref[:, a0:a1] = dzk.astype(dz_ref.dtype)
        dz_ref[:, a1:MLA_ZPAD] = _rope_t(dkr_ref[...].astype(F32), c_ref[...], sa_ref[...], sb_ref[...]).astype(dz_ref.dtype)

        @pl.when(pl.program_id(0) == 0)
        def _():
            dqn_ref[...] = jnp.zeros_like(dqn_ref)
            dkvn_ref[...] = jnp.zeros_like(dkvn_ref)

        dqn_ref[...] += dqn
        dkvn_ref[...] += dkvn

    dz, dqn, dkvn = pl.pallas_call(
        body, name=name, grid=(T // ts,),
        in_specs=[_row_spec(ts, MLA_ZPAD), _par_spec(MLA_QR), _par_spec(MLA_KVR)] + [_row_spec(ts, LANES)] * 3
        + [_row_spec(ts, MLA_QR), _row_spec(ts, MLA_KVR), _row_spec(ts, LANES)],
        out_specs=[_row_spec(ts, MLA_ZPAD), _par_spec(MLA_QR), _par_spec(MLA_KVR)],
        out_shape=[jax.ShapeDtypeStruct((T, MLA_ZPAD), BF16), jax.ShapeDtypeStruct((1, MLA_QR), F32),
                   jax.ShapeDtypeStruct((1, MLA_KVR), F32)],
        compiler_params=_cparams(("arbitrary",)),
    )(z, qn.reshape(1, -1), kvn.reshape(1, -1), *tabs, dcq, dckv, dkr)
    return dz, dqn.reshape(-1), dkvn.reshape(-1)


def rope_q(q, tabs, transpose, name, ts=256):
    T, W = q.shape
    ts = min(ts, T)
    fn = _rope_t if transpose else _rope
    hw = 2 * LANES

    def body(q_ref, c_ref, sa_ref, sb_ref, o_ref):
        c, sa, sb = c_ref[...], sa_ref[...], sb_ref[...]
        for h in range(W // hw):
            o_ref[:, h * hw:h * hw + LANES] = q_ref[:, h * hw:h * hw + LANES].astype(o_ref.dtype)
            o_ref[:, h * hw + LANES:(h + 1) * hw] = fn(q_ref[:, h * hw + LANES:(h + 1) * hw].astype(F32), c, sa, sb).astype(o_ref.dtype)

    return pl.pallas_call(
        body, name=name, grid=(T // ts,),
        in_specs=[_row_spec(ts, W)] + [_row_spec(ts, LANES)] * 3, out_specs=_row_spec(ts, W),
        out_shape=jax.ShapeDtypeStruct((T, W), BF16), compiler_params=_cparams(("parallel",)),
    )(q, *tabs)


def loss_head(x, g, target, name="loss_head", ts=256):
    T, D = x.shape
    ts = min(ts, T)

    def body(x_ref, g_ref, t_ref, se_ref, dx_ref, dg_ref):
        xv = x_ref[...]
        r = lax.rsqrt(jnp.mean(xv * xv, axis=-1, keepdims=True) + EPS)
        xh = xv * r
        err = xh * g_ref[...] - t_ref[...]
        dy = err * (1.0 / D)
        dxh = dy * g_ref[...]
        dx_ref[...] = r * (dxh - xh * jnp.mean(dxh * xh, axis=-1, keepdims=True))

        @pl.when(pl.program_id(0) == 0)
        def _():
            se_ref[...] = jnp.zeros_like(se_ref)
            dg_ref[...] = jnp.zeros_like(dg_ref)

        se_ref[...] += jnp.sum(err * err, axis=0, keepdims=True)
        dg_ref[...] += jnp.sum(dy * xh, axis=0, keepdims=True)

    se, dx, dg = pl.pallas_call(
        body, name=name, grid=(T // ts,),
        in_specs=[_row_spec(ts, D), _par_spec(D), _row_spec(ts, D)],
        out_specs=[_par_spec(D), _row_spec(ts, D), _par_spec(D)],
        out_shape=[jax.ShapeDtypeStruct((1, D), F32), jax.ShapeDtypeStruct((T, D), F32), jax.ShapeDtypeStruct((1, D), F32)],
        compiler_params=_cparams(("arbitrary",)),
    )(x, g.reshape(1, D), target)
    return se, dx, dg.reshape(D)


def _dot_nt(a, b):
    return lax.dot_general(a, b, (((1,), (1,)), ((), ())), preferred_element_type=F32)


def _dot_tn(a, b):
    return lax.dot_general(a, b, (((0,), (0,)), ((), ())), preferred_element_type=F32)


def _dot_nn(a, b):
    return lax.dot_general(a, b, (((1,), (0,)), ((), ())), preferred_element_type=F32)


class _Attn:
    def __init__(self, H, dq, dk1, dv, k1_col, v_col, causal, scale, blk=256):
        self.H, self.dq, self.dk1, self.dv = H, dq, dk1, dv
        self.k1_col, self.v_col, self.causal, self.scale, self.blk = k1_col, v_col, causal, scale, blk


def _keys(k1_ref, k2_ref, rows):
    ks = k1_ref[rows, :]
    if k2_ref is not None:
        ks = jnp.concatenate([ks, k2_ref[rows, :]], axis=1)
    return ks


def _mask(s, cfg, i, j, t):
    if not cfg.causal:
        return s
    row = i * t + lax.broadcasted_iota(jnp.int32, s.shape, 0)
    col = j * t + lax.broadcasted_iota(jnp.int32, s.shape, 1)
    return jnp.where(row >= col, s, NEG)


def flash_fwd(cfg, q, k1, v, k2, name):
    Tq, Tk = q.shape[0], k1.shape[0]
    t = min(cfg.blk, Tq, Tk)
    nkb = Tk // t
    has_k2 = k2 is not None

    def body(*refs):
        q_ref, k1_ref, v_ref = refs[:3]
        k2_ref = refs[3] if has_k2 else None
        o_ref, lse_ref = refs[-2], refs[-1]
        i = pl.program_id(1)
        qv = q_ref[...]

        def step(j, carry):
            m, l, acc = carry
            rows = pl.ds(pl.multiple_of(j * t, t), t)
            s = _mask(_dot_nt(qv, _keys(k1_ref, k2_ref, rows)) * cfg.scale, cfg, i, j, t)
            m2 = jnp.maximum(m, jnp.max(s, axis=-1, keepdims=True))
            p = jnp.exp(s - m2)
            alpha = jnp.exp(m - m2)
            l2 = alpha * l + jnp.sum(p, axis=-1, keepdims=True)
            acc2 = alpha * acc + _dot_nn(p.astype(BF16), v_ref[rows, :])
            return m2, l2, acc2

        init = (jnp.full((t, 1), NEG, F32), jnp.zeros((t, 1), F32), jnp.zeros((t, cfg.dv), F32))
        m, l, acc = lax.fori_loop(0, (i + 1) if cfg.causal else nkb, step, init)
        o_ref[...] = (acc / l).astype(o_ref.dtype)
        lse_ref[...] = m + jnp.log(l)

    in_specs = [pl.BlockSpec((t, cfg.dq), lambda h, i: (i, h)),
                pl.BlockSpec((Tk, cfg.dk1), lambda h, i: (0, cfg.k1_col + h)),
                pl.BlockSpec((Tk, cfg.dv), lambda h, i: (0, cfg.v_col + h))]
    args = [q, k1, v]
    if has_k2:
        in_specs.append(pl.BlockSpec((Tk, LANES), lambda h, i: (0, 0)))
        args.append(k2)
    return pl.pallas_call(
        body, name=name, grid=(cfg.H, Tq // t), in_specs=in_specs,
        out_specs=[pl.BlockSpec((t, cfg.dv), lambda h, i: (i, h)), pl.BlockSpec((None, t, 1), lambda h, i: (h, i, 0))],
        out_shape=[jax.ShapeDtypeStruct((Tq, cfg.H * cfg.dv), BF16), jax.ShapeDtypeStruct((cfg.H, Tq, 1), F32)],
        compiler_params=_cparams(("parallel", "parallel")),
    )(*args)


def flash_dq(cfg, q, k1, v, k2, o, do, lse, out_dtype, name):
    Tq, Tk = q.shape[0], k1.shape[0]
    t = min(cfg.blk, Tq, Tk)
    nkb = Tk // t
    has_k2 = k2 is not None

    def body(*refs):
        q_ref, k1_ref, v_ref = refs[:3]
        k2_ref = refs[3] if has_k2 else None
        o_ref, do_ref, lse_ref, dq_ref, dl_ref = refs[-5:]
        i = pl.program_id(1)
        qv, dov, lsev = q_ref[...], do_ref[...], lse_ref[...]
        delta = jnp.sum(dov.astype(F32) * o_ref[...].astype(F32), axis=-1, keepdims=True)
        dl_ref[...] = delta

        def step(j, dq):
            rows = pl.ds(pl.multiple_of(j * t, t), t)
            ks = _keys(k1_ref, k2_ref, rows)
            s = _mask(_dot_nt(qv, ks) * cfg.scale, cfg, i, j, t)
            p = jnp.exp(s - lsev)
            dp = _dot_nt(dov, v_ref[rows, :])
            ds = p * (dp - delta) * cfg.scale
            return dq + _dot_nn(ds.astype(BF16), ks)

        dq = lax.fori_loop(0, (i + 1) if cfg.causal else nkb, step, jnp.zeros((t, cfg.dq), F32))
        dq_ref[...] = dq.astype(dq_ref.dtype)

    in_specs = [pl.BlockSpec((t, cfg.dq), lambda h, i: (i, h)),
                pl.BlockSpec((Tk, cfg.dk1), lambda h, i: (0, cfg.k1_col + h)),
                pl.BlockSpec((Tk, cfg.dv), lambda h, i: (0, cfg.v_col + h))]
    args = [q, k1, v]
    if has_k2:
        in_specs.append(pl.BlockSpec((Tk, LANES), lambda h, i: (0, 0)))
        args.append(k2)
    in_specs += [pl.BlockSpec((t, cfg.dv), lambda h, i: (i, h)), pl.BlockSpec((t, cfg.dv), lambda h, i: (i, h)),
                 pl.BlockSpec((None, t, 1), lambda h, i: (h, i, 0))]
    args += [o, do, lse]
    return pl.pallas_call(
        body, name=name, grid=(cfg.H, Tq // t), in_specs=in_specs,
        out_specs=[pl.BlockSpec((t, cfg.dq), lambda h, i: (i, h)), pl.BlockSpec((None, t, 1), lambda h, i: (h, i, 0))],
        out_shape=[jax.ShapeDtypeStruct((Tq, cfg.H * cfg.dq), out_dtype), jax.ShapeDtypeStruct((cfg.H, Tq, 1), F32)],
        compiler_params=_cparams(("parallel", "parallel")),
    )(*args)


def flash_dkv(cfg, q, k1, v, k2, do, lse, delta, out_dtype, name):
    Tq, Tk = q.shape[0], k1.shape[0]
    t = min(cfg.blk, Tq, Tk)
    nqb = Tq // t
    has_k2 = k2 is not None

    def body(*refs):
        q_ref, k1_ref, v_ref = refs[:3]
        k2_ref = refs[3] if has_k2 else None
        n_in = 4 if has_k2 else 3
        do_ref, lse_ref, dl_ref = refs[n_in:n_in + 3]
        dk1_ref, dv_ref = refs[n_in + 3], refs[n_in + 4]
        j, h = pl.program_id(0), pl.program_id(1)
        ks = _keys(k1_ref, k2_ref, slice(None))
        vs = v_ref[...]

        def step(i, carry):
            dk, dv = carry
            rows = pl.ds(pl.multiple_of(i * t, t), t)
            qi, doi = q_ref[rows, :], do_ref[rows, :]
            s = _mask(_dot_nt(qi, ks) * cfg.scale, cfg, i, j, t)
            p = jnp.exp(s - lse_ref[rows, :])
            dv = dv + _dot_tn(p.astype(BF16), doi)
            ds = p * (_dot_nt(doi, vs) - dl_ref[rows, :]) * cfg.scale
            dk = dk + _dot_tn(ds.astype(BF16), qi)
            return dk, dv

        init = (jnp.zeros((t, cfg.dq), F32), jnp.zeros((t, cfg.dv), F32))
        dk, dv = lax.fori_loop(j if cfg.causal else 0, nqb, step, init)
        dv_ref[...] = dv.astype(dv_ref.dtype)
        dk1_ref[...] = dk[:, 0:cfg.dk1].astype(dk1_ref.dtype)
        if has_k2:
            dk2_ref = refs[n_in + 5]

            @pl.when(h == 0)
            def _():
                dk2_ref[...] = jnp.zeros_like(dk2_ref)

            dk2_ref[...] += dk[:, cfg.dk1:]

    in_specs = [pl.BlockSpec((Tq, cfg.dq), lambda j, h: (0, h)),
                pl.BlockSpec((t, cfg.dk1), lambda j, h: (j, cfg.k1_col + h)),
                pl.BlockSpec((t, cfg.dv), lambda j, h: (j, cfg.v_col + h))]
    args = [q, k1, v]
    if has_k2:
        in_specs.append(pl.BlockSpec((t, LANES), lambda j, h: (j, 0)))
        args.append(k2)
    in_specs += [pl.BlockSpec((Tq, cfg.dv), lambda j, h: (0, h)), pl.BlockSpec((None, Tq, 1), lambda j, h: (h, 0, 0)),
                 pl.BlockSpec((None, Tq, 1), lambda j, h: (h, 0, 0))]
    args += [do, lse, delta]
    out_specs = [pl.BlockSpec((t, cfg.dk1), lambda j, h: (j, h)), pl.BlockSpec((t, cfg.dv), lambda j, h: (j, h))]
    out_shape = [jax.ShapeDtypeStruct((Tk, cfg.H * cfg.dk1), out_dtype), jax.ShapeDtypeStruct((Tk, cfg.H * cfg.dv), out_dtype)]
    if has_k2:
        out_specs.append(pl.BlockSpec((t, LANES), lambda j, h: (j, 0)))
        out_shape.append(jax.ShapeDtypeStruct((Tk, LANES), F32))
    return pl.pallas_call(
        body, name=name, grid=(Tk // t, cfg.H), in_specs=in_specs, out_specs=out_specs, out_shape=out_shape,
        compiler_params=_cparams(("parallel", "arbitrary")),
    )(*args)


def _shift_down(x, s):
    if s == 0:
        return x
    t = lax.broadcasted_iota(jnp.int32, x.shape, 0)
    return jnp.where(t >= s, pltpu.roll(x, s, 0), 0.0)


def _shift_up(x, s):
    if s == 0:
        return x
    n = x.shape[0]
    t = lax.broadcasted_iota(jnp.int32, x.shape, 0)
    return jnp.where(t < n - s, pltpu.roll(x, n - s, 0), 0.0)


def _conv(x, w_ref, kw):
    y = x * w_ref[kw - 1:kw, :]
    for j in range(kw - 1):
        y = y + _shift_down(x, kw - 1 - j) * w_ref[j:j + 1, :]
    return y


def _conv_t(d, w_ref, kw):
    y = d * w_ref[kw - 1:kw, :]
    for j in range(kw - 1):
        y = y + _shift_up(d, kw - 1 - j) * w_ref[j:j + 1, :]
    return y


def _conv_dw(d, x, kw):
    rows = lax.broadcasted_iota(jnp.int32, (kw, d.shape[1]), 0)
    dw = jnp.zeros((kw, d.shape[1]), F32)
    for j in range(kw):
        r = jnp.sum(d * _shift_down(x, kw - 1 - j), axis=0, keepdims=True)
        dw = jnp.where(rows == j, r, dw)
    return dw


def _silu(x):
    return x * jax.nn.sigmoid(x)


def _silu_grad(x):
    s = jax.nn.sigmoid(x)
    return s * (1.0 + x * (1.0 - s))


def gdn_conv_fwd(z, w, name, tc=256):
    T, C = z.shape[0], w.shape[1]
    kw = w.shape[0]

    def body(x_ref, w_ref, o_ref):
        o_ref[...] = _silu(_conv(x_ref[...], w_ref, kw))

    return pl.pallas_call(
        body, name=name, grid=(C // tc,),
        in_specs=[pl.BlockSpec((T, tc), lambda j: (0, j)), pl.BlockSpec((kw, tc), lambda j: (0, j))],
        out_specs=pl.BlockSpec((T, tc), lambda j: (0, j)),
        out_shape=jax.ShapeDtypeStruct((T, C), F32), compiler_params=_cparams(("parallel",)),
    )(z, w)


def gdn_conv_bwd(z, w, dy, name, tc=256):
    T, C = z.shape[0], w.shape[1]
    kw = w.shape[0]

    def body(x_ref, w_ref, dy_ref, dx_ref, dw_ref):
        xv = x_ref[...]
        dc = dy_ref[...] * _silu_grad(_conv(xv, w_ref, kw))
        dx_ref[...] = _conv_t(dc, w_ref, kw).astype(dx_ref.dtype)
        dw_ref[...] = _conv_dw(dc, xv, kw)

    col = lambda j: (0, j)
    return pl.pallas_call(
        body, name=name, grid=(C // tc,),
        in_specs=[pl.BlockSpec((T, tc), col), pl.BlockSpec((kw, tc), col), pl.BlockSpec((T, tc), col)],
        out_specs=[pl.BlockSpec((T, tc), col), pl.BlockSpec((kw, tc), col)],
        out_shape=[jax.ShapeDtypeStruct((T, C), BF16), jax.ShapeDtypeStruct((kw, C), F32)],
        compiler_params=_cparams(("parallel",)),
    )(z, w, dy)


def sc_fwd(z, w, name, tc=256):
    T, C = z.shape[0], w.shape[1]
    kw, nb = w.shape[0], C // tc

    def body(b_ref, c_ref, u_ref, w_ref, o_ref):
        o_ref[...] = (b_ref[...] * _conv(c_ref[...] * u_ref[...], w_ref, kw)).astype(o_ref.dtype)

    return pl.pallas_call(
        body, name=name, grid=(nb,),
        in_specs=[pl.BlockSpec((T, tc), lambda j: (0, j)), pl.BlockSpec((T, tc), lambda j: (0, nb + j)),
                  pl.BlockSpec((T, tc), lambda j: (0, 2 * nb + j)), pl.BlockSpec((kw, tc), lambda j: (0, j))],
        out_specs=pl.BlockSpec((T, tc), lambda j: (0, j)),
        out_shape=jax.ShapeDtypeStruct((T, C), BF16), compiler_params=_cparams(("parallel",)),
    )(z, z, z, w)


def sc_bwd(z, w, dy, name, tc=256):
    T, C = z.shape[0], w.shape[1]
    kw, nb = w.shape[0], C // tc

    def body(b_ref, c_ref, u_ref, w_ref, dy_ref, db_ref, dc_ref, du_ref, dw_ref):
        cv, uv, dyv = c_ref[...], u_ref[...], dy_ref[...]
        cu = cv * uv
        db_ref[...] = (dyv * _conv(cu, w_ref, kw)).astype(db_ref.dtype)
        dcv = dyv * b_ref[...]
        dcu = _conv_t(dcv, w_ref, kw)
        dc_ref[...] = (dcu * uv).astype(dc_ref.dtype)
        du_ref[...] = (dcu * cv).astype(du_ref.dtype)
        dw_ref[...] = _conv_dw(dcv, cu, kw)

    col = lambda j: (0, j)
    act = jax.ShapeDtypeStruct((T, C), BF16)
    return pl.pallas_call(
        body, name=name, grid=(nb,),
        in_specs=[pl.BlockSpec((T, tc), col), pl.BlockSpec((T, tc), lambda j: (0, nb + j)),
                  pl.BlockSpec((T, tc), lambda j: (0, 2 * nb + j)), pl.BlockSpec((kw, tc), col), pl.BlockSpec((T, tc), col)],
        out_specs=[pl.BlockSpec((T, tc), col)] * 3 + [pl.BlockSpec((kw, tc), col)],
        out_shape=[act, act, act, jax.ShapeDtypeStruct((kw, C), F32)],
        compiler_params=_cparams(("parallel",)),
    )(z, z, z, w, dy)


def _hdot(a, b, dims):
    return lax.dot_general(a, b, (dims, ((), ())), precision=HI, preferred_element_type=F32)


def _hnn(a, b):
    return _hdot(a, b, ((1,), (0,)))


def _hnt(a, b):
    return _hdot(a, b, ((1,), (1,)))


def _htn(a, b):
    return _hdot(a, b, ((0,), (0,)))


@jax.custom_vjp
def _unit_lower_inverse(m):
    c = m.shape[0]
    eye = (lax.broadcasted_iota(jnp.int32, (c, c), 0) == lax.broadcasted_iota(jnp.int32, (c, c), 1)).astype(F32)
    t = eye - m
    p = _hnn(m, m)
    n = 2
    while n < c:
        t = t + _hnn(t, p)
        n *= 2
        if n < c:
            p = _hnn(p, p)
    return t


def _uli_fwd(m):
    t = _unit_lower_inverse(m)
    return t, t


def _uli_bwd(t, dt):
    return (-_htn(t, _hnt(dt, t)),)


_unit_lower_inverse.defvjp(_uli_fwd, _uli_bwd)


def _gdn_chunk(q, k, v, gate, bl, al, a_log, dt_bias, o_norm, st):
    c = q.shape[0]
    ii = lax.broadcasted_iota(jnp.int32, (c, c), 0)
    jj = lax.broadcasted_iota(jnp.int32, (c, c), 1)
    tri, strict = ii >= jj, ii > jj
    q = q * lax.rsqrt(jnp.sum(q * q, -1, keepdims=True) + EPS) * (GDN_D ** -0.5)
    k = k * lax.rsqrt(jnp.sum(k * k, -1, keepdims=True) + EPS)
    beta = jax.nn.sigmoid(bl)
    g = -jnp.exp(a_log) * jax.nn.softplus(al + dt_bias)
    gc = _hnn(tri.astype(F32), g)
    gcol = _hnn(gc, jnp.full((LANES, c), 1.0 / LANES, F32))
    grow = _hnt(jnp.full((c, LANES), 1.0 / LANES, F32), gc)
    decay = jnp.where(tri, jnp.exp(jnp.where(tri, gcol - grow, 0.0)), 0.0)
    kb = k * beta
    m = jnp.where(strict, _hnt(kb, k) * decay, 0.0)
    t_inv = _unit_lower_inverse(m)
    eg = jnp.exp(gc)
    u = _hnn(t_inv, v * beta)
    w = _hnn(t_inv, kb * eg)
    attn = _hnt(q, k) * decay
    v_new = u - _hnn(w, st)
    o = _hnn(q * eg, st) + _hnn(attn, v_new)
    g_last = jnp.sum(g, axis=0, keepdims=True)
    st_new = st * jnp.exp(g_last) + _htn(k * jnp.exp(g_last - gc), v_new)
    o = o * lax.rsqrt(jnp.mean(o * o, -1, keepdims=True) + EPS) * o_norm
    return o * _silu(gate), st_new


def _gdn_specs(n_chunks, rev):
    def tok(col):
        if rev:
            return pl.BlockSpec((GDN_C, GDN_D), lambda h, n: (n_chunks - 1 - n, col + h))
        return pl.BlockSpec((GDN_C, GDN_D), lambda h, n: (n, col + h))
    par = pl.BlockSpec((1, GDN_D), lambda h, n: (0, h))
    shared = pl.BlockSpec((1, GDN_D), lambda h, n: (0, 0))
    if rev:
        st = pl.BlockSpec((None, None, GDN_D, GDN_D), lambda h, n: (h, n_chunks - 1 - n, 0, 0))
    else:
        st = pl.BlockSpec((None, None, GDN_D, GDN_D), lambda h, n: (h, n, 0, 0))
    return tok, par, shared, st


def gdn_chunk_fwd(qkv, z, a_log_x, dt_bias_x, o_norm, name):
    T = qkv.shape[0]
    n_chunks = T // GDN_C
    H = GDN_H
    tok, par, shared, st_spec = _gdn_specs(n_chunks, False)

    def body(q_ref, k_ref, v_ref, g_ref, bl_ref, al_ref, a_ref, dt_ref, on_ref, o_ref, st_ref, state):
        @pl.when(pl.program_id(1) == 0)
        def _():
            state[...] = jnp.zeros_like(state)

        st = state[...]
        st_ref[...] = st
        o, st_new = _gdn_chunk(q_ref[...], k_ref[...], v_ref[...], g_ref[...], bl_ref[...], al_ref[...],
                               a_ref[...], dt_ref[...], on_ref[...], st)
        o_ref[...] = o.astype(o_ref.dtype)
        state[...] = st_new

    return pl.pallas_call(
        body, name=name, grid=(H, n_chunks),
        in_specs=[tok(0), tok(H), tok(2 * H), tok(3 * H), tok(4 * H), tok(5 * H), par, par, shared],
        out_specs=[tok(0), st_spec],
        out_shape=[jax.ShapeDtypeStruct((T, H * GDN_D), BF16), jax.ShapeDtypeStruct((H, n_chunks, GDN_D, GDN_D), F32)],
        scratch_shapes=[pltpu.VMEM((GDN_D, GDN_D), F32)],
        compiler_params=_cparams(("parallel", "arbitrary")),
    )(qkv, qkv, qkv, z, z, z, a_log_x, dt_bias_x, o_norm)


def gdn_chunk_bwd(qkv, z, a_log_x, dt_bias_x, o_norm, states, do, name):
    T = qkv.shape[0]
    n_chunks = T // GDN_C
    H = GDN_H
    tok, par, shared, st_spec = _gdn_specs(n_chunks, True)

    def body(q_ref, k_ref, v_ref, g_ref, bl_ref, al_ref, a_ref, dt_ref, on_ref, st_ref, do_ref,
             dq_ref, dk_ref, dv_ref, dg_ref, dbl_ref, dal_ref, da_ref, ddt_ref, don_ref, dstate):
        h, n = pl.program_id(0), pl.program_id(1)

        @pl.when(n == 0)
        def _():
            dstate[...] = jnp.zeros_like(dstate)
            da_ref[...] = jnp.zeros_like(da_ref)
            ddt_ref[...] = jnp.zeros_like(ddt_ref)

        @pl.when((n == 0) & (h == 0))
        def _():
            don_ref[...] = jnp.zeros_like(don_ref)

        _, vjp = jax.vjp(_gdn_chunk, q_ref[...], k_ref[...], v_ref[...], g_ref[...], bl_ref[...], al_ref[...],
                         a_ref[...], dt_ref[...], on_ref[...], st_ref[...])
        dq, dk, dv, dg, dbl, dal, da, ddt, don, dst = vjp((do_ref[...].astype(F32), dstate[...]))
        dq_ref[...] = dq
        dk_ref[...] = dk
        dv_ref[...] = dv
        dg_ref[...] = dg.astype(dg_ref.dtype)
        dbl_ref[...] = dbl.astype(dbl_ref.dtype)
        dal_ref[...] = dal.astype(dal_ref.dtype)
        da_ref[...] += da
        ddt_ref[...] += ddt
        don_ref[...] += don
        dstate[...] = dst

    tok0 = tok(0)
    f32_tok = jax.ShapeDtypeStruct((T, H * GDN_D), F32)
    bf_tok = jax.ShapeDtypeStruct((T, H * GDN_D), BF16)
    par_sh = jax.ShapeDtypeStruct((1, H * GDN_D), F32)
    return pl.pallas_call(
        body, name=name, grid=(H, n_chunks),
        in_specs=[tok(0), tok(H), tok(2 * H), tok(3 * H), tok(4 * H), tok(5 * H), par, par, shared, st_spec, tok0],
        out_specs=[tok0] * 6 + [par, par, shared],
        out_shape=[f32_tok, f32_tok, f32_tok, bf_tok, bf_tok, bf_tok, par_sh, par_sh, jax.ShapeDtypeStruct((1, GDN_D), F32)],
        scratch_shapes=[pltpu.VMEM((GDN_D, GDN_D), F32)],
        compiler_params=_cparams(("arbitrary", "arbitrary")),
    )(qkv, qkv, qkv, z, z, z, a_log_x, dt_bias_x, o_norm, states, do)


def _prefetch_call(body, name, grid, in_specs, out_specs, out_shape, aliases=None):
    return pl.pallas_call(
        body, name=name,
        grid_spec=pltpu.PrefetchScalarGridSpec(num_scalar_prefetch=1, grid=grid, in_specs=in_specs, out_specs=out_specs),
        out_shape=out_shape, input_output_aliases=aliases or {},
        compiler_params=_cparams(("parallel",) * len(grid)))


def cast_into(src, slab, row0, me, name):
    rows, width = src.shape
    tr = _pick_rows(rows)
    assert row0 % tr == 0

    def body(me_ref, s_ref, *refs):
        refs[-1][...] = s_ref[...].astype(refs[-1].dtype)

    in_specs = [pl.BlockSpec((tr, width), lambda r, me_ref: (r, 0))]
    args = [src]
    aliases = {}
    if slab.arr is not None:
        in_specs.append(_ANY)
        args.append(slab.arr)
        aliases = {2: 0}
    slab.arr = _prefetch_call(
        body, name, (rows // tr,), in_specs,
        pl.BlockSpec((None, tr, width), lambda r, me_ref: (me_ref[0], row0 // tr + r, 0)),
        jax.ShapeDtypeStruct(slab.shape, slab.dtype), aliases)(me, *args)


def pair_add(g, b, c_idx, name):
    n, rh, w = b.shape
    tr = _pick_rows(rh)
    nb = rh // tr

    def body(c_ref, g_ref, b_ref, o_ref):
        o_ref[...] = (g_ref[...].astype(F32) + b_ref[...].astype(F32)).astype(o_ref.dtype)

    return _prefetch_call(
        body, name, (n, nb),
        [pl.BlockSpec((None, tr, w), lambda k, r, c: (k, c[0] * nb + r, 0)), pl.BlockSpec((None, tr, w), lambda k, r, c: (k, r, 0))],
        pl.BlockSpec((None, tr, w), lambda k, r, c: (k, r, 0)), jax.ShapeDtypeStruct(b.shape, BF16))(c_idx, g, b)


def chip_sum(p, rv, mc, name):
    n, rh, w = p.shape
    tr = _pick_rows(rh)
    nb = rh // tr

    def body(mc_ref, p_ref, rv_ref, o_ref):
        me = mc_ref[0]
        acc = None
        for k in range(n):
            part = jnp.where(me == k, p_ref[...], rv_ref[k]).astype(F32)
            acc = part if acc is None else acc + part
        o_ref[...] = acc

    return _prefetch_call(
        body, name, (nb,),
        [pl.BlockSpec((None, tr, w), lambda r, mc_ref: (mc_ref[0], r, 0)), pl.BlockSpec((n, tr, w), lambda r, mc_ref: (0, r, 0))],
        pl.BlockSpec((tr, w), lambda r, mc_ref: (mc_ref[1] * nb + r, 0)), jax.ShapeDtypeStruct((2 * rh, w), F32))(mc, p, rv)


def adamw(red, row0, w, m, v, name):
    rows, cols = w.shape
    tr = _pick_rows(rows)
    assert row0 % tr == 0

    def body(g_ref, w_ref, m_ref, v_ref, go_ref, d_ref, nm_ref, nv_ref):
        gv = g_ref[...]
        nm = ADAM_B1 * m_ref[...] + (1.0 - ADAM_B1) * gv
        nv = ADAM_B2 * v_ref[...] + (1.0 - ADAM_B2) * (gv * gv)
        m_hat = nm / (1.0 - ADAM_B1 ** ADAM_STEP)
        v_hat = nv / (1.0 - ADAM_B2 ** ADAM_STEP)
        go_ref[...] = gv
        d_ref[...] = -ADAM_LR * (m_hat / (jnp.sqrt(v_hat) + ADAM_EPS) + ADAM_WD * w_ref[...])
        nm_ref[...] = nm
        nv_ref[...] = nv

    spec = pl.BlockSpec((tr, cols), lambda r: (r, 0))
    sh = jax.ShapeDtypeStruct((rows, cols), F32)
    return pl.pallas_call(
        body, name=name, grid=(rows // tr,),
        in_specs=[pl.BlockSpec((tr, cols), lambda r: (row0 // tr + r, 0)), spec, spec, spec],
        out_specs=[spec] * 4, out_shape=[sh] * 4, compiler_params=_cparams(("parallel",)),
    )(red, w, m, v)


def _place():
    x, y, c = lax.axis_index("x"), lax.axis_index("y"), lax.axis_index("c")
    chips = [(1 - x, y), (x, 1 - y), (1 - x, 1 - y)]
    return x, y, c, chips


def _chip_index(cx, cy):
    return 2 * cx + cy


def _remote(src, dst, send_sem, recv_sem, to):
    return pltpu.make_async_remote_copy(src_ref=src, dst_ref=dst, send_sem=send_sem, recv_sem=recv_sem,
                                        device_id=to, device_id_type=MESH)


def _comm_call(body, name, ins, out_shapes, n_sems, aliases):
    return pl.pallas_call(
        body, name=name, in_specs=[_ANY] * len(ins), out_specs=[_ANY] * len(out_shapes), out_shape=out_shapes,
        scratch_shapes=[pltpu.SemaphoreType.DMA((n_sems,)), pltpu.SemaphoreType.DMA((n_sems,))],
        input_output_aliases=aliases,
    )(*ins)


def gather_slabs(slabs, name="weight_all_gather"):
    n = len(slabs)

    def body(*refs):
        in_refs, out_refs, send_sems, recv_sems = refs[:n], refs[n:2 * n], refs[-2], refs[-1]
        x, y, c, chips = _place()
        me = _chip_index(x, y)
        sib = (x, y, 1 - c)
        first, passed = [], []
        for a in range(n):
            rh = in_refs[a].shape[1] // 2
            mine = pl.ds(c * rh, rh)
            for j, chip in enumerate(chips):
                cp = _remote(in_refs[a].at[me, mine], out_refs[a].at[me, mine], send_sems.at[6 * a + j],
                             recv_sems.at[6 * a + j], (*chip, c))
                cp.start()
                first.append(cp)
        for a in range(n):
            rh = in_refs[a].shape[1] // 2
            mine = pl.ds(c * rh, rh)
            for j, chip in enumerate(chips):
                landed = out_refs[a].at[_chip_index(*chip), mine]
                _remote(landed, landed, send_sems.at[6 * a + j], recv_sems.at[6 * a + j], (*chip, c)).wait_recv()
                cp = _remote(landed, landed, send_sems.at[6 * a + 3 + j], recv_sems.at[6 * a + 3 + j], sib)
                cp.start()
                passed.append(cp)
        for a in range(n):
            rh = in_refs[a].shape[1] // 2
            theirs = pl.ds((1 - c) * rh, rh)
            for j, chip in enumerate(chips):
                got = out_refs[a].at[_chip_index(*chip), theirs]
                _remote(got, got, send_sems.at[6 * a + 3 + j], recv_sems.at[6 * a + 3 + j], sib).wait_recv()
        for cp in first + passed:
            cp.wait_send()

    return _comm_call(body, name, slabs, [jax.ShapeDtypeStruct(s.shape, s.dtype) for s in slabs], 6 * n,
                      {a: a for a in range(n)})


def pair_swap_halves(slabs, name="grad_pair_swap"):
    n = len(slabs)

    def body(*refs):
        in_refs, out_refs, send_sems, recv_sems = refs[:n], refs[n:2 * n], refs[-2], refs[-1]
        x, y, c, _ = _place()
        cps = []
        for a in range(n):
            rh = in_refs[a].shape[1] // 2
            cp = _remote(in_refs[a].at[:, pl.ds((1 - c) * rh, rh), :], out_refs[a], send_sems.at[a], recv_sems.at[a], (x, y, 1 - c))
            cp.start()
            cps.append(cp)
        for cp in cps:
            cp.wait()

    outs = [jax.ShapeDtypeStruct((s.shape[0], s.shape[1] // 2, s.shape[2]), s.dtype) for s in slabs]
    return _comm_call(body, name, slabs, outs, n, {})


def chip_exchange(parts, name="grad_chip_exchange"):
    n = len(parts)

    def body(*refs):
        in_refs, out_refs, send_sems, recv_sems = refs[:n], refs[n:2 * n], refs[-2], refs[-1]
        x, y, c, chips = _place()
        me = _chip_index(x, y)
        sends = []
        for a in range(n):
            for j, chip in enumerate(chips):
                cp = _remote(in_refs[a].at[_chip_index(*chip)], out_refs[a].at[me], send_sems.at[3 * a + j],
                             recv_sems.at[3 * a + j], (*chip, c))
                cp.start()
                sends.append(cp)
        for a in range(n):
            for j, chip in enumerate(chips):
                got = out_refs[a].at[_chip_index(*chip)]
                _remote(got, got, send_sems.at[3 * a + j], recv_sems.at[3 * a + j], (*chip, c)).wait_recv()
        for cp in sends:
            cp.wait_send()

    return _comm_call(body, name, parts, [jax.ShapeDtypeStruct(p.shape, p.dtype) for p in parts], 3 * n, {})


def pair_join_halves(reds, name="grad_pair_join"):
    n = len(reds)

    def body(*refs):
        in_refs, out_refs, send_sems, recv_sems = refs[:n], refs[n:2 * n], refs[-2], refs[-1]
        x, y, c, _ = _place()
        cps = []
        for a in range(n):
            rh = in_refs[a].shape[0] // 2
            mine = pl.ds(c * rh, rh)
            cp = _remote(in_refs[a].at[mine], out_refs[a].at[mine], send_sems.at[a], recv_sems.at[a], (x, y, 1 - c))
            cp.start()
            cps.append(cp)
        for a in range(n):
            rh = in_refs[a].shape[0] // 2
            got = out_refs[a].at[pl.ds((1 - c) * rh, rh)]
            _remote(got, got, send_sems.at[a], recv_sems.at[a], (x, y, 1 - c)).wait_recv()
        for cp in cps:
            cp.wait_send()

    return _comm_call(body, name, reds, [jax.ShapeDtypeStruct(r.shape, r.dtype) for r in reds], n, {a: a for a in range(n)})


_SLABS = {
    "w1024": [("mlp_w1", 2), ("mlp_w2", 1), ("xa_w_q", 1), ("xa_w_o", 1), ("mla_w_o", 1), ("gdn_w_o", 1), ("sc_w_o", 1)],
    "xa_w_kv": [("xa_w_kv", 2)], "sc_w_in": [("sc_w_in", 2)],
    "mla_w_in": [("mla_w_in", 1)], "mla_w_uq": [("mla_w_uq", 2)], "mla_w_ukv": [("mla_w_ukv", 2)], "gdn_w_in": [("gdn_w_in", 2)],
}
_RELAID = ("mla_w_in", "mla_w_uq", "mla_w_ukv", "gdn_w_in")
_SMALL = [("mla_q_norm", 1), ("mla_kv_norm", 1), ("gdn_conv_w", 2), ("sc_conv_w", 2)]
_REPL = ["gdn_a_log", "gdn_dt_bias", "gdn_o_norm", "norm_mix", "norm_mem", "norm_mlp", "mem_norm", "final_norm"]
_WEIGHTS = ['mla_w_in', 'mla_q_norm', 'mla_kv_norm', 'mla_w_uq', 'mla_w_ukv', 'mla_w_o', 'gdn_w_in', 'gdn_conv_w',
            'gdn_a_log', 'gdn_dt_bias', 'gdn_o_norm', 'gdn_w_o', 'sc_w_in', 'sc_conv_w', 'sc_w_o', 'norm_mix',
            'norm_mem', 'norm_mlp', 'xa_w_q', 'xa_w_kv', 'xa_w_o', 'mlp_w1', 'mlp_w2', 'mem_norm', 'final_norm']


class Layout:
    def __init__(self, shard_shapes):
        self.where, self.slab_dims = {}, {}
        for slab, members in _SLABS.items():
            off = 0
            for name, axis in members:
                layers, rpl, width = shard_shapes[name]
                self.where[name] = (slab, off, layers, rpl, width, axis)
                off += layers * rpl
            self.slab_dims[slab] = (off, width)

    def new_slabs(self, dtype):
        return {s: Slab(rows, width, dtype) for s, (rows, width) in self.slab_dims.items()}

    def loc(self, slabs, name, layer):
        slab, off, _, rpl, width, axis = self.where[name]
        if axis == 1:
            return Loc(slabs[slab], off + layer * rpl, N_CHIPS * rpl, width, 0)
        return Loc(slabs[slab], off + layer * rpl, rpl, N_CHIPS * width, 1)

    def full(self, slabs, name):
        slab, off, layers, rpl, width, axis = self.where[name]
        blocks = slabs[slab].arr[:, off:off + layers * rpl].reshape(N_CHIPS, layers, rpl, width)
        return jnp.concatenate([blocks[s] for s in range(N_CHIPS)], axis=axis)

    def put_full(self, slabs, name, grad):
        slab, off, layers, rpl, width, axis = self.where[name]
        assert off == 0 and len(_SLABS[slab]) == 1
        parts = jnp.stack(jnp.split(grad, N_CHIPS, axis=axis)).reshape(N_CHIPS, layers * rpl, width)
        slabs[slab].arr = parts.astype(slabs[slab].dtype)


def _small_pack(vals, names):
    flat = jnp.concatenate([vals[n].astype(F32).reshape(-1) for n in names])
    return jnp.pad(flat, (0, SMALL_ROWS * SMALL_COLS - flat.shape[0])).reshape(SMALL_ROWS, SMALL_COLS)


def _small_unpack(flat, like, names):
    out, off = {}, 0
    flat = flat.reshape(-1)
    for n in names:
        out[n] = flat[off:off + like[n].size].reshape(like[n].shape)
        off += like[n].size
    return out


_MLA_CFG = _Attn(MLA_H, 2 * LANES, MLA_NOPE, MLA_V, 0, MLA_H, True, (MLA_NOPE + MLA_ROPE) ** -0.5)
_XA_CFG = _Attn(XA_H, XA_D, XA_D, XA_D, 0, XA_H, False, XA_D ** -0.5)


def _mla_weights(w_in, w_uq, w_ukv):
    w_in_p = jnp.pad(w_in, ((0, 0), (0, MLA_ZPAD - w_in.shape[1])))
    w_uq_p = jnp.pad(w_uq.reshape(MLA_QR, MLA_H, MLA_NOPE + MLA_ROPE), ((0, 0), (0, 0), (0, 2 * LANES - MLA_NOPE - MLA_ROPE)))
    w_uq_p = w_uq_p.reshape(MLA_QR, MLA_H * 2 * LANES)
    kv = w_ukv.reshape(MLA_KVR, MLA_H, MLA_NOPE + MLA_V)
    w_ukv_p = jnp.concatenate([kv[:, :, :MLA_NOPE].reshape(MLA_KVR, -1), kv[:, :, MLA_NOPE:].reshape(MLA_KVR, -1)], axis=1)
    return w_in_p, w_uq_p, w_ukv_p


def _mla_weight_grads(d_in_p, d_uq_p, d_ukv_p):
    d_in = d_in_p[:, :MLA_QR + MLA_KVR + MLA_ROPE]
    d_uq = d_uq_p.reshape(MLA_QR, MLA_H, 2 * LANES)[:, :, :MLA_NOPE + MLA_ROPE].reshape(MLA_QR, -1)
    half = MLA_H * MLA_NOPE
    d_ukv = jnp.concatenate([d_ukv_p[:, :half].reshape(MLA_KVR, MLA_H, MLA_NOPE),
                             d_ukv_p[:, half:].reshape(MLA_KVR, MLA_H, MLA_V)], axis=2).reshape(MLA_KVR, -1)
    return d_in, d_uq, d_ukv


def _mla_fwd(xs, h, wts, w_o, qn, kvn, tabs, tag):
    w_in_p, w_uq_p, w_ukv_p = wts
    z = mm(h, w_in_p, "nn", f"{tag}_in")
    cq, ckv, kr = mla_mid_fwd(z, qn, kvn, tabs, f"{tag}_mid")
    q = rope_q(mm(cq, w_uq_p, "nn", f"{tag}_uq"), tabs, False, f"{tag}_ropeq")
    kv = mm(ckv, w_ukv_p, "nn", f"{tag}_ukv", outs=(BF16,))
    o, lse = flash_fwd(_MLA_CFG, q, kv, kv, kr, f"{tag}_attn")
    xs = mm(o, w_o, "nn", f"{tag}_out", epi=_epi_add, extras=(xs,))
    return xs, (z, cq, ckv, kr, q, kv, o, lse)


def _mla_bwd(dx, h, wts, w_o, g_wo, qn, kvn, tabs, saved, tag):
    w_in_p, w_uq_p, w_ukv_p = wts
    z, cq, ckv, kr, q, kv, o, lse = saved
    mm(o, dx, "tn", f"{tag}_dwo", outs=(BF16,), out_loc=g_wo)
    do = mm(dx, w_o, "nt", f"{tag}_do", outs=(BF16,))
    dq, delta = flash_dq(_MLA_CFG, q, kv, kv, kr, o, do, lse, F32, f"{tag}_attn_dq")
    dk1, dv, dkr = flash_dkv(_MLA_CFG, q, kv, kv, kr, do, lse, delta, BF16, f"{tag}_attn_dkv")
    dqp = rope_q(dq, tabs, True, f"{tag}_ropeq_t")
    d_uq_p = mm(cq, dqp, "tn", f"{tag}_duq")
    dcq = mm(dqp, w_uq_p, "nt", f"{tag}_dcq")
    dkv = jnp.concatenate([dk1, dv], axis=1)
    d_ukv_p = mm(ckv, dkv, "tn", f"{tag}_dukv")
    dckv = mm(dkv, w_ukv_p, "nt", f"{tag}_dckv")
    dz, dqn, dkvn = mla_mid_bwd(z, qn, kvn, tabs, dcq, dckv, dkr, f"{tag}_mid_bwd")
    d_in_p = mm(h, dz, "tn", f"{tag}_din")
    dh = mm(dz, w_in_p, "nt", f"{tag}_dh")
    d_in, d_uq, d_ukv = _mla_weight_grads(d_in_p, d_uq_p, d_ukv_p)
    return dh, dict(mla_w_in=d_in, mla_w_uq=d_uq, mla_w_ukv=d_ukv, mla_q_norm=dqn, mla_kv_norm=dkvn)


_GDN_QKV = 3 * GDN_H * GDN_D
_GDN_GATE_END = _GDN_QKV + GDN_H * GDN_D


def _gdn_weights(w_in):
    rep = lambda cols: jnp.repeat(cols, GDN_D, axis=1)
    return jnp.concatenate([w_in[:, :_GDN_GATE_END], rep(w_in[:, _GDN_GATE_END:_GDN_GATE_END + GDN_H]),
                            rep(w_in[:, _GDN_GATE_END + GDN_H:])], axis=1)


def _fold(x):
    return x.reshape(x.shape[0], -1, GDN_D).sum(-1)


def _gdn_fwd(xs, h, w_in_x, conv_w, a_log, dt_bias, o_norm, w_o, tag):
    z = mm(h, w_in_x, "nn", f"{tag}_in")
    qkv = gdn_conv_fwd(z, conv_w, f"{tag}_conv")
    a_x, dt_x = jnp.repeat(a_log.reshape(1, -1), GDN_D, axis=1), jnp.repeat(dt_bias.reshape(1, -1), GDN_D, axis=1)
    og, states = gdn_chunk_fwd(qkv, z, a_x, dt_x, o_norm.reshape(1, -1), f"{tag}_chunks")
    xs = mm(og, w_o, "nn", f"{tag}_out", epi=_epi_add, extras=(xs,))
    return xs, (z, qkv, a_x, dt_x, og, states)


def _gdn_bwd(dx, h, w_in_x, conv_w, o_norm, w_o, g_wo, saved, tag):
    z, qkv, a_x, dt_x, og, states = saved
    mm(og, dx, "tn", f"{tag}_dwo", outs=(BF16,), out_loc=g_wo)
    dog = mm(dx, w_o, "nt", f"{tag}_dog")
    dq, dk, dv, dgate, dbl, dal, da_x, ddt_x, don = gdn_chunk_bwd(qkv, z, a_x, dt_x, o_norm.reshape(1, -1), states, dog,
                                                                  f"{tag}_chunks_bwd")
    dpre, dconv = gdn_conv_bwd(z, conv_w, jnp.concatenate([dq, dk, dv], axis=1), f"{tag}_conv_bwd")
    dz = jnp.concatenate([dpre, dgate, dbl, dal], axis=1)
    d_in_x = mm(h, dz, "tn", f"{tag}_din")
    dh = mm(dz, w_in_x, "nt", f"{tag}_dh", tn=512)
    ge = _GDN_GATE_END
    d_in = jnp.concatenate([d_in_x[:, :ge], _fold(d_in_x[:, ge:ge + GDN_H * GDN_D]), _fold(d_in_x[:, ge + GDN_H * GDN_D:])], axis=1)
    return dh, dict(gdn_w_in=d_in, gdn_conv_w=dconv, gdn_a_log=_fold(da_x).reshape(-1), gdn_dt_bias=_fold(ddt_x).reshape(-1),
                    gdn_o_norm=don.reshape(-1))


def _sc_fwd(xs, h, w_in, conv_w, w_o, tag):
    z = mm(h, w_in, "nn", f"{tag}_in")
    y = sc_fwd(z, conv_w, f"{tag}_conv")
    xs = mm(y, w_o, "nn", f"{tag}_out", epi=_epi_add, extras=(xs,))
    return xs, (z, y)


def _sc_bwd(dx, h, w_in, g_win, conv_w, w_o, g_wo, saved, tag):
    z, y = saved
    mm(y, dx, "tn", f"{tag}_dwo", outs=(BF16,), out_loc=g_wo)
    dy = mm(dx, w_o, "nt", f"{tag}_dy")
    db, dc, du, dconv = sc_bwd(z, conv_w, dy, f"{tag}_conv_bwd")
    dz = jnp.concatenate([db, dc, du], axis=1)
    mm(h, dz, "tn", f"{tag}_din", outs=(BF16,), out_loc=g_win)
    dh = mm(dz, w_in, "nt", f"{tag}_dh")
    return dh, dict(sc_conv_w=dconv)


def local_step(x, mem, pos, target, lay, wslabs, gslabs, small):
    depth = small["norm_mix"].shape[0]
    W = lambda name, layer: lay.loc(wslabs, name, layer)
    G = lambda name, layer: lay.loc(gslabs, name, layer)
    tabs = rope_tables(pos)
    mem_n = rmsnorm_fwd(mem, small["mem_norm"], "mem_norm")
    full = {n: lay.full(wslabs, n) for n in _RELAID}
    mla_w = [_mla_weights(full["mla_w_in"][j], full["mla_w_uq"][j], full["mla_w_ukv"][j]) for j in range(full["mla_w_in"].shape[0])]
    gdn_in_x = [_gdn_weights(full["gdn_w_in"][j]) for j in range(full["gdn_w_in"].shape[0])]

    xs = x
    saved = []
    for i in range(depth):
        j, kind = i // 3, i % 3
        tag = f"l{i}"
        x_a = xs
        h = rmsnorm_fwd(xs, small["norm_mix"][i], f"{tag}_norm_mix")
        if kind == 0:
            xs, mix = _mla_fwd(xs, h, mla_w[j], W("mla_w_o", j), small["mla_q_norm"][j], small["mla_kv_norm"][j], tabs, f"{tag}_mla")
        elif kind == 1:
            xs, mix = _gdn_fwd(xs, h, gdn_in_x[j], small["gdn_conv_w"][j], small["gdn_a_log"][j], small["gdn_dt_bias"][j],
                               small["gdn_o_norm"][j], W("gdn_w_o", j), f"{tag}_gdn")
        else:
            xs, mix = _sc_fwd(xs, h, W("sc_w_in", j), small["sc_conv_w"][j], W("sc_w_o", j), f"{tag}_sc")
        x_b = xs
        hn = rmsnorm_fwd(xs, small["norm_mem"][i], f"{tag}_norm_mem")
        xq = mm(hn, W("xa_w_q", i), "nn", f"{tag}_xa_q", outs=(BF16,))
        xkv = mm(mem_n, W("xa_w_kv", i), "nn", f"{tag}_xa_kv", outs=(BF16,))
        xo, xlse = flash_fwd(_XA_CFG, xq, xkv, xkv, None, f"{tag}_xa_attn")
        xs = mm(xo, W("xa_w_o", i), "nn", f"{tag}_xa_out", epi=_epi_add, extras=(xs,))
        x_c = xs
        hm = rmsnorm_fwd(xs, small["norm_mlp"][i], f"{tag}_norm_mlp")
        h1, act = mm(hm, W("mlp_w1", i), "nn", f"{tag}_mlp_up", outs=(BF16, BF16), epi=_epi_relu2)
        xs = mm(act, W("mlp_w2", i), "nn", f"{tag}_mlp_down", epi=_epi_add, extras=(xs,))
        saved.append((x_a, h, mix, x_b, hn, xq, xkv, xo, xlse, x_c, hm, h1, act))

    se, dx, d_final = loss_head(xs, small["final_norm"], target)

    per_layer = {n: [None] * depth for n in ("norm_mix", "norm_mem", "norm_mlp")}
    mixer = {}
    dmem_n = jnp.zeros(mem.shape, F32)
    for i in reversed(range(depth)):
        j, kind = i // 3, i % 3
        tag = f"l{i}"
        x_a, h, mix, x_b, hn, xq, xkv, xo, xlse, x_c, hm, h1, act = saved[i]
        mm(act, dx, "tn", f"{tag}_mlp_dw2", outs=(BF16,), out_loc=G("mlp_w2", i))
        dh1 = mm(dx, W("mlp_w2", i), "nt", f"{tag}_mlp_dh1", outs=(BF16,), epi=_epi_relu2_bwd, extras=(h1,))
        mm(hm, dh1, "tn", f"{tag}_mlp_dw1", outs=(BF16,), out_loc=G("mlp_w1", i))
        dhm = mm(dh1, W("mlp_w1", i), "nt", f"{tag}_mlp_dhm")
        dx, per_layer["norm_mlp"][i] = rmsnorm_bwd(x_c, small["norm_mlp"][i], dhm, dx, f"{tag}_norm_mlp_bwd")
        mm(xo, dx, "tn", f"{tag}_xa_dwo", outs=(BF16,), out_loc=G("xa_w_o", i))
        dxo = mm(dx, W("xa_w_o", i), "nt", f"{tag}_xa_do", outs=(BF16,))
        dxq, xdelta = flash_dq(_XA_CFG, xq, xkv, xkv, None, xo, dxo, xlse, BF16, f"{tag}_xa_attn_dq")
        dxk, dxv = flash_dkv(_XA_CFG, xq, xkv, xkv, None, dxo, xlse, xdelta, BF16, f"{tag}_xa_attn_dkv")
        dxkv = jnp.concatenate([dxk, dxv], axis=1)
        mm(hn, dxq, "tn", f"{tag}_xa_dwq", outs=(BF16,), out_loc=G("xa_w_q", i))
        dhn = mm(dxq, W("xa_w_q", i), "nt", f"{tag}_xa_dhn")
        mm(mem_n, dxkv, "tn", f"{tag}_xa_dwkv", outs=(BF16,), out_loc=G("xa_w_kv", i))
        dmem_n = mm(dxkv, W("xa_w_kv", i), "nt", f"{tag}_xa_dmem", epi=_epi_add, extras=(dmem_n,))
        dx, per_layer["norm_mem"][i] = rmsnorm_bwd(x_b, small["norm_mem"][i], dhn, dx, f"{tag}_norm_mem_bwd")
        if kind == 0:
            dh, gr = _mla_bwd(dx, h, mla_w[j], W("mla_w_o", j), G("mla_w_o", j), small["mla_q_norm"][j], small["mla_kv_norm"][j],
                              tabs, mix, f"{tag}_mla")
        elif kind == 1:
            dh, gr = _gdn_bwd(dx, h, gdn_in_x[j], small["gdn_conv_w"][j], small["gdn_o_norm"][j], W("gdn_w_o", j), G("gdn_w_o", j),
                              mix, f"{tag}_gdn")
        else:
            dh, gr = _sc_bwd(dx, h, W("sc_w_in", j), G("sc_w_in", j), small["sc_conv_w"][j], W("sc_w_o", j), G("sc_w_o", j),
                             mix, f"{tag}_sc")
        for n, g in gr.items():
            mixer.setdefault(n, {})[j] = g
        dx, per_layer["norm_mix"][i] = rmsnorm_bwd(x_a, small["norm_mix"][i], dh, dx, f"{tag}_norm_mix_bwd")

    _, d_mem_norm = rmsnorm_bwd(mem, small["mem_norm"], dmem_n, jnp.zeros(mem.shape, F32), "mem_norm_bwd")
    grads = {n: jnp.stack(v) for n, v in per_layer.items()}
    for n, by_j in mixer.items():
        grads[n] = jnp.stack([by_j[j] for j in sorted(by_j)])
    grads["mem_norm"] = d_mem_norm
    grads["final_norm"] = d_final
    for n in _RELAID:
        lay.put_full(gslabs, n, grads.pop(n))
    return se, dx, grads


def kernel(x, mem, positions, mla_w_in, mla_q_norm, mla_kv_norm, mla_w_uq, mla_w_ukv, mla_w_o, gdn_w_in, gdn_conv_w, gdn_a_log, gdn_dt_bias, gdn_o_norm, gdn_w_o, sc_w_in, sc_conv_w, sc_w_o, norm_mix, norm_mem, norm_mlp, xa_w_q, xa_w_kv, xa_w_o, mlp_w1, mlp_w2, mem_norm, final_norm, loss_target, m_mla_w_in, m_mla_q_norm, m_mla_kv_norm, m_mla_w_uq, m_mla_w_ukv, m_mla_w_o, m_gdn_w_in, m_gdn_conv_w, m_gdn_a_log, m_gdn_dt_bias, m_gdn_o_norm, m_gdn_w_o, m_sc_w_in, m_sc_conv_w, m_sc_w_o, m_norm_mix, m_norm_mem, m_norm_mlp, m_xa_w_q, m_xa_w_kv, m_xa_w_o, m_mlp_w1, m_mlp_w2, m_mem_norm, m_final_norm, v_mla_w_in, v_mla_q_norm, v_mla_kv_norm, v_mla_w_uq, v_mla_w_ukv, v_mla_w_o, v_gdn_w_in, v_gdn_conv_w, v_gdn_a_log, v_gdn_dt_bias, v_gdn_o_norm, v_gdn_w_o, v_sc_w_in, v_sc_conv_w, v_sc_w_o, v_norm_mix, v_norm_mem, v_norm_mlp, v_xa_w_q, v_xa_w_kv, v_xa_w_o, v_mlp_w1, v_mlp_w2, v_mem_norm, v_final_norm):
    given = dict(locals())
    p = {n: given[n] for n in _WEIGHTS}
    mom = {n: given["m_" + n] for n in _WEIGHTS}
    var = {n: given["v_" + n] for n in _WEIGHTS}
    split = [n for members in _SLABS.values() for n, _ in members]
    lay = Layout({n: p[n].shape for n in split})
    flat2d = lambda a: a.reshape(-1, a.shape[-1])

    me = (2 * lax.axis_index("x") + lax.axis_index("y")).astype(jnp.int32)
    core = lax.axis_index("c").astype(jnp.int32)
    me1, c1, mc = me.reshape(1), core.reshape(1), jnp.stack([me, core])

    wslabs = lay.new_slabs(BF16)
    for n in split:
        slab, off = lay.where[n][0], lay.where[n][1]
        cast_into(flat2d(p[n]), wslabs[slab], off, me1, f"cast_{n}")
    small_names = [n for n, _ in _SMALL]
    words = lax.bitcast_convert_type(jnp.concatenate([p[n].reshape(-1) for n in small_names]), BF16).reshape(-1)
    words = jnp.pad(words, (0, SMALL_ROWS * SMALL_COLS - words.shape[0])).reshape(1, SMALL_ROWS, SMALL_COLS)
    small_slab = lax.dynamic_update_slice(jnp.zeros((N_CHIPS, SMALL_ROWS, SMALL_COLS), BF16), words, (me, 0, 0))
    order = list(_SLABS)
    gathered = gather_slabs([wslabs[s].arr for s in order] + [small_slab])
    for s, arr in zip(order, gathered):
        wslabs[s].arr = arr
    small = {n: p[n] for n in _REPL}
    got, off = gathered[-1].reshape(N_CHIPS, -1), 0
    for n, ax in _SMALL:
        vals = lax.bitcast_convert_type(got[:, off:off + 2 * p[n].size].reshape(N_CHIPS, p[n].size, 2), F32)
        vals = vals.reshape((N_CHIPS,) + p[n].shape)
        small[n] = jnp.concatenate([vals[s] for s in range(N_CHIPS)], axis=ax)
        off += 2 * p[n].size

    gslabs = lay.new_slabs(BF16)
    se, dx, sgrads = local_step(x[0], mem[0], positions.reshape(-1, 1), loss_target[0], lay, wslabs, gslabs, small)
    loss = lax.psum(0.5 * jnp.sum(se) / x.shape[-1], ("x", "y", "c"))

    axes = dict(_SMALL)
    small_order = small_names + _REPL
    slots = []
    for s in range(N_CHIPS):
        vals = {n: (lax.slice_in_dim(g, s * p[n].shape[axes[n]], (s + 1) * p[n].shape[axes[n]], axis=axes[n]) if n in axes else g)
                for n, g in sgrads.items()}
        slots.append(_small_pack(vals, small_order))
    g_list = [gslabs[s].arr for s in order] + [jnp.stack(slots).astype(BF16)]

    swapped = pair_swap_halves(g_list)
    names = order + ["small"]
    partial = [pair_add(g, b, c1, f"pair_add_{s}") for g, b, s in zip(g_list, swapped, names)]
    received = chip_exchange(partial)
    halves = [chip_sum(q, r, mc, f"chip_sum_{s}") for q, r, s in zip(partial, received, names)]
    reduced = dict(zip(names, pair_join_halves(halves)))

    res = {}
    for n in split:
        slab, off = lay.where[n][0], lay.where[n][1]
        outs = adamw(reduced[slab], off, flat2d(p[n]), flat2d(mom[n]), flat2d(var[n]), f"adamw_{n}")
        res[n] = [o.reshape(p[n].shape) for o in outs]
    outs = adamw(reduced["small"], 0, _small_pack(p, small_order), _small_pack(mom, small_order), _small_pack(var, small_order),
                 "adamw_small")
    unpacked = [_small_unpack(o, p, small_order) for o in outs]
    for n in small_order:
        res[n] = [u[n] for u in unpacked]
    return (loss, dx[None], *[res[n][k] for k in range(4) for n in _WEIGHTS])
```

```python
import jax
import jax.numpy as jnp
from jax import lax
from jax.experimental import pallas as pl
from jax.experimental.pallas import tpu as pltpu

F32 = jnp.float32
BF16 = jnp.bfloat16
HI = lax.Precision.HIGHEST
MESH = pl.DeviceIdType.MESH

EPS = 1e-6
ROPE_THETA = 10000.0
N_CHIPS = 4
LANES = 128
VMEM_LIMIT = 56 * 1024 * 1024
NEG = -1e30

MLA_H, MLA_NOPE, MLA_ROPE, MLA_V = 8, 128, 64, 128
MLA_QR, MLA_KVR = 384, 256
MLA_ZPAD = 768
GDN_H, GDN_D, GDN_C = 8, 128, 64
XA_H, XA_D = 4, 256

ADAM_LR, ADAM_B1, ADAM_B2, ADAM_EPS, ADAM_WD, ADAM_STEP = 0.001, 0.9, 0.999, 1e-08, 0.01, 10

SMALL_ROWS, SMALL_COLS = 32, 1024


def _cparams(sem=None):
    return pltpu.CompilerParams(dimension_semantics=sem, vmem_limit_bytes=VMEM_LIMIT)


def _pick(dim, pref):
    t = (min(pref, dim) // LANES) * LANES
    while t >= LANES:
        if dim % t == 0:
            return t
        t -= LANES
    return dim


def _pick_rows(rows, pref=256):
    t = pref
    while rows % t:
        t //= 2
    return t


class Slab:
    def __init__(self, rows, width, dtype, arr=None):
        self.shape, self.dtype, self.arr = (N_CHIPS, rows, width), dtype, arr


class Loc:
    def __init__(self, slab, row0, K, N, axis):
        self.slab, self.row0, self.K, self.N, self.axis = slab, row0, K, N, axis
        self.Ks = K // N_CHIPS if axis == 0 else K
        self.Ns = N // N_CHIPS if axis == 1 else N

    def tile_spec(self, tr, tc, rc):
        assert self.row0 % tr == 0 and self.Ks % tr == 0 and self.Ns % tc == 0, (self.row0, self.Ks, self.Ns, tr, tc)
        r0, rb, cb = self.row0 // tr, self.Ks // tr, self.Ns // tc
        if self.axis == 0:
            return pl.BlockSpec((None, tr, tc), lambda i, j: (rc(i, j)[0] // rb, r0 + rc(i, j)[0] % rb, rc(i, j)[1]))
        return pl.BlockSpec((None, tr, tc), lambda i, j: (rc(i, j)[1] // cb, r0 + rc(i, j)[0], rc(i, j)[1] % cb))

    def slot_spec(self, slot, tr, tc, rc):
        assert self.row0 % tr == 0, (self.row0, tr)
        r0 = self.row0 // tr
        return pl.BlockSpec((None, tr, tc), lambda i, j: (slot, r0 + rc(i, j)[0], rc(i, j)[1]))


_DIMS = {"nn": ((1,), (0,)), "nt": ((1,), (1,)), "tn": ((0,), (0,))}
_ANY = pl.BlockSpec(memory_space=pl.ANY)


def mm(a, b, mode, name, outs=(F32,), epi=None, extras=(), tm=512, tn=1024, out_loc=None):
    b_loc = b if isinstance(b, Loc) else None
    if mode == "nn":
        M, K = a.shape
        K2, N = (b_loc.K, b_loc.N) if b_loc else b.shape
    elif mode == "nt":
        M, K = a.shape
        N, K2 = (b_loc.K, b_loc.N) if b_loc else b.shape
    else:
        K, M = a.shape
        K2, N = b.shape
    assert K == K2, (name, a.shape, K2, N)
    tm = _pick(out_loc.Ks if (out_loc and out_loc.axis == 0) else M, tm)
    if out_loc is not None and out_loc.axis == 1:
        tn = _pick(out_loc.Ns, tn)
    elif b_loc is not None and ((mode == "nn" and b_loc.axis == 1) or (mode == "nt" and b_loc.axis == 0)):
        tn = _pick(b_loc.Ns if mode == "nn" else b_loc.Ks, tn)
    else:
        tn = _pick(N, tn)

    parts = 1
    if mode == "tn":
        a_spec = pl.BlockSpec((K, tm), lambda i, j: (0, i))
        b_specs, b_args = [pl.BlockSpec((K, tn), lambda i, j: (0, j))], [b]
    else:
        a_spec = pl.BlockSpec((tm, K), lambda i, j: (i, 0))
        if b_loc is None:
            b_specs = [pl.BlockSpec((K, tn), lambda i, j: (0, j)) if mode == "nn" else pl.BlockSpec((tn, K), lambda i, j: (j, 0))]
            b_args = [b]
        elif mode == "nn" and b_loc.axis == 1:
            b_specs, b_args = [b_loc.tile_spec(K, tn, lambda i, j: (0, j))], [b_loc.slab.arr]
        elif mode == "nt" and b_loc.axis == 0:
            b_specs, b_args = [b_loc.tile_spec(tn, K, lambda i, j: (j, 0))], [b_loc.slab.arr]
        elif mode == "nn":
            parts = N_CHIPS
            b_specs = [b_loc.slot_spec(s, b_loc.Ks, tn, lambda i, j: (0, j)) for s in range(parts)]
            b_args = [b_loc.slab.arr] * parts
        else:
            parts = N_CHIPS
            b_specs = [b_loc.slot_spec(s, tn, b_loc.Ns, lambda i, j: (j, 0)) for s in range(parts)]
            b_args = [b_loc.slab.arr] * parts
    kp = K // parts
    n_ex, n_out = len(extras), len(outs)
    dims = (_DIMS[mode], ((), ()))

    def body(*refs):
        a_ref = refs[0]
        b_refs = refs[1:1 + parts]
        ex_refs = refs[1 + parts:1 + parts + n_ex]
        o_refs = refs[-n_out:]
        acc = None
        for s in range(parts):
            av = a_ref[...] if parts == 1 else a_ref[:, s * kp:(s + 1) * kp]
            d = lax.dot_general(av.astype(BF16), b_refs[s][...].astype(BF16), dims, preferred_element_type=F32)
            acc = d if acc is None else acc + d
        res = epi(acc, *[e[...] for e in ex_refs]) if epi is not None else (acc,)
        for o_ref, v in zip(o_refs, res):
            o_ref[...] = v.astype(o_ref.dtype)

    mn_spec = pl.BlockSpec((tm, tn), lambda i, j: (i, j))
    in_specs = [a_spec] + b_specs + [mn_spec] * n_ex
    args = [a] + b_args + list(extras)
    aliases = {}
    if out_loc is None:
        out_specs = [mn_spec] * n_out
        out_shape = [jax.ShapeDtypeStruct((M, N), d) for d in outs]
    else:
        assert n_out == 1 and mode == "tn"
        out_specs = [out_loc.tile_spec(tm, tn, lambda i, j: (i, j))]
        out_shape = [jax.ShapeDtypeStruct(out_loc.slab.shape, out_loc.slab.dtype)]
        if out_loc.slab.arr is not None:
            in_specs.append(_ANY)
            args.append(out_loc.slab.arr)
            aliases = {len(args) - 1: 0}

    res = pl.pallas_call(
        body, name=name, grid=(M // tm, N // tn), in_specs=in_specs, out_specs=out_specs, out_shape=out_shape,
        input_output_aliases=aliases, compiler_params=_cparams(("parallel", "parallel")),
    )(*args)
    if out_loc is not None:
        out_loc.slab.arr = res[0]
        return None
    return res[0] if n_out == 1 else tuple(res)


def _epi_add(acc, r):
    return (acc + r,)


def _epi_relu2(acc):
    r = jnp.maximum(acc, 0.0)
    return acc, r * r


def _epi_relu2_bwd(acc, h1):
    return (acc * (2.0 * jnp.maximum(h1.astype(F32), 0.0)),)


def _rms(x, g):
    return x * lax.rsqrt(jnp.mean(x * x, axis=-1, keepdims=True) + EPS) * g


def _row_spec(ts, cols):
    return pl.BlockSpec((ts, cols), lambda i: (i, 0))


def _par_spec(cols):
    return pl.BlockSpec((1, cols), lambda i: (0, 0))


def rmsnorm_fwd(x, g, name, ts=256):
    T, D = x.shape
    ts = min(ts, T)

    def body(x_ref, g_ref, o_ref):
        o_ref[...] = _rms(x_ref[...], g_ref[...]).astype(o_ref.dtype)

    return pl.pallas_call(
        body, name=name, grid=(T // ts,),
        in_specs=[_row_spec(ts, D), _par_spec(D)], out_specs=_row_spec(ts, D),
        out_shape=jax.ShapeDtypeStruct((T, D), BF16), compiler_params=_cparams(("parallel",)),
    )(x, g.reshape(1, D))


def rmsnorm_bwd(x, g, dy, dx_in, name, ts=256):
    T, D = x.shape
    ts = min(ts, T)

    def body(x_ref, g_ref, dy_ref, dxi_ref, dx_ref, dg_ref):
        xv = x_ref[...]
        r = lax.rsqrt(jnp.mean(xv * xv, axis=-1, keepdims=True) + EPS)
        xh = xv * r
        dyv = dy_ref[...].astype(F32)
        dxh = dyv * g_ref[...]
        dx_ref[...] = dxi_ref[...] + r * (dxh - xh * jnp.mean(dxh * xh, axis=-1, keepdims=True))
        dg = jnp.sum(dyv * xh, axis=0, keepdims=True)

        @pl.when(pl.program_id(0) == 0)
        def _():
            dg_ref[...] = jnp.zeros_like(dg_ref)

        dg_ref[...] += dg

    dx, dg = pl.pallas_call(
        body, name=name, grid=(T // ts,),
        in_specs=[_row_spec(ts, D), _par_spec(D), _row_spec(ts, D), _row_spec(ts, D)],
        out_specs=[_row_spec(ts, D), _par_spec(D)],
        out_shape=[jax.ShapeDtypeStruct((T, D), F32), jax.ShapeDtypeStruct((1, D), F32)],
        compiler_params=_cparams(("arbitrary",)),
    )(x, g.reshape(1, D), dy, dx_in)
    return dx, dg.reshape(D)


def rope_tables(pos, name="rope_tables"):
    T = pos.shape[0]
    half = MLA_ROPE // 2
    inv = ROPE_THETA ** (-jnp.arange(0, MLA_ROPE, 2, dtype=F32) / MLA_ROPE)
    inv_row = jnp.concatenate([inv, inv, jnp.zeros((LANES - MLA_ROPE,), F32)]).reshape(1, LANES)

    def body(p_ref, f_ref, c_ref, a_ref, b_ref):
        ang = p_ref[...].astype(F32) * f_ref[...]
        lane = lax.broadcasted_iota(jnp.int32, ang.shape, 1)
        c, s = jnp.cos(ang), jnp.sin(ang)
        c_ref[...] = jnp.where(lane < MLA_ROPE, c, 0.0)
        a_ref[...] = jnp.where(lane < half, -s, 0.0)
        b_ref[...] = jnp.where((lane >= half) & (lane < MLA_ROPE), s, 0.0)

    sh = jax.ShapeDtypeStruct((T, LANES), F32)
    return pl.pallas_call(body, name=name, out_shape=[sh, sh, sh], compiler_params=_cparams())(pos, inv_row)


def _roll_l(x):
    return pltpu.roll(x, LANES - MLA_ROPE // 2, 1)


def _roll_r(x):
    return pltpu.roll(x, MLA_ROPE // 2, 1)


def _rope(r, c, sa, sb):
    return r * c + _roll_l(r) * sa + _roll_r(r) * sb


def _rope_t(d, c, sa, sb):
    return d * c + _roll_r(d * sa) + _roll_l(d * sb)


def mla_mid_fwd(z, qn, kvn, tabs, name, ts=256):
    T = z.shape[0]
    ts = min(ts, T)
    a0, a1 = MLA_QR, MLA_QR + MLA_KVR

    def body(z_ref, qn_ref, kvn_ref, c_ref, sa_ref, sb_ref, cq_ref, ckv_ref, kr_ref):
        cq_ref[...] = _rms(z_ref[:, 0:a0], qn_ref[...]).astype(BF16)
        ckv_ref[...] = _rms(z_ref[:, a0:a1], kvn_ref[...]).astype(BF16)
        kr_ref[...] = _rope(z_ref[:, a1:MLA_ZPAD], c_ref[...], sa_ref[...], sb_ref[...]).astype(BF16)

    return pl.pallas_call(
        body, name=name, grid=(T // ts,),
        in_specs=[_row_spec(ts, MLA_ZPAD), _par_spec(MLA_QR), _par_spec(MLA_KVR)] + [_row_spec(ts, LANES)] * 3,
        out_specs=[_row_spec(ts, MLA_QR), _row_spec(ts, MLA_KVR), _row_spec(ts, LANES)],
        out_shape=[jax.ShapeDtypeStruct((T, MLA_QR), BF16), jax.ShapeDtypeStruct((T, MLA_KVR), BF16),
                   jax.ShapeDtypeStruct((T, LANES), BF16)],
        compiler_params=_cparams(("parallel",)),
    )(z, qn.reshape(1, -1), kvn.reshape(1, -1), *tabs)


def mla_mid_bwd(z, qn, kvn, tabs, dcq, dckv, dkr, name, ts=256):
    T = z.shape[0]
    ts = min(ts, T)
    a0, a1 = MLA_QR, MLA_QR + MLA_KVR

    def body(z_ref, qn_ref, kvn_ref, c_ref, sa_ref, sb_ref, dcq_ref, dckv_ref, dkr_ref, dz_ref, dqn_ref, dkvn_ref):
        _, vq = jax.vjp(_rms, z_ref[:, 0:a0], qn_ref[...])
        dzq, dqn = vq(dcq_ref[...].astype(F32))
        _, vk = jax.vjp(_rms, z_ref[:, a0:a1], kvn_ref[...])
        dzk, dkvn = vk(dckv_ref[...].astype(F32))
        dz_ref[:, 0:a0] = dzq.astype(dz_ref.dtype)
        dz_ref[:, a0:a1] = dzk.astype(dz_ref.dtype)
        dz_ref[:, a1:MLA_ZPAD] = _rope_t(dkr_ref[...].astype(F32), c_ref[...], sa_ref[...], sb_ref[...]).astype(dz_ref.dtype)

        @pl.when(pl.program_id(0) == 0)
        def _():
            dqn_ref[...] = jnp.zeros_like(dqn_ref)
            dkvn_ref[...] = jnp.zeros_like(dkvn_ref)

        dqn_ref[...] += dqn
        dkvn_ref[...] += dkvn

    dz, dqn, dkvn = pl.pallas_call(
        body, name=name, grid=(T // ts,),
        in_specs=[_row_spec(ts, MLA_ZPAD), _par_spec(MLA_QR), _par_spec(MLA_KVR)] + [_row_spec(ts, LANES)] * 3
        + [_row_spec(ts, MLA_QR), _row_spec(ts, MLA_KVR), _row_spec(ts, LANES)],
        out_specs=[_row_spec(ts, MLA_ZPAD), _par_spec(MLA_QR), _par_spec(MLA_KVR)],
        out_shape=[jax.ShapeDtypeStruct((T, MLA_ZPAD), BF16), jax.ShapeDtypeStruct((1, MLA_QR), F32),
                   jax.ShapeDtypeStruct((1, MLA_KVR), F32)],
        compiler_params=_cparams(("arbitrary",)),
    )(z, qn.reshape(1, -1), kvn.reshape(1, -1), *tabs, dcq, dckv, dkr)
    return dz, dqn.reshape(-1), dkvn.reshape(-1)


def rope_q(q, tabs, transpose, name, ts=256):
    T, W = q.shape
    ts = min(ts, T)
    fn = _rope_t if transpose else _rope
    hw = 2 * LANES

    def body(q_ref, c_ref, sa_ref, sb_ref, o_ref):
        c, sa, sb = c_ref[...], sa_ref[...], sb_ref[...]
        for h in range(W // hw):
            o_ref[:, h * hw:h * hw + LANES] = q_ref[:, h * hw:h * hw + LANES].astype(o_ref.dtype)
            o_ref[:, h * hw + LANES:(h + 1) * hw] = fn(q_ref[:, h * hw + LANES:(h + 1) * hw].astype(F32), c, sa, sb).astype(o_ref.dtype)

    return pl.pallas_call(
        body, name=name, grid=(T // ts,),
        in_specs=[_row_spec(ts, W)] + [_row_spec(ts, LANES)] * 3, out_specs=_row_spec(ts, W),
        out_shape=jax.ShapeDtypeStruct((T, W), BF16), compiler_params=_cparams(("parallel",)),
    )(q, *tabs)


def loss_head(x, g, target, name="loss_head", ts=256):
    T, D = x.shape
    ts = min(ts, T)

    def body(x_ref, g_ref, t_ref, se_ref, dx_ref, dg_ref):
        xv = x_ref[...]
        r = lax.rsqrt(jnp.mean(xv * xv, axis=-1, keepdims=True) + EPS)
        xh = xv * r
        err = xh * g_ref[...] - t_ref[...]
        dy = err * (1.0 / D)
        dxh = dy * g_ref[...]
        dx_ref[...] = r * (dxh - xh * jnp.mean(dxh * xh, axis=-1, keepdims=True))

        @pl.when(pl.program_id(0) == 0)
        def _():
            se_ref[...] = jnp.zeros_like(se_ref)
            dg_ref[...] = jnp.zeros_like(dg_ref)

        se_ref[...] += jnp.sum(err * err, axis=0, keepdims=True)
        dg_ref[...] += jnp.sum(dy * xh, axis=0, keepdims=True)

    se, dx, dg = pl.pallas_call(
        body, name=name, grid=(T // ts,),
        in_specs=[_row_spec(ts, D), _par_spec(D), _row_spec(ts, D)],
        out_specs=[_par_spec(D), _row_spec(ts, D), _par_spec(D)],
        out_shape=[jax.ShapeDtypeStruct((1, D), F32), jax.ShapeDtypeStruct((T, D), F32), jax.ShapeDtypeStruct((1, D), F32)],
        compiler_params=_cparams(("arbitrary",)),
    )(x, g.reshape(1, D), target)
    return se, dx, dg.reshape(D)


def _dot_nt(a, b):
    return lax.dot_general(a, b, (((1,), (1,)), ((), ())), preferred_element_type=F32)


def _dot_tn(a, b):
    return lax.dot_general(a, b, (((0,), (0,)), ((), ())), preferred_element_type=F32)


def _dot_nn(a, b):
    return lax.dot_general(a, b, (((1,), (0,)), ((), ())), preferred_element_type=F32)


class _Attn:
    def __init__(self, H, dq, dk1, dv, k1_col, v_col, causal, scale, blk=256):
        self.H, self.dq, self.dk1, self.dv = H, dq, dk1, dv
        self.k1_col, self.v_col, self.causal, self.scale, self.blk = k1_col, v_col, causal, scale, blk


def _keys(k1_ref, k2_ref, rows):
    ks = k1_ref[rows, :]
    if k2_ref is not None:
        ks = jnp.concatenate([ks, k2_ref[rows, :]], axis=1)
    return ks


def _mask(s, cfg, i, j, t):
    if not cfg.causal:
        return s
    row = i * t + lax.broadcasted_iota(jnp.int32, s.shape, 0)
    col = j * t + lax.broadcasted_iota(jnp.int32, s.shape, 1)
    return jnp.where(row >= col, s, NEG)


def flash_fwd(cfg, q, k1, v, k2, name):
    Tq, Tk = q.shape[0], k1.shape[0]
    t = min(cfg.blk, Tq, Tk)
    nkb = Tk // t
    has_k2 = k2 is not None

    def body(*refs):
        q_ref, k1_ref, v_ref = refs[:3]
        k2_ref = refs[3] if has_k2 else None
        o_ref, lse_ref = refs[-2], refs[-1]
        i = pl.program_id(1)
        qv = q_ref[...]

        def step(j, carry):
            m, l, acc = carry
            rows = pl.ds(pl.multiple_of(j * t, t), t)
            s = _mask(_dot_nt(qv, _keys(k1_ref, k2_ref, rows)) * cfg.scale, cfg, i, j, t)
            m2 = jnp.maximum(m, jnp.max(s, axis=-1, keepdims=True))
            p = jnp.exp(s - m2)
            alpha = jnp.exp(m - m2)
            l2 = alpha * l + jnp.sum(p, axis=-1, keepdims=True)
            acc2 = alpha * acc + _dot_nn(p.astype(BF16), v_ref[rows, :])
            return m2, l2, acc2

        init = (jnp.full((t, 1), NEG, F32), jnp.zeros((t, 1), F32), jnp.zeros((t, cfg.dv), F32))
        m, l, acc = lax.fori_loop(0, (i + 1) if cfg.causal else nkb, step, init)
        o_ref[...] = (acc / l).astype(o_ref.dtype)
        lse_ref[...] = m + jnp.log(l)

    in_specs = [pl.BlockSpec((t, cfg.dq), lambda h, i: (i, h)),
                pl.BlockSpec((Tk, cfg.dk1), lambda h, i: (0, cfg.k1_col + h)),
                pl.BlockSpec((Tk, cfg.dv), lambda h, i: (0, cfg.v_col + h))]
    args = [q, k1, v]
    if has_k2:
        in_specs.append(pl.BlockSpec((Tk, LANES), lambda h, i: (0, 0)))
        args.append(k2)
    return pl.pallas_call(
        body, name=name, grid=(cfg.H, Tq // t), in_specs=in_specs,
        out_specs=[pl.BlockSpec((t, cfg.dv), lambda h, i: (i, h)), pl.BlockSpec((None, t, 1), lambda h, i: (h, i, 0))],
        out_shape=[jax.ShapeDtypeStruct((Tq, cfg.H * cfg.dv), BF16), jax.ShapeDtypeStruct((cfg.H, Tq, 1), F32)],
        compiler_params=_cparams(("parallel", "parallel")),
    )(*args)


def flash_dq(cfg, q, k1, v, k2, o, do, lse, out_dtype, name):
    Tq, Tk = q.shape[0], k1.shape[0]
    t = min(cfg.blk, Tq, Tk)
    nkb = Tk // t
    has_k2 = k2 is not None

    def body(*refs):
        q_ref, k1_ref, v_ref = refs[:3]
        k2_ref = refs[3] if has_k2 else None
        o_ref, do_ref, lse_ref, dq_ref, dl_ref = refs[-5:]
        i = pl.program_id(1)
        qv, dov, lsev = q_ref[...], do_ref[...], lse_ref[...]
        delta = jnp.sum(dov.astype(F32) * o_ref[...].astype(F32), axis=-1, keepdims=True)
        dl_ref[...] = delta

        def step(j, dq):
            rows = pl.ds(pl.multiple_of(j * t, t), t)
            ks = _keys(k1_ref, k2_ref, rows)
            s = _mask(_dot_nt(qv, ks) * cfg.scale, cfg, i, j, t)
            p = jnp.exp(s - lsev)
            dp = _dot_nt(dov, v_ref[rows, :])
            ds = p * (dp - delta) * cfg.scale
            return dq + _dot_nn(ds.astype(BF16), ks)

        dq = lax.fori_loop(0, (i + 1) if cfg.causal else nkb, step, jnp.zeros((t, cfg.dq), F32))
        dq_ref[...] = dq.astype(dq_ref.dtype)

    in_specs = [pl.BlockSpec((t, cfg.dq), lambda h, i: (i, h)),
                pl.BlockSpec((Tk, cfg.dk1), lambda h, i: (0, cfg.k1_col + h)),
                pl.BlockSpec((Tk, cfg.dv), lambda h, i: (0, cfg.v_col + h))]
    args = [q, k1, v]
    if has_k2:
        in_specs.append(pl.BlockSpec((Tk, LANES), lambda h, i: (0, 0)))
        args.append(k2)
    in_specs += [pl.BlockSpec((t, cfg.dv), lambda h, i: (i, h)), pl.BlockSpec((t, cfg.dv), lambda h, i: (i, h)),
                 pl.BlockSpec((None, t, 1), lambda h, i: (h, i, 0))]
    args += [o, do, lse]
    return pl.pallas_call(
        body, name=name, grid=(cfg.H, Tq // t), in_specs=in_specs,
        out_specs=[pl.BlockSpec((t, cfg.dq), lambda h, i: (i, h)), pl.BlockSpec((None, t, 1), lambda h, i: (h, i, 0))],
        out_shape=[jax.ShapeDtypeStruct((Tq, cfg.H * cfg.dq), out_dtype), jax.ShapeDtypeStruct((cfg.H, Tq, 1), F32)],
        compiler_params=_cparams(("parallel", "parallel")),
    )(*args)


def flash_dkv(cfg, q, k1, v, k2, do, lse, delta, out_dtype, name):
    Tq, Tk = q.shape[0], k1.shape[0]
    t = min(cfg.blk, Tq, Tk)
    nqb = Tq // t
    has_k2 = k2 is not None

    def body(*refs):
        q_ref, k1_ref, v_ref = refs[:3]
        k2_ref = refs[3] if has_k2 else None
        n_in = 4 if has_k2 else 3
        do_ref, lse_ref, dl_ref = refs[n_in:n_in + 3]
        dk1_ref, dv_ref = refs[n_in + 3], refs[n_in + 4]
        j, h = pl.program_id(0), pl.program_id(1)
        ks = _keys(k1_ref, k2_ref, slice(None))
        vs = v_ref[...]

        def step(i, carry):
            dk, dv = carry
            rows = pl.ds(pl.multiple_of(i * t, t), t)
            qi, doi = q_ref[rows, :], do_ref[rows, :]
            s = _mask(_dot_nt(qi, ks) * cfg.scale, cfg, i, j, t)
            p = jnp.exp(s - lse_ref[rows, :])
            dv = dv + _dot_tn(p.astype(BF16), doi)
            ds = p * (_dot_nt(doi, vs) - dl_ref[rows, :]) * cfg.scale
            dk = dk + _dot_tn(ds.astype(BF16), qi)
            return dk, dv

        init = (jnp.zeros((t, cfg.dq), F32), jnp.zeros((t, cfg.dv), F32))
        dk, dv = lax.fori_loop(j if cfg.causal else 0, nqb, step, init)
        dv_ref[...] = dv.astype(dv_ref.dtype)
        dk1_ref[...] = dk[:, 0:cfg.dk1].astype(dk1_ref.dtype)
        if has_k2:
            dk2_ref = refs[n_in + 5]

            @pl.when(h == 0)
            def _():
                dk2_ref[...] = jnp.zeros_like(dk2_ref)

            dk2_ref[...] += dk[:, cfg.dk1:]

    in_specs = [pl.BlockSpec((Tq, cfg.dq), lambda j, h: (0, h)),
                pl.BlockSpec((t, cfg.dk1), lambda j, h: (j, cfg.k1_col + h)),
                pl.BlockSpec((t, cfg.dv), lambda j, h: (j, cfg.v_col + h))]
    args = [q, k1, v]
    if has_k2:
        in_specs.append(pl.BlockSpec((t, LANES), lambda j, h: (j, 0)))
        args.append(k2)
    in_specs += [pl.BlockSpec((Tq, cfg.dv), lambda j, h: (0, h)), pl.BlockSpec((None, Tq, 1), lambda j, h: (h, 0, 0)),
                 pl.BlockSpec((None, Tq, 1), lambda j, h: (h, 0, 0))]
    args += [do, lse, delta]
    out_specs = [pl.BlockSpec((t, cfg.dk1), lambda j, h: (j, h)), pl.BlockSpec((t, cfg.dv), lambda j, h: (j, h))]
    out_shape = [jax.ShapeDtypeStruct((Tk, cfg.H * cfg.dk1), out_dtype), jax.ShapeDtypeStruct((Tk, cfg.H * cfg.dv), out_dtype)]
    if has_k2:
        out_specs.append(pl.BlockSpec((t, LANES), lambda j, h: (j, 0)))
        out_shape.append(jax.ShapeDtypeStruct((Tk, LANES), F32))
    return pl.pallas_call(
        body, name=name, grid=(Tk // t, cfg.H), in_specs=in_specs, out_specs=out_specs, out_shape=out_shape,
        compiler_params=_cparams(("parallel", "arbitrary")),
    )(*args)


def _shift_down(x, s):
    if s == 0:
        return x
    t = lax.broadcasted_iota(jnp.int32, x.shape, 0)
    return jnp.where(t >= s, pltpu.roll(x, s, 0), 0.0)


def _shift_up(x, s):
    if s == 0:
        return x
    n = x.shape[0]
    t = lax.broadcasted_iota(jnp.int32, x.shape, 0)
    return jnp.where(t < n - s, pltpu.roll(x, n - s, 0), 0.0)


def _conv(x, w_ref, kw):
    y = x * w_ref[kw - 1:kw, :]
    for j in range(kw - 1):
        y = y + _shift_down(x, kw - 1 - j) * w_ref[j:j + 1, :]
    return y


def _conv_t(d, w_ref, kw):
    y = d * w_ref[kw - 1:kw, :]
    for j in range(kw - 1):
        y = y + _shift_up(d, kw - 1 - j) * w_ref[j:j + 1, :]
    return y


def _conv_dw(d, x, kw):
    rows = lax.broadcasted_iota(jnp.int32, (kw, d.shape[1]), 0)
    dw = jnp.zeros((kw, d.shape[1]), F32)
    for j in range(kw):
        r = jnp.sum(d * _shift_down(x, kw - 1 - j), axis=0, keepdims=True)
        dw = jnp.where(rows == j, r, dw)
    return dw


def _silu(x):
    return x * jax.nn.sigmoid(x)


def _silu_grad(x):
    s = jax.nn.sigmoid(x)
    return s * (1.0 + x * (1.0 - s))


def gdn_conv_fwd(z, w, name, tc=256):
    T, C = z.shape[0], w.shape[1]
    kw = w.shape[0]

    def body(x_ref, w_ref, o_ref):
        o_ref[...] = _silu(_conv(x_ref[...], w_ref, kw))

    return pl.pallas_call(
        body, name=name, grid=(C // tc,),
        in_specs=[pl.BlockSpec((T, tc), lambda j: (0, j)), pl.BlockSpec((kw, tc), lambda j: (0, j))],
        out_specs=pl.BlockSpec((T, tc), lambda j: (0, j)),
        out_shape=jax.ShapeDtypeStruct((T, C), F32), compiler_params=_cparams(("parallel",)),
    )(z, w)


def gdn_conv_bwd(z, w, dy, name, tc=256):
    T, C = z.shape[0], w.shape[1]
    kw = w.shape[0]

    def body(x_ref, w_ref, dy_ref, dx_ref, dw_ref):
        xv = x_ref[...]
        dc = dy_ref[...] * _silu_grad(_conv(xv, w_ref, kw))
        dx_ref[...] = _conv_t(dc, w_ref, kw).astype(dx_ref.dtype)
        dw_ref[...] = _conv_dw(dc, xv, kw)

    col = lambda j: (0, j)
    return pl.pallas_call(
        body, name=name, grid=(C // tc,),
        in_specs=[pl.BlockSpec((T, tc), col), pl.BlockSpec((kw, tc), col), pl.BlockSpec((T, tc), col)],
        out_specs=[pl.BlockSpec((T, tc), col), pl.BlockSpec((kw, tc), col)],
        out_shape=[jax.ShapeDtypeStruct((T, C), BF16), jax.ShapeDtypeStruct((kw, C), F32)],
        compiler_params=_cparams(("parallel",)),
    )(z, w, dy)


def sc_fwd(z, w, name, tc=256):
    T, C = z.shape[0], w.shape[1]
    kw, nb = w.shape[0], C // tc

    def body(b_ref, c_ref, u_ref, w_ref, o_ref):
        o_ref[...] = (b_ref[...] * _conv(c_ref[...] * u_ref[...], w_ref, kw)).astype(o_ref.dtype)

    return pl.pallas_call(
        body, name=name, grid=(nb,),
        in_specs=[pl.BlockSpec((T, tc), lambda j: (0, j)), pl.BlockSpec((T, tc), lambda j: (0, nb + j)),
                  pl.BlockSpec((T, tc), lambda j: (0, 2 * nb + j)), pl.BlockSpec((kw, tc), lambda j: (0, j))],
        out_specs=pl.BlockSpec((T, tc), lambda j: (0, j)),
        out_shape=jax.ShapeDtypeStruct((T, C), BF16), compiler_params=_cparams(("parallel",)),
    )(z, z, z, w)


def sc_bwd(z, w, dy, name, tc=256):
    T, C = z.shape[0], w.shape[1]
    kw, nb = w.shape[0], C // tc

    def body(b_ref, c_ref, u_ref, w_ref, dy_ref, db_ref, dc_ref, du_ref, dw_ref):
        cv, uv, dyv = c_ref[...], u_ref[...], dy_ref[...]
        cu = cv * uv
        db_ref[...] = (dyv * _conv(cu, w_ref, kw)).astype(db_ref.dtype)
        dcv = dyv * b_ref[...]
        dcu = _conv_t(dcv, w_ref, kw)
        dc_ref[...] = (dcu * uv).astype(dc_ref.dtype)
        du_ref[...] = (dcu * cv).astype(du_ref.dtype)
        dw_ref[...] = _conv_dw(dcv, cu, kw)

    col = lambda j: (0, j)
    act = jax.ShapeDtypeStruct((T, C), BF16)
    return pl.pallas_call(
        body, name=name, grid=(nb,),
        in_specs=[pl.BlockSpec((T, tc), col), pl.BlockSpec((T, tc), lambda j: (0, nb + j)),
                  pl.BlockSpec((T, tc), lambda j: (0, 2 * nb + j)), pl.BlockSpec((kw, tc), col), pl.BlockSpec((T, tc), col)],
        out_specs=[pl.BlockSpec((T, tc), col)] * 3 + [pl.BlockSpec((kw, tc), col)],
        out_shape=[act, act, act, jax.ShapeDtypeStruct((kw, C), F32)],
        compiler_params=_cparams(("parallel",)),
    )(z, z, z, w, dy)


def _hdot(a, b, dims):
    return lax.dot_general(a, b, (dims, ((), ())), precision=HI, preferred_element_type=F32)


def _bdot(a, b, dims):
    return lax.dot_general(a.astype(BF16), b.astype(BF16), (dims, ((), ())), preferred_element_type=F32)


_NN, _NT, _TN = ((1,), (0,)), ((1,), (1,)), ((0,), (0,))


def _per_head_dots(dot2d):
    def stacked(a, b, dims):
        return jnp.stack([dot2d(a[h], b[h], dims) for h in range(a.shape[0])])

    @jax.custom_vjp
    def nn(a, b):
        return stacked(a, b, _NN)

    @jax.custom_vjp
    def nt(a, b):
        return stacked(a, b, _NT)

    @jax.custom_vjp
    def tn(a, b):
        return stacked(a, b, _TN)

    nn.defvjp(lambda a, b: (nn(a, b), (a, b)), lambda r, d: (stacked(d, r[1], _NT), stacked(r[0], d, _TN)))
    nt.defvjp(lambda a, b: (nt(a, b), (a, b)), lambda r, d: (stacked(d, r[1], _NN), stacked(d, r[0], _TN)))
    tn.defvjp(lambda a, b: (tn(a, b), (a, b)), lambda r, d: (stacked(r[1], d, _NT), stacked(r[0], d, _NN)))
    return nn, nt, tn


_hnn, _hnt, _htn = _per_head_dots(_hdot)
_bnn, _bnt, _btn = _per_head_dots(_bdot)


@jax.custom_vjp
def _unit_lower_inverse(m):
    c = m.shape[-1]
    eye = (lax.broadcasted_iota(jnp.int32, (c, c), 0) == lax.broadcasted_iota(jnp.int32, (c, c), 1)).astype(F32)
    t = eye - m
    p = _hnn(m, m)
    n = 2
    while n < c:
        t = t + _hnn(t, p)
        n *= 2
        if n < c:
            p = _hnn(p, p)
    return t


def _uli_fwd(m):
    t = _unit_lower_inverse(m)
    return t, t


def _uli_bwd(t, dt):
    return (-_htn(t, _hnt(dt, t)),)


_unit_lower_inverse.defvjp(_uli_fwd, _uli_bwd)


def _gdn_chunk(q, k, v, gate, bl, al, a_log, dt_bias, o_norm, st):
    nh, c = q.shape[0], q.shape[1]
    ii = lax.broadcasted_iota(jnp.int32, (c, c), 0)
    jj = lax.broadcasted_iota(jnp.int32, (c, c), 1)
    tri, strict = ii >= jj, ii > jj
    q = q * lax.rsqrt(jnp.sum(q * q, -1, keepdims=True) + EPS) * (GDN_D ** -0.5)
    k = k * lax.rsqrt(jnp.sum(k * k, -1, keepdims=True) + EPS)
    beta = jax.nn.sigmoid(bl)
    g = -jnp.exp(a_log) * jax.nn.softplus(al + dt_bias)
    gc = _hnn(jnp.broadcast_to(tri.astype(F32), (nh, c, c)), g)
    gcol = _hnn(gc, jnp.full((nh, LANES, c), 1.0 / LANES, F32))
    grow = _hnt(jnp.full((nh, c, LANES), 1.0 / LANES, F32), gc)
    decay = jnp.where(tri, jnp.exp(jnp.where(tri, gcol - grow, 0.0)), 0.0)
    kb = k * beta
    m = jnp.where(strict, _bnt(kb, k) * decay, 0.0)
    t_inv = _unit_lower_inverse(m)
    eg = jnp.exp(gc)
    u = _bnn(t_inv, v * beta)
    w = _bnn(t_inv, kb * eg)
    attn = _bnt(q, k) * decay
    v_new = u - _bnn(w, st)
    o = _bnn(q * eg, st) + _bnn(attn, v_new)
    g_last = jnp.sum(g, axis=1, keepdims=True)
    st_new = st * jnp.exp(g_last) + _btn(k * jnp.exp(g_last - gc), v_new)
    o = o * lax.rsqrt(jnp.mean(o * o, -1, keepdims=True) + EPS) * o_norm
    return o * _silu(gate), st_new


GDN_HP = 8
_GW = GDN_HP * GDN_D
_GB = GDN_H // GDN_HP


def _gdn_specs(n_chunks, rev):
    def tok(col):
        if rev:
            return pl.BlockSpec((GDN_C, _GW), lambda h, n: (n_chunks - 1 - n, col + h))
        return pl.BlockSpec((GDN_C, _GW), lambda h, n: (n, col + h))
    par = pl.BlockSpec((1, _GW), lambda h, n: (0, h))
    shared = pl.BlockSpec((1, GDN_D), lambda h, n: (0, 0))
    if rev:
        st = pl.BlockSpec((GDN_HP, None, GDN_D, GDN_D), lambda h, n: (h, n_chunks - 1 - n, 0, 0))
    else:
        st = pl.BlockSpec((GDN_HP, None, GDN_D, GDN_D), lambda h, n: (h, n, 0, 0))
    return tok, par, shared, st


def _heads(ref):
    return jnp.stack([ref[:, h * GDN_D:(h + 1) * GDN_D] for h in range(ref.shape[1] // GDN_D)])


def gdn_chunk_fwd(qkv, z, a_log_x, dt_bias_x, o_norm, name):
    T = qkv.shape[0]
    n_chunks = T // GDN_C
    H = GDN_H
    tok, par, shared, st_spec = _gdn_specs(n_chunks, False)

    def body(q_ref, k_ref, v_ref, g_ref, bl_ref, al_ref, a_ref, dt_ref, on_ref, o_ref, st_ref, state):
        @pl.when(pl.program_id(1) == 0)
        def _():
            state[...] = jnp.zeros_like(state)

        st = state[...]
        st_ref[...] = st
        o, st_new = _gdn_chunk(_heads(q_ref), _heads(k_ref), _heads(v_ref), _heads(g_ref), _heads(bl_ref), _heads(al_ref),
                               _heads(a_ref), _heads(dt_ref), on_ref[...], st)
        for hh in range(GDN_HP):
            o_ref[:, hh * GDN_D:(hh + 1) * GDN_D] = o[hh].astype(o_ref.dtype)
        state[...] = st_new

    B = _GB
    return pl.pallas_call(
        body, name=name, grid=(B, n_chunks),
        in_specs=[tok(0), tok(B), tok(2 * B), tok(3 * B), tok(4 * B), tok(5 * B), par, par, shared],
        out_specs=[tok(0), st_spec],
        out_shape=[jax.ShapeDtypeStruct((T, H * GDN_D), BF16), jax.ShapeDtypeStruct((H, n_chunks, GDN_D, GDN_D), F32)],
        scratch_shapes=[pltpu.VMEM((GDN_HP, GDN_D, GDN_D), F32)],
        compiler_params=_cparams(("parallel", "arbitrary")),
    )(qkv, qkv, qkv, z, z, z, a_log_x, dt_bias_x, o_norm)


def gdn_chunk_bwd(qkv, z, a_log_x, dt_bias_x, o_norm, states, do, name):
    T = qkv.shape[0]
    n_chunks = T // GDN_C
    H = GDN_H
    tok, par, shared, st_spec = _gdn_specs(n_chunks, True)

    def body(q_ref, k_ref, v_ref, g_ref, bl_ref, al_ref, a_ref, dt_ref, on_ref, st_ref, do_ref,
             dq_ref, dk_ref, dv_ref, dg_ref, dbl_ref, dal_ref, da_ref, ddt_ref, don_ref, dstate):
        h, n = pl.program_id(0), pl.program_id(1)

        @pl.when(n == 0)
        def _():
            dstate[...] = jnp.zeros_like(dstate)
            da_ref[...] = jnp.zeros_like(da_ref)
            ddt_ref[...] = jnp.zeros_like(ddt_ref)

        @pl.when((n == 0) & (h == 0))
        def _():
            don_ref[...] = jnp.zeros_like(don_ref)

        _, vjp = jax.vjp(_gdn_chunk, _heads(q_ref), _heads(k_ref), _heads(v_ref), _heads(g_ref), _heads(bl_ref), _heads(al_ref),
                         _heads(a_ref), _heads(dt_ref), on_ref[...], st_ref[...])
        dq, dk, dv, dg, dbl, dal, da, ddt, don, dst = vjp((_heads(do_ref).astype(F32), dstate[...]))
        for hh in range(GDN_HP):
            cols = slice(hh * GDN_D, (hh + 1) * GDN_D)
            dq_ref[:, cols] = dq[hh]
            dk_ref[:, cols] = dk[hh]
            dv_ref[:, cols] = dv[hh]
            dg_ref[:, cols] = dg[hh].astype(dg_ref.dtype)
            dbl_ref[:, cols] = dbl[hh].astype(dbl_ref.dtype)
            dal_ref[:, cols] = dal[hh].astype(dal_ref.dtype)
            da_ref[:, cols] += da[hh]
            ddt_ref[:, cols] += ddt[hh]
        don_ref[...] += don
        dstate[...] = dst

    tok0 = tok(0)
    B = _GB
    f32_tok = jax.ShapeDtypeStruct((T, H * GDN_D), F32)
    bf_tok = jax.ShapeDtypeStruct((T, H * GDN_D), BF16)
    par_sh = jax.ShapeDtypeStruct((1, H * GDN_D), F32)
    return pl.pallas_call(
        body, name=name, grid=(B, n_chunks),
        in_specs=[tok(0), tok(B), tok(2 * B), tok(3 * B), tok(4 * B), tok(5 * B), par, par, shared, st_spec, tok0],
        out_specs=[tok0] * 6 + [par, par, shared],
        out_shape=[f32_tok, f32_tok, f32_tok, bf_tok, bf_tok, bf_tok, par_sh, par_sh, jax.ShapeDtypeStruct((1, GDN_D), F32)],
        scratch_shapes=[pltpu.VMEM((GDN_HP, GDN_D, GDN_D), F32)],
        compiler_params=_cparams(("arbitrary", "arbitrary")),
    )(qkv, qkv, qkv, z, z, z, a_log_x, dt_bias_x, o_norm, states, do)


def _prefetch_call(body, name, grid, in_specs, out_specs, out_shape, aliases=None):
    return pl.pallas_call(
        body, name=name,
        grid_spec=pltpu.PrefetchScalarGridSpec(num_scalar_prefetch=1, grid=grid, in_specs=in_specs, out_specs=out_specs),
        out_shape=out_shape, input_output_aliases=aliases or {},
        compiler_params=_cparams(("parallel",) * len(grid)))


def cast_into(src, slab, row0, me, name):
    rows, width = src.shape
    tr = _pick_rows(rows)
    assert row0 % tr == 0

    def body(me_ref, s_ref, *refs):
        refs[-1][...] = s_ref[...].astype(refs[-1].dtype)

    in_specs = [pl.BlockSpec((tr, width), lambda r, me_ref: (r, 0))]
    args = [src]
    aliases = {}
    if slab.arr is not None:
        in_specs.append(_ANY)
        args.append(slab.arr)
        aliases = {2: 0}
    slab.arr = _prefetch_call(
        body, name, (rows // tr,), in_specs,
        pl.BlockSpec((None, tr, width), lambda r, me_ref: (me_ref[0], row0 // tr + r, 0)),
        jax.ShapeDtypeStruct(slab.shape, slab.dtype), aliases)(me, *args)


def pair_add(g, b, c_idx, name):
    n, rh, w = b.shape
    tr = _pick_rows(rh)
    nb = rh // tr

    def body(c_ref, g_ref, b_ref, o_ref):
        o_ref[...] = (g_ref[...].astype(F32) + b_ref[...].astype(F32)).astype(o_ref.dtype)

    return _prefetch_call(
        body, name, (n, nb),
        [pl.BlockSpec((None, tr, w), lambda k, r, c: (k, c[0] * nb + r, 0)), pl.BlockSpec((None, tr, w), lambda k, r, c: (k, r, 0))],
        pl.BlockSpec((None, tr, w), lambda k, r, c: (k, r, 0)), jax.ShapeDtypeStruct(b.shape, BF16))(c_idx, g, b)


def chip_sum(p, rv, mc, name):
    n, rh, w = p.shape
    tr = _pick_rows(rh)
    nb = rh // tr

    def body(mc_ref, p_ref, rv_ref, o_ref):
        me = mc_ref[0]
        acc = None
        for k in range(n):
            part = jnp.where(me == k, p_ref[...], rv_ref[k]).astype(F32)
            acc = part if acc is None else acc + part
        o_ref[...] = acc

    return _prefetch_call(
        body, name, (nb,),
        [pl.BlockSpec((None, tr, w), lambda r, mc_ref: (mc_ref[0], r, 0)), pl.BlockSpec((n, tr, w), lambda r, mc_ref: (0, r, 0))],
        pl.BlockSpec((tr, w), lambda r, mc_ref: (mc_ref[1] * nb + r, 0)), jax.ShapeDtypeStruct((2 * rh, w), F32))(mc, p, rv)


def adamw(red, row0, w, m, v, name):
    rows, cols = w.shape
    tr = _pick_rows(rows)
    assert row0 % tr == 0

    def body(g_ref, w_ref, m_ref, v_ref, go_ref, d_ref, nm_ref, nv_ref):
        gv = g_ref[...]
        nm = ADAM_B1 * m_ref[...] + (1.0 - ADAM_B1) * gv
        nv = ADAM_B2 * v_ref[...] + (1.0 - ADAM_B2) * (gv * gv)
        m_hat = nm / (1.0 - ADAM_B1 ** ADAM_STEP)
        v_hat = nv / (1.0 - ADAM_B2 ** ADAM_STEP)
        go_ref[...] = gv
        d_ref[...] = -ADAM_LR * (m_hat / (jnp.sqrt(v_hat) + ADAM_EPS) + ADAM_WD * w_ref[...])
        nm_ref[...] = nm
        nv_ref[...] = nv

    spec = pl.BlockSpec((tr, cols), lambda r: (r, 0))
    sh = jax.ShapeDtypeStruct((rows, cols), F32)
    return pl.pallas_call(
        body, name=name, grid=(rows // tr,),
        in_specs=[pl.BlockSpec((tr, cols), lambda r: (row0 // tr + r, 0)), spec, spec, spec],
        out_specs=[spec] * 4, out_shape=[sh] * 4, compiler_params=_cparams(("parallel",)),
    )(red, w, m, v)


def _place():
    x, y, c = lax.axis_index("x"), lax.axis_index("y"), lax.axis_index("c")
    chips = [(1 - x, y), (x, 1 - y), (1 - x, 1 - y)]
    return x, y, c, chips


def _chip_index(cx, cy):
    return 2 * cx + cy


def _remote(src, dst, send_sem, recv_sem, to):
    return pltpu.make_async_remote_copy(src_ref=src, dst_ref=dst, send_sem=send_sem, recv_sem=recv_sem,
                                        device_id=to, device_id_type=MESH)


def _comm_call(body, name, ins, out_shapes, n_sems, aliases):
    return pl.pallas_call(
        body, name=name, in_specs=[_ANY] * len(ins), out_specs=[_ANY] * len(out_shapes), out_shape=out_shapes,
        scratch_shapes=[pltpu.SemaphoreType.DMA((n_sems,)), pltpu.SemaphoreType.DMA((n_sems,))],
        input_output_aliases=aliases,
    )(*ins)


def gather_slabs(slabs, name="weight_all_gather"):
    n = len(slabs)

    def body(*refs):
        in_refs, out_refs, send_sems, recv_sems = refs[:n], refs[n:2 * n], refs[-2], refs[-1]
        x, y, c, chips = _place()
        me = _chip_index(x, y)
        sib = (x, y, 1 - c)
        first, passed = [], []
        for a in range(n):
            rh = in_refs[a].shape[1] // 2
            mine = pl.ds(c * rh, rh)
            for j, chip in enumerate(chips):
                cp = _remote(in_refs[a].at[me, mine], out_refs[a].at[me, mine], send_sems.at[6 * a + j],
                             recv_sems.at[6 * a + j], (*chip, c))
                cp.start()
                first.append(cp)
        for a in range(n):
            rh = in_refs[a].shape[1] // 2
            mine = pl.ds(c * rh, rh)
            for j, chip in enumerate(chips):
                landed = out_refs[a].at[_chip_index(*chip), mine]
                _remote(landed, landed, send_sems.at[6 * a + j], recv_sems.at[6 * a + j], (*chip, c)).wait_recv()
                cp = _remote(landed, landed, send_sems.at[6 * a + 3 + j], recv_sems.at[6 * a + 3 + j], sib)
                cp.start()
                passed.append(cp)
        for a in range(n):
            rh = in_refs[a].shape[1] // 2
            theirs = pl.ds((1 - c) * rh, rh)
            for j, chip in enumerate(chips):
                got = out_refs[a].at[_chip_index(*chip), theirs]
                _remote(got, got, send_sems.at[6 * a + 3 + j], recv_sems.at[6 * a + 3 + j], sib).wait_recv()
        for cp in first + passed:
            cp.wait_send()

    return _comm_call(body, name, slabs, [jax.ShapeDtypeStruct(s.shape, s.dtype) for s in slabs], 6 * n,
                      {a: a for a in range(n)})


def pair_swap_halves(slabs, name="grad_pair_swap"):
    n = len(slabs)

    def body(*refs):
        in_refs, out_refs, send_sems, recv_sems = refs[:n], refs[n:2 * n], refs[-2], refs[-1]
        x, y, c, _ = _place()
        cps = []
        for a in range(n):
            rh = in_refs[a].shape[1] // 2
            cp = _remote(in_refs[a].at[:, pl.ds((1 - c) * rh, rh), :], out_refs[a], send_sems.at[a], recv_sems.at[a], (x, y, 1 - c))
            cp.start()
            cps.append(cp)
        for cp in cps:
            cp.wait()

    outs = [jax.ShapeDtypeStruct((s.shape[0], s.shape[1] // 2, s.shape[2]), s.dtype) for s in slabs]
    return _comm_call(body, name, slabs, outs, n, {})


def chip_exchange(parts, name="grad_chip_exchange"):
    n = len(parts)

    def body(*refs):
        in_refs, out_refs, send_sems, recv_sems = refs[:n], refs[n:2 * n], refs[-2], refs[-1]
        x, y, c, chips = _place()
        me = _chip_index(x, y)
        sends = []
        for a in range(n):
            for j, chip in enumerate(chips):
                cp = _remote(in_refs[a].at[_chip_index(*chip)], out_refs[a].at[me], send_sems.at[3 * a + j],
                             recv_sems.at[3 * a + j], (*chip, c))
                cp.start()
                sends.append(cp)
        for a in range(n):
            for j, chip in enumerate(chips):
                got = out_refs[a].at[_chip_index(*chip)]
                _remote(got, got, send_sems.at[3 * a + j], recv_sems.at[3 * a + j], (*chip, c)).wait_recv()
        for cp in sends:
            cp.wait_send()

    return _comm_call(body, name, parts, [jax.ShapeDtypeStruct(p.shape, p.dtype) for p in parts], 3 * n, {})


def pair_join_halves(reds, name="grad_pair_join"):
    n = len(reds)

    def body(*refs):
        in_refs, out_refs, send_sems, recv_sems = refs[:n], refs[n:2 * n], refs[-2], refs[-1]
        x, y, c, _ = _place()
        cps = []
        for a in range(n):
            rh = in_refs[a].shape[0] // 2
            mine = pl.ds(c * rh, rh)
            cp = _remote(in_refs[a].at[mine], out_refs[a].at[mine], send_sems.at[a], recv_sems.at[a], (x, y, 1 - c))
            cp.start()
            cps.append(cp)
        for a in range(n):
            rh = in_refs[a].shape[0] // 2
            got = out_refs[a].at[pl.ds((1 - c) * rh, rh)]
            _remote(got, got, send_sems.at[a], recv_sems.at[a], (x, y, 1 - c)).wait_recv()
        for cp in cps:
            cp.wait_send()

    return _comm_call(body, name, reds, [jax.ShapeDtypeStruct(r.shape, r.dtype) for r in reds], n, {a: a for a in range(n)})


_SLABS = {
    "w1024": [("mlp_w1", 2), ("mlp_w2", 1), ("xa_w_q", 1), ("xa_w_o", 1), ("mla_w_o", 1), ("gdn_w_o", 1), ("sc_w_o", 1)],
    "xa_w_kv": [("xa_w_kv", 2)], "sc_w_in": [("sc_w_in", 2)],
    "mla_w_in": [("mla_w_in", 1)], "mla_w_uq": [("mla_w_uq", 2)], "mla_w_ukv": [("mla_w_ukv", 2)], "gdn_w_in": [("gdn_w_in", 2)],
}
_RELAID = ("mla_w_in", "mla_w_uq", "mla_w_ukv", "gdn_w_in")
_SMALL = [("mla_q_norm", 1), ("mla_kv_norm", 1), ("gdn_conv_w", 2), ("sc_conv_w", 2)]
_REPL = ["gdn_a_log", "gdn_dt_bias", "gdn_o_norm", "norm_mix", "norm_mem", "norm_mlp", "mem_norm", "final_norm"]
_WEIGHTS = ['mla_w_in', 'mla_q_norm', 'mla_kv_norm', 'mla_w_uq', 'mla_w_ukv', 'mla_w_o', 'gdn_w_in', 'gdn_conv_w',
            'gdn_a_log', 'gdn_dt_bias', 'gdn_o_norm', 'gdn_w_o', 'sc_w_in', 'sc_conv_w', 'sc_w_o', 'norm_mix',
            'norm_mem', 'norm_mlp', 'xa_w_q', 'xa_w_kv', 'xa_w_o', 'mlp_w1', 'mlp_w2', 'mem_norm', 'final_norm']


class Layout:
    def __init__(self, shard_shapes):
        self.where, self.slab_dims = {}, {}
        for slab, members in _SLABS.items():
            off = 0
            for name, axis in members:
                layers, rpl, width = shard_shapes[name]
                self.where[name] = (slab, off, layers, rpl, width, axis)
                off += layers * rpl
            self.slab_dims[slab] = (off, width)

    def new_slabs(self, dtype):
        return {s: Slab(rows, width, dtype) for s, (rows, width) in self.slab_dims.items()}

    def loc(self, slabs, name, layer):
        slab, off, _, rpl, width, axis = self.where[name]
        if axis == 1:
            return Loc(slabs[slab], off + layer * rpl, N_CHIPS * rpl, width, 0)
        return Loc(slabs[slab], off + layer * rpl, rpl, N_CHIPS * width, 1)

    def full(self, slabs, name):
        slab, off, layers, rpl, width, axis = self.where[name]
        blocks = slabs[slab].arr[:, off:off + layers * rpl].reshape(N_CHIPS, layers, rpl, width)
        return jnp.concatenate([blocks[s] for s in range(N_CHIPS)], axis=axis)

    def put_full(self, slabs, name, grad):
        slab, off, layers, rpl, width, axis = self.where[name]
        assert off == 0 and len(_SLABS[slab]) == 1
        parts = jnp.stack(jnp.split(grad, N_CHIPS, axis=axis)).reshape(N_CHIPS, layers * rpl, width)
        slabs[slab].arr = parts.astype(slabs[slab].dtype)


def _small_pack(vals, names):
    flat = jnp.concatenate([vals[n].astype(F32).reshape(-1) for n in names])
    return jnp.pad(flat, (0, SMALL_ROWS * SMALL_COLS - flat.shape[0])).reshape(SMALL_ROWS, SMALL_COLS)


def _small_unpack(flat, like, names):
    out, off = {}, 0
    flat = flat.reshape(-1)
    for n in names:
        out[n] = flat[off:off + like[n].size].reshape(like[n].shape)
        off += like[n].size
    return out


_MLA_CFG = _Attn(MLA_H, 2 * LANES, MLA_NOPE, MLA_V, 0, MLA_H, True, (MLA_NOPE + MLA_ROPE) ** -0.5)
_XA_CFG = _Attn(XA_H, XA_D, XA_D, XA_D, 0, XA_H, False, XA_D ** -0.5)


def _mla_weights(w_in, w_uq, w_ukv):
    w_in_p = jnp.pad(w_in, ((0, 0), (0, MLA_ZPAD - w_in.shape[1])))
    w_uq_p = jnp.pad(w_uq.reshape(MLA_QR, MLA_H, MLA_NOPE + MLA_ROPE), ((0, 0), (0, 0), (0, 2 * LANES - MLA_NOPE - MLA_ROPE)))
    w_uq_p = w_uq_p.reshape(MLA_QR, MLA_H * 2 * LANES)
    kv = w_ukv.reshape(MLA_KVR, MLA_H, MLA_NOPE + MLA_V)
    w_ukv_p = jnp.concatenate([kv[:, :, :MLA_NOPE].reshape(MLA_KVR, -1), kv[:, :, MLA_NOPE:].reshape(MLA_KVR, -1)], axis=1)
    return w_in_p, w_uq_p, w_ukv_p


def _mla_weight_grads(d_in_p, d_uq_p, d_ukv_p):
    d_in = d_in_p[:, :MLA_QR + MLA_KVR + MLA_ROPE]
    d_uq = d_uq_p.reshape(MLA_QR, MLA_H, 2 * LANES)[:, :, :MLA_NOPE + MLA_ROPE].reshape(MLA_QR, -1)
    half = MLA_H * MLA_NOPE
    d_ukv = jnp.concatenate([d_ukv_p[:, :half].reshape(MLA_KVR, MLA_H, MLA_NOPE),
                             d_ukv_p[:, half:].reshape(MLA_KVR, MLA_H, MLA_V)], axis=2).reshape(MLA_KVR, -1)
    return d_in, d_uq, d_ukv


def _mla_fwd(xs, h, wts, w_o, qn, kvn, tabs, tag):
    w_in_p, w_uq_p, w_ukv_p = wts
    z = mm(h, w_in_p, "nn", f"{tag}_in")
    cq, ckv, kr = mla_mid_fwd(z, qn, kvn, tabs, f"{tag}_mid")
    q = rope_q(mm(cq, w_uq_p, "nn", f"{tag}_uq"), tabs, False, f"{tag}_ropeq")
    kv = mm(ckv, w_ukv_p, "nn", f"{tag}_ukv", outs=(BF16,))
    o, lse = flash_fwd(_MLA_CFG, q, kv, kv, kr, f"{tag}_attn")
    xs = mm(o, w_o, "nn", f"{tag}_out", epi=_epi_add, extras=(xs,))
    return xs, (z, cq, ckv, kr, q, kv, o, lse)


def _mla_bwd(dx, h, wts, w_o, g_wo, qn, kvn, tabs, saved, tag):
    w_in_p, w_uq_p, w_ukv_p = wts
    z, cq, ckv, kr, q, kv, o, lse = saved
    mm(o, dx, "tn", f"{tag}_dwo", outs=(BF16,), out_loc=g_wo)
    do = mm(dx, w_o, "nt", f"{tag}_do", outs=(BF16,))
    dq, delta = flash_dq(_MLA_CFG, q, kv, kv, kr, o, do, lse, F32, f"{tag}_attn_dq")
    dk1, dv, dkr = flash_dkv(_MLA_CFG, q, kv, kv, kr, do, lse, delta, BF16, f"{tag}_attn_dkv")
    dqp = rope_q(dq, tabs, True, f"{tag}_ropeq_t")
    d_uq_p = mm(cq, dqp, "tn", f"{tag}_duq")
    dcq = mm(dqp, w_uq_p, "nt", f"{tag}_dcq")
    dkv = jnp.concatenate([dk1, dv], axis=1)
    d_ukv_p = mm(ckv, dkv, "tn", f"{tag}_dukv")
    dckv = mm(dkv, w_ukv_p, "nt", f"{tag}_dckv")
    dz, dqn, dkvn = mla_mid_bwd(z, qn, kvn, tabs, dcq, dckv, dkr, f"{tag}_mid_bwd")
    d_in_p = mm(h, dz, "tn", f"{tag}_din")
    dh = mm(dz, w_in_p, "nt", f"{tag}_dh")
    d_in, d_uq, d_ukv = _mla_weight_grads(d_in_p, d_uq_p, d_ukv_p)
    return dh, dict(mla_w_in=d_in, mla_w_uq=d_uq, mla_w_ukv=d_ukv, mla_q_norm=dqn, mla_kv_norm=dkvn)


_GDN_QKV = 3 * GDN_H * GDN_D
_GDN_GATE_END = _GDN_QKV + GDN_H * GDN_D


def _gdn_weights(w_in):
    rep = lambda cols: jnp.repeat(cols, GDN_D, axis=1)
    return jnp.concatenate([w_in[:, :_GDN_GATE_END], rep(w_in[:, _GDN_GATE_END:_GDN_GATE_END + GDN_H]),
                            rep(w_in[:, _GDN_GATE_END + GDN_H:])], axis=1)


def _fold(x):
    return x.reshape(x.shape[0], -1, GDN_D).sum(-1)


def _gdn_fwd(xs, h, w_in_x, conv_w, a_log, dt_bias, o_norm, w_o, tag):
    z = mm(h, w_in_x, "nn", f"{tag}_in")
    qkv = gdn_conv_fwd(z, conv_w, f"{tag}_conv")
    a_x, dt_x = jnp.repeat(a_log.reshape(1, -1), GDN_D, axis=1), jnp.repeat(dt_bias.reshape(1, -1), GDN_D, axis=1)
    og, states = gdn_chunk_fwd(qkv, z, a_x, dt_x, o_norm.reshape(1, -1), f"{tag}_chunks")
    xs = mm(og, w_o, "nn", f"{tag}_out", epi=_epi_add, extras=(xs,))
    return xs, (z, qkv, a_x, dt_x, og, states)


def _gdn_bwd(dx, h, w_in_x, conv_w, o_norm, w_o, g_wo, saved, tag):
    z, qkv, a_x, dt_x, og, states = saved
    mm(og, dx, "tn", f"{tag}_dwo", outs=(BF16,), out_loc=g_wo)
    dog = mm(dx, w_o, "nt", f"{tag}_dog")
    dq, dk, dv, dgate, dbl, dal, da_x, ddt_x, don = gdn_chunk_bwd(qkv, z, a_x, dt_x, o_norm.reshape(1, -1), states, dog,
                                                                  f"{tag}_chunks_bwd")
    dpre, dconv = gdn_conv_bwd(z, conv_w, jnp.concatenate([dq, dk, dv], axis=1), f"{tag}_conv_bwd")
    dz = jnp.concatenate([dpre, dgate, dbl, dal], axis=1)
    d_in_x = mm(h, dz, "tn", f"{tag}_din")
    dh = mm(dz, w_in_x, "nt", f"{tag}_dh", tn=512)
    ge = _GDN_GATE_END
    d_in = jnp.concatenate([d_in_x[:, :ge], _fold(d_in_x[:, ge:ge + GDN_H * GDN_D]), _fold(d_in_x[:, ge + GDN_H * GDN_D:])], axis=1)
    return dh, dict(gdn_w_in=d_in, gdn_conv_w=dconv, gdn_a_log=_fold(da_x).reshape(-1), gdn_dt_bias=_fold(ddt_x).reshape(-1),
                    gdn_o_norm=don.reshape(-1))


def _sc_fwd(xs, h, w_in, conv_w, w_o, tag):
    z = mm(h, w_in, "nn", f"{tag}_in")
    y = sc_fwd(z, conv_w, f"{tag}_conv")
    xs = mm(y, w_o, "nn", f"{tag}_out", epi=_epi_add, extras=(xs,))
    return xs, (z, y)


def _sc_bwd(dx, h, w_in, g_win, conv_w, w_o, g_wo, saved, tag):
    z, y = saved
    mm(y, dx, "tn", f"{tag}_dwo", outs=(BF16,), out_loc=g_wo)
    dy = mm(dx, w_o, "nt", f"{tag}_dy")
    db, dc, du, dconv = sc_bwd(z, conv_w, dy, f"{tag}_conv_bwd")
    dz = jnp.concatenate([db, dc, du], axis=1)
    mm(h, dz, "tn", f"{tag}_din", outs=(BF16,), out_loc=g_win)
    dh = mm(dz, w_in, "nt", f"{tag}_dh")
    return dh, dict(sc_conv_w=dconv)


def local_step(x, mem, pos, target, lay, wslabs, gslabs, small):
    depth = small["norm_mix"].shape[0]
    W = lambda name, layer: lay.loc(wslabs, name, layer)
    G = lambda name, layer: lay.loc(gslabs, name, layer)
    tabs = rope_tables(pos)
    mem_n = rmsnorm_fwd(mem, small["mem_norm"], "mem_norm")
    full = {n: lay.full(wslabs, n) for n in _RELAID}
    mla_w = [_mla_weights(full["mla_w_in"][j], full["mla_w_uq"][j], full["mla_w_ukv"][j]) for j in range(full["mla_w_in"].shape[0])]
    gdn_in_x = [_gdn_weights(full["gdn_w_in"][j]) for j in range(full["gdn_w_in"].shape[0])]

    xs = x
    saved = []
    for i in range(depth):
        j, kind = i // 3, i % 3
        tag = f"l{i}"
        x_a = xs
        h = rmsnorm_fwd(xs, small["norm_mix"][i], f"{tag}_norm_mix")
        if kind == 0:
            xs, mix = _mla_fwd(xs, h, mla_w[j], W("mla_w_o", j), small["mla_q_norm"][j], small["mla_kv_norm"][j], tabs, f"{tag}_mla")
        elif kind == 1:
            xs, mix = _gdn_fwd(xs, h, gdn_in_x[j], small["gdn_conv_w"][j], small["gdn_a_log"][j], small["gdn_dt_bias"][j],
                               small["gdn_o_norm"][j], W("gdn_w_o", j), f"{tag}_gdn")
        else:
            xs, mix = _sc_fwd(xs, h, W("sc_w_in", j), small["sc_conv_w"][j], W("sc_w_o", j), f"{tag}_sc")
        x_b = xs
        hn = rmsnorm_fwd(xs, small["norm_mem"][i], f"{tag}_norm_mem")
        xq = mm(hn, W("xa_w_q", i), "nn", f"{tag}_xa_q", outs=(BF16,))
        xkv = mm(mem_n, W("xa_w_kv", i), "nn", f"{tag}_xa_kv", outs=(BF16,))
        xo, xlse = flash_fwd(_XA_CFG, xq, xkv, xkv, None, f"{tag}_xa_attn")
        xs = mm(xo, W("xa_w_o", i), "nn", f"{tag}_xa_out", epi=_epi_add, extras=(xs,))
        x_c = xs
        hm = rmsnorm_fwd(xs, small["norm_mlp"][i], f"{tag}_norm_mlp")
        h1, act = mm(hm, W("mlp_w1", i), "nn", f"{tag}_mlp_up", outs=(BF16, BF16), epi=_epi_relu2)
        xs = mm(act, W("mlp_w2", i), "nn", f"{tag}_mlp_down", epi=_epi_add, extras=(xs,))
        saved.append((x_a, h, mix, x_b, hn, xq, xkv, xo, xlse, x_c, hm, h1, act))

    se, dx, d_final = loss_head(xs, small["final_norm"], target)

    per_layer = {n: [None] * depth for n in ("norm_mix", "norm_mem", "norm_mlp")}
    mixer = {}
    dmem_n = jnp.zeros(mem.shape, F32)
    for i in reversed(range(depth)):
        j, kind = i // 3, i % 3
        tag = f"l{i}"
        x_a, h, mix, x_b, hn, xq, xkv, xo, xlse, x_c, hm, h1, act = saved[i]
        mm(act, dx, "tn", f"{tag}_mlp_dw2", outs=(BF16,), out_loc=G("mlp_w2", i))
        dh1 = mm(dx, W("mlp_w2", i), "nt", f"{tag}_mlp_dh1", outs=(BF16,), epi=_epi_relu2_bwd, extras=(h1,))
        mm(hm, dh1, "tn", f"{tag}_mlp_dw1", outs=(BF16,), out_loc=G("mlp_w1", i))
        dhm = mm(dh1, W("mlp_w1", i), "nt", f"{tag}_mlp_dhm")
        dx, per_layer["norm_mlp"][i] = rmsnorm_bwd(x_c, small["norm_mlp"][i], dhm, dx, f"{tag}_norm_mlp_bwd")
        mm(xo, dx, "tn", f"{tag}_xa_dwo", outs=(BF16,), out_loc=G("xa_w_o", i))
        dxo = mm(dx, W("xa_w_o", i), "nt", f"{tag}_xa_do", outs=(BF16,))
        dxq, xdelta = flash_dq(_XA_CFG, xq, xkv, xkv, None, xo, dxo, xlse, BF16, f"{tag}_xa_attn_dq")
        dxk, dxv = flash_dkv(_XA_CFG, xq, xkv, xkv, None, dxo, xlse, xdelta, BF16, f"{tag}_xa_attn_dkv")
        dxkv = jnp.concatenate([dxk, dxv], axis=1)
        mm(hn, dxq, "tn", f"{tag}_xa_dwq", outs=(BF16,), out_loc=G("xa_w_q", i))
        dhn = mm(dxq, W("xa_w_q", i), "nt", f"{tag}_xa_dhn")
        mm(mem_n, dxkv, "tn", f"{tag}_xa_dwkv", outs=(BF16,), out_loc=G("xa_w_kv", i))
        dmem_n = mm(dxkv, W("xa_w_kv", i), "nt", f"{tag}_xa_dmem", epi=_epi_add, extras=(dmem_n,))
        dx, per_layer["norm_mem"][i] = rmsnorm_bwd(x_b, small["norm_mem"][i], dhn, dx, f"{tag}_norm_mem_bwd")
        if kind == 0:
            dh, gr = _mla_bwd(dx, h, mla_w[j], W("mla_w_o", j), G("mla_w_o", j), small["mla_q_norm"][j], small["mla_kv_norm"][j],
                              tabs, mix, f"{tag}_mla")
        elif kind == 1:
            dh, gr = _gdn_bwd(dx, h, gdn_in_x[j], small["gdn_conv_w"][j], small["gdn_o_norm"][j], W("gdn_w_o", j), G("gdn_w_o", j),
                              mix, f"{tag}_gdn")
        else:
            dh, gr = _sc_bwd(dx, h, W("sc_w_in", j), G("sc_w_in", j), small["sc_conv_w"][j], W("sc_w_o", j), G("sc_w_o", j),
                             mix, f"{tag}_sc")
        for n, g in gr.items():
            mixer.setdefault(n, {})[j] = g
        dx, per_layer["norm_mix"][i] = rmsnorm_bwd(x_a, small["norm_mix"][i], dh, dx, f"{tag}_norm_mix_bwd")

    _, d_mem_norm = rmsnorm_bwd(mem, small["mem_norm"], dmem_n, jnp.zeros(mem.shape, F32), "mem_norm_bwd")
    grads = {n: jnp.stack(v) for n, v in per_layer.items()}
    for n, by_j in mixer.items():
        grads[n] = jnp.stack([by_j[j] for j in sorted(by_j)])
    grads["mem_norm"] = d_mem_norm
    grads["final_norm"] = d_final
    for n in _RELAID:
        lay.put_full(gslabs, n, grads.pop(n))
    return se, dx, grads


def kernel(x, mem, positions, mla_w_in, mla_q_norm, mla_kv_norm, mla_w_uq, mla_w_ukv, mla_w_o, gdn_w_in, gdn_conv_w, gdn_a_log, gdn_dt_bias, gdn_o_norm, gdn_w_o, sc_w_in, sc_conv_w, sc_w_o, norm_mix, norm_mem, norm_mlp, xa_w_q, xa_w_kv, xa_w_o, mlp_w1, mlp_w2, mem_norm, final_norm, loss_target, m_mla_w_in, m_mla_q_norm, m_mla_kv_norm, m_mla_w_uq, m_mla_w_ukv, m_mla_w_o, m_gdn_w_in, m_gdn_conv_w, m_gdn_a_log, m_gdn_dt_bias, m_gdn_o_norm, m_gdn_w_o, m_sc_w_in, m_sc_conv_w, m_sc_w_o, m_norm_mix, m_norm_mem, m_norm_mlp, m_xa_w_q, m_xa_w_kv, m_xa_w_o, m_mlp_w1, m_mlp_w2, m_mem_norm, m_final_norm, v_mla_w_in, v_mla_q_norm, v_mla_kv_norm, v_mla_w_uq, v_mla_w_ukv, v_mla_w_o, v_gdn_w_in, v_gdn_conv_w, v_gdn_a_log, v_gdn_dt_bias, v_gdn_o_norm, v_gdn_w_o, v_sc_w_in, v_sc_conv_w, v_sc_w_o, v_norm_mix, v_norm_mem, v_norm_mlp, v_xa_w_q, v_xa_w_kv, v_xa_w_o, v_mlp_w1, v_mlp_w2, v_mem_norm, v_final_norm):
    given = dict(locals())
    p = {n: given[n] for n in _WEIGHTS}
    mom = {n: given["m_" + n] for n in _WEIGHTS}
    var = {n: given["v_" + n] for n in _WEIGHTS}
    split = [n for members in _SLABS.values() for n, _ in members]
    lay = Layout({n: p[n].shape for n in split})
    flat2d = lambda a: a.reshape(-1, a.shape[-1])

    me = (2 * lax.axis_index("x") + lax.axis_index("y")).astype(jnp.int32)
    core = lax.axis_index("c").astype(jnp.int32)
    me1, c1, mc = me.reshape(1), core.reshape(1), jnp.stack([me, core])

    wslabs = lay.new_slabs(BF16)
    for n in split:
        slab, off = lay.where[n][0], lay.where[n][1]
        cast_into(flat2d(p[n]), wslabs[slab], off, me1, f"cast_{n}")
    small_names = [n for n, _ in _SMALL]
    words = lax.bitcast_convert_type(jnp.concatenate([p[n].reshape(-1) for n in small_names]), BF16).reshape(-1)
    words = jnp.pad(words, (0, SMALL_ROWS * SMALL_COLS - words.shape[0])).reshape(1, SMALL_ROWS, SMALL_COLS)
    small_slab = lax.dynamic_update_slice(jnp.zeros((N_CHIPS, SMALL_ROWS, SMALL_COLS), BF16), words, (me, 0, 0))
    order = list(_SLABS)
    gathered = gather_slabs([wslabs[s].arr for s in order] + [small_slab])
    for s, arr in zip(order, gathered):
        wslabs[s].arr = arr
    small = {n: p[n] for n in _REPL}
    got, off = gathered[-1].reshape(N_CHIPS, -1), 0
    for n, ax in _SMALL:
        vals = lax.bitcast_convert_type(got[:, off:off + 2 * p[n].size].reshape(N_CHIPS, p[n].size, 2), F32)
        vals = vals.reshape((N_CHIPS,) + p[n].shape)
        small[n] = jnp.concatenate([vals[s] for s in range(N_CHIPS)], axis=ax)
        off += 2 * p[n].size

    gslabs = lay.new_slabs(BF16)
    se, dx, sgrads = local_step(x[0], mem[0], positions.reshape(-1, 1), loss_target[0], lay, wslabs, gslabs, small)
    loss = lax.psum(0.5 * jnp.sum(se) / x.shape[-1], ("x", "y", "c"))

    axes = dict(_SMALL)
    small_order = small_names + _REPL
    slots = []
    for s in range(N_CHIPS):
        vals = {n: (lax.slice_in_dim(g, s * p[n].shape[axes[n]], (s + 1) * p[n].shape[axes[n]], axis=axes[n]) if n in axes else g)
                for n, g in sgrads.items()}
        slots.append(_small_pack(vals, small_order))
    g_list = [gslabs[s].arr for s in order] + [jnp.stack(slots).astype(BF16)]

    swapped = pair_swap_halves(g_list)
    names = order + ["small"]
    partial = [pair_add(g, b, c1, f"pair_add_{s}") for g, b, s in zip(g_list, swapped, names)]
    received = chip_exchange(partial)
    halves = [chip_sum(q, r, mc, f"chip_sum_{s}") for q, r, s in zip(partial, received, names)]
    reduced = dict(zip(names, pair_join_halves(halves)))

    res = {}
    for n in split:
        slab, off = lay.where[n][0], lay.where[n][1]
        outs = adamw(reduced[slab], off, flat2d(p[n]), flat2d(mom[n]), flat2d(var[n]), f"adamw_{n}")
        res[n] = [o.reshape(p[n].shape) for o in outs]
    outs = adamw(reduced["small"], 0, _small_pack(p, small_order), _small_pack(mom, small_order), _small_pack(var, small_order),
                 "adamw_small")
    unpacked = [_small_unpack(o, p, small_order) for o in outs]
    for n in small_order:
        res[n] = [u[n] for u in unpacked]
    return (loss, dx[None], *[res[n][k] for k in range(4) for n in _WEIGHTS])
```

```python
import jax
import jax.numpy as jnp
from jax import lax
from jax.experimental import pallas as pl
from jax.experimental.pallas import tpu as pltpu

F32 = jnp.float32
BF16 = jnp.bfloat16
HI = lax.Precision.HIGHEST
MESH = pl.DeviceIdType.MESH

EPS = 1e-6
ROPE_THETA = 10000.0
N_CHIPS = 4
LANES = 128
VMEM_LIMIT = 56 * 1024 * 1024
NEG = -1e30

MLA_H, MLA_NOPE, MLA_ROPE, MLA_V = 8, 128, 64, 128
MLA_QR, MLA_KVR = 384, 256
MLA_ZPAD = 768
GDN_H, GDN_D, GDN_C = 8, 128, 64
XA_H, XA_D = 4, 256

ADAM_LR, ADAM_B1, ADAM_B2, ADAM_EPS, ADAM_WD, ADAM_STEP = 0.001, 0.9, 0.999, 1e-08, 0.01, 10

SMALL_ROWS, SMALL_COLS = 32, 1024


def _cparams(sem=None):
    return pltpu.CompilerParams(dimension_semantics=sem, vmem_limit_bytes=VMEM_LIMIT)


def _pick(dim, pref):
    t = (min(pref, dim) // LANES) * LANES
    while t >= LANES:
        if dim % t == 0:
            return t
        t -= LANES
    return dim


def _pick_rows(rows, pref=256):
    t = pref
    while rows % t:
        t //= 2
    return t


class Slab:
    def __init__(self, rows, width, dtype, arr=None):
        self.shape, self.dtype, self.arr = (N_CHIPS, rows, width), dtype, arr


class Loc:
    def __init__(self, slab, row0, K, N, axis):
        self.slab, self.row0, self.K, self.N, self.axis = slab, row0, K, N, axis
        self.Ks = K // N_CHIPS if axis == 0 else K
        self.Ns = N // N_CHIPS if axis == 1 else N

    def tile_spec(self, tr, tc, rc):
        assert self.row0 % tr == 0 and self.Ks % tr == 0 and self.Ns % tc == 0, (self.row0, self.Ks, self.Ns, tr, tc)
        r0, rb, cb = self.row0 // tr, self.Ks // tr, self.Ns // tc
        if self.axis == 0:
            return pl.BlockSpec((None, tr, tc), lambda i, j: (rc(i, j)[0] // rb, r0 + rc(i, j)[0] % rb, rc(i, j)[1]))
        return pl.BlockSpec((None, tr, tc), lambda i, j: (rc(i, j)[1] // cb, r0 + rc(i, j)[0], rc(i, j)[1] % cb))

    def slot_spec(self, slot, tr, tc, rc):
        assert self.row0 % tr == 0, (self.row0, tr)
        r0 = self.row0 // tr
        return pl.BlockSpec((None, tr, tc), lambda i, j: (slot, r0 + rc(i, j)[0], rc(i, j)[1]))


_DIMS = {"nn": ((1,), (0,)), "nt": ((1,), (1,)), "tn": ((0,), (0,))}
_ANY = pl.BlockSpec(memory_space=pl.ANY)


def mm(a, b, mode, name, outs=(F32,), epi=None, extras=(), tm=512, tn=1024, out_loc=None):
    b_loc = b if isinstance(b, Loc) else None
    if mode == "nn":
        M, K = a.shape
        K2, N = (b_loc.K, b_loc.N) if b_loc else b.shape
    elif mode == "nt":
        M, K = a.shape
        N, K2 = (b_loc.K, b_loc.N) if b_loc else b.shape
    else:
        K, M = a.shape
        K2, N = b.shape
    assert K == K2, (name, a.shape, K2, N)
    tm = _pick(out_loc.Ks if (out_loc and out_loc.axis == 0) else M, tm)
    if out_loc is not None and out_loc.axis == 1:
        tn = _pick(out_loc.Ns, tn)
    elif b_loc is not None and ((mode == "nn" and b_loc.axis == 1) or (mode == "nt" and b_loc.axis == 0)):
        tn = _pick(b_loc.Ns if mode == "nn" else b_loc.Ks, tn)
    else:
        tn = _pick(N, tn)

    parts = 1
    if mode == "tn":
        a_spec = pl.BlockSpec((K, tm), lambda i, j: (0, i))
        b_specs, b_args = [pl.BlockSpec((K, tn), lambda i, j: (0, j))], [b]
    else:
        a_spec = pl.BlockSpec((tm, K), lambda i, j: (i, 0))
        if b_loc is None:
            b_specs = [pl.BlockSpec((K, tn), lambda i, j: (0, j)) if mode == "nn" else pl.BlockSpec((tn, K), lambda i, j: (j, 0))]
            b_args = [b]
        elif mode == "nn" and b_loc.axis == 1:
            b_specs, b_args = [b_loc.tile_spec(K, tn, lambda i, j: (0, j))], [b_loc.slab.arr]
        elif mode == "nt" and b_loc.axis == 0:
            b_specs, b_args = [b_loc.tile_spec(tn, K, lambda i, j: (j, 0))], [b_loc.slab.arr]
        elif mode == "nn":
            parts = N_CHIPS
            b_specs = [b_loc.slot_spec(s, b_loc.Ks, tn, lambda i, j: (0, j)) for s in range(parts)]
            b_args = [b_loc.slab.arr] * parts
        else:
            parts = N_CHIPS
            b_specs = [b_loc.slot_spec(s, tn, b_loc.Ns, lambda i, j: (j, 0)) for s in range(parts)]
            b_args = [b_loc.slab.arr] * parts
    kp = K // parts
    n_ex, n_out = len(extras), len(outs)
    dims = (_DIMS[mode], ((), ()))

    def body(*refs):
        a_ref = refs[0]
        b_refs = refs[1:1 + parts]
        ex_refs = refs[1 + parts:1 + parts + n_ex]
        o_refs = refs[-n_out:]
        acc = None
        for s in range(parts):
            av = a_ref[...] if parts == 1 else a_ref[:, s * kp:(s + 1) * kp]
            d = lax.dot_general(av.astype(BF16), b_refs[s][...].astype(BF16), dims, preferred_element_type=F32)
            acc = d if acc is None else acc + d
        res = epi(acc, *[e[...] for e in ex_refs]) if epi is not None else (acc,)
        for o_ref, v in zip(o_refs, res):
            o_ref[...] = v.astype(o_ref.dtype)

    mn_spec = pl.BlockSpec((tm, tn), lambda i, j: (i, j))
    in_specs = [a_spec] + b_specs + [mn_spec] * n_ex
    args = [a] + b_args + list(extras)
    aliases = {}
    if out_loc is None:
        out_specs = [mn_spec] * n_out
        out_shape = [jax.ShapeDtypeStruct((M, N), d) for d in outs]
    else:
        assert n_out == 1 and mode == "tn"
        out_specs = [out_loc.tile_spec(tm, tn, lambda i, j: (i, j))]
        out_shape = [jax.ShapeDtypeStruct(out_loc.slab.shape, out_loc.slab.dtype)]
        if out_loc.slab.arr is not None:
            in_specs.append(_ANY)
            args.append(out_loc.slab.arr)
            aliases = {len(args) - 1: 0}

    res = pl.pallas_call(
        body, name=name, grid=(M // tm, N // tn), in_specs=in_specs, out_specs=out_specs, out_shape=out_shape,
        input_output_aliases=aliases, compiler_params=_cparams(("parallel", "parallel")),
    )(*args)
    if out_loc is not None:
        out_loc.slab.arr = res[0]
        return None
    return res[0] if n_out == 1 else tuple(res)


def _epi_add(acc, r):
    return (acc + r,)


def _epi_relu2(acc):
    r = jnp.maximum(acc, 0.0)
    return acc, r * r


def _epi_relu2_bwd(acc, h1):
    return (acc * (2.0 * jnp.maximum(h1.astype(F32), 0.0)),)


def _rms(x, g):
    return x * lax.rsqrt(jnp.mean(x * x, axis=-1, keepdims=True) + EPS) * g


def _row_spec(ts, cols):
    return pl.BlockSpec((ts, cols), lambda i: (i, 0))


def _par_spec(cols):
    return pl.BlockSpec((1, cols), lambda i: (0, 0))


def rmsnorm_fwd(x, g, name, ts=256):
    T, D = x.shape
    ts = min(ts, T)

    def body(x_ref, g_ref, o_ref):
        o_ref[...] = _rms(x_ref[...], g_ref[...]).astype(o_ref.dtype)

    return pl.pallas_call(
        body, name=name, grid=(T // ts,),
        in_specs=[_row_spec(ts, D), _par_spec(D)], out_specs=_row_spec(ts, D),
        out_shape=jax.ShapeDtypeStruct((T, D), BF16), compiler_params=_cparams(("parallel",)),
    )(x, g.reshape(1, D))


def rmsnorm_bwd(x, g, dy, dx_in, name, ts=256):
    T, D = x.shape
    ts = min(ts, T)

    def body(x_ref, g_ref, dy_ref, dxi_ref, dx_ref, dg_ref):
        xv = x_ref[...]
        r = lax.rsqrt(jnp.mean(xv * xv, axis=-1, keepdims=True) + EPS)
        xh = xv * r
        dyv = dy_ref[...].astype(F32)
        dxh = dyv * g_ref[...]
        dx_ref[...] = dxi_ref[...] + r * (dxh - xh * jnp.mean(dxh * xh, axis=-1, keepdims=True))
        dg = jnp.sum(dyv * xh, axis=0, keepdims=True)

        @pl.when(pl.program_id(0) == 0)
        def _():
            dg_ref[...] = jnp.zeros_like(dg_ref)

        dg_ref[...] += dg

    dx, dg = pl.pallas_call(
        body, name=name, grid=(T // ts,),
        in_specs=[_row_spec(ts, D), _par_spec(D), _row_spec(ts, D), _row_spec(ts, D)],
        out_specs=[_row_spec(ts, D), _par_spec(D)],
        out_shape=[jax.ShapeDtypeStruct((T, D), F32), jax.ShapeDtypeStruct((1, D), F32)],
        compiler_params=_cparams(("arbitrary",)),
    )(x, g.reshape(1, D), dy, dx_in)
    return dx, dg.reshape(D)


def rope_tables(pos, name="rope_tables"):
    T = pos.shape[0]
    half = MLA_ROPE // 2
    inv = ROPE_THETA ** (-jnp.arange(0, MLA_ROPE, 2, dtype=F32) / MLA_ROPE)
    inv_row = jnp.concatenate([inv, inv, jnp.zeros((LANES - MLA_ROPE,), F32)]).reshape(1, LANES)

    def body(p_ref, f_ref, c_ref, a_ref, b_ref):
        ang = p_ref[...].astype(F32) * f_ref[...]
        lane = lax.broadcasted_iota(jnp.int32, ang.shape, 1)
        c, s = jnp.cos(ang), jnp.sin(ang)
        c_ref[...] = jnp.where(lane < MLA_ROPE, c, 0.0)
        a_ref[...] = jnp.where(lane < half, -s, 0.0)
        b_ref[...] = jnp.where((lane >= half) & (lane < MLA_ROPE), s, 0.0)

    sh = jax.ShapeDtypeStruct((T, LANES), F32)
    return pl.pallas_call(body, name=name, out_shape=[sh, sh, sh], compiler_params=_cparams())(pos, inv_row)


def _roll_l(x):
    return pltpu.roll(x, LANES - MLA_ROPE // 2, 1)


def _roll_r(x):
    return pltpu.roll(x, MLA_ROPE // 2, 1)


def _rope(r, c, sa, sb):
    return r * c + _roll_l(r) * sa + _roll_r(r) * sb


def _rope_t(d, c, sa, sb):
    return d * c + _roll_r(d * sa) + _roll_l(d * sb)


def mla_mid_fwd(z, qn, kvn, tabs, name, ts=256):
    T = z.shape[0]
    ts = min(ts, T)
    a0, a1 = MLA_QR, MLA_QR + MLA_KVR

    def body(z_ref, qn_ref, kvn_ref, c_ref, sa_ref, sb_ref, cq_ref, ckv_ref, kr_ref):
        cq_ref[...] = _rms(z_ref[:, 0:a0], qn_ref[...]).astype(BF16)
        ckv_ref[...] = _rms(z_ref[:, a0:a1], kvn_ref[...]).astype(BF16)
        kr_ref[...] = _rope(z_ref[:, a1:MLA_ZPAD], c_ref[...], sa_ref[...], sb_ref[...]).astype(BF16)

    return pl.pallas_call(
        body, name=name, grid=(T // ts,),
        in_specs=[_row_spec(ts, MLA_ZPAD), _par_spec(MLA_QR), _par_spec(MLA_KVR)] + [_row_spec(ts, LANES)] * 3,
        out_specs=[_row_spec(ts, MLA_QR), _row_spec(ts, MLA_KVR), _row_spec(ts, LANES)],
        out_shape=[jax.ShapeDtypeStruct((T, MLA_QR), BF16), jax.ShapeDtypeStruct((T, MLA_KVR), BF16),
                   jax.ShapeDtypeStruct((T, LANES), BF16)],
        compiler_params=_cparams(("parallel",)),
    )(z, qn.reshape(1, -1), kvn.reshape(1, -1), *tabs)


def mla_mid_bwd(z, qn, kvn, tabs, dcq, dckv, dkr, name, ts=256):
    T = z.shape[0]
    ts = min(ts, T)
    a0, a1 = MLA_QR, MLA_QR + MLA_KVR

    def body(z_ref, qn_ref, kvn_ref, c_ref, sa_ref, sb_ref, dcq_ref, dckv_ref, dkr_ref, dz_ref, dqn_ref, dkvn_ref):
        _, vq = jax.vjp(_rms, z_ref[:, 0:a0], qn_ref[...])
        dzq, dqn = vq(dcq_ref[...].astype(F32))
        _, vk = jax.vjp(_rms, z_ref[:, a0:a1], kvn_ref[...])
        dzk, dkvn = vk(dckv_ref[...].astype(F32))
        dz_ref[:, 0:a0] = dzq.astype(dz_ref.dtype)
        dz_ref[:, a0:a1] = dzk.astype(dz_ref.dtype)
        dz_ref[:, a1:MLA_ZPAD] = _rope_t(dkr_ref[...].astype(F32), c_ref[...], sa_ref[...], sb_ref[...]).astype(dz_ref.dtype)

        @pl.when(pl.program_id(0) == 0)
        def _():
            dqn_ref[...] = jnp.zeros_like(dqn_ref)
            dkvn_ref[...] = jnp.zeros_like(dkvn_ref)

        dqn_ref[...] += dqn
        dkvn_ref[...] += dkvn

    dz, dqn, dkvn = pl.pallas_call(
        body, name=name, grid=(T // ts,),
        in_specs=[_row_spec(ts, MLA_ZPAD), _par_spec(MLA_QR), _par_spec(MLA_KVR)] + [_row_spec(ts, LANES)] * 3
        + [_row_spec(ts, MLA_QR), _row_spec(ts, MLA_KVR), _row_spec(ts, LANES)],
        out_specs=[_row_spec(ts, MLA_ZPAD), _par_spec(MLA_QR), _par_spec(MLA_KVR)],
        out_shape=[jax.ShapeDtypeStruct((T, MLA_ZPAD), BF16), jax.ShapeDtypeStruct((1, MLA_QR), F32),
                   jax.ShapeDtypeStruct((1, MLA_KVR), F32)],
        compiler_params=_cparams(("arbitrary",)),
    )(z, qn.reshape(1, -1), kvn.reshape(1, -1), *tabs, dcq, dckv, dkr)
    return dz, dqn.reshape(-1), dkvn.reshape(-1)


def rope_q(q, tabs, transpose, name, ts=256):
    T, W = q.shape
    ts = min(ts, T)
    fn = _rope_t if transpose else _rope
    hw = 2 * LANES

    def body(q_ref, c_ref, sa_ref, sb_ref, o_ref):
        c, sa, sb = c_ref[...], sa_ref[...], sb_ref[...]
        for h in range(W // hw):
            o_ref[:, h * hw:h * hw + LANES] = q_ref[:, h * hw:h * hw + LANES].astype(o_ref.dtype)
            o_ref[:, h * hw + LANES:(h + 1) * hw] = fn(q_ref[:, h * hw + LANES:(h + 1) * hw].astype(F32), c, sa, sb).astype(o_ref.dtype)

    return pl.pallas_call(
        body, name=name, grid=(T // ts,),
        in_specs=[_row_spec(ts, W)] + [_row_spec(ts, LANES)] * 3, out_specs=_row_spec(ts, W),
        out_shape=jax.ShapeDtypeStruct((T, W), BF16), compiler_params=_cparams(("parallel",)),
    )(q, *tabs)


def loss_head(x, g, target, name="loss_head", ts=256):
    T, D = x.shape
    ts = min(ts, T)

    def body(x_ref, g_ref, t_ref, se_ref, dx_ref, dg_ref):
        xv = x_ref[...]
        r = lax.rsqrt(jnp.mean(xv * xv, axis=-1, keepdims=True) + EPS)
        xh = xv * r
        err = xh * g_ref[...] - t_ref[...]
        dy = err * (1.0 / D)
        dxh = dy * g_ref[...]
        dx_ref[...] = r * (dxh - xh * jnp.mean(dxh * xh, axis=-1, keepdims=True))

        @pl.when(pl.program_id(0) == 0)
        def _():
            se_ref[...] = jnp.zeros_like(se_ref)
            dg_ref[...] = jnp.zeros_like(dg_ref)

        se_ref[...] += jnp.sum(err * err, axis=0, keepdims=True)
        dg_ref[...] += jnp.sum(dy * xh, axis=0, keepdims=True)

    se, dx, dg = pl.pallas_call(
        body, name=name, grid=(T // ts,),
        in_specs=[_row_spec(ts, D), _par_spec(D), _row_spec(ts, D)],
        out_specs=[_par_spec(D), _row_spec(ts, D), _par_spec(D)],
        out_shape=[jax.ShapeDtypeStruct((1, D), F32), jax.ShapeDtypeStruct((T, D), F32), jax.ShapeDtypeStruct((1, D), F32)],
        compiler_params=_cparams(("arbitrary",)),
    )(x, g.reshape(1, D), target)
    return se, dx, dg.reshape(D)


def _dot_nt(a, b):
    return lax.dot_general(a, b, (((1,), (1,)), ((), ())), preferred_element_type=F32)


def _dot_tn(a, b):
    return lax.dot_general(a, b, (((0,), (0,)), ((), ())), preferred_element_type=F32)


def _dot_nn(a, b):
    return lax.dot_general(a, b, (((1,), (0,)), ((), ())), preferred_element_type=F32)


class _Attn:
    def __init__(self, H, dq, dk1, dv, causal, scale, hp, hp_kv, blk=256):
        self.H, self.dq, self.dk1, self.dv, self.causal, self.scale, self.blk = H, dq, dk1, dv, causal, scale, blk
        self.hp, self.hp_kv = hp, hp_kv


def _cols(ref, rows, hh, width):
    return ref[rows, hh * width:(hh + 1) * width]


def _keys(cfg, k1_ref, k2_ref, rows, hh):
    ks = _cols(k1_ref, rows, hh, cfg.dk1)
    if k2_ref is not None:
        ks = jnp.concatenate([ks, k2_ref[rows, :]], axis=1)
    return ks


def _attn_specs(cfg, hp, t, Tk, has_k2, by_q):
    g = cfg.H // hp
    if by_q:
        specs = [pl.BlockSpec((t, hp * cfg.dq), lambda h, i: (i, h)),
                 pl.BlockSpec((Tk, hp * cfg.dk1), lambda h, i: (0, h)),
                 pl.BlockSpec((Tk, hp * cfg.dv), lambda h, i: (0, g + h))]
        if has_k2:
            specs.append(pl.BlockSpec((Tk, LANES), lambda h, i: (0, 0)))
    else:
        specs = [None,
                 pl.BlockSpec((t, hp * cfg.dk1), lambda j, h: (j, h)),
                 pl.BlockSpec((t, hp * cfg.dv), lambda j, h: (j, g + h))]
        if has_k2:
            specs.append(pl.BlockSpec((t, LANES), lambda j, h: (j, 0)))
    return specs


def _mask(s, cfg, i, j, t):
    if not cfg.causal:
        return s
    row = i * t + lax.broadcasted_iota(jnp.int32, s.shape, 0)
    col = j * t + lax.broadcasted_iota(jnp.int32, s.shape, 1)
    return jnp.where(row >= col, s, NEG)


def flash_fwd(cfg, q, k1, v, k2, name):
    Tq, Tk = q.shape[0], k1.shape[0]
    t = min(cfg.blk, Tq, Tk)
    nkb = Tk // t
    has_k2 = k2 is not None
    hp = cfg.hp

    def body(*refs):
        q_ref, k1_ref, v_ref = refs[:3]
        k2_ref = refs[3] if has_k2 else None
        o_ref, lse_ref = refs[-2], refs[-1]
        i = pl.program_id(1)
        qs = [_cols(q_ref, slice(None), hh, cfg.dq) for hh in range(hp)]

        def step(j, carry):
            rows = pl.ds(pl.multiple_of(j * t, t), t)
            out = []
            for hh in range(hp):
                m, l, acc = carry[hh]
                s = _mask(_dot_nt(qs[hh], _keys(cfg, k1_ref, k2_ref, rows, hh)) * cfg.scale, cfg, i, j, t)
                m2 = jnp.maximum(m, jnp.max(s, axis=-1, keepdims=True))
                p = jnp.exp(s - m2)
                alpha = jnp.exp(m - m2)
                l2 = alpha * l + jnp.sum(p, axis=-1, keepdims=True)
                acc2 = alpha * acc + _dot_nn(p.astype(BF16), _cols(v_ref, rows, hh, cfg.dv))
                out.append((m2, l2, acc2))
            return tuple(out)

        init = tuple((jnp.full((t, 1), NEG, F32), jnp.zeros((t, 1), F32), jnp.zeros((t, cfg.dv), F32)) for _ in range(hp))
        res = lax.fori_loop(0, (i + 1) if cfg.causal else nkb, step, init)
        for hh in range(hp):
            m, l, acc = res[hh]
            o_ref[:, hh * cfg.dv:(hh + 1) * cfg.dv] = (acc / l).astype(o_ref.dtype)
            lse_ref[hh] = m + jnp.log(l)

    args = [q, k1, v] + ([k2] if has_k2 else [])
    return pl.pallas_call(
        body, name=name, grid=(cfg.H // hp, Tq // t), in_specs=_attn_specs(cfg, hp, t, Tk, has_k2, True),
        out_specs=[pl.BlockSpec((t, hp * cfg.dv), lambda h, i: (i, h)), pl.BlockSpec((hp, t, 1), lambda h, i: (h, i, 0))],
        out_shape=[jax.ShapeDtypeStruct((Tq, cfg.H * cfg.dv), BF16), jax.ShapeDtypeStruct((cfg.H, Tq, 1), F32)],
        compiler_params=_cparams(("parallel", "parallel")),
    )(*args)


def flash_dq(cfg, q, k1, v, k2, o, do, lse, out_dtype, name):
    Tq, Tk = q.shape[0], k1.shape[0]
    t = min(cfg.blk, Tq, Tk)
    nkb = Tk // t
    has_k2 = k2 is not None
    hp = cfg.hp

    def body(*refs):
        q_ref, k1_ref, v_ref = refs[:3]
        k2_ref = refs[3] if has_k2 else None
        o_ref, do_ref, lse_ref, dq_ref, dl_ref = refs[-5:]
        i = pl.program_id(1)
        qs = [_cols(q_ref, slice(None), hh, cfg.dq) for hh in range(hp)]
        dos = [_cols(do_ref, slice(None), hh, cfg.dv) for hh in range(hp)]
        lses = [lse_ref[hh] for hh in range(hp)]
        deltas = []
        for hh in range(hp):
            d = jnp.sum(dos[hh].astype(F32) * _cols(o_ref, slice(None), hh, cfg.dv).astype(F32), axis=-1, keepdims=True)
            dl_ref[hh] = d
            deltas.append(d)

        def step(j, dqs):
            rows = pl.ds(pl.multiple_of(j * t, t), t)
            out = []
            for hh in range(hp):
                ks = _keys(cfg, k1_ref, k2_ref, rows, hh)
                s = _mask(_dot_nt(qs[hh], ks) * cfg.scale, cfg, i, j, t)
                p = jnp.exp(s - lses[hh])
                dp = _dot_nt(dos[hh], _cols(v_ref, rows, hh, cfg.dv))
                ds = p * (dp - deltas[hh]) * cfg.scale
                out.append(dqs[hh] + _dot_nn(ds.astype(BF16), ks))
            return tuple(out)

        dqs = lax.fori_loop(0, (i + 1) if cfg.causal else nkb, step, tuple(jnp.zeros((t, cfg.dq), F32) for _ in range(hp)))
        for hh in range(hp):
            dq_ref[:, hh * cfg.dq:(hh + 1) * cfg.dq] = dqs[hh].astype(dq_ref.dtype)

    ov = pl.BlockSpec((t, hp * cfg.dv), lambda h, i: (i, h))
    row1 = pl.BlockSpec((hp, t, 1), lambda h, i: (h, i, 0))
    args = [q, k1, v] + ([k2] if has_k2 else []) + [o, do, lse]
    return pl.pallas_call(
        body, name=name, grid=(cfg.H // hp, Tq // t), in_specs=_attn_specs(cfg, hp, t, Tk, has_k2, True) + [ov, ov, row1],
        out_specs=[pl.BlockSpec((t, hp * cfg.dq), lambda h, i: (i, h)), row1],
        out_shape=[jax.ShapeDtypeStruct((Tq, cfg.H * cfg.dq), out_dtype), jax.ShapeDtypeStruct((cfg.H, Tq, 1), F32)],
        compiler_params=_cparams(("parallel", "parallel")),
    )(*args)


def flash_dkv(cfg, q, k1, v, k2, do, lse, delta, out_dtype, name):
    Tq, Tk = q.shape[0], k1.shape[0]
    t = min(cfg.blk, Tq, Tk)
    nqb = Tq // t
    has_k2 = k2 is not None
    hp = cfg.hp_kv

    def body(*refs):
        q_ref, k1_ref, v_ref = refs[:3]
        k2_ref = refs[3] if has_k2 else None
        n_in = 4 if has_k2 else 3
        do_ref, lse_ref, dl_ref = refs[n_in:n_in + 3]
        dk1_ref, dv_ref = refs[n_in + 3], refs[n_in + 4]
        j, h = pl.program_id(0), pl.program_id(1)
        kss = [_keys(cfg, k1_ref, k2_ref, slice(None), hh) for hh in range(hp)]
        vss = [_cols(v_ref, slice(None), hh, cfg.dv) for hh in range(hp)]

        def step(i, carry):
            rows = pl.ds(pl.multiple_of(i * t, t), t)
            out = []
            for hh in range(hp):
                dk, dv = carry[hh]
                qi, doi = _cols(q_ref, rows, hh, cfg.dq), _cols(do_ref, rows, hh, cfg.dv)
                s = _mask(_dot_nt(qi, kss[hh]) * cfg.scale, cfg, i, j, t)
                p = jnp.exp(s - lse_ref[hh, rows, :])
                dv = dv + _dot_tn(p.astype(BF16), doi)
                ds = p * (_dot_nt(doi, vss[hh]) - dl_ref[hh, rows, :]) * cfg.scale
                dk = dk + _dot_tn(ds.astype(BF16), qi)
                out.append((dk, dv))
            return tuple(out)

        init = tuple((jnp.zeros((t, cfg.dq), F32), jnp.zeros((t, cfg.dv), F32)) for _ in range(hp))
        res = lax.fori_loop(j if cfg.causal else 0, nqb, step, init)
        for hh in range(hp):
            dk, dv = res[hh]
            dv_ref[:, hh * cfg.dv:(hh + 1) * cfg.dv] = dv.astype(dv_ref.dtype)
            dk1_ref[:, hh * cfg.dk1:(hh + 1) * cfg.dk1] = dk[:, 0:cfg.dk1].astype(dk1_ref.dtype)
        if has_k2:
            dk2_ref = refs[n_in + 5]

            @pl.when(h == 0)
            def _():
                dk2_ref[...] = jnp.zeros_like(dk2_ref)

            for hh in range(hp):
                dk2_ref[...] += res[hh][0][:, cfg.dk1:]

    specs = _attn_specs(cfg, hp, t, Tk, has_k2, False)
    specs[0] = pl.BlockSpec((Tq, hp * cfg.dq), lambda j, h: (0, h))
    rows_all = pl.BlockSpec((hp, Tq, 1), lambda j, h: (h, 0, 0))
    specs += [pl.BlockSpec((Tq, hp * cfg.dv), lambda j, h: (0, h)), rows_all, rows_all]
    args = [q, k1, v] + ([k2] if has_k2 else []) + [do, lse, delta]
    out_specs = [pl.BlockSpec((t, hp * cfg.dk1), lambda j, h: (j, h)), pl.BlockSpec((t, hp * cfg.dv), lambda j, h: (j, h))]
    out_shape = [jax.ShapeDtypeStruct((Tk, cfg.H * cfg.dk1), out_dtype), jax.ShapeDtypeStruct((Tk, cfg.H * cfg.dv), out_dtype)]
    if has_k2:
        out_specs.append(pl.BlockSpec((t, LANES), lambda j, h: (j, 0)))
        out_shape.append(jax.ShapeDtypeStruct((Tk, LANES), F32))
    return pl.pallas_call(
        body, name=name, grid=(Tk // t, cfg.H // hp), in_specs=specs, out_specs=out_specs, out_shape=out_shape,
        compiler_params=_cparams(("parallel", "arbitrary")),
    )(*args)


def _shift_down(x, s):
    if s == 0:
        return x
    t = lax.broadcasted_iota(jnp.int32, x.shape, 0)
    return jnp.where(t >= s, pltpu.roll(x, s, 0), 0.0)


def _shift_up(x, s):
    if s == 0:
        return x
    n = x.shape[0]
    t = lax.broadcasted_iota(jnp.int32, x.shape, 0)
    return jnp.where(t < n - s, pltpu.roll(x, n - s, 0), 0.0)


def _conv(x, w_ref, kw):
    y = x * w_ref[kw - 1:kw, :]
    for j in range(kw - 1):
        y = y + _shift_down(x, kw - 1 - j) * w_ref[j:j + 1, :]
    return y


def _conv_t(d, w_ref, kw):
    y = d * w_ref[kw - 1:kw, :]
    for j in range(kw - 1):
        y = y + _shift_up(d, kw - 1 - j) * w_ref[j:j + 1, :]
    return y


def _conv_dw(d, x, kw):
    rows = lax.broadcasted_iota(jnp.int32, (kw, d.shape[1]), 0)
    dw = jnp.zeros((kw, d.shape[1]), F32)
    for j in range(kw):
        r = jnp.sum(d * _shift_down(x, kw - 1 - j), axis=0, keepdims=True)
        dw = jnp.where(rows == j, r, dw)
    return dw


def _silu(x):
    return x * jax.nn.sigmoid(x)


def _silu_grad(x):
    s = jax.nn.sigmoid(x)
    return s * (1.0 + x * (1.0 - s))


def gdn_conv_fwd(z, w, name, tc=256):
    T, C = z.shape[0], w.shape[1]
    kw = w.shape[0]

    def body(x_ref, w_ref, o_ref):
        o_ref[...] = _silu(_conv(x_ref[...], w_ref, kw))

    return pl.pallas_call(
        body, name=name, grid=(C // tc,),
        in_specs=[pl.BlockSpec((T, tc), lambda j: (0, j)), pl.BlockSpec((kw, tc), lambda j: (0, j))],
        out_specs=pl.BlockSpec((T, tc), lambda j: (0, j)),
        out_shape=jax.ShapeDtypeStruct((T, C), F32), compiler_params=_cparams(("parallel",)),
    )(z, w)


def gdn_conv_bwd(z, w, dy, name, tc=256):
    T, C = z.shape[0], w.shape[1]
    kw = w.shape[0]

    def body(x_ref, w_ref, dy_ref, dx_ref, dw_ref):
        xv = x_ref[...]
        dc = dy_ref[...] * _silu_grad(_conv(xv, w_ref, kw))
        dx_ref[...] = _conv_t(dc, w_ref, kw).astype(dx_ref.dtype)
        dw_ref[...] = _conv_dw(dc, xv, kw)

    col = lambda j: (0, j)
    return pl.pallas_call(
        body, name=name, grid=(C // tc,),
        in_specs=[pl.BlockSpec((T, tc), col), pl.BlockSpec((kw, tc), col), pl.BlockSpec((T, tc), col)],
        out_specs=[pl.BlockSpec((T, tc), col), pl.BlockSpec((kw, tc), col)],
        out_shape=[jax.ShapeDtypeStruct((T, C), BF16), jax.ShapeDtypeStruct((kw, C), F32)],
        compiler_params=_cparams(("parallel",)),
    )(z, w, dy)


def sc_fwd(z, w, name, tc=256):
    T, C = z.shape[0], w.shape[1]
    kw, nb = w.shape[0], C // tc

    def body(b_ref, c_ref, u_ref, w_ref, o_ref):
        o_ref[...] = (b_ref[...] * _conv(c_ref[...] * u_ref[...], w_ref, kw)).astype(o_ref.dtype)

    return pl.pallas_call(
        body, name=name, grid=(nb,),
        in_specs=[pl.BlockSpec((T, tc), lambda j: (0, j)), pl.BlockSpec((T, tc), lambda j: (0, nb + j)),
                  pl.BlockSpec((T, tc), lambda j: (0, 2 * nb + j)), pl.BlockSpec((kw, tc), lambda j: (0, j))],
        out_specs=pl.BlockSpec((T, tc), lambda j: (0, j)),
        out_shape=jax.ShapeDtypeStruct((T, C), BF16), compiler_params=_cparams(("parallel",)),
    )(z, z, z, w)


def sc_bwd(z, w, dy, name, tc=256):
    T, C = z.shape[0], w.shape[1]
    kw, nb = w.shape[0], C // tc

    def body(b_ref, c_ref, u_ref, w_ref, dy_ref, db_ref, dc_ref, du_ref, dw_ref):
        cv, uv, dyv = c_ref[...], u_ref[...], dy_ref[...]
        cu = cv * uv
        db_ref[...] = (dyv * _conv(cu, w_ref, kw)).astype(db_ref.dtype)
        dcv = dyv * b_ref[...]
        dcu = _conv_t(dcv, w_ref, kw)
        dc_ref[...] = (dcu * uv).astype(dc_ref.dtype)
        du_ref[...] = (dcu * cv).astype(du_ref.dtype)
        dw_ref[...] = _conv_dw(dcv, cu, kw)

    col = lambda j: (0, j)
    act = jax.ShapeDtypeStruct((T, C), BF16)
    return pl.pallas_call(
        body, name=name, grid=(nb,),
        in_specs=[pl.BlockSpec((T, tc), col), pl.BlockSpec((T, tc), lambda j: (0, nb + j)),
                  pl.BlockSpec((T, tc), lambda j: (0, 2 * nb + j)), pl.BlockSpec((kw, tc), col), pl.BlockSpec((T, tc), col)],
        out_specs=[pl.BlockSpec((T, tc), col)] * 3 + [pl.BlockSpec((kw, tc), col)],
        out_shape=[act, act, act, jax.ShapeDtypeStruct((kw, C), F32)],
        compiler_params=_cparams(("parallel",)),
    )(z, z, z, w, dy)


def _hdot(a, b, dims):
    return lax.dot_general(a, b, (dims, ((), ())), precision=HI, preferred_element_type=F32)


def _bdot(a, b, dims):
    return lax.dot_general(a.astype(BF16), b.astype(BF16), (dims, ((), ())), preferred_element_type=F32)


_NN, _NT, _TN = ((1,), (0,)), ((1,), (1,)), ((0,), (0,))


def _per_head_dots(dot2d):
    def stacked(a, b, dims):
        return jnp.stack([dot2d(a[h], b[h], dims) for h in range(a.shape[0])])

    @jax.custom_vjp
    def nn(a, b):
        return stacked(a, b, _NN)

    @jax.custom_vjp
    def nt(a, b):
        return stacked(a, b, _NT)

    @jax.custom_vjp
    def tn(a, b):
        return stacked(a, b, _TN)

    nn.defvjp(lambda a, b: (nn(a, b), (a, b)), lambda r, d: (stacked(d, r[1], _NT), stacked(r[0], d, _TN)))
    nt.defvjp(lambda a, b: (nt(a, b), (a, b)), lambda r, d: (stacked(d, r[1], _NN), stacked(d, r[0], _TN)))
    tn.defvjp(lambda a, b: (tn(a, b), (a, b)), lambda r, d: (stacked(r[1], d, _NT), stacked(r[0], d, _NN)))
    return nn, nt, tn


_hnn, _hnt, _htn = _per_head_dots(_hdot)
_bnn, _bnt, _btn = _per_head_dots(_bdot)


@jax.custom_vjp
def _unit_lower_inverse(m):
    c = m.shape[-1]
    eye = (lax.broadcasted_iota(jnp.int32, (c, c), 0) == lax.broadcasted_iota(jnp.int32, (c, c), 1)).astype(F32)
    t = eye - m
    p = _hnn(m, m)
    n = 2
    while n < c:
        t = t + _hnn(t, p)
        n *= 2
        if n < c:
            p = _hnn(p, p)
    return t


def _uli_fwd(m):
    t = _unit_lower_inverse(m)
    return t, t


def _uli_bwd(t, dt):
    return (-_htn(t, _hnt(dt, t)),)


_unit_lower_inverse.defvjp(_uli_fwd, _uli_bwd)


def _gdn_chunk(q, k, v, gate, bl, al, a_log, dt_bias, o_norm, st):
    nh, c = q.shape[0], q.shape[1]
    ii = lax.broadcasted_iota(jnp.int32, (c, c), 0)
    jj = lax.broadcasted_iota(jnp.int32, (c, c), 1)
    tri, strict = ii >= jj, ii > jj
    q = q * lax.rsqrt(jnp.sum(q * q, -1, keepdims=True) + EPS) * (GDN_D ** -0.5)
    k = k * lax.rsqrt(jnp.sum(k * k, -1, keepdims=True) + EPS)
    beta = jax.nn.sigmoid(bl)
    g = -jnp.exp(a_log) * jax.nn.softplus(al + dt_bias)
    gc = _hnn(jnp.broadcast_to(tri.astype(F32), (nh, c, c)), g)
    gcol = _hnn(gc, jnp.full((nh, LANES, c), 1.0 / LANES, F32))
    grow = _hnt(jnp.full((nh, c, LANES), 1.0 / LANES, F32), gc)
    decay = jnp.where(tri, jnp.exp(jnp.where(tri, gcol - grow, 0.0)), 0.0)
    kb = k * beta
    m = jnp.where(strict, _bnt(kb, k) * decay, 0.0)
    t_inv = _unit_lower_inverse(m)
    eg = jnp.exp(gc)
    u = _bnn(t_inv, v * beta)
    w = _bnn(t_inv, kb * eg)
    attn = _bnt(q, k) * decay
    v_new = u - _bnn(w, st)
    o = _bnn(q * eg, st) + _bnn(attn, v_new)
    g_last = jnp.sum(g, axis=1, keepdims=True)
    st_new = st * jnp.exp(g_last) + _btn(k * jnp.exp(g_last - gc), v_new)
    o = o * lax.rsqrt(jnp.mean(o * o, -1, keepdims=True) + EPS) * o_norm
    return o * _silu(gate), st_new


GDN_HP = 8
_GW = GDN_HP * GDN_D
_GB = GDN_H // GDN_HP


def _gdn_specs(n_chunks, rev):
    def tok(col):
        if rev:
            return pl.BlockSpec((GDN_C, _GW), lambda h, n: (n_chunks - 1 - n, col + h))
        return pl.BlockSpec((GDN_C, _GW), lambda h, n: (n, col + h))
    par = pl.BlockSpec((1, _GW), lambda h, n: (0, h))
    shared = pl.BlockSpec((1, GDN_D), lambda h, n: (0, 0))
    if rev:
        st = pl.BlockSpec((GDN_HP, None, GDN_D, GDN_D), lambda h, n: (h, n_chunks - 1 - n, 0, 0))
    else:
        st = pl.BlockSpec((GDN_HP, None, GDN_D, GDN_D), lambda h, n: (h, n, 0, 0))
    return tok, par, shared, st


def _heads(ref):
    return jnp.stack([ref[:, h * GDN_D:(h + 1) * GDN_D] for h in range(ref.shape[1] // GDN_D)])


def gdn_chunk_fwd(qkv, z, a_log_x, dt_bias_x, o_norm, name):
    T = qkv.shape[0]
    n_chunks = T // GDN_C
    H = GDN_H
    tok, par, shared, st_spec = _gdn_specs(n_chunks, False)

    def body(q_ref, k_ref, v_ref, g_ref, bl_ref, al_ref, a_ref, dt_ref, on_ref, o_ref, st_ref, state):
        @pl.when(pl.program_id(1) == 0)
        def _():
            state[...] = jnp.zeros_like(state)

        st = state[...]
        st_ref[...] = st
        o, st_new = _gdn_chunk(_heads(q_ref), _heads(k_ref), _heads(v_ref), _heads(g_ref), _heads(bl_ref), _heads(al_ref),
                               _heads(a_ref), _heads(dt_ref), on_ref[...], st)
        for hh in range(GDN_HP):
            o_ref[:, hh * GDN_D:(hh + 1) * GDN_D] = o[hh].astype(o_ref.dtype)
        state[...] = st_new

    B = _GB
    return pl.pallas_call(
        body, name=name, grid=(B, n_chunks),
        in_specs=[tok(0), tok(B), tok(2 * B), tok(3 * B), tok(4 * B), tok(5 * B), par, par, shared],
        out_specs=[tok(0), st_spec],
        out_shape=[jax.ShapeDtypeStruct((T, H * GDN_D), BF16), jax.ShapeDtypeStruct((H, n_chunks, GDN_D, GDN_D), F32)],
        scratch_shapes=[pltpu.VMEM((GDN_HP, GDN_D, GDN_D), F32)],
        compiler_params=_cparams(("parallel", "arbitrary")),
    )(qkv, qkv, qkv, z, z, z, a_log_x, dt_bias_x, o_norm)


def gdn_chunk_bwd(qkv, z, a_log_x, dt_bias_x, o_norm, states, do, name):
    T = qkv.shape[0]
    n_chunks = T // GDN_C
    H = GDN_H
    tok, par, shared, st_spec = _gdn_specs(n_chunks, True)

    def body(q_ref, k_ref, v_ref, g_ref, bl_ref, al_ref, a_ref, dt_ref, on_ref, st_ref, do_ref,
             dq_ref, dk_ref, dv_ref, dg_ref, dbl_ref, dal_ref, da_ref, ddt_ref, don_ref, dstate):
        h, n = pl.program_id(0), pl.program_id(1)

        @pl.when(n == 0)
        def _():
            dstate[...] = jnp.zeros_like(dstate)
            da_ref[...] = jnp.zeros_like(da_ref)
            ddt_ref[...] = jnp.zeros_like(ddt_ref)

        @pl.when((n == 0) & (h == 0))
        def _():
            don_ref[...] = jnp.zeros_like(don_ref)

        _, vjp = jax.vjp(_gdn_chunk, _heads(q_ref), _heads(k_ref), _heads(v_ref), _heads(g_ref), _heads(bl_ref), _heads(al_ref),
                         _heads(a_ref), _heads(dt_ref), on_ref[...], st_ref[...])
        dq, dk, dv, dg, dbl, dal, da, ddt, don, dst = vjp((_heads(do_ref).astype(F32), dstate[...]))
        for hh in range(GDN_HP):
            cols = slice(hh * GDN_D, (hh + 1) * GDN_D)
            dq_ref[:, cols] = dq[hh]
            dk_ref[:, cols] = dk[hh]
            dv_ref[:, cols] = dv[hh]
            dg_ref[:, cols] = dg[hh].astype(dg_ref.dtype)
            dbl_ref[:, cols] = dbl[hh].astype(dbl_ref.dtype)
            dal_ref[:, cols] = dal[hh].astype(dal_ref.dtype)
            da_ref[:, cols] += da[hh]
            ddt_ref[:, cols] += ddt[hh]
        don_ref[...] += don
        dstate[...] = dst

    tok0 = tok(0)
    B = _GB
    f32_tok = jax.ShapeDtypeStruct((T, H * GDN_D), F32)
    bf_tok = jax.ShapeDtypeStruct((T, H * GDN_D), BF16)
    par_sh = jax.ShapeDtypeStruct((1, H * GDN_D), F32)
    return pl.pallas_call(
        body, name=name, grid=(B, n_chunks),
        in_specs=[tok(0), tok(B), tok(2 * B), tok(3 * B), tok(4 * B), tok(5 * B), par, par, shared, st_spec, tok0],
        out_specs=[tok0] * 6 + [par, par, shared],
        out_shape=[f32_tok, f32_tok, f32_tok, bf_tok, bf_tok, bf_tok, par_sh, par_sh, jax.ShapeDtypeStruct((1, GDN_D), F32)],
        scratch_shapes=[pltpu.VMEM((GDN_HP, GDN_D, GDN_D), F32)],
        compiler_params=_cparams(("arbitrary", "arbitrary")),
    )(qkv, qkv, qkv, z, z, z, a_log_x, dt_bias_x, o_norm, states, do)


def _prefetch_call(body, name, grid, in_specs, out_specs, out_shape, aliases=None):
    return pl.pallas_call(
        body, name=name,
        grid_spec=pltpu.PrefetchScalarGridSpec(num_scalar_prefetch=1, grid=grid, in_specs=in_specs, out_specs=out_specs),
        out_shape=out_shape, input_output_aliases=aliases or {},
        compiler_params=_cparams(("parallel",) * len(grid)))


def cast_into(src, slab, row0, me, name):
    rows, width = src.shape
    tr = _pick_rows(rows)
    assert row0 % tr == 0

    def body(me_ref, s_ref, *refs):
        refs[-1][...] = s_ref[...].astype(refs[-1].dtype)

    in_specs = [pl.BlockSpec((tr, width), lambda r, me_ref: (r, 0))]
    args = [src]
    aliases = {}
    if slab.arr is not None:
        in_specs.append(_ANY)
        args.append(slab.arr)
        aliases = {2: 0}
    slab.arr = _prefetch_call(
        body, name, (rows // tr,), in_specs,
        pl.BlockSpec((None, tr, width), lambda r, me_ref: (me_ref[0], row0 // tr + r, 0)),
        jax.ShapeDtypeStruct(slab.shape, slab.dtype), aliases)(me, *args)


def pair_add(g, b, c_idx, name):
    n, rh, w = b.shape
    tr = _pick_rows(rh)
    nb = rh // tr

    def body(c_ref, g_ref, b_ref, o_ref):
        o_ref[...] = (g_ref[...].astype(F32) + b_ref[...].astype(F32)).astype(o_ref.dtype)

    return _prefetch_call(
        body, name, (n, nb),
        [pl.BlockSpec((None, tr, w), lambda k, r, c: (k, c[0] * nb + r, 0)), pl.BlockSpec((None, tr, w), lambda k, r, c: (k, r, 0))],
        pl.BlockSpec((None, tr, w), lambda k, r, c: (k, r, 0)), jax.ShapeDtypeStruct(b.shape, BF16))(c_idx, g, b)


def chip_sum(p, rv, mc, name):
    n, rh, w = p.shape
    tr = _pick_rows(rh)
    nb = rh // tr

    def body(mc_ref, p_ref, rv_ref, o_ref):
        me = mc_ref[0]
        acc = None
        for k in range(n):
            part = jnp.where(me == k, p_ref[...], rv_ref[k]).astype(F32)
            acc = part if acc is None else acc + part
        o_ref[...] = acc

    return _prefetch_call(
        body, name, (nb,),
        [pl.BlockSpec((None, tr, w), lambda r, mc_ref: (mc_ref[0], r, 0)), pl.BlockSpec((n, tr, w), lambda r, mc_ref: (0, r, 0))],
        pl.BlockSpec((tr, w), lambda r, mc_ref: (mc_ref[1] * nb + r, 0)), jax.ShapeDtypeStruct((2 * rh, w), F32))(mc, p, rv)


def adamw(red, row0, w, m, v, name):
    rows, cols = w.shape
    tr = _pick_rows(rows)
    assert row0 % tr == 0

    def body(g_ref, w_ref, m_ref, v_ref, go_ref, d_ref, nm_ref, nv_ref):
        gv = g_ref[...]
        nm = ADAM_B1 * m_ref[...] + (1.0 - ADAM_B1) * gv
        nv = ADAM_B2 * v_ref[...] + (1.0 - ADAM_B2) * (gv * gv)
        m_hat = nm / (1.0 - ADAM_B1 ** ADAM_STEP)
        v_hat = nv / (1.0 - ADAM_B2 ** ADAM_STEP)
        go_ref[...] = gv
        d_ref[...] = -ADAM_LR * (m_hat / (jnp.sqrt(v_hat) + ADAM_EPS) + ADAM_WD * w_ref[...])
        nm_ref[...] = nm
        nv_ref[...] = nv

    spec = pl.BlockSpec((tr, cols), lambda r: (r, 0))
    sh = jax.ShapeDtypeStruct((rows, cols), F32)
    return pl.pallas_call(
        body, name=name, grid=(rows // tr,),
        in_specs=[pl.BlockSpec((tr, cols), lambda r: (row0 // tr + r, 0)), spec, spec, spec],
        out_specs=[spec] * 4, out_shape=[sh] * 4, compiler_params=_cparams(("parallel",)),
    )(red, w, m, v)


def _place():
    x, y, c = lax.axis_index("x"), lax.axis_index("y"), lax.axis_index("c")
    chips = [(1 - x, y), (x, 1 - y), (1 - x, 1 - y)]
    return x, y, c, chips


def _chip_index(cx, cy):
    return 2 * cx + cy


def _remote(src, dst, send_sem, recv_sem, to):
    return pltpu.make_async_remote_copy(src_ref=src, dst_ref=dst, send_sem=send_sem, recv_sem=recv_sem,
                                        device_id=to, device_id_type=MESH)


def _comm_call(body, name, ins, out_shapes, n_sems, aliases):
    return pl.pallas_call(
        body, name=name, in_specs=[_ANY] * len(ins), out_specs=[_ANY] * len(out_shapes), out_shape=out_shapes,
        scratch_shapes=[pltpu.SemaphoreType.DMA((n_sems,)), pltpu.SemaphoreType.DMA((n_sems,))],
        input_output_aliases=aliases,
    )(*ins)


def gather_slabs(slabs, name="weight_all_gather"):
    n = len(slabs)

    def body(*refs):
        in_refs, out_refs, send_sems, recv_sems = refs[:n], refs[n:2 * n], refs[-2], refs[-1]
        x, y, c, chips = _place()
        me = _chip_index(x, y)
        sib = (x, y, 1 - c)
        first, passed = [], []
        for a in range(n):
            rh = in_refs[a].shape[1] // 2
            mine = pl.ds(c * rh, rh)
            for j, chip in enumerate(chips):
                cp = _remote(in_refs[a].at[me, mine], out_refs[a].at[me, mine], send_sems.at[6 * a + j],
                             recv_sems.at[6 * a + j], (*chip, c))
                cp.start()
                first.append(cp)
        for a in range(n):
            rh = in_refs[a].shape[1] // 2
            mine = pl.ds(c * rh, rh)
            for j, chip in enumerate(chips):
                landed = out_refs[a].at[_chip_index(*chip), mine]
                _remote(landed, landed, send_sems.at[6 * a + j], recv_sems.at[6 * a + j], (*chip, c)).wait_recv()
                cp = _remote(landed, landed, send_sems.at[6 * a + 3 + j], recv_sems.at[6 * a + 3 + j], sib)
                cp.start()
                passed.append(cp)
        for a in range(n):
            rh = in_refs[a].shape[1] // 2
            theirs = pl.ds((1 - c) * rh, rh)
            for j, chip in enumerate(chips):
                got = out_refs[a].at[_chip_index(*chip), theirs]
                _remote(got, got, send_sems.at[6 * a + 3 + j], recv_sems.at[6 * a + 3 + j], sib).wait_recv()
        for cp in first + passed:
            cp.wait_send()

    return _comm_call(body, name, slabs, [jax.ShapeDtypeStruct(s.shape, s.dtype) for s in slabs], 6 * n,
                      {a: a for a in range(n)})


def pair_swap_halves(slabs, name="grad_pair_swap"):
    n = len(slabs)

    def body(*refs):
        in_refs, out_refs, send_sems, recv_sems = refs[:n], refs[n:2 * n], refs[-2], refs[-1]
        x, y, c, _ = _place()
        cps = []
        for a in range(n):
            rh = in_refs[a].shape[1] // 2
            cp = _remote(in_refs[a].at[:, pl.ds((1 - c) * rh, rh), :], out_refs[a], send_sems.at[a], recv_sems.at[a], (x, y, 1 - c))
            cp.start()
            cps.append(cp)
        for cp in cps:
            cp.wait()

    outs = [jax.ShapeDtypeStruct((s.shape[0], s.shape[1] // 2, s.shape[2]), s.dtype) for s in slabs]
    return _comm_call(body, name, slabs, outs, n, {})


def chip_exchange(parts, name="grad_chip_exchange"):
    n = len(parts)

    def body(*refs):
        in_refs, out_refs, send_sems, recv_sems = refs[:n], refs[n:2 * n], refs[-2], refs[-1]
        x, y, c, chips = _place()
        me = _chip_index(x, y)
        sends = []
        for a in range(n):
            for j, chip in enumerate(chips):
                cp = _remote(in_refs[a].at[_chip_index(*chip)], out_refs[a].at[me], send_sems.at[3 * a + j],
                             recv_sems.at[3 * a + j], (*chip, c))
                cp.start()
                sends.append(cp)
        for a in range(n):
            for j, chip in enumerate(chips):
                got = out_refs[a].at[_chip_index(*chip)]
                _remote(got, got, send_sems.at[3 * a + j], recv_sems.at[3 * a + j], (*chip, c)).wait_recv()
        for cp in sends:
            cp.wait_send()

    return _comm_call(body, name, parts, [jax.ShapeDtypeStruct(p.shape, p.dtype) for p in parts], 3 * n, {})


def pair_join_halves(reds, name="grad_pair_join"):
    n = len(reds)

    def body(*refs):
        in_refs, out_refs, send_sems, recv_sems = refs[:n], refs[n:2 * n], refs[-2], refs[-1]
        x, y, c, _ = _place()
        cps = []
        for a in range(n):
            rh = in_refs[a].shape[0] // 2
            mine = pl.ds(c * rh, rh)
            cp = _remote(in_refs[a].at[mine], out_refs[a].at[mine], send_sems.at[a], recv_sems.at[a], (x, y, 1 - c))
            cp.start()
            cps.append(cp)
        for a in range(n):
            rh = in_refs[a].shape[0] // 2
            got = out_refs[a].at[pl.ds((1 - c) * rh, rh)]
            _remote(got, got, send_sems.at[a], recv_sems.at[a], (x, y, 1 - c)).wait_recv()
        for cp in cps:
            cp.wait_send()

    return _comm_call(body, name, reds, [jax.ShapeDtypeStruct(r.shape, r.dtype) for r in reds], n, {a: a for a in range(n)})


_SLABS = {
    "w1024": [("mlp_w1", 2), ("mlp_w2", 1), ("xa_w_q", 1), ("xa_w_o", 1), ("mla_w_o", 1), ("gdn_w_o", 1), ("sc_w_o", 1)],
    "xa_w_kv": [("xa_w_kv", 2)], "sc_w_in": [("sc_w_in", 2)],
    "mla_w_in": [("mla_w_in", 1)], "mla_w_uq": [("mla_w_uq", 2)], "mla_w_ukv": [("mla_w_ukv", 2)], "gdn_w_in": [("gdn_w_in", 2)],
}
_RELAID = ("mla_w_in", "mla_w_uq", "mla_w_ukv", "gdn_w_in")
_SMALL = [("mla_q_norm", 1), ("mla_kv_norm", 1), ("gdn_conv_w", 2), ("sc_conv_w", 2)]
_REPL = ["gdn_a_log", "gdn_dt_bias", "gdn_o_norm", "norm_mix", "norm_mem", "norm_mlp", "mem_norm", "final_norm"]
_WEIGHTS = ['mla_w_in', 'mla_q_norm', 'mla_kv_norm', 'mla_w_uq', 'mla_w_ukv', 'mla_w_o', 'gdn_w_in', 'gdn_conv_w',
            'gdn_a_log', 'gdn_dt_bias', 'gdn_o_norm', 'gdn_w_o', 'sc_w_in', 'sc_conv_w', 'sc_w_o', 'norm_mix',
            'norm_mem', 'norm_mlp', 'xa_w_q', 'xa_w_kv', 'xa_w_o', 'mlp_w1', 'mlp_w2', 'mem_norm', 'final_norm']


class Layout:
    def __init__(self, shard_shapes):
        self.where, self.slab_dims = {}, {}
        for slab, members in _SLABS.items():
            off = 0
            for name, axis in members:
                layers, rpl, width = shard_shapes[name]
                self.where[name] = (slab, off, layers, rpl, width, axis)
                off += layers * rpl
            self.slab_dims[slab] = (off, width)

    def new_slabs(self, dtype):
        return {s: Slab(rows, width, dtype) for s, (rows, width) in self.slab_dims.items()}

    def loc(self, slabs, name, layer):
        slab, off, _, rpl, width, axis = self.where[name]
        if axis == 1:
            return Loc(slabs[slab], off + layer * rpl, N_CHIPS * rpl, width, 0)
        return Loc(slabs[slab], off + layer * rpl, rpl, N_CHIPS * width, 1)

    def full(self, slabs, name):
        slab, off, layers, rpl, width, axis = self.where[name]
        blocks = slabs[slab].arr[:, off:off + layers * rpl].reshape(N_CHIPS, layers, rpl, width)
        return jnp.concatenate([blocks[s] for s in range(N_CHIPS)], axis=axis)

    def put_full(self, slabs, name, grad):
        slab, off, layers, rpl, width, axis = self.where[name]
        assert off == 0 and len(_SLABS[slab]) == 1
        parts = jnp.stack(jnp.split(grad, N_CHIPS, axis=axis)).reshape(N_CHIPS, layers * rpl, width)
        slabs[slab].arr = parts.astype(slabs[slab].dtype)


def _small_pack(vals, names):
    flat = jnp.concatenate([vals[n].astype(F32).reshape(-1) for n in names])
    return jnp.pad(flat, (0, SMALL_ROWS * SMALL_COLS - flat.shape[0])).reshape(SMALL_ROWS, SMALL_COLS)


def _small_unpack(flat, like, names):
    out, off = {}, 0
    flat = flat.reshape(-1)
    for n in names:
        out[n] = flat[off:off + like[n].size].reshape(like[n].shape)
        off += like[n].size
    return out


_MLA_CFG = _Attn(MLA_H, 2 * LANES, MLA_NOPE, MLA_V, True, (MLA_NOPE + MLA_ROPE) ** -0.5, hp=8, hp_kv=4)
_XA_CFG = _Attn(XA_H, XA_D, XA_D, XA_D, False, XA_D ** -0.5, hp=4, hp_kv=4)


def _mla_weights(w_in, w_uq, w_ukv):
    w_in_p = jnp.pad(w_in, ((0, 0), (0, MLA_ZPAD - w_in.shape[1])))
    w_uq_p = jnp.pad(w_uq.reshape(MLA_QR, MLA_H, MLA_NOPE + MLA_ROPE), ((0, 0), (0, 0), (0, 2 * LANES - MLA_NOPE - MLA_ROPE)))
    w_uq_p = w_uq_p.reshape(MLA_QR, MLA_H * 2 * LANES)
    kv = w_ukv.reshape(MLA_KVR, MLA_H, MLA_NOPE + MLA_V)
    w_ukv_p = jnp.concatenate([kv[:, :, :MLA_NOPE].reshape(MLA_KVR, -1), kv[:, :, MLA_NOPE:].reshape(MLA_KVR, -1)], axis=1)
    return w_in_p, w_uq_p, w_ukv_p


def _mla_weight_grads(d_in_p, d_uq_p, d_ukv_p):
    d_in = d_in_p[:, :MLA_QR + MLA_KVR + MLA_ROPE]
    d_uq = d_uq_p.reshape(MLA_QR, MLA_H, 2 * LANES)[:, :, :MLA_NOPE + MLA_ROPE].reshape(MLA_QR, -1)
    half = MLA_H * MLA_NOPE
    d_ukv = jnp.concatenate([d_ukv_p[:, :half].reshape(MLA_KVR, MLA_H, MLA_NOPE),
                             d_ukv_p[:, half:].reshape(MLA_KVR, MLA_H, MLA_V)], axis=2).reshape(MLA_KVR, -1)
    return d_in, d_uq, d_ukv


def _mla_fwd(xs, h, wts, w_o, qn, kvn, tabs, tag):
    w_in_p, w_uq_p, w_ukv_p = wts
    z = mm(h, w_in_p, "nn", f"{tag}_in")
    cq, ckv, kr = mla_mid_fwd(z, qn, kvn, tabs, f"{tag}_mid")
    q = rope_q(mm(cq, w_uq_p, "nn", f"{tag}_uq"), tabs, False, f"{tag}_ropeq")
    kv = mm(ckv, w_ukv_p, "nn", f"{tag}_ukv", outs=(BF16,))
    o, lse = flash_fwd(_MLA_CFG, q, kv, kv, kr, f"{tag}_attn")
    xs = mm(o, w_o, "nn", f"{tag}_out", epi=_epi_add, extras=(xs,))
    return xs, (z, cq, ckv, kr, q, kv, o, lse)


def _mla_bwd(dx, h, wts, w_o, g_wo, qn, kvn, tabs, saved, tag):
    w_in_p, w_uq_p, w_ukv_p = wts
    z, cq, ckv, kr, q, kv, o, lse = saved
    mm(o, dx, "tn", f"{tag}_dwo", outs=(BF16,), out_loc=g_wo)
    do = mm(dx, w_o, "nt", f"{tag}_do", outs=(BF16,))
    dq, delta = flash_dq(_MLA_CFG, q, kv, kv, kr, o, do, lse, F32, f"{tag}_attn_dq")
    dk1, dv, dkr = flash_dkv(_MLA_CFG, q, kv, kv, kr, do, lse, delta, BF16, f"{tag}_attn_dkv")
    dqp = rope_q(dq, tabs, True, f"{tag}_ropeq_t")
    d_uq_p = mm(cq, dqp, "tn", f"{tag}_duq")
    dcq = mm(dqp, w_uq_p, "nt", f"{tag}_dcq")
    dkv = jnp.concatenate([dk1, dv], axis=1)
    d_ukv_p = mm(ckv, dkv, "tn", f"{tag}_dukv")
    dckv = mm(dkv, w_ukv_p, "nt", f"{tag}_dckv")
    dz, dqn, dkvn = mla_mid_bwd(z, qn, kvn, tabs, dcq, dckv, dkr, f"{tag}_mid_bwd")
    d_in_p = mm(h, dz, "tn", f"{tag}_din")
    dh = mm(dz, w_in_p, "nt", f"{tag}_dh")
    d_in, d_uq, d_ukv = _mla_weight_grads(d_in_p, d_uq_p, d_ukv_p)
    return dh, dict(mla_w_in=d_in, mla_w_uq=d_uq, mla_w_ukv=d_ukv, mla_q_norm=dqn, mla_kv_norm=dkvn)


_GDN_QKV = 3 * GDN_H * GDN_D
_GDN_GATE_END = _GDN_QKV + GDN_H * GDN_D


def _gdn_weights(w_in):
    rep = lambda cols: jnp.repeat(cols, GDN_D, axis=1)
    return jnp.concatenate([w_in[:, :_GDN_GATE_END], rep(w_in[:, _GDN_GATE_END:_GDN_GATE_END + GDN_H]),
                            rep(w_in[:, _GDN_GATE_END + GDN_H:])], axis=1)


def _fold(x):
    return x.reshape(x.shape[0], -1, GDN_D).sum(-1)


def _gdn_fwd(xs, h, w_in_x, conv_w, a_log, dt_bias, o_norm, w_o, tag):
    z = mm(h, w_in_x, "nn", f"{tag}_in")
    qkv = gdn_conv_fwd(z, conv_w, f"{tag}_conv")
    a_x, dt_x = jnp.repeat(a_log.reshape(1, -1), GDN_D, axis=1), jnp.repeat(dt_bias.reshape(1, -1), GDN_D, axis=1)
    og, states = gdn_chunk_fwd(qkv, z, a_x, dt_x, o_norm.reshape(1, -1), f"{tag}_chunks")
    xs = mm(og, w_o, "nn", f"{tag}_out", epi=_epi_add, extras=(xs,))
    return xs, (z, qkv, a_x, dt_x, og, states)


def _gdn_bwd(dx, h, w_in_x, conv_w, o_norm, w_o, g_wo, saved, tag):
    z, qkv, a_x, dt_x, og, states = saved
    mm(og, dx, "tn", f"{tag}_dwo", outs=(BF16,), out_loc=g_wo)
    dog = mm(dx, w_o, "nt", f"{tag}_dog")
    dq, dk, dv, dgate, dbl, dal, da_x, ddt_x, don = gdn_chunk_bwd(qkv, z, a_x, dt_x, o_norm.reshape(1, -1), states, dog,
                                                                  f"{tag}_chunks_bwd")
    dpre, dconv = gdn_conv_bwd(z, conv_w, jnp.concatenate([dq, dk, dv], axis=1), f"{tag}_conv_bwd")
    dz = jnp.concatenate([dpre, dgate, dbl, dal], axis=1)
    d_in_x = mm(h, dz, "tn", f"{tag}_din")
    dh = mm(dz, w_in_x, "nt", f"{tag}_dh", tn=512)
    ge = _GDN_GATE_END
    d_in = jnp.concatenate([d_in_x[:, :ge], _fold(d_in_x[:, ge:ge + GDN_H * GDN_D]), _fold(d_in_x[:, ge + GDN_H * GDN_D:])], axis=1)
    return dh, dict(gdn_w_in=d_in, gdn_conv_w=dconv, gdn_a_log=_fold(da_x).reshape(-1), gdn_dt_bias=_fold(ddt_x).reshape(-1),
                    gdn_o_norm=don.reshape(-1))


def _sc_fwd(xs, h, w_in, conv_w, w_o, tag):
    z = mm(h, w_in, "nn", f"{tag}_in")
    y = sc_fwd(z, conv_w, f"{tag}_conv")
    xs = mm(y, w_o, "nn", f"{tag}_out", epi=_epi_add, extras=(xs,))
    return xs, (z, y)


def _sc_bwd(dx, h, w_in, g_win, conv_w, w_o, g_wo, saved, tag):
    z, y = saved
    mm(y, dx, "tn", f"{tag}_dwo", outs=(BF16,), out_loc=g_wo)
    dy = mm(dx, w_o, "nt", f"{tag}_dy")
    db, dc, du, dconv = sc_bwd(z, conv_w, dy, f"{tag}_conv_bwd")
    dz = jnp.concatenate([db, dc, du], axis=1)
    mm(h, dz, "tn", f"{tag}_din", outs=(BF16,), out_loc=g_win)
    dh = mm(dz, w_in, "nt", f"{tag}_dh")
    return dh, dict(sc_conv_w=dconv)


def local_step(x, mem, pos, target, lay, wslabs, gslabs, small):
    depth = small["norm_mix"].shape[0]
    W = lambda name, layer: lay.loc(wslabs, name, layer)
    G = lambda name, layer: lay.loc(gslabs, name, layer)
    tabs = rope_tables(pos)
    mem_n = rmsnorm_fwd(mem, small["mem_norm"], "mem_norm")
    full = {n: lay.full(wslabs, n) for n in _RELAID}
    mla_w = [_mla_weights(full["mla_w_in"][j], full["mla_w_uq"][j], full["mla_w_ukv"][j]) for j in range(full["mla_w_in"].shape[0])]
    gdn_in_x = [_gdn_weights(full["gdn_w_in"][j]) for j in range(full["gdn_w_in"].shape[0])]

    xs = x
    saved = []
    for i in range(depth):
        j, kind = i // 3, i % 3
        tag = f"l{i}"
        x_a = xs
        h = rmsnorm_fwd(xs, small["norm_mix"][i], f"{tag}_norm_mix")
        if kind == 0:
            xs, mix = _mla_fwd(xs, h, mla_w[j], W("mla_w_o", j), small["mla_q_norm"][j], small["mla_kv_norm"][j], tabs, f"{tag}_mla")
        elif kind == 1:
            xs, mix = _gdn_fwd(xs, h, gdn_in_x[j], small["gdn_conv_w"][j], small["gdn_a_log"][j], small["gdn_dt_bias"][j],
                               small["gdn_o_norm"][j], W("gdn_w_o", j), f"{tag}_gdn")
        else:
            xs, mix = _sc_fwd(xs, h, W("sc_w_in", j), small["sc_conv_w"][j], W("sc_w_o", j), f"{tag}_sc")
        x_b = xs
        hn = rmsnorm_fwd(xs, small["norm_mem"][i], f"{tag}_norm_mem")
        xq = mm(hn, W("xa_w_q", i), "nn", f"{tag}_xa_q", outs=(BF16,))
        xkv = mm(mem_n, W("xa_w_kv", i), "nn", f"{tag}_xa_kv", outs=(BF16,))
        xo, xlse = flash_fwd(_XA_CFG, xq, xkv, xkv, None, f"{tag}_xa_attn")
        xs = mm(xo, W("xa_w_o", i), "nn", f"{tag}_xa_out", epi=_epi_add, extras=(xs,))
        x_c = xs
        hm = rmsnorm_fwd(xs, small["norm_mlp"][i], f"{tag}_norm_mlp")
        h1, act = mm(hm, W("mlp_w1", i), "nn", f"{tag}_mlp_up", outs=(BF16, BF16), epi=_epi_relu2)
        xs = mm(act, W("mlp_w2", i), "nn", f"{tag}_mlp_down", epi=_epi_add, extras=(xs,))
        saved.append((x_a, h, mix, x_b, hn, xq, xkv, xo, xlse, x_c, hm, h1, act))

    se, dx, d_final = loss_head(xs, small["final_norm"], target)

    per_layer = {n: [None] * depth for n in ("norm_mix", "norm_mem", "norm_mlp")}
    mixer = {}
    dmem_n = jnp.zeros(mem.shape, F32)
    for i in reversed(range(depth)):
        j, kind = i // 3, i % 3
        tag = f"l{i}"
        x_a, h, mix, x_b, hn, xq, xkv, xo, xlse, x_c, hm, h1, act = saved[i]
        mm(act, dx, "tn", f"{tag}_mlp_dw2", outs=(BF16,), out_loc=G("mlp_w2", i))
        dh1 = mm(dx, W("mlp_w2", i), "nt", f"{tag}_mlp_dh1", outs=(BF16,), epi=_epi_relu2_bwd, extras=(h1,))
        mm(hm, dh1, "tn", f"{tag}_mlp_dw1", outs=(BF16,), out_loc=G("mlp_w1", i))
        dhm = mm(dh1, W("mlp_w1", i), "nt", f"{tag}_mlp_dhm")
        dx, per_layer["norm_mlp"][i] = rmsnorm_bwd(x_c, small["norm_mlp"][i], dhm, dx, f"{tag}_norm_mlp_bwd")
        mm(xo, dx, "tn", f"{tag}_xa_dwo", outs=(BF16,), out_loc=G("xa_w_o", i))
        dxo = mm(dx, W("xa_w_o", i), "nt", f"{tag}_xa_do", outs=(BF16,))
        dxq, xdelta = flash_dq(_XA_CFG, xq, xkv, xkv, None, xo, dxo, xlse, BF16, f"{tag}_xa_attn_dq")
        dxk, dxv = flash_dkv(_XA_CFG, xq, xkv, xkv, None, dxo, xlse, xdelta, BF16, f"{tag}_xa_attn_dkv")
        dxkv = jnp.concatenate([dxk, dxv], axis=1)
        mm(hn, dxq, "tn", f"{tag}_xa_dwq", outs=(BF16,), out_loc=G("xa_w_q", i))
        dhn = mm(dxq, W("xa_w_q", i), "nt", f"{tag}_xa_dhn")
        mm(mem_n, dxkv, "tn", f"{tag}_xa_dwkv", outs=(BF16,), out_loc=G("xa_w_kv", i))
        dmem_n = mm(dxkv, W("xa_w_kv", i), "nt", f"{tag}_xa_dmem", epi=_epi_add, extras=(dmem_n,))
        dx, per_layer["norm_mem"][i] = rmsnorm_bwd(x_b, small["norm_mem"][i], dhn, dx, f"{tag}_norm_mem_bwd")
        if kind == 0:
            dh, gr = _mla_bwd(dx, h, mla_w[j], W("mla_w_o", j), G("mla_w_o", j), small["mla_q_norm"][j], small["mla_kv_norm"][j],
                              tabs, mix, f"{tag}_mla")
        elif kind == 1:
            dh, gr = _gdn_bwd(dx, h, gdn_in_x[j], small["gdn_conv_w"][j], small["gdn_o_norm"][j], W("gdn_w_o", j), G("gdn_w_o", j),
                              mix, f"{tag}_gdn")
        else:
            dh, gr = _sc_bwd(dx, h, W("sc_w_in", j), G("sc_w_in", j), small["sc_conv_w"][j], W("sc_w_o", j), G("sc_w_o", j),
                             mix, f"{tag}_sc")
        for n, g in gr.items():
            mixer.setdefault(n, {})[j] = g
        dx, per_layer["norm_mix"][i] = rmsnorm_bwd(x_a, small["norm_mix"][i], dh, dx, f"{tag}_norm_mix_bwd")

    _, d_mem_norm = rmsnorm_bwd(mem, small["mem_norm"], dmem_n, jnp.zeros(mem.shape, F32), "mem_norm_bwd")
    grads = {n: jnp.stack(v) for n, v in per_layer.items()}
    for n, by_j in mixer.items():
        grads[n] = jnp.stack([by_j[j] for j in sorted(by_j)])
    grads["mem_norm"] = d_mem_norm
    grads["final_norm"] = d_final
    for n in _RELAID:
        lay.put_full(gslabs, n, grads.pop(n))
    return se, dx, grads


def kernel(x, mem, positions, mla_w_in, mla_q_norm, mla_kv_norm, mla_w_uq, mla_w_ukv, mla_w_o, gdn_w_in, gdn_conv_w, gdn_a_log, gdn_dt_bias, gdn_o_norm, gdn_w_o, sc_w_in, sc_conv_w, sc_w_o, norm_mix, norm_mem, norm_mlp, xa_w_q, xa_w_kv, xa_w_o, mlp_w1, mlp_w2, mem_norm, final_norm, loss_target, m_mla_w_in, m_mla_q_norm, m_mla_kv_norm, m_mla_w_uq, m_mla_w_ukv, m_mla_w_o, m_gdn_w_in, m_gdn_conv_w, m_gdn_a_log, m_gdn_dt_bias, m_gdn_o_norm, m_gdn_w_o, m_sc_w_in, m_sc_conv_w, m_sc_w_o, m_norm_mix, m_norm_mem, m_norm_mlp, m_xa_w_q, m_xa_w_kv, m_xa_w_o, m_mlp_w1, m_mlp_w2, m_mem_norm, m_final_norm, v_mla_w_in, v_mla_q_norm, v_mla_kv_norm, v_mla_w_uq, v_mla_w_ukv, v_mla_w_o, v_gdn_w_in, v_gdn_conv_w, v_gdn_a_log, v_gdn_dt_bias, v_gdn_o_norm, v_gdn_w_o, v_sc_w_in, v_sc_conv_w, v_sc_w_o, v_norm_mix, v_norm_mem, v_norm_mlp, v_xa_w_q, v_xa_w_kv, v_xa_w_o, v_mlp_w1, v_mlp_w2, v_mem_norm, v_final_norm):
    given = dict(locals())
    p = {n: given[n] for n in _WEIGHTS}
    mom = {n: given["m_" + n] for n in _WEIGHTS}
    var = {n: given["v_" + n] for n in _WEIGHTS}
    split = [n for members in _SLABS.values() for n, _ in members]
    lay = Layout({n: p[n].shape for n in split})
    flat2d = lambda a: a.reshape(-1, a.shape[-1])

    me = (2 * lax.axis_index("x") + lax.axis_index("y")).astype(jnp.int32)
    core = lax.axis_index("c").astype(jnp.int32)
    me1, c1, mc = me.reshape(1), core.reshape(1), jnp.stack([me, core])

    wslabs = lay.new_slabs(BF16)
    for n in split:
        slab, off = lay.where[n][0], lay.where[n][1]
        cast_into(flat2d(p[n]), wslabs[slab], off, me1, f"cast_{n}")
    small_names = [n for n, _ in _SMALL]
    words = lax.bitcast_convert_type(jnp.concatenate([p[n].reshape(-1) for n in small_names]), BF16).reshape(-1)
    words = jnp.pad(words, (0, SMALL_ROWS * SMALL_COLS - words.shape[0])).reshape(1, SMALL_ROWS, SMALL_COLS)
    small_slab = lax.dynamic_update_slice(jnp.zeros((N_CHIPS, SMALL_ROWS, SMALL_COLS), BF16), words, (me, 0, 0))
    order = list(_SLABS)
    gathered = gather_slabs([wslabs[s].arr for s in order] + [small_slab])
    for s, arr in zip(order, gathered):
        wslabs[s].arr = arr
    small = {n: p[n] for n in _REPL}
    got, off = gathered[-1].reshape(N_CHIPS, -1), 0
    for n, ax in _SMALL:
        vals = lax.bitcast_convert_type(got[:, off:off + 2 * p[n].size].reshape(N_CHIPS, p[n].size, 2), F32)
        vals = vals.reshape((N_CHIPS,) + p[n].shape)
        small[n] = jnp.concatenate([vals[s] for s in range(N_CHIPS)], axis=ax)
        off += 2 * p[n].size

    gslabs = lay.new_slabs(BF16)
    se, dx, sgrads = local_step(x[0], mem[0], positions.reshape(-1, 1), loss_target[0], lay, wslabs, gslabs, small)
    loss = lax.psum(0.5 * jnp.sum(se) / x.shape[-1], ("x", "y", "c"))

    axes = dict(_SMALL)
    small_order = small_names + _REPL
    slots = []
    for s in range(N_CHIPS):
        vals = {n: (lax.slice_in_dim(g, s * p[n].shape[axes[n]], (s + 1) * p[n].shape[axes[n]], axis=axes[n]) if n in axes else g)
                for n, g in sgrads.items()}
        slots.append(_small_pack(vals, small_order))
    g_list = [gslabs[s].arr for s in order] + [jnp.stack(slots).astype(BF16)]

    swapped = pair_swap_halves(g_list)
    names = order + ["small"]
    partial = [pair_add(g, b, c1, f"pair_add_{s}") for g, b, s in zip(g_list, swapped, names)]
    received = chip_exchange(partial)
    halves = [chip_sum(q, r, mc, f"chip_sum_{s}") for q, r, s in zip(partial, received, names)]
    reduced = dict(zip(names, pair_join_halves(halves)))

    res = {}
    for n in split:
        slab, off = lay.where[n][0], lay.where[n][1]
        outs = adamw(reduced[slab], off, flat2d(p[n]), flat2d(mom[n]), flat2d(var[n]), f"adamw_{n}")
        res[n] = [o.reshape(p[n].shape) for o in outs]
    outs = adamw(reduced["small"], 0, _small_pack(p, small_order), _small_pack(mom, small_order), _small_pack(var, small_order),
                 "adamw_small")
    unpacked = [_small_unpack(o, p, small_order) for o in outs]
    for n in small_order:
        res[n] = [u[n] for u in unpacked]
    return (loss, dx[None], *[res[n][k] for k in range(4) for n in _WEIGHTS])
```

```python
import jax
import jax.numpy as jnp
from jax import lax
from jax.experimental import pallas as pl
from jax.experimental.pallas import tpu as pltpu

F32 = jnp.float32
BF16 = jnp.bfloat16
HI = lax.Precision.HIGHEST
MESH = pl.DeviceIdType.MESH

EPS = 1e-6
ROPE_THETA = 10000.0
N_CHIPS = 4
LANES = 128
VMEM_LIMIT = 56 * 1024 * 1024
NEG = -1e30

MLA_H, MLA_NOPE, MLA_ROPE, MLA_V = 8, 128, 64, 128
MLA_QR, MLA_KVR = 384, 256
MLA_ZPAD = 768
GDN_H, GDN_D, GDN_C = 8, 128, 64
XA_H, XA_D = 4, 256

ADAM_LR, ADAM_B1, ADAM_B2, ADAM_EPS, ADAM_WD, ADAM_STEP = 0.001, 0.9, 0.999, 1e-08, 0.01, 10

SMALL_ROWS, SMALL_COLS = 32, 1024


def _cparams(sem=None):
    return pltpu.CompilerParams(dimension_semantics=sem, vmem_limit_bytes=VMEM_LIMIT)


def _pick(dim, pref):
    t = (min(pref, dim) // LANES) * LANES
    while t >= LANES:
        if dim % t == 0:
            return t
        t -= LANES
    return dim


def _pick_rows(rows, pref=256):
    t = pref
    while rows % t:
        t //= 2
    return t


class Slab:
    def __init__(self, rows, width, dtype, arr=None):
        self.shape, self.dtype, self.arr = (N_CHIPS, rows, width), dtype, arr


class Loc:
    def __init__(self, slab, row0, K, N, axis):
        self.slab, self.row0, self.K, self.N, self.axis = slab, row0, K, N, axis
        self.Ks = K // N_CHIPS if axis == 0 else K
        self.Ns = N // N_CHIPS if axis == 1 else N

    def tile_spec(self, tr, tc, rc):
        assert self.row0 % tr == 0 and self.Ks % tr == 0 and self.Ns % tc == 0, (self.row0, self.Ks, self.Ns, tr, tc)
        r0, rb, cb = self.row0 // tr, self.Ks // tr, self.Ns // tc
        if self.axis == 0:
            return pl.BlockSpec((None, tr, tc), lambda i, j: (rc(i, j)[0] // rb, r0 + rc(i, j)[0] % rb, rc(i, j)[1]))
        return pl.BlockSpec((None, tr, tc), lambda i, j: (rc(i, j)[1] // cb, r0 + rc(i, j)[0], rc(i, j)[1] % cb))

    def slot_spec(self, slot, tr, tc, rc):
        assert self.row0 % tr == 0, (self.row0, tr)
        r0 = self.row0 // tr
        return pl.BlockSpec((None, tr, tc), lambda i, j: (slot, r0 + rc(i, j)[0], rc(i, j)[1]))


_DIMS = {"nn": ((1,), (0,)), "nt": ((1,), (1,)), "tn": ((0,), (0,))}
_ANY = pl.BlockSpec(memory_space=pl.ANY)


def mm(a, b, mode, name, outs=(F32,), epi=None, extras=(), tm=1024, tn=1024, out_loc=None):
    b_loc = b if isinstance(b, Loc) else None
    if mode == "nn":
        M, K = a.shape
        K2, N = (b_loc.K, b_loc.N) if b_loc else b.shape
    elif mode == "nt":
        M, K = a.shape
        N, K2 = (b_loc.K, b_loc.N) if b_loc else b.shape
    else:
        K, M = a.shape
        K2, N = b.shape
    assert K == K2, (name, a.shape, K2, N)
    tm = _pick(out_loc.Ks if (out_loc and out_loc.axis == 0) else M, tm)
    if out_loc is not None and out_loc.axis == 1:
        tn = _pick(out_loc.Ns, tn)
    elif b_loc is not None and ((mode == "nn" and b_loc.axis == 1) or (mode == "nt" and b_loc.axis == 0)):
        tn = _pick(b_loc.Ns if mode == "nn" else b_loc.Ks, tn)
    elif b_loc is not None:
        tn = _pick(N, min(tn, 512))
    else:
        tn = _pick(N, tn)

    parts = 1
    if mode == "tn":
        a_spec = pl.BlockSpec((K, tm), lambda i, j: (0, i))
        b_specs, b_args = [pl.BlockSpec((K, tn), lambda i, j: (0, j))], [b]
    else:
        a_spec = pl.BlockSpec((tm, K), lambda i, j: (i, 0))
        if b_loc is None:
            b_specs = [pl.BlockSpec((K, tn), lambda i, j: (0, j)) if mode == "nn" else pl.BlockSpec((tn, K), lambda i, j: (j, 0))]
            b_args = [b]
        elif mode == "nn" and b_loc.axis == 1:
            b_specs, b_args = [b_loc.tile_spec(K, tn, lambda i, j: (0, j))], [b_loc.slab.arr]
        elif mode == "nt" and b_loc.axis == 0:
            b_specs, b_args = [b_loc.tile_spec(tn, K, lambda i, j: (j, 0))], [b_loc.slab.arr]
        elif mode == "nn":
            parts = N_CHIPS
            b_specs = [b_loc.slot_spec(s, b_loc.Ks, tn, lambda i, j: (0, j)) for s in range(parts)]
            b_args = [b_loc.slab.arr] * parts
        else:
            parts = N_CHIPS
            b_specs = [b_loc.slot_spec(s, tn, b_loc.Ns, lambda i, j: (j, 0)) for s in range(parts)]
            b_args = [b_loc.slab.arr] * parts
    kp = K // parts
    n_ex, n_out = len(extras), len(outs)
    dims = (_DIMS[mode], ((), ()))

    def body(*refs):
        a_ref = refs[0]
        b_refs = refs[1:1 + parts]
        ex_refs = refs[1 + parts:1 + parts + n_ex]
        o_refs = refs[-n_out:]
        acc = None
        for s in range(parts):
            av = a_ref[...] if parts == 1 else a_ref[:, s * kp:(s + 1) * kp]
            d = lax.dot_general(av.astype(BF16), b_refs[s][...].astype(BF16), dims, preferred_element_type=F32)
            acc = d if acc is None else acc + d
        res = epi(acc, *[e[...] for e in ex_refs]) if epi is not None else (acc,)
        for o_ref, v in zip(o_refs, res):
            o_ref[...] = v.astype(o_ref.dtype)

    mn_spec = pl.BlockSpec((tm, tn), lambda i, j: (i, j))
    in_specs = [a_spec] + b_specs + [mn_spec] * n_ex
    args = [a] + b_args + list(extras)
    aliases = {}
    if out_loc is None:
        out_specs = [mn_spec] * n_out
        out_shape = [jax.ShapeDtypeStruct((M, N), d) for d in outs]
    else:
        assert n_out == 1 and mode == "tn"
        out_specs = [out_loc.tile_spec(tm, tn, lambda i, j: (i, j))]
        out_shape = [jax.ShapeDtypeStruct(out_loc.slab.shape, out_loc.slab.dtype)]
        if out_loc.slab.arr is not None:
            in_specs.append(_ANY)
            args.append(out_loc.slab.arr)
            aliases = {len(args) - 1: 0}

    res = pl.pallas_call(
        body, name=name, grid=(M // tm, N // tn), in_specs=in_specs, out_specs=out_specs, out_shape=out_shape,
        input_output_aliases=aliases, compiler_params=_cparams(("parallel", "parallel")),
    )(*args)
    if out_loc is not None:
        out_loc.slab.arr = res[0]
        return None
    return res[0] if n_out == 1 else tuple(res)


def _epi_add(acc, r):
    return (acc + r,)


def _epi_relu2(acc):
    r = jnp.maximum(acc, 0.0)
    return acc, r * r


def _epi_relu2_bwd(acc, h1):
    return (acc * (2.0 * jnp.maximum(h1.astype(F32), 0.0)),)


def _rms(x, g):
    return x * lax.rsqrt(jnp.mean(x * x, axis=-1, keepdims=True) + EPS) * g


def _row_spec(ts, cols):
    return pl.BlockSpec((ts, cols), lambda i: (i, 0))


def _par_spec(cols):
    return pl.BlockSpec((1, cols), lambda i: (0, 0))


def rmsnorm_fwd(x, g, name, ts=256):
    T, D = x.shape
    ts = min(ts, T)

    def body(x_ref, g_ref, o_ref):
        o_ref[...] = _rms(x_ref[...], g_ref[...]).astype(o_ref.dtype)

    return pl.pallas_call(
        body, name=name, grid=(T // ts,),
        in_specs=[_row_spec(ts, D), _par_spec(D)], out_specs=_row_spec(ts, D),
        out_shape=jax.ShapeDtypeStruct((T, D), BF16), compiler_params=_cparams(("parallel",)),
    )(x, g.reshape(1, D))


def rmsnorm_bwd(x, g, dy, dx_in, name, ts=256):
    T, D = x.shape
    ts = min(ts, T)

    def body(x_ref, g_ref, dy_ref, dxi_ref, dx_ref, dg_ref):
        xv = x_ref[...]
        r = lax.rsqrt(jnp.mean(xv * xv, axis=-1, keepdims=True) + EPS)
        xh = xv * r
        dyv = dy_ref[...].astype(F32)
        dxh = dyv * g_ref[...]
        dx_ref[...] = dxi_ref[...] + r * (dxh - xh * jnp.mean(dxh * xh, axis=-1, keepdims=True))
        dg = jnp.sum(dyv * xh, axis=0, keepdims=True)

        @pl.when(pl.program_id(0) == 0)
        def _():
            dg_ref[...] = jnp.zeros_like(dg_ref)

        dg_ref[...] += dg

    dx, dg = pl.pallas_call(
        body, name=name, grid=(T // ts,),
        in_specs=[_row_spec(ts, D), _par_spec(D), _row_spec(ts, D), _row_spec(ts, D)],
        out_specs=[_row_spec(ts, D), _par_spec(D)],
        out_shape=[jax.ShapeDtypeStruct((T, D), F32), jax.ShapeDtypeStruct((1, D), F32)],
        compiler_params=_cparams(("arbitrary",)),
    )(x, g.reshape(1, D), dy, dx_in)
    return dx, dg.reshape(D)


def rope_tables(pos, name="rope_tables"):
    T = pos.shape[0]
    half = MLA_ROPE // 2
    inv = ROPE_THETA ** (-jnp.arange(0, MLA_ROPE, 2, dtype=F32) / MLA_ROPE)
    inv_row = jnp.concatenate([inv, inv, jnp.zeros((LANES - MLA_ROPE,), F32)]).reshape(1, LANES)

    def body(p_ref, f_ref, c_ref, a_ref, b_ref):
        ang = p_ref[...].astype(F32) * f_ref[...]
        lane = lax.broadcasted_iota(jnp.int32, ang.shape, 1)
        c, s = jnp.cos(ang), jnp.sin(ang)
        c_ref[...] = jnp.where(lane < MLA_ROPE, c, 0.0)
        a_ref[...] = jnp.where(lane < half, -s, 0.0)
        b_ref[...] = jnp.where((lane >= half) & (lane < MLA_ROPE), s, 0.0)

    sh = jax.ShapeDtypeStruct((T, LANES), F32)
    return pl.pallas_call(body, name=name, out_shape=[sh, sh, sh], compiler_params=_cparams())(pos, inv_row)


def _roll_l(x):
    return pltpu.roll(x, LANES - MLA_ROPE // 2, 1)


def _roll_r(x):
    return pltpu.roll(x, MLA_ROPE // 2, 1)


def _rope(r, c, sa, sb):
    return r * c + _roll_l(r) * sa + _roll_r(r) * sb


def _rope_t(d, c, sa, sb):
    return d * c + _roll_r(d * sa) + _roll_l(d * sb)


def mla_mid_fwd(z, qn, kvn, tabs, name, ts=256):
    T = z.shape[0]
    ts = min(ts, T)
    a0, a1 = MLA_QR, MLA_QR + MLA_KVR

    def body(z_ref, qn_ref, kvn_ref, c_ref, sa_ref, sb_ref, cq_ref, ckv_ref, kr_ref):
        cq_ref[...] = _rms(z_ref[:, 0:a0], qn_ref[...]).astype(BF16)
        ckv_ref[...] = _rms(z_ref[:, a0:a1], kvn_ref[...]).astype(BF16)
        kr_ref[...] = _rope(z_ref[:, a1:MLA_ZPAD], c_ref[...], sa_ref[...], sb_ref[...]).astype(BF16)

    return pl.pallas_call(
        body, name=name, grid=(T // ts,),
        in_specs=[_row_spec(ts, MLA_ZPAD), _par_spec(MLA_QR), _par_spec(MLA_KVR)] + [_row_spec(ts, LANES)] * 3,
        out_specs=[_row_spec(ts, MLA_QR), _row_spec(ts, MLA_KVR), _row_spec(ts, LANES)],
        out_shape=[jax.ShapeDtypeStruct((T, MLA_QR), BF16), jax.ShapeDtypeStruct((T, MLA_KVR), BF16),
                   jax.ShapeDtypeStruct((T, LANES), BF16)],
        compiler_params=_cparams(("parallel",)),
    )(z, qn.reshape(1, -1), kvn.reshape(1, -1), *tabs)


def mla_mid_bwd(z, qn, kvn, tabs, dcq, dckv, dkr, name, ts=256):
    T = z.shape[0]
    ts = min(ts, T)
    a0, a1 = MLA_QR, MLA_QR + MLA_KVR

    def body(z_ref, qn_ref, kvn_ref, c_ref, sa_ref, sb_ref, dcq_ref, dckv_ref, dkr_ref, dz_ref, dqn_ref, dkvn_ref):
        _, vq = jax.vjp(_rms, z_ref[:, 0:a0], qn_ref[...])
        dzq, dqn = vq(dcq_ref[...].astype(F32))
        _, vk = jax.vjp(_rms, z_ref[:, a0:a1], kvn_ref[...])
        dzk, dkvn = vk(dckv_ref[...].astype(F32))
        dz_ref[:, 0:a0] = dzq.astype(dz_ref.dtype)
        dz_ref[:, a0:a1] = dzk.astype(dz_ref.dtype)
        dz_ref[:, a1:MLA_ZPAD] = _rope_t(dkr_ref[...].astype(F32), c_ref[...], sa_ref[...], sb_ref[...]).astype(dz_ref.dtype)

        @pl.when(pl.program_id(0) == 0)
        def _():
            dqn_ref[...] = jnp.zeros_like(dqn_ref)
            dkvn_ref[...] = jnp.zeros_like(dkvn_ref)

        dqn_ref[...] += dqn
        dkvn_ref[...] += dkvn

    dz, dqn, dkvn = pl.pallas_call(
        body, name=name, grid=(T // ts,),
        in_specs=[_row_spec(ts, MLA_ZPAD), _par_spec(MLA_QR), _par_spec(MLA_KVR)] + [_row_spec(ts, LANES)] * 3
        + [_row_spec(ts, MLA_QR), _row_spec(ts, MLA_KVR), _row_spec(ts, LANES)],
        out_specs=[_row_spec(ts, MLA_ZPAD), _par_spec(MLA_QR), _par_spec(MLA_KVR)],
        out_shape=[jax.ShapeDtypeStruct((T, MLA_ZPAD), BF16), jax.ShapeDtypeStruct((1, MLA_QR), F32),
                   jax.ShapeDtypeStruct((1, MLA_KVR), F32)],
        compiler_params=_cparams(("arbitrary",)),
    )(z, qn.reshape(1, -1), kvn.reshape(1, -1), *tabs, dcq, dckv, dkr)
    return dz, dqn.reshape(-1), dkvn.reshape(-1)


def rope_q(q, tabs, transpose, name, ts=256):
    T, W = q.shape
    ts = min(ts, T)
    fn = _rope_t if transpose else _rope
    hw = 2 * LANES

    def body(q_ref, c_ref, sa_ref, sb_ref, o_ref):
        c, sa, sb = c_ref[...], sa_ref[...], sb_ref[...]
        for h in range(W // hw):
            o_ref[:, h * hw:h * hw + LANES] = q_ref[:, h * hw:h * hw + LANES].astype(o_ref.dtype)
            o_ref[:, h * hw + LANES:(h + 1) * hw] = fn(q_ref[:, h * hw + LANES:(h + 1) * hw].astype(F32), c, sa, sb).astype(o_ref.dtype)

    return pl.pallas_call(
        body, name=name, grid=(T // ts,),
        in_specs=[_row_spec(ts, W)] + [_row_spec(ts, LANES)] * 3, out_specs=_row_spec(ts, W),
        out_shape=jax.ShapeDtypeStruct((T, W), BF16), compiler_params=_cparams(("parallel",)),
    )(q, *tabs)


def loss_head(x, g, target, name="loss_head", ts=256):
    T, D = x.shape
    ts = min(ts, T)

    def body(x_ref, g_ref, t_ref, se_ref, dx_ref, dg_ref):
        xv = x_ref[...]
        r = lax.rsqrt(jnp.mean(xv * xv, axis=-1, keepdims=True) + EPS)
        xh = xv * r
        err = xh * g_ref[...] - t_ref[...]
        dy = err * (1.0 / D)
        dxh = dy * g_ref[...]
        dx_ref[...] = r * (dxh - xh * jnp.mean(dxh * xh, axis=-1, keepdims=True))

        @pl.when(pl.program_id(0) == 0)
        def _():
            se_ref[...] = jnp.zeros_like(se_ref)
            dg_ref[...] = jnp.zeros_like(dg_ref)

        se_ref[...] += jnp.sum(err * err, axis=0, keepdims=True)
        dg_ref[...] += jnp.sum(dy * xh, axis=0, keepdims=True)

    se, dx, dg = pl.pallas_call(
        body, name=name, grid=(T // ts,),
        in_specs=[_row_spec(ts, D), _par_spec(D), _row_spec(ts, D)],
        out_specs=[_par_spec(D), _row_spec(ts, D), _par_spec(D)],
        out_shape=[jax.ShapeDtypeStruct((1, D), F32), jax.ShapeDtypeStruct((T, D), F32), jax.ShapeDtypeStruct((1, D), F32)],
        compiler_params=_cparams(("arbitrary",)),
    )(x, g.reshape(1, D), target)
    return se, dx, dg.reshape(D)


def _dot_nt(a, b):
    return lax.dot_general(a, b, (((1,), (1,)), ((), ())), preferred_element_type=F32)


def _dot_tn(a, b):
    return lax.dot_general(a, b, (((0,), (0,)), ((), ())), preferred_element_type=F32)


def _dot_nn(a, b):
    return lax.dot_general(a, b, (((1,), (0,)), ((), ())), preferred_element_type=F32)


class _Attn:
    def __init__(self, H, dq, dk1, dv, causal, scale, hp, hp_kv, blk=256):
        self.H, self.dq, self.dk1, self.dv, self.causal, self.scale, self.blk = H, dq, dk1, dv, causal, scale, blk
        self.hp, self.hp_kv = hp, hp_kv


def _cols(ref, rows, hh, width):
    return ref[rows, hh * width:(hh + 1) * width]


def _keys(cfg, k1_ref, k2_ref, rows, hh):
    ks = _cols(k1_ref, rows, hh, cfg.dk1)
    if k2_ref is not None:
        ks = jnp.concatenate([ks, k2_ref[rows, :]], axis=1)
    return ks


def _attn_specs(cfg, hp, t, Tk, has_k2, by_q):
    g = cfg.H // hp
    if by_q:
        specs = [pl.BlockSpec((t, hp * cfg.dq), lambda h, i: (i, h)),
                 pl.BlockSpec((Tk, hp * cfg.dk1), lambda h, i: (0, h)),
                 pl.BlockSpec((Tk, hp * cfg.dv), lambda h, i: (0, g + h))]
        if has_k2:
            specs.append(pl.BlockSpec((Tk, LANES), lambda h, i: (0, 0)))
    else:
        specs = [None,
                 pl.BlockSpec((t, hp * cfg.dk1), lambda j, h: (j, h)),
                 pl.BlockSpec((t, hp * cfg.dv), lambda j, h: (j, g + h))]
        if has_k2:
            specs.append(pl.BlockSpec((t, LANES), lambda j, h: (j, 0)))
    return specs


def _mask(s, cfg, i, j, t):
    if not cfg.causal:
        return s
    row = i * t + lax.broadcasted_iota(jnp.int32, s.shape, 0)
    col = j * t + lax.broadcasted_iota(jnp.int32, s.shape, 1)
    return jnp.where(row >= col, s, NEG)


def flash_fwd(cfg, q, k1, v, k2, name):
    Tq, Tk = q.shape[0], k1.shape[0]
    t = min(cfg.blk, Tq, Tk)
    nkb = Tk // t
    has_k2 = k2 is not None
    hp = cfg.hp

    def body(*refs):
        q_ref, k1_ref, v_ref = refs[:3]
        k2_ref = refs[3] if has_k2 else None
        o_ref, lse_ref = refs[-2], refs[-1]
        i = pl.program_id(1)
        qs = [_cols(q_ref, slice(None), hh, cfg.dq) for hh in range(hp)]

        def step(j, carry):
            rows = pl.ds(pl.multiple_of(j * t, t), t)
            out = []
            for hh in range(hp):
                m, l, acc = carry[hh]
                s = _mask(_dot_nt(qs[hh], _keys(cfg, k1_ref, k2_ref, rows, hh)) * cfg.scale, cfg, i, j, t)
                m2 = jnp.maximum(m, jnp.max(s, axis=-1, keepdims=True))
                p = jnp.exp(s - m2)
                alpha = jnp.exp(m - m2)
                l2 = alpha * l + jnp.sum(p, axis=-1, keepdims=True)
                acc2 = alpha * acc + _dot_nn(p.astype(BF16), _cols(v_ref, rows, hh, cfg.dv))
                out.append((m2, l2, acc2))
            return tuple(out)

        init = tuple((jnp.full((t, 1), NEG, F32), jnp.zeros((t, 1), F32), jnp.zeros((t, cfg.dv), F32)) for _ in range(hp))
        res = lax.fori_loop(0, (i + 1) if cfg.causal else nkb, step, init)
        for hh in range(hp):
            m, l, acc = res[hh]
            o_ref[:, hh * cfg.dv:(hh + 1) * cfg.dv] = (acc / l).astype(o_ref.dtype)
            lse_ref[hh] = m + jnp.log(l)

    args = [q, k1, v] + ([k2] if has_k2 else [])
    return pl.pallas_call(
        body, name=name, grid=(cfg.H // hp, Tq // t), in_specs=_attn_specs(cfg, hp, t, Tk, has_k2, True),
        out_specs=[pl.BlockSpec((t, hp * cfg.dv), lambda h, i: (i, h)), pl.BlockSpec((hp, t, 1), lambda h, i: (h, i, 0))],
        out_shape=[jax.ShapeDtypeStruct((Tq, cfg.H * cfg.dv), BF16), jax.ShapeDtypeStruct((cfg.H, Tq, 1), F32)],
        compiler_params=_cparams(("parallel", "parallel")),
    )(*args)


def flash_dq(cfg, q, k1, v, k2, o, do, lse, out_dtype, name):
    Tq, Tk = q.shape[0], k1.shape[0]
    t = min(cfg.blk, Tq, Tk)
    nkb = Tk // t
    has_k2 = k2 is not None
    hp = cfg.hp

    def body(*refs):
        q_ref, k1_ref, v_ref = refs[:3]
        k2_ref = refs[3] if has_k2 else None
        o_ref, do_ref, lse_ref, dq_ref, dl_ref = refs[-5:]
        i = pl.program_id(1)
        qs = [_cols(q_ref, slice(None), hh, cfg.dq) for hh in range(hp)]
        dos = [_cols(do_ref, slice(None), hh, cfg.dv) for hh in range(hp)]
        lses = [lse_ref[hh] for hh in range(hp)]
        deltas = []
        for hh in range(hp):
            d = jnp.sum(dos[hh].astype(F32) * _cols(o_ref, slice(None), hh, cfg.dv).astype(F32), axis=-1, keepdims=True)
            dl_ref[hh] = d
            deltas.append(d)

        def step(j, dqs):
            rows = pl.ds(pl.multiple_of(j * t, t), t)
            out = []
            for hh in range(hp):
                ks = _keys(cfg, k1_ref, k2_ref, rows, hh)
                s = _mask(_dot_nt(qs[hh], ks) * cfg.scale, cfg, i, j, t)
                p = jnp.exp(s - lses[hh])
                dp = _dot_nt(dos[hh], _cols(v_ref, rows, hh, cfg.dv))
                ds = p * (dp - deltas[hh]) * cfg.scale
                out.append(dqs[hh] + _dot_nn(ds.astype(BF16), ks))
            return tuple(out)

        dqs = lax.fori_loop(0, (i + 1) if cfg.causal else nkb, step, tuple(jnp.zeros((t, cfg.dq), F32) for _ in range(hp)))
        for hh in range(hp):
            dq_ref[:, hh * cfg.dq:(hh + 1) * cfg.dq] = dqs[hh].astype(dq_ref.dtype)

    ov = pl.BlockSpec((t, hp * cfg.dv), lambda h, i: (i, h))
    row1 = pl.BlockSpec((hp, t, 1), lambda h, i: (h, i, 0))
    args = [q, k1, v] + ([k2] if has_k2 else []) + [o, do, lse]
    return pl.pallas_call(
        body, name=name, grid=(cfg.H // hp, Tq // t), in_specs=_attn_specs(cfg, hp, t, Tk, has_k2, True) + [ov, ov, row1],
        out_specs=[pl.BlockSpec((t, hp * cfg.dq), lambda h, i: (i, h)), row1],
        out_shape=[jax.ShapeDtypeStruct((Tq, cfg.H * cfg.dq), out_dtype), jax.ShapeDtypeStruct((cfg.H, Tq, 1), F32)],
        compiler_params=_cparams(("parallel", "parallel")),
    )(*args)


def flash_dkv(cfg, q, k1, v, k2, do, lse, delta, out_dtype, name):
    Tq, Tk = q.shape[0], k1.shape[0]
    t = min(cfg.blk, Tq, Tk)
    nqb = Tq // t
    has_k2 = k2 is not None
    hp = cfg.hp_kv

    def body(*refs):
        q_ref, k1_ref, v_ref = refs[:3]
        k2_ref = refs[3] if has_k2 else None
        n_in = 4 if has_k2 else 3
        do_ref, lse_ref, dl_ref = refs[n_in:n_in + 3]
        dk1_ref, dv_ref = refs[n_in + 3], refs[n_in + 4]
        j, h = pl.program_id(0), pl.program_id(1)
        kss = [_keys(cfg, k1_ref, k2_ref, slice(None), hh) for hh in range(hp)]
        vss = [_cols(v_ref, slice(None), hh, cfg.dv) for hh in range(hp)]

        def step(i, carry):
            rows = pl.ds(pl.multiple_of(i * t, t), t)
            out = []
            for hh in range(hp):
                dk, dv = carry[hh]
                qi, doi = _cols(q_ref, rows, hh, cfg.dq), _cols(do_ref, rows, hh, cfg.dv)
                s = _mask(_dot_nt(qi, kss[hh]) * cfg.scale, cfg, i, j, t)
                p = jnp.exp(s - lse_ref[hh, rows, :])
                dv = dv + _dot_tn(p.astype(BF16), doi)
                ds = p * (_dot_nt(doi, vss[hh]) - dl_ref[hh, rows, :]) * cfg.scale
                dk = dk + _dot_tn(ds.astype(BF16), qi)
                out.append((dk, dv))
            return tuple(out)

        init = tuple((jnp.zeros((t, cfg.dq), F32), jnp.zeros((t, cfg.dv), F32)) for _ in range(hp))
        res = lax.fori_loop(j if cfg.causal else 0, nqb, step, init)
        for hh in range(hp):
            dk, dv = res[hh]
            dv_ref[:, hh * cfg.dv:(hh + 1) * cfg.dv] = dv.astype(dv_ref.dtype)
            dk1_ref[:, hh * cfg.dk1:(hh + 1) * cfg.dk1] = dk[:, 0:cfg.dk1].astype(dk1_ref.dtype)
        if has_k2:
            dk2_ref = refs[n_in + 5]

            @pl.when(h == 0)
            def _():
                dk2_ref[...] = jnp.zeros_like(dk2_ref)

            for hh in range(hp):
                dk2_ref[...] += res[hh][0][:, cfg.dk1:]

    specs = _attn_specs(cfg, hp, t, Tk, has_k2, False)
    specs[0] = pl.BlockSpec((Tq, hp * cfg.dq), lambda j, h: (0, h))
    rows_all = pl.BlockSpec((hp, Tq, 1), lambda j, h: (h, 0, 0))
    specs += [pl.BlockSpec((Tq, hp * cfg.dv), lambda j, h: (0, h)), rows_all, rows_all]
    args = [q, k1, v] + ([k2] if has_k2 else []) + [do, lse, delta]
    out_specs = [pl.BlockSpec((t, hp * cfg.dk1), lambda j, h: (j, h)), pl.BlockSpec((t, hp * cfg.dv), lambda j, h: (j, h))]
    out_shape = [jax.ShapeDtypeStruct((Tk, cfg.H * cfg.dk1), out_dtype), jax.ShapeDtypeStruct((Tk, cfg.H * cfg.dv), out_dtype)]
    if has_k2:
        out_specs.append(pl.BlockSpec((t, LANES), lambda j, h: (j, 0)))
        out_shape.append(jax.ShapeDtypeStruct((Tk, LANES), F32))
    return pl.pallas_call(
        body, name=name, grid=(Tk // t, cfg.H // hp), in_specs=specs, out_specs=out_specs, out_shape=out_shape,
        compiler_params=_cparams(("parallel", "arbitrary")),
    )(*args)


def _shift_down(x, s):
    if s == 0:
        return x
    t = lax.broadcasted_iota(jnp.int32, x.shape, 0)
    return jnp.where(t >= s, pltpu.roll(x, s, 0), 0.0)


def _shift_up(x, s):
    if s == 0:
        return x
    n = x.shape[0]
    t = lax.broadcasted_iota(jnp.int32, x.shape, 0)
    return jnp.where(t < n - s, pltpu.roll(x, n - s, 0), 0.0)


def _conv(x, w_ref, kw):
    y = x * w_ref[kw - 1:kw, :]
    for j in range(kw - 1):
        y = y + _shift_down(x, kw - 1 - j) * w_ref[j:j + 1, :]
    return y


def _conv_t(d, w_ref, kw):
    y = d * w_ref[kw - 1:kw, :]
    for j in range(kw - 1):
        y = y + _shift_up(d, kw - 1 - j) * w_ref[j:j + 1, :]
    return y


def _conv_dw(d, x, kw):
    rows = lax.broadcasted_iota(jnp.int32, (kw, d.shape[1]), 0)
    dw = jnp.zeros((kw, d.shape[1]), F32)
    for j in range(kw):
        r = jnp.sum(d * _shift_down(x, kw - 1 - j), axis=0, keepdims=True)
        dw = jnp.where(rows == j, r, dw)
    return dw


def _silu(x):
    return x * jax.nn.sigmoid(x)


def _silu_grad(x):
    s = jax.nn.sigmoid(x)
    return s * (1.0 + x * (1.0 - s))


def gdn_conv_fwd(z, w, name, tc=256):
    T, C = z.shape[0], w.shape[1]
    kw = w.shape[0]

    def body(x_ref, w_ref, o_ref):
        o_ref[...] = _silu(_conv(x_ref[...], w_ref, kw))

    return pl.pallas_call(
        body, name=name, grid=(C // tc,),
        in_specs=[pl.BlockSpec((T, tc), lambda j: (0, j)), pl.BlockSpec((kw, tc), lambda j: (0, j))],
        out_specs=pl.BlockSpec((T, tc), lambda j: (0, j)),
        out_shape=jax.ShapeDtypeStruct((T, C), F32), compiler_params=_cparams(("parallel",)),
    )(z, w)


def gdn_conv_bwd(z, w, dy, name, tc=256):
    T, C = z.shape[0], w.shape[1]
    kw = w.shape[0]

    def body(x_ref, w_ref, dy_ref, dx_ref, dw_ref):
        xv = x_ref[...]
        dc = dy_ref[...] * _silu_grad(_conv(xv, w_ref, kw))
        dx_ref[...] = _conv_t(dc, w_ref, kw).astype(dx_ref.dtype)
        dw_ref[...] = _conv_dw(dc, xv, kw)

    col = lambda j: (0, j)
    return pl.pallas_call(
        body, name=name, grid=(C // tc,),
        in_specs=[pl.BlockSpec((T, tc), col), pl.BlockSpec((kw, tc), col), pl.BlockSpec((T, tc), col)],
        out_specs=[pl.BlockSpec((T, tc), col), pl.BlockSpec((kw, tc), col)],
        out_shape=[jax.ShapeDtypeStruct((T, C), BF16), jax.ShapeDtypeStruct((kw, C), F32)],
        compiler_params=_cparams(("parallel",)),
    )(z, w, dy)


def sc_fwd(z, w, name, tc=256):
    T, C = z.shape[0], w.shape[1]
    kw, nb = w.shape[0], C // tc

    def body(b_ref, c_ref, u_ref, w_ref, o_ref):
        o_ref[...] = (b_ref[...] * _conv(c_ref[...] * u_ref[...], w_ref, kw)).astype(o_ref.dtype)

    return pl.pallas_call(
        body, name=name, grid=(nb,),
        in_specs=[pl.BlockSpec((T, tc), lambda j: (0, j)), pl.BlockSpec((T, tc), lambda j: (0, nb + j)),
                  pl.BlockSpec((T, tc), lambda j: (0, 2 * nb + j)), pl.BlockSpec((kw, tc), lambda j: (0, j))],
        out_specs=pl.BlockSpec((T, tc), lambda j: (0, j)),
        out_shape=jax.ShapeDtypeStruct((T, C), BF16), compiler_params=_cparams(("parallel",)),
    )(z, z, z, w)


def sc_bwd(z, w, dy, name, tc=256):
    T, C = z.shape[0], w.shape[1]
    kw, nb = w.shape[0], C // tc

    def body(b_ref, c_ref, u_ref, w_ref, dy_ref, db_ref, dc_ref, du_ref, dw_ref):
        cv, uv, dyv = c_ref[...], u_ref[...], dy_ref[...]
        cu = cv * uv
        db_ref[...] = (dyv * _conv(cu, w_ref, kw)).astype(db_ref.dtype)
        dcv = dyv * b_ref[...]
        dcu = _conv_t(dcv, w_ref, kw)
        dc_ref[...] = (dcu * uv).astype(dc_ref.dtype)
        du_ref[...] = (dcu * cv).astype(du_ref.dtype)
        dw_ref[...] = _conv_dw(dcv, cu, kw)

    col = lambda j: (0, j)
    act = jax.ShapeDtypeStruct((T, C), BF16)
    return pl.pallas_call(
        body, name=name, grid=(nb,),
        in_specs=[pl.BlockSpec((T, tc), col), pl.BlockSpec((T, tc), lambda j: (0, nb + j)),
                  pl.BlockSpec((T, tc), lambda j: (0, 2 * nb + j)), pl.BlockSpec((kw, tc), col), pl.BlockSpec((T, tc), col)],
        out_specs=[pl.BlockSpec((T, tc), col)] * 3 + [pl.BlockSpec((kw, tc), col)],
        out_shape=[act, act, act, jax.ShapeDtypeStruct((kw, C), F32)],
        compiler_params=_cparams(("parallel",)),
    )(z, z, z, w, dy)


def _hdot(a, b, dims):
    return lax.dot_general(a, b, (dims, ((), ())), precision=HI, preferred_element_type=F32)


def _bdot(a, b, dims):
    return lax.dot_general(a.astype(BF16), b.astype(BF16), (dims, ((), ())), preferred_element_type=F32)


_NN, _NT, _TN = ((1,), (0,)), ((1,), (1,)), ((0,), (0,))


def _per_head_dots(dot2d):
    def stacked(a, b, dims):
        return jnp.stack([dot2d(a[h], b[h], dims) for h in range(a.shape[0])])

    @jax.custom_vjp
    def nn(a, b):
        return stacked(a, b, _NN)

    @jax.custom_vjp
    def nt(a, b):
        return stacked(a, b, _NT)

    @jax.custom_vjp
    def tn(a, b):
        return stacked(a, b, _TN)

    nn.defvjp(lambda a, b: (nn(a, b), (a, b)), lambda r, d: (stacked(d, r[1], _NT), stacked(r[0], d, _TN)))
    nt.defvjp(lambda a, b: (nt(a, b), (a, b)), lambda r, d: (stacked(d, r[1], _NN), stacked(d, r[0], _TN)))
    tn.defvjp(lambda a, b: (tn(a, b), (a, b)), lambda r, d: (stacked(r[1], d, _NT), stacked(r[0], d, _NN)))
    return nn, nt, tn


_hnn, _hnt, _htn = _per_head_dots(_hdot)
_bnn, _bnt, _btn = _per_head_dots(_bdot)


@jax.custom_vjp
def _unit_lower_inverse(m):
    c = m.shape[-1]
    eye = (lax.broadcasted_iota(jnp.int32, (c, c), 0) == lax.broadcasted_iota(jnp.int32, (c, c), 1)).astype(F32)
    t = eye - m
    p = _hnn(m, m)
    n = 2
    while n < c:
        t = t + _hnn(t, p)
        n *= 2
        if n < c:
            p = _hnn(p, p)
    return t


def _uli_fwd(m):
    t = _unit_lower_inverse(m)
    return t, t


def _uli_bwd(t, dt):
    return (-_htn(t, _hnt(dt, t)),)


_unit_lower_inverse.defvjp(_uli_fwd, _uli_bwd)


def _gdn_chunk(q, k, v, gate, bl, al, a_log, dt_bias, o_norm, st):
    nh, c = q.shape[0], q.shape[1]
    ii = lax.broadcasted_iota(jnp.int32, (c, c), 0)
    jj = lax.broadcasted_iota(jnp.int32, (c, c), 1)
    tri, strict = ii >= jj, ii > jj
    q = q * lax.rsqrt(jnp.sum(q * q, -1, keepdims=True) + EPS) * (GDN_D ** -0.5)
    k = k * lax.rsqrt(jnp.sum(k * k, -1, keepdims=True) + EPS)
    beta = jax.nn.sigmoid(bl)
    g = -jnp.exp(a_log) * jax.nn.softplus(al + dt_bias)
    gc = _hnn(jnp.broadcast_to(tri.astype(F32), (nh, c, c)), g)
    gcol = _hnn(gc, jnp.full((nh, LANES, c), 1.0 / LANES, F32))
    grow = _hnt(jnp.full((nh, c, LANES), 1.0 / LANES, F32), gc)
    decay = jnp.where(tri, jnp.exp(jnp.where(tri, gcol - grow, 0.0)), 0.0)
    kb = k * beta
    m = jnp.where(strict, _bnt(kb, k) * decay, 0.0)
    t_inv = _unit_lower_inverse(m)
    eg = jnp.exp(gc)
    u = _bnn(t_inv, v * beta)
    w = _bnn(t_inv, kb * eg)
    attn = _bnt(q, k) * decay
    v_new = u - _bnn(w, st)
    o = _bnn(q * eg, st) + _bnn(attn, v_new)
    g_last = jnp.sum(g, axis=1, keepdims=True)
    st_new = st * jnp.exp(g_last) + _btn(k * jnp.exp(g_last - gc), v_new)
    o = o * lax.rsqrt(jnp.mean(o * o, -1, keepdims=True) + EPS) * o_norm
    return o * _silu(gate), st_new


GDN_HP = 8
_GW = GDN_HP * GDN_D
_GB = GDN_H // GDN_HP


def _gdn_specs(n_chunks, rev):
    def tok(col):
        if rev:
            return pl.BlockSpec((GDN_C, _GW), lambda h, n: (n_chunks - 1 - n, col + h))
        return pl.BlockSpec((GDN_C, _GW), lambda h, n: (n, col + h))
    par = pl.BlockSpec((1, _GW), lambda h, n: (0, h))
    shared = pl.BlockSpec((1, GDN_D), lambda h, n: (0, 0))
    if rev:
        st = pl.BlockSpec((GDN_HP, None, GDN_D, GDN_D), lambda h, n: (h, n_chunks - 1 - n, 0, 0))
    else:
        st = pl.BlockSpec((GDN_HP, None, GDN_D, GDN_D), lambda h, n: (h, n, 0, 0))
    return tok, par, shared, st


def _heads(ref):
    return jnp.stack([ref[:, h * GDN_D:(h + 1) * GDN_D] for h in range(ref.shape[1] // GDN_D)])


def gdn_chunk_fwd(qkv, z, a_log_x, dt_bias_x, o_norm, name):
    T = qkv.shape[0]
    n_chunks = T // GDN_C
    H = GDN_H
    tok, par, shared, st_spec = _gdn_specs(n_chunks, False)

    def body(q_ref, k_ref, v_ref, g_ref, bl_ref, al_ref, a_ref, dt_ref, on_ref, o_ref, st_ref, state):
        @pl.when(pl.program_id(1) == 0)
        def _():
            state[...] = jnp.zeros_like(state)

        st = state[...]
        st_ref[...] = st
        o, st_new = _gdn_chunk(_heads(q_ref), _heads(k_ref), _heads(v_ref), _heads(g_ref), _heads(bl_ref), _heads(al_ref),
                               _heads(a_ref), _heads(dt_ref), on_ref[...], st)
        for hh in range(GDN_HP):
            o_ref[:, hh * GDN_D:(hh + 1) * GDN_D] = o[hh].astype(o_ref.dtype)
        state[...] = st_new

    B = _GB
    return pl.pallas_call(
        body, name=name, grid=(B, n_chunks),
        in_specs=[tok(0), tok(B), tok(2 * B), tok(3 * B), tok(4 * B), tok(5 * B), par, par, shared],
        out_specs=[tok(0), st_spec],
        out_shape=[jax.ShapeDtypeStruct((T, H * GDN_D), BF16), jax.ShapeDtypeStruct((H, n_chunks, GDN_D, GDN_D), F32)],
        scratch_shapes=[pltpu.VMEM((GDN_HP, GDN_D, GDN_D), F32)],
        compiler_params=_cparams(("parallel", "arbitrary")),
    )(qkv, qkv, qkv, z, z, z, a_log_x, dt_bias_x, o_norm)


def gdn_chunk_bwd(qkv, z, a_log_x, dt_bias_x, o_norm, states, do, name):
    T = qkv.shape[0]
    n_chunks = T // GDN_C
    H = GDN_H
    tok, par, shared, st_spec = _gdn_specs(n_chunks, True)

    def body(q_ref, k_ref, v_ref, g_ref, bl_ref, al_ref, a_ref, dt_ref, on_ref, st_ref, do_ref,
             dq_ref, dk_ref, dv_ref, dg_ref, dbl_ref, dal_ref, da_ref, ddt_ref, don_ref, dstate):
        h, n = pl.program_id(0), pl.program_id(1)

        @pl.when(n == 0)
        def _():
            dstate[...] = jnp.zeros_like(dstate)
            da_ref[...] = jnp.zeros_like(da_ref)
            ddt_ref[...] = jnp.zeros_like(ddt_ref)

        @pl.when((n == 0) & (h == 0))
        def _():
            don_ref[...] = jnp.zeros_like(don_ref)

        _, vjp = jax.vjp(_gdn_chunk, _heads(q_ref), _heads(k_ref), _heads(v_ref), _heads(g_ref), _heads(bl_ref), _heads(al_ref),
                         _heads(a_ref), _heads(dt_ref), on_ref[...], st_ref[...])
        dq, dk, dv, dg, dbl, dal, da, ddt, don, dst = vjp((_heads(do_ref).astype(F32), dstate[...]))
        for hh in range(GDN_HP):
            cols = slice(hh * GDN_D, (hh + 1) * GDN_D)
            dq_ref[:, cols] = dq[hh]
            dk_ref[:, cols] = dk[hh]
            dv_ref[:, cols] = dv[hh]
            dg_ref[:, cols] = dg[hh].astype(dg_ref.dtype)
            dbl_ref[:, cols] = dbl[hh].astype(dbl_ref.dtype)
            dal_ref[:, cols] = dal[hh].astype(dal_ref.dtype)
            da_ref[:, cols] += da[hh]
            ddt_ref[:, cols] += ddt[hh]
        don_ref[...] += don
        dstate[...] = dst

    tok0 = tok(0)
    B = _GB
    f32_tok = jax.ShapeDtypeStruct((T, H * GDN_D), F32)
    bf_tok = jax.ShapeDtypeStruct((T, H * GDN_D), BF16)
    par_sh = jax.ShapeDtypeStruct((1, H * GDN_D), F32)
    return pl.pallas_call(
        body, name=name, grid=(B, n_chunks),
        in_specs=[tok(0), tok(B), tok(2 * B), tok(3 * B), tok(4 * B), tok(5 * B), par, par, shared, st_spec, tok0],
        out_specs=[tok0] * 6 + [par, par, shared],
        out_shape=[f32_tok, f32_tok, f32_tok, bf_tok, bf_tok, bf_tok, par_sh, par_sh, jax.ShapeDtypeStruct((1, GDN_D), F32)],
        scratch_shapes=[pltpu.VMEM((GDN_HP, GDN_D, GDN_D), F32)],
        compiler_params=_cparams(("arbitrary", "arbitrary")),
    )(qkv, qkv, qkv, z, z, z, a_log_x, dt_bias_x, o_norm, states, do)


def _prefetch_call(body, name, grid, in_specs, out_specs, out_shape, aliases=None):
    return pl.pallas_call(
        body, name=name,
        grid_spec=pltpu.PrefetchScalarGridSpec(num_scalar_prefetch=1, grid=grid, in_specs=in_specs, out_specs=out_specs),
        out_shape=out_shape, input_output_aliases=aliases or {},
        compiler_params=_cparams(("parallel",) * len(grid)))


def cast_into(src, src_row0, rows, slab, row0, me, name):
    width = src.shape[1]
    tr = _pick_rows(rows)
    assert row0 % tr == 0 and src_row0 % tr == 0

    def body(me_ref, s_ref, *refs):
        refs[-1][...] = s_ref[...].astype(refs[-1].dtype)

    in_specs = [pl.BlockSpec((tr, width), lambda r, me_ref: (src_row0 // tr + r, 0))]
    args = [src]
    aliases = {}
    if slab.arr is not None:
        in_specs.append(_ANY)
        args.append(slab.arr)
        aliases = {2: 0}
    slab.arr = _prefetch_call(
        body, name, (rows // tr,), in_specs,
        pl.BlockSpec((None, tr, width), lambda r, me_ref: (me_ref[0], row0 // tr + r, 0)),
        jax.ShapeDtypeStruct(slab.shape, slab.dtype), aliases)(me, *args)


def pair_add(g, b, c_idx, name):
    n, rh, w = b.shape
    tr = _pick_rows(rh)
    nb = rh // tr

    def body(c_ref, g_ref, b_ref, o_ref):
        o_ref[...] = (g_ref[...].astype(F32) + b_ref[...].astype(F32)).astype(o_ref.dtype)

    return _prefetch_call(
        body, name, (n, nb),
        [pl.BlockSpec((None, tr, w), lambda k, r, c: (k, c[0] * nb + r, 0)), pl.BlockSpec((None, tr, w), lambda k, r, c: (k, r, 0))],
        pl.BlockSpec((None, tr, w), lambda k, r, c: (k, r, 0)), jax.ShapeDtypeStruct(b.shape, BF16))(c_idx, g, b)


def chip_sum(p, rv, mc, name):
    n, rh, w = p.shape
    tr = _pick_rows(rh)
    nb = rh // tr

    def body(mc_ref, p_ref, rv_ref, o_ref):
        me = mc_ref[0]
        acc = None
        for k in range(n):
            part = jnp.where(me == k, p_ref[...], rv_ref[k]).astype(F32)
            acc = part if acc is None else acc + part
        o_ref[...] = acc

    return _prefetch_call(
        body, name, (nb,),
        [pl.BlockSpec((None, tr, w), lambda r, mc_ref: (mc_ref[0], r, 0)), pl.BlockSpec((n, tr, w), lambda r, mc_ref: (0, r, 0))],
        pl.BlockSpec((tr, w), lambda r, mc_ref: (mc_ref[1] * nb + r, 0)), jax.ShapeDtypeStruct((2 * rh, w), F32))(mc, p, rv)


def adamw(red, row0, w, m, v, w_row0, rows, prev, name):
    cols = w.shape[1]
    tr = _pick_rows(rows)
    assert row0 % tr == 0 and w_row0 % tr == 0

    def body(g_ref, w_ref, m_ref, v_ref, *refs):
        go_ref, d_ref, nm_ref, nv_ref = refs[-4:]
        gv = g_ref[...]
        nm = ADAM_B1 * m_ref[...] + (1.0 - ADAM_B1) * gv
        nv = ADAM_B2 * v_ref[...] + (1.0 - ADAM_B2) * (gv * gv)
        m_hat = nm / (1.0 - ADAM_B1 ** ADAM_STEP)
        v_hat = nv / (1.0 - ADAM_B2 ** ADAM_STEP)
        go_ref[...] = gv
        d_ref[...] = -ADAM_LR * (m_hat / (jnp.sqrt(v_hat) + ADAM_EPS) + ADAM_WD * w_ref[...])
        nm_ref[...] = nm
        nv_ref[...] = nv

    spec = pl.BlockSpec((tr, cols), lambda r: (w_row0 // tr + r, 0))
    sh = jax.ShapeDtypeStruct(w.shape, F32)
    in_specs = [pl.BlockSpec((tr, cols), lambda r: (row0 // tr + r, 0)), spec, spec, spec]
    args, aliases = [red, w, m, v], {}
    if prev is not None:
        in_specs += [_ANY] * 4
        args += list(prev)
        aliases = {4 + k: k for k in range(4)}
    return pl.pallas_call(
        body, name=name, grid=(rows // tr,), in_specs=in_specs, out_specs=[spec] * 4, out_shape=[sh] * 4,
        input_output_aliases=aliases, compiler_params=_cparams(("parallel",)),
    )(*args)


def _place():
    x, y, c = lax.axis_index("x"), lax.axis_index("y"), lax.axis_index("c")
    chips = [(1 - x, y), (x, 1 - y), (1 - x, 1 - y)]
    return x, y, c, chips


def _chip_index(cx, cy):
    return 2 * cx + cy


def _remote(src, dst, send_sem, recv_sem, to):
    return pltpu.make_async_remote_copy(src_ref=src, dst_ref=dst, send_sem=send_sem, recv_sem=recv_sem,
                                        device_id=to, device_id_type=MESH)


def _comm_call(body, name, ins, out_shapes, n_sems, aliases):
    return pl.pallas_call(
        body, name=name, in_specs=[_ANY] * len(ins), out_specs=[_ANY] * len(out_shapes), out_shape=out_shapes,
        scratch_shapes=[pltpu.SemaphoreType.DMA((n_sems,)), pltpu.SemaphoreType.DMA((n_sems,))],
        input_output_aliases=aliases,
    )(*ins)


def gather_slabs(slabs, name="weight_all_gather"):
    n = len(slabs)

    def body(*refs):
        in_refs, out_refs, send_sems, recv_sems = refs[:n], refs[n:2 * n], refs[-2], refs[-1]
        x, y, c, chips = _place()
        me = _chip_index(x, y)
        sib = (x, y, 1 - c)
        first, passed = [], []
        for a in range(n):
            rh = in_refs[a].shape[1] // 2
            mine = pl.ds(c * rh, rh)
            for j, chip in enumerate(chips):
                cp = _remote(in_refs[a].at[me, mine], out_refs[a].at[me, mine], send_sems.at[6 * a + j],
                             recv_sems.at[6 * a + j], (*chip, c))
                cp.start()
                first.append(cp)
        for a in range(n):
            rh = in_refs[a].shape[1] // 2
            mine = pl.ds(c * rh, rh)
            for j, chip in enumerate(chips):
                landed = out_refs[a].at[_chip_index(*chip), mine]
                _remote(landed, landed, send_sems.at[6 * a + j], recv_sems.at[6 * a + j], (*chip, c)).wait_recv()
                cp = _remote(landed, landed, send_sems.at[6 * a + 3 + j], recv_sems.at[6 * a + 3 + j], sib)
                cp.start()
                passed.append(cp)
        for a in range(n):
            rh = in_refs[a].shape[1] // 2
            theirs = pl.ds((1 - c) * rh, rh)
            for j, chip in enumerate(chips):
                got = out_refs[a].at[_chip_index(*chip), theirs]
                _remote(got, got, send_sems.at[6 * a + 3 + j], recv_sems.at[6 * a + 3 + j], sib).wait_recv()
        for cp in first + passed:
            cp.wait_send()

    return _comm_call(body, name, slabs, [jax.ShapeDtypeStruct(s.shape, s.dtype) for s in slabs], 6 * n,
                      {a: a for a in range(n)})


def pair_swap_halves(slabs, name="grad_pair_swap"):
    n = len(slabs)

    def body(*refs):
        in_refs, out_refs, send_sems, recv_sems = refs[:n], refs[n:2 * n], refs[-2], refs[-1]
        x, y, c, _ = _place()
        cps = []
        for a in range(n):
            rh = in_refs[a].shape[1] // 2
            cp = _remote(in_refs[a].at[:, pl.ds((1 - c) * rh, rh), :], out_refs[a], send_sems.at[a], recv_sems.at[a], (x, y, 1 - c))
            cp.start()
            cps.append(cp)
        for cp in cps:
            cp.wait()

    outs = [jax.ShapeDtypeStruct((s.shape[0], s.shape[1] // 2, s.shape[2]), s.dtype) for s in slabs]
    return _comm_call(body, name, slabs, outs, n, {})


def chip_exchange(parts, name="grad_chip_exchange"):
    n = len(parts)

    def body(*refs):
        in_refs, out_refs, send_sems, recv_sems = refs[:n], refs[n:2 * n], refs[-2], refs[-1]
        x, y, c, chips = _place()
        me = _chip_index(x, y)
        sends = []
        for a in range(n):
            for j, chip in enumerate(chips):
                cp = _remote(in_refs[a].at[_chip_index(*chip)], out_refs[a].at[me], send_sems.at[3 * a + j],
                             recv_sems.at[3 * a + j], (*chip, c))
                cp.start()
                sends.append(cp)
        for a in range(n):
            for j, chip in enumerate(chips):
                got = out_refs[a].at[_chip_index(*chip)]
                _remote(got, got, send_sems.at[3 * a + j], recv_sems.at[3 * a + j], (*chip, c)).wait_recv()
        for cp in sends:
            cp.wait_send()

    return _comm_call(body, name, parts, [jax.ShapeDtypeStruct(p.shape, p.dtype) for p in parts], 3 * n, {})


def pair_join_halves(reds, name="grad_pair_join"):
    n = len(reds)

    def body(*refs):
        in_refs, out_refs, send_sems, recv_sems = refs[:n], refs[n:2 * n], refs[-2], refs[-1]
        x, y, c, _ = _place()
        cps = []
        for a in range(n):
            rh = in_refs[a].shape[0] // 2
            mine = pl.ds(c * rh, rh)
            cp = _remote(in_refs[a].at[mine], out_refs[a].at[mine], send_sems.at[a], recv_sems.at[a], (x, y, 1 - c))
            cp.start()
            cps.append(cp)
        for a in range(n):
            rh = in_refs[a].shape[0] // 2
            got = out_refs[a].at[pl.ds((1 - c) * rh, rh)]
            _remote(got, got, send_sems.at[a], recv_sems.at[a], (x, y, 1 - c)).wait_recv()
        for cp in cps:
            cp.wait_send()

    return _comm_call(body, name, reds, [jax.ShapeDtypeStruct(r.shape, r.dtype) for r in reds], n, {a: a for a in range(n)})


_HBM = pl.BlockSpec(memory_space=pltpu.HBM)
_SEM = pl.BlockSpec(memory_space=pltpu.SEMAPHORE)
_EFFECT = pltpu.SideEffectType.DATAFLOW_SIDE_EFFECTING


def _in_hbm(a):
    return pltpu.with_memory_space_constraint(a, pltpu.HBM)


def _hbm_like(a):
    return pltpu.HBM(a.shape, a.dtype)


def _start_call(body, name, ins, n_sems, after):
    n = len(ins)
    res = pl.pallas_call(
        body, name=name, in_specs=[_HBM] * n + [_ANY],
        out_specs=[_SEM, _SEM] + [_HBM] * n + [pl.BlockSpec(memory_space=pltpu.VMEM)],
        out_shape=[pltpu.SemaphoreType.DMA((n_sems,)), pltpu.SemaphoreType.DMA((n_sems,))] + [_hbm_like(a) for a in ins]
        + [jax.ShapeDtypeStruct((8, LANES), F32)],
        input_output_aliases={a: 2 + a for a in range(n)},
        compiler_params=pltpu.CompilerParams(has_side_effects=_EFFECT),
    )(*[_in_hbm(a) for a in ins], after)
    return res[0], res[1], list(res[2:2 + n]), res[-1]


def _wait_call(body, name, thru, send_sems, recv_sems, after):
    n = len(thru)
    return pl.pallas_call(
        body, name=name, in_specs=[_HBM] * n + [_SEM, _SEM, _ANY], out_specs=[_HBM] * n,
        out_shape=[_hbm_like(a) for a in thru], input_output_aliases={a: a for a in range(n)},
        compiler_params=pltpu.CompilerParams(has_side_effects=_EFFECT),
    )(*thru, send_sems, recv_sems, after)


def gather_start(slabs, after, name="weight_gather_start"):
    n = len(slabs)

    def body(*refs):
        g_refs, send_sems, recv_sems, token = refs[:n], refs[n + 1], refs[n + 2], refs[-1]
        x, y, c, chips = _place()
        me = _chip_index(x, y)
        for a in range(n):
            rh = g_refs[a].shape[1] // 2
            mine = g_refs[a].at[me, pl.ds(c * rh, rh)]
            for j, chip in enumerate(chips):
                _remote(mine, mine, send_sems.at[3 * a + j], recv_sems.at[3 * a + j], (*chip, c)).start()
        token[...] = jnp.zeros_like(token)

    return _start_call(body, name, slabs, 3 * n, after)


def gather_wait(send_sems, recv_sems, thru, after, name="weight_gather_wait"):
    n = len(thru)

    def body(*refs):
        g_refs, send_sems, recv_sems = refs[:n], refs[n], refs[n + 1]
        x, y, c, chips = _place()
        me = _chip_index(x, y)
        for a in range(n):
            rh = g_refs[a].shape[1] // 2
            rows = pl.ds(c * rh, rh)
            for j, chip in enumerate(chips):
                mine, got = g_refs[a].at[me, rows], g_refs[a].at[_chip_index(*chip), rows]
                _remote(mine, mine, send_sems.at[3 * a + j], recv_sems.at[3 * a + j], (*chip, c)).wait_send()
                _remote(got, got, send_sems.at[3 * a + j], recv_sems.at[3 * a + j], (*chip, c)).wait_recv()

    return _wait_call(body, name, thru, send_sems, recv_sems, after)


def gather_forward(slabs, name="weight_gather_forward"):
    n = len(slabs)

    def body(*refs):
        in_refs, out_refs, send_sems, recv_sems = refs[:n], refs[n:2 * n], refs[-2], refs[-1]
        x, y, c, chips = _place()
        sib = (x, y, 1 - c)
        sends = []
        for a in range(n):
            rh = in_refs[a].shape[1] // 2
            for j, chip in enumerate(chips):
                k = _chip_index(*chip)
                cp = _remote(in_refs[a].at[k, pl.ds(c * rh, rh)], out_refs[a].at[k, pl.ds(c * rh, rh)], send_sems.at[3 * a + j],
                             recv_sems.at[3 * a + j], sib)
                cp.start()
                sends.append(cp)
        for a in range(n):
            rh = in_refs[a].shape[1] // 2
            for j, chip in enumerate(chips):
                got = out_refs[a].at[_chip_index(*chip), pl.ds((1 - c) * rh, rh)]
                _remote(got, got, send_sems.at[3 * a + j], recv_sems.at[3 * a + j], sib).wait_recv()
        for cp in sends:
            cp.wait_send()

    return _comm_call(body, name, slabs, [jax.ShapeDtypeStruct(s.shape, s.dtype) for s in slabs], 3 * n, {a: a for a in range(n)})


def exchange_start(parts, after, name="grad_exchange_start"):
    n = len(parts)

    def body(*refs):
        p_refs, land_refs, send_sems, recv_sems, token = refs[:n], refs[n:2 * n], refs[2 * n + 1], refs[2 * n + 2], refs[-1]
        x, y, c, chips = _place()
        me = _chip_index(x, y)
        for a in range(n):
            for j, chip in enumerate(chips):
                _remote(p_refs[a].at[_chip_index(*chip)], land_refs[a].at[me], send_sems.at[3 * a + j], recv_sems.at[3 * a + j],
                        (*chip, c)).start()
        token[...] = jnp.zeros_like(token)

    return _start_call(body, name, list(parts) + [lax.empty(p.shape, p.dtype) for p in parts], 3 * n, after)


def exchange_wait(send_sems, recv_sems, thru, after, name="grad_exchange_wait"):
    n = len(thru) // 2

    def body(*refs):
        p_refs, land_refs, send_sems, recv_sems = refs[:n], refs[n:2 * n], refs[2 * n], refs[2 * n + 1]
        x, y, c, chips = _place()
        me = _chip_index(x, y)
        for a in range(n):
            for j, chip in enumerate(chips):
                k = _chip_index(*chip)
                _remote(p_refs[a].at[k], land_refs[a].at[me], send_sems.at[3 * a + j], recv_sems.at[3 * a + j], (*chip, c)).wait_send()
                _remote(land_refs[a].at[k], land_refs[a].at[k], send_sems.at[3 * a + j], recv_sems.at[3 * a + j], (*chip, c)).wait_recv()

    res = _wait_call(body, name, thru, send_sems, recv_sems, after)
    return res[:n], res[n:]


_SLABS = {
    "first_w1024": [("mlp_w1", 2, 0, 1), ("mlp_w2", 1, 0, 1), ("xa_w_q", 1, 0, 1), ("xa_w_o", 1, 0, 1), ("mla_w_o", 1, 0, 1)],
    "first_xa_w_kv": [("xa_w_kv", 2, 0, 1)],
    "mla_w_in": [("mla_w_in", 1, 0, 2)], "mla_w_uq": [("mla_w_uq", 2, 0, 2)], "mla_w_ukv": [("mla_w_ukv", 2, 0, 2)],
    "gdn_w_in": [("gdn_w_in", 2, 0, 1)],
    "rest_w1024": [("mlp_w1", 2, 1, 4), ("mlp_w2", 1, 1, 4), ("xa_w_q", 1, 1, 4), ("xa_w_o", 1, 1, 4), ("mla_w_o", 1, 1, 2),
                   ("gdn_w_o", 1, 0, 1), ("sc_w_o", 1, 0, 1)],
    "rest_xa_w_kv": [("xa_w_kv", 2, 1, 4)], "rest_sc_w_in": [("sc_w_in", 2, 0, 1)],
}
_FIRST = ["first_w1024", "first_xa_w_kv", "mla_w_in", "mla_w_uq", "mla_w_ukv", "gdn_w_in"]
_REST = ["rest_w1024", "rest_xa_w_kv", "rest_sc_w_in"]
_RELAID = ("mla_w_in", "mla_w_uq", "mla_w_ukv", "gdn_w_in")
_SMALL = [("mla_q_norm", 1), ("mla_kv_norm", 1), ("gdn_conv_w", 2), ("sc_conv_w", 2)]
_REPL = ["gdn_a_log", "gdn_dt_bias", "gdn_o_norm", "norm_mix", "norm_mem", "norm_mlp", "mem_norm", "final_norm"]
_WEIGHTS = ['mla_w_in', 'mla_q_norm', 'mla_kv_norm', 'mla_w_uq', 'mla_w_ukv', 'mla_w_o', 'gdn_w_in', 'gdn_conv_w',
            'gdn_a_log', 'gdn_dt_bias', 'gdn_o_norm', 'gdn_w_o', 'sc_w_in', 'sc_conv_w', 'sc_w_o', 'norm_mix',
            'norm_mem', 'norm_mlp', 'xa_w_q', 'xa_w_kv', 'xa_w_o', 'mlp_w1', 'mlp_w2', 'mem_norm', 'final_norm']


class Layout:
    def __init__(self, shard_shapes):
        self.members, self.where, self.slab_dims = {}, {}, {}
        for slab, members in _SLABS.items():
            off, rows = 0, []
            for name, axis, l0, l1 in members:
                _, rpl, width = shard_shapes[name]
                rows.append((name, off, l0, l1, rpl))
                for layer in range(l0, l1):
                    self.where[(name, layer)] = (slab, off + (layer - l0) * rpl, rpl, width, axis)
                off += (l1 - l0) * rpl
            self.members[slab], self.slab_dims[slab] = rows, (off, width)

    def new_slabs(self, dtype):
        return {s: Slab(rows, width, dtype) for s, (rows, width) in self.slab_dims.items()}

    def loc(self, slabs, name, layer):
        slab, row0, rpl, width, axis = self.where[(name, layer)]
        if axis == 1:
            return Loc(slabs[slab], row0, N_CHIPS * rpl, width, 0)
        return Loc(slabs[slab], row0, rpl, N_CHIPS * width, 1)

    def _whole(self, name):
        (member,) = self.members[name]
        _, off, l0, l1, rpl = member
        assert off == 0 and l0 == 0
        return l1, rpl, self.slab_dims[name][1], dict((n, a) for n, a, _, _ in _SLABS[name])[name]

    def full(self, slabs, name):
        layers, rpl, width, axis = self._whole(name)
        blocks = slabs[name].arr.reshape(N_CHIPS, layers, rpl, width)
        return jnp.concatenate([blocks[s] for s in range(N_CHIPS)], axis=axis)

    def put_full(self, slabs, name, grad):
        layers, rpl, width, axis = self._whole(name)
        parts = jnp.stack(jnp.split(grad, N_CHIPS, axis=axis)).reshape(N_CHIPS, layers * rpl, width)
        slabs[name].arr = parts.astype(slabs[name].dtype)


def _small_pack(vals, names):
    flat = jnp.concatenate([vals[n].astype(F32).reshape(-1) for n in names])
    return jnp.pad(flat, (0, SMALL_ROWS * SMALL_COLS - flat.shape[0])).reshape(SMALL_ROWS, SMALL_COLS)


def _small_unpack(flat, like, names):
    out, off = {}, 0
    flat = flat.reshape(-1)
    for n in names:
        out[n] = flat[off:off + like[n].size].reshape(like[n].shape)
        off += like[n].size
    return out


_MLA_CFG = _Attn(MLA_H, 2 * LANES, MLA_NOPE, MLA_V, True, (MLA_NOPE + MLA_ROPE) ** -0.5, hp=8, hp_kv=4)
_XA_CFG = _Attn(XA_H, XA_D, XA_D, XA_D, False, XA_D ** -0.5, hp=4, hp_kv=4)


def _mla_weights(w_in, w_uq, w_ukv):
    w_in_p = jnp.pad(w_in, ((0, 0), (0, MLA_ZPAD - w_in.shape[1])))
    w_uq_p = jnp.pad(w_uq.reshape(MLA_QR, MLA_H, MLA_NOPE + MLA_ROPE), ((0, 0), (0, 0), (0, 2 * LANES - MLA_NOPE - MLA_ROPE)))
    w_uq_p = w_uq_p.reshape(MLA_QR, MLA_H * 2 * LANES)
    kv = w_ukv.reshape(MLA_KVR, MLA_H, MLA_NOPE + MLA_V)
    w_ukv_p = jnp.concatenate([kv[:, :, :MLA_NOPE].reshape(MLA_KVR, -1), kv[:, :, MLA_NOPE:].reshape(MLA_KVR, -1)], axis=1)
    return w_in_p, w_uq_p, w_ukv_p


def _mla_weight_grads(d_in_p, d_uq_p, d_ukv_p):
    d_in = d_in_p[:, :MLA_QR + MLA_KVR + MLA_ROPE]
    d_uq = d_uq_p.reshape(MLA_QR, MLA_H, 2 * LANES)[:, :, :MLA_NOPE + MLA_ROPE].reshape(MLA_QR, -1)
    half = MLA_H * MLA_NOPE
    d_ukv = jnp.concatenate([d_ukv_p[:, :half].reshape(MLA_KVR, MLA_H, MLA_NOPE),
                             d_ukv_p[:, half:].reshape(MLA_KVR, MLA_H, MLA_V)], axis=2).reshape(MLA_KVR, -1)
    return d_in, d_uq, d_ukv


def _mla_fwd(xs, h, wts, w_o, qn, kvn, tabs, tag):
    w_in_p, w_uq_p, w_ukv_p = wts
    z = mm(h, w_in_p, "nn", f"{tag}_in")
    cq, ckv, kr = mla_mid_fwd(z, qn, kvn, tabs, f"{tag}_mid")
    q = rope_q(mm(cq, w_uq_p, "nn", f"{tag}_uq"), tabs, False, f"{tag}_ropeq")
    kv = mm(ckv, w_ukv_p, "nn", f"{tag}_ukv", outs=(BF16,))
    o, lse = flash_fwd(_MLA_CFG, q, kv, kv, kr, f"{tag}_attn")
    xs = mm(o, w_o, "nn", f"{tag}_out", epi=_epi_add, extras=(xs,))
    return xs, (z, cq, ckv, kr, q, kv, o, lse)


def _mla_bwd(dx, h, wts, w_o, g_wo, qn, kvn, tabs, saved, tag):
    w_in_p, w_uq_p, w_ukv_p = wts
    z, cq, ckv, kr, q, kv, o, lse = saved
    mm(o, dx, "tn", f"{tag}_dwo", outs=(BF16,), out_loc=g_wo)
    do = mm(dx, w_o, "nt", f"{tag}_do", outs=(BF16,))
    dq, delta = flash_dq(_MLA_CFG, q, kv, kv, kr, o, do, lse, F32, f"{tag}_attn_dq")
    dk1, dv, dkr = flash_dkv(_MLA_CFG, q, kv, kv, kr, do, lse, delta, BF16, f"{tag}_attn_dkv")
    dqp = rope_q(dq, tabs, True, f"{tag}_ropeq_t")
    d_uq_p = mm(cq, dqp, "tn", f"{tag}_duq")
    dcq = mm(dqp, w_uq_p, "nt", f"{tag}_dcq")
    dkv = jnp.concatenate([dk1, dv], axis=1)
    d_ukv_p = mm(ckv, dkv, "tn", f"{tag}_dukv")
    dckv = mm(dkv, w_ukv_p, "nt", f"{tag}_dckv")
    dz, dqn, dkvn = mla_mid_bwd(z, qn, kvn, tabs, dcq, dckv, dkr, f"{tag}_mid_bwd")
    d_in_p = mm(h, dz, "tn", f"{tag}_din")
    dh = mm(dz, w_in_p, "nt", f"{tag}_dh")
    d_in, d_uq, d_ukv = _mla_weight_grads(d_in_p, d_uq_p, d_ukv_p)
    return dh, dict(mla_w_in=d_in, mla_w_uq=d_uq, mla_w_ukv=d_ukv, mla_q_norm=dqn, mla_kv_norm=dkvn)


_GDN_QKV = 3 * GDN_H * GDN_D
_GDN_GATE_END = _GDN_QKV + GDN_H * GDN_D


def _gdn_weights(w_in):
    rep = lambda cols: jnp.repeat(cols, GDN_D, axis=1)
    return jnp.concatenate([w_in[:, :_GDN_GATE_END], rep(w_in[:, _GDN_GATE_END:_GDN_GATE_END + GDN_H]),
                            rep(w_in[:, _GDN_GATE_END + GDN_H:])], axis=1)


def _fold(x):
    return x.reshape(x.shape[0], -1, GDN_D).sum(-1)


def _gdn_fwd(xs, h, w_in_x, conv_w, a_log, dt_bias, o_norm, w_o, tag):
    z = mm(h, w_in_x, "nn", f"{tag}_in")
    qkv = gdn_conv_fwd(z, conv_w, f"{tag}_conv")
    a_x, dt_x = jnp.repeat(a_log.reshape(1, -1), GDN_D, axis=1), jnp.repeat(dt_bias.reshape(1, -1), GDN_D, axis=1)
    og, states = gdn_chunk_fwd(qkv, z, a_x, dt_x, o_norm.reshape(1, -1), f"{tag}_chunks")
    xs = mm(og, w_o, "nn", f"{tag}_out", epi=_epi_add, extras=(xs,))
    return xs, (z, qkv, a_x, dt_x, og, states)


def _gdn_bwd(dx, h, w_in_x, conv_w, o_norm, w_o, g_wo, saved, tag):
    z, qkv, a_x, dt_x, og, states = saved
    mm(og, dx, "tn", f"{tag}_dwo", outs=(BF16,), out_loc=g_wo)
    dog = mm(dx, w_o, "nt", f"{tag}_dog")
    dq, dk, dv, dgate, dbl, dal, da_x, ddt_x, don = gdn_chunk_bwd(qkv, z, a_x, dt_x, o_norm.reshape(1, -1), states, dog,
                                                                  f"{tag}_chunks_bwd")
    dpre, dconv = gdn_conv_bwd(z, conv_w, jnp.concatenate([dq, dk, dv], axis=1), f"{tag}_conv_bwd")
    dz = jnp.concatenate([dpre, dgate, dbl, dal], axis=1)
    d_in_x = mm(h, dz, "tn", f"{tag}_din")
    dh = mm(dz, w_in_x, "nt", f"{tag}_dh", tn=512)
    ge = _GDN_GATE_END
    d_in = jnp.concatenate([d_in_x[:, :ge], _fold(d_in_x[:, ge:ge + GDN_H * GDN_D]), _fold(d_in_x[:, ge + GDN_H * GDN_D:])], axis=1)
    return dh, dict(gdn_w_in=d_in, gdn_conv_w=dconv, gdn_a_log=_fold(da_x).reshape(-1), gdn_dt_bias=_fold(ddt_x).reshape(-1),
                    gdn_o_norm=don.reshape(-1))


def _sc_fwd(xs, h, w_in, conv_w, w_o, tag):
    z = mm(h, w_in, "nn", f"{tag}_in")
    y = sc_fwd(z, conv_w, f"{tag}_conv")
    xs = mm(y, w_o, "nn", f"{tag}_out", epi=_epi_add, extras=(xs,))
    return xs, (z, y)


def _sc_bwd(dx, h, w_in, g_win, conv_w, w_o, g_wo, saved, tag):
    z, y = saved
    mm(y, dx, "tn", f"{tag}_dwo", outs=(BF16,), out_loc=g_wo)
    dy = mm(dx, w_o, "nt", f"{tag}_dy")
    db, dc, du, dconv = sc_bwd(z, conv_w, dy, f"{tag}_conv_bwd")
    dz = jnp.concatenate([db, dc, du], axis=1)
    mm(h, dz, "tn", f"{tag}_din", outs=(BF16,), out_loc=g_win)
    dh = mm(dz, w_in, "nt", f"{tag}_dh")
    return dh, dict(sc_conv_w=dconv)


def local_step(x, mem, pos, target, lay, wslabs, gslabs, small, before_layer=None, after_layer_bwd=None):
    depth = small["norm_mix"].shape[0]
    W = lambda name, layer: lay.loc(wslabs, name, layer)
    G = lambda name, layer: lay.loc(gslabs, name, layer)
    tabs = rope_tables(pos)
    mem_n = rmsnorm_fwd(mem, small["mem_norm"], "mem_norm")
    full = {n: lay.full(wslabs, n) for n in _RELAID}
    mla_w = [_mla_weights(full["mla_w_in"][j], full["mla_w_uq"][j], full["mla_w_ukv"][j]) for j in range(full["mla_w_in"].shape[0])]
    gdn_in_x = [_gdn_weights(full["gdn_w_in"][j]) for j in range(full["gdn_w_in"].shape[0])]

    xs = x
    saved = []
    for i in range(depth):
        j, kind = i // 3, i % 3
        tag = f"l{i}"
        if before_layer is not None:
            xs = before_layer(i, xs)
        x_a = xs
        h = rmsnorm_fwd(xs, small["norm_mix"][i], f"{tag}_norm_mix")
        if kind == 0:
            xs, mix = _mla_fwd(xs, h, mla_w[j], W("mla_w_o", j), small["mla_q_norm"][j], small["mla_kv_norm"][j], tabs, f"{tag}_mla")
        elif kind == 1:
            xs, mix = _gdn_fwd(xs, h, gdn_in_x[j], small["gdn_conv_w"][j], small["gdn_a_log"][j], small["gdn_dt_bias"][j],
                               small["gdn_o_norm"][j], W("gdn_w_o", j), f"{tag}_gdn")
        else:
            xs, mix = _sc_fwd(xs, h, W("sc_w_in", j), small["sc_conv_w"][j], W("sc_w_o", j), f"{tag}_sc")
        x_b = xs
        hn = rmsnorm_fwd(xs, small["norm_mem"][i], f"{tag}_norm_mem")
        xq = mm(hn, W("xa_w_q", i), "nn", f"{tag}_xa_q", outs=(BF16,))
        xkv = mm(mem_n, W("xa_w_kv", i), "nn", f"{tag}_xa_kv", outs=(BF16,))
        xo, xlse = flash_fwd(_XA_CFG, xq, xkv, xkv, None, f"{tag}_xa_attn")
        xs = mm(xo, W("xa_w_o", i), "nn", f"{tag}_xa_out", epi=_epi_add, extras=(xs,))
        x_c = xs
        hm = rmsnorm_fwd(xs, small["norm_mlp"][i], f"{tag}_norm_mlp")
        h1, act = mm(hm, W("mlp_w1", i), "nn", f"{tag}_mlp_up", outs=(BF16, BF16), epi=_epi_relu2)
        xs = mm(act, W("mlp_w2", i), "nn", f"{tag}_mlp_down", epi=_epi_add, extras=(xs,))
        saved.append((x_a, h, mix, x_b, hn, xq, xkv, xo, xlse, x_c, hm, h1, act))

    se, dx, d_final = loss_head(xs, small["final_norm"], target)

    per_layer = {n: [None] * depth for n in ("norm_mix", "norm_mem", "norm_mlp")}
    mixer = {}
    dmem_n = jnp.zeros(mem.shape, F32)
    for i in reversed(range(depth)):
        j, kind = i // 3, i % 3
        tag = f"l{i}"
        x_a, h, mix, x_b, hn, xq, xkv, xo, xlse, x_c, hm, h1, act = saved[i]
        mm(act, dx, "tn", f"{tag}_mlp_dw2", outs=(BF16,), out_loc=G("mlp_w2", i))
        dh1 = mm(dx, W("mlp_w2", i), "nt", f"{tag}_mlp_dh1", outs=(BF16,), epi=_epi_relu2_bwd, extras=(h1,))
        mm(hm, dh1, "tn", f"{tag}_mlp_dw1", outs=(BF16,), out_loc=G("mlp_w1", i))
        dhm = mm(dh1, W("mlp_w1", i), "nt", f"{tag}_mlp_dhm")
        dx, per_layer["norm_mlp"][i] = rmsnorm_bwd(x_c, small["norm_mlp"][i], dhm, dx, f"{tag}_norm_mlp_bwd")
        mm(xo, dx, "tn", f"{tag}_xa_dwo", outs=(BF16,), out_loc=G("xa_w_o", i))
        dxo = mm(dx, W("xa_w_o", i), "nt", f"{tag}_xa_do", outs=(BF16,))
        dxq, xdelta = flash_dq(_XA_CFG, xq, xkv, xkv, None, xo, dxo, xlse, BF16, f"{tag}_xa_attn_dq")
        dxk, dxv = flash_dkv(_XA_CFG, xq, xkv, xkv, None, dxo, xlse, xdelta, BF16, f"{tag}_xa_attn_dkv")
        dxkv = jnp.concatenate([dxk, dxv], axis=1)
        mm(hn, dxq, "tn", f"{tag}_xa_dwq", outs=(BF16,), out_loc=G("xa_w_q", i))
        dhn = mm(dxq, W("xa_w_q", i), "nt", f"{tag}_xa_dhn")
        mm(mem_n, dxkv, "tn", f"{tag}_xa_dwkv", outs=(BF16,), out_loc=G("xa_w_kv", i))
        dmem_n = mm(dxkv, W("xa_w_kv", i), "nt", f"{tag}_xa_dmem", epi=_epi_add, extras=(dmem_n,))
        dx, per_layer["norm_mem"][i] = rmsnorm_bwd(x_b, small["norm_mem"][i], dhn, dx, f"{tag}_norm_mem_bwd")
        if kind == 0:
            dh, gr = _mla_bwd(dx, h, mla_w[j], W("mla_w_o", j), G("mla_w_o", j), small["mla_q_norm"][j], small["mla_kv_norm"][j],
                              tabs, mix, f"{tag}_mla")
        elif kind == 1:
            dh, gr = _gdn_bwd(dx, h, gdn_in_x[j], small["gdn_conv_w"][j], small["gdn_o_norm"][j], W("gdn_w_o", j), G("gdn_w_o", j),
                              mix, f"{tag}_gdn")
        else:
            dh, gr = _sc_bwd(dx, h, W("sc_w_in", j), G("sc_w_in", j), small["sc_conv_w"][j], W("sc_w_o", j), G("sc_w_o", j),
                             mix, f"{tag}_sc")
        for n, g in gr.items():
            mixer.setdefault(n, {})[j] = g
        dx, per_layer["norm_mix"][i] = rmsnorm_bwd(x_a, small["norm_mix"][i], dh, dx, f"{tag}_norm_mix_bwd")
        if after_layer_bwd is not None:
            dx = after_layer_bwd(i, dx)

    _, d_mem_norm = rmsnorm_bwd(mem, small["mem_norm"], dmem_n, jnp.zeros(mem.shape, F32), "mem_norm_bwd")
    grads = {n: jnp.stack(v) for n, v in per_layer.items()}
    for n, by_j in mixer.items():
        grads[n] = jnp.stack([by_j[j] for j in sorted(by_j)])
    grads["mem_norm"] = d_mem_norm
    grads["final_norm"] = d_final
    for n in _RELAID:
        lay.put_full(gslabs, n, grads.pop(n))
    return se, dx, grads


def kernel(x, mem, positions, mla_w_in, mla_q_norm, mla_kv_norm, mla_w_uq, mla_w_ukv, mla_w_o, gdn_w_in, gdn_conv_w, gdn_a_log, gdn_dt_bias, gdn_o_norm, gdn_w_o, sc_w_in, sc_conv_w, sc_w_o, norm_mix, norm_mem, norm_mlp, xa_w_q, xa_w_kv, xa_w_o, mlp_w1, mlp_w2, mem_norm, final_norm, loss_target, m_mla_w_in, m_mla_q_norm, m_mla_kv_norm, m_mla_w_uq, m_mla_w_ukv, m_mla_w_o, m_gdn_w_in, m_gdn_conv_w, m_gdn_a_log, m_gdn_dt_bias, m_gdn_o_norm, m_gdn_w_o, m_sc_w_in, m_sc_conv_w, m_sc_w_o, m_norm_mix, m_norm_mem, m_norm_mlp, m_xa_w_q, m_xa_w_kv, m_xa_w_o, m_mlp_w1, m_mlp_w2, m_mem_norm, m_final_norm, v_mla_w_in, v_mla_q_norm, v_mla_kv_norm, v_mla_w_uq, v_mla_w_ukv, v_mla_w_o, v_gdn_w_in, v_gdn_conv_w, v_gdn_a_log, v_gdn_dt_bias, v_gdn_o_norm, v_gdn_w_o, v_sc_w_in, v_sc_conv_w, v_sc_w_o, v_norm_mix, v_norm_mem, v_norm_mlp, v_xa_w_q, v_xa_w_kv, v_xa_w_o, v_mlp_w1, v_mlp_w2, v_mem_norm, v_final_norm):
    given = dict(locals())
    p = {n: given[n] for n in _WEIGHTS}
    mom = {n: given["m_" + n] for n in _WEIGHTS}
    var = {n: given["v_" + n] for n in _WEIGHTS}
    split = sorted({n for members in _SLABS.values() for n, _, _, _ in members})
    lay = Layout({n: p[n].shape for n in split})
    flat2d = lambda a: a.reshape(-1, a.shape[-1])

    me = (2 * lax.axis_index("x") + lax.axis_index("y")).astype(jnp.int32)
    core = lax.axis_index("c").astype(jnp.int32)
    me1, c1, mc = me.reshape(1), core.reshape(1), jnp.stack([me, core])

    wslabs = lay.new_slabs(BF16)
    for slab, members in lay.members.items():
        for name, off, l0, l1, rpl in members:
            cast_into(flat2d(p[name]), l0 * rpl, (l1 - l0) * rpl, wslabs[slab], off, me1, f"cast_{slab}_{name}")
    small_names = [n for n, _ in _SMALL]
    words = lax.bitcast_convert_type(jnp.concatenate([p[n].reshape(-1) for n in small_names]), BF16).reshape(-1)
    words = jnp.pad(words, (0, SMALL_ROWS * SMALL_COLS - words.shape[0])).reshape(1, SMALL_ROWS, SMALL_COLS)
    small_slab = lax.dynamic_update_slice(jnp.zeros((N_CHIPS, SMALL_ROWS, SMALL_COLS), BF16), words, (me, 0, 0))

    gathered = gather_slabs([wslabs[s].arr for s in _FIRST] + [small_slab])
    for s, arr in zip(_FIRST, gathered):
        wslabs[s].arr = arr
    g_send, g_recv, g_thru, g_token = gather_start([wslabs[s].arr for s in _REST], gathered[-1])

    def before_layer(i, xs):
        if i == 0:
            return xs + g_token[0, 0]
        if i == 1:
            landed = gather_wait(g_send, g_recv, g_thru, xs)
            for s, arr in zip(_REST, gather_forward(landed)):
                wslabs[s].arr = arr
        return xs

    small = {n: p[n] for n in _REPL}
    got, off = gathered[-1].reshape(N_CHIPS, -1), 0
    for n, ax in _SMALL:
        vals = lax.bitcast_convert_type(got[:, off:off + 2 * p[n].size].reshape(N_CHIPS, p[n].size, 2), F32)
        vals = vals.reshape((N_CHIPS,) + p[n].shape)
        small[n] = jnp.concatenate([vals[s] for s in range(N_CHIPS)], axis=ax)
        off += 2 * p[n].size

    gslabs = lay.new_slabs(BF16)
    early = {}

    def after_layer_bwd(i, dx):
        if i != 1:
            return dx
        g_rest = [gslabs[s].arr for s in _REST]
        swapped_rest = pair_swap_halves(g_rest, "grad_pair_swap_rest")
        part_rest = [pair_add(g, b, c1, f"pair_add_{s}") for g, b, s in zip(g_rest, swapped_rest, _REST)]
        early["send"], early["recv"], early["thru"], token = exchange_start(part_rest, c1)
        return dx + token[0, 0]

    se, dx, sgrads = local_step(x[0], mem[0], positions.reshape(-1, 1), loss_target[0], lay, wslabs, gslabs, small,
                                before_layer, after_layer_bwd)
    loss = lax.psum(0.5 * jnp.sum(se) / x.shape[-1], ("x", "y", "c"))
    part_rest, received_rest = exchange_wait(early["send"], early["recv"], early["thru"], dx)

    axes = dict(_SMALL)
    small_order = small_names + _REPL
    slots = []
    for s in range(N_CHIPS):
        vals = {n: (lax.slice_in_dim(g, s * p[n].shape[axes[n]], (s + 1) * p[n].shape[axes[n]], axis=axes[n]) if n in axes else g)
                for n, g in sgrads.items()}
        slots.append(_small_pack(vals, small_order))
    g_first = [gslabs[s].arr for s in _FIRST] + [jnp.stack(slots).astype(BF16)]
    names_first = _FIRST + ["small"]
    swapped_first = pair_swap_halves(g_first, "grad_pair_swap_first")
    part_first = [pair_add(g, b, c1, f"pair_add_{s}") for g, b, s in zip(g_first, swapped_first, names_first)]
    received_first = chip_exchange(part_first)
    names = _REST + names_first
    halves = [chip_sum(q, r, mc, f"chip_sum_{s}")
              for q, r, s in zip(list(part_rest) + part_first, list(received_rest) + list(received_first), names)]
    reduced = dict(zip(names, pair_join_halves(halves)))

    res = {}
    for slab in _FIRST + _REST:
        for name, off, l0, l1, rpl in lay.members[slab]:
            res[name] = adamw(reduced[slab], off, flat2d(p[name]), flat2d(mom[name]), flat2d(var[name]), l0 * rpl, (l1 - l0) * rpl,
                              res.get(name), f"adamw_{slab}_{name}")
    for name in split:
        res[name] = [o.reshape(p[name].shape) for o in res[name]]
    sp = {k: _small_pack(d, small_order) for k, d in (("w", p), ("m", mom), ("v", var))}
    outs = adamw(reduced["small"], 0, sp["w"], sp["m"], sp["v"], 0, SMALL_ROWS, None, "adamw_small")
    unpacked = [_small_unpack(o, p, small_order) for o in outs]
    for n in small_order:
        res[n] = [u[n] for u in unpacked]
    return (loss, dx[None], *[res[n][k] for k in range(4) for n in _WEIGHTS])
```

```python
import jax
import jax.numpy as jnp
from jax import lax
from jax.experimental import pallas as pl
from jax.experimental.pallas import tpu as pltpu

F32 = jnp.float32
BF16 = jnp.bfloat16
HI = lax.Precision.HIGHEST
MESH = pl.DeviceIdType.MESH

EPS = 1e-6
ROPE_THETA = 10000.0
N_CHIPS = 4
LANES = 128
VMEM_LIMIT = 56 * 1024 * 1024
NEG = -1e30

MLA_H, MLA_NOPE, MLA_ROPE, MLA_V = 8, 128, 64, 128
MLA_QR, MLA_KVR = 384, 256
MLA_ZPAD = 768
GDN_H, GDN_D, GDN_C = 8, 128, 64
XA_H, XA_D = 4, 256

ADAM_LR, ADAM_B1, ADAM_B2, ADAM_EPS, ADAM_WD, ADAM_STEP = 0.001, 0.9, 0.999, 1e-08, 0.01, 10

SMALL_ROWS, SMALL_COLS = 32, 1024


def _cparams(sem=None):
    return pltpu.CompilerParams(dimension_semantics=sem, vmem_limit_bytes=VMEM_LIMIT)


def _pick(dim, pref):
    t = (min(pref, dim) // LANES) * LANES
    while t >= LANES:
        if dim % t == 0:
            return t
        t -= LANES
    return dim


def _pick_rows(rows, pref=256):
    t = pref
    while rows % t:
        t //= 2
    return t


class Slab:
    def __init__(self, rows, width, dtype, arr=None):
        self.shape, self.dtype, self.arr = (N_CHIPS, rows, width), dtype, arr


class Loc:
    def __init__(self, slab, row0, K, N, axis):
        self.slab, self.row0, self.K, self.N, self.axis = slab, row0, K, N, axis
        self.Ks = K // N_CHIPS if axis == 0 else K
        self.Ns = N // N_CHIPS if axis == 1 else N

    def tile_spec(self, tr, tc, rc):
        assert self.row0 % tr == 0 and self.Ks % tr == 0 and self.Ns % tc == 0, (self.row0, self.Ks, self.Ns, tr, tc)
        r0, rb, cb = self.row0 // tr, self.Ks // tr, self.Ns // tc
        if self.axis == 0:
            return pl.BlockSpec((None, tr, tc), lambda i, j: (rc(i, j)[0] // rb, r0 + rc(i, j)[0] % rb, rc(i, j)[1]))
        return pl.BlockSpec((None, tr, tc), lambda i, j: (rc(i, j)[1] // cb, r0 + rc(i, j)[0], rc(i, j)[1] % cb))

    def slot_spec(self, slot, tr, tc, rc):
        assert self.row0 % tr == 0, (self.row0, tr)
        r0 = self.row0 // tr
        return pl.BlockSpec((None, tr, tc), lambda i, j: (slot, r0 + rc(i, j)[0], rc(i, j)[1]))


_DIMS = {"nn": ((1,), (0,)), "nt": ((1,), (1,)), "tn": ((0,), (0,))}
_ANY = pl.BlockSpec(memory_space=pl.ANY)


def mm(a, b, mode, name, outs=(F32,), epi=None, extras=(), tm=1024, tn=1024, out_loc=None):
    b_loc = b if isinstance(b, Loc) else None
    if mode == "nn":
        M, K = a.shape
        K2, N = (b_loc.K, b_loc.N) if b_loc else b.shape
    elif mode == "nt":
        M, K = a.shape
        N, K2 = (b_loc.K, b_loc.N) if b_loc else b.shape
    else:
        K, M = a.shape
        K2, N = b.shape
    assert K == K2, (name, a.shape, K2, N)
    tm = _pick(out_loc.Ks if (out_loc and out_loc.axis == 0) else M, tm)
    if out_loc is not None and out_loc.axis == 1:
        tn = _pick(out_loc.Ns, tn)
    elif b_loc is not None and ((mode == "nn" and b_loc.axis == 1) or (mode == "nt" and b_loc.axis == 0)):
        tn = _pick(b_loc.Ns if mode == "nn" else b_loc.Ks, tn)
    elif b_loc is not None:
        tn = _pick(N, min(tn, 512))
    else:
        tn = _pick(N, tn)

    parts = 1
    if mode == "tn":
        a_spec = pl.BlockSpec((K, tm), lambda i, j: (0, i))
        b_specs, b_args = [pl.BlockSpec((K, tn), lambda i, j: (0, j))], [b]
    else:
        a_spec = pl.BlockSpec((tm, K), lambda i, j: (i, 0))
        if b_loc is None:
            b_specs = [pl.BlockSpec((K, tn), lambda i, j: (0, j)) if mode == "nn" else pl.BlockSpec((tn, K), lambda i, j: (j, 0))]
            b_args = [b]
        elif mode == "nn" and b_loc.axis == 1:
            b_specs, b_args = [b_loc.tile_spec(K, tn, lambda i, j: (0, j))], [b_loc.slab.arr]
        elif mode == "nt" and b_loc.axis == 0:
            b_specs, b_args = [b_loc.tile_spec(tn, K, lambda i, j: (j, 0))], [b_loc.slab.arr]
        elif mode == "nn":
            parts = N_CHIPS
            b_specs = [b_loc.slot_spec(s, b_loc.Ks, tn, lambda i, j: (0, j)) for s in range(parts)]
            b_args = [b_loc.slab.arr] * parts
        else:
            parts = N_CHIPS
            b_specs = [b_loc.slot_spec(s, tn, b_loc.Ns, lambda i, j: (j, 0)) for s in range(parts)]
            b_args = [b_loc.slab.arr] * parts
    kp = K // parts
    n_ex, n_out = len(extras), len(outs)
    dims = (_DIMS[mode], ((), ()))

    def body(*refs):
        a_ref = refs[0]
        b_refs = refs[1:1 + parts]
        ex_refs = refs[1 + parts:1 + parts + n_ex]
        o_refs = refs[-n_out:]
        acc = None
        for s in range(parts):
            av = a_ref[...] if parts == 1 else a_ref[:, s * kp:(s + 1) * kp]
            d = lax.dot_general(av.astype(BF16), b_refs[s][...].astype(BF16), dims, preferred_element_type=F32)
            acc = d if acc is None else acc + d
        res = epi(acc, *[e[...] for e in ex_refs]) if epi is not None else (acc,)
        for o_ref, v in zip(o_refs, res):
            o_ref[...] = v.astype(o_ref.dtype)

    mn_spec = pl.BlockSpec((tm, tn), lambda i, j: (i, j))
    in_specs = [a_spec] + b_specs + [mn_spec] * n_ex
    args = [a] + b_args + list(extras)
    aliases = {}
    if out_loc is None:
        out_specs = [mn_spec] * n_out
        out_shape = [jax.ShapeDtypeStruct((M, N), d) for d in outs]
    else:
        assert n_out == 1 and mode == "tn"
        out_specs = [out_loc.tile_spec(tm, tn, lambda i, j: (i, j))]
        out_shape = [jax.ShapeDtypeStruct(out_loc.slab.shape, out_loc.slab.dtype)]
        if out_loc.slab.arr is not None:
            in_specs.append(_ANY)
            args.append(out_loc.slab.arr)
            aliases = {len(args) - 1: 0}

    res = pl.pallas_call(
        body, name=name, grid=(M // tm, N // tn), in_specs=in_specs, out_specs=out_specs, out_shape=out_shape,
        input_output_aliases=aliases, compiler_params=_cparams(("parallel", "parallel")),
    )(*args)
    if out_loc is not None:
        out_loc.slab.arr = res[0]
        return None
    return res[0] if n_out == 1 else tuple(res)


def _epi_add(acc, r):
    return (acc + r,)


def _epi_relu2(acc):
    r = jnp.maximum(acc, 0.0)
    return acc, r * r


def _epi_relu2_bwd(acc, h1):
    return (acc * (2.0 * jnp.maximum(h1.astype(F32), 0.0)),)


def _rms(x, g):
    return x * lax.rsqrt(jnp.mean(x * x, axis=-1, keepdims=True) + EPS) * g


def _row_spec(ts, cols):
    return pl.BlockSpec((ts, cols), lambda i: (i, 0))


def _par_spec(cols):
    return pl.BlockSpec((1, cols), lambda i: (0, 0))


def rmsnorm_fwd(x, g, name, ts=256):
    T, D = x.shape
    ts = min(ts, T)

    def body(x_ref, g_ref, o_ref):
        o_ref[...] = _rms(x_ref[...], g_ref[...]).astype(o_ref.dtype)

    return pl.pallas_call(
        body, name=name, grid=(T // ts,),
        in_specs=[_row_spec(ts, D), _par_spec(D)], out_specs=_row_spec(ts, D),
        out_shape=jax.ShapeDtypeStruct((T, D), BF16), compiler_params=_cparams(("parallel",)),
    )(x, g.reshape(1, D))


def rmsnorm_bwd(x, g, dy, dx_in, name, ts=256):
    T, D = x.shape
    ts = min(ts, T)

    def body(x_ref, g_ref, dy_ref, dxi_ref, dx_ref, dg_ref):
        xv = x_ref[...]
        r = lax.rsqrt(jnp.mean(xv * xv, axis=-1, keepdims=True) + EPS)
        xh = xv * r
        dyv = dy_ref[...].astype(F32)
        dxh = dyv * g_ref[...]
        dx_ref[...] = dxi_ref[...] + r * (dxh - xh * jnp.mean(dxh * xh, axis=-1, keepdims=True))
        dg = jnp.sum(dyv * xh, axis=0, keepdims=True)

        @pl.when(pl.program_id(0) == 0)
        def _():
            dg_ref[...] = jnp.zeros_like(dg_ref)

        dg_ref[...] += dg

    dx, dg = pl.pallas_call(
        body, name=name, grid=(T // ts,),
        in_specs=[_row_spec(ts, D), _par_spec(D), _row_spec(ts, D), _row_spec(ts, D)],
        out_specs=[_row_spec(ts, D), _par_spec(D)],
        out_shape=[jax.ShapeDtypeStruct((T, D), F32), jax.ShapeDtypeStruct((1, D), F32)],
        compiler_params=_cparams(("arbitrary",)),
    )(x, g.reshape(1, D), dy, dx_in)
    return dx, dg.reshape(D)


def rope_tables(pos, name="rope_tables"):
    T = pos.shape[0]
    half = MLA_ROPE // 2
    inv = ROPE_THETA ** (-jnp.arange(0, MLA_ROPE, 2, dtype=F32) / MLA_ROPE)
    inv_row = jnp.concatenate([inv, inv, jnp.zeros((LANES - MLA_ROPE,), F32)]).reshape(1, LANES)

    def body(p_ref, f_ref, c_ref, a_ref, b_ref):
        ang = p_ref[...].astype(F32) * f_ref[...]
        lane = lax.broadcasted_iota(jnp.int32, ang.shape, 1)
        c, s = jnp.cos(ang), jnp.sin(ang)
        c_ref[...] = jnp.where(lane < MLA_ROPE, c, 0.0)
        a_ref[...] = jnp.where(lane < half, -s, 0.0)
        b_ref[...] = jnp.where((lane >= half) & (lane < MLA_ROPE), s, 0.0)

    sh = jax.ShapeDtypeStruct((T, LANES), F32)
    return pl.pallas_call(body, name=name, out_shape=[sh, sh, sh], compiler_params=_cparams())(pos, inv_row)


def _roll_l(x):
    return pltpu.roll(x, LANES - MLA_ROPE // 2, 1)


def _roll_r(x):
    return pltpu.roll(x, MLA_ROPE // 2, 1)


def _rope(r, c, sa, sb):
    return r * c + _roll_l(r) * sa + _roll_r(r) * sb


def _rope_t(d, c, sa, sb):
    return d * c + _roll_r(d * sa) + _roll_l(d * sb)


def mla_mid_fwd(z, qn, kvn, tabs, name, ts=256):
    T = z.shape[0]
    ts = min(ts, T)
    a0, a1 = MLA_QR, MLA_QR + MLA_KVR

    def body(z_ref, qn_ref, kvn_ref, c_ref, sa_ref, sb_ref, cq_ref, ckv_ref, kr_ref):
        cq_ref[...] = _rms(z_ref[:, 0:a0], qn_ref[...]).astype(BF16)
        ckv_ref[...] = _rms(z_ref[:, a0:a1], kvn_ref[...]).astype(BF16)
        kr_ref[...] = _rope(z_ref[:, a1:MLA_ZPAD], c_ref[...], sa_ref[...], sb_ref[...]).astype(BF16)

    return pl.pallas_call(
        body, name=name, grid=(T // ts,),
        in_specs=[_row_spec(ts, MLA_ZPAD), _par_spec(MLA_QR), _par_spec(MLA_KVR)] + [_row_spec(ts, LANES)] * 3,
        out_specs=[_row_spec(ts, MLA_QR), _row_spec(ts, MLA_KVR), _row_spec(ts, LANES)],
        out_shape=[jax.ShapeDtypeStruct((T, MLA_QR), BF16), jax.ShapeDtypeStruct((T, MLA_KVR), BF16),
                   jax.ShapeDtypeStruct((T, LANES), BF16)],
        compiler_params=_cparams(("parallel",)),
    )(z, qn.reshape(1, -1), kvn.reshape(1, -1), *tabs)


def mla_mid_bwd(z, qn, kvn, tabs, dcq, dckv, dkr, name, ts=256):
    T = z.shape[0]
    ts = min(ts, T)
    a0, a1 = MLA_QR, MLA_QR + MLA_KVR

    def body(z_ref, qn_ref, kvn_ref, c_ref, sa_ref, sb_ref, dcq_ref, dckv_ref, dkr_ref, dz_ref, dqn_ref, dkvn_ref):
        _, vq = jax.vjp(_rms, z_ref[:, 0:a0], qn_ref[...])
        dzq, dqn = vq(dcq_ref[...].astype(F32))
        _, vk = jax.vjp(_rms, z_ref[:, a0:a1], kvn_ref[...])
        dzk, dkvn = vk(dckv_ref[...].astype(F32))
        dz_ref[:, 0:a0] = dzq.astype(dz_ref.dtype)
        dz_ref[:, a0:a1] = dzk.astype(dz_ref.dtype)
        dz_ref[:, a1:MLA_ZPAD] = _rope_t(dkr_ref[...].astype(F32), c_ref[...], sa_ref[...], sb_ref[...]).astype(dz_ref.dtype)

        @pl.when(pl.program_id(0) == 0)
        def _():
            dqn_ref[...] = jnp.zeros_like(dqn_ref)
            dkvn_ref[...] = jnp.zeros_like(dkvn_ref)

        dqn_ref[...] += dqn
        dkvn_ref[...] += dkvn

    dz, dqn, dkvn = pl.pallas_call(
        body, name=name, grid=(T // ts,),
        in_specs=[_row_spec(ts, MLA_ZPAD), _par_spec(MLA_QR), _par_spec(MLA_KVR)] + [_row_spec(ts, LANES)] * 3
        + [_row_spec(ts, MLA_QR), _row_spec(ts, MLA_KVR), _row_spec(ts, LANES)],
        out_specs=[_row_spec(ts, MLA_ZPAD), _par_spec(MLA_QR), _par_spec(MLA_KVR)],
        out_shape=[jax.ShapeDtypeStruct((T, MLA_ZPAD), BF16), jax.ShapeDtypeStruct((1, MLA_QR), F32),
                   jax.ShapeDtypeStruct((1, MLA_KVR), F32)],
        compiler_params=_cparams(("arbitrary",)),
    )(z, qn.reshape(1, -1), kvn.reshape(1, -1), *tabs, dcq, dckv, dkr)
    return dz, dqn.reshape(-1), dkvn.reshape(-1)


def rope_q(q, tabs, transpose, name, ts=256):
    T, W = q.shape
    ts = min(ts, T)
    fn = _rope_t if transpose else _rope
    hw = 2 * LANES

    def body(q_ref, c_ref, sa_ref, sb_ref, o_ref):
        c, sa, sb = c_ref[...], sa_ref[...], sb_ref[...]
        for h in range(W // hw):
            o_ref[:, h * hw:h * hw + LANES] = q_ref[:, h * hw:h * hw + LANES].astype(o_ref.dtype)
            o_ref[:, h * hw + LANES:(h + 1) * hw] = fn(q_ref[:, h * hw + LANES:(h + 1) * hw].astype(F32), c, sa, sb).astype(o_ref.dtype)

    return pl.pallas_call(
        body, name=name, grid=(T // ts,),
        in_specs=[_row_spec(ts, W)] + [_row_spec(ts, LANES)] * 3, out_specs=_row_spec(ts, W),
        out_shape=jax.ShapeDtypeStruct((T, W), BF16), compiler_params=_cparams(("parallel",)),
    )(q, *tabs)


def loss_head(x, g, target, name="loss_head", ts=256):
    T, D = x.shape
    ts = min(ts, T)

    def body(x_ref, g_ref, t_ref, se_ref, dx_ref, dg_ref):
        xv = x_ref[...]
        r = lax.rsqrt(jnp.mean(xv * xv, axis=-1, keepdims=True) + EPS)
        xh = xv * r
        err = xh * g_ref[...] - t_ref[...]
        dy = err * (1.0 / D)
        dxh = dy * g_ref[...]
        dx_ref[...] = r * (dxh - xh * jnp.mean(dxh * xh, axis=-1, keepdims=True))

        @pl.when(pl.program_id(0) == 0)
        def _():
            se_ref[...] = jnp.zeros_like(se_ref)
            dg_ref[...] = jnp.zeros_like(dg_ref)

        se_ref[...] += jnp.sum(err * err, axis=0, keepdims=True)
        dg_ref[...] += jnp.sum(dy * xh, axis=0, keepdims=True)

    se, dx, dg = pl.pallas_call(
        body, name=name, grid=(T // ts,),
        in_specs=[_row_spec(ts, D), _par_spec(D), _row_spec(ts, D)],
        out_specs=[_par_spec(D), _row_spec(ts, D), _par_spec(D)],
        out_shape=[jax.ShapeDtypeStruct((1, D), F32), jax.ShapeDtypeStruct((T, D), F32), jax.ShapeDtypeStruct((1, D), F32)],
        compiler_params=_cparams(("arbitrary",)),
    )(x, g.reshape(1, D), target)
    return se, dx, dg.reshape(D)


def _dot_nt(a, b):
    return lax.dot_general(a, b, (((1,), (1,)), ((), ())), preferred_element_type=F32)


def _dot_tn(a, b):
    return lax.dot_general(a, b, (((0,), (0,)), ((), ())), preferred_element_type=F32)


def _dot_nn(a, b):
    return lax.dot_general(a, b, (((1,), (0,)), ((), ())), preferred_element_type=F32)


class _Attn:
    def __init__(self, H, dq, dk1, dv, causal, scale, hp, hp_kv, blk=256):
        self.H, self.dq, self.dk1, self.dv, self.causal, self.scale, self.blk = H, dq, dk1, dv, causal, scale, blk
        self.hp, self.hp_kv = hp, hp_kv


def _cols(ref, rows, hh, width):
    return ref[rows, hh * width:(hh + 1) * width]


def _keys(cfg, k1_ref, k2_ref, rows, hh):
    ks = _cols(k1_ref, rows, hh, cfg.dk1)
    if k2_ref is not None:
        ks = jnp.concatenate([ks, k2_ref[rows, :]], axis=1)
    return ks


def _attn_specs(cfg, hp, t, Tk, has_k2, by_q):
    g = cfg.H // hp
    if by_q:
        specs = [pl.BlockSpec((t, hp * cfg.dq), lambda h, i: (i, h)),
                 pl.BlockSpec((Tk, hp * cfg.dk1), lambda h, i: (0, h)),
                 pl.BlockSpec((Tk, hp * cfg.dv), lambda h, i: (0, g + h))]
        if has_k2:
            specs.append(pl.BlockSpec((Tk, LANES), lambda h, i: (0, 0)))
    else:
        specs = [None,
                 pl.BlockSpec((t, hp * cfg.dk1), lambda j, h: (j, h)),
                 pl.BlockSpec((t, hp * cfg.dv), lambda j, h: (j, g + h))]
        if has_k2:
            specs.append(pl.BlockSpec((t, LANES), lambda j, h: (j, 0)))
    return specs


def _mask(s, cfg, i, j, t):
    if not cfg.causal:
        return s
    row = i * t + lax.broadcasted_iota(jnp.int32, s.shape, 0)
    col = j * t + lax.broadcasted_iota(jnp.int32, s.shape, 1)
    return jnp.where(row >= col, s, NEG)


def flash_fwd(cfg, q, k1, v, k2, name):
    Tq, Tk = q.shape[0], k1.shape[0]
    t = min(cfg.blk, Tq, Tk)
    nkb = Tk // t
    has_k2 = k2 is not None
    hp = cfg.hp

    def body(*refs):
        q_ref, k1_ref, v_ref = refs[:3]
        k2_ref = refs[3] if has_k2 else None
        o_ref, lse_ref = refs[-2], refs[-1]
        i = pl.program_id(1)
        qs = [_cols(q_ref, slice(None), hh, cfg.dq) for hh in range(hp)]

        def step(j, carry):
            rows = pl.ds(pl.multiple_of(j * t, t), t)
            out = []
            for hh in range(hp):
                m, l, acc = carry[hh]
                s = _mask(_dot_nt(qs[hh], _keys(cfg, k1_ref, k2_ref, rows, hh)) * cfg.scale, cfg, i, j, t)
                m2 = jnp.maximum(m, jnp.max(s, axis=-1, keepdims=True))
                p = jnp.exp(s - m2)
                alpha = jnp.exp(m - m2)
                l2 = alpha * l + jnp.sum(p, axis=-1, keepdims=True)
                acc2 = alpha * acc + _dot_nn(p.astype(BF16), _cols(v_ref, rows, hh, cfg.dv))
                out.append((m2, l2, acc2))
            return tuple(out)

        init = tuple((jnp.full((t, 1), NEG, F32), jnp.zeros((t, 1), F32), jnp.zeros((t, cfg.dv), F32)) for _ in range(hp))
        res = lax.fori_loop(0, (i + 1) if cfg.causal else nkb, step, init)
        for hh in range(hp):
            m, l, acc = res[hh]
            o_ref[:, hh * cfg.dv:(hh + 1) * cfg.dv] = (acc / l).astype(o_ref.dtype)
            lse_ref[hh] = m + jnp.log(l)

    args = [q, k1, v] + ([k2] if has_k2 else [])
    return pl.pallas_call(
        body, name=name, grid=(cfg.H // hp, Tq // t), in_specs=_attn_specs(cfg, hp, t, Tk, has_k2, True),
        out_specs=[pl.BlockSpec((t, hp * cfg.dv), lambda h, i: (i, h)), pl.BlockSpec((hp, t, 1), lambda h, i: (h, i, 0))],
        out_shape=[jax.ShapeDtypeStruct((Tq, cfg.H * cfg.dv), BF16), jax.ShapeDtypeStruct((cfg.H, Tq, 1), F32)],
        compiler_params=_cparams(("parallel", "parallel")),
    )(*args)


def flash_dq(cfg, q, k1, v, k2, o, do, lse, out_dtype, name):
    Tq, Tk = q.shape[0], k1.shape[0]
    t = min(cfg.blk, Tq, Tk)
    nkb = Tk // t
    has_k2 = k2 is not None
    hp = cfg.hp

    def body(*refs):
        q_ref, k1_ref, v_ref = refs[:3]
        k2_ref = refs[3] if has_k2 else None
        o_ref, do_ref, lse_ref, dq_ref, dl_ref = refs[-5:]
        i = pl.program_id(1)
        qs = [_cols(q_ref, slice(None), hh, cfg.dq) for hh in range(hp)]
        dos = [_cols(do_ref, slice(None), hh, cfg.dv) for hh in range(hp)]
        lses = [lse_ref[hh] for hh in range(hp)]
        deltas = []
        for hh in range(hp):
            d = jnp.sum(dos[hh].astype(F32) * _cols(o_ref, slice(None), hh, cfg.dv).astype(F32), axis=-1, keepdims=True)
            dl_ref[hh] = d
            deltas.append(d)

        def step(j, dqs):
            rows = pl.ds(pl.multiple_of(j * t, t), t)
            out = []
            for hh in range(hp):
                ks = _keys(cfg, k1_ref, k2_ref, rows, hh)
                s = _mask(_dot_nt(qs[hh], ks) * cfg.scale, cfg, i, j, t)
                p = jnp.exp(s - lses[hh])
                dp = _dot_nt(dos[hh], _cols(v_ref, rows, hh, cfg.dv))
                ds = p * (dp - deltas[hh]) * cfg.scale
                out.append(dqs[hh] + _dot_nn(ds.astype(BF16), ks))
            return tuple(out)

        dqs = lax.fori_loop(0, (i + 1) if cfg.causal else nkb, step, tuple(jnp.zeros((t, cfg.dq), F32) for _ in range(hp)))
        for hh in range(hp):
            dq_ref[:, hh * cfg.dq:(hh + 1) * cfg.dq] = dqs[hh].astype(dq_ref.dtype)

    ov = pl.BlockSpec((t, hp * cfg.dv), lambda h, i: (i, h))
    row1 = pl.BlockSpec((hp, t, 1), lambda h, i: (h, i, 0))
    args = [q, k1, v] + ([k2] if has_k2 else []) + [o, do, lse]
    return pl.pallas_call(
        body, name=name, grid=(cfg.H // hp, Tq // t), in_specs=_attn_specs(cfg, hp, t, Tk, has_k2, True) + [ov, ov, row1],
        out_specs=[pl.BlockSpec((t, hp * cfg.dq), lambda h, i: (i, h)), row1],
        out_shape=[jax.ShapeDtypeStruct((Tq, cfg.H * cfg.dq), out_dtype), jax.ShapeDtypeStruct((cfg.H, Tq, 1), F32)],
        compiler_params=_cparams(("parallel", "parallel")),
    )(*args)


def flash_dkv(cfg, q, k1, v, k2, do, lse, delta, out_dtype, name):
    Tq, Tk = q.shape[0], k1.shape[0]
    t = min(cfg.blk, Tq, Tk)
    nqb = Tq // t
    has_k2 = k2 is not None
    hp = cfg.hp_kv

    def body(*refs):
        q_ref, k1_ref, v_ref = refs[:3]
        k2_ref = refs[3] if has_k2 else None
        n_in = 4 if has_k2 else 3
        do_ref, lse_ref, dl_ref = refs[n_in:n_in + 3]
        dk1_ref, dv_ref = refs[n_in + 3], refs[n_in + 4]
        j, h = pl.program_id(0), pl.program_id(1)
        kss = [_keys(cfg, k1_ref, k2_ref, slice(None), hh) for hh in range(hp)]
        vss = [_cols(v_ref, slice(None), hh, cfg.dv) for hh in range(hp)]

        def step(i, carry):
            rows = pl.ds(pl.multiple_of(i * t, t), t)
            out = []
            for hh in range(hp):
                dk, dv = carry[hh]
                qi, doi = _cols(q_ref, rows, hh, cfg.dq), _cols(do_ref, rows, hh, cfg.dv)
                s = _mask(_dot_nt(qi, kss[hh]) * cfg.scale, cfg, i, j, t)
                p = jnp.exp(s - lse_ref[hh, rows, :])
                dv = dv + _dot_tn(p.astype(BF16), doi)
                ds = p * (_dot_nt(doi, vss[hh]) - dl_ref[hh, rows, :]) * cfg.scale
                dk = dk + _dot_tn(ds.astype(BF16), qi)
                out.append((dk, dv))
            return tuple(out)

        init = tuple((jnp.zeros((t, cfg.dq), F32), jnp.zeros((t, cfg.dv), F32)) for _ in range(hp))
        res = lax.fori_loop(j if cfg.causal else 0, nqb, step, init)
        for hh in range(hp):
            dk, dv = res[hh]
            dv_ref[:, hh * cfg.dv:(hh + 1) * cfg.dv] = dv.astype(dv_ref.dtype)
            dk1_ref[:, hh * cfg.dk1:(hh + 1) * cfg.dk1] = dk[:, 0:cfg.dk1].astype(dk1_ref.dtype)
        if has_k2:
            dk2_ref = refs[n_in + 5]

            @pl.when(h == 0)
            def _():
                dk2_ref[...] = jnp.zeros_like(dk2_ref)

            for hh in range(hp):
                dk2_ref[...] += res[hh][0][:, cfg.dk1:]

    specs = _attn_specs(cfg, hp, t, Tk, has_k2, False)
    specs[0] = pl.BlockSpec((Tq, hp * cfg.dq), lambda j, h: (0, h))
    rows_all = pl.BlockSpec((hp, Tq, 1), lambda j, h: (h, 0, 0))
    specs += [pl.BlockSpec((Tq, hp * cfg.dv), lambda j, h: (0, h)), rows_all, rows_all]
    args = [q, k1, v] + ([k2] if has_k2 else []) + [do, lse, delta]
    out_specs = [pl.BlockSpec((t, hp * cfg.dk1), lambda j, h: (j, h)), pl.BlockSpec((t, hp * cfg.dv), lambda j, h: (j, h))]
    out_shape = [jax.ShapeDtypeStruct((Tk, cfg.H * cfg.dk1), out_dtype), jax.ShapeDtypeStruct((Tk, cfg.H * cfg.dv), out_dtype)]
    if has_k2:
        out_specs.append(pl.BlockSpec((t, LANES), lambda j, h: (j, 0)))
        out_shape.append(jax.ShapeDtypeStruct((Tk, LANES), F32))
    return pl.pallas_call(
        body, name=name, grid=(Tk // t, cfg.H // hp), in_specs=specs, out_specs=out_specs, out_shape=out_shape,
        compiler_params=_cparams(("parallel", "arbitrary")),
    )(*args)


def _shift_down(x, s):
    if s == 0:
        return x
    t = lax.broadcasted_iota(jnp.int32, x.shape, 0)
    return jnp.where(t >= s, pltpu.roll(x, s, 0), 0.0)


def _shift_up(x, s):
    if s == 0:
        return x
    n = x.shape[0]
    t = lax.broadcasted_iota(jnp.int32, x.shape, 0)
    return jnp.where(t < n - s, pltpu.roll(x, n - s, 0), 0.0)


def _conv(x, w_ref, kw):
    y = x * w_ref[kw - 1:kw, :]
    for j in range(kw - 1):
        y = y + _shift_down(x, kw - 1 - j) * w_ref[j:j + 1, :]
    return y


def _conv_t(d, w_ref, kw):
    y = d * w_ref[kw - 1:kw, :]
    for j in range(kw - 1):
        y = y + _shift_up(d, kw - 1 - j) * w_ref[j:j + 1, :]
    return y


def _conv_dw(d, x, kw):
    rows = lax.broadcasted_iota(jnp.int32, (kw, d.shape[1]), 0)
    dw = jnp.zeros((kw, d.shape[1]), F32)
    for j in range(kw):
        r = jnp.sum(d * _shift_down(x, kw - 1 - j), axis=0, keepdims=True)
        dw = jnp.where(rows == j, r, dw)
    return dw


def _silu(x):
    return x * jax.nn.sigmoid(x)


def _silu_grad(x):
    s = jax.nn.sigmoid(x)
    return s * (1.0 + x * (1.0 - s))


def gdn_conv_fwd(z, w, name, tc=256):
    T, C = z.shape[0], w.shape[1]
    kw = w.shape[0]

    def body(x_ref, w_ref, o_ref):
        o_ref[...] = _silu(_conv(x_ref[...], w_ref, kw))

    return pl.pallas_call(
        body, name=name, grid=(C // tc,),
        in_specs=[pl.BlockSpec((T, tc), lambda j: (0, j)), pl.BlockSpec((kw, tc), lambda j: (0, j))],
        out_specs=pl.BlockSpec((T, tc), lambda j: (0, j)),
        out_shape=jax.ShapeDtypeStruct((T, C), F32), compiler_params=_cparams(("parallel",)),
    )(z, w)


def gdn_conv_bwd(z, w, dy, name, tc=256):
    T, C = z.shape[0], w.shape[1]
    kw = w.shape[0]

    def body(x_ref, w_ref, dy_ref, dx_ref, dw_ref):
        xv = x_ref[...]
        dc = dy_ref[...] * _silu_grad(_conv(xv, w_ref, kw))
        dx_ref[...] = _conv_t(dc, w_ref, kw).astype(dx_ref.dtype)
        dw_ref[...] = _conv_dw(dc, xv, kw)

    col = lambda j: (0, j)
    return pl.pallas_call(
        body, name=name, grid=(C // tc,),
        in_specs=[pl.BlockSpec((T, tc), col), pl.BlockSpec((kw, tc), col), pl.BlockSpec((T, tc), col)],
        out_specs=[pl.BlockSpec((T, tc), col), pl.BlockSpec((kw, tc), col)],
        out_shape=[jax.ShapeDtypeStruct((T, C), BF16), jax.ShapeDtypeStruct((kw, C), F32)],
        compiler_params=_cparams(("parallel",)),
    )(z, w, dy)


def sc_fwd(z, w, name, tc=256):
    T, C = z.shape[0], w.shape[1]
    kw, nb = w.shape[0], C // tc

    def body(b_ref, c_ref, u_ref, w_ref, o_ref):
        o_ref[...] = (b_ref[...] * _conv(c_ref[...] * u_ref[...], w_ref, kw)).astype(o_ref.dtype)

    return pl.pallas_call(
        body, name=name, grid=(nb,),
        in_specs=[pl.BlockSpec((T, tc), lambda j: (0, j)), pl.BlockSpec((T, tc), lambda j: (0, nb + j)),
                  pl.BlockSpec((T, tc), lambda j: (0, 2 * nb + j)), pl.BlockSpec((kw, tc), lambda j: (0, j))],
        out_specs=pl.BlockSpec((T, tc), lambda j: (0, j)),
        out_shape=jax.ShapeDtypeStruct((T, C), BF16), compiler_params=_cparams(("parallel",)),
    )(z, z, z, w)


def sc_bwd(z, w, dy, name, tc=256):
    T, C = z.shape[0], w.shape[1]
    kw, nb = w.shape[0], C // tc

    def body(b_ref, c_ref, u_ref, w_ref, dy_ref, db_ref, dc_ref, du_ref, dw_ref):
        cv, uv, dyv = c_ref[...], u_ref[...], dy_ref[...]
        cu = cv * uv
        db_ref[...] = (dyv * _conv(cu, w_ref, kw)).astype(db_ref.dtype)
        dcv = dyv * b_ref[...]
        dcu = _conv_t(dcv, w_ref, kw)
        dc_ref[...] = (dcu * uv).astype(dc_ref.dtype)
        du_ref[...] = (dcu * cv).astype(du_ref.dtype)
        dw_ref[...] = _conv_dw(dcv, cu, kw)

    col = lambda j: (0, j)
    act = jax.ShapeDtypeStruct((T, C), BF16)
    return pl.pallas_call(
        body, name=name, grid=(nb,),
        in_specs=[pl.BlockSpec((T, tc), col), pl.BlockSpec((T, tc), lambda j: (0, nb + j)),
                  pl.BlockSpec((T, tc), lambda j: (0, 2 * nb + j)), pl.BlockSpec((kw, tc), col), pl.BlockSpec((T, tc), col)],
        out_specs=[pl.BlockSpec((T, tc), col)] * 3 + [pl.BlockSpec((kw, tc), col)],
        out_shape=[act, act, act, jax.ShapeDtypeStruct((kw, C), F32)],
        compiler_params=_cparams(("parallel",)),
    )(z, z, z, w, dy)


def _hdot(a, b, dims):
    return lax.dot_general(a, b, (dims, ((), ())), precision=HI, preferred_element_type=F32)


def _bdot(a, b, dims):
    return lax.dot_general(a.astype(BF16), b.astype(BF16), (dims, ((), ())), preferred_element_type=F32)


_NN, _NT, _TN = ((1,), (0,)), ((1,), (1,)), ((0,), (0,))


def _per_head_dots(dot2d):
    def stacked(a, b, dims):
        return jnp.stack([dot2d(a[h], b[h], dims) for h in range(a.shape[0])])

    @jax.custom_vjp
    def nn(a, b):
        return stacked(a, b, _NN)

    @jax.custom_vjp
    def nt(a, b):
        return stacked(a, b, _NT)

    @jax.custom_vjp
    def tn(a, b):
        return stacked(a, b, _TN)

    nn.defvjp(lambda a, b: (nn(a, b), (a, b)), lambda r, d: (stacked(d, r[1], _NT), stacked(r[0], d, _TN)))
    nt.defvjp(lambda a, b: (nt(a, b), (a, b)), lambda r, d: (stacked(d, r[1], _NN), stacked(d, r[0], _TN)))
    tn.defvjp(lambda a, b: (tn(a, b), (a, b)), lambda r, d: (stacked(r[1], d, _NT), stacked(r[0], d, _NN)))
    return nn, nt, tn


_hnn, _hnt, _htn = _per_head_dots(_hdot)
_bnn, _bnt, _btn = _per_head_dots(_bdot)


@jax.custom_vjp
def _unit_lower_inverse(m):
    c = m.shape[-1]
    eye = (lax.broadcasted_iota(jnp.int32, (c, c), 0) == lax.broadcasted_iota(jnp.int32, (c, c), 1)).astype(F32)
    t = eye - m
    p = _hnn(m, m)
    n = 2
    while n < c:
        t = t + _hnn(t, p)
        n *= 2
        if n < c:
            p = _hnn(p, p)
    return t


def _uli_fwd(m):
    t = _unit_lower_inverse(m)
    return t, t


def _uli_bwd(t, dt):
    return (-_htn(t, _hnt(dt, t)),)


_unit_lower_inverse.defvjp(_uli_fwd, _uli_bwd)


def _gdn_chunk(q, k, v, gate, bl, al, a_log, dt_bias, o_norm, st):
    nh, c = q.shape[0], q.shape[1]
    ii = lax.broadcasted_iota(jnp.int32, (c, c), 0)
    jj = lax.broadcasted_iota(jnp.int32, (c, c), 1)
    tri, strict = ii >= jj, ii > jj
    q = q * lax.rsqrt(jnp.sum(q * q, -1, keepdims=True) + EPS) * (GDN_D ** -0.5)
    k = k * lax.rsqrt(jnp.sum(k * k, -1, keepdims=True) + EPS)
    beta = jax.nn.sigmoid(bl)
    g = -jnp.exp(a_log) * jax.nn.softplus(al + dt_bias)
    gc = _hnn(jnp.broadcast_to(tri.astype(F32), (nh, c, c)), g)
    gcol = _hnn(gc, jnp.full((nh, LANES, c), 1.0 / LANES, F32))
    grow = _hnt(jnp.full((nh, c, LANES), 1.0 / LANES, F32), gc)
    decay = jnp.where(tri, jnp.exp(jnp.where(tri, gcol - grow, 0.0)), 0.0)
    kb = k * beta
    m = jnp.where(strict, _bnt(kb, k) * decay, 0.0)
    t_inv = _unit_lower_inverse(m)
    eg = jnp.exp(gc)
    u = _bnn(t_inv, v * beta)
    w = _bnn(t_inv, kb * eg)
    attn = _bnt(q, k) * decay
    v_new = u - _bnn(w, st)
    o = _bnn(q * eg, st) + _bnn(attn, v_new)
    g_last = jnp.sum(g, axis=1, keepdims=True)
    st_new = st * jnp.exp(g_last) + _btn(k * jnp.exp(g_last - gc), v_new)
    o = o * lax.rsqrt(jnp.mean(o * o, -1, keepdims=True) + EPS) * o_norm
    return o * _silu(gate), st_new


GDN_HP = 8
_GW = GDN_HP * GDN_D
_GB = GDN_H // GDN_HP


def _gdn_specs(n_chunks, rev):
    def tok(col):
        if rev:
            return pl.BlockSpec((GDN_C, _GW), lambda h, n: (n_chunks - 1 - n, col + h))
        return pl.BlockSpec((GDN_C, _GW), lambda h, n: (n, col + h))
    par = pl.BlockSpec((1, _GW), lambda h, n: (0, h))
    shared = pl.BlockSpec((1, GDN_D), lambda h, n: (0, 0))
    if rev:
        st = pl.BlockSpec((GDN_HP, None, GDN_D, GDN_D), lambda h, n: (h, n_chunks - 1 - n, 0, 0))
    else:
        st = pl.BlockSpec((GDN_HP, None, GDN_D, GDN_D), lambda h, n: (h, n, 0, 0))
    return tok, par, shared, st


def _heads(ref):
    return jnp.stack([ref[:, h * GDN_D:(h + 1) * GDN_D] for h in range(ref.shape[1] // GDN_D)])


def gdn_chunk_fwd(qkv, z, a_log_x, dt_bias_x, o_norm, name):
    T = qkv.shape[0]
    n_chunks = T // GDN_C
    H = GDN_H
    tok, par, shared, st_spec = _gdn_specs(n_chunks, False)

    def body(q_ref, k_ref, v_ref, g_ref, bl_ref, al_ref, a_ref, dt_ref, on_ref, o_ref, st_ref, state):
        @pl.when(pl.program_id(1) == 0)
        def _():
            state[...] = jnp.zeros_like(state)

        st = state[...]
        st_ref[...] = st
        o, st_new = _gdn_chunk(_heads(q_ref), _heads(k_ref), _heads(v_ref), _heads(g_ref), _heads(bl_ref), _heads(al_ref),
                               _heads(a_ref), _heads(dt_ref), on_ref[...], st)
        for hh in range(GDN_HP):
            o_ref[:, hh * GDN_D:(hh + 1) * GDN_D] = o[hh].astype(o_ref.dtype)
        state[...] = st_new

    B = _GB
    return pl.pallas_call(
        body, name=name, grid=(B, n_chunks),
        in_specs=[tok(0), tok(B), tok(2 * B), tok(3 * B), tok(4 * B), tok(5 * B), par, par, shared],
        out_specs=[tok(0), st_spec],
        out_shape=[jax.ShapeDtypeStruct((T, H * GDN_D), BF16), jax.ShapeDtypeStruct((H, n_chunks, GDN_D, GDN_D), F32)],
        scratch_shapes=[pltpu.VMEM((GDN_HP, GDN_D, GDN_D), F32)],
        compiler_params=_cparams(("parallel", "arbitrary")),
    )(qkv, qkv, qkv, z, z, z, a_log_x, dt_bias_x, o_norm)


def gdn_chunk_bwd(qkv, z, a_log_x, dt_bias_x, o_norm, states, do, name):
    T = qkv.shape[0]
    n_chunks = T // GDN_C
    H = GDN_H
    tok, par, shared, st_spec = _gdn_specs(n_chunks, True)

    def body(q_ref, k_ref, v_ref, g_ref, bl_ref, al_ref, a_ref, dt_ref, on_ref, st_ref, do_ref,
             dq_ref, dk_ref, dv_ref, dg_ref, dbl_ref, dal_ref, da_ref, ddt_ref, don_ref, dstate):
        h, n = pl.program_id(0), pl.program_id(1)

        @pl.when(n == 0)
        def _():
            dstate[...] = jnp.zeros_like(dstate)
            da_ref[...] = jnp.zeros_like(da_ref)
            ddt_ref[...] = jnp.zeros_like(ddt_ref)

        @pl.when((n == 0) & (h == 0))
        def _():
            don_ref[...] = jnp.zeros_like(don_ref)

        _, vjp = jax.vjp(_gdn_chunk, _heads(q_ref), _heads(k_ref), _heads(v_ref), _heads(g_ref), _heads(bl_ref), _heads(al_ref),
                         _heads(a_ref), _heads(dt_ref), on_ref[...], st_ref[...])
        dq, dk, dv, dg, dbl, dal, da, ddt, don, dst = vjp((_heads(do_ref).astype(F32), dstate[...]))
        for hh in range(GDN_HP):
            cols = slice(hh * GDN_D, (hh + 1) * GDN_D)
            dq_ref[:, cols] = dq[hh]
            dk_ref[:, cols] = dk[hh]
            dv_ref[:, cols] = dv[hh]
            dg_ref[:, cols] = dg[hh].astype(dg_ref.dtype)
            dbl_ref[:, cols] = dbl[hh].astype(dbl_ref.dtype)
            dal_ref[:, cols] = dal[hh].astype(dal_ref.dtype)
            da_ref[:, cols] += da[hh]
            ddt_ref[:, cols] += ddt[hh]
        don_ref[...] += don
        dstate[...] = dst

    tok0 = tok(0)
    B = _GB
    f32_tok = jax.ShapeDtypeStruct((T, H * GDN_D), F32)
    bf_tok = jax.ShapeDtypeStruct((T, H * GDN_D), BF16)
    par_sh = jax.ShapeDtypeStruct((1, H * GDN_D), F32)
    return pl.pallas_call(
        body, name=name, grid=(B, n_chunks),
        in_specs=[tok(0), tok(B), tok(2 * B), tok(3 * B), tok(4 * B), tok(5 * B), par, par, shared, st_spec, tok0],
        out_specs=[tok0] * 6 + [par, par, shared],
        out_shape=[f32_tok, f32_tok, f32_tok, bf_tok, bf_tok, bf_tok, par_sh, par_sh, jax.ShapeDtypeStruct((1, GDN_D), F32)],
        scratch_shapes=[pltpu.VMEM((GDN_HP, GDN_D, GDN_D), F32)],
        compiler_params=_cparams(("arbitrary", "arbitrary")),
    )(qkv, qkv, qkv, z, z, z, a_log_x, dt_bias_x, o_norm, states, do)


def _prefetch_call(body, name, grid, in_specs, out_specs, out_shape, aliases=None):
    return pl.pallas_call(
        body, name=name,
        grid_spec=pltpu.PrefetchScalarGridSpec(num_scalar_prefetch=1, grid=grid, in_specs=in_specs, out_specs=out_specs),
        out_shape=out_shape, input_output_aliases=aliases or {},
        compiler_params=_cparams(("parallel",) * len(grid)))


def cast_into(src, src_row0, rows, slab, row0, me, name):
    width = src.shape[1]
    tr = _pick_rows(rows)
    assert row0 % tr == 0 and src_row0 % tr == 0

    def body(me_ref, s_ref, *refs):
        refs[-1][...] = s_ref[...].astype(refs[-1].dtype)

    in_specs = [pl.BlockSpec((tr, width), lambda r, me_ref: (src_row0 // tr + r, 0))]
    args = [src]
    aliases = {}
    if slab.arr is not None:
        in_specs.append(_ANY)
        args.append(slab.arr)
        aliases = {2: 0}
    slab.arr = _prefetch_call(
        body, name, (rows // tr,), in_specs,
        pl.BlockSpec((None, tr, width), lambda r, me_ref: (me_ref[0], row0 // tr + r, 0)),
        jax.ShapeDtypeStruct(slab.shape, slab.dtype), aliases)(me, *args)


def pair_add(g, b, c_idx, name):
    n, rh, w = b.shape
    tr = _pick_rows(rh)
    nb = rh // tr

    def body(c_ref, g_ref, b_ref, o_ref):
        o_ref[...] = (g_ref[...].astype(F32) + b_ref[...].astype(F32)).astype(o_ref.dtype)

    return _prefetch_call(
        body, name, (n, nb),
        [pl.BlockSpec((None, tr, w), lambda k, r, c: (k, c[0] * nb + r, 0)), pl.BlockSpec((None, tr, w), lambda k, r, c: (k, r, 0))],
        pl.BlockSpec((None, tr, w), lambda k, r, c: (k, r, 0)), jax.ShapeDtypeStruct(b.shape, BF16))(c_idx, g, b)


def chip_sum(p, rv, mc, name):
    n, rh, w = p.shape
    tr = _pick_rows(rh)
    nb = rh // tr

    def body(mc_ref, p_ref, rv_ref, o_ref):
        me = mc_ref[0]
        acc = None
        for k in range(n):
            part = jnp.where(me == k, p_ref[...], rv_ref[k]).astype(F32)
            acc = part if acc is None else acc + part
        o_ref[...] = acc

    return _prefetch_call(
        body, name, (nb,),
        [pl.BlockSpec((None, tr, w), lambda r, mc_ref: (mc_ref[0], r, 0)), pl.BlockSpec((n, tr, w), lambda r, mc_ref: (0, r, 0))],
        pl.BlockSpec((tr, w), lambda r, mc_ref: (mc_ref[1] * nb + r, 0)), jax.ShapeDtypeStruct((2 * rh, w), F32))(mc, p, rv)


def adamw(red, row0, w, m, v, w_row0, rows, prev, name):
    cols = w.shape[1]
    tr = _pick_rows(rows)
    assert row0 % tr == 0 and w_row0 % tr == 0

    def body(g_ref, w_ref, m_ref, v_ref, *refs):
        go_ref, d_ref, nm_ref, nv_ref = refs[-4:]
        gv = g_ref[...]
        nm = ADAM_B1 * m_ref[...] + (1.0 - ADAM_B1) * gv
        nv = ADAM_B2 * v_ref[...] + (1.0 - ADAM_B2) * (gv * gv)
        m_hat = nm / (1.0 - ADAM_B1 ** ADAM_STEP)
        v_hat = nv / (1.0 - ADAM_B2 ** ADAM_STEP)
        go_ref[...] = gv
        d_ref[...] = -ADAM_LR * (m_hat / (jnp.sqrt(v_hat) + ADAM_EPS) + ADAM_WD * w_ref[...])
        nm_ref[...] = nm
        nv_ref[...] = nv

    spec = pl.BlockSpec((tr, cols), lambda r: (w_row0 // tr + r, 0))
    sh = jax.ShapeDtypeStruct(w.shape, F32)
    in_specs = [pl.BlockSpec((tr, cols), lambda r: (row0 // tr + r, 0)), spec, spec, spec]
    args, aliases = [red, w, m, v], {}
    if prev is not None:
        in_specs += [_ANY] * 4
        args += list(prev)
        aliases = {4 + k: k for k in range(4)}
    return pl.pallas_call(
        body, name=name, grid=(rows // tr,), in_specs=in_specs, out_specs=[spec] * 4, out_shape=[sh] * 4,
        input_output_aliases=aliases, compiler_params=_cparams(("parallel",)),
    )(*args)


def _place():
    x, y, c = lax.axis_index("x"), lax.axis_index("y"), lax.axis_index("c")
    chips = [(1 - x, y), (x, 1 - y), (1 - x, 1 - y)]
    return x, y, c, chips


def _chip_index(cx, cy):
    return 2 * cx + cy


def _remote(src, dst, send_sem, recv_sem, to):
    return pltpu.make_async_remote_copy(src_ref=src, dst_ref=dst, send_sem=send_sem, recv_sem=recv_sem,
                                        device_id=to, device_id_type=MESH)


def _comm_call(body, name, ins, out_shapes, n_sems, aliases):
    return pl.pallas_call(
        body, name=name, in_specs=[_ANY] * len(ins), out_specs=[_ANY] * len(out_shapes), out_shape=out_shapes,
        scratch_shapes=[pltpu.SemaphoreType.DMA((n_sems,)), pltpu.SemaphoreType.DMA((n_sems,))],
        input_output_aliases=aliases,
    )(*ins)


def gather_slabs(slabs, name="weight_all_gather"):
    n = len(slabs)

    def body(*refs):
        in_refs, out_refs, send_sems, recv_sems = refs[:n], refs[n:2 * n], refs[-2], refs[-1]
        x, y, c, chips = _place()
        me = _chip_index(x, y)
        sib = (x, y, 1 - c)
        first, passed = [], []
        for a in range(n):
            rh = in_refs[a].shape[1] // 2
            mine = pl.ds(c * rh, rh)
            for j, chip in enumerate(chips):
                cp = _remote(in_refs[a].at[me, mine], out_refs[a].at[me, mine], send_sems.at[6 * a + j],
                             recv_sems.at[6 * a + j], (*chip, c))
                cp.start()
                first.append(cp)
        for a in range(n):
            rh = in_refs[a].shape[1] // 2
            mine = pl.ds(c * rh, rh)
            for j, chip in enumerate(chips):
                landed = out_refs[a].at[_chip_index(*chip), mine]
                _remote(landed, landed, send_sems.at[6 * a + j], recv_sems.at[6 * a + j], (*chip, c)).wait_recv()
                cp = _remote(landed, landed, send_sems.at[6 * a + 3 + j], recv_sems.at[6 * a + 3 + j], sib)
                cp.start()
                passed.append(cp)
        for a in range(n):
            rh = in_refs[a].shape[1] // 2
            theirs = pl.ds((1 - c) * rh, rh)
            for j, chip in enumerate(chips):
                got = out_refs[a].at[_chip_index(*chip), theirs]
                _remote(got, got, send_sems.at[6 * a + 3 + j], recv_sems.at[6 * a + 3 + j], sib).wait_recv()
        for cp in first + passed:
            cp.wait_send()

    return _comm_call(body, name, slabs, [jax.ShapeDtypeStruct(s.shape, s.dtype) for s in slabs], 6 * n,
                      {a: a for a in range(n)})


def pair_swap_halves(slabs, name="grad_pair_swap"):
    n = len(slabs)

    def body(*refs):
        in_refs, out_refs, send_sems, recv_sems = refs[:n], refs[n:2 * n], refs[-2], refs[-1]
        x, y, c, _ = _place()
        cps = []
        for a in range(n):
            rh = in_refs[a].shape[1] // 2
            cp = _remote(in_refs[a].at[:, pl.ds((1 - c) * rh, rh), :], out_refs[a], send_sems.at[a], recv_sems.at[a], (x, y, 1 - c))
            cp.start()
            cps.append(cp)
        for cp in cps:
            cp.wait()

    outs = [jax.ShapeDtypeStruct((s.shape[0], s.shape[1] // 2, s.shape[2]), s.dtype) for s in slabs]
    return _comm_call(body, name, slabs, outs, n, {})


def chip_exchange(parts, name="grad_chip_exchange"):
    n = len(parts)

    def body(*refs):
        in_refs, out_refs, send_sems, recv_sems = refs[:n], refs[n:2 * n], refs[-2], refs[-1]
        x, y, c, chips = _place()
        me = _chip_index(x, y)
        sends = []
        for a in range(n):
            for j, chip in enumerate(chips):
                cp = _remote(in_refs[a].at[_chip_index(*chip)], out_refs[a].at[me], send_sems.at[3 * a + j],
                             recv_sems.at[3 * a + j], (*chip, c))
                cp.start()
                sends.append(cp)
        for a in range(n):
            for j, chip in enumerate(chips):
                got = out_refs[a].at[_chip_index(*chip)]
                _remote(got, got, send_sems.at[3 * a + j], recv_sems.at[3 * a + j], (*chip, c)).wait_recv()
        for cp in sends:
            cp.wait_send()

    return _comm_call(body, name, parts, [jax.ShapeDtypeStruct(p.shape, p.dtype) for p in parts], 3 * n, {})


def pair_join_halves(reds, name="grad_pair_join"):
    n = len(reds)

    def body(*refs):
        in_refs, out_refs, send_sems, recv_sems = refs[:n], refs[n:2 * n], refs[-2], refs[-1]
        x, y, c, _ = _place()
        cps = []
        for a in range(n):
            rh = in_refs[a].shape[0] // 2
            mine = pl.ds(c * rh, rh)
            cp = _remote(in_refs[a].at[mine], out_refs[a].at[mine], send_sems.at[a], recv_sems.at[a], (x, y, 1 - c))
            cp.start()
            cps.append(cp)
        for a in range(n):
            rh = in_refs[a].shape[0] // 2
            got = out_refs[a].at[pl.ds((1 - c) * rh, rh)]
            _remote(got, got, send_sems.at[a], recv_sems.at[a], (x, y, 1 - c)).wait_recv()
        for cp in cps:
            cp.wait_send()

    return _comm_call(body, name, reds, [jax.ShapeDtypeStruct(r.shape, r.dtype) for r in reds], n, {a: a for a in range(n)})


_HBM = pl.BlockSpec(memory_space=pltpu.HBM)
_SEM = pl.BlockSpec(memory_space=pltpu.SEMAPHORE)
_EFFECT = pltpu.SideEffectType.DATAFLOW_SIDE_EFFECTING


def _in_hbm(a):
    return pltpu.with_memory_space_constraint(a, pltpu.HBM)


def _hbm_like(a):
    return pltpu.HBM(a.shape, a.dtype)


def _start_call(body, name, ins, n_sems, after):
    n = len(ins)
    res = pl.pallas_call(
        body, name=name, in_specs=[_HBM] * n + [_ANY],
        out_specs=[_SEM, _SEM] + [_HBM] * n + [pl.BlockSpec(memory_space=pltpu.VMEM)],
        out_shape=[pltpu.SemaphoreType.DMA((n_sems,)), pltpu.SemaphoreType.DMA((n_sems,))] + [_hbm_like(a) for a in ins]
        + [jax.ShapeDtypeStruct((8, LANES), F32)],
        input_output_aliases={a: 2 + a for a in range(n)},
        compiler_params=pltpu.CompilerParams(has_side_effects=_EFFECT),
    )(*[_in_hbm(a) for a in ins], after)
    return res[0], res[1], list(res[2:2 + n]), res[-1]


def _wait_call(body, name, thru, send_sems, recv_sems, after):
    n = len(thru)
    return pl.pallas_call(
        body, name=name, in_specs=[_HBM] * n + [_SEM, _SEM, _ANY], out_specs=[_HBM] * n,
        out_shape=[_hbm_like(a) for a in thru], input_output_aliases={a: a for a in range(n)},
        compiler_params=pltpu.CompilerParams(has_side_effects=_EFFECT),
    )(*thru, send_sems, recv_sems, after)


def gather_start(slabs, after, name="weight_gather_start"):
    n = len(slabs)

    def body(*refs):
        g_refs, send_sems, recv_sems, token = refs[:n], refs[n + 1], refs[n + 2], refs[-1]
        x, y, c, chips = _place()
        me = _chip_index(x, y)
        for a in range(n):
            rh = g_refs[a].shape[1] // 2
            mine = g_refs[a].at[me, pl.ds(c * rh, rh)]
            for j, chip in enumerate(chips):
                _remote(mine, mine, send_sems.at[3 * a + j], recv_sems.at[3 * a + j], (*chip, c)).start()
        token[...] = jnp.zeros_like(token)

    return _start_call(body, name, slabs, 3 * n, after)


def gather_wait(send_sems, recv_sems, thru, after, name="weight_gather_wait"):
    n = len(thru)

    def body(*refs):
        g_refs, send_sems, recv_sems = refs[:n], refs[n], refs[n + 1]
        x, y, c, chips = _place()
        me = _chip_index(x, y)
        for a in range(n):
            rh = g_refs[a].shape[1] // 2
            rows = pl.ds(c * rh, rh)
            for j, chip in enumerate(chips):
                mine, got = g_refs[a].at[me, rows], g_refs[a].at[_chip_index(*chip), rows]
                _remote(mine, mine, send_sems.at[3 * a + j], recv_sems.at[3 * a + j], (*chip, c)).wait_send()
                _remote(got, got, send_sems.at[3 * a + j], recv_sems.at[3 * a + j], (*chip, c)).wait_recv()

    return _wait_call(body, name, thru, send_sems, recv_sems, after)


def gather_forward(slabs, name="weight_gather_forward"):
    n = len(slabs)

    def body(*refs):
        in_refs, out_refs, send_sems, recv_sems = refs[:n], refs[n:2 * n], refs[-2], refs[-1]
        x, y, c, chips = _place()
        sib = (x, y, 1 - c)
        sends = []
        for a in range(n):
            rh = in_refs[a].shape[1] // 2
            for j, chip in enumerate(chips):
                k = _chip_index(*chip)
                cp = _remote(in_refs[a].at[k, pl.ds(c * rh, rh)], out_refs[a].at[k, pl.ds(c * rh, rh)], send_sems.at[3 * a + j],
                             recv_sems.at[3 * a + j], sib)
                cp.start()
                sends.append(cp)
        for a in range(n):
            rh = in_refs[a].shape[1] // 2
            for j, chip in enumerate(chips):
                got = out_refs[a].at[_chip_index(*chip), pl.ds((1 - c) * rh, rh)]
                _remote(got, got, send_sems.at[3 * a + j], recv_sems.at[3 * a + j], sib).wait_recv()
        for cp in sends:
            cp.wait_send()

    return _comm_call(body, name, slabs, [jax.ShapeDtypeStruct(s.shape, s.dtype) for s in slabs], 3 * n, {a: a for a in range(n)})


def exchange_start(parts, after, name="grad_exchange_start"):
    n = len(parts)

    def body(*refs):
        p_refs, land_refs, send_sems, recv_sems, token = refs[:n], refs[n:2 * n], refs[2 * n + 1], refs[2 * n + 2], refs[-1]
        x, y, c, chips = _place()
        me = _chip_index(x, y)
        for a in range(n):
            for j, chip in enumerate(chips):
                _remote(p_refs[a].at[_chip_index(*chip)], land_refs[a].at[me], send_sems.at[3 * a + j], recv_sems.at[3 * a + j],
                        (*chip, c)).start()
        token[...] = jnp.zeros_like(token)

    return _start_call(body, name, list(parts) + [lax.empty(p.shape, p.dtype) for p in parts], 3 * n, after)


def exchange_wait(send_sems, recv_sems, thru, after, name="grad_exchange_wait"):
    n = len(thru) // 2

    def body(*refs):
        p_refs, land_refs, send_sems, recv_sems = refs[:n], refs[n:2 * n], refs[2 * n], refs[2 * n + 1]
        x, y, c, chips = _place()
        me = _chip_index(x, y)
        for a in range(n):
            for j, chip in enumerate(chips):
                k = _chip_index(*chip)
                _remote(p_refs[a].at[k], land_refs[a].at[me], send_sems.at[3 * a + j], recv_sems.at[3 * a + j], (*chip, c)).wait_send()
                _remote(land_refs[a].at[k], land_refs[a].at[k], send_sems.at[3 * a + j], recv_sems.at[3 * a + j], (*chip, c)).wait_recv()

    res = _wait_call(body, name, thru, send_sems, recv_sems, after)
    return res[:n], res[n:]


_SLABS = {
    "mla_w_in": [("mla_w_in", 1, 0, 2)], "mla_w_uq": [("mla_w_uq", 2, 0, 2)], "mla_w_ukv": [("mla_w_ukv", 2, 0, 2)],
    "l0_mla_w_o": [("mla_w_o", 1, 0, 1)],
    "l0_w1024": [("mlp_w1", 2, 0, 1), ("mlp_w2", 1, 0, 1), ("xa_w_q", 1, 0, 1), ("xa_w_o", 1, 0, 1)],
    "l0_xa_w_kv": [("xa_w_kv", 2, 0, 1)],
    "l1_w1024": [("mlp_w1", 2, 1, 2), ("mlp_w2", 1, 1, 2), ("xa_w_q", 1, 1, 2), ("xa_w_o", 1, 1, 2), ("gdn_w_o", 1, 0, 1)],
    "l1_xa_w_kv": [("xa_w_kv", 2, 1, 2)], "gdn_w_in": [("gdn_w_in", 2, 0, 1)],
    "l23_w1024": [("mlp_w1", 2, 2, 4), ("mlp_w2", 1, 2, 4), ("xa_w_q", 1, 2, 4), ("xa_w_o", 1, 2, 4), ("mla_w_o", 1, 1, 2),
                  ("sc_w_o", 1, 0, 1)],
    "l23_xa_w_kv": [("xa_w_kv", 2, 2, 4)], "sc_w_in": [("sc_w_in", 2, 0, 1)],
}
_GROUPS = [(["mla_w_in", "mla_w_uq", "mla_w_ukv", "l0_mla_w_o"], None),
           (["l0_w1024", "l0_xa_w_kv"], (0, "xa")),
           (["l1_w1024", "l1_xa_w_kv", "gdn_w_in"], (1, "mix")),
           (["l23_w1024", "l23_xa_w_kv", "sc_w_in"], (2, "mix"))]
_RELAID = ("mla_w_in", "mla_w_uq", "mla_w_ukv", "gdn_w_in")
_SMALL = [("mla_q_norm", 1), ("mla_kv_norm", 1), ("gdn_conv_w", 2), ("sc_conv_w", 2)]
_REPL = ["gdn_a_log", "gdn_dt_bias", "gdn_o_norm", "norm_mix", "norm_mem", "norm_mlp", "mem_norm", "final_norm"]
_WEIGHTS = ['mla_w_in', 'mla_q_norm', 'mla_kv_norm', 'mla_w_uq', 'mla_w_ukv', 'mla_w_o', 'gdn_w_in', 'gdn_conv_w',
            'gdn_a_log', 'gdn_dt_bias', 'gdn_o_norm', 'gdn_w_o', 'sc_w_in', 'sc_conv_w', 'sc_w_o', 'norm_mix',
            'norm_mem', 'norm_mlp', 'xa_w_q', 'xa_w_kv', 'xa_w_o', 'mlp_w1', 'mlp_w2', 'mem_norm', 'final_norm']


class Layout:
    def __init__(self, shard_shapes):
        self.members, self.where, self.slab_dims = {}, {}, {}
        for slab, members in _SLABS.items():
            off, rows = 0, []
            for name, axis, l0, l1 in members:
                _, rpl, width = shard_shapes[name]
                rows.append((name, off, l0, l1, rpl))
                for layer in range(l0, l1):
                    self.where[(name, layer)] = (slab, off + (layer - l0) * rpl, rpl, width, axis)
                off += (l1 - l0) * rpl
            self.members[slab], self.slab_dims[slab] = rows, (off, width)

    def new_slabs(self, dtype):
        return {s: Slab(rows, width, dtype) for s, (rows, width) in self.slab_dims.items()}

    def loc(self, slabs, name, layer):
        slab, row0, rpl, width, axis = self.where[(name, layer)]
        if axis == 1:
            return Loc(slabs[slab], row0, N_CHIPS * rpl, width, 0)
        return Loc(slabs[slab], row0, rpl, N_CHIPS * width, 1)

    def _whole(self, name):
        (member,) = self.members[name]
        _, off, l0, l1, rpl = member
        assert off == 0 and l0 == 0
        return l1, rpl, self.slab_dims[name][1], dict((n, a) for n, a, _, _ in _SLABS[name])[name]

    def full(self, slabs, name):
        layers, rpl, width, axis = self._whole(name)
        blocks = slabs[name].arr.reshape(N_CHIPS, layers, rpl, width)
        return jnp.concatenate([blocks[s] for s in range(N_CHIPS)], axis=axis)

    def put_full(self, slabs, name, grad):
        layers, rpl, width, axis = self._whole(name)
        parts = jnp.stack(jnp.split(grad, N_CHIPS, axis=axis)).reshape(N_CHIPS, layers * rpl, width)
        slabs[name].arr = parts.astype(slabs[name].dtype)


def _small_pack(vals, names):
    flat = jnp.concatenate([vals[n].astype(F32).reshape(-1) for n in names])
    return jnp.pad(flat, (0, SMALL_ROWS * SMALL_COLS - flat.shape[0])).reshape(SMALL_ROWS, SMALL_COLS)


def _small_unpack(flat, like, names):
    out, off = {}, 0
    flat = flat.reshape(-1)
    for n in names:
        out[n] = flat[off:off + like[n].size].reshape(like[n].shape)
        off += like[n].size
    return out


_MLA_CFG = _Attn(MLA_H, 2 * LANES, MLA_NOPE, MLA_V, True, (MLA_NOPE + MLA_ROPE) ** -0.5, hp=8, hp_kv=4)
_XA_CFG = _Attn(XA_H, XA_D, XA_D, XA_D, False, XA_D ** -0.5, hp=4, hp_kv=4)


def _mla_weights(w_in, w_uq, w_ukv):
    w_in_p = jnp.pad(w_in, ((0, 0), (0, MLA_ZPAD - w_in.shape[1])))
    w_uq_p = jnp.pad(w_uq.reshape(MLA_QR, MLA_H, MLA_NOPE + MLA_ROPE), ((0, 0), (0, 0), (0, 2 * LANES - MLA_NOPE - MLA_ROPE)))
    w_uq_p = w_uq_p.reshape(MLA_QR, MLA_H * 2 * LANES)
    kv = w_ukv.reshape(MLA_KVR, MLA_H, MLA_NOPE + MLA_V)
    w_ukv_p = jnp.concatenate([kv[:, :, :MLA_NOPE].reshape(MLA_KVR, -1), kv[:, :, MLA_NOPE:].reshape(MLA_KVR, -1)], axis=1)
    return w_in_p, w_uq_p, w_ukv_p


def _mla_weight_grads(d_in_p, d_uq_p, d_ukv_p):
    d_in = d_in_p[:, :MLA_QR + MLA_KVR + MLA_ROPE]
    d_uq = d_uq_p.reshape(MLA_QR, MLA_H, 2 * LANES)[:, :, :MLA_NOPE + MLA_ROPE].reshape(MLA_QR, -1)
    half = MLA_H * MLA_NOPE
    d_ukv = jnp.concatenate([d_ukv_p[:, :half].reshape(MLA_KVR, MLA_H, MLA_NOPE),
                             d_ukv_p[:, half:].reshape(MLA_KVR, MLA_H, MLA_V)], axis=2).reshape(MLA_KVR, -1)
    return d_in, d_uq, d_ukv


def _mla_fwd(xs, h, wts, w_o, qn, kvn, tabs, tag):
    w_in_p, w_uq_p, w_ukv_p = wts
    z = mm(h, w_in_p, "nn", f"{tag}_in")
    cq, ckv, kr = mla_mid_fwd(z, qn, kvn, tabs, f"{tag}_mid")
    q = rope_q(mm(cq, w_uq_p, "nn", f"{tag}_uq"), tabs, False, f"{tag}_ropeq")
    kv = mm(ckv, w_ukv_p, "nn", f"{tag}_ukv", outs=(BF16,))
    o, lse = flash_fwd(_MLA_CFG, q, kv, kv, kr, f"{tag}_attn")
    xs = mm(o, w_o, "nn", f"{tag}_out", epi=_epi_add, extras=(xs,))
    return xs, (z, cq, ckv, kr, q, kv, o, lse)


def _mla_bwd(dx, h, wts, w_o, g_wo, qn, kvn, tabs, saved, tag):
    w_in_p, w_uq_p, w_ukv_p = wts
    z, cq, ckv, kr, q, kv, o, lse = saved
    mm(o, dx, "tn", f"{tag}_dwo", outs=(BF16,), out_loc=g_wo)
    do = mm(dx, w_o, "nt", f"{tag}_do", outs=(BF16,))
    dq, delta = flash_dq(_MLA_CFG, q, kv, kv, kr, o, do, lse, F32, f"{tag}_attn_dq")
    dk1, dv, dkr = flash_dkv(_MLA_CFG, q, kv, kv, kr, do, lse, delta, BF16, f"{tag}_attn_dkv")
    dqp = rope_q(dq, tabs, True, f"{tag}_ropeq_t")
    d_uq_p = mm(cq, dqp, "tn", f"{tag}_duq")
    dcq = mm(dqp, w_uq_p, "nt", f"{tag}_dcq")
    dkv = jnp.concatenate([dk1, dv], axis=1)
    d_ukv_p = mm(ckv, dkv, "tn", f"{tag}_dukv")
    dckv = mm(dkv, w_ukv_p, "nt", f"{tag}_dckv")
    dz, dqn, dkvn = mla_mid_bwd(z, qn, kvn, tabs, dcq, dckv, dkr, f"{tag}_mid_bwd")
    d_in_p = mm(h, dz, "tn", f"{tag}_din")
    dh = mm(dz, w_in_p, "nt", f"{tag}_dh")
    d_in, d_uq, d_ukv = _mla_weight_grads(d_in_p, d_uq_p, d_ukv_p)
    return dh, dict(mla_w_in=d_in, mla_w_uq=d_uq, mla_w_ukv=d_ukv, mla_q_norm=dqn, mla_kv_norm=dkvn)


_GDN_QKV = 3 * GDN_H * GDN_D
_GDN_GATE_END = _GDN_QKV + GDN_H * GDN_D


def _gdn_weights(w_in):
    rep = lambda cols: jnp.repeat(cols, GDN_D, axis=1)
    return jnp.concatenate([w_in[:, :_GDN_GATE_END], rep(w_in[:, _GDN_GATE_END:_GDN_GATE_END + GDN_H]),
                            rep(w_in[:, _GDN_GATE_END + GDN_H:])], axis=1)


def _fold(x):
    return x.reshape(x.shape[0], -1, GDN_D).sum(-1)


def _gdn_fwd(xs, h, w_in_x, conv_w, a_log, dt_bias, o_norm, w_o, tag):
    z = mm(h, w_in_x, "nn", f"{tag}_in")
    qkv = gdn_conv_fwd(z, conv_w, f"{tag}_conv")
    a_x, dt_x = jnp.repeat(a_log.reshape(1, -1), GDN_D, axis=1), jnp.repeat(dt_bias.reshape(1, -1), GDN_D, axis=1)
    og, states = gdn_chunk_fwd(qkv, z, a_x, dt_x, o_norm.reshape(1, -1), f"{tag}_chunks")
    xs = mm(og, w_o, "nn", f"{tag}_out", epi=_epi_add, extras=(xs,))
    return xs, (z, qkv, a_x, dt_x, og, states)


def _gdn_bwd(dx, h, w_in_x, conv_w, o_norm, w_o, g_wo, saved, tag):
    z, qkv, a_x, dt_x, og, states = saved
    mm(og, dx, "tn", f"{tag}_dwo", outs=(BF16,), out_loc=g_wo)
    dog = mm(dx, w_o, "nt", f"{tag}_dog")
    dq, dk, dv, dgate, dbl, dal, da_x, ddt_x, don = gdn_chunk_bwd(qkv, z, a_x, dt_x, o_norm.reshape(1, -1), states, dog,
                                                                  f"{tag}_chunks_bwd")
    dpre, dconv = gdn_conv_bwd(z, conv_w, jnp.concatenate([dq, dk, dv], axis=1), f"{tag}_conv_bwd")
    dz = jnp.concatenate([dpre, dgate, dbl, dal], axis=1)
    d_in_x = mm(h, dz, "tn", f"{tag}_din")
    dh = mm(dz, w_in_x, "nt", f"{tag}_dh", tn=512)
    ge = _GDN_GATE_END
    d_in = jnp.concatenate([d_in_x[:, :ge], _fold(d_in_x[:, ge:ge + GDN_H * GDN_D]), _fold(d_in_x[:, ge + GDN_H * GDN_D:])], axis=1)
    return dh, dict(gdn_w_in=d_in, gdn_conv_w=dconv, gdn_a_log=_fold(da_x).reshape(-1), gdn_dt_bias=_fold(ddt_x).reshape(-1),
                    gdn_o_norm=don.reshape(-1))


def _sc_fwd(xs, h, w_in, conv_w, w_o, tag):
    z = mm(h, w_in, "nn", f"{tag}_in")
    y = sc_fwd(z, conv_w, f"{tag}_conv")
    xs = mm(y, w_o, "nn", f"{tag}_out", epi=_epi_add, extras=(xs,))
    return xs, (z, y)


def _sc_bwd(dx, h, w_in, g_win, conv_w, w_o, g_wo, saved, tag):
    z, y = saved
    mm(y, dx, "tn", f"{tag}_dwo", outs=(BF16,), out_loc=g_wo)
    dy = mm(dx, w_o, "nt", f"{tag}_dy")
    db, dc, du, dconv = sc_bwd(z, conv_w, dy, f"{tag}_conv_bwd")
    dz = jnp.concatenate([db, dc, du], axis=1)
    mm(h, dz, "tn", f"{tag}_din", outs=(BF16,), out_loc=g_win)
    dh = mm(dz, w_in, "nt", f"{tag}_dh")
    return dh, dict(sc_conv_w=dconv)


def local_step(x, mem, pos, target, lay, wslabs, gslabs, small, before=None, after_bwd=None):
    depth = small["norm_mix"].shape[0]
    W = lambda name, layer: lay.loc(wslabs, name, layer)
    G = lambda name, layer: lay.loc(gslabs, name, layer)
    tabs = rope_tables(pos)
    mem_n = rmsnorm_fwd(mem, small["mem_norm"], "mem_norm")
    full = {n: lay.full(wslabs, n) for n in ("mla_w_in", "mla_w_uq", "mla_w_ukv")}
    mla_w = [_mla_weights(full["mla_w_in"][j], full["mla_w_uq"][j], full["mla_w_ukv"][j]) for j in range(full["mla_w_in"].shape[0])]
    gdn_in_x = {}

    xs = x
    saved = []
    for i in range(depth):
        j, kind = i // 3, i % 3
        tag = f"l{i}"
        if before is not None:
            xs = before(i, "mix", xs)
        if kind == 1:
            gdn_in_x[j] = _gdn_weights(lay.full(wslabs, "gdn_w_in")[j])
        x_a = xs
        h = rmsnorm_fwd(xs, small["norm_mix"][i], f"{tag}_norm_mix")
        if kind == 0:
            xs, mix = _mla_fwd(xs, h, mla_w[j], W("mla_w_o", j), small["mla_q_norm"][j], small["mla_kv_norm"][j], tabs, f"{tag}_mla")
        elif kind == 1:
            xs, mix = _gdn_fwd(xs, h, gdn_in_x[j], small["gdn_conv_w"][j], small["gdn_a_log"][j], small["gdn_dt_bias"][j],
                               small["gdn_o_norm"][j], W("gdn_w_o", j), f"{tag}_gdn")
        else:
            xs, mix = _sc_fwd(xs, h, W("sc_w_in", j), small["sc_conv_w"][j], W("sc_w_o", j), f"{tag}_sc")
        if before is not None:
            xs = before(i, "xa", xs)
        x_b = xs
        hn = rmsnorm_fwd(xs, small["norm_mem"][i], f"{tag}_norm_mem")
        xq = mm(hn, W("xa_w_q", i), "nn", f"{tag}_xa_q", outs=(BF16,))
        xkv = mm(mem_n, W("xa_w_kv", i), "nn", f"{tag}_xa_kv", outs=(BF16,))
        xo, xlse = flash_fwd(_XA_CFG, xq, xkv, xkv, None, f"{tag}_xa_attn")
        xs = mm(xo, W("xa_w_o", i), "nn", f"{tag}_xa_out", epi=_epi_add, extras=(xs,))
        x_c = xs
        hm = rmsnorm_fwd(xs, small["norm_mlp"][i], f"{tag}_norm_mlp")
        h1, act = mm(hm, W("mlp_w1", i), "nn", f"{tag}_mlp_up", outs=(BF16, BF16), epi=_epi_relu2)
        xs = mm(act, W("mlp_w2", i), "nn", f"{tag}_mlp_down", epi=_epi_add, extras=(xs,))
        saved.append((x_a, h, mix, x_b, hn, xq, xkv, xo, xlse, x_c, hm, h1, act))

    se, dx, d_final = loss_head(xs, small["final_norm"], target)

    per_layer = {n: [None] * depth for n in ("norm_mix", "norm_mem", "norm_mlp")}
    mixer = {}
    dmem_n = jnp.zeros(mem.shape, F32)
    for i in reversed(range(depth)):
        j, kind = i // 3, i % 3
        tag = f"l{i}"
        x_a, h, mix, x_b, hn, xq, xkv, xo, xlse, x_c, hm, h1, act = saved[i]
        mm(act, dx, "tn", f"{tag}_mlp_dw2", outs=(BF16,), out_loc=G("mlp_w2", i))
        dh1 = mm(dx, W("mlp_w2", i), "nt", f"{tag}_mlp_dh1", outs=(BF16,), epi=_epi_relu2_bwd, extras=(h1,))
        mm(hm, dh1, "tn", f"{tag}_mlp_dw1", outs=(BF16,), out_loc=G("mlp_w1", i))
        dhm = mm(dh1, W("mlp_w1", i), "nt", f"{tag}_mlp_dhm")
        dx, per_layer["norm_mlp"][i] = rmsnorm_bwd(x_c, small["norm_mlp"][i], dhm, dx, f"{tag}_norm_mlp_bwd")
        mm(xo, dx, "tn", f"{tag}_xa_dwo", outs=(BF16,), out_loc=G("xa_w_o", i))
        dxo = mm(dx, W("xa_w_o", i), "nt", f"{tag}_xa_do", outs=(BF16,))
        dxq, xdelta = flash_dq(_XA_CFG, xq, xkv, xkv, None, xo, dxo, xlse, BF16, f"{tag}_xa_attn_dq")
        dxk, dxv = flash_dkv(_XA_CFG, xq, xkv, xkv, None, dxo, xlse, xdelta, BF16, f"{tag}_xa_attn_dkv")
        dxkv = jnp.concatenate([dxk, dxv], axis=1)
        mm(hn, dxq, "tn", f"{tag}_xa_dwq", outs=(BF16,), out_loc=G("xa_w_q", i))
        dhn = mm(dxq, W("xa_w_q", i), "nt", f"{tag}_xa_dhn")
        mm(mem_n, dxkv, "tn", f"{tag}_xa_dwkv", outs=(BF16,), out_loc=G("xa_w_kv", i))
        dmem_n = mm(dxkv, W("xa_w_kv", i), "nt", f"{tag}_xa_dmem", epi=_epi_add, extras=(dmem_n,))
        dx, per_layer["norm_mem"][i] = rmsnorm_bwd(x_b, small["norm_mem"][i], dhn, dx, f"{tag}_norm_mem_bwd")
        if after_bwd is not None:
            dx = after_bwd(i, "xa", dx)
        if kind == 0:
            dh, gr = _mla_bwd(dx, h, mla_w[j], W("mla_w_o", j), G("mla_w_o", j), small["mla_q_norm"][j], small["mla_kv_norm"][j],
                              tabs, mix, f"{tag}_mla")
        elif kind == 1:
            dh, gr = _gdn_bwd(dx, h, gdn_in_x[j], small["gdn_conv_w"][j], small["gdn_o_norm"][j], W("gdn_w_o", j), G("gdn_w_o", j),
                              mix, f"{tag}_gdn")
        else:
            dh, gr = _sc_bwd(dx, h, W("sc_w_in", j), G("sc_w_in", j), small["sc_conv_w"][j], W("sc_w_o", j), G("sc_w_o", j),
                             mix, f"{tag}_sc")
        if kind == 1:
            lay.put_full(gslabs, "gdn_w_in", gr.pop("gdn_w_in")[None])
        for n, g in gr.items():
            mixer.setdefault(n, {})[j] = g
        dx, per_layer["norm_mix"][i] = rmsnorm_bwd(x_a, small["norm_mix"][i], dh, dx, f"{tag}_norm_mix_bwd")
        if after_bwd is not None:
            dx = after_bwd(i, "mix", dx)

    _, d_mem_norm = rmsnorm_bwd(mem, small["mem_norm"], dmem_n, jnp.zeros(mem.shape, F32), "mem_norm_bwd")
    grads = {n: jnp.stack(v) for n, v in per_layer.items()}
    for n, by_j in mixer.items():
        grads[n] = jnp.stack([by_j[j] for j in sorted(by_j)])
    grads["mem_norm"] = d_mem_norm
    grads["final_norm"] = d_final
    for n in ("mla_w_in", "mla_w_uq", "mla_w_ukv"):
        lay.put_full(gslabs, n, grads.pop(n))
    return se, dx, grads


def kernel(x, mem, positions, mla_w_in, mla_q_norm, mla_kv_norm, mla_w_uq, mla_w_ukv, mla_w_o, gdn_w_in, gdn_conv_w, gdn_a_log, gdn_dt_bias, gdn_o_norm, gdn_w_o, sc_w_in, sc_conv_w, sc_w_o, norm_mix, norm_mem, norm_mlp, xa_w_q, xa_w_kv, xa_w_o, mlp_w1, mlp_w2, mem_norm, final_norm, loss_target, m_mla_w_in, m_mla_q_norm, m_mla_kv_norm, m_mla_w_uq, m_mla_w_ukv, m_mla_w_o, m_gdn_w_in, m_gdn_conv_w, m_gdn_a_log, m_gdn_dt_bias, m_gdn_o_norm, m_gdn_w_o, m_sc_w_in, m_sc_conv_w, m_sc_w_o, m_norm_mix, m_norm_mem, m_norm_mlp, m_xa_w_q, m_xa_w_kv, m_xa_w_o, m_mlp_w1, m_mlp_w2, m_mem_norm, m_final_norm, v_mla_w_in, v_mla_q_norm, v_mla_kv_norm, v_mla_w_uq, v_mla_w_ukv, v_mla_w_o, v_gdn_w_in, v_gdn_conv_w, v_gdn_a_log, v_gdn_dt_bias, v_gdn_o_norm, v_gdn_w_o, v_sc_w_in, v_sc_conv_w, v_sc_w_o, v_norm_mix, v_norm_mem, v_norm_mlp, v_xa_w_q, v_xa_w_kv, v_xa_w_o, v_mlp_w1, v_mlp_w2, v_mem_norm, v_final_norm):
    given = dict(locals())
    p = {n: given[n] for n in _WEIGHTS}
    mom = {n: given["m_" + n] for n in _WEIGHTS}
    var = {n: given["v_" + n] for n in _WEIGHTS}
    split = sorted({n for members in _SLABS.values() for n, _, _, _ in members})
    lay = Layout({n: p[n].shape for n in split})
    flat2d = lambda a: a.reshape(-1, a.shape[-1])

    me = (2 * lax.axis_index("x") + lax.axis_index("y")).astype(jnp.int32)
    core = lax.axis_index("c").astype(jnp.int32)
    me1, c1, mc = me.reshape(1), core.reshape(1), jnp.stack([me, core])

    wslabs = lay.new_slabs(BF16)
    for slab, members in lay.members.items():
        for name, off, l0, l1, rpl in members:
            cast_into(flat2d(p[name]), l0 * rpl, (l1 - l0) * rpl, wslabs[slab], off, me1, f"cast_{slab}_{name}")
    small_names = [n for n, _ in _SMALL]
    words = lax.bitcast_convert_type(jnp.concatenate([p[n].reshape(-1) for n in small_names]), BF16).reshape(-1)
    words = jnp.pad(words, (0, SMALL_ROWS * SMALL_COLS - words.shape[0])).reshape(1, SMALL_ROWS, SMALL_COLS)
    small_slab = lax.dynamic_update_slice(jnp.zeros((N_CHIPS, SMALL_ROWS, SMALL_COLS), BF16), words, (me, 0, 0))

    first = _GROUPS[0][0]
    gathered = gather_slabs([wslabs[s].arr for s in first] + [small_slab])
    for s, arr in zip(first, gathered):
        wslabs[s].arr = arr
    in_flight, token = {}, gathered[-1]
    for slabs, point in _GROUPS[1:]:
        send, recv, thru, token = gather_start([wslabs[s].arr for s in slabs], token, f"weight_gather_start_{slabs[0]}")
        in_flight[point] = (send, recv, thru, slabs)
    started_token = token

    def before(i, stage, xs):
        if (i, stage) == (0, "mix"):
            return xs + started_token[0, 0]
        if (i, stage) in in_flight:
            send, recv, thru, slabs = in_flight[(i, stage)]
            landed = gather_wait(send, recv, thru, xs, f"weight_gather_wait_{slabs[0]}")
            for s, arr in zip(slabs, gather_forward(landed, f"weight_gather_forward_{slabs[0]}")):
                wslabs[s].arr = arr
        return xs

    small = {n: p[n] for n in _REPL}
    got, off = gathered[-1].reshape(N_CHIPS, -1), 0
    for n, ax in _SMALL:
        vals = lax.bitcast_convert_type(got[:, off:off + 2 * p[n].size].reshape(N_CHIPS, p[n].size, 2), F32)
        vals = vals.reshape((N_CHIPS,) + p[n].shape)
        small[n] = jnp.concatenate([vals[s] for s in range(N_CHIPS)], axis=ax)
        off += 2 * p[n].size

    gslabs = lay.new_slabs(BF16)
    complete_at = {point: slabs for slabs, point in _GROUPS[1:]}
    exchanging = []

    def after_bwd(i, stage, dx):
        if (i, stage) not in complete_at:
            return dx
        slabs = complete_at[(i, stage)]
        g = [gslabs[s].arr for s in slabs]
        swapped = pair_swap_halves(g, f"grad_pair_swap_{slabs[0]}")
        part = [pair_add(a, b, c1, f"pair_add_{s}") for a, b, s in zip(g, swapped, slabs)]
        send, recv, thru, token = exchange_start(part, c1, f"grad_exchange_start_{slabs[0]}")
        exchanging.append((slabs, send, recv, thru))
        return dx + token[0, 0]

    se, dx, sgrads = local_step(x[0], mem[0], positions.reshape(-1, 1), loss_target[0], lay, wslabs, gslabs, small,
                                before, after_bwd)
    loss = lax.psum(0.5 * jnp.sum(se) / x.shape[-1], ("x", "y", "c"))
    names, parts, received = [], [], []
    for slabs, send, recv, thru in exchanging:
        part, got = exchange_wait(send, recv, thru, dx, f"grad_exchange_wait_{slabs[0]}")
        names, parts, received = names + slabs, parts + list(part), received + list(got)

    axes = dict(_SMALL)
    small_order = small_names + _REPL
    slots = []
    for s in range(N_CHIPS):
        vals = {n: (lax.slice_in_dim(g, s * p[n].shape[axes[n]], (s + 1) * p[n].shape[axes[n]], axis=axes[n]) if n in axes else g)
                for n, g in sgrads.items()}
        slots.append(_small_pack(vals, small_order))
    g_last = [gslabs[s].arr for s in first] + [jnp.stack(slots).astype(BF16)]
    names_last = first + ["small"]
    swapped_last = pair_swap_halves(g_last, "grad_pair_swap_last")
    part_last = [pair_add(g, b, c1, f"pair_add_{s}") for g, b, s in zip(g_last, swapped_last, names_last)]
    names, parts, received = names + names_last, parts + part_last, received + list(chip_exchange(part_last))
    halves = [chip_sum(q, r, mc, f"chip_sum_{s}") for q, r, s in zip(parts, received, names)]
    reduced = dict(zip(names, pair_join_halves(halves)))

    res = {}
    for slab in _SLABS:
        for name, off, l0, l1, rpl in lay.members[slab]:
            res[name] = adamw(reduced[slab], off, flat2d(p[name]), flat2d(mom[name]), flat2d(var[name]), l0 * rpl, (l1 - l0) * rpl,
                              res.get(name), f"adamw_{slab}_{name}")
    for name in split:
        res[name] = [o.reshape(p[name].shape) for o in res[name]]
    sp = {k: _small_pack(d, small_order) for k, d in (("w", p), ("m", mom), ("v", var))}
    outs = adamw(reduced["small"], 0, sp["w"], sp["m"], sp["v"], 0, SMALL_ROWS, None, "adamw_small")
    unpacked = [_small_unpack(o, p, small_order) for o in outs]
    for n in small_order:
        res[n] = [u[n] for u in unpacked]
    return (loss, dx[None], *[res[n][k] for k in range(4) for n in _WEIGHTS])
```

```python
import jax
import jax.numpy as jnp
from jax import lax
from jax.experimental import pallas as pl
from jax.experimental.pallas import tpu as pltpu

F32 = jnp.float32
BF16 = jnp.bfloat16
HI = lax.Precision.HIGHEST
MESH = pl.DeviceIdType.MESH

EPS = 1e-6
ROPE_THETA = 10000.0
N_CHIPS = 4
LANES = 128
VMEM_LIMIT = 56 * 1024 * 1024
NEG = -1e30

MLA_H, MLA_NOPE, MLA_ROPE, MLA_V = 8, 128, 64, 128
MLA_QR, MLA_KVR = 384, 256
MLA_ZPAD = 768
GDN_H, GDN_D, GDN_C = 8, 128, 64
XA_H, XA_D = 4, 256

ADAM_LR, ADAM_B1, ADAM_B2, ADAM_EPS, ADAM_WD, ADAM_STEP = 0.001, 0.9, 0.999, 1e-08, 0.01, 10

SMALL_ROWS, SMALL_COLS = 32, 1024


def _cparams(sem=None):
    return pltpu.CompilerParams(dimension_semantics=sem, vmem_limit_bytes=VMEM_LIMIT)


def _pick(dim, pref):
    t = (min(pref, dim) // LANES) * LANES
    while t >= LANES:
        if dim % t == 0:
            return t
        t -= LANES
    return dim


def _pick_rows(rows, pref, *offsets):
    t = (min(pref, rows) // 16) * 16
    while t > 16 and (rows % t or any(o % t for o in offsets)):
        t -= 16
    return t


class Slab:
    def __init__(self, rows, width, dtype, arr=None):
        self.shape, self.dtype, self.arr = (N_CHIPS, rows, width), dtype, arr


class Loc:
    def __init__(self, slab, row0, K, N, axis):
        self.slab, self.row0, self.K, self.N, self.axis = slab, row0, K, N, axis
        self.Ks = K // N_CHIPS if axis == 0 else K
        self.Ns = N // N_CHIPS if axis == 1 else N

    def tile_spec(self, tr, tc, rc):
        assert self.row0 % tr == 0 and self.Ks % tr == 0 and self.Ns % tc == 0, (self.row0, self.Ks, self.Ns, tr, tc)
        r0, rb, cb = self.row0 // tr, self.Ks // tr, self.Ns // tc
        if self.axis == 0:
            return pl.BlockSpec((None, tr, tc), lambda i, j: (rc(i, j)[0] // rb, r0 + rc(i, j)[0] % rb, rc(i, j)[1]))
        return pl.BlockSpec((None, tr, tc), lambda i, j: (rc(i, j)[1] // cb, r0 + rc(i, j)[0], rc(i, j)[1] % cb))

    def slot_spec(self, slot, tr, tc, rc):
        assert self.row0 % tr == 0, (self.row0, tr)
        r0 = self.row0 // tr
        return pl.BlockSpec((None, tr, tc), lambda i, j: (slot, r0 + rc(i, j)[0], rc(i, j)[1]))


_DIMS = {"nn": ((1,), (0,)), "nt": ((1,), (1,)), "tn": ((0,), (0,))}
_ANY = pl.BlockSpec(memory_space=pl.ANY)


def mm(a, b, mode, name, outs=(F32,), epi=None, extras=(), tm=1024, tn=1024, out_loc=None, vecs=(), row_outs=0):
    full_rows = bool(vecs) or row_outs > 0
    b_loc = b if isinstance(b, Loc) else None
    if mode == "nn":
        M, K = a.shape
        K2, N = (b_loc.K, b_loc.N) if b_loc else b.shape
    elif mode == "nt":
        M, K = a.shape
        N, K2 = (b_loc.K, b_loc.N) if b_loc else b.shape
    else:
        K, M = a.shape
        K2, N = b.shape
    assert K == K2, (name, a.shape, K2, N)
    tm = _pick(out_loc.Ks if (out_loc and out_loc.axis == 0) else M, tm)
    if out_loc is not None and out_loc.axis == 1:
        tn = _pick(out_loc.Ns, tn)
    elif b_loc is not None and ((mode == "nn" and b_loc.axis == 1) or (mode == "nt" and b_loc.axis == 0)):
        tn = _pick(b_loc.Ns if mode == "nn" else b_loc.Ks, tn)
    elif b_loc is not None:
        tn = N if full_rows else _pick(N, min(tn, 512))
    else:
        tn = N if full_rows else _pick(N, tn)
    assert tn == N or not full_rows, name

    parts = 1
    if mode == "tn":
        a_spec = pl.BlockSpec((K, tm), lambda i, j: (0, i))
        b_specs, b_args = [pl.BlockSpec((K, tn), lambda i, j: (0, j))], [b]
    else:
        a_spec = pl.BlockSpec((tm, K), lambda i, j: (i, 0))
        if b_loc is None:
            b_specs = [pl.BlockSpec((K, tn), lambda i, j: (0, j)) if mode == "nn" else pl.BlockSpec((tn, K), lambda i, j: (j, 0))]
            b_args = [b]
        elif mode == "nn" and b_loc.axis == 1:
            b_specs, b_args = [b_loc.tile_spec(K, tn, lambda i, j: (0, j))], [b_loc.slab.arr]
        elif mode == "nt" and b_loc.axis == 0:
            b_specs, b_args = [b_loc.tile_spec(tn, K, lambda i, j: (j, 0))], [b_loc.slab.arr]
        elif mode == "nn":
            parts = N_CHIPS
            b_specs = [b_loc.slot_spec(s, b_loc.Ks, tn, lambda i, j: (0, j)) for s in range(parts)]
            b_args = [b_loc.slab.arr] * parts
        else:
            parts = N_CHIPS
            b_specs = [b_loc.slot_spec(s, tn, b_loc.Ns, lambda i, j: (j, 0)) for s in range(parts)]
            b_args = [b_loc.slab.arr] * parts
    kp = K // parts
    n_ex, n_out = len(extras) + len(vecs), len(outs)
    dims = (_DIMS[mode], ((), ()))

    def body(*refs):
        a_ref = refs[0]
        b_refs = refs[1:1 + parts]
        ex_refs = refs[1 + parts:1 + parts + n_ex]
        o_refs = refs[len(refs) - n_out - row_outs:len(refs) - row_outs]
        r_refs = refs[len(refs) - row_outs:]
        acc = None
        for s in range(parts):
            av = a_ref[...] if parts == 1 else a_ref[:, s * kp:(s + 1) * kp]
            d = lax.dot_general(av.astype(BF16), b_refs[s][...].astype(BF16), dims, preferred_element_type=F32)
            acc = d if acc is None else acc + d
        res = epi(acc, *[e[...] for e in ex_refs]) if epi is not None else (acc,)
        for o_ref, v in zip(o_refs, res[:n_out]):
            o_ref[...] = v.astype(o_ref.dtype)
        for r_ref, v in zip(r_refs, res[n_out:]):
            @pl.when(pl.program_id(0) == 0)
            def _():
                r_ref[...] = jnp.zeros_like(r_ref)

            r_ref[...] += v

    mn_spec = pl.BlockSpec((tm, tn), lambda i, j: (i, j))
    row_spec = pl.BlockSpec((1, tn), lambda i, j: (0, j))
    in_specs = [a_spec] + b_specs + [mn_spec] * len(extras) + [row_spec] * len(vecs)
    args = [a] + b_args + list(extras) + [v.reshape(1, N) for v in vecs]
    aliases = {}
    if out_loc is None:
        out_specs = [mn_spec] * n_out + [row_spec] * row_outs
        out_shape = [jax.ShapeDtypeStruct((M, N), d) for d in outs] + [jax.ShapeDtypeStruct((1, N), F32)] * row_outs
    else:
        assert n_out == 1 and mode == "tn"
        out_specs = [out_loc.tile_spec(tm, tn, lambda i, j: (i, j))]
        out_shape = [jax.ShapeDtypeStruct(out_loc.slab.shape, out_loc.slab.dtype)]
        if out_loc.slab.arr is not None:
            in_specs.append(_ANY)
            args.append(out_loc.slab.arr)
            aliases = {len(args) - 1: 0}

    res = pl.pallas_call(
        body, name=name, grid=(M // tm, N // tn), in_specs=in_specs, out_specs=out_specs, out_shape=out_shape,
        input_output_aliases=aliases, compiler_params=_cparams(("arbitrary" if row_outs else "parallel", "parallel")),
    )(*args)
    if out_loc is not None:
        out_loc.slab.arr = res[0]
        return None
    return res[0] if len(res) == 1 else tuple(res)


def _epi_add(acc, r):
    return (acc + r,)


def _epi_add_norm(acc, r, g):
    x = acc + r
    return x, _rms(x, g)


def _epi_norm_bwd(acc, x, dx_in, g):
    r = lax.rsqrt(jnp.mean(x * x, axis=-1, keepdims=True) + EPS)
    xh = x * r
    dxh = acc * g
    dx = dx_in + r * (dxh - xh * jnp.mean(dxh * xh, axis=-1, keepdims=True))
    return dx, jnp.sum(acc * xh, axis=0, keepdims=True)


def residual_norm(a, w, xs, g, name, tm=1024):
    if g is None:
        return mm(a, w, "nn", name, epi=_epi_add, extras=(xs,), tm=tm), None
    return mm(a, w, "nn", name, outs=(F32, BF16), epi=_epi_add_norm, extras=(xs,), vecs=(g,), tm=tm)


def _epi_relu2(acc):
    r = jnp.maximum(acc, 0.0)
    return acc, r * r


def _epi_relu2_bwd(acc, h1):
    return (acc * (2.0 * jnp.maximum(h1.astype(F32), 0.0)),)


def _rms(x, g):
    return x * lax.rsqrt(jnp.mean(x * x, axis=-1, keepdims=True) + EPS) * g


def _row_spec(ts, cols):
    return pl.BlockSpec((ts, cols), lambda i: (i, 0))


def _par_spec(cols):
    return pl.BlockSpec((1, cols), lambda i: (0, 0))


def rmsnorm_fwd(x, g, name, ts=256):
    T, D = x.shape
    ts = min(ts, T)

    def body(x_ref, g_ref, o_ref):
        o_ref[...] = _rms(x_ref[...], g_ref[...]).astype(o_ref.dtype)

    return pl.pallas_call(
        body, name=name, grid=(T // ts,),
        in_specs=[_row_spec(ts, D), _par_spec(D)], out_specs=_row_spec(ts, D),
        out_shape=jax.ShapeDtypeStruct((T, D), BF16), compiler_params=_cparams(("parallel",)),
    )(x, g.reshape(1, D))


def rmsnorm_bwd(x, g, dy, dx_in, name, ts=256):
    T, D = x.shape
    ts = min(ts, T)

    def body(x_ref, g_ref, dy_ref, dxi_ref, dx_ref, dg_ref):
        xv = x_ref[...]
        r = lax.rsqrt(jnp.mean(xv * xv, axis=-1, keepdims=True) + EPS)
        xh = xv * r
        dyv = dy_ref[...].astype(F32)
        dxh = dyv * g_ref[...]
        dx_ref[...] = dxi_ref[...] + r * (dxh - xh * jnp.mean(dxh * xh, axis=-1, keepdims=True))
        dg = jnp.sum(dyv * xh, axis=0, keepdims=True)

        @pl.when(pl.program_id(0) == 0)
        def _():
            dg_ref[...] = jnp.zeros_like(dg_ref)

        dg_ref[...] += dg

    dx, dg = pl.pallas_call(
        body, name=name, grid=(T // ts,),
        in_specs=[_row_spec(ts, D), _par_spec(D), _row_spec(ts, D), _row_spec(ts, D)],
        out_specs=[_row_spec(ts, D), _par_spec(D)],
        out_shape=[jax.ShapeDtypeStruct((T, D), F32), jax.ShapeDtypeStruct((1, D), F32)],
        compiler_params=_cparams(("arbitrary",)),
    )(x, g.reshape(1, D), dy, dx_in)
    return dx, dg.reshape(D)


def rope_tables(pos, name="rope_tables"):
    T = pos.shape[0]
    half = MLA_ROPE // 2
    inv = ROPE_THETA ** (-jnp.arange(0, MLA_ROPE, 2, dtype=F32) / MLA_ROPE)
    inv_row = jnp.concatenate([inv, inv, jnp.zeros((LANES - MLA_ROPE,), F32)]).reshape(1, LANES)

    def body(p_ref, f_ref, c_ref, a_ref, b_ref):
        ang = p_ref[...].astype(F32) * f_ref[...]
        lane = lax.broadcasted_iota(jnp.int32, ang.shape, 1)
        c, s = jnp.cos(ang), jnp.sin(ang)
        c_ref[...] = jnp.where(lane < MLA_ROPE, c, 0.0)
        a_ref[...] = jnp.where(lane < half, -s, 0.0)
        b_ref[...] = jnp.where((lane >= half) & (lane < MLA_ROPE), s, 0.0)

    sh = jax.ShapeDtypeStruct((T, LANES), F32)
    return pl.pallas_call(body, name=name, out_shape=[sh, sh, sh], compiler_params=_cparams())(pos, inv_row)


def _roll_l(x):
    return pltpu.roll(x, LANES - MLA_ROPE // 2, 1)


def _roll_r(x):
    return pltpu.roll(x, MLA_ROPE // 2, 1)


def _rope(r, c, sa, sb):
    return r * c + _roll_l(r) * sa + _roll_r(r) * sb


def _rope_t(d, c, sa, sb):
    return d * c + _roll_r(d * sa) + _roll_l(d * sb)


def mla_mid_fwd(z, qn, kvn, tabs, name, ts=256):
    T = z.shape[0]
    ts = min(ts, T)
    a0, a1 = MLA_QR, MLA_QR + MLA_KVR

    def body(z_ref, qn_ref, kvn_ref, c_ref, sa_ref, sb_ref, cq_ref, ckv_ref, kr_ref):
        cq_ref[...] = _rms(z_ref[:, 0:a0], qn_ref[...]).astype(BF16)
        ckv_ref[...] = _rms(z_ref[:, a0:a1], kvn_ref[...]).astype(BF16)
        kr_ref[...] = _rope(z_ref[:, a1:MLA_ZPAD], c_ref[...], sa_ref[...], sb_ref[...]).astype(BF16)

    return pl.pallas_call(
        body, name=name, grid=(T // ts,),
        in_specs=[_row_spec(ts, MLA_ZPAD), _par_spec(MLA_QR), _par_spec(MLA_KVR)] + [_row_spec(ts, LANES)] * 3,
        out_specs=[_row_spec(ts, MLA_QR), _row_spec(ts, MLA_KVR), _row_spec(ts, LANES)],
        out_shape=[jax.ShapeDtypeStruct((T, MLA_QR), BF16), jax.ShapeDtypeStruct((T, MLA_KVR), BF16),
                   jax.ShapeDtypeStruct((T, LANES), BF16)],
        compiler_params=_cparams(("parallel",)),
    )(z, qn.reshape(1, -1), kvn.reshape(1, -1), *tabs)


def mla_mid_bwd(z, qn, kvn, tabs, dcq, dckv, dkr, name, ts=256):
    T = z.shape[0]
    ts = min(ts, T)
    a0, a1 = MLA_QR, MLA_QR + MLA_KVR

    def body(z_ref, qn_ref, kvn_ref, c_ref, sa_ref, sb_ref, dcq_ref, dckv_ref, dkr_ref, dz_ref, dqn_ref, dkvn_ref):
        _, vq = jax.vjp(_rms, z_ref[:, 0:a0], qn_ref[...])
        dzq, dqn = vq(dcq_ref[...].astype(F32))
        _, vk = jax.vjp(_rms, z_ref[:, a0:a1], kvn_ref[...])
        dzk, dkvn = vk(dckv_ref[...].astype(F32))
        dz_ref[:, 0:a0] = dzq.astype(dz_ref.dtype)
        dz_ref[:, a0:a1] = dzk.astype(dz_ref.dtype)
        dz_ref[:, a1:MLA_ZPAD] = _rope_t(dkr_ref[...].astype(F32), c_ref[...], sa_ref[...], sb_ref[...]).astype(dz_ref.dtype)

        @pl.when(pl.program_id(0) == 0)
        def _():
            dqn_ref[...] = jnp.zeros_like(dqn_ref)
            dkvn_ref[...] = jnp.zeros_like(dkvn_ref)

        dqn_ref[...] += dqn
        dkvn_ref[...] += dkvn

    dz, dqn, dkvn = pl.pallas_call(
        body, name=name, grid=(T // ts,),
        in_specs=[_row_spec(ts, MLA_ZPAD), _par_spec(MLA_QR), _par_spec(MLA_KVR)] + [_row_spec(ts, LANES)] * 3
        + [_row_spec(ts, MLA_QR), _row_spec(ts, MLA_KVR), _row_spec(ts, LANES)],
        out_specs=[_row_spec(ts, MLA_ZPAD), _par_spec(MLA_QR), _par_spec(MLA_KVR)],
        out_shape=[jax.ShapeDtypeStruct((T, MLA_ZPAD), BF16), jax.ShapeDtypeStruct((1, MLA_QR), F32),
                   jax.ShapeDtypeStruct((1, MLA_KVR), F32)],
        compiler_params=_cparams(("arbitrary",)),
    )(z, qn.reshape(1, -1), kvn.reshape(1, -1), *tabs, dcq, dckv, dkr)
    return dz, dqn.reshape(-1), dkvn.reshape(-1)


def rope_q(q, tabs, transpose, name, ts=256):
    T, W = q.shape
    ts = min(ts, T)
    fn = _rope_t if transpose else _rope
    hw = 2 * LANES

    def body(q_ref, c_ref, sa_ref, sb_ref, o_ref):
        c, sa, sb = c_ref[...], sa_ref[...], sb_ref[...]
        for h in range(W // hw):
            o_ref[:, h * hw:h * hw + LANES] = q_ref[:, h * hw:h * hw + LANES].astype(o_ref.dtype)
            o_ref[:, h * hw + LANES:(h + 1) * hw] = fn(q_ref[:, h * hw + LANES:(h + 1) * hw].astype(F32), c, sa, sb).astype(o_ref.dtype)

    return pl.pallas_call(
        body, name=name, grid=(T // ts,),
        in_specs=[_row_spec(ts, W)] + [_row_spec(ts, LANES)] * 3, out_specs=_row_spec(ts, W),
        out_shape=jax.ShapeDtypeStruct((T, W), BF16), compiler_params=_cparams(("parallel",)),
    )(q, *tabs)


def loss_head(x, g, target, name="loss_head", ts=256):
    T, D = x.shape
    ts = min(ts, T)

    def body(x_ref, g_ref, t_ref, se_ref, dx_ref, dg_ref):
        xv = x_ref[...]
        r = lax.rsqrt(jnp.mean(xv * xv, axis=-1, keepdims=True) + EPS)
        xh = xv * r
        err = xh * g_ref[...] - t_ref[...]
        dy = err * (1.0 / D)
        dxh = dy * g_ref[...]
        dx_ref[...] = r * (dxh - xh * jnp.mean(dxh * xh, axis=-1, keepdims=True))

        @pl.when(pl.program_id(0) == 0)
        def _():
            se_ref[...] = jnp.zeros_like(se_ref)
            dg_ref[...] = jnp.zeros_like(dg_ref)

        se_ref[...] += jnp.sum(err * err, axis=0, keepdims=True)
        dg_ref[...] += jnp.sum(dy * xh, axis=0, keepdims=True)

    se, dx, dg = pl.pallas_call(
        body, name=name, grid=(T // ts,),
        in_specs=[_row_spec(ts, D), _par_spec(D), _row_spec(ts, D)],
        out_specs=[_par_spec(D), _row_spec(ts, D), _par_spec(D)],
        out_shape=[jax.ShapeDtypeStruct((1, D), F32), jax.ShapeDtypeStruct((T, D), F32), jax.ShapeDtypeStruct((1, D), F32)],
        compiler_params=_cparams(("arbitrary",)),
    )(x, g.reshape(1, D), target)
    return se, dx, dg.reshape(D)


def _dot_nt(a, b):
    return lax.dot_general(a, b, (((1,), (1,)), ((), ())), preferred_element_type=F32)


def _dot_tn(a, b):
    return lax.dot_general(a, b, (((0,), (0,)), ((), ())), preferred_element_type=F32)


def _dot_nn(a, b):
    return lax.dot_general(a, b, (((1,), (0,)), ((), ())), preferred_element_type=F32)


class _Attn:
    def __init__(self, H, dq, dk1, dv, causal, scale, hp, hp_kv, blk=256):
        self.H, self.dq, self.dk1, self.dv, self.causal, self.scale, self.blk = H, dq, dk1, dv, causal, scale, blk
        self.hp, self.hp_kv = hp, hp_kv


def _cols(ref, rows, hh, width):
    return ref[rows, hh * width:(hh + 1) * width]


def _keys(cfg, k1_ref, k2_ref, rows, hh):
    ks = _cols(k1_ref, rows, hh, cfg.dk1)
    if k2_ref is not None:
        ks = jnp.concatenate([ks, k2_ref[rows, :]], axis=1)
    return ks


def _attn_specs(cfg, hp, t, Tk, has_k2, by_q):
    g = cfg.H // hp
    if by_q:
        specs = [pl.BlockSpec((t, hp * cfg.dq), lambda h, i: (i, h)),
                 pl.BlockSpec((Tk, hp * cfg.dk1), lambda h, i: (0, h)),
                 pl.BlockSpec((Tk, hp * cfg.dv), lambda h, i: (0, g + h))]
        if has_k2:
            specs.append(pl.BlockSpec((Tk, LANES), lambda h, i: (0, 0)))
    else:
        specs = [None,
                 pl.BlockSpec((t, hp * cfg.dk1), lambda j, h: (j, h)),
                 pl.BlockSpec((t, hp * cfg.dv), lambda j, h: (j, g + h))]
        if has_k2:
            specs.append(pl.BlockSpec((t, LANES), lambda j, h: (j, 0)))
    return specs


def _mask(s, diagonal):
    if not diagonal:
        return s
    return jnp.where(lax.broadcasted_iota(jnp.int32, s.shape, 0) >= lax.broadcasted_iota(jnp.int32, s.shape, 1), s, NEG)


def flash_fwd(cfg, q, k1, v, k2, name):
    Tq, Tk = q.shape[0], k1.shape[0]
    t = min(cfg.blk, Tq, Tk)
    nkb = Tk // t
    has_k2 = k2 is not None
    hp = cfg.hp

    def body(*refs):
        q_ref, k1_ref, v_ref = refs[:3]
        k2_ref = refs[3] if has_k2 else None
        o_ref, lse_ref = refs[-2], refs[-1]
        i = pl.program_id(1)
        qs = [_cols(q_ref, slice(None), hh, cfg.dq) for hh in range(hp)]

        def step(j, carry, diagonal=False):
            rows = pl.ds(pl.multiple_of(j * t, t), t)
            out = []
            for hh in range(hp):
                m, l, acc = carry[hh]
                s = _mask(_dot_nt(qs[hh], _keys(cfg, k1_ref, k2_ref, rows, hh)) * cfg.scale, diagonal)
                m2 = jnp.maximum(m, jnp.max(s, axis=-1, keepdims=True))
                p = jnp.exp(s - m2)
                alpha = jnp.exp(m - m2)
                l2 = alpha * l + jnp.sum(p, axis=-1, keepdims=True)
                acc2 = alpha * acc + _dot_nn(p.astype(BF16), _cols(v_ref, rows, hh, cfg.dv))
                out.append((m2, l2, acc2))
            return tuple(out)

        init = tuple((jnp.full((t, 1), NEG, F32), jnp.zeros((t, 1), F32), jnp.zeros((t, cfg.dv), F32)) for _ in range(hp))
        res = lax.fori_loop(0, i if cfg.causal else nkb, step, init)
        if cfg.causal:
            res = step(i, res, True)
        for hh in range(hp):
            m, l, acc = res[hh]
            o_ref[:, hh * cfg.dv:(hh + 1) * cfg.dv] = (acc / l).astype(o_ref.dtype)
            lse_ref[hh] = m + jnp.log(l)

    args = [q, k1, v] + ([k2] if has_k2 else [])
    return pl.pallas_call(
        body, name=name, grid=(cfg.H // hp, Tq // t), in_specs=_attn_specs(cfg, hp, t, Tk, has_k2, True),
        out_specs=[pl.BlockSpec((t, hp * cfg.dv), lambda h, i: (i, h)), pl.BlockSpec((hp, t, 1), lambda h, i: (h, i, 0))],
        out_shape=[jax.ShapeDtypeStruct((Tq, cfg.H * cfg.dv), BF16), jax.ShapeDtypeStruct((cfg.H, Tq, 1), F32)],
        compiler_params=_cparams(("parallel", "parallel")),
    )(*args)


def flash_dq(cfg, q, k1, v, k2, o, do, lse, out_dtype, name):
    Tq, Tk = q.shape[0], k1.shape[0]
    t = min(cfg.blk, Tq, Tk)
    nkb = Tk // t
    has_k2 = k2 is not None
    hp = cfg.hp

    def body(*refs):
        q_ref, k1_ref, v_ref = refs[:3]
        k2_ref = refs[3] if has_k2 else None
        o_ref, do_ref, lse_ref, dq_ref, dl_ref = refs[-5:]
        i = pl.program_id(1)
        qs = [_cols(q_ref, slice(None), hh, cfg.dq) for hh in range(hp)]
        dos = [_cols(do_ref, slice(None), hh, cfg.dv) for hh in range(hp)]
        lses = [lse_ref[hh] for hh in range(hp)]
        deltas = []
        for hh in range(hp):
            d = jnp.sum(dos[hh].astype(F32) * _cols(o_ref, slice(None), hh, cfg.dv).astype(F32), axis=-1, keepdims=True)
            dl_ref[hh] = d
            deltas.append(d)

        def step(j, dqs, diagonal=False):
            rows = pl.ds(pl.multiple_of(j * t, t), t)
            out = []
            for hh in range(hp):
                ks = _keys(cfg, k1_ref, k2_ref, rows, hh)
                s = _mask(_dot_nt(qs[hh], ks) * cfg.scale, diagonal)
                p = jnp.exp(s - lses[hh])
                dp = _dot_nt(dos[hh], _cols(v_ref, rows, hh, cfg.dv))
                ds = p * (dp - deltas[hh]) * cfg.scale
                out.append(dqs[hh] + _dot_nn(ds.astype(BF16), ks))
            return tuple(out)

        dqs = lax.fori_loop(0, i if cfg.causal else nkb, step, tuple(jnp.zeros((t, cfg.dq), F32) for _ in range(hp)))
        if cfg.causal:
            dqs = step(i, dqs, True)
        for hh in range(hp):
            dq_ref[:, hh * cfg.dq:(hh + 1) * cfg.dq] = dqs[hh].astype(dq_ref.dtype)

    ov = pl.BlockSpec((t, hp * cfg.dv), lambda h, i: (i, h))
    row1 = pl.BlockSpec((hp, t, 1), lambda h, i: (h, i, 0))
    args = [q, k1, v] + ([k2] if has_k2 else []) + [o, do, lse]
    return pl.pallas_call(
        body, name=name, grid=(cfg.H // hp, Tq // t), in_specs=_attn_specs(cfg, hp, t, Tk, has_k2, True) + [ov, ov, row1],
        out_specs=[pl.BlockSpec((t, hp * cfg.dq), lambda h, i: (i, h)), row1],
        out_shape=[jax.ShapeDtypeStruct((Tq, cfg.H * cfg.dq), out_dtype), jax.ShapeDtypeStruct((cfg.H, Tq, 1), F32)],
        compiler_params=_cparams(("parallel", "parallel")),
    )(*args)


def flash_dkv(cfg, q, k1, v, k2, do, lse, delta, out_dtype, name):
    Tq, Tk = q.shape[0], k1.shape[0]
    t = min(cfg.blk, Tq, Tk)
    nqb = Tq // t
    has_k2 = k2 is not None
    hp = cfg.hp_kv

    def body(*refs):
        q_ref, k1_ref, v_ref = refs[:3]
        k2_ref = refs[3] if has_k2 else None
        n_in = 4 if has_k2 else 3
        do_ref, lse_ref, dl_ref = refs[n_in:n_in + 3]
        dk1_ref, dv_ref = refs[n_in + 3], refs[n_in + 4]
        j, h = pl.program_id(0), pl.program_id(1)
        kss = [_keys(cfg, k1_ref, k2_ref, slice(None), hh) for hh in range(hp)]
        vss = [_cols(v_ref, slice(None), hh, cfg.dv) for hh in range(hp)]

        def step(i, carry, diagonal=False):
            rows = pl.ds(pl.multiple_of(i * t, t), t)
            out = []
            for hh in range(hp):
                dk, dv = carry[hh]
                qi, doi = _cols(q_ref, rows, hh, cfg.dq), _cols(do_ref, rows, hh, cfg.dv)
                s = _mask(_dot_nt(qi, kss[hh]) * cfg.scale, diagonal)
                p = jnp.exp(s - lse_ref[hh, rows, :])
                dv = dv + _dot_tn(p.astype(BF16), doi)
                ds = p * (_dot_nt(doi, vss[hh]) - dl_ref[hh, rows, :]) * cfg.scale
                dk = dk + _dot_tn(ds.astype(BF16), qi)
                out.append((dk, dv))
            return tuple(out)

        init = tuple((jnp.zeros((t, cfg.dq), F32), jnp.zeros((t, cfg.dv), F32)) for _ in range(hp))
        if cfg.causal:
            res = lax.fori_loop(j + 1, nqb, step, step(j, init, True))
        else:
            res = lax.fori_loop(0, nqb, step, init)
        for hh in range(hp):
            dk, dv = res[hh]
            dv_ref[:, hh * cfg.dv:(hh + 1) * cfg.dv] = dv.astype(dv_ref.dtype)
            dk1_ref[:, hh * cfg.dk1:(hh + 1) * cfg.dk1] = dk[:, 0:cfg.dk1].astype(dk1_ref.dtype)
        if has_k2:
            dk2_ref = refs[n_in + 5]

            @pl.when(h == 0)
            def _():
                dk2_ref[...] = jnp.zeros_like(dk2_ref)

            for hh in range(hp):
                dk2_ref[...] += res[hh][0][:, cfg.dk1:]

    specs = _attn_specs(cfg, hp, t, Tk, has_k2, False)
    specs[0] = pl.BlockSpec((Tq, hp * cfg.dq), lambda j, h: (0, h))
    rows_all = pl.BlockSpec((hp, Tq, 1), lambda j, h: (h, 0, 0))
    specs += [pl.BlockSpec((Tq, hp * cfg.dv), lambda j, h: (0, h)), rows_all, rows_all]
    args = [q, k1, v] + ([k2] if has_k2 else []) + [do, lse, delta]
    out_specs = [pl.BlockSpec((t, hp * cfg.dk1), lambda j, h: (j, h)), pl.BlockSpec((t, hp * cfg.dv), lambda j, h: (j, h))]
    out_shape = [jax.ShapeDtypeStruct((Tk, cfg.H * cfg.dk1), out_dtype), jax.ShapeDtypeStruct((Tk, cfg.H * cfg.dv), out_dtype)]
    if has_k2:
        out_specs.append(pl.BlockSpec((t, LANES), lambda j, h: (j, 0)))
        out_shape.append(jax.ShapeDtypeStruct((Tk, LANES), F32))
    return pl.pallas_call(
        body, name=name, grid=(Tk // t, cfg.H // hp), in_specs=specs, out_specs=out_specs, out_shape=out_shape,
        compiler_params=_cparams(("parallel", "arbitrary")),
    )(*args)


def _shift_down(x, s):
    if s == 0:
        return x
    t = lax.broadcasted_iota(jnp.int32, x.shape, 0)
    return jnp.where(t >= s, pltpu.roll(x, s, 0), 0.0)


def _shift_up(x, s):
    if s == 0:
        return x
    n = x.shape[0]
    t = lax.broadcasted_iota(jnp.int32, x.shape, 0)
    return jnp.where(t < n - s, pltpu.roll(x, n - s, 0), 0.0)


def _conv(x, w_ref, kw):
    y = x * w_ref[kw - 1:kw, :]
    for j in range(kw - 1):
        y = y + _shift_down(x, kw - 1 - j) * w_ref[j:j + 1, :]
    return y


def _conv_t(d, w_ref, kw):
    y = d * w_ref[kw - 1:kw, :]
    for j in range(kw - 1):
        y = y + _shift_up(d, kw - 1 - j) * w_ref[j:j + 1, :]
    return y


def _conv_dw(d, x, kw):
    rows = lax.broadcasted_iota(jnp.int32, (kw, d.shape[1]), 0)
    dw = jnp.zeros((kw, d.shape[1]), F32)
    for j in range(kw):
        r = jnp.sum(d * _shift_down(x, kw - 1 - j), axis=0, keepdims=True)
        dw = jnp.where(rows == j, r, dw)
    return dw


def _silu(x):
    return x * jax.nn.sigmoid(x)


def _silu_grad(x):
    s = jax.nn.sigmoid(x)
    return s * (1.0 + x * (1.0 - s))


def gdn_conv_fwd(z, w, name, tc=256):
    T, C = z.shape[0], w.shape[1]
    kw = w.shape[0]

    def body(x_ref, w_ref, o_ref):
        o_ref[...] = _silu(_conv(x_ref[...], w_ref, kw))

    return pl.pallas_call(
        body, name=name, grid=(C // tc,),
        in_specs=[pl.BlockSpec((T, tc), lambda j: (0, j)), pl.BlockSpec((kw, tc), lambda j: (0, j))],
        out_specs=pl.BlockSpec((T, tc), lambda j: (0, j)),
        out_shape=jax.ShapeDtypeStruct((T, C), F32), compiler_params=_cparams(("parallel",)),
    )(z, w)


def gdn_conv_bwd(z, w, dy, name, tc=256):
    T, C = z.shape[0], w.shape[1]
    kw = w.shape[0]

    def body(x_ref, w_ref, dy_ref, dx_ref, dw_ref):
        xv = x_ref[...]
        dc = dy_ref[...] * _silu_grad(_conv(xv, w_ref, kw))
        dx_ref[...] = _conv_t(dc, w_ref, kw).astype(dx_ref.dtype)
        dw_ref[...] = _conv_dw(dc, xv, kw)

    col = lambda j: (0, j)
    return pl.pallas_call(
        body, name=name, grid=(C // tc,),
        in_specs=[pl.BlockSpec((T, tc), col), pl.BlockSpec((kw, tc), col), pl.BlockSpec((T, tc), col)],
        out_specs=[pl.BlockSpec((T, tc), col), pl.BlockSpec((kw, tc), col)],
        out_shape=[jax.ShapeDtypeStruct((T, C), BF16), jax.ShapeDtypeStruct((kw, C), F32)],
        compiler_params=_cparams(("parallel",)),
    )(z, w, dy)


def sc_fwd(z, w, name, tc=256):
    T, C = z.shape[0], w.shape[1]
    kw, nb = w.shape[0], C // tc

    def body(b_ref, c_ref, u_ref, w_ref, o_ref):
        o_ref[...] = (b_ref[...] * _conv(c_ref[...] * u_ref[...], w_ref, kw)).astype(o_ref.dtype)

    return pl.pallas_call(
        body, name=name, grid=(nb,),
        in_specs=[pl.BlockSpec((T, tc), lambda j: (0, j)), pl.BlockSpec((T, tc), lambda j: (0, nb + j)),
                  pl.BlockSpec((T, tc), lambda j: (0, 2 * nb + j)), pl.BlockSpec((kw, tc), lambda j: (0, j))],
        out_specs=pl.BlockSpec((T, tc), lambda j: (0, j)),
        out_shape=jax.ShapeDtypeStruct((T, C), BF16), compiler_params=_cparams(("parallel",)),
    )(z, z, z, w)


def sc_bwd(z, w, dy, name, tc=256):
    T, C = z.shape[0], w.shape[1]
    kw, nb = w.shape[0], C // tc

    def body(b_ref, c_ref, u_ref, w_ref, dy_ref, db_ref, dc_ref, du_ref, dw_ref):
        cv, uv, dyv = c_ref[...], u_ref[...], dy_ref[...]
        cu = cv * uv
        db_ref[...] = (dyv * _conv(cu, w_ref, kw)).astype(db_ref.dtype)
        dcv = dyv * b_ref[...]
        dcu = _conv_t(dcv, w_ref, kw)
        dc_ref[...] = (dcu * uv).astype(dc_ref.dtype)
        du_ref[...] = (dcu * cv).astype(du_ref.dtype)
        dw_ref[...] = _conv_dw(dcv, cu, kw)

    col = lambda j: (0, j)
    act = jax.ShapeDtypeStruct((T, C), BF16)
    return pl.pallas_call(
        body, name=name, grid=(nb,),
        in_specs=[pl.BlockSpec((T, tc), col), pl.BlockSpec((T, tc), lambda j: (0, nb + j)),
                  pl.BlockSpec((T, tc), lambda j: (0, 2 * nb + j)), pl.BlockSpec((kw, tc), col), pl.BlockSpec((T, tc), col)],
        out_specs=[pl.BlockSpec((T, tc), col)] * 3 + [pl.BlockSpec((kw, tc), col)],
        out_shape=[act, act, act, jax.ShapeDtypeStruct((kw, C), F32)],
        compiler_params=_cparams(("parallel",)),
    )(z, z, z, w, dy)


def _hdot(a, b, dims):
    return lax.dot_general(a, b, (dims, ((), ())), precision=HI, preferred_element_type=F32)


def _bdot(a, b, dims):
    return lax.dot_general(a.astype(BF16), b.astype(BF16), (dims, ((), ())), preferred_element_type=F32)


_NN, _NT, _TN = ((1,), (0,)), ((1,), (1,)), ((0,), (0,))


def _per_head_dots(dot2d):
    def stacked(a, b, dims):
        return jnp.stack([dot2d(a[h], b[h], dims) for h in range(a.shape[0])])

    @jax.custom_vjp
    def nn(a, b):
        return stacked(a, b, _NN)

    @jax.custom_vjp
    def nt(a, b):
        return stacked(a, b, _NT)

    @jax.custom_vjp
    def tn(a, b):
        return stacked(a, b, _TN)

    nn.defvjp(lambda a, b: (nn(a, b), (a, b)), lambda r, d: (stacked(d, r[1], _NT), stacked(r[0], d, _TN)))
    nt.defvjp(lambda a, b: (nt(a, b), (a, b)), lambda r, d: (stacked(d, r[1], _NN), stacked(d, r[0], _TN)))
    tn.defvjp(lambda a, b: (tn(a, b), (a, b)), lambda r, d: (stacked(r[1], d, _NT), stacked(r[0], d, _NN)))
    return nn, nt, tn


_hnn, _hnt, _htn = _per_head_dots(_hdot)
_bnn, _bnt, _btn = _per_head_dots(_bdot)


@jax.custom_vjp
def _unit_lower_inverse(m):
    c = m.shape[-1]
    eye = (lax.broadcasted_iota(jnp.int32, (c, c), 0) == lax.broadcasted_iota(jnp.int32, (c, c), 1)).astype(F32)
    t = eye - m
    p = _hnn(m, m)
    n = 2
    while n < c:
        t = t + _hnn(t, p)
        n *= 2
        if n < c:
            p = _hnn(p, p)
    return t


def _uli_fwd(m):
    t = _unit_lower_inverse(m)
    return t, t


def _uli_bwd(t, dt):
    return (-_htn(t, _hnt(dt, t)),)


_unit_lower_inverse.defvjp(_uli_fwd, _uli_bwd)


@jax.custom_vjp
def _known_inverse(m, t):
    return t


_known_inverse.defvjp(lambda m, t: (t, t), lambda t, dt: (_uli_bwd(t, dt)[0], jnp.zeros_like(t)))


def _gdn_chunk(q, k, v, gate, bl, al, a_log, dt_bias, o_norm, st, t_known=None):
    nh, c = q.shape[0], q.shape[1]
    ii = lax.broadcasted_iota(jnp.int32, (c, c), 0)
    jj = lax.broadcasted_iota(jnp.int32, (c, c), 1)
    tri, strict = ii >= jj, ii > jj
    q = q * lax.rsqrt(jnp.sum(q * q, -1, keepdims=True) + EPS) * (GDN_D ** -0.5)
    k = k * lax.rsqrt(jnp.sum(k * k, -1, keepdims=True) + EPS)
    beta = jax.nn.sigmoid(bl)
    g = -jnp.exp(a_log) * jax.nn.softplus(al + dt_bias)
    gc = _hnn(jnp.broadcast_to(tri.astype(F32), (nh, c, c)), g)
    gcol = _hnn(gc, jnp.full((nh, LANES, c), 1.0 / LANES, F32))
    grow = _hnt(jnp.full((nh, c, LANES), 1.0 / LANES, F32), gc)
    decay = jnp.where(tri, jnp.exp(jnp.where(tri, gcol - grow, 0.0)), 0.0)
    kb = k * beta
    m = jnp.where(strict, _bnt(kb, k) * decay, 0.0)
    t_inv = _unit_lower_inverse(m) if t_known is None else _known_inverse(m, t_known)
    eg = jnp.exp(gc)
    u = _bnn(t_inv, v * beta)
    w = _bnn(t_inv, kb * eg)
    attn = _bnt(q, k) * decay
    v_new = u - _bnn(w, st)
    o = _bnn(q * eg, st) + _bnn(attn, v_new)
    g_last = jnp.sum(g, axis=1, keepdims=True)
    st_new = st * jnp.exp(g_last) + _btn(k * jnp.exp(g_last - gc), v_new)
    o = o * lax.rsqrt(jnp.mean(o * o, -1, keepdims=True) + EPS) * o_norm
    return o * _silu(gate), st_new, t_inv


GDN_HP = 8
_GW = GDN_HP * GDN_D
_GB = GDN_H // GDN_HP


def _gdn_specs(n_chunks, rev):
    def tok(col):
        if rev:
            return pl.BlockSpec((GDN_C, _GW), lambda h, n: (n_chunks - 1 - n, col + h))
        return pl.BlockSpec((GDN_C, _GW), lambda h, n: (n, col + h))
    par = pl.BlockSpec((1, _GW), lambda h, n: (0, h))
    shared = pl.BlockSpec((1, GDN_D), lambda h, n: (0, 0))
    if rev:
        st = pl.BlockSpec((GDN_HP, None, GDN_D, GDN_D), lambda h, n: (h, n_chunks - 1 - n, 0, 0))
    else:
        st = pl.BlockSpec((GDN_HP, None, GDN_D, GDN_D), lambda h, n: (h, n, 0, 0))
    return tok, par, shared, st


def _heads(ref):
    return jnp.stack([ref[:, h * GDN_D:(h + 1) * GDN_D] for h in range(ref.shape[1] // GDN_D)])


def gdn_chunk_fwd(qkv, z, a_log_x, dt_bias_x, o_norm, name):
    T = qkv.shape[0]
    n_chunks = T // GDN_C
    H = GDN_H
    tok, par, shared, st_spec = _gdn_specs(n_chunks, False)

    def body(q_ref, k_ref, v_ref, g_ref, bl_ref, al_ref, a_ref, dt_ref, on_ref, o_ref, st_ref, ti_ref, state):
        @pl.when(pl.program_id(1) == 0)
        def _():
            state[...] = jnp.zeros_like(state)

        st = state[...]
        st_ref[...] = st
        o, st_new, t_inv = _gdn_chunk(_heads(q_ref), _heads(k_ref), _heads(v_ref), _heads(g_ref), _heads(bl_ref), _heads(al_ref),
                                      _heads(a_ref), _heads(dt_ref), on_ref[...], st)
        for hh in range(GDN_HP):
            o_ref[:, hh * GDN_D:(hh + 1) * GDN_D] = o[hh].astype(o_ref.dtype)
        ti_ref[...] = t_inv
        state[...] = st_new

    B = _GB
    return pl.pallas_call(
        body, name=name, grid=(B, n_chunks),
        in_specs=[tok(0), tok(B), tok(2 * B), tok(3 * B), tok(4 * B), tok(5 * B), par, par, shared],
        out_specs=[tok(0), st_spec, pl.BlockSpec((GDN_HP, None, GDN_C, GDN_C), lambda h, n: (h, n, 0, 0))],
        out_shape=[jax.ShapeDtypeStruct((T, H * GDN_D), BF16), jax.ShapeDtypeStruct((H, n_chunks, GDN_D, GDN_D), F32),
                   jax.ShapeDtypeStruct((H, n_chunks, GDN_C, GDN_C), F32)],
        scratch_shapes=[pltpu.VMEM((GDN_HP, GDN_D, GDN_D), F32)],
        compiler_params=_cparams(("parallel", "arbitrary")),
    )(qkv, qkv, qkv, z, z, z, a_log_x, dt_bias_x, o_norm)


def gdn_chunk_bwd(qkv, z, a_log_x, dt_bias_x, o_norm, states, t_invs, do, name):
    T = qkv.shape[0]
    n_chunks = T // GDN_C
    H = GDN_H
    tok, par, shared, st_spec = _gdn_specs(n_chunks, True)

    def body(q_ref, k_ref, v_ref, g_ref, bl_ref, al_ref, a_ref, dt_ref, on_ref, st_ref, ti_ref, do_ref,
             dq_ref, dk_ref, dv_ref, dg_ref, dbl_ref, dal_ref, da_ref, ddt_ref, don_ref, dstate):
        h, n = pl.program_id(0), pl.program_id(1)

        @pl.when(n == 0)
        def _():
            dstate[...] = jnp.zeros_like(dstate)
            da_ref[...] = jnp.zeros_like(da_ref)
            ddt_ref[...] = jnp.zeros_like(ddt_ref)

        @pl.when((n == 0) & (h == 0))
        def _():
            don_ref[...] = jnp.zeros_like(don_ref)

        t_known = ti_ref[...]
        _, vjp = jax.vjp(lambda *ins: _gdn_chunk(*ins, t_known=t_known)[:2],
                         _heads(q_ref), _heads(k_ref), _heads(v_ref), _heads(g_ref), _heads(bl_ref), _heads(al_ref),
                         _heads(a_ref), _heads(dt_ref), on_ref[...], st_ref[...])
        dq, dk, dv, dg, dbl, dal, da, ddt, don, dst = vjp((_heads(do_ref).astype(F32), dstate[...]))
        for hh in range(GDN_HP):
            cols = slice(hh * GDN_D, (hh + 1) * GDN_D)
            dq_ref[:, cols] = dq[hh]
            dk_ref[:, cols] = dk[hh]
            dv_ref[:, cols] = dv[hh]
            dg_ref[:, cols] = dg[hh].astype(dg_ref.dtype)
            dbl_ref[:, cols] = dbl[hh].astype(dbl_ref.dtype)
            dal_ref[:, cols] = dal[hh].astype(dal_ref.dtype)
            da_ref[:, cols] += da[hh]
            ddt_ref[:, cols] += ddt[hh]
        don_ref[...] += don
        dstate[...] = dst

    tok0 = tok(0)
    B = _GB
    f32_tok = jax.ShapeDtypeStruct((T, H * GDN_D), F32)
    bf_tok = jax.ShapeDtypeStruct((T, H * GDN_D), BF16)
    par_sh = jax.ShapeDtypeStruct((1, H * GDN_D), F32)
    return pl.pallas_call(
        body, name=name, grid=(B, n_chunks),
        in_specs=[tok(0), tok(B), tok(2 * B), tok(3 * B), tok(4 * B), tok(5 * B), par, par, shared, st_spec,
                  pl.BlockSpec((GDN_HP, None, GDN_C, GDN_C), lambda h, n: (h, n_chunks - 1 - n, 0, 0)), tok0],
        out_specs=[tok0] * 6 + [par, par, shared],
        out_shape=[f32_tok, f32_tok, f32_tok, bf_tok, bf_tok, bf_tok, par_sh, par_sh, jax.ShapeDtypeStruct((1, GDN_D), F32)],
        scratch_shapes=[pltpu.VMEM((GDN_HP, GDN_D, GDN_D), F32)],
        compiler_params=_cparams(("arbitrary", "arbitrary")),
    )(qkv, qkv, qkv, z, z, z, a_log_x, dt_bias_x, o_norm, states, t_invs, do)


def _prefetch_call(body, name, grid, in_specs, out_specs, out_shape, aliases=None):
    return pl.pallas_call(
        body, name=name,
        grid_spec=pltpu.PrefetchScalarGridSpec(num_scalar_prefetch=1, grid=grid, in_specs=in_specs, out_specs=out_specs),
        out_shape=out_shape, input_output_aliases=aliases or {},
        compiler_params=_cparams(("parallel",) * len(grid)))


def cast_into(src, src_row0, rows, slab, row0, me, name):
    width = src.shape[1]
    tr = _pick_rows(rows, 1024, row0, src_row0)
    assert rows % tr == 0 and row0 % tr == 0 and src_row0 % tr == 0

    def body(me_ref, s_ref, *refs):
        refs[-1][...] = s_ref[...].astype(refs[-1].dtype)

    in_specs = [pl.BlockSpec((tr, width), lambda r, me_ref: (src_row0 // tr + r, 0))]
    args = [src]
    aliases = {}
    if slab.arr is not None:
        in_specs.append(_ANY)
        args.append(slab.arr)
        aliases = {2: 0}
    slab.arr = _prefetch_call(
        body, name, (rows // tr,), in_specs,
        pl.BlockSpec((None, tr, width), lambda r, me_ref: (me_ref[0], row0 // tr + r, 0)),
        jax.ShapeDtypeStruct(slab.shape, slab.dtype), aliases)(me, *args)


def pair_add(g, b, c_idx, name):
    n, rh, w = b.shape
    tr = _pick_rows(rh, 1024)
    nb = rh // tr

    def body(c_ref, g_ref, b_ref, o_ref):
        o_ref[...] = (g_ref[...].astype(F32) + b_ref[...].astype(F32)).astype(o_ref.dtype)

    return _prefetch_call(
        body, name, (n, nb),
        [pl.BlockSpec((None, tr, w), lambda k, r, c: (k, c[0] * nb + r, 0)), pl.BlockSpec((None, tr, w), lambda k, r, c: (k, r, 0))],
        pl.BlockSpec((None, tr, w), lambda k, r, c: (k, r, 0)), jax.ShapeDtypeStruct(b.shape, BF16))(c_idx, g, b)


def chip_sum(p, rv, mc, name):
    n, rh, w = p.shape
    tr = _pick_rows(rh, 512)
    nb = rh // tr

    def body(mc_ref, p_ref, rv_ref, o_ref):
        me = mc_ref[0]
        acc = None
        for k in range(n):
            part = jnp.where(me == k, p_ref[...], rv_ref[k]).astype(F32)
            acc = part if acc is None else acc + part
        o_ref[...] = acc

    return _prefetch_call(
        body, name, (nb,),
        [pl.BlockSpec((None, tr, w), lambda r, mc_ref: (mc_ref[0], r, 0)), pl.BlockSpec((n, tr, w), lambda r, mc_ref: (0, r, 0))],
        pl.BlockSpec((tr, w), lambda r, mc_ref: (mc_ref[1] * nb + r, 0)), jax.ShapeDtypeStruct((2 * rh, w), F32))(mc, p, rv)


def adamw(red, row0, w, m, v, w_row0, rows, prev, name):
    cols = w.shape[1]
    tr = _pick_rows(rows, 512, row0, w_row0)
    assert rows % tr == 0 and row0 % tr == 0 and w_row0 % tr == 0

    def body(g_ref, w_ref, m_ref, v_ref, *refs):
        go_ref, d_ref, nm_ref, nv_ref = refs[-4:]
        gv = g_ref[...]
        nm = ADAM_B1 * m_ref[...] + (1.0 - ADAM_B1) * gv
        nv = ADAM_B2 * v_ref[...] + (1.0 - ADAM_B2) * (gv * gv)
        m_hat = nm / (1.0 - ADAM_B1 ** ADAM_STEP)
        v_hat = nv / (1.0 - ADAM_B2 ** ADAM_STEP)
        go_ref[...] = gv
        d_ref[...] = -ADAM_LR * (m_hat / (jnp.sqrt(v_hat) + ADAM_EPS) + ADAM_WD * w_ref[...])
        nm_ref[...] = nm
        nv_ref[...] = nv

    spec = pl.BlockSpec((tr, cols), lambda r: (w_row0 // tr + r, 0))
    sh = jax.ShapeDtypeStruct(w.shape, F32)
    in_specs = [pl.BlockSpec((tr, cols), lambda r: (row0 // tr + r, 0)), spec, spec, spec]
    args, aliases = [red, w, m, v], {}
    if prev is not None:
        in_specs += [_ANY] * 4
        args += list(prev)
        aliases = {4 + k: k for k in range(4)}
    return pl.pallas_call(
        body, name=name, grid=(rows // tr,), in_specs=in_specs, out_specs=[spec] * 4, out_shape=[sh] * 4,
        input_output_aliases=aliases, compiler_params=_cparams(("parallel",)),
    )(*args)


def _place():
    x, y, c = lax.axis_index("x"), lax.axis_index("y"), lax.axis_index("c")
    chips = [(1 - x, y), (x, 1 - y), (1 - x, 1 - y)]
    return x, y, c, chips


def _chip_index(cx, cy):
    return 2 * cx + cy


def _remote(src, dst, send_sem, recv_sem, to):
    return pltpu.make_async_remote_copy(src_ref=src, dst_ref=dst, send_sem=send_sem, recv_sem=recv_sem,
                                        device_id=to, device_id_type=MESH)


def _comm_call(body, name, ins, out_shapes, n_sems, aliases):
    return pl.pallas_call(
        body, name=name, in_specs=[_ANY] * len(ins), out_specs=[_ANY] * len(out_shapes), out_shape=out_shapes,
        scratch_shapes=[pltpu.SemaphoreType.DMA((n_sems,)), pltpu.SemaphoreType.DMA((n_sems,))],
        input_output_aliases=aliases,
    )(*ins)


def gather_slabs(slabs, name="weight_all_gather"):
    n = len(slabs)

    def body(*refs):
        in_refs, out_refs, send_sems, recv_sems = refs[:n], refs[n:2 * n], refs[-2], refs[-1]
        x, y, c, chips = _place()
        me = _chip_index(x, y)
        sib = (x, y, 1 - c)
        first, passed = [], []
        for a in range(n):
            rh = in_refs[a].shape[1] // 2
            mine = pl.ds(c * rh, rh)
            for j, chip in enumerate(chips):
                cp = _remote(in_refs[a].at[me, mine], out_refs[a].at[me, mine], send_sems.at[6 * a + j],
                             recv_sems.at[6 * a + j], (*chip, c))
                cp.start()
                first.append(cp)
        for a in range(n):
            rh = in_refs[a].shape[1] // 2
            mine = pl.ds(c * rh, rh)
            for j, chip in enumerate(chips):
                landed = out_refs[a].at[_chip_index(*chip), mine]
                _remote(landed, landed, send_sems.at[6 * a + j], recv_sems.at[6 * a + j], (*chip, c)).wait_recv()
                cp = _remote(landed, landed, send_sems.at[6 * a + 3 + j], recv_sems.at[6 * a + 3 + j], sib)
                cp.start()
                passed.append(cp)
        for a in range(n):
            rh = in_refs[a].shape[1] // 2
            theirs = pl.ds((1 - c) * rh, rh)
            for j, chip in enumerate(chips):
                got = out_refs[a].at[_chip_index(*chip), theirs]
                _remote(got, got, send_sems.at[6 * a + 3 + j], recv_sems.at[6 * a + 3 + j], sib).wait_recv()
        for cp in first + passed:
            cp.wait_send()

    return _comm_call(body, name, slabs, [jax.ShapeDtypeStruct(s.shape, s.dtype) for s in slabs], 6 * n,
                      {a: a for a in range(n)})


def pair_swap_halves(slabs, name="grad_pair_swap"):
    n = len(slabs)

    def body(*refs):
        in_refs, out_refs, send_sems, recv_sems = refs[:n], refs[n:2 * n], refs[-2], refs[-1]
        x, y, c, _ = _place()
        cps = []
        for a in range(n):
            rh = in_refs[a].shape[1] // 2
            cp = _remote(in_refs[a].at[:, pl.ds((1 - c) * rh, rh), :], out_refs[a], send_sems.at[a], recv_sems.at[a], (x, y, 1 - c))
            cp.start()
            cps.append(cp)
        for cp in cps:
            cp.wait()

    outs = [jax.ShapeDtypeStruct((s.shape[0], s.shape[1] // 2, s.shape[2]), s.dtype) for s in slabs]
    return _comm_call(body, name, slabs, outs, n, {})


def chip_exchange(parts, name="grad_chip_exchange"):
    n = len(parts)

    def body(*refs):
        in_refs, out_refs, send_sems, recv_sems = refs[:n], refs[n:2 * n], refs[-2], refs[-1]
        x, y, c, chips = _place()
        me = _chip_index(x, y)
        sends = []
        for a in range(n):
            for j, chip in enumerate(chips):
                cp = _remote(in_refs[a].at[_chip_index(*chip)], out_refs[a].at[me], send_sems.at[3 * a + j],
                             recv_sems.at[3 * a + j], (*chip, c))
                cp.start()
                sends.append(cp)
        for a in range(n):
            for j, chip in enumerate(chips):
                got = out_refs[a].at[_chip_index(*chip)]
                _remote(got, got, send_sems.at[3 * a + j], recv_sems.at[3 * a + j], (*chip, c)).wait_recv()
        for cp in sends:
            cp.wait_send()

    return _comm_call(body, name, parts, [jax.ShapeDtypeStruct(p.shape, p.dtype) for p in parts], 3 * n, {})


def pair_join_halves(reds, name="grad_pair_join"):
    n = len(reds)

    def body(*refs):
        in_refs, out_refs, send_sems, recv_sems = refs[:n], refs[n:2 * n], refs[-2], refs[-1]
        x, y, c, _ = _place()
        cps = []
        for a in range(n):
            rh = in_refs[a].shape[0] // 2
            mine = pl.ds(c * rh, rh)
            cp = _remote(in_refs[a].at[mine], out_refs[a].at[mine], send_sems.at[a], recv_sems.at[a], (x, y, 1 - c))
            cp.start()
            cps.append(cp)
        for a in range(n):
            rh = in_refs[a].shape[0] // 2
            got = out_refs[a].at[pl.ds((1 - c) * rh, rh)]
            _remote(got, got, send_sems.at[a], recv_sems.at[a], (x, y, 1 - c)).wait_recv()
        for cp in cps:
            cp.wait_send()

    return _comm_call(body, name, reds, [jax.ShapeDtypeStruct(r.shape, r.dtype) for r in reds], n, {a: a for a in range(n)})


_HBM = pl.BlockSpec(memory_space=pltpu.HBM)
_SEM = pl.BlockSpec(memory_space=pltpu.SEMAPHORE)
_EFFECT = pltpu.SideEffectType.DATAFLOW_SIDE_EFFECTING


def _in_hbm(a):
    return pltpu.with_memory_space_constraint(a, pltpu.HBM)


def _hbm_like(a):
    return pltpu.HBM(a.shape, a.dtype)


def _start_call(body, name, ins, n_sems, after):
    n = len(ins)
    res = pl.pallas_call(
        body, name=name, in_specs=[_HBM] * n + [_ANY],
        out_specs=[_SEM, _SEM] + [_HBM] * n + [pl.BlockSpec(memory_space=pltpu.VMEM)],
        out_shape=[pltpu.SemaphoreType.DMA((n_sems,)), pltpu.SemaphoreType.DMA((n_sems,))] + [_hbm_like(a) for a in ins]
        + [jax.ShapeDtypeStruct((8, LANES), F32)],
        input_output_aliases={a: 2 + a for a in range(n)},
        compiler_params=pltpu.CompilerParams(has_side_effects=_EFFECT),
    )(*[_in_hbm(a) for a in ins], after)
    return res[0], res[1], list(res[2:2 + n]), res[-1]


def _wait_call(body, name, thru, send_sems, recv_sems, after):
    n = len(thru)
    return pl.pallas_call(
        body, name=name, in_specs=[_HBM] * n + [_SEM, _SEM, _ANY], out_specs=[_HBM] * n,
        out_shape=[_hbm_like(a) for a in thru], input_output_aliases={a: a for a in range(n)},
        compiler_params=pltpu.CompilerParams(has_side_effects=_EFFECT),
    )(*thru, send_sems, recv_sems, after)


def gather_start(slabs, after, name="weight_gather_start"):
    n = len(slabs)

    def body(*refs):
        g_refs, send_sems, recv_sems, token = refs[:n], refs[n + 1], refs[n + 2], refs[-1]
        x, y, c, chips = _place()
        me = _chip_index(x, y)
        for a in range(n):
            rh = g_refs[a].shape[1] // 2
            mine = g_refs[a].at[me, pl.ds(c * rh, rh)]
            for j, chip in enumerate(chips):
                _remote(mine, mine, send_sems.at[3 * a + j], recv_sems.at[3 * a + j], (*chip, c)).start()
        token[...] = jnp.zeros_like(token)

    return _start_call(body, name, slabs, 3 * n, after)


def gather_wait(send_sems, recv_sems, thru, after, name="weight_gather_wait"):
    n = len(thru)

    def body(*refs):
        g_refs, send_sems, recv_sems = refs[:n], refs[n], refs[n + 1]
        x, y, c, chips = _place()
        me = _chip_index(x, y)
        for a in range(n):
            rh = g_refs[a].shape[1] // 2
            rows = pl.ds(c * rh, rh)
            for j, chip in enumerate(chips):
                mine, got = g_refs[a].at[me, rows], g_refs[a].at[_chip_index(*chip), rows]
                _remote(mine, mine, send_sems.at[3 * a + j], recv_sems.at[3 * a + j], (*chip, c)).wait_send()
                _remote(got, got, send_sems.at[3 * a + j], recv_sems.at[3 * a + j], (*chip, c)).wait_recv()

    return _wait_call(body, name, thru, send_sems, recv_sems, after)


def gather_forward(slabs, name="weight_gather_forward"):
    n = len(slabs)

    def body(*refs):
        in_refs, out_refs, send_sems, recv_sems = refs[:n], refs[n:2 * n], refs[-2], refs[-1]
        x, y, c, chips = _place()
        sib = (x, y, 1 - c)
        sends = []
        for a in range(n):
            rh = in_refs[a].shape[1] // 2
            for j, chip in enumerate(chips):
                k = _chip_index(*chip)
                cp = _remote(in_refs[a].at[k, pl.ds(c * rh, rh)], out_refs[a].at[k, pl.ds(c * rh, rh)], send_sems.at[3 * a + j],
                             recv_sems.at[3 * a + j], sib)
                cp.start()
                sends.append(cp)
        for a in range(n):
            rh = in_refs[a].shape[1] // 2
            for j, chip in enumerate(chips):
                got = out_refs[a].at[_chip_index(*chip), pl.ds((1 - c) * rh, rh)]
                _remote(got, got, send_sems.at[3 * a + j], recv_sems.at[3 * a + j], sib).wait_recv()
        for cp in sends:
            cp.wait_send()

    return _comm_call(body, name, slabs, [jax.ShapeDtypeStruct(s.shape, s.dtype) for s in slabs], 3 * n, {a: a for a in range(n)})


def exchange_start(parts, after, name="grad_exchange_start"):
    n = len(parts)

    def body(*refs):
        p_refs, land_refs, send_sems, recv_sems, token = refs[:n], refs[n:2 * n], refs[2 * n + 1], refs[2 * n + 2], refs[-1]
        x, y, c, chips = _place()
        me = _chip_index(x, y)
        for a in range(n):
            for j, chip in enumerate(chips):
                _remote(p_refs[a].at[_chip_index(*chip)], land_refs[a].at[me], send_sems.at[3 * a + j], recv_sems.at[3 * a + j],
                        (*chip, c)).start()
        token[...] = jnp.zeros_like(token)

    return _start_call(body, name, list(parts) + [lax.empty(p.shape, p.dtype) for p in parts], 3 * n, after)


def exchange_wait(send_sems, recv_sems, thru, after, name="grad_exchange_wait"):
    n = len(thru) // 2

    def body(*refs):
        p_refs, land_refs, send_sems, recv_sems = refs[:n], refs[n:2 * n], refs[2 * n], refs[2 * n + 1]
        x, y, c, chips = _place()
        me = _chip_index(x, y)
        for a in range(n):
            for j, chip in enumerate(chips):
                k = _chip_index(*chip)
                _remote(p_refs[a].at[k], land_refs[a].at[me], send_sems.at[3 * a + j], recv_sems.at[3 * a + j], (*chip, c)).wait_send()
                _remote(land_refs[a].at[k], land_refs[a].at[k], send_sems.at[3 * a + j], recv_sems.at[3 * a + j], (*chip, c)).wait_recv()

    res = _wait_call(body, name, thru, send_sems, recv_sems, after)
    return res[:n], res[n:]


_SLABS = {
    "mla_w_in": [("mla_w_in", 1, 0, 2)], "mla_w_uq": [("mla_w_uq", 2, 0, 2)], "mla_w_ukv": [("mla_w_ukv", 2, 0, 2)],
    "l0_mla_w_o": [("mla_w_o", 1, 0, 1)],
    "l0_w1024": [("mlp_w1", 2, 0, 1), ("mlp_w2", 1, 0, 1), ("xa_w_q", 1, 0, 1), ("xa_w_o", 1, 0, 1)],
    "l0_xa_w_kv": [("xa_w_kv", 2, 0, 1)],
    "l1_w1024": [("mlp_w1", 2, 1, 2), ("mlp_w2", 1, 1, 2), ("xa_w_q", 1, 1, 2), ("xa_w_o", 1, 1, 2), ("gdn_w_o", 1, 0, 1)],
    "l1_xa_w_kv": [("xa_w_kv", 2, 1, 2)], "gdn_w_in": [("gdn_w_in", 2, 0, 1)],
    "l23_w1024": [("mlp_w1", 2, 2, 4), ("mlp_w2", 1, 2, 4), ("xa_w_q", 1, 2, 4), ("xa_w_o", 1, 2, 4), ("mla_w_o", 1, 1, 2),
                  ("sc_w_o", 1, 0, 1)],
    "l23_xa_w_kv": [("xa_w_kv", 2, 2, 4)], "sc_w_in": [("sc_w_in", 2, 0, 1)],
}
_GROUPS = [(["mla_w_in", "mla_w_uq", "mla_w_ukv", "l0_mla_w_o"], None),
           (["l0_w1024", "l0_xa_w_kv"], (0, "xa")),
           (["l1_w1024", "l1_xa_w_kv", "gdn_w_in"], (1, "mix")),
           (["l23_w1024", "l23_xa_w_kv", "sc_w_in"], (2, "mix"))]
_RELAID = ("mla_w_in", "mla_w_uq", "mla_w_ukv", "gdn_w_in")
_SMALL = [("mla_q_norm", 1), ("mla_kv_norm", 1), ("gdn_conv_w", 2), ("sc_conv_w", 2)]
_REPL = ["gdn_a_log", "gdn_dt_bias", "gdn_o_norm", "norm_mix", "norm_mem", "norm_mlp", "mem_norm", "final_norm"]
_WEIGHTS = ['mla_w_in', 'mla_q_norm', 'mla_kv_norm', 'mla_w_uq', 'mla_w_ukv', 'mla_w_o', 'gdn_w_in', 'gdn_conv_w',
            'gdn_a_log', 'gdn_dt_bias', 'gdn_o_norm', 'gdn_w_o', 'sc_w_in', 'sc_conv_w', 'sc_w_o', 'norm_mix',
            'norm_mem', 'norm_mlp', 'xa_w_q', 'xa_w_kv', 'xa_w_o', 'mlp_w1', 'mlp_w2', 'mem_norm', 'final_norm']


class Layout:
    def __init__(self, shard_shapes):
        self.members, self.where, self.slab_dims = {}, {}, {}
        for slab, members in _SLABS.items():
            off, rows = 0, []
            for name, axis, l0, l1 in members:
                _, rpl, width = shard_shapes[name]
                rows.append((name, off, l0, l1, rpl))
                for layer in range(l0, l1):
                    self.where[(name, layer)] = (slab, off + (layer - l0) * rpl, rpl, width, axis)
                off += (l1 - l0) * rpl
            self.members[slab], self.slab_dims[slab] = rows, (off, width)

    def new_slabs(self, dtype):
        return {s: Slab(rows, width, dtype) for s, (rows, width) in self.slab_dims.items()}

    def loc(self, slabs, name, layer):
        slab, row0, rpl, width, axis = self.where[(name, layer)]
        if axis == 1:
            return Loc(slabs[slab], row0, N_CHIPS * rpl, width, 0)
        return Loc(slabs[slab], row0, rpl, N_CHIPS * width, 1)

    def _whole(self, name):
        (member,) = self.members[name]
        _, off, l0, l1, rpl = member
        assert off == 0 and l0 == 0
        return l1, rpl, self.slab_dims[name][1], dict((n, a) for n, a, _, _ in _SLABS[name])[name]

    def full(self, slabs, name):
        layers, rpl, width, axis = self._whole(name)
        blocks = slabs[name].arr.reshape(N_CHIPS, layers, rpl, width)
        return jnp.concatenate([blocks[s] for s in range(N_CHIPS)], axis=axis)

    def put_full(self, slabs, name, grad):
        layers, rpl, width, axis = self._whole(name)
        parts = jnp.stack(jnp.split(grad, N_CHIPS, axis=axis)).reshape(N_CHIPS, layers * rpl, width)
        slabs[name].arr = parts.astype(slabs[name].dtype)


def _small_pack(vals, names):
    flat = jnp.concatenate([vals[n].astype(F32).reshape(-1) for n in names])
    return jnp.pad(flat, (0, SMALL_ROWS * SMALL_COLS - flat.shape[0])).reshape(SMALL_ROWS, SMALL_COLS)


def _small_unpack(flat, like, names):
    out, off = {}, 0
    flat = flat.reshape(-1)
    for n in names:
        out[n] = flat[off:off + like[n].size].reshape(like[n].shape)
        off += like[n].size
    return out


_MLA_CFG = _Attn(MLA_H, 2 * LANES, MLA_NOPE, MLA_V, True, (MLA_NOPE + MLA_ROPE) ** -0.5, hp=8, hp_kv=4)
_XA_CFG = _Attn(XA_H, XA_D, XA_D, XA_D, False, XA_D ** -0.5, hp=4, hp_kv=4)


def _mla_weights(w_in, w_uq, w_ukv):
    w_in_p = jnp.pad(w_in, ((0, 0), (0, MLA_ZPAD - w_in.shape[1])))
    w_uq_p = jnp.pad(w_uq.reshape(MLA_QR, MLA_H, MLA_NOPE + MLA_ROPE), ((0, 0), (0, 0), (0, 2 * LANES - MLA_NOPE - MLA_ROPE)))
    w_uq_p = w_uq_p.reshape(MLA_QR, MLA_H * 2 * LANES)
    kv = w_ukv.reshape(MLA_KVR, MLA_H, MLA_NOPE + MLA_V)
    w_ukv_p = jnp.concatenate([kv[:, :, :MLA_NOPE].reshape(MLA_KVR, -1), kv[:, :, MLA_NOPE:].reshape(MLA_KVR, -1)], axis=1)
    return w_in_p, w_uq_p, w_ukv_p


def _mla_weight_grads(d_in_p, d_uq_p, d_ukv_p):
    d_in = d_in_p[:, :MLA_QR + MLA_KVR + MLA_ROPE]
    d_uq = d_uq_p.reshape(MLA_QR, MLA_H, 2 * LANES)[:, :, :MLA_NOPE + MLA_ROPE].reshape(MLA_QR, -1)
    half = MLA_H * MLA_NOPE
    d_ukv = jnp.concatenate([d_ukv_p[:, :half].reshape(MLA_KVR, MLA_H, MLA_NOPE),
                             d_ukv_p[:, half:].reshape(MLA_KVR, MLA_H, MLA_V)], axis=2).reshape(MLA_KVR, -1)
    return d_in, d_uq, d_ukv


def _mla_fwd(xs, h, wts, w_o, qn, kvn, tabs, g_next, tag):
    w_in_p, w_uq_p, w_ukv_p = wts
    z = mm(h, w_in_p, "nn", f"{tag}_in")
    cq, ckv, kr = mla_mid_fwd(z, qn, kvn, tabs, f"{tag}_mid")
    q = rope_q(mm(cq, w_uq_p, "nn", f"{tag}_uq"), tabs, False, f"{tag}_ropeq")
    kv = mm(ckv, w_ukv_p, "nn", f"{tag}_ukv", outs=(BF16,))
    o, lse = flash_fwd(_MLA_CFG, q, kv, kv, kr, f"{tag}_attn")
    xs, h_next = residual_norm(o, w_o, xs, g_next, f"{tag}_out")
    return xs, h_next, (z, cq, ckv, kr, q, kv, o, lse)


def _mla_bwd(dx, h, wts, w_o, g_wo, qn, kvn, tabs, saved, tag):
    w_in_p, w_uq_p, w_ukv_p = wts
    z, cq, ckv, kr, q, kv, o, lse = saved
    mm(o, dx, "tn", f"{tag}_dwo", outs=(BF16,), out_loc=g_wo)
    do = mm(dx, w_o, "nt", f"{tag}_do", outs=(BF16,))
    dq, delta = flash_dq(_MLA_CFG, q, kv, kv, kr, o, do, lse, F32, f"{tag}_attn_dq")
    dk1, dv, dkr = flash_dkv(_MLA_CFG, q, kv, kv, kr, do, lse, delta, BF16, f"{tag}_attn_dkv")
    dqp = rope_q(dq, tabs, True, f"{tag}_ropeq_t")
    d_uq_p = mm(cq, dqp, "tn", f"{tag}_duq")
    dcq = mm(dqp, w_uq_p, "nt", f"{tag}_dcq")
    dkv = jnp.concatenate([dk1, dv], axis=1)
    d_ukv_p = mm(ckv, dkv, "tn", f"{tag}_dukv")
    dckv = mm(dkv, w_ukv_p, "nt", f"{tag}_dckv")
    dz, dqn, dkvn = mla_mid_bwd(z, qn, kvn, tabs, dcq, dckv, dkr, f"{tag}_mid_bwd")
    d_in_p = mm(h, dz, "tn", f"{tag}_din")
    dh = mm(dz, w_in_p, "nt", f"{tag}_dh")
    d_in, d_uq, d_ukv = _mla_weight_grads(d_in_p, d_uq_p, d_ukv_p)
    return dh, dict(mla_w_in=d_in, mla_w_uq=d_uq, mla_w_ukv=d_ukv, mla_q_norm=dqn, mla_kv_norm=dkvn)


_GDN_QKV = 3 * GDN_H * GDN_D
_GDN_GATE_END = _GDN_QKV + GDN_H * GDN_D


def _gdn_weights(w_in):
    rep = lambda cols: jnp.repeat(cols, GDN_D, axis=1)
    return jnp.concatenate([w_in[:, :_GDN_GATE_END], rep(w_in[:, _GDN_GATE_END:_GDN_GATE_END + GDN_H]),
                            rep(w_in[:, _GDN_GATE_END + GDN_H:])], axis=1)


def _fold(x):
    return x.reshape(x.shape[0], -1, GDN_D).sum(-1)


def _gdn_fwd(xs, h, w_in_x, conv_w, a_log, dt_bias, o_norm, w_o, g_next, tag):
    z = mm(h, w_in_x, "nn", f"{tag}_in")
    qkv = gdn_conv_fwd(z, conv_w, f"{tag}_conv")
    a_x, dt_x = jnp.repeat(a_log.reshape(1, -1), GDN_D, axis=1), jnp.repeat(dt_bias.reshape(1, -1), GDN_D, axis=1)
    og, states, t_invs = gdn_chunk_fwd(qkv, z, a_x, dt_x, o_norm.reshape(1, -1), f"{tag}_chunks")
    xs, h_next = residual_norm(og, w_o, xs, g_next, f"{tag}_out")
    return xs, h_next, (z, qkv, a_x, dt_x, og, states, t_invs)


def _gdn_bwd(dx, h, w_in_x, conv_w, o_norm, w_o, g_wo, saved, tag):
    z, qkv, a_x, dt_x, og, states, t_invs = saved
    mm(og, dx, "tn", f"{tag}_dwo", outs=(BF16,), out_loc=g_wo)
    dog = mm(dx, w_o, "nt", f"{tag}_dog")
    dq, dk, dv, dgate, dbl, dal, da_x, ddt_x, don = gdn_chunk_bwd(qkv, z, a_x, dt_x, o_norm.reshape(1, -1), states, t_invs, dog,
                                                                  f"{tag}_chunks_bwd")
    dpre, dconv = gdn_conv_bwd(z, conv_w, jnp.concatenate([dq, dk, dv], axis=1), f"{tag}_conv_bwd")
    dz = jnp.concatenate([dpre, dgate, dbl, dal], axis=1)
    d_in_x = mm(h, dz, "tn", f"{tag}_din")
    dh = mm(dz, w_in_x, "nt", f"{tag}_dh", tn=512)
    ge = _GDN_GATE_END
    d_in = jnp.concatenate([d_in_x[:, :ge], _fold(d_in_x[:, ge:ge + GDN_H * GDN_D]), _fold(d_in_x[:, ge + GDN_H * GDN_D:])], axis=1)
    return dh, dict(gdn_w_in=d_in, gdn_conv_w=dconv, gdn_a_log=_fold(da_x).reshape(-1), gdn_dt_bias=_fold(ddt_x).reshape(-1),
                    gdn_o_norm=don.reshape(-1))


def _sc_fwd(xs, h, w_in, conv_w, w_o, g_next, tag):
    z = mm(h, w_in, "nn", f"{tag}_in")
    y = sc_fwd(z, conv_w, f"{tag}_conv")
    xs, h_next = residual_norm(y, w_o, xs, g_next, f"{tag}_out")
    return xs, h_next, (z, y)


def _sc_bwd(dx, h, w_in, g_win, conv_w, w_o, g_wo, saved, tag):
    z, y = saved
    mm(y, dx, "tn", f"{tag}_dwo", outs=(BF16,), out_loc=g_wo)
    dy = mm(dx, w_o, "nt", f"{tag}_dy")
    db, dc, du, dconv = sc_bwd(z, conv_w, dy, f"{tag}_conv_bwd")
    dz = jnp.concatenate([db, dc, du], axis=1)
    mm(h, dz, "tn", f"{tag}_din", outs=(BF16,), out_loc=g_win)
    dh = mm(dz, w_in, "nt", f"{tag}_dh")
    return dh, dict(sc_conv_w=dconv)


def local_step(x, mem, pos, target, lay, wslabs, gslabs, small, before=None, after_bwd=None):
    depth = small["norm_mix"].shape[0]
    W = lambda name, layer: lay.loc(wslabs, name, layer)
    G = lambda name, layer: lay.loc(gslabs, name, layer)
    tabs = rope_tables(pos)
    mem_n = rmsnorm_fwd(mem, small["mem_norm"], "mem_norm")
    full = {n: lay.full(wslabs, n) for n in ("mla_w_in", "mla_w_uq", "mla_w_ukv")}
    mla_w = [_mla_weights(full["mla_w_in"][j], full["mla_w_uq"][j], full["mla_w_ukv"][j]) for j in range(full["mla_w_in"].shape[0])]
    gdn_in_x = {}

    xs, h_pre = x, None
    saved = []
    for i in range(depth):
        j, kind = i // 3, i % 3
        tag = f"l{i}"
        if before is not None:
            xs = before(i, "mix", xs)
        if kind == 1:
            gdn_in_x[j] = _gdn_weights(lay.full(wslabs, "gdn_w_in")[j])
        x_a = xs
        h = h_pre if h_pre is not None else rmsnorm_fwd(xs, small["norm_mix"][i], f"{tag}_norm_mix")
        g_mem = small["norm_mem"][i]
        if kind == 0:
            xs, hn, mix = _mla_fwd(xs, h, mla_w[j], W("mla_w_o", j), small["mla_q_norm"][j], small["mla_kv_norm"][j], tabs, g_mem,
                                   f"{tag}_mla")
        elif kind == 1:
            xs, hn, mix = _gdn_fwd(xs, h, gdn_in_x[j], small["gdn_conv_w"][j], small["gdn_a_log"][j], small["gdn_dt_bias"][j],
                                   small["gdn_o_norm"][j], W("gdn_w_o", j), g_mem, f"{tag}_gdn")
        else:
            xs, hn, mix = _sc_fwd(xs, h, W("sc_w_in", j), small["sc_conv_w"][j], W("sc_w_o", j), g_mem, f"{tag}_sc")
        if before is not None:
            xs = before(i, "xa", xs)
        x_b = xs
        xq = mm(hn, W("xa_w_q", i), "nn", f"{tag}_xa_q", outs=(BF16,))
        xkv = mm(mem_n, W("xa_w_kv", i), "nn", f"{tag}_xa_kv", outs=(BF16,))
        xo, xlse = flash_fwd(_XA_CFG, xq, xkv, xkv, None, f"{tag}_xa_attn")
        xs, hm = residual_norm(xo, W("xa_w_o", i), xs, small["norm_mlp"][i], f"{tag}_xa_out")
        x_c = xs
        h1, act = mm(hm, W("mlp_w1", i), "nn", f"{tag}_mlp_up", outs=(BF16, BF16), epi=_epi_relu2)
        xs, h_pre = residual_norm(act, W("mlp_w2", i), xs, small["norm_mix"][i + 1] if i + 1 < depth else None,
                                  f"{tag}_mlp_down", tm=512)
        saved.append((x_a, h, mix, x_b, hn, xq, xkv, xo, xlse, x_c, hm, h1, act))

    se, dx, d_final = loss_head(xs, small["final_norm"], target)

    per_layer = {n: [None] * depth for n in ("norm_mix", "norm_mem", "norm_mlp")}
    mixer = {}
    dmem_n = jnp.zeros(mem.shape, F32)
    for i in reversed(range(depth)):
        j, kind = i // 3, i % 3
        tag = f"l{i}"
        x_a, h, mix, x_b, hn, xq, xkv, xo, xlse, x_c, hm, h1, act = saved[i]
        mm(act, dx, "tn", f"{tag}_mlp_dw2", outs=(BF16,), out_loc=G("mlp_w2", i))
        dh1 = mm(dx, W("mlp_w2", i), "nt", f"{tag}_mlp_dh1", outs=(BF16,), epi=_epi_relu2_bwd, extras=(h1,))
        mm(hm, dh1, "tn", f"{tag}_mlp_dw1", outs=(BF16,), out_loc=G("mlp_w1", i))
        dx, dg = mm(dh1, W("mlp_w1", i), "nt", f"{tag}_mlp_dhm", epi=_epi_norm_bwd, extras=(x_c, dx), vecs=(small["norm_mlp"][i],),
                    row_outs=1, tm=512)
        per_layer["norm_mlp"][i] = dg.reshape(-1)
        mm(xo, dx, "tn", f"{tag}_xa_dwo", outs=(BF16,), out_loc=G("xa_w_o", i))
        dxo = mm(dx, W("xa_w_o", i), "nt", f"{tag}_xa_do", outs=(BF16,))
        dxq, xdelta = flash_dq(_XA_CFG, xq, xkv, xkv, None, xo, dxo, xlse, BF16, f"{tag}_xa_attn_dq")
        dxk, dxv = flash_dkv(_XA_CFG, xq, xkv, xkv, None, dxo, xlse, xdelta, BF16, f"{tag}_xa_attn_dkv")
        dxkv = jnp.concatenate([dxk, dxv], axis=1)
        mm(hn, dxq, "tn", f"{tag}_xa_dwq", outs=(BF16,), out_loc=G("xa_w_q", i))
        dhn = mm(dxq, W("xa_w_q", i), "nt", f"{tag}_xa_dhn")
        mm(mem_n, dxkv, "tn", f"{tag}_xa_dwkv", outs=(BF16,), out_loc=G("xa_w_kv", i))
        dmem_n = mm(dxkv, W("xa_w_kv", i), "nt", f"{tag}_xa_dmem", epi=_epi_add, extras=(dmem_n,))
        dx, per_layer["norm_mem"][i] = rmsnorm_bwd(x_b, small["norm_mem"][i], dhn, dx, f"{tag}_norm_mem_bwd")
        if after_bwd is not None:
            dx = after_bwd(i, "xa", dx)
        if kind == 0:
            dh, gr = _mla_bwd(dx, h, mla_w[j], W("mla_w_o", j), G("mla_w_o", j), small["mla_q_norm"][j], small["mla_kv_norm"][j],
                              tabs, mix, f"{tag}_mla")
        elif kind == 1:
            dh, gr = _gdn_bwd(dx, h, gdn_in_x[j], small["gdn_conv_w"][j], small["gdn_o_norm"][j], W("gdn_w_o", j), G("gdn_w_o", j),
                              mix, f"{tag}_gdn")
        else:
            dh, gr = _sc_bwd(dx, h, W("sc_w_in", j), G("sc_w_in", j), small["sc_conv_w"][j], W("sc_w_o", j), G("sc_w_o", j),
                             mix, f"{tag}_sc")
        if kind == 1:
            lay.put_full(gslabs, "gdn_w_in", gr.pop("gdn_w_in")[None])
        for n, g in gr.items():
            mixer.setdefault(n, {})[j] = g
        dx, per_layer["norm_mix"][i] = rmsnorm_bwd(x_a, small["norm_mix"][i], dh, dx, f"{tag}_norm_mix_bwd")
        if after_bwd is not None:
            dx = after_bwd(i, "mix", dx)

    _, d_mem_norm = rmsnorm_bwd(mem, small["mem_norm"], dmem_n, jnp.zeros(mem.shape, F32), "mem_norm_bwd")
    grads = {n: jnp.stack(v) for n, v in per_layer.items()}
    for n, by_j in mixer.items():
        grads[n] = jnp.stack([by_j[j] for j in sorted(by_j)])
    grads["mem_norm"] = d_mem_norm
    grads["final_norm"] = d_final
    for n in ("mla_w_in", "mla_w_uq", "mla_w_ukv"):
        lay.put_full(gslabs, n, grads.pop(n))
    return se, dx, grads


def kernel(x, mem, positions, mla_w_in, mla_q_norm, mla_kv_norm, mla_w_uq, mla_w_ukv, mla_w_o, gdn_w_in, gdn_conv_w, gdn_a_log, gdn_dt_bias, gdn_o_norm, gdn_w_o, sc_w_in, sc_conv_w, sc_w_o, norm_mix, norm_mem, norm_mlp, xa_w_q, xa_w_kv, xa_w_o, mlp_w1, mlp_w2, mem_norm, final_norm, loss_target, m_mla_w_in, m_mla_q_norm, m_mla_kv_norm, m_mla_w_uq, m_mla_w_ukv, m_mla_w_o, m_gdn_w_in, m_gdn_conv_w, m_gdn_a_log, m_gdn_dt_bias, m_gdn_o_norm, m_gdn_w_o, m_sc_w_in, m_sc_conv_w, m_sc_w_o, m_norm_mix, m_norm_mem, m_norm_mlp, m_xa_w_q, m_xa_w_kv, m_xa_w_o, m_mlp_w1, m_mlp_w2, m_mem_norm, m_final_norm, v_mla_w_in, v_mla_q_norm, v_mla_kv_norm, v_mla_w_uq, v_mla_w_ukv, v_mla_w_o, v_gdn_w_in, v_gdn_conv_w, v_gdn_a_log, v_gdn_dt_bias, v_gdn_o_norm, v_gdn_w_o, v_sc_w_in, v_sc_conv_w, v_sc_w_o, v_norm_mix, v_norm_mem, v_norm_mlp, v_xa_w_q, v_xa_w_kv, v_xa_w_o, v_mlp_w1, v_mlp_w2, v_mem_norm, v_final_norm):
    given = dict(locals())
    p = {n: given[n] for n in _WEIGHTS}
    mom = {n: given["m_" + n] for n in _WEIGHTS}
    var = {n: given["v_" + n] for n in _WEIGHTS}
    split = sorted({n for members in _SLABS.values() for n, _, _, _ in members})
    lay = Layout({n: p[n].shape for n in split})
    flat2d = lambda a: a.reshape(-1, a.shape[-1])

    me = (2 * lax.axis_index("x") + lax.axis_index("y")).astype(jnp.int32)
    core = lax.axis_index("c").astype(jnp.int32)
    me1, c1, mc = me.reshape(1), core.reshape(1), jnp.stack([me, core])

    wslabs = lay.new_slabs(BF16)
    for slab, members in lay.members.items():
        for name, off, l0, l1, rpl in members:
            cast_into(flat2d(p[name]), l0 * rpl, (l1 - l0) * rpl, wslabs[slab], off, me1, f"cast_{slab}_{name}")
    small_names = [n for n, _ in _SMALL]
    words = lax.bitcast_convert_type(jnp.concatenate([p[n].reshape(-1) for n in small_names]), BF16).reshape(-1)
    words = jnp.pad(words, (0, SMALL_ROWS * SMALL_COLS - words.shape[0])).reshape(1, SMALL_ROWS, SMALL_COLS)
    small_slab = lax.dynamic_update_slice(jnp.zeros((N_CHIPS, SMALL_ROWS, SMALL_COLS), BF16), words, (me, 0, 0))

    first = _GROUPS[0][0]
    gathered = gather_slabs([wslabs[s].arr for s in first] + [small_slab])
    for s, arr in zip(first, gathered):
        wslabs[s].arr = arr
    in_flight, token = {}, gathered[-1]
    for slabs, point in _GROUPS[1:]:
        send, recv, thru, token = gather_start([wslabs[s].arr for s in slabs], token, f"weight_gather_start_{slabs[0]}")
        in_flight[point] = (send, recv, thru, slabs)
    started_token = token

    def before(i, stage, xs):
        if (i, stage) == (0, "mix"):
            return xs + started_token[0, 0]
        if (i, stage) in in_flight:
            send, recv, thru, slabs = in_flight[(i, stage)]
            landed = gather_wait(send, recv, thru, xs, f"weight_gather_wait_{slabs[0]}")
            for s, arr in zip(slabs, gather_forward(landed, f"weight_gather_forward_{slabs[0]}")):
                wslabs[s].arr = arr
        return xs

    small = {n: p[n] for n in _REPL}
    got, off = gathered[-1].reshape(N_CHIPS, -1), 0
    for n, ax in _SMALL:
        vals = lax.bitcast_convert_type(got[:, off:off + 2 * p[n].size].reshape(N_CHIPS, p[n].size, 2), F32)
        vals = vals.reshape((N_CHIPS,) + p[n].shape)
        small[n] = jnp.concatenate([vals[s] for s in range(N_CHIPS)], axis=ax)
        off += 2 * p[n].size

    gslabs = lay.new_slabs(BF16)
    complete_at = {point: slabs for slabs, point in _GROUPS[1:]}
    exchanging = []

    def after_bwd(i, stage, dx):
        if (i, stage) not in complete_at:
            return dx
        slabs = complete_at[(i, stage)]
        g = [gslabs[s].arr for s in slabs]
        swapped = pair_swap_halves(g, f"grad_pair_swap_{slabs[0]}")
        part = [pair_add(a, b, c1, f"pair_add_{s}") for a, b, s in zip(g, swapped, slabs)]
        send, recv, thru, token = exchange_start(part, c1, f"grad_exchange_start_{slabs[0]}")
        exchanging.append((slabs, send, recv, thru))
        return dx + token[0, 0]

    se, dx, sgrads = local_step(x[0], mem[0], positions.reshape(-1, 1), loss_target[0], lay, wslabs, gslabs, small,
                                before, after_bwd)
    loss = lax.psum(0.5 * jnp.sum(se) / x.shape[-1], ("x", "y", "c"))
    names, parts, received = [], [], []
    for slabs, send, recv, thru in exchanging:
        part, got = exchange_wait(send, recv, thru, dx, f"grad_exchange_wait_{slabs[0]}")
        names, parts, received = names + slabs, parts + list(part), received + list(got)

    axes = dict(_SMALL)
    small_order = small_names + _REPL
    slots = []
    for s in range(N_CHIPS):
        vals = {n: (lax.slice_in_dim(g, s * p[n].shape[axes[n]], (s + 1) * p[n].shape[axes[n]], axis=axes[n]) if n in axes else g)
                for n, g in sgrads.items()}
        slots.append(_small_pack(vals, small_order))
    g_last = [gslabs[s].arr for s in first] + [jnp.stack(slots).astype(BF16)]
    names_last = first + ["small"]
    swapped_last = pair_swap_halves(g_last, "grad_pair_swap_last")
    part_last = [pair_add(g, b, c1, f"pair_add_{s}") for g, b, s in zip(g_last, swapped_last, names_last)]
    names, parts, received = names + names_last, parts + part_last, received + list(chip_exchange(part_last))
    halves = [chip_sum(q, r, mc, f"chip_sum_{s}") for q, r, s in zip(parts, received, names)]
    reduced = dict(zip(names, pair_join_halves(halves)))

    res = {}
    for slab in _SLABS:
        for name, off, l0, l1, rpl in lay.members[slab]:
            res[name] = adamw(reduced[slab], off, flat2d(p[name]), flat2d(mom[name]), flat2d(var[name]), l0 * rpl, (l1 - l0) * rpl,
                              res.get(name), f"adamw_{slab}_{name}")
    for name in split:
        res[name] = [o.reshape(p[name].shape) for o in res[name]]
    sp = {k: _small_pack(d, small_order) for k, d in (("w", p), ("m", mom), ("v", var))}
    outs = adamw(reduced["small"], 0, sp["w"], sp["m"], sp["v"], 0, SMALL_ROWS, None, "adamw_small")
    unpacked = [_small_unpack(o, p, small_order) for o in outs]
    for n in small_order:
        res[n] = [u[n] for u in unpacked]
    return (loss, dx[None], *[res[n][k] for k in range(4) for n in _WEIGHTS])
```

```python
import jax
import jax.numpy as jnp
from jax import lax
from jax.experimental import pallas as pl
from jax.experimental.pallas import tpu as pltpu

F32 = jnp.float32
BF16 = jnp.bfloat16
HI = lax.Precision.HIGHEST
MESH = pl.DeviceIdType.MESH

EPS = 1e-6
ROPE_THETA = 10000.0
N_CHIPS = 4
LANES = 128
VMEM_LIMIT = 56 * 1024 * 1024
NEG = -1e30

MLA_H, MLA_NOPE, MLA_ROPE, MLA_V = 8, 128, 64, 128
MLA_QR, MLA_KVR = 384, 256
MLA_ZPAD = 768
GDN_H, GDN_D, GDN_C = 8, 128, 64
XA_H, XA_D = 4, 256

ADAM_LR, ADAM_B1, ADAM_B2, ADAM_EPS, ADAM_WD, ADAM_STEP = 0.001, 0.9, 0.999, 1e-08, 0.01, 10

SMALL_ROWS, SMALL_COLS = 32, 1024


def _cparams(sem=None):
    return pltpu.CompilerParams(dimension_semantics=sem, vmem_limit_bytes=VMEM_LIMIT)


def _pick(dim, pref):
    t = (min(pref, dim) // LANES) * LANES
    while t >= LANES:
        if dim % t == 0:
            return t
        t -= LANES
    return dim


def _pick_rows(rows, pref, *offsets):
    t = (min(pref, rows) // 16) * 16
    while t > 16 and (rows % t or any(o % t for o in offsets)):
        t -= 16
    return t


class Slab:
    def __init__(self, rows, width, dtype, arr=None):
        self.shape, self.dtype, self.arr = (N_CHIPS, rows, width), dtype, arr


class Loc:
    def __init__(self, slab, row0, K, N, axis):
        self.slab, self.row0, self.K, self.N, self.axis = slab, row0, K, N, axis
        self.Ks = K // N_CHIPS if axis == 0 else K
        self.Ns = N // N_CHIPS if axis == 1 else N

    def tile_spec(self, tr, tc, rc):
        assert self.row0 % tr == 0 and self.Ks % tr == 0 and self.Ns % tc == 0, (self.row0, self.Ks, self.Ns, tr, tc)
        r0, rb, cb = self.row0 // tr, self.Ks // tr, self.Ns // tc
        if self.axis == 0:
            return pl.BlockSpec((None, tr, tc), lambda i, j: (rc(i, j)[0] // rb, r0 + rc(i, j)[0] % rb, rc(i, j)[1]))
        return pl.BlockSpec((None, tr, tc), lambda i, j: (rc(i, j)[1] // cb, r0 + rc(i, j)[0], rc(i, j)[1] % cb))

    def slot_spec(self, slot, tr, tc, rc):
        assert self.row0 % tr == 0, (self.row0, tr)
        r0 = self.row0 // tr
        return pl.BlockSpec((None, tr, tc), lambda i, j: (slot, r0 + rc(i, j)[0], rc(i, j)[1]))


_DIMS = {"nn": ((1,), (0,)), "nt": ((1,), (1,)), "tn": ((0,), (0,))}
_ANY = pl.BlockSpec(memory_space=pl.ANY)


def mm(a, b, mode, name, outs=(F32,), epi=None, extras=(), tm=1024, tn=1024, out_loc=None, vecs=(), row_outs=0):
    full_rows = bool(vecs) or row_outs > 0
    b_loc = b if isinstance(b, Loc) else None
    if mode == "nn":
        M, K = a.shape
        K2, N = (b_loc.K, b_loc.N) if b_loc else b.shape
    elif mode == "nt":
        M, K = a.shape
        N, K2 = (b_loc.K, b_loc.N) if b_loc else b.shape
    else:
        K, M = a.shape
        K2, N = b.shape
    assert K == K2, (name, a.shape, K2, N)
    tm = _pick(out_loc.Ks if (out_loc and out_loc.axis == 0) else M, tm)
    n_split = full_rows and b_loc is not None and mode == "nt" and b_loc.axis == 0
    if out_loc is not None and out_loc.axis == 1:
        tn = _pick(out_loc.Ns, tn)
    elif n_split:
        tn = N
    elif b_loc is not None and ((mode == "nn" and b_loc.axis == 1) or (mode == "nt" and b_loc.axis == 0)):
        tn = _pick(b_loc.Ns if mode == "nn" else b_loc.Ks, tn)
    elif b_loc is not None:
        tn = N if full_rows else _pick(N, min(tn, 512))
    else:
        tn = N if full_rows else _pick(N, tn)
    assert tn == N or not full_rows, name

    parts = 1
    if mode == "tn":
        a_spec = pl.BlockSpec((K, tm), lambda i, j: (0, i))
        b_specs, b_args = [pl.BlockSpec((K, tn), lambda i, j: (0, j))], [b]
    else:
        a_spec = pl.BlockSpec((tm, K), lambda i, j: (i, 0))
        if b_loc is None:
            b_specs = [pl.BlockSpec((K, tn), lambda i, j: (0, j)) if mode == "nn" else pl.BlockSpec((tn, K), lambda i, j: (j, 0))]
            b_args = [b]
        elif mode == "nn" and b_loc.axis == 1:
            b_specs, b_args = [b_loc.tile_spec(K, tn, lambda i, j: (0, j))], [b_loc.slab.arr]
        elif n_split:
            b_specs = [b_loc.slot_spec(s, b_loc.Ks, K, lambda i, j: (0, 0)) for s in range(N_CHIPS)]
            b_args = [b_loc.slab.arr] * N_CHIPS
        elif mode == "nt" and b_loc.axis == 0:
            b_specs, b_args = [b_loc.tile_spec(tn, K, lambda i, j: (j, 0))], [b_loc.slab.arr]
        elif mode == "nn":
            parts = N_CHIPS
            b_specs = [b_loc.slot_spec(s, b_loc.Ks, tn, lambda i, j: (0, j)) for s in range(parts)]
            b_args = [b_loc.slab.arr] * parts
        else:
            parts = N_CHIPS
            b_specs = [b_loc.slot_spec(s, tn, b_loc.Ns, lambda i, j: (j, 0)) for s in range(parts)]
            b_args = [b_loc.slab.arr] * parts
    kp = K // parts
    n_b = N_CHIPS if n_split else parts
    n_ex, n_out = len(extras) + len(vecs), len(outs)
    dims = (_DIMS[mode], ((), ()))

    def body(*refs):
        a_ref = refs[0]
        b_refs = refs[1:1 + n_b]
        ex_refs = refs[1 + n_b:1 + n_b + n_ex]
        o_refs = refs[len(refs) - n_out - row_outs:len(refs) - row_outs]
        r_refs = refs[len(refs) - row_outs:]
        acc = None
        if n_split:
            av = a_ref[...].astype(BF16)
            acc = jnp.concatenate([lax.dot_general(av, b_ref[...].astype(BF16), dims, preferred_element_type=F32)
                                   for b_ref in b_refs], axis=1)
        for s in range(0 if n_split else parts):
            av = a_ref[...] if parts == 1 else a_ref[:, s * kp:(s + 1) * kp]
            d = lax.dot_general(av.astype(BF16), b_refs[s][...].astype(BF16), dims, preferred_element_type=F32)
            acc = d if acc is None else acc + d
        res = epi(acc, *[e[...] for e in ex_refs]) if epi is not None else (acc,)
        for o_ref, v in zip(o_refs, res[:n_out]):
            o_ref[...] = v.astype(o_ref.dtype)
        for r_ref, v in zip(r_refs, res[n_out:]):
            @pl.when(pl.program_id(0) == 0)
            def _():
                r_ref[...] = jnp.zeros_like(r_ref)

            r_ref[...] += v

    mn_spec = pl.BlockSpec((tm, tn), lambda i, j: (i, j))
    row_spec = pl.BlockSpec((1, tn), lambda i, j: (0, j))
    in_specs = [a_spec] + b_specs + [mn_spec] * len(extras) + [row_spec] * len(vecs)
    args = [a] + b_args + list(extras) + [v.reshape(1, N) for v in vecs]
    aliases = {}
    if out_loc is None:
        out_specs = [mn_spec] * n_out + [row_spec] * row_outs
        out_shape = [jax.ShapeDtypeStruct((M, N), d) for d in outs] + [jax.ShapeDtypeStruct((1, N), F32)] * row_outs
    else:
        assert n_out == 1 and mode == "tn"
        out_specs = [out_loc.tile_spec(tm, tn, lambda i, j: (i, j))]
        out_shape = [jax.ShapeDtypeStruct(out_loc.slab.shape, out_loc.slab.dtype)]
        if out_loc.slab.arr is not None:
            in_specs.append(_ANY)
            args.append(out_loc.slab.arr)
            aliases = {len(args) - 1: 0}

    res = pl.pallas_call(
        body, name=name, grid=(M // tm, N // tn), in_specs=in_specs, out_specs=out_specs, out_shape=out_shape,
        input_output_aliases=aliases, compiler_params=_cparams(("arbitrary" if row_outs else "parallel", "parallel")),
    )(*args)
    if out_loc is not None:
        out_loc.slab.arr = res[0]
        return None
    return res[0] if len(res) == 1 else tuple(res)


def _epi_add(acc, r):
    return (acc + r,)


def _epi_add_norm(acc, r, g):
    x = acc + r
    return x, _rms(x, g)


def _epi_norm_bwd(acc, x, dx_in, g):
    r = lax.rsqrt(jnp.mean(x * x, axis=-1, keepdims=True) + EPS)
    xh = x * r
    dxh = acc * g
    dx = dx_in + r * (dxh - xh * jnp.mean(dxh * xh, axis=-1, keepdims=True))
    return dx, jnp.sum(acc * xh, axis=0, keepdims=True)


def residual_norm(a, w, xs, g, name, tm=1024):
    if g is None:
        return mm(a, w, "nn", name, epi=_epi_add, extras=(xs,), tm=tm), None
    return mm(a, w, "nn", name, outs=(F32, BF16), epi=_epi_add_norm, extras=(xs,), vecs=(g,), tm=tm)


def _epi_relu2(acc):
    r = jnp.maximum(acc, 0.0)
    return acc, r * r


def _epi_relu2_bwd(acc, h1):
    return (acc * (2.0 * jnp.maximum(h1.astype(F32), 0.0)),)


def _rms(x, g):
    return x * lax.rsqrt(jnp.mean(x * x, axis=-1, keepdims=True) + EPS) * g


def _row_spec(ts, cols):
    return pl.BlockSpec((ts, cols), lambda i: (i, 0))


def _par_spec(cols):
    return pl.BlockSpec((1, cols), lambda i: (0, 0))


def rmsnorm_fwd(x, g, name, ts=256):
    T, D = x.shape
    ts = min(ts, T)

    def body(x_ref, g_ref, o_ref):
        o_ref[...] = _rms(x_ref[...], g_ref[...]).astype(o_ref.dtype)

    return pl.pallas_call(
        body, name=name, grid=(T // ts,),
        in_specs=[_row_spec(ts, D), _par_spec(D)], out_specs=_row_spec(ts, D),
        out_shape=jax.ShapeDtypeStruct((T, D), BF16), compiler_params=_cparams(("parallel",)),
    )(x, g.reshape(1, D))


def rmsnorm_bwd(x, g, dy, dx_in, name, ts=256):
    T, D = x.shape
    ts = min(ts, T)

    def body(x_ref, g_ref, dy_ref, dxi_ref, dx_ref, dg_ref):
        xv = x_ref[...]
        r = lax.rsqrt(jnp.mean(xv * xv, axis=-1, keepdims=True) + EPS)
        xh = xv * r
        dyv = dy_ref[...].astype(F32)
        dxh = dyv * g_ref[...]
        dx_ref[...] = dxi_ref[...] + r * (dxh - xh * jnp.mean(dxh * xh, axis=-1, keepdims=True))
        dg = jnp.sum(dyv * xh, axis=0, keepdims=True)

        @pl.when(pl.program_id(0) == 0)
        def _():
            dg_ref[...] = jnp.zeros_like(dg_ref)

        dg_ref[...] += dg

    dx, dg = pl.pallas_call(
        body, name=name, grid=(T // ts,),
        in_specs=[_row_spec(ts, D), _par_spec(D), _row_spec(ts, D), _row_spec(ts, D)],
        out_specs=[_row_spec(ts, D), _par_spec(D)],
        out_shape=[jax.ShapeDtypeStruct((T, D), F32), jax.ShapeDtypeStruct((1, D), F32)],
        compiler_params=_cparams(("arbitrary",)),
    )(x, g.reshape(1, D), dy, dx_in)
    return dx, dg.reshape(D)


def rope_tables(pos, name="rope_tables"):
    T = pos.shape[0]
    half = MLA_ROPE // 2
    inv = ROPE_THETA ** (-jnp.arange(0, MLA_ROPE, 2, dtype=F32) / MLA_ROPE)
    inv_row = jnp.concatenate([inv, inv, jnp.zeros((LANES - MLA_ROPE,), F32)]).reshape(1, LANES)

    def body(p_ref, f_ref, c_ref, a_ref, b_ref):
        ang = p_ref[...].astype(F32) * f_ref[...]
        lane = lax.broadcasted_iota(jnp.int32, ang.shape, 1)
        c, s = jnp.cos(ang), jnp.sin(ang)
        c_ref[...] = jnp.where(lane < MLA_ROPE, c, 0.0)
        a_ref[...] = jnp.where(lane < half, -s, 0.0)
        b_ref[...] = jnp.where((lane >= half) & (lane < MLA_ROPE), s, 0.0)

    sh = jax.ShapeDtypeStruct((T, LANES), F32)
    return pl.pallas_call(body, name=name, out_shape=[sh, sh, sh], compiler_params=_cparams())(pos, inv_row)


def _roll_l(x):
    return pltpu.roll(x, LANES - MLA_ROPE // 2, 1)


def _roll_r(x):
    return pltpu.roll(x, MLA_ROPE // 2, 1)


def _rope(r, c, sa, sb):
    return r * c + _roll_l(r) * sa + _roll_r(r) * sb


def _rope_t(d, c, sa, sb):
    return d * c + _roll_r(d * sa) + _roll_l(d * sb)


def mla_mid_fwd(z, qn, kvn, tabs, name, ts=256):
    T = z.shape[0]
    ts = min(ts, T)
    a0, a1 = MLA_QR, MLA_QR + MLA_KVR

    def body(z_ref, qn_ref, kvn_ref, c_ref, sa_ref, sb_ref, cq_ref, ckv_ref, kr_ref):
        cq_ref[...] = _rms(z_ref[:, 0:a0], qn_ref[...]).astype(BF16)
        ckv_ref[...] = _rms(z_ref[:, a0:a1], kvn_ref[...]).astype(BF16)
        kr_ref[...] = _rope(z_ref[:, a1:MLA_ZPAD], c_ref[...], sa_ref[...], sb_ref[...]).astype(BF16)

    return pl.pallas_call(
        body, name=name, grid=(T // ts,),
        in_specs=[_row_spec(ts, MLA_ZPAD), _par_spec(MLA_QR), _par_spec(MLA_KVR)] + [_row_spec(ts, LANES)] * 3,
        out_specs=[_row_spec(ts, MLA_QR), _row_spec(ts, MLA_KVR), _row_spec(ts, LANES)],
        out_shape=[jax.ShapeDtypeStruct((T, MLA_QR), BF16), jax.ShapeDtypeStruct((T, MLA_KVR), BF16),
                   jax.ShapeDtypeStruct((T, LANES), BF16)],
        compiler_params=_cparams(("parallel",)),
    )(z, qn.reshape(1, -1), kvn.reshape(1, -1), *tabs)


def mla_mid_bwd(z, qn, kvn, tabs, dcq, dckv, dkr, name, ts=256):
    T = z.shape[0]
    ts = min(ts, T)
    a0, a1 = MLA_QR, MLA_QR + MLA_KVR

    def body(z_ref, qn_ref, kvn_ref, c_ref, sa_ref, sb_ref, dcq_ref, dckv_ref, dkr_ref, dz_ref, dqn_ref, dkvn_ref):
        _, vq = jax.vjp(_rms, z_ref[:, 0:a0], qn_ref[...])
        dzq, dqn = vq(dcq_ref[...].astype(F32))
        _, vk = jax.vjp(_rms, z_ref[:, a0:a1], kvn_ref[...])
        dzk, dkvn = vk(dckv_ref[...].astype(F32))
        dz_ref[:, 0:a0] = dzq.astype(dz_ref.dtype)
        dz_ref[:, a0:a1] = dzk.astype(dz_ref.dtype)
        dz_ref[:, a1:MLA_ZPAD] = _rope_t(dkr_ref[...].astype(F32), c_ref[...], sa_ref[...], sb_ref[...]).astype(dz_ref.dtype)

        @pl.when(pl.program_id(0) == 0)
        def _():
            dqn_ref[...] = jnp.zeros_like(dqn_ref)
            dkvn_ref[...] = jnp.zeros_like(dkvn_ref)

        dqn_ref[...] += dqn
        dkvn_ref[...] += dkvn

    dz, dqn, dkvn = pl.pallas_call(
        body, name=name, grid=(T // ts,),
        in_specs=[_row_spec(ts, MLA_ZPAD), _par_spec(MLA_QR), _par_spec(MLA_KVR)] + [_row_spec(ts, LANES)] * 3
        + [_row_spec(ts, MLA_QR), _row_spec(ts, MLA_KVR), _row_spec(ts, LANES)],
        out_specs=[_row_spec(ts, MLA_ZPAD), _par_spec(MLA_QR), _par_spec(MLA_KVR)],
        out_shape=[jax.ShapeDtypeStruct((T, MLA_ZPAD), BF16), jax.ShapeDtypeStruct((1, MLA_QR), F32),
                   jax.ShapeDtypeStruct((1, MLA_KVR), F32)],
        compiler_params=_cparams(("arbitrary",)),
    )(z, qn.reshape(1, -1), kvn.reshape(1, -1), *tabs, dcq, dckv, dkr)
    return dz, dqn.reshape(-1), dkvn.reshape(-1)


def rope_q(q, tabs, transpose, name, ts=256):
    T, W = q.shape
    ts = min(ts, T)
    fn = _rope_t if transpose else _rope
    hw = 2 * LANES

    def body(q_ref, c_ref, sa_ref, sb_ref, o_ref):
        c, sa, sb = c_ref[...], sa_ref[...], sb_ref[...]
        for h in range(W // hw):
            o_ref[:, h * hw:h * hw + LANES] = q_ref[:, h * hw:h * hw + LANES].astype(o_ref.dtype)
            o_ref[:, h * hw + LANES:(h + 1) * hw] = fn(q_ref[:, h * hw + LANES:(h + 1) * hw].astype(F32), c, sa, sb).astype(o_ref.dtype)

    return pl.pallas_call(
        body, name=name, grid=(T // ts,),
        in_specs=[_row_spec(ts, W)] + [_row_spec(ts, LANES)] * 3, out_specs=_row_spec(ts, W),
        out_shape=jax.ShapeDtypeStruct((T, W), BF16), compiler_params=_cparams(("parallel",)),
    )(q, *tabs)


def loss_head(x, g, target, name="loss_head", ts=256):
    T, D = x.shape
    ts = min(ts, T)

    def body(x_ref, g_ref, t_ref, se_ref, dx_ref, dg_ref):
        xv = x_ref[...]
        r = lax.rsqrt(jnp.mean(xv * xv, axis=-1, keepdims=True) + EPS)
        xh = xv * r
        err = xh * g_ref[...] - t_ref[...]
        dy = err * (1.0 / D)
        dxh = dy * g_ref[...]
        dx_ref[...] = r * (dxh - xh * jnp.mean(dxh * xh, axis=-1, keepdims=True))

        @pl.when(pl.program_id(0) == 0)
        def _():
            se_ref[...] = jnp.zeros_like(se_ref)
            dg_ref[...] = jnp.zeros_like(dg_ref)

        se_ref[...] += jnp.sum(err * err, axis=0, keepdims=True)
        dg_ref[...] += jnp.sum(dy * xh, axis=0, keepdims=True)

    se, dx, dg = pl.pallas_call(
        body, name=name, grid=(T // ts,),
        in_specs=[_row_spec(ts, D), _par_spec(D), _row_spec(ts, D)],
        out_specs=[_par_spec(D), _row_spec(ts, D), _par_spec(D)],
        out_shape=[jax.ShapeDtypeStruct((1, D), F32), jax.ShapeDtypeStruct((T, D), F32), jax.ShapeDtypeStruct((1, D), F32)],
        compiler_params=_cparams(("arbitrary",)),
    )(x, g.reshape(1, D), target)
    return se, dx, dg.reshape(D)


def _dot_nt(a, b):
    return lax.dot_general(a, b, (((1,), (1,)), ((), ())), preferred_element_type=F32)


def _dot_tn(a, b):
    return lax.dot_general(a, b, (((0,), (0,)), ((), ())), preferred_element_type=F32)


def _dot_nn(a, b):
    return lax.dot_general(a, b, (((1,), (0,)), ((), ())), preferred_element_type=F32)


class _Attn:
    def __init__(self, H, dq, dk1, dv, causal, scale, hp, hp_kv, blk=256):
        self.H, self.dq, self.dk1, self.dv, self.causal, self.scale, self.blk = H, dq, dk1, dv, causal, scale, blk
        self.hp, self.hp_kv = hp, hp_kv


def _cols(ref, rows, hh, width):
    return ref[rows, hh * width:(hh + 1) * width]


def _keys(cfg, k1_ref, k2_ref, rows, hh):
    ks = _cols(k1_ref, rows, hh, cfg.dk1)
    if k2_ref is not None:
        ks = jnp.concatenate([ks, k2_ref[rows, :]], axis=1)
    return ks


def _attn_specs(cfg, hp, t, Tk, has_k2, by_q):
    g = cfg.H // hp
    if by_q:
        specs = [pl.BlockSpec((t, hp * cfg.dq), lambda h, i: (i, h)),
                 pl.BlockSpec((Tk, hp * cfg.dk1), lambda h, i: (0, h)),
                 pl.BlockSpec((Tk, hp * cfg.dv), lambda h, i: (0, g + h))]
        if has_k2:
            specs.append(pl.BlockSpec((Tk, LANES), lambda h, i: (0, 0)))
    else:
        specs = [None,
                 pl.BlockSpec((t, hp * cfg.dk1), lambda j, h: (j, h)),
                 pl.BlockSpec((t, hp * cfg.dv), lambda j, h: (j, g + h))]
        if has_k2:
            specs.append(pl.BlockSpec((t, LANES), lambda j, h: (j, 0)))
    return specs


def _mask(s, diagonal):
    if not diagonal:
        return s
    return jnp.where(lax.broadcasted_iota(jnp.int32, s.shape, 0) >= lax.broadcasted_iota(jnp.int32, s.shape, 1), s, NEG)


def flash_fwd(cfg, q, k1, v, k2, name):
    Tq, Tk = q.shape[0], k1.shape[0]
    t = min(cfg.blk, Tq, Tk)
    nkb = Tk // t
    has_k2 = k2 is not None
    hp = cfg.hp

    def body(*refs):
        q_ref, k1_ref, v_ref = refs[:3]
        k2_ref = refs[3] if has_k2 else None
        o_ref, lse_ref = refs[-2], refs[-1]
        i = pl.program_id(1)
        qs = [_cols(q_ref, slice(None), hh, cfg.dq) for hh in range(hp)]

        def step(j, carry, diagonal=False):
            rows = pl.ds(pl.multiple_of(j * t, t), t)
            out = []
            for hh in range(hp):
                m, l, acc = carry[hh]
                s = _mask(_dot_nt(qs[hh], _keys(cfg, k1_ref, k2_ref, rows, hh)) * cfg.scale, diagonal)
                m2 = jnp.maximum(m, jnp.max(s, axis=-1, keepdims=True))
                p = jnp.exp(s - m2)
                alpha = jnp.exp(m - m2)
                l2 = alpha * l + jnp.sum(p, axis=-1, keepdims=True)
                acc2 = alpha * acc + _dot_nn(p.astype(BF16), _cols(v_ref, rows, hh, cfg.dv))
                out.append((m2, l2, acc2))
            return tuple(out)

        init = tuple((jnp.full((t, 1), NEG, F32), jnp.zeros((t, 1), F32), jnp.zeros((t, cfg.dv), F32)) for _ in range(hp))
        res = lax.fori_loop(0, i if cfg.causal else nkb, step, init)
        if cfg.causal:
            res = step(i, res, True)
        for hh in range(hp):
            m, l, acc = res[hh]
            o_ref[:, hh * cfg.dv:(hh + 1) * cfg.dv] = (acc / l).astype(o_ref.dtype)
            lse_ref[hh] = m + jnp.log(l)

    args = [q, k1, v] + ([k2] if has_k2 else [])
    return pl.pallas_call(
        body, name=name, grid=(cfg.H // hp, Tq // t), in_specs=_attn_specs(cfg, hp, t, Tk, has_k2, True),
        out_specs=[pl.BlockSpec((t, hp * cfg.dv), lambda h, i: (i, h)), pl.BlockSpec((hp, t, 1), lambda h, i: (h, i, 0))],
        out_shape=[jax.ShapeDtypeStruct((Tq, cfg.H * cfg.dv), BF16), jax.ShapeDtypeStruct((cfg.H, Tq, 1), F32)],
        compiler_params=_cparams(("parallel", "parallel")),
    )(*args)


def flash_dq(cfg, q, k1, v, k2, o, do, lse, out_dtype, name):
    Tq, Tk = q.shape[0], k1.shape[0]
    t = min(cfg.blk, Tq, Tk)
    nkb = Tk // t
    has_k2 = k2 is not None
    hp = cfg.hp

    def body(*refs):
        q_ref, k1_ref, v_ref = refs[:3]
        k2_ref = refs[3] if has_k2 else None
        o_ref, do_ref, lse_ref, dq_ref, dl_ref = refs[-5:]
        i = pl.program_id(1)
        qs = [_cols(q_ref, slice(None), hh, cfg.dq) for hh in range(hp)]
        dos = [_cols(do_ref, slice(None), hh, cfg.dv) for hh in range(hp)]
        lses = [lse_ref[hh] for hh in range(hp)]
        deltas = []
        for hh in range(hp):
            d = jnp.sum(dos[hh].astype(F32) * _cols(o_ref, slice(None), hh, cfg.dv).astype(F32), axis=-1, keepdims=True)
            dl_ref[hh] = d
            deltas.append(d)

        def step(j, dqs, diagonal=False):
            rows = pl.ds(pl.multiple_of(j * t, t), t)
            out = []
            for hh in range(hp):
                ks = _keys(cfg, k1_ref, k2_ref, rows, hh)
                s = _mask(_dot_nt(qs[hh], ks) * cfg.scale, diagonal)
                p = jnp.exp(s - lses[hh])
                dp = _dot_nt(dos[hh], _cols(v_ref, rows, hh, cfg.dv))
                ds = p * (dp - deltas[hh]) * cfg.scale
                out.append(dqs[hh] + _dot_nn(ds.astype(BF16), ks))
            return tuple(out)

        dqs = lax.fori_loop(0, i if cfg.causal else nkb, step, tuple(jnp.zeros((t, cfg.dq), F32) for _ in range(hp)))
        if cfg.causal:
            dqs = step(i, dqs, True)
        for hh in range(hp):
            dq_ref[:, hh * cfg.dq:(hh + 1) * cfg.dq] = dqs[hh].astype(dq_ref.dtype)

    ov = pl.BlockSpec((t, hp * cfg.dv), lambda h, i: (i, h))
    row1 = pl.BlockSpec((hp, t, 1), lambda h, i: (h, i, 0))
    args = [q, k1, v] + ([k2] if has_k2 else []) + [o, do, lse]
    return pl.pallas_call(
        body, name=name, grid=(cfg.H // hp, Tq // t), in_specs=_attn_specs(cfg, hp, t, Tk, has_k2, True) + [ov, ov, row1],
        out_specs=[pl.BlockSpec((t, hp * cfg.dq), lambda h, i: (i, h)), row1],
        out_shape=[jax.ShapeDtypeStruct((Tq, cfg.H * cfg.dq), out_dtype), jax.ShapeDtypeStruct((cfg.H, Tq, 1), F32)],
        compiler_params=_cparams(("parallel", "parallel")),
    )(*args)


def flash_dkv(cfg, q, k1, v, k2, do, lse, delta, out_dtype, name):
    Tq, Tk = q.shape[0], k1.shape[0]
    t = min(cfg.blk, Tq, Tk)
    nqb = Tq // t
    has_k2 = k2 is not None
    hp = cfg.hp_kv

    def body(*refs):
        q_ref, k1_ref, v_ref = refs[:3]
        k2_ref = refs[3] if has_k2 else None
        n_in = 4 if has_k2 else 3
        do_ref, lse_ref, dl_ref = refs[n_in:n_in + 3]
        dk1_ref, dv_ref = refs[n_in + 3], refs[n_in + 4]
        j, h = pl.program_id(0), pl.program_id(1)
        kss = [_keys(cfg, k1_ref, k2_ref, slice(None), hh) for hh in range(hp)]
        vss = [_cols(v_ref, slice(None), hh, cfg.dv) for hh in range(hp)]

        def step(i, carry, diagonal=False):
            rows = pl.ds(pl.multiple_of(i * t, t), t)
            out = []
            for hh in range(hp):
                dk, dv = carry[hh]
                qi, doi = _cols(q_ref, rows, hh, cfg.dq), _cols(do_ref, rows, hh, cfg.dv)
                s = _mask(_dot_nt(qi, kss[hh]) * cfg.scale, diagonal)
                p = jnp.exp(s - lse_ref[hh, rows, :])
                dv = dv + _dot_tn(p.astype(BF16), doi)
                ds = p * (_dot_nt(doi, vss[hh]) - dl_ref[hh, rows, :]) * cfg.scale
                dk = dk + _dot_tn(ds.astype(BF16), qi)
                out.append((dk, dv))
            return tuple(out)

        init = tuple((jnp.zeros((t, cfg.dq), F32), jnp.zeros((t, cfg.dv), F32)) for _ in range(hp))
        if cfg.causal:
            res = lax.fori_loop(j + 1, nqb, step, step(j, init, True))
        else:
            res = lax.fori_loop(0, nqb, step, init)
        for hh in range(hp):
            dk, dv = res[hh]
            dv_ref[:, hh * cfg.dv:(hh + 1) * cfg.dv] = dv.astype(dv_ref.dtype)
            dk1_ref[:, hh * cfg.dk1:(hh + 1) * cfg.dk1] = dk[:, 0:cfg.dk1].astype(dk1_ref.dtype)
        if has_k2:
            dk2_ref = refs[n_in + 5]

            @pl.when(h == 0)
            def _():
                dk2_ref[...] = jnp.zeros_like(dk2_ref)

            for hh in range(hp):
                dk2_ref[...] += res[hh][0][:, cfg.dk1:]

    specs = _attn_specs(cfg, hp, t, Tk, has_k2, False)
    specs[0] = pl.BlockSpec((Tq, hp * cfg.dq), lambda j, h: (0, h))
    rows_all = pl.BlockSpec((hp, Tq, 1), lambda j, h: (h, 0, 0))
    specs += [pl.BlockSpec((Tq, hp * cfg.dv), lambda j, h: (0, h)), rows_all, rows_all]
    args = [q, k1, v] + ([k2] if has_k2 else []) + [do, lse, delta]
    out_specs = [pl.BlockSpec((t, hp * cfg.dk1), lambda j, h: (j, h)), pl.BlockSpec((t, hp * cfg.dv), lambda j, h: (j, h))]
    out_shape = [jax.ShapeDtypeStruct((Tk, cfg.H * cfg.dk1), out_dtype), jax.ShapeDtypeStruct((Tk, cfg.H * cfg.dv), out_dtype)]
    if has_k2:
        out_specs.append(pl.BlockSpec((t, LANES), lambda j, h: (j, 0)))
        out_shape.append(jax.ShapeDtypeStruct((Tk, LANES), F32))
    return pl.pallas_call(
        body, name=name, grid=(Tk // t, cfg.H // hp), in_specs=specs, out_specs=out_specs, out_shape=out_shape,
        compiler_params=_cparams(("parallel", "arbitrary")),
    )(*args)


def _shift_down(x, s):
    if s == 0:
        return x
    t = lax.broadcasted_iota(jnp.int32, x.shape, 0)
    return jnp.where(t >= s, pltpu.roll(x, s, 0), 0.0)


def _shift_up(x, s):
    if s == 0:
        return x
    n = x.shape[0]
    t = lax.broadcasted_iota(jnp.int32, x.shape, 0)
    return jnp.where(t < n - s, pltpu.roll(x, n - s, 0), 0.0)


def _conv(x, w_ref, kw):
    y = x * w_ref[kw - 1:kw, :]
    for j in range(kw - 1):
        y = y + _shift_down(x, kw - 1 - j) * w_ref[j:j + 1, :]
    return y


def _conv_t(d, w_ref, kw):
    y = d * w_ref[kw - 1:kw, :]
    for j in range(kw - 1):
        y = y + _shift_up(d, kw - 1 - j) * w_ref[j:j + 1, :]
    return y


def _conv_dw(d, x, kw):
    rows = lax.broadcasted_iota(jnp.int32, (kw, d.shape[1]), 0)
    dw = jnp.zeros((kw, d.shape[1]), F32)
    for j in range(kw):
        r = jnp.sum(d * _shift_down(x, kw - 1 - j), axis=0, keepdims=True)
        dw = jnp.where(rows == j, r, dw)
    return dw


def _silu(x):
    return x * jax.nn.sigmoid(x)


def _silu_grad(x):
    s = jax.nn.sigmoid(x)
    return s * (1.0 + x * (1.0 - s))


def gdn_conv_fwd(z, w, name, tc=256):
    T, C = z.shape[0], w.shape[1]
    kw = w.shape[0]

    def body(x_ref, w_ref, o_ref):
        o_ref[...] = _silu(_conv(x_ref[...], w_ref, kw))

    return pl.pallas_call(
        body, name=name, grid=(C // tc,),
        in_specs=[pl.BlockSpec((T, tc), lambda j: (0, j)), pl.BlockSpec((kw, tc), lambda j: (0, j))],
        out_specs=pl.BlockSpec((T, tc), lambda j: (0, j)),
        out_shape=jax.ShapeDtypeStruct((T, C), F32), compiler_params=_cparams(("parallel",)),
    )(z, w)


def gdn_conv_bwd(z, w, dy, name, tc=256):
    T, C = z.shape[0], w.shape[1]
    kw = w.shape[0]

    def body(x_ref, w_ref, dy_ref, dx_ref, dw_ref):
        xv = x_ref[...]
        dc = dy_ref[...] * _silu_grad(_conv(xv, w_ref, kw))
        dx_ref[...] = _conv_t(dc, w_ref, kw).astype(dx_ref.dtype)
        dw_ref[...] = _conv_dw(dc, xv, kw)

    col = lambda j: (0, j)
    return pl.pallas_call(
        body, name=name, grid=(C // tc,),
        in_specs=[pl.BlockSpec((T, tc), col), pl.BlockSpec((kw, tc), col), pl.BlockSpec((T, tc), col)],
        out_specs=[pl.BlockSpec((T, tc), col), pl.BlockSpec((kw, tc), col)],
        out_shape=[jax.ShapeDtypeStruct((T, C), BF16), jax.ShapeDtypeStruct((kw, C), F32)],
        compiler_params=_cparams(("parallel",)),
    )(z, w, dy)


def sc_fwd(z, w, name, tc=256):
    T, C = z.shape[0], w.shape[1]
    kw, nb = w.shape[0], C // tc

    def body(b_ref, c_ref, u_ref, w_ref, o_ref):
        o_ref[...] = (b_ref[...] * _conv(c_ref[...] * u_ref[...], w_ref, kw)).astype(o_ref.dtype)

    return pl.pallas_call(
        body, name=name, grid=(nb,),
        in_specs=[pl.BlockSpec((T, tc), lambda j: (0, j)), pl.BlockSpec((T, tc), lambda j: (0, nb + j)),
                  pl.BlockSpec((T, tc), lambda j: (0, 2 * nb + j)), pl.BlockSpec((kw, tc), lambda j: (0, j))],
        out_specs=pl.BlockSpec((T, tc), lambda j: (0, j)),
        out_shape=jax.ShapeDtypeStruct((T, C), BF16), compiler_params=_cparams(("parallel",)),
    )(z, z, z, w)


def sc_bwd(z, w, dy, name, tc=256):
    T, C = z.shape[0], w.shape[1]
    kw, nb = w.shape[0], C // tc

    def body(b_ref, c_ref, u_ref, w_ref, dy_ref, db_ref, dc_ref, du_ref, dw_ref):
        cv, uv, dyv = c_ref[...], u_ref[...], dy_ref[...]
        cu = cv * uv
        db_ref[...] = (dyv * _conv(cu, w_ref, kw)).astype(db_ref.dtype)
        dcv = dyv * b_ref[...]
        dcu = _conv_t(dcv, w_ref, kw)
        dc_ref[...] = (dcu * uv).astype(dc_ref.dtype)
        du_ref[...] = (dcu * cv).astype(du_ref.dtype)
        dw_ref[...] = _conv_dw(dcv, cu, kw)

    col = lambda j: (0, j)
    act = jax.ShapeDtypeStruct((T, C), BF16)
    return pl.pallas_call(
        body, name=name, grid=(nb,),
        in_specs=[pl.BlockSpec((T, tc), col), pl.BlockSpec((T, tc), lambda j: (0, nb + j)),
                  pl.BlockSpec((T, tc), lambda j: (0, 2 * nb + j)), pl.BlockSpec((kw, tc), col), pl.BlockSpec((T, tc), col)],
        out_specs=[pl.BlockSpec((T, tc), col)] * 3 + [pl.BlockSpec((kw, tc), col)],
        out_shape=[act, act, act, jax.ShapeDtypeStruct((kw, C), F32)],
        compiler_params=_cparams(("parallel",)),
    )(z, z, z, w, dy)


def _hdot(a, b, dims):
    return lax.dot_general(a, b, (dims, ((), ())), precision=HI, preferred_element_type=F32)


def _bdot(a, b, dims):
    return lax.dot_general(a.astype(BF16), b.astype(BF16), (dims, ((), ())), preferred_element_type=F32)


_NN, _NT, _TN = ((1,), (0,)), ((1,), (1,)), ((0,), (0,))


def _per_head_dots(dot2d):
    def stacked(a, b, dims):
        return jnp.stack([dot2d(a[h], b[h], dims) for h in range(a.shape[0])])

    @jax.custom_vjp
    def nn(a, b):
        return stacked(a, b, _NN)

    @jax.custom_vjp
    def nt(a, b):
        return stacked(a, b, _NT)

    @jax.custom_vjp
    def tn(a, b):
        return stacked(a, b, _TN)

    nn.defvjp(lambda a, b: (nn(a, b), (a, b)), lambda r, d: (stacked(d, r[1], _NT), stacked(r[0], d, _TN)))
    nt.defvjp(lambda a, b: (nt(a, b), (a, b)), lambda r, d: (stacked(d, r[1], _NN), stacked(d, r[0], _TN)))
    tn.defvjp(lambda a, b: (tn(a, b), (a, b)), lambda r, d: (stacked(r[1], d, _NT), stacked(r[0], d, _NN)))
    return nn, nt, tn


_hnn, _hnt, _htn = _per_head_dots(_hdot)
_bnn, _bnt, _btn = _per_head_dots(_bdot)


@jax.custom_vjp
def _unit_lower_inverse(m):
    c = m.shape[-1]
    eye = (lax.broadcasted_iota(jnp.int32, (c, c), 0) == lax.broadcasted_iota(jnp.int32, (c, c), 1)).astype(F32)
    t = eye - m
    p = _hnn(m, m)
    n = 2
    while n < c:
        t = t + _hnn(t, p)
        n *= 2
        if n < c:
            p = _hnn(p, p)
    return t


def _uli_fwd(m):
    t = _unit_lower_inverse(m)
    return t, t


def _uli_bwd(t, dt):
    return (-_htn(t, _hnt(dt, t)),)


_unit_lower_inverse.defvjp(_uli_fwd, _uli_bwd)


@jax.custom_vjp
def _known_inverse(m, t):
    return t


_known_inverse.defvjp(lambda m, t: (t, t), lambda t, dt: (_uli_bwd(t, dt)[0], jnp.zeros_like(t)))


def _gdn_chunk(q, k, v, gate, bl, al, a_log, dt_bias, o_norm, st, t_known=None):
    nh, c = q.shape[0], q.shape[1]
    ii = lax.broadcasted_iota(jnp.int32, (c, c), 0)
    jj = lax.broadcasted_iota(jnp.int32, (c, c), 1)
    tri, strict = ii >= jj, ii > jj
    q = q * lax.rsqrt(jnp.sum(q * q, -1, keepdims=True) + EPS) * (GDN_D ** -0.5)
    k = k * lax.rsqrt(jnp.sum(k * k, -1, keepdims=True) + EPS)
    beta = jax.nn.sigmoid(bl)
    g = -jnp.exp(a_log) * jax.nn.softplus(al + dt_bias)
    gc = _hnn(jnp.broadcast_to(tri.astype(F32), (nh, c, c)), g)
    gcol = _hnn(gc, jnp.full((nh, LANES, c), 1.0 / LANES, F32))
    grow = _hnt(jnp.full((nh, c, LANES), 1.0 / LANES, F32), gc)
    decay = jnp.where(tri, jnp.exp(jnp.where(tri, gcol - grow, 0.0)), 0.0)
    kb = k * beta
    m = jnp.where(strict, _bnt(kb, k) * decay, 0.0)
    t_inv = _unit_lower_inverse(m) if t_known is None else _known_inverse(m, t_known)
    eg = jnp.exp(gc)
    u = _bnn(t_inv, v * beta)
    w = _bnn(t_inv, kb * eg)
    attn = _bnt(q, k) * decay
    v_new = u - _bnn(w, st)
    o = _bnn(q * eg, st) + _bnn(attn, v_new)
    g_last = jnp.sum(g, axis=1, keepdims=True)
    st_new = st * jnp.exp(g_last) + _btn(k * jnp.exp(g_last - gc), v_new)
    o = o * lax.rsqrt(jnp.mean(o * o, -1, keepdims=True) + EPS) * o_norm
    return o * _silu(gate), st_new, t_inv


GDN_HP = 8
_GW = GDN_HP * GDN_D
_GB = GDN_H // GDN_HP


def _gdn_specs(n_chunks, rev):
    def tok(col):
        if rev:
            return pl.BlockSpec((GDN_C, _GW), lambda h, n: (n_chunks - 1 - n, col + h))
        return pl.BlockSpec((GDN_C, _GW), lambda h, n: (n, col + h))
    par = pl.BlockSpec((1, _GW), lambda h, n: (0, h))
    shared = pl.BlockSpec((1, GDN_D), lambda h, n: (0, 0))
    if rev:
        st = pl.BlockSpec((GDN_HP, None, GDN_D, GDN_D), lambda h, n: (h, n_chunks - 1 - n, 0, 0))
    else:
        st = pl.BlockSpec((GDN_HP, None, GDN_D, GDN_D), lambda h, n: (h, n, 0, 0))
    return tok, par, shared, st


def _heads(ref):
    return jnp.stack([ref[:, h * GDN_D:(h + 1) * GDN_D] for h in range(ref.shape[1] // GDN_D)])


def gdn_chunk_fwd(qkv, z, a_log_x, dt_bias_x, o_norm, name):
    T = qkv.shape[0]
    n_chunks = T // GDN_C
    H = GDN_H
    tok, par, shared, st_spec = _gdn_specs(n_chunks, False)

    def body(q_ref, k_ref, v_ref, g_ref, bl_ref, al_ref, a_ref, dt_ref, on_ref, o_ref, st_ref, ti_ref, state):
        @pl.when(pl.program_id(1) == 0)
        def _():
            state[...] = jnp.zeros_like(state)

        st = state[...]
        st_ref[...] = st
        o, st_new, t_inv = _gdn_chunk(_heads(q_ref), _heads(k_ref), _heads(v_ref), _heads(g_ref), _heads(bl_ref), _heads(al_ref),
                                      _heads(a_ref), _heads(dt_ref), on_ref[...], st)
        for hh in range(GDN_HP):
            o_ref[:, hh * GDN_D:(hh + 1) * GDN_D] = o[hh].astype(o_ref.dtype)
        ti_ref[...] = t_inv
        state[...] = st_new

    B = _GB
    return pl.pallas_call(
        body, name=name, grid=(B, n_chunks),
        in_specs=[tok(0), tok(B), tok(2 * B), tok(3 * B), tok(4 * B), tok(5 * B), par, par, shared],
        out_specs=[tok(0), st_spec, pl.BlockSpec((GDN_HP, None, GDN_C, GDN_C), lambda h, n: (h, n, 0, 0))],
        out_shape=[jax.ShapeDtypeStruct((T, H * GDN_D), BF16), jax.ShapeDtypeStruct((H, n_chunks, GDN_D, GDN_D), F32),
                   jax.ShapeDtypeStruct((H, n_chunks, GDN_C, GDN_C), F32)],
        scratch_shapes=[pltpu.VMEM((GDN_HP, GDN_D, GDN_D), F32)],
        compiler_params=_cparams(("parallel", "arbitrary")),
    )(qkv, qkv, qkv, z, z, z, a_log_x, dt_bias_x, o_norm)


def gdn_chunk_bwd(qkv, z, a_log_x, dt_bias_x, o_norm, states, t_invs, do, name):
    T = qkv.shape[0]
    n_chunks = T // GDN_C
    H = GDN_H
    tok, par, shared, st_spec = _gdn_specs(n_chunks, True)

    def body(q_ref, k_ref, v_ref, g_ref, bl_ref, al_ref, a_ref, dt_ref, on_ref, st_ref, ti_ref, do_ref,
             dq_ref, dk_ref, dv_ref, dg_ref, dbl_ref, dal_ref, da_ref, ddt_ref, don_ref, dstate):
        h, n = pl.program_id(0), pl.program_id(1)

        @pl.when(n == 0)
        def _():
            dstate[...] = jnp.zeros_like(dstate)
            da_ref[...] = jnp.zeros_like(da_ref)
            ddt_ref[...] = jnp.zeros_like(ddt_ref)

        @pl.when((n == 0) & (h == 0))
        def _():
            don_ref[...] = jnp.zeros_like(don_ref)

        t_known = ti_ref[...]
        _, vjp = jax.vjp(lambda *ins: _gdn_chunk(*ins, t_known=t_known)[:2],
                         _heads(q_ref), _heads(k_ref), _heads(v_ref), _heads(g_ref), _heads(bl_ref), _heads(al_ref),
                         _heads(a_ref), _heads(dt_ref), on_ref[...], st_ref[...])
        dq, dk, dv, dg, dbl, dal, da, ddt, don, dst = vjp((_heads(do_ref).astype(F32), dstate[...]))
        for hh in range(GDN_HP):
            cols = slice(hh * GDN_D, (hh + 1) * GDN_D)
            dq_ref[:, cols] = dq[hh]
            dk_ref[:, cols] = dk[hh]
            dv_ref[:, cols] = dv[hh]
            dg_ref[:, cols] = dg[hh].astype(dg_ref.dtype)
            dbl_ref[:, cols] = dbl[hh].astype(dbl_ref.dtype)
            dal_ref[:, cols] = dal[hh].astype(dal_ref.dtype)
            da_ref[:, cols] += da[hh]
            ddt_ref[:, cols] += ddt[hh]
        don_ref[...] += don
        dstate[...] = dst

    tok0 = tok(0)
    B = _GB
    f32_tok = jax.ShapeDtypeStruct((T, H * GDN_D), F32)
    bf_tok = jax.ShapeDtypeStruct((T, H * GDN_D), BF16)
    par_sh = jax.ShapeDtypeStruct((1, H * GDN_D), F32)
    return pl.pallas_call(
        body, name=name, grid=(B, n_chunks),
        in_specs=[tok(0), tok(B), tok(2 * B), tok(3 * B), tok(4 * B), tok(5 * B), par, par, shared, st_spec,
                  pl.BlockSpec((GDN_HP, None, GDN_C, GDN_C), lambda h, n: (h, n_chunks - 1 - n, 0, 0)), tok0],
        out_specs=[tok0] * 6 + [par, par, shared],
        out_shape=[f32_tok, f32_tok, f32_tok, bf_tok, bf_tok, bf_tok, par_sh, par_sh, jax.ShapeDtypeStruct((1, GDN_D), F32)],
        scratch_shapes=[pltpu.VMEM((GDN_HP, GDN_D, GDN_D), F32)],
        compiler_params=_cparams(("arbitrary", "arbitrary")),
    )(qkv, qkv, qkv, z, z, z, a_log_x, dt_bias_x, o_norm, states, t_invs, do)


def _prefetch_call(body, name, grid, in_specs, out_specs, out_shape, aliases=None):
    return pl.pallas_call(
        body, name=name,
        grid_spec=pltpu.PrefetchScalarGridSpec(num_scalar_prefetch=1, grid=grid, in_specs=in_specs, out_specs=out_specs),
        out_shape=out_shape, input_output_aliases=aliases or {},
        compiler_params=_cparams(("parallel",) * len(grid)))


def cast_into(src, src_row0, rows, slab, row0, me, name):
    width = src.shape[1]
    tr = _pick_rows(rows, 1024, row0, src_row0)
    assert rows % tr == 0 and row0 % tr == 0 and src_row0 % tr == 0

    def body(me_ref, s_ref, *refs):
        refs[-1][...] = s_ref[...].astype(refs[-1].dtype)

    in_specs = [pl.BlockSpec((tr, width), lambda r, me_ref: (src_row0 // tr + r, 0))]
    args = [src]
    aliases = {}
    if slab.arr is not None:
        in_specs.append(_ANY)
        args.append(slab.arr)
        aliases = {2: 0}
    slab.arr = _prefetch_call(
        body, name, (rows // tr,), in_specs,
        pl.BlockSpec((None, tr, width), lambda r, me_ref: (me_ref[0], row0 // tr + r, 0)),
        jax.ShapeDtypeStruct(slab.shape, slab.dtype), aliases)(me, *args)


def pair_add(g, b, c_idx, name):
    n, rh, w = b.shape
    tr = _pick_rows(rh, 1024)
    nb = rh // tr

    def body(c_ref, g_ref, b_ref, o_ref):
        o_ref[...] = (g_ref[...].astype(F32) + b_ref[...].astype(F32)).astype(o_ref.dtype)

    return _prefetch_call(
        body, name, (n, nb),
        [pl.BlockSpec((None, tr, w), lambda k, r, c: (k, c[0] * nb + r, 0)), pl.BlockSpec((None, tr, w), lambda k, r, c: (k, r, 0))],
        pl.BlockSpec((None, tr, w), lambda k, r, c: (k, r, 0)), jax.ShapeDtypeStruct(b.shape, BF16))(c_idx, g, b)


def chip_sum(p, rv, mc, name):
    n, rh, w = p.shape
    tr = _pick_rows(rh, 512)
    nb = rh // tr

    def body(mc_ref, p_ref, rv_ref, o_ref):
        me = mc_ref[0]
        acc = None
        for k in range(n):
            part = jnp.where(me == k, p_ref[...], rv_ref[k]).astype(F32)
            acc = part if acc is None else acc + part
        o_ref[...] = acc

    return _prefetch_call(
        body, name, (nb,),
        [pl.BlockSpec((None, tr, w), lambda r, mc_ref: (mc_ref[0], r, 0)), pl.BlockSpec((n, tr, w), lambda r, mc_ref: (0, r, 0))],
        pl.BlockSpec((tr, w), lambda r, mc_ref: (mc_ref[1] * nb + r, 0)), jax.ShapeDtypeStruct((2 * rh, w), F32))(mc, p, rv)


def adamw(red, row0, w, m, v, w_row0, rows, prev, name):
    cols = w.shape[1]
    tr = _pick_rows(rows, 512, row0, w_row0)
    assert rows % tr == 0 and row0 % tr == 0 and w_row0 % tr == 0

    def body(g_ref, w_ref, m_ref, v_ref, *refs):
        go_ref, d_ref, nm_ref, nv_ref = refs[-4:]
        gv = g_ref[...]
        nm = ADAM_B1 * m_ref[...] + (1.0 - ADAM_B1) * gv
        nv = ADAM_B2 * v_ref[...] + (1.0 - ADAM_B2) * (gv * gv)
        m_hat = nm / (1.0 - ADAM_B1 ** ADAM_STEP)
        v_hat = nv / (1.0 - ADAM_B2 ** ADAM_STEP)
        go_ref[...] = gv
        d_ref[...] = -ADAM_LR * (m_hat / (jnp.sqrt(v_hat) + ADAM_EPS) + ADAM_WD * w_ref[...])
        nm_ref[...] = nm
        nv_ref[...] = nv

    spec = pl.BlockSpec((tr, cols), lambda r: (w_row0 // tr + r, 0))
    sh = jax.ShapeDtypeStruct(w.shape, F32)
    in_specs = [pl.BlockSpec((tr, cols), lambda r: (row0 // tr + r, 0)), spec, spec, spec]
    args, aliases = [red, w, m, v], {}
    if prev is not None:
        in_specs += [_ANY] * 4
        args += list(prev)
        aliases = {4 + k: k for k in range(4)}
    return pl.pallas_call(
        body, name=name, grid=(rows // tr,), in_specs=in_specs, out_specs=[spec] * 4, out_shape=[sh] * 4,
        input_output_aliases=aliases, compiler_params=_cparams(("parallel",)),
    )(*args)


def _place():
    x, y, c = lax.axis_index("x"), lax.axis_index("y"), lax.axis_index("c")
    chips = [(1 - x, y), (x, 1 - y), (1 - x, 1 - y)]
    return x, y, c, chips


def _chip_index(cx, cy):
    return 2 * cx + cy


def _remote(src, dst, send_sem, recv_sem, to):
    return pltpu.make_async_remote_copy(src_ref=src, dst_ref=dst, send_sem=send_sem, recv_sem=recv_sem,
                                        device_id=to, device_id_type=MESH)


def _comm_call(body, name, ins, out_shapes, n_sems, aliases):
    return pl.pallas_call(
        body, name=name, in_specs=[_ANY] * len(ins), out_specs=[_ANY] * len(out_shapes), out_shape=out_shapes,
        scratch_shapes=[pltpu.SemaphoreType.DMA((n_sems,)), pltpu.SemaphoreType.DMA((n_sems,))],
        input_output_aliases=aliases,
    )(*ins)


def gather_slabs(slabs, name="weight_all_gather"):
    n = len(slabs)

    def body(*refs):
        in_refs, out_refs, send_sems, recv_sems = refs[:n], refs[n:2 * n], refs[-2], refs[-1]
        x, y, c, chips = _place()
        me = _chip_index(x, y)
        sib = (x, y, 1 - c)
        first, passed = [], []
        for a in range(n):
            rh = in_refs[a].shape[1] // 2
            mine = pl.ds(c * rh, rh)
            for j, chip in enumerate(chips):
                cp = _remote(in_refs[a].at[me, mine], out_refs[a].at[me, mine], send_sems.at[6 * a + j],
                             recv_sems.at[6 * a + j], (*chip, c))
                cp.start()
                first.append(cp)
        for a in range(n):
            rh = in_refs[a].shape[1] // 2
            mine = pl.ds(c * rh, rh)
            for j, chip in enumerate(chips):
                landed = out_refs[a].at[_chip_index(*chip), mine]
                _remote(landed, landed, send_sems.at[6 * a + j], recv_sems.at[6 * a + j], (*chip, c)).wait_recv()
                cp = _remote(landed, landed, send_sems.at[6 * a + 3 + j], recv_sems.at[6 * a + 3 + j], sib)
                cp.start()
                passed.append(cp)
        for a in range(n):
            rh = in_refs[a].shape[1] // 2
            theirs = pl.ds((1 - c) * rh, rh)
            for j, chip in enumerate(chips):
                got = out_refs[a].at[_chip_index(*chip), theirs]
                _remote(got, got, send_sems.at[6 * a + 3 + j], recv_sems.at[6 * a + 3 + j], sib).wait_recv()
        for cp in first + passed:
            cp.wait_send()

    return _comm_call(body, name, slabs, [jax.ShapeDtypeStruct(s.shape, s.dtype) for s in slabs], 6 * n,
                      {a: a for a in range(n)})


def pair_swap_halves(slabs, name="grad_pair_swap"):
    n = len(slabs)

    def body(*refs):
        in_refs, out_refs, send_sems, recv_sems = refs[:n], refs[n:2 * n], refs[-2], refs[-1]
        x, y, c, _ = _place()
        cps = []
        for a in range(n):
            rh = in_refs[a].shape[1] // 2
            cp = _remote(in_refs[a].at[:, pl.ds((1 - c) * rh, rh), :], out_refs[a], send_sems.at[a], recv_sems.at[a], (x, y, 1 - c))
            cp.start()
            cps.append(cp)
        for cp in cps:
            cp.wait()

    outs = [jax.ShapeDtypeStruct((s.shape[0], s.shape[1] // 2, s.shape[2]), s.dtype) for s in slabs]
    return _comm_call(body, name, slabs, outs, n, {})


def chip_exchange(parts, name="grad_chip_exchange"):
    n = len(parts)

    def body(*refs):
        in_refs, out_refs, send_sems, recv_sems = refs[:n], refs[n:2 * n], refs[-2], refs[-1]
        x, y, c, chips = _place()
        me = _chip_index(x, y)
        sends = []
        for a in range(n):
            for j, chip in enumerate(chips):
                cp = _remote(in_refs[a].at[_chip_index(*chip)], out_refs[a].at[me], send_sems.at[3 * a + j],
                             recv_sems.at[3 * a + j], (*chip, c))
                cp.start()
                sends.append(cp)
        for a in range(n):
            for j, chip in enumerate(chips):
                got = out_refs[a].at[_chip_index(*chip)]
                _remote(got, got, send_sems.at[3 * a + j], recv_sems.at[3 * a + j], (*chip, c)).wait_recv()
        for cp in sends:
            cp.wait_send()

    return _comm_call(body, name, parts, [jax.ShapeDtypeStruct(p.shape, p.dtype) for p in parts], 3 * n, {})


def pair_join_halves(reds, name="grad_pair_join"):
    n = len(reds)

    def body(*refs):
        in_refs, out_refs, send_sems, recv_sems = refs[:n], refs[n:2 * n], refs[-2], refs[-1]
        x, y, c, _ = _place()
        cps = []
        for a in range(n):
            rh = in_refs[a].shape[0] // 2
            mine = pl.ds(c * rh, rh)
            cp = _remote(in_refs[a].at[mine], out_refs[a].at[mine], send_sems.at[a], recv_sems.at[a], (x, y, 1 - c))
            cp.start()
            cps.append(cp)
        for a in range(n):
            rh = in_refs[a].shape[0] // 2
            got = out_refs[a].at[pl.ds((1 - c) * rh, rh)]
            _remote(got, got, send_sems.at[a], recv_sems.at[a], (x, y, 1 - c)).wait_recv()
        for cp in cps:
            cp.wait_send()

    return _comm_call(body, name, reds, [jax.ShapeDtypeStruct(r.shape, r.dtype) for r in reds], n, {a: a for a in range(n)})


_HBM = pl.BlockSpec(memory_space=pltpu.HBM)
_SEM = pl.BlockSpec(memory_space=pltpu.SEMAPHORE)
_EFFECT = pltpu.SideEffectType.DATAFLOW_SIDE_EFFECTING


def _in_hbm(a):
    return pltpu.with_memory_space_constraint(a, pltpu.HBM)


def _hbm_like(a):
    return pltpu.HBM(a.shape, a.dtype)


def _start_call(body, name, ins, n_sems, after):
    n = len(ins)
    res = pl.pallas_call(
        body, name=name, in_specs=[_HBM] * n + [_ANY],
        out_specs=[_SEM, _SEM] + [_HBM] * n + [pl.BlockSpec(memory_space=pltpu.VMEM)],
        out_shape=[pltpu.SemaphoreType.DMA((n_sems,)), pltpu.SemaphoreType.DMA((n_sems,))] + [_hbm_like(a) for a in ins]
        + [jax.ShapeDtypeStruct((8, LANES), F32)],
        input_output_aliases={a: 2 + a for a in range(n)},
        compiler_params=pltpu.CompilerParams(has_side_effects=_EFFECT),
    )(*[_in_hbm(a) for a in ins], after)
    return res[0], res[1], list(res[2:2 + n]), res[-1]


def _wait_call(body, name, thru, send_sems, recv_sems, after):
    n = len(thru)
    return pl.pallas_call(
        body, name=name, in_specs=[_HBM] * n + [_SEM, _SEM, _ANY], out_specs=[_HBM] * n,
        out_shape=[_hbm_like(a) for a in thru], input_output_aliases={a: a for a in range(n)},
        compiler_params=pltpu.CompilerParams(has_side_effects=_EFFECT),
    )(*thru, send_sems, recv_sems, after)


def gather_start(slabs, after, name="weight_gather_start"):
    n = len(slabs)

    def body(*refs):
        g_refs, send_sems, recv_sems, token = refs[:n], refs[n + 1], refs[n + 2], refs[-1]
        x, y, c, chips = _place()
        me = _chip_index(x, y)
        for a in range(n):
            rh = g_refs[a].shape[1] // 2
            mine = g_refs[a].at[me, pl.ds(c * rh, rh)]
            for j, chip in enumerate(chips):
                _remote(mine, mine, send_sems.at[3 * a + j], recv_sems.at[3 * a + j], (*chip, c)).start()
        token[...] = jnp.zeros_like(token)

    return _start_call(body, name, slabs, 3 * n, after)


def gather_wait(send_sems, recv_sems, thru, after, name="weight_gather_wait"):
    n = len(thru)

    def body(*refs):
        g_refs, send_sems, recv_sems = refs[:n], refs[n], refs[n + 1]
        x, y, c, chips = _place()
        me = _chip_index(x, y)
        for a in range(n):
            rh = g_refs[a].shape[1] // 2
            rows = pl.ds(c * rh, rh)
            for j, chip in enumerate(chips):
                mine, got = g_refs[a].at[me, rows], g_refs[a].at[_chip_index(*chip), rows]
                _remote(mine, mine, send_sems.at[3 * a + j], recv_sems.at[3 * a + j], (*chip, c)).wait_send()
                _remote(got, got, send_sems.at[3 * a + j], recv_sems.at[3 * a + j], (*chip, c)).wait_recv()

    return _wait_call(body, name, thru, send_sems, recv_sems, after)


def gather_forward(slabs, name="weight_gather_forward"):
    n = len(slabs)

    def body(*refs):
        in_refs, out_refs, send_sems, recv_sems = refs[:n], refs[n:2 * n], refs[-2], refs[-1]
        x, y, c, chips = _place()
        sib = (x, y, 1 - c)
        sends = []
        for a in range(n):
            rh = in_refs[a].shape[1] // 2
            for j, chip in enumerate(chips):
                k = _chip_index(*chip)
                cp = _remote(in_refs[a].at[k, pl.ds(c * rh, rh)], out_refs[a].at[k, pl.ds(c * rh, rh)], send_sems.at[3 * a + j],
                             recv_sems.at[3 * a + j], sib)
                cp.start()
                sends.append(cp)
        for a in range(n):
            rh = in_refs[a].shape[1] // 2
            for j, chip in enumerate(chips):
                got = out_refs[a].at[_chip_index(*chip), pl.ds((1 - c) * rh, rh)]
                _remote(got, got, send_sems.at[3 * a + j], recv_sems.at[3 * a + j], sib).wait_recv()
        for cp in sends:
            cp.wait_send()

    return _comm_call(body, name, slabs, [jax.ShapeDtypeStruct(s.shape, s.dtype) for s in slabs], 3 * n, {a: a for a in range(n)})


def exchange_start(parts, after, name="grad_exchange_start"):
    n = len(parts)

    def body(*refs):
        p_refs, land_refs, send_sems, recv_sems, token = refs[:n], refs[n:2 * n], refs[2 * n + 1], refs[2 * n + 2], refs[-1]
        x, y, c, chips = _place()
        me = _chip_index(x, y)
        for a in range(n):
            for j, chip in enumerate(chips):
                _remote(p_refs[a].at[_chip_index(*chip)], land_refs[a].at[me], send_sems.at[3 * a + j], recv_sems.at[3 * a + j],
                        (*chip, c)).start()
        token[...] = jnp.zeros_like(token)

    return _start_call(body, name, list(parts) + [lax.empty(p.shape, p.dtype) for p in parts], 3 * n, after)


def exchange_wait(send_sems, recv_sems, thru, after, name="grad_exchange_wait"):
    n = len(thru) // 2

    def body(*refs):
        p_refs, land_refs, send_sems, recv_sems = refs[:n], refs[n:2 * n], refs[2 * n], refs[2 * n + 1]
        x, y, c, chips = _place()
        me = _chip_index(x, y)
        for a in range(n):
            for j, chip in enumerate(chips):
                k = _chip_index(*chip)
                _remote(p_refs[a].at[k], land_refs[a].at[me], send_sems.at[3 * a + j], recv_sems.at[3 * a + j], (*chip, c)).wait_send()
                _remote(land_refs[a].at[k], land_refs[a].at[k], send_sems.at[3 * a + j], recv_sems.at[3 * a + j], (*chip, c)).wait_recv()

    res = _wait_call(body, name, thru, send_sems, recv_sems, after)
    return res[:n], res[n:]


_SLABS = {
    "mla_w_in": [("mla_w_in", 1, 0, 2)], "mla_w_uq": [("mla_w_uq", 2, 0, 2)], "mla_w_ukv": [("mla_w_ukv", 2, 0, 2)],
    "l0_mla_w_o": [("mla_w_o", 1, 0, 1)],
    "l0_w1024": [("mlp_w1", 2, 0, 1), ("mlp_w2", 1, 0, 1), ("xa_w_q", 1, 0, 1), ("xa_w_o", 1, 0, 1)],
    "l0_xa_w_kv": [("xa_w_kv", 2, 0, 1)],
    "l1_w1024": [("mlp_w1", 2, 1, 2), ("mlp_w2", 1, 1, 2), ("xa_w_q", 1, 1, 2), ("xa_w_o", 1, 1, 2), ("gdn_w_o", 1, 0, 1)],
    "l1_xa_w_kv": [("xa_w_kv", 2, 1, 2)], "gdn_w_in": [("gdn_w_in", 2, 0, 1)],
    "l23_w1024": [("mlp_w1", 2, 2, 4), ("mlp_w2", 1, 2, 4), ("xa_w_q", 1, 2, 4), ("xa_w_o", 1, 2, 4), ("mla_w_o", 1, 1, 2),
                  ("sc_w_o", 1, 0, 1)],
    "l23_xa_w_kv": [("xa_w_kv", 2, 2, 4)], "sc_w_in": [("sc_w_in", 2, 0, 1)],
}
_GROUPS = [(["mla_w_in", "mla_w_uq", "mla_w_ukv", "l0_mla_w_o"], None),
           (["l0_w1024", "l0_xa_w_kv"], (0, "xa")),
           (["l1_w1024", "l1_xa_w_kv", "gdn_w_in"], (1, "mix")),
           (["l23_w1024", "l23_xa_w_kv", "sc_w_in"], (2, "mix"))]
_RELAID = ("mla_w_in", "mla_w_uq", "mla_w_ukv", "gdn_w_in")
_SMALL = [("mla_q_norm", 1), ("mla_kv_norm", 1), ("gdn_conv_w", 2), ("sc_conv_w", 2)]
_REPL = ["gdn_a_log", "gdn_dt_bias", "gdn_o_norm", "norm_mix", "norm_mem", "norm_mlp", "mem_norm", "final_norm"]
_WEIGHTS = ['mla_w_in', 'mla_q_norm', 'mla_kv_norm', 'mla_w_uq', 'mla_w_ukv', 'mla_w_o', 'gdn_w_in', 'gdn_conv_w',
            'gdn_a_log', 'gdn_dt_bias', 'gdn_o_norm', 'gdn_w_o', 'sc_w_in', 'sc_conv_w', 'sc_w_o', 'norm_mix',
            'norm_mem', 'norm_mlp', 'xa_w_q', 'xa_w_kv', 'xa_w_o', 'mlp_w1', 'mlp_w2', 'mem_norm', 'final_norm']


class Layout:
    def __init__(self, shard_shapes):
        self.members, self.where, self.slab_dims = {}, {}, {}
        for slab, members in _SLABS.items():
            off, rows = 0, []
            for name, axis, l0, l1 in members:
                _, rpl, width = shard_shapes[name]
                rows.append((name, off, l0, l1, rpl))
                for layer in range(l0, l1):
                    self.where[(name, layer)] = (slab, off + (layer - l0) * rpl, rpl, width, axis)
                off += (l1 - l0) * rpl
            self.members[slab], self.slab_dims[slab] = rows, (off, width)

    def new_slabs(self, dtype):
        return {s: Slab(rows, width, dtype) for s, (rows, width) in self.slab_dims.items()}

    def loc(self, slabs, name, layer):
        slab, row0, rpl, width, axis = self.where[(name, layer)]
        if axis == 1:
            return Loc(slabs[slab], row0, N_CHIPS * rpl, width, 0)
        return Loc(slabs[slab], row0, rpl, N_CHIPS * width, 1)

    def _whole(self, name):
        (member,) = self.members[name]
        _, off, l0, l1, rpl = member
        assert off == 0 and l0 == 0
        return l1, rpl, self.slab_dims[name][1], dict((n, a) for n, a, _, _ in _SLABS[name])[name]

    def full(self, slabs, name):
        layers, rpl, width, axis = self._whole(name)
        blocks = slabs[name].arr.reshape(N_CHIPS, layers, rpl, width)
        return jnp.concatenate([blocks[s] for s in range(N_CHIPS)], axis=axis)

    def put_full(self, slabs, name, grad):
        layers, rpl, width, axis = self._whole(name)
        parts = jnp.stack(jnp.split(grad, N_CHIPS, axis=axis)).reshape(N_CHIPS, layers * rpl, width)
        slabs[name].arr = parts.astype(slabs[name].dtype)


def _small_pack(vals, names):
    flat = jnp.concatenate([vals[n].astype(F32).reshape(-1) for n in names])
    return jnp.pad(flat, (0, SMALL_ROWS * SMALL_COLS - flat.shape[0])).reshape(SMALL_ROWS, SMALL_COLS)


def _small_unpack(flat, like, names):
    out, off = {}, 0
    flat = flat.reshape(-1)
    for n in names:
        out[n] = flat[off:off + like[n].size].reshape(like[n].shape)
        off += like[n].size
    return out


_MLA_CFG = _Attn(MLA_H, 2 * LANES, MLA_NOPE, MLA_V, True, (MLA_NOPE + MLA_ROPE) ** -0.5, hp=8, hp_kv=4)
_XA_CFG = _Attn(XA_H, XA_D, XA_D, XA_D, False, XA_D ** -0.5, hp=4, hp_kv=4)


def _mla_weights(w_in, w_uq, w_ukv):
    w_in_p = jnp.pad(w_in, ((0, 0), (0, MLA_ZPAD - w_in.shape[1])))
    w_uq_p = jnp.pad(w_uq.reshape(MLA_QR, MLA_H, MLA_NOPE + MLA_ROPE), ((0, 0), (0, 0), (0, 2 * LANES - MLA_NOPE - MLA_ROPE)))
    w_uq_p = w_uq_p.reshape(MLA_QR, MLA_H * 2 * LANES)
    kv = w_ukv.reshape(MLA_KVR, MLA_H, MLA_NOPE + MLA_V)
    w_ukv_p = jnp.concatenate([kv[:, :, :MLA_NOPE].reshape(MLA_KVR, -1), kv[:, :, MLA_NOPE:].reshape(MLA_KVR, -1)], axis=1)
    return w_in_p, w_uq_p, w_ukv_p


def _mla_weight_grads(d_in_p, d_uq_p, d_ukv_p):
    d_in = d_in_p[:, :MLA_QR + MLA_KVR + MLA_ROPE]
    d_uq = d_uq_p.reshape(MLA_QR, MLA_H, 2 * LANES)[:, :, :MLA_NOPE + MLA_ROPE].reshape(MLA_QR, -1)
    half = MLA_H * MLA_NOPE
    d_ukv = jnp.concatenate([d_ukv_p[:, :half].reshape(MLA_KVR, MLA_H, MLA_NOPE),
                             d_ukv_p[:, half:].reshape(MLA_KVR, MLA_H, MLA_V)], axis=2).reshape(MLA_KVR, -1)
    return d_in, d_uq, d_ukv


def _mla_fwd(xs, h, wts, w_o, qn, kvn, tabs, g_next, tag):
    w_in_p, w_uq_p, w_ukv_p = wts
    z = mm(h, w_in_p, "nn", f"{tag}_in")
    cq, ckv, kr = mla_mid_fwd(z, qn, kvn, tabs, f"{tag}_mid")
    q = rope_q(mm(cq, w_uq_p, "nn", f"{tag}_uq"), tabs, False, f"{tag}_ropeq")
    kv = mm(ckv, w_ukv_p, "nn", f"{tag}_ukv", outs=(BF16,))
    o, lse = flash_fwd(_MLA_CFG, q, kv, kv, kr, f"{tag}_attn")
    xs, h_next = residual_norm(o, w_o, xs, g_next, f"{tag}_out")
    return xs, h_next, (z, cq, ckv, kr, q, kv, o, lse)


def _mla_bwd(dx, h, wts, w_o, g_wo, qn, kvn, tabs, saved, tag):
    w_in_p, w_uq_p, w_ukv_p = wts
    z, cq, ckv, kr, q, kv, o, lse = saved
    mm(o, dx, "tn", f"{tag}_dwo", outs=(BF16,), out_loc=g_wo)
    do = mm(dx, w_o, "nt", f"{tag}_do", outs=(BF16,))
    dq, delta = flash_dq(_MLA_CFG, q, kv, kv, kr, o, do, lse, F32, f"{tag}_attn_dq")
    dk1, dv, dkr = flash_dkv(_MLA_CFG, q, kv, kv, kr, do, lse, delta, BF16, f"{tag}_attn_dkv")
    dqp = rope_q(dq, tabs, True, f"{tag}_ropeq_t")
    d_uq_p = mm(cq, dqp, "tn", f"{tag}_duq")
    dcq = mm(dqp, w_uq_p, "nt", f"{tag}_dcq")
    dkv = jnp.concatenate([dk1, dv], axis=1)
    d_ukv_p = mm(ckv, dkv, "tn", f"{tag}_dukv")
    dckv = mm(dkv, w_ukv_p, "nt", f"{tag}_dckv")
    dz, dqn, dkvn = mla_mid_bwd(z, qn, kvn, tabs, dcq, dckv, dkr, f"{tag}_mid_bwd")
    d_in_p = mm(h, dz, "tn", f"{tag}_din")
    dh = (dz, w_in_p)
    d_in, d_uq, d_ukv = _mla_weight_grads(d_in_p, d_uq_p, d_ukv_p)
    return dh, dict(mla_w_in=d_in, mla_w_uq=d_uq, mla_w_ukv=d_ukv, mla_q_norm=dqn, mla_kv_norm=dkvn)


_GDN_QKV = 3 * GDN_H * GDN_D
_GDN_GATE_END = _GDN_QKV + GDN_H * GDN_D


def _gdn_weights(w_in):
    rep = lambda cols: jnp.repeat(cols, GDN_D, axis=1)
    return jnp.concatenate([w_in[:, :_GDN_GATE_END], rep(w_in[:, _GDN_GATE_END:_GDN_GATE_END + GDN_H]),
                            rep(w_in[:, _GDN_GATE_END + GDN_H:])], axis=1)


def _fold(x):
    return x.reshape(x.shape[0], -1, GDN_D).sum(-1)


def _gdn_fwd(xs, h, w_in_x, conv_w, a_log, dt_bias, o_norm, w_o, g_next, tag):
    z = mm(h, w_in_x, "nn", f"{tag}_in")
    qkv = gdn_conv_fwd(z, conv_w, f"{tag}_conv")
    a_x, dt_x = jnp.repeat(a_log.reshape(1, -1), GDN_D, axis=1), jnp.repeat(dt_bias.reshape(1, -1), GDN_D, axis=1)
    og, states, t_invs = gdn_chunk_fwd(qkv, z, a_x, dt_x, o_norm.reshape(1, -1), f"{tag}_chunks")
    xs, h_next = residual_norm(og, w_o, xs, g_next, f"{tag}_out")
    return xs, h_next, (z, qkv, a_x, dt_x, og, states, t_invs)


def _gdn_bwd(dx, h, w_in_x, conv_w, o_norm, w_o, g_wo, saved, tag):
    z, qkv, a_x, dt_x, og, states, t_invs = saved
    mm(og, dx, "tn", f"{tag}_dwo", outs=(BF16,), out_loc=g_wo)
    dog = mm(dx, w_o, "nt", f"{tag}_dog")
    dq, dk, dv, dgate, dbl, dal, da_x, ddt_x, don = gdn_chunk_bwd(qkv, z, a_x, dt_x, o_norm.reshape(1, -1), states, t_invs, dog,
                                                                  f"{tag}_chunks_bwd")
    dpre, dconv = gdn_conv_bwd(z, conv_w, jnp.concatenate([dq, dk, dv], axis=1), f"{tag}_conv_bwd")
    dz = jnp.concatenate([dpre, dgate, dbl, dal], axis=1)
    d_in_x = mm(h, dz, "tn", f"{tag}_din")
    dh = (dz, w_in_x)
    ge = _GDN_GATE_END
    d_in = jnp.concatenate([d_in_x[:, :ge], _fold(d_in_x[:, ge:ge + GDN_H * GDN_D]), _fold(d_in_x[:, ge + GDN_H * GDN_D:])], axis=1)
    return dh, dict(gdn_w_in=d_in, gdn_conv_w=dconv, gdn_a_log=_fold(da_x).reshape(-1), gdn_dt_bias=_fold(ddt_x).reshape(-1),
                    gdn_o_norm=don.reshape(-1))


def _sc_fwd(xs, h, w_in, conv_w, w_o, g_next, tag):
    z = mm(h, w_in, "nn", f"{tag}_in")
    y = sc_fwd(z, conv_w, f"{tag}_conv")
    xs, h_next = residual_norm(y, w_o, xs, g_next, f"{tag}_out")
    return xs, h_next, (z, y)


def _sc_bwd(dx, h, w_in, g_win, conv_w, w_o, g_wo, saved, tag):
    z, y = saved
    mm(y, dx, "tn", f"{tag}_dwo", outs=(BF16,), out_loc=g_wo)
    dy = mm(dx, w_o, "nt", f"{tag}_dy")
    db, dc, du, dconv = sc_bwd(z, conv_w, dy, f"{tag}_conv_bwd")
    dz = jnp.concatenate([db, dc, du], axis=1)
    mm(h, dz, "tn", f"{tag}_din", outs=(BF16,), out_loc=g_win)
    dh = (dz, w_in)
    return dh, dict(sc_conv_w=dconv)


def local_step(x, mem, pos, target, lay, wslabs, gslabs, small, before=None, after_bwd=None):
    depth = small["norm_mix"].shape[0]
    W = lambda name, layer: lay.loc(wslabs, name, layer)
    G = lambda name, layer: lay.loc(gslabs, name, layer)
    tabs = rope_tables(pos)
    mem_n = rmsnorm_fwd(mem, small["mem_norm"], "mem_norm")
    full = {n: lay.full(wslabs, n) for n in ("mla_w_in", "mla_w_uq", "mla_w_ukv")}
    mla_w = [_mla_weights(full["mla_w_in"][j], full["mla_w_uq"][j], full["mla_w_ukv"][j]) for j in range(full["mla_w_in"].shape[0])]
    gdn_in_x = {}

    xs, h_pre = x, None
    saved = []
    for i in range(depth):
        j, kind = i // 3, i % 3
        tag = f"l{i}"
        if before is not None:
            xs = before(i, "mix", xs)
        if kind == 1:
            gdn_in_x[j] = _gdn_weights(lay.full(wslabs, "gdn_w_in")[j])
        x_a = xs
        h = h_pre if h_pre is not None else rmsnorm_fwd(xs, small["norm_mix"][i], f"{tag}_norm_mix")
        g_mem = small["norm_mem"][i]
        if kind == 0:
            xs, hn, mix = _mla_fwd(xs, h, mla_w[j], W("mla_w_o", j), small["mla_q_norm"][j], small["mla_kv_norm"][j], tabs, g_mem,
                                   f"{tag}_mla")
        elif kind == 1:
            xs, hn, mix = _gdn_fwd(xs, h, gdn_in_x[j], small["gdn_conv_w"][j], small["gdn_a_log"][j], small["gdn_dt_bias"][j],
                                   small["gdn_o_norm"][j], W("gdn_w_o", j), g_mem, f"{tag}_gdn")
        else:
            xs, hn, mix = _sc_fwd(xs, h, W("sc_w_in", j), small["sc_conv_w"][j], W("sc_w_o", j), g_mem, f"{tag}_sc")
        if before is not None:
            xs = before(i, "xa", xs)
        x_b = xs
        xq = mm(hn, W("xa_w_q", i), "nn", f"{tag}_xa_q", outs=(BF16,))
        xkv = mm(mem_n, W("xa_w_kv", i), "nn", f"{tag}_xa_kv", outs=(BF16,))
        xo, xlse = flash_fwd(_XA_CFG, xq, xkv, xkv, None, f"{tag}_xa_attn")
        xs, hm = residual_norm(xo, W("xa_w_o", i), xs, small["norm_mlp"][i], f"{tag}_xa_out")
        x_c = xs
        h1, act = mm(hm, W("mlp_w1", i), "nn", f"{tag}_mlp_up", outs=(BF16, BF16), epi=_epi_relu2)
        xs, h_pre = residual_norm(act, W("mlp_w2", i), xs, small["norm_mix"][i + 1] if i + 1 < depth else None,
                                  f"{tag}_mlp_down", tm=512)
        saved.append((x_a, h, mix, x_b, hn, xq, xkv, xo, xlse, x_c, hm, h1, act))

    se, dx, d_final = loss_head(xs, small["final_norm"], target)

    per_layer = {n: [None] * depth for n in ("norm_mix", "norm_mem", "norm_mlp")}
    mixer = {}
    dmem_n = jnp.zeros(mem.shape, F32)
    for i in reversed(range(depth)):
        j, kind = i // 3, i % 3
        tag = f"l{i}"
        x_a, h, mix, x_b, hn, xq, xkv, xo, xlse, x_c, hm, h1, act = saved[i]
        mm(act, dx, "tn", f"{tag}_mlp_dw2", outs=(BF16,), out_loc=G("mlp_w2", i))
        dh1 = mm(dx, W("mlp_w2", i), "nt", f"{tag}_mlp_dh1", outs=(BF16,), epi=_epi_relu2_bwd, extras=(h1,))
        mm(hm, dh1, "tn", f"{tag}_mlp_dw1", outs=(BF16,), out_loc=G("mlp_w1", i))
        dx, dg = mm(dh1, W("mlp_w1", i), "nt", f"{tag}_mlp_dhm", epi=_epi_norm_bwd, extras=(x_c, dx), vecs=(small["norm_mlp"][i],),
                    row_outs=1, tm=512)
        per_layer["norm_mlp"][i] = dg.reshape(-1)
        mm(xo, dx, "tn", f"{tag}_xa_dwo", outs=(BF16,), out_loc=G("xa_w_o", i))
        dxo = mm(dx, W("xa_w_o", i), "nt", f"{tag}_xa_do", outs=(BF16,))
        dxq, xdelta = flash_dq(_XA_CFG, xq, xkv, xkv, None, xo, dxo, xlse, BF16, f"{tag}_xa_attn_dq")
        dxk, dxv = flash_dkv(_XA_CFG, xq, xkv, xkv, None, dxo, xlse, xdelta, BF16, f"{tag}_xa_attn_dkv")
        dxkv = jnp.concatenate([dxk, dxv], axis=1)
        mm(hn, dxq, "tn", f"{tag}_xa_dwq", outs=(BF16,), out_loc=G("xa_w_q", i))
        dx, dg = mm(dxq, W("xa_w_q", i), "nt", f"{tag}_xa_dhn", epi=_epi_norm_bwd, extras=(x_b, dx), vecs=(small["norm_mem"][i],),
                    row_outs=1, tm=512)
        per_layer["norm_mem"][i] = dg.reshape(-1)
        mm(mem_n, dxkv, "tn", f"{tag}_xa_dwkv", outs=(BF16,), out_loc=G("xa_w_kv", i))
        dmem_n = mm(dxkv, W("xa_w_kv", i), "nt", f"{tag}_xa_dmem", epi=_epi_add, extras=(dmem_n,))
        if after_bwd is not None:
            dx = after_bwd(i, "xa", dx)
        if kind == 0:
            dh, gr = _mla_bwd(dx, h, mla_w[j], W("mla_w_o", j), G("mla_w_o", j), small["mla_q_norm"][j], small["mla_kv_norm"][j],
                              tabs, mix, f"{tag}_mla")
        elif kind == 1:
            dh, gr = _gdn_bwd(dx, h, gdn_in_x[j], small["gdn_conv_w"][j], small["gdn_o_norm"][j], W("gdn_w_o", j), G("gdn_w_o", j),
                              mix, f"{tag}_gdn")
        else:
            dh, gr = _sc_bwd(dx, h, W("sc_w_in", j), G("sc_w_in", j), small["sc_conv_w"][j], W("sc_w_o", j), G("sc_w_o", j),
                             mix, f"{tag}_sc")
        if kind == 1:
            lay.put_full(gslabs, "gdn_w_in", gr.pop("gdn_w_in")[None])
        for n, g in gr.items():
            mixer.setdefault(n, {})[j] = g
        dz_mix, w_mix = dh
        dx, dg = mm(dz_mix, w_mix, "nt", f"{tag}_mix_dh", epi=_epi_norm_bwd, extras=(x_a, dx), vecs=(small["norm_mix"][i],),
                    row_outs=1, tm=256 if kind == 1 else 512)
        per_layer["norm_mix"][i] = dg.reshape(-1)
        if after_bwd is not None:
            dx = after_bwd(i, "mix", dx)

    _, d_mem_norm = rmsnorm_bwd(mem, small["mem_norm"], dmem_n, jnp.zeros(mem.shape, F32), "mem_norm_bwd")
    grads = {n: jnp.stack(v) for n, v in per_layer.items()}
    for n, by_j in mixer.items():
        grads[n] = jnp.stack([by_j[j] for j in sorted(by_j)])
    grads["mem_norm"] = d_mem_norm
    grads["final_norm"] = d_final
    for n in ("mla_w_in", "mla_w_uq", "mla_w_ukv"):
        lay.put_full(gslabs, n, grads.pop(n))
    return se, dx, grads


def kernel(x, mem, positions, mla_w_in, mla_q_norm, mla_kv_norm, mla_w_uq, mla_w_ukv, mla_w_o, gdn_w_in, gdn_conv_w, gdn_a_log, gdn_dt_bias, gdn_o_norm, gdn_w_o, sc_w_in, sc_conv_w, sc_w_o, norm_mix, norm_mem, norm_mlp, xa_w_q, xa_w_kv, xa_w_o, mlp_w1, mlp_w2, mem_norm, final_norm, loss_target, m_mla_w_in, m_mla_q_norm, m_mla_kv_norm, m_mla_w_uq, m_mla_w_ukv, m_mla_w_o, m_gdn_w_in, m_gdn_conv_w, m_gdn_a_log, m_gdn_dt_bias, m_gdn_o_norm, m_gdn_w_o, m_sc_w_in, m_sc_conv_w, m_sc_w_o, m_norm_mix, m_norm_mem, m_norm_mlp, m_xa_w_q, m_xa_w_kv, m_xa_w_o, m_mlp_w1, m_mlp_w2, m_mem_norm, m_final_norm, v_mla_w_in, v_mla_q_norm, v_mla_kv_norm, v_mla_w_uq, v_mla_w_ukv, v_mla_w_o, v_gdn_w_in, v_gdn_conv_w, v_gdn_a_log, v_gdn_dt_bias, v_gdn_o_norm, v_gdn_w_o, v_sc_w_in, v_sc_conv_w, v_sc_w_o, v_norm_mix, v_norm_mem, v_norm_mlp, v_xa_w_q, v_xa_w_kv, v_xa_w_o, v_mlp_w1, v_mlp_w2, v_mem_norm, v_final_norm):
    given = dict(locals())
    p = {n: given[n] for n in _WEIGHTS}
    mom = {n: given["m_" + n] for n in _WEIGHTS}
    var = {n: given["v_" + n] for n in _WEIGHTS}
    split = sorted({n for members in _SLABS.values() for n, _, _, _ in members})
    lay = Layout({n: p[n].shape for n in split})
    flat2d = lambda a: a.reshape(-1, a.shape[-1])

    me = (2 * lax.axis_index("x") + lax.axis_index("y")).astype(jnp.int32)
    core = lax.axis_index("c").astype(jnp.int32)
    me1, c1, mc = me.reshape(1), core.reshape(1), jnp.stack([me, core])

    wslabs = lay.new_slabs(BF16)

    def cast_group(slabs, chip):
        for slab in slabs:
            for name, off, l0, l1, rpl in lay.members[slab]:
                cast_into(flat2d(p[name]), l0 * rpl, (l1 - l0) * rpl, wslabs[slab], off, chip, f"cast_{slab}_{name}")

    first = _GROUPS[0][0]
    cast_group(first, me1)
    small_names = [n for n, _ in _SMALL]
    words = lax.bitcast_convert_type(jnp.concatenate([p[n].reshape(-1) for n in small_names]), BF16).reshape(-1)
    words = jnp.pad(words, (0, SMALL_ROWS * SMALL_COLS - words.shape[0])).reshape(1, SMALL_ROWS, SMALL_COLS)
    small_slab = lax.dynamic_update_slice(jnp.zeros((N_CHIPS, SMALL_ROWS, SMALL_COLS), BF16), words, (me, 0, 0))

    send, recv, thru, token = gather_start([wslabs[s].arr for s in first] + [small_slab], me1, "weight_gather_start_first")
    me_after = me1 + token[0, 0].astype(jnp.int32)
    for slabs, _ in _GROUPS[1:]:
        cast_group(slabs, me_after)
    landed = gather_wait(send, recv, thru, wslabs[_GROUPS[-1][0][-1]].arr, "weight_gather_wait_first")
    gathered = gather_forward(landed, "weight_gather_forward_first")
    for s, arr in zip(first, gathered):
        wslabs[s].arr = arr
    in_flight, token = {}, gathered[-1]
    for slabs, point in _GROUPS[1:]:
        send, recv, thru, token = gather_start([wslabs[s].arr for s in slabs], token, f"weight_gather_start_{slabs[0]}")
        in_flight[point] = (send, recv, thru, slabs)
    started_token = token

    def before(i, stage, xs):
        if (i, stage) == (0, "mix"):
            return xs + started_token[0, 0]
        if (i, stage) in in_flight:
            send, recv, thru, slabs = in_flight[(i, stage)]
            landed = gather_wait(send, recv, thru, xs, f"weight_gather_wait_{slabs[0]}")
            for s, arr in zip(slabs, gather_forward(landed, f"weight_gather_forward_{slabs[0]}")):
                wslabs[s].arr = arr
        return xs

    small = {n: p[n] for n in _REPL}
    got, off = gathered[-1].reshape(N_CHIPS, -1), 0
    for n, ax in _SMALL:
        vals = lax.bitcast_convert_type(got[:, off:off + 2 * p[n].size].reshape(N_CHIPS, p[n].size, 2), F32)
        vals = vals.reshape((N_CHIPS,) + p[n].shape)
        small[n] = jnp.concatenate([vals[s] for s in range(N_CHIPS)], axis=ax)
        off += 2 * p[n].size

    gslabs = lay.new_slabs(BF16)
    complete_at = {point: slabs for slabs, point in _GROUPS[1:]}
    exchanging = []

    def after_bwd(i, stage, dx):
        if (i, stage) not in complete_at:
            return dx
        slabs = complete_at[(i, stage)]
        g = [gslabs[s].arr for s in slabs]
        swapped = pair_swap_halves(g, f"grad_pair_swap_{slabs[0]}")
        part = [pair_add(a, b, c1, f"pair_add_{s}") for a, b, s in zip(g, swapped, slabs)]
        send, recv, thru, token = exchange_start(part, c1, f"grad_exchange_start_{slabs[0]}")
        exchanging.append((slabs, send, recv, thru))
        return dx + token[0, 0]

    se, dx, sgrads = local_step(x[0], mem[0], positions.reshape(-1, 1), loss_target[0], lay, wslabs, gslabs, small,
                                before, after_bwd)
    loss = lax.psum(0.5 * jnp.sum(se) / x.shape[-1], ("x", "y", "c"))
    names, parts, received = [], [], []
    for slabs, send, recv, thru in exchanging:
        part, got = exchange_wait(send, recv, thru, dx, f"grad_exchange_wait_{slabs[0]}")
        names, parts, received = names + slabs, parts + list(part), received + list(got)

    axes = dict(_SMALL)
    small_order = small_names + _REPL
    slots = []
    for s in range(N_CHIPS):
        vals = {n: (lax.slice_in_dim(g, s * p[n].shape[axes[n]], (s + 1) * p[n].shape[axes[n]], axis=axes[n]) if n in axes else g)
                for n, g in sgrads.items()}
        slots.append(_small_pack(vals, small_order))
    g_last = [gslabs[s].arr for s in first] + [jnp.stack(slots).astype(BF16)]
    names_last = first + ["small"]
    swapped_last = pair_swap_halves(g_last, "grad_pair_swap_last")
    part_last = [pair_add(g, b, c1, f"pair_add_{s}") for g, b, s in zip(g_last, swapped_last, names_last)]
    send, recv, thru, token = exchange_start(part_last, c1, "grad_exchange_start_last")
    mc_after = mc + token[0, 0].astype(jnp.int32)
    halves = [chip_sum(q, r, mc_after, f"chip_sum_{s}") for q, r, s in zip(parts, received, names)]
    part_last, got_last = exchange_wait(send, recv, thru, halves[-1], "grad_exchange_wait_last")
    halves += [chip_sum(q, r, mc, f"chip_sum_{s}") for q, r, s in zip(part_last, got_last, names_last)]
    reduced = dict(zip(names + names_last, pair_join_halves(halves)))

    res = {}
    for slab in _SLABS:
        for name, off, l0, l1, rpl in lay.members[slab]:
            res[name] = adamw(reduced[slab], off, flat2d(p[name]), flat2d(mom[name]), flat2d(var[name]), l0 * rpl, (l1 - l0) * rpl,
                              res.get(name), f"adamw_{slab}_{name}")
    for name in split:
        res[name] = [o.reshape(p[name].shape) for o in res[name]]
    sp = {k: _small_pack(d, small_order) for k, d in (("w", p), ("m", mom), ("v", var))}
    outs = adamw(reduced["small"], 0, sp["w"], sp["m"], sp["v"], 0, SMALL_ROWS, None, "adamw_small")
    unpacked = [_small_unpack(o, p, small_order) for o in outs]
    for n in small_order:
        res[n] = [u[n] for u in unpacked]
    return (loss, dx[None], *[res[n][k] for k in range(4) for n in _WEIGHTS])
```

```python
import jax
import jax.numpy as jnp
from jax import lax
from jax.experimental import pallas as pl
from jax.experimental.pallas import tpu as pltpu

F32 = jnp.float32
BF16 = jnp.bfloat16
HI = lax.Precision.HIGHEST
MESH = pl.DeviceIdType.MESH

EPS = 1e-6
ROPE_THETA = 10000.0
N_CHIPS = 4
LANES = 128
VMEM_LIMIT = 56 * 1024 * 1024
NEG = -1e30

MLA_H, MLA_NOPE, MLA_ROPE, MLA_V = 8, 128, 64, 128
MLA_QR, MLA_KVR = 384, 256
MLA_ZPAD = 768
GDN_H, GDN_D, GDN_C = 8, 128, 64
XA_H, XA_D = 4, 256

ADAM_LR, ADAM_B1, ADAM_B2, ADAM_EPS, ADAM_WD, ADAM_STEP = 0.001, 0.9, 0.999, 1e-08, 0.01, 10

SMALL_ROWS, SMALL_COLS = 32, 1024


def _cparams(sem=None):
    return pltpu.CompilerParams(dimension_semantics=sem, vmem_limit_bytes=VMEM_LIMIT)


def _pick(dim, pref):
    t = (min(pref, dim) // LANES) * LANES
    while t >= LANES:
        if dim % t == 0:
            return t
        t -= LANES
    return dim


def _pick_rows(rows, pref, *offsets):
    t = (min(pref, rows) // 16) * 16
    while t > 16 and (rows % t or any(o % t for o in offsets)):
        t -= 16
    return t


class Slab:
    def __init__(self, rows, width, dtype, arr=None):
        self.shape, self.dtype, self.arr = (N_CHIPS, rows, width), dtype, arr


class Loc:
    def __init__(self, slab, row0, K, N, axis):
        self.slab, self.row0, self.K, self.N, self.axis = slab, row0, K, N, axis
        self.Ks = K // N_CHIPS if axis == 0 else K
        self.Ns = N // N_CHIPS if axis == 1 else N

    def tile_spec(self, tr, tc, rc):
        assert self.row0 % tr == 0 and self.Ks % tr == 0 and self.Ns % tc == 0, (self.row0, self.Ks, self.Ns, tr, tc)
        r0, rb, cb = self.row0 // tr, self.Ks // tr, self.Ns // tc
        if self.axis == 0:
            return pl.BlockSpec((None, tr, tc), lambda i, j: (rc(i, j)[0] // rb, r0 + rc(i, j)[0] % rb, rc(i, j)[1]))
        return pl.BlockSpec((None, tr, tc), lambda i, j: (rc(i, j)[1] // cb, r0 + rc(i, j)[0], rc(i, j)[1] % cb))

    def slot_spec(self, slot, tr, tc, rc):
        assert self.row0 % tr == 0, (self.row0, tr)
        r0 = self.row0 // tr
        return pl.BlockSpec((None, tr, tc), lambda i, j: (slot, r0 + rc(i, j)[0], rc(i, j)[1]))


_DIMS = {"nn": ((1,), (0,)), "nt": ((1,), (1,)), "tn": ((0,), (0,))}
_ANY = pl.BlockSpec(memory_space=pl.ANY)


def mm(a, b, mode, name, outs=(F32,), epi=None, extras=(), tm=1024, tn=1024, out_loc=None, vecs=(), row_outs=0):
    full_rows = bool(vecs) or row_outs > 0
    b_loc = b if isinstance(b, Loc) else None
    if mode == "nn":
        M, K = a.shape
        K2, N = (b_loc.K, b_loc.N) if b_loc else b.shape
    elif mode == "nt":
        M, K = a.shape
        N, K2 = (b_loc.K, b_loc.N) if b_loc else b.shape
    else:
        K, M = a.shape
        K2, N = b.shape
    assert K == K2, (name, a.shape, K2, N)
    tm = _pick(out_loc.Ks if (out_loc and out_loc.axis == 0) else M, tm)
    n_split = full_rows and b_loc is not None and mode == "nt" and b_loc.axis == 0
    if out_loc is not None and out_loc.axis == 1:
        tn = _pick(out_loc.Ns, tn)
    elif n_split:
        tn = N
    elif b_loc is not None and ((mode == "nn" and b_loc.axis == 1) or (mode == "nt" and b_loc.axis == 0)):
        tn = _pick(b_loc.Ns if mode == "nn" else b_loc.Ks, tn)
    elif b_loc is not None:
        tn = N if full_rows else _pick(N, min(tn, 512))
    else:
        tn = N if full_rows else _pick(N, tn)
    assert tn == N or not full_rows, name

    parts = 1
    if mode == "tn":
        a_spec = pl.BlockSpec((K, tm), lambda i, j: (0, i))
        b_specs, b_args = [pl.BlockSpec((K, tn), lambda i, j: (0, j))], [b]
    else:
        a_spec = pl.BlockSpec((tm, K), lambda i, j: (i, 0))
        if b_loc is None:
            b_specs = [pl.BlockSpec((K, tn), lambda i, j: (0, j)) if mode == "nn" else pl.BlockSpec((tn, K), lambda i, j: (j, 0))]
            b_args = [b]
        elif mode == "nn" and b_loc.axis == 1:
            b_specs, b_args = [b_loc.tile_spec(K, tn, lambda i, j: (0, j))], [b_loc.slab.arr]
        elif n_split:
            b_specs = [b_loc.slot_spec(s, b_loc.Ks, K, lambda i, j: (0, 0)) for s in range(N_CHIPS)]
            b_args = [b_loc.slab.arr] * N_CHIPS
        elif mode == "nt" and b_loc.axis == 0:
            b_specs, b_args = [b_loc.tile_spec(tn, K, lambda i, j: (j, 0))], [b_loc.slab.arr]
        elif mode == "nn":
            parts = N_CHIPS
            b_specs = [b_loc.slot_spec(s, b_loc.Ks, tn, lambda i, j: (0, j)) for s in range(parts)]
            b_args = [b_loc.slab.arr] * parts
        else:
            parts = N_CHIPS
            b_specs = [b_loc.slot_spec(s, tn, b_loc.Ns, lambda i, j: (j, 0)) for s in range(parts)]
            b_args = [b_loc.slab.arr] * parts
    kp = K // parts
    n_b = N_CHIPS if n_split else parts
    n_ex, n_out = len(extras) + len(vecs), len(outs)
    dims = (_DIMS[mode], ((), ()))

    def body(*refs):
        a_ref = refs[0]
        b_refs = refs[1:1 + n_b]
        ex_refs = refs[1 + n_b:1 + n_b + n_ex]
        o_refs = refs[len(refs) - n_out - row_outs:len(refs) - row_outs]
        r_refs = refs[len(refs) - row_outs:]
        acc = None
        if n_split:
            av = a_ref[...].astype(BF16)
            acc = jnp.concatenate([lax.dot_general(av, b_ref[...].astype(BF16), dims, preferred_element_type=F32)
                                   for b_ref in b_refs], axis=1)
        for s in range(0 if n_split else parts):
            av = a_ref[...] if parts == 1 else a_ref[:, s * kp:(s + 1) * kp]
            d = lax.dot_general(av.astype(BF16), b_refs[s][...].astype(BF16), dims, preferred_element_type=F32)
            acc = d if acc is None else acc + d
        res = epi(acc, *[e[...] for e in ex_refs]) if epi is not None else (acc,)
        for o_ref, v in zip(o_refs, res[:n_out]):
            o_ref[...] = v.astype(o_ref.dtype)
        for r_ref, v in zip(r_refs, res[n_out:]):
            @pl.when(pl.program_id(0) == 0)
            def _():
                r_ref[...] = jnp.zeros_like(r_ref)

            r_ref[...] += v

    mn_spec = pl.BlockSpec((tm, tn), lambda i, j: (i, j))
    row_spec = pl.BlockSpec((1, tn), lambda i, j: (0, j))
    in_specs = [a_spec] + b_specs + [mn_spec] * len(extras) + [row_spec] * len(vecs)
    args = [a] + b_args + list(extras) + [v.reshape(1, N) for v in vecs]
    aliases = {}
    if out_loc is None:
        out_specs = [mn_spec] * n_out + [row_spec] * row_outs
        out_shape = [jax.ShapeDtypeStruct((M, N), d) for d in outs] + [jax.ShapeDtypeStruct((1, N), F32)] * row_outs
    else:
        assert n_out == 1 and mode == "tn"
        out_specs = [out_loc.tile_spec(tm, tn, lambda i, j: (i, j))]
        out_shape = [jax.ShapeDtypeStruct(out_loc.slab.shape, out_loc.slab.dtype)]
        if out_loc.slab.arr is not None:
            in_specs.append(_ANY)
            args.append(out_loc.slab.arr)
            aliases = {len(args) - 1: 0}

    res = pl.pallas_call(
        body, name=name, grid=(M // tm, N // tn), in_specs=in_specs, out_specs=out_specs, out_shape=out_shape,
        input_output_aliases=aliases, compiler_params=_cparams(("arbitrary" if row_outs else "parallel", "parallel")),
    )(*args)
    if out_loc is not None:
        out_loc.slab.arr = res[0]
        return None
    return res[0] if len(res) == 1 else tuple(res)


def _epi_add(acc, r):
    return (acc + r,)


def _epi_add_norm(acc, r, g):
    x = acc + r
    return x, _rms(x, g)


def _epi_norm_bwd(acc, x, dx_in, g):
    r = lax.rsqrt(jnp.mean(x * x, axis=-1, keepdims=True) + EPS)
    xh = x * r
    dxh = acc * g
    dx = dx_in + r * (dxh - xh * jnp.mean(dxh * xh, axis=-1, keepdims=True))
    return dx, jnp.sum(acc * xh, axis=0, keepdims=True)


def residual_norm(a, w, xs, g, name, tm=1024):
    if g is None:
        return mm(a, w, "nn", name, epi=_epi_add, extras=(xs,), tm=tm), None
    return mm(a, w, "nn", name, outs=(F32, BF16), epi=_epi_add_norm, extras=(xs,), vecs=(g,), tm=tm)


def _epi_relu2(acc):
    r = jnp.maximum(acc, 0.0)
    return acc, r * r


def _epi_relu2_bwd(acc, h1):
    return (acc * (2.0 * jnp.maximum(h1.astype(F32), 0.0)),)


def _rms(x, g):
    return x * lax.rsqrt(jnp.mean(x * x, axis=-1, keepdims=True) + EPS) * g


def _row_spec(ts, cols):
    return pl.BlockSpec((ts, cols), lambda i: (i, 0))


def _par_spec(cols):
    return pl.BlockSpec((1, cols), lambda i: (0, 0))


def rmsnorm_fwd(x, g, name, ts=256):
    T, D = x.shape
    ts = min(ts, T)

    def body(x_ref, g_ref, o_ref):
        o_ref[...] = _rms(x_ref[...], g_ref[...]).astype(o_ref.dtype)

    return pl.pallas_call(
        body, name=name, grid=(T // ts,),
        in_specs=[_row_spec(ts, D), _par_spec(D)], out_specs=_row_spec(ts, D),
        out_shape=jax.ShapeDtypeStruct((T, D), BF16), compiler_params=_cparams(("parallel",)),
    )(x, g.reshape(1, D))


def rmsnorm_bwd(x, g, dy, dx_in, name, ts=256):
    T, D = x.shape
    ts = min(ts, T)

    def body(x_ref, g_ref, dy_ref, dxi_ref, dx_ref, dg_ref):
        xv = x_ref[...]
        r = lax.rsqrt(jnp.mean(xv * xv, axis=-1, keepdims=True) + EPS)
        xh = xv * r
        dyv = dy_ref[...].astype(F32)
        dxh = dyv * g_ref[...]
        dx_ref[...] = dxi_ref[...] + r * (dxh - xh * jnp.mean(dxh * xh, axis=-1, keepdims=True))
        dg = jnp.sum(dyv * xh, axis=0, keepdims=True)

        @pl.when(pl.program_id(0) == 0)
        def _():
            dg_ref[...] = jnp.zeros_like(dg_ref)

        dg_ref[...] += dg

    dx, dg = pl.pallas_call(
        body, name=name, grid=(T // ts,),
        in_specs=[_row_spec(ts, D), _par_spec(D), _row_spec(ts, D), _row_spec(ts, D)],
        out_specs=[_row_spec(ts, D), _par_spec(D)],
        out_shape=[jax.ShapeDtypeStruct((T, D), F32), jax.ShapeDtypeStruct((1, D), F32)],
        compiler_params=_cparams(("arbitrary",)),
    )(x, g.reshape(1, D), dy, dx_in)
    return dx, dg.reshape(D)


def rope_tables(pos, name="rope_tables"):
    T = pos.shape[0]
    half = MLA_ROPE // 2
    inv = ROPE_THETA ** (-jnp.arange(0, MLA_ROPE, 2, dtype=F32) / MLA_ROPE)
    inv_row = jnp.concatenate([inv, inv, jnp.zeros((LANES - MLA_ROPE,), F32)]).reshape(1, LANES)

    def body(p_ref, f_ref, c_ref, a_ref, b_ref):
        ang = p_ref[...].astype(F32) * f_ref[...]
        lane = lax.broadcasted_iota(jnp.int32, ang.shape, 1)
        c, s = jnp.cos(ang), jnp.sin(ang)
        c_ref[...] = jnp.where(lane < MLA_ROPE, c, 0.0)
        a_ref[...] = jnp.where(lane < half, -s, 0.0)
        b_ref[...] = jnp.where((lane >= half) & (lane < MLA_ROPE), s, 0.0)

    sh = jax.ShapeDtypeStruct((T, LANES), F32)
    return pl.pallas_call(body, name=name, out_shape=[sh, sh, sh], compiler_params=_cparams())(pos, inv_row)


def _roll_l(x):
    return pltpu.roll(x, LANES - MLA_ROPE // 2, 1)


def _roll_r(x):
    return pltpu.roll(x, MLA_ROPE // 2, 1)


def _rope(r, c, sa, sb):
    return r * c + _roll_l(r) * sa + _roll_r(r) * sb


def _rope_t(d, c, sa, sb):
    return d * c + _roll_r(d * sa) + _roll_l(d * sb)


def mla_mid_fwd(z, qn, kvn, tabs, name, ts=256):
    T = z.shape[0]
    ts = min(ts, T)
    a0, a1 = MLA_QR, MLA_QR + MLA_KVR

    def body(z_ref, qn_ref, kvn_ref, c_ref, sa_ref, sb_ref, cq_ref, ckv_ref, kr_ref):
        cq_ref[...] = _rms(z_ref[:, 0:a0], qn_ref[...]).astype(BF16)
        ckv_ref[...] = _rms(z_ref[:, a0:a1], kvn_ref[...]).astype(BF16)
        kr_ref[...] = _rope(z_ref[:, a1:MLA_ZPAD], c_ref[...], sa_ref[...], sb_ref[...]).astype(BF16)

    return pl.pallas_call(
        body, name=name, grid=(T // ts,),
        in_specs=[_row_spec(ts, MLA_ZPAD), _par_spec(MLA_QR), _par_spec(MLA_KVR)] + [_row_spec(ts, LANES)] * 3,
        out_specs=[_row_spec(ts, MLA_QR), _row_spec(ts, MLA_KVR), _row_spec(ts, LANES)],
        out_shape=[jax.ShapeDtypeStruct((T, MLA_QR), BF16), jax.ShapeDtypeStruct((T, MLA_KVR), BF16),
                   jax.ShapeDtypeStruct((T, LANES), BF16)],
        compiler_params=_cparams(("parallel",)),
    )(z, qn.reshape(1, -1), kvn.reshape(1, -1), *tabs)


def mla_mid_bwd(z, qn, kvn, tabs, dcq, dckv, dkr, name, ts=256):
    T = z.shape[0]
    ts = min(ts, T)
    a0, a1 = MLA_QR, MLA_QR + MLA_KVR

    def body(z_ref, qn_ref, kvn_ref, c_ref, sa_ref, sb_ref, dcq_ref, dckv_ref, dkr_ref, dz_ref, dqn_ref, dkvn_ref):
        _, vq = jax.vjp(_rms, z_ref[:, 0:a0], qn_ref[...])
        dzq, dqn = vq(dcq_ref[...].astype(F32))
        _, vk = jax.vjp(_rms, z_ref[:, a0:a1], kvn_ref[...])
        dzk, dkvn = vk(dckv_ref[...].astype(F32))
        dz_ref[:, 0:a0] = dzq.astype(dz_ref.dtype)
        dz_ref[:, a0:a1] = dzk.astype(dz_ref.dtype)
        dz_ref[:, a1:MLA_ZPAD] = _rope_t(dkr_ref[...].astype(F32), c_ref[...], sa_ref[...], sb_ref[...]).astype(dz_ref.dtype)

        @pl.when(pl.program_id(0) == 0)
        def _():
            dqn_ref[...] = jnp.zeros_like(dqn_ref)
            dkvn_ref[...] = jnp.zeros_like(dkvn_ref)

        dqn_ref[...] += dqn
        dkvn_ref[...] += dkvn

    dz, dqn, dkvn = pl.pallas_call(
        body, name=name, grid=(T // ts,),
        in_specs=[_row_spec(ts, MLA_ZPAD), _par_spec(MLA_QR), _par_spec(MLA_KVR)] + [_row_spec(ts, LANES)] * 3
        + [_row_spec(ts, MLA_QR), _row_spec(ts, MLA_KVR), _row_spec(ts, LANES)],
        out_specs=[_row_spec(ts, MLA_ZPAD), _par_spec(MLA_QR), _par_spec(MLA_KVR)],
        out_shape=[jax.ShapeDtypeStruct((T, MLA_ZPAD), BF16), jax.ShapeDtypeStruct((1, MLA_QR), F32),
                   jax.ShapeDtypeStruct((1, MLA_KVR), F32)],
        compiler_params=_cparams(("arbitrary",)),
    )(z, qn.reshape(1, -1), kvn.reshape(1, -1), *tabs, dcq, dckv, dkr)
    return dz, dqn.reshape(-1), dkvn.reshape(-1)


def rope_q(q, tabs, transpose, name, ts=256):
    T, W = q.shape
    ts = min(ts, T)
    fn = _rope_t if transpose else _rope
    hw = 2 * LANES

    def body(q_ref, c_ref, sa_ref, sb_ref, o_ref):
        c, sa, sb = c_ref[...], sa_ref[...], sb_ref[...]
        for h in range(W // hw):
            o_ref[:, h * hw:h * hw + LANES] = q_ref[:, h * hw:h * hw + LANES].astype(o_ref.dtype)
            o_ref[:, h * hw + LANES:(h + 1) * hw] = fn(q_ref[:, h * hw + LANES:(h + 1) * hw].astype(F32), c, sa, sb).astype(o_ref.dtype)

    return pl.pallas_call(
        body, name=name, grid=(T // ts,),
        in_specs=[_row_spec(ts, W)] + [_row_spec(ts, LANES)] * 3, out_specs=_row_spec(ts, W),
        out_shape=jax.ShapeDtypeStruct((T, W), BF16), compiler_params=_cparams(("parallel",)),
    )(q, *tabs)


def loss_head(x, g, target, name="loss_head", ts=256):
    T, D = x.shape
    ts = min(ts, T)

    def body(x_ref, g_ref, t_ref, se_ref, dx_ref, dg_ref):
        xv = x_ref[...]
        r = lax.rsqrt(jnp.mean(xv * xv, axis=-1, keepdims=True) + EPS)
        xh = xv * r
        err = xh * g_ref[...] - t_ref[...]
        dy = err * (1.0 / D)
        dxh = dy * g_ref[...]
        dx_ref[...] = r * (dxh - xh * jnp.mean(dxh * xh, axis=-1, keepdims=True))

        @pl.when(pl.program_id(0) == 0)
        def _():
            se_ref[...] = jnp.zeros_like(se_ref)
            dg_ref[...] = jnp.zeros_like(dg_ref)

        se_ref[...] += jnp.sum(err * err, axis=0, keepdims=True)
        dg_ref[...] += jnp.sum(dy * xh, axis=0, keepdims=True)

    se, dx, dg = pl.pallas_call(
        body, name=name, grid=(T // ts,),
        in_specs=[_row_spec(ts, D), _par_spec(D), _row_spec(ts, D)],
        out_specs=[_par_spec(D), _row_spec(ts, D), _par_spec(D)],
        out_shape=[jax.ShapeDtypeStruct((1, D), F32), jax.ShapeDtypeStruct((T, D), F32), jax.ShapeDtypeStruct((1, D), F32)],
        compiler_params=_cparams(("arbitrary",)),
    )(x, g.reshape(1, D), target)
    return se, dx, dg.reshape(D)


def _dot_nt(a, b):
    return lax.dot_general(a, b, (((1,), (1,)), ((), ())), preferred_element_type=F32)


def _dot_tn(a, b):
    return lax.dot_general(a, b, (((0,), (0,)), ((), ())), preferred_element_type=F32)


def _dot_nn(a, b):
    return lax.dot_general(a, b, (((1,), (0,)), ((), ())), preferred_element_type=F32)


class _Attn:
    def __init__(self, H, dq, dk1, dv, causal, scale, hp, hp_kv, blk=256):
        self.H, self.dq, self.dk1, self.dv, self.causal, self.scale, self.blk = H, dq, dk1, dv, causal, scale, blk
        self.hp, self.hp_kv = hp, hp_kv


def _cols(ref, rows, hh, width):
    return ref[rows, hh * width:(hh + 1) * width]


def _keys(cfg, k1_ref, k2_ref, rows, hh):
    ks = _cols(k1_ref, rows, hh, cfg.dk1)
    if k2_ref is not None:
        ks = jnp.concatenate([ks, k2_ref[rows, :]], axis=1)
    return ks


def _attn_specs(cfg, hp, t, Tk, has_k2, by_q):
    g = cfg.H // hp
    if by_q:
        specs = [pl.BlockSpec((t, hp * cfg.dq), lambda h, i: (i, h)),
                 pl.BlockSpec((Tk, hp * cfg.dk1), lambda h, i: (0, h)),
                 pl.BlockSpec((Tk, hp * cfg.dv), lambda h, i: (0, g + h))]
        if has_k2:
            specs.append(pl.BlockSpec((Tk, LANES), lambda h, i: (0, 0)))
    else:
        specs = [None,
                 pl.BlockSpec((t, hp * cfg.dk1), lambda j, h: (j, h)),
                 pl.BlockSpec((t, hp * cfg.dv), lambda j, h: (j, g + h))]
        if has_k2:
            specs.append(pl.BlockSpec((t, LANES), lambda j, h: (j, 0)))
    return specs


def _mask(s, diagonal):
    if not diagonal:
        return s
    return jnp.where(lax.broadcasted_iota(jnp.int32, s.shape, 0) >= lax.broadcasted_iota(jnp.int32, s.shape, 1), s, NEG)


def flash_fwd(cfg, q, k1, v, k2, name):
    Tq, Tk = q.shape[0], k1.shape[0]
    t = min(cfg.blk, Tq, Tk)
    nkb = Tk // t
    has_k2 = k2 is not None
    hp = cfg.hp

    def body(*refs):
        q_ref, k1_ref, v_ref = refs[:3]
        k2_ref = refs[3] if has_k2 else None
        o_ref, lse_ref = refs[-2], refs[-1]
        i = pl.program_id(1)
        qs = [_cols(q_ref, slice(None), hh, cfg.dq) for hh in range(hp)]

        def step(j, carry, diagonal=False):
            rows = pl.ds(pl.multiple_of(j * t, t), t)
            out = []
            for hh in range(hp):
                m, l, acc = carry[hh]
                s = _mask(_dot_nt(qs[hh], _keys(cfg, k1_ref, k2_ref, rows, hh)) * cfg.scale, diagonal)
                m2 = jnp.maximum(m, jnp.max(s, axis=-1, keepdims=True))
                p = jnp.exp(s - m2)
                alpha = jnp.exp(m - m2)
                l2 = alpha * l + jnp.sum(p, axis=-1, keepdims=True)
                acc2 = alpha * acc + _dot_nn(p.astype(BF16), _cols(v_ref, rows, hh, cfg.dv))
                out.append((m2, l2, acc2))
            return tuple(out)

        init = tuple((jnp.full((t, 1), NEG, F32), jnp.zeros((t, 1), F32), jnp.zeros((t, cfg.dv), F32)) for _ in range(hp))
        res = lax.fori_loop(0, i if cfg.causal else nkb, step, init)
        if cfg.causal:
            res = step(i, res, True)
        for hh in range(hp):
            m, l, acc = res[hh]
            o_ref[:, hh * cfg.dv:(hh + 1) * cfg.dv] = (acc / l).astype(o_ref.dtype)
            lse_ref[hh] = m + jnp.log(l)

    args = [q, k1, v] + ([k2] if has_k2 else [])
    return pl.pallas_call(
        body, name=name, grid=(cfg.H // hp, Tq // t), in_specs=_attn_specs(cfg, hp, t, Tk, has_k2, True),
        out_specs=[pl.BlockSpec((t, hp * cfg.dv), lambda h, i: (i, h)), pl.BlockSpec((hp, t, 1), lambda h, i: (h, i, 0))],
        out_shape=[jax.ShapeDtypeStruct((Tq, cfg.H * cfg.dv), BF16), jax.ShapeDtypeStruct((cfg.H, Tq, 1), F32)],
        compiler_params=_cparams(("parallel", "parallel")),
    )(*args)


def flash_dq(cfg, q, k1, v, k2, o, do, lse, out_dtype, name):
    Tq, Tk = q.shape[0], k1.shape[0]
    t = min(cfg.blk, Tq, Tk)
    nkb = Tk // t
    has_k2 = k2 is not None
    hp = cfg.hp

    def body(*refs):
        q_ref, k1_ref, v_ref = refs[:3]
        k2_ref = refs[3] if has_k2 else None
        o_ref, do_ref, lse_ref, dq_ref, dl_ref = refs[-5:]
        i = pl.program_id(1)
        qs = [_cols(q_ref, slice(None), hh, cfg.dq) for hh in range(hp)]
        dos = [_cols(do_ref, slice(None), hh, cfg.dv) for hh in range(hp)]
        lses = [lse_ref[hh] for hh in range(hp)]
        deltas = []
        for hh in range(hp):
            d = jnp.sum(dos[hh].astype(F32) * _cols(o_ref, slice(None), hh, cfg.dv).astype(F32), axis=-1, keepdims=True)
            dl_ref[hh] = d
            deltas.append(d)

        def step(j, dqs, diagonal=False):
            rows = pl.ds(pl.multiple_of(j * t, t), t)
            out = []
            for hh in range(hp):
                ks = _keys(cfg, k1_ref, k2_ref, rows, hh)
                s = _mask(_dot_nt(qs[hh], ks) * cfg.scale, diagonal)
                p = jnp.exp(s - lses[hh])
                dp = _dot_nt(dos[hh], _cols(v_ref, rows, hh, cfg.dv))
                ds = p * (dp - deltas[hh]) * cfg.scale
                out.append(dqs[hh] + _dot_nn(ds.astype(BF16), ks))
            return tuple(out)

        dqs = lax.fori_loop(0, i if cfg.causal else nkb, step, tuple(jnp.zeros((t, cfg.dq), F32) for _ in range(hp)))
        if cfg.causal:
            dqs = step(i, dqs, True)
        for hh in range(hp):
            dq_ref[:, hh * cfg.dq:(hh + 1) * cfg.dq] = dqs[hh].astype(dq_ref.dtype)

    ov = pl.BlockSpec((t, hp * cfg.dv), lambda h, i: (i, h))
    row1 = pl.BlockSpec((hp, t, 1), lambda h, i: (h, i, 0))
    args = [q, k1, v] + ([k2] if has_k2 else []) + [o, do, lse]
    return pl.pallas_call(
        body, name=name, grid=(cfg.H // hp, Tq // t), in_specs=_attn_specs(cfg, hp, t, Tk, has_k2, True) + [ov, ov, row1],
        out_specs=[pl.BlockSpec((t, hp * cfg.dq), lambda h, i: (i, h)), row1],
        out_shape=[jax.ShapeDtypeStruct((Tq, cfg.H * cfg.dq), out_dtype), jax.ShapeDtypeStruct((cfg.H, Tq, 1), F32)],
        compiler_params=_cparams(("parallel", "parallel")),
    )(*args)


def flash_dkv(cfg, q, k1, v, k2, do, lse, delta, out_dtype, name):
    Tq, Tk = q.shape[0], k1.shape[0]
    t = min(cfg.blk, Tq, Tk)
    nqb = Tq // t
    has_k2 = k2 is not None
    hp = cfg.hp_kv

    def body(*refs):
        q_ref, k1_ref, v_ref = refs[:3]
        k2_ref = refs[3] if has_k2 else None
        n_in = 4 if has_k2 else 3
        do_ref, lse_ref, dl_ref = refs[n_in:n_in + 3]
        dk1_ref, dv_ref = refs[n_in + 3], refs[n_in + 4]
        j, h = pl.program_id(0), pl.program_id(1)
        kss = [_keys(cfg, k1_ref, k2_ref, slice(None), hh) for hh in range(hp)]
        vss = [_cols(v_ref, slice(None), hh, cfg.dv) for hh in range(hp)]

        def step(i, carry, diagonal=False):
            rows = pl.ds(pl.multiple_of(i * t, t), t)
            out = []
            for hh in range(hp):
                dk, dv = carry[hh]
                qi, doi = _cols(q_ref, rows, hh, cfg.dq), _cols(do_ref, rows, hh, cfg.dv)
                s = _dot_nt(kss[hh], qi) * cfg.scale
                if diagonal:
                    s = jnp.where(lax.broadcasted_iota(jnp.int32, s.shape, 0) <= lax.broadcasted_iota(jnp.int32, s.shape, 1), s, NEG)
                p = jnp.exp(s - lse_ref[hh, :, rows])
                dv = dv + _dot_nn(p.astype(BF16), doi)
                ds = p * (_dot_nt(vss[hh], doi) - dl_ref[hh, :, rows]) * cfg.scale
                dk = dk + _dot_nn(ds.astype(BF16), qi)
                out.append((dk, dv))
            return tuple(out)

        init = tuple((jnp.zeros((t, cfg.dq), F32), jnp.zeros((t, cfg.dv), F32)) for _ in range(hp))
        if cfg.causal:
            res = lax.fori_loop(j + 1, nqb, step, step(j, init, True))
        else:
            res = lax.fori_loop(0, nqb, step, init)
        for hh in range(hp):
            dk, dv = res[hh]
            dv_ref[:, hh * cfg.dv:(hh + 1) * cfg.dv] = dv.astype(dv_ref.dtype)
            dk1_ref[:, hh * cfg.dk1:(hh + 1) * cfg.dk1] = dk[:, 0:cfg.dk1].astype(dk1_ref.dtype)
        if has_k2:
            dk2_ref = refs[n_in + 5]

            @pl.when(h == 0)
            def _():
                dk2_ref[...] = jnp.zeros_like(dk2_ref)

            for hh in range(hp):
                dk2_ref[...] += res[hh][0][:, cfg.dk1:]

    specs = _attn_specs(cfg, hp, t, Tk, has_k2, False)
    specs[0] = pl.BlockSpec((Tq, hp * cfg.dq), lambda j, h: (0, h))
    rows_all = pl.BlockSpec((hp, 1, Tq), lambda j, h: (h, 0, 0))
    specs += [pl.BlockSpec((Tq, hp * cfg.dv), lambda j, h: (0, h)), rows_all, rows_all]
    args = [q, k1, v] + ([k2] if has_k2 else []) + [do, lse.reshape(cfg.H, 1, Tq), delta.reshape(cfg.H, 1, Tq)]
    out_specs = [pl.BlockSpec((t, hp * cfg.dk1), lambda j, h: (j, h)), pl.BlockSpec((t, hp * cfg.dv), lambda j, h: (j, h))]
    out_shape = [jax.ShapeDtypeStruct((Tk, cfg.H * cfg.dk1), out_dtype), jax.ShapeDtypeStruct((Tk, cfg.H * cfg.dv), out_dtype)]
    if has_k2:
        out_specs.append(pl.BlockSpec((t, LANES), lambda j, h: (j, 0)))
        out_shape.append(jax.ShapeDtypeStruct((Tk, LANES), F32))
    return pl.pallas_call(
        body, name=name, grid=(Tk // t, cfg.H // hp), in_specs=specs, out_specs=out_specs, out_shape=out_shape,
        compiler_params=_cparams(("parallel", "arbitrary")),
    )(*args)


def _shift_down(x, s):
    if s == 0:
        return x
    t = lax.broadcasted_iota(jnp.int32, x.shape, 0)
    return jnp.where(t >= s, pltpu.roll(x, s, 0), 0.0)


def _shift_up(x, s):
    if s == 0:
        return x
    n = x.shape[0]
    t = lax.broadcasted_iota(jnp.int32, x.shape, 0)
    return jnp.where(t < n - s, pltpu.roll(x, n - s, 0), 0.0)


def _conv(x, w_ref, kw):
    y = x * w_ref[kw - 1:kw, :]
    for j in range(kw - 1):
        y = y + _shift_down(x, kw - 1 - j) * w_ref[j:j + 1, :]
    return y


def _conv_t(d, w_ref, kw):
    y = d * w_ref[kw - 1:kw, :]
    for j in range(kw - 1):
        y = y + _shift_up(d, kw - 1 - j) * w_ref[j:j + 1, :]
    return y


def _conv_dw(d, x, kw):
    rows = lax.broadcasted_iota(jnp.int32, (kw, d.shape[1]), 0)
    dw = jnp.zeros((kw, d.shape[1]), F32)
    for j in range(kw):
        r = jnp.sum(d * _shift_down(x, kw - 1 - j), axis=0, keepdims=True)
        dw = jnp.where(rows == j, r, dw)
    return dw


def _silu(x):
    return x * jax.nn.sigmoid(x)


def _silu_grad(x):
    s = jax.nn.sigmoid(x)
    return s * (1.0 + x * (1.0 - s))


def gdn_conv_fwd(z, w, name, tc=256):
    T, C = z.shape[0], w.shape[1]
    kw = w.shape[0]

    def body(x_ref, w_ref, o_ref):
        o_ref[...] = _silu(_conv(x_ref[...], w_ref, kw))

    return pl.pallas_call(
        body, name=name, grid=(C // tc,),
        in_specs=[pl.BlockSpec((T, tc), lambda j: (0, j)), pl.BlockSpec((kw, tc), lambda j: (0, j))],
        out_specs=pl.BlockSpec((T, tc), lambda j: (0, j)),
        out_shape=jax.ShapeDtypeStruct((T, C), F32), compiler_params=_cparams(("parallel",)),
    )(z, w)


def gdn_conv_bwd(z, w, dy, name, tc=256):
    T, C = z.shape[0], w.shape[1]
    kw = w.shape[0]

    def body(x_ref, w_ref, dy_ref, dx_ref, dw_ref):
        xv = x_ref[...]
        dc = dy_ref[...] * _silu_grad(_conv(xv, w_ref, kw))
        dx_ref[...] = _conv_t(dc, w_ref, kw).astype(dx_ref.dtype)
        dw_ref[...] = _conv_dw(dc, xv, kw)

    col = lambda j: (0, j)
    return pl.pallas_call(
        body, name=name, grid=(C // tc,),
        in_specs=[pl.BlockSpec((T, tc), col), pl.BlockSpec((kw, tc), col), pl.BlockSpec((T, tc), col)],
        out_specs=[pl.BlockSpec((T, tc), col), pl.BlockSpec((kw, tc), col)],
        out_shape=[jax.ShapeDtypeStruct((T, C), BF16), jax.ShapeDtypeStruct((kw, C), F32)],
        compiler_params=_cparams(("parallel",)),
    )(z, w, dy)


def sc_fwd(z, w, name, tc=256):
    T, C = z.shape[0], w.shape[1]
    kw, nb = w.shape[0], C // tc

    def body(b_ref, c_ref, u_ref, w_ref, o_ref):
        o_ref[...] = (b_ref[...] * _conv(c_ref[...] * u_ref[...], w_ref, kw)).astype(o_ref.dtype)

    return pl.pallas_call(
        body, name=name, grid=(nb,),
        in_specs=[pl.BlockSpec((T, tc), lambda j: (0, j)), pl.BlockSpec((T, tc), lambda j: (0, nb + j)),
                  pl.BlockSpec((T, tc), lambda j: (0, 2 * nb + j)), pl.BlockSpec((kw, tc), lambda j: (0, j))],
        out_specs=pl.BlockSpec((T, tc), lambda j: (0, j)),
        out_shape=jax.ShapeDtypeStruct((T, C), BF16), compiler_params=_cparams(("parallel",)),
    )(z, z, z, w)


def sc_bwd(z, w, dy, name, tc=256):
    T, C = z.shape[0], w.shape[1]
    kw, nb = w.shape[0], C // tc

    def body(b_ref, c_ref, u_ref, w_ref, dy_ref, db_ref, dc_ref, du_ref, dw_ref):
        cv, uv, dyv = c_ref[...], u_ref[...], dy_ref[...]
        cu = cv * uv
        db_ref[...] = (dyv * _conv(cu, w_ref, kw)).astype(db_ref.dtype)
        dcv = dyv * b_ref[...]
        dcu = _conv_t(dcv, w_ref, kw)
        dc_ref[...] = (dcu * uv).astype(dc_ref.dtype)
        du_ref[...] = (dcu * cv).astype(du_ref.dtype)
        dw_ref[...] = _conv_dw(dcv, cu, kw)

    col = lambda j: (0, j)
    act = jax.ShapeDtypeStruct((T, C), BF16)
    return pl.pallas_call(
        body, name=name, grid=(nb,),
        in_specs=[pl.BlockSpec((T, tc), col), pl.BlockSpec((T, tc), lambda j: (0, nb + j)),
                  pl.BlockSpec((T, tc), lambda j: (0, 2 * nb + j)), pl.BlockSpec((kw, tc), col), pl.BlockSpec((T, tc), col)],
        out_specs=[pl.BlockSpec((T, tc), col)] * 3 + [pl.BlockSpec((kw, tc), col)],
        out_shape=[act, act, act, jax.ShapeDtypeStruct((kw, C), F32)],
        compiler_params=_cparams(("parallel",)),
    )(z, z, z, w, dy)


def _hdot(a, b, dims):
    a_hi, b_hi = a.astype(BF16), b.astype(BF16)
    a_lo, b_lo = (a - a_hi.astype(F32)).astype(BF16), (b - b_hi.astype(F32)).astype(BF16)
    dot = lambda x, y: lax.dot_general(x, y, (dims, ((), ())), preferred_element_type=F32)
    return dot(a_hi, b_hi) + (dot(a_hi, b_lo) + dot(a_lo, b_hi))


def _bdot(a, b, dims):
    return lax.dot_general(a.astype(BF16), b.astype(BF16), (dims, ((), ())), preferred_element_type=F32)


_NN, _NT, _TN = ((1,), (0,)), ((1,), (1,)), ((0,), (0,))


def _per_head_dots(dot2d):
    def stacked(a, b, dims):
        return jnp.stack([dot2d(a[h], b[h], dims) for h in range(a.shape[0])])

    @jax.custom_vjp
    def nn(a, b):
        return stacked(a, b, _NN)

    @jax.custom_vjp
    def nt(a, b):
        return stacked(a, b, _NT)

    @jax.custom_vjp
    def tn(a, b):
        return stacked(a, b, _TN)

    nn.defvjp(lambda a, b: (nn(a, b), (a, b)), lambda r, d: (stacked(d, r[1], _NT), stacked(r[0], d, _TN)))
    nt.defvjp(lambda a, b: (nt(a, b), (a, b)), lambda r, d: (stacked(d, r[1], _NN), stacked(d, r[0], _TN)))
    tn.defvjp(lambda a, b: (tn(a, b), (a, b)), lambda r, d: (stacked(r[1], d, _NT), stacked(r[0], d, _NN)))
    return nn, nt, tn


_hnn, _hnt, _htn = _per_head_dots(_hdot)
_bnn, _bnt, _btn = _per_head_dots(_bdot)


@jax.custom_vjp
def _unit_lower_inverse(m):
    c = m.shape[-1]
    eye = (lax.broadcasted_iota(jnp.int32, (c, c), 0) == lax.broadcasted_iota(jnp.int32, (c, c), 1)).astype(F32)
    t = eye - m
    p = _hnn(m, m)
    n = 2
    while n < c:
        t = t + _hnn(t, p)
        n *= 2
        if n < c:
            p = _hnn(p, p)
    return t


def _uli_fwd(m):
    t = _unit_lower_inverse(m)
    return t, t


def _uli_bwd(t, dt):
    return (-_htn(t, _hnt(dt, t)),)


_unit_lower_inverse.defvjp(_uli_fwd, _uli_bwd)


@jax.custom_vjp
def _known_inverse(m, t):
    return t


_known_inverse.defvjp(lambda m, t: (t, t), lambda t, dt: (_uli_bwd(t, dt)[0], jnp.zeros_like(t)))


def _gdn_chunk(q, k, v, gate, bl, al, a_log, dt_bias, o_norm, st, t_known=None):
    nh, c = q.shape[0], q.shape[1]
    ii = lax.broadcasted_iota(jnp.int32, (c, c), 0)
    jj = lax.broadcasted_iota(jnp.int32, (c, c), 1)
    tri, strict = ii >= jj, ii > jj
    q = q * lax.rsqrt(jnp.sum(q * q, -1, keepdims=True) + EPS) * (GDN_D ** -0.5)
    k = k * lax.rsqrt(jnp.sum(k * k, -1, keepdims=True) + EPS)
    beta = jax.nn.sigmoid(bl)
    g = -jnp.exp(a_log) * jax.nn.softplus(al + dt_bias)
    gc = _hnn(jnp.broadcast_to(tri.astype(F32), (nh, c, c)), g)
    gcol = _hnn(gc, jnp.full((nh, LANES, c), 1.0 / LANES, F32))
    grow = _hnt(jnp.full((nh, c, LANES), 1.0 / LANES, F32), gc)
    decay = jnp.where(tri, jnp.exp(jnp.where(tri, gcol - grow, 0.0)), 0.0)
    kb = k * beta
    m = jnp.where(strict, _bnt(kb, k) * decay, 0.0)
    t_inv = _unit_lower_inverse(m) if t_known is None else _known_inverse(m, t_known)
    eg = jnp.exp(gc)
    u = _bnn(t_inv, v * beta)
    w = _bnn(t_inv, kb * eg)
    attn = _bnt(q, k) * decay
    v_new = u - _bnn(w, st)
    o = _bnn(q * eg, st) + _bnn(attn, v_new)
    g_last = jnp.sum(g, axis=1, keepdims=True)
    st_new = st * jnp.exp(g_last) + _btn(k * jnp.exp(g_last - gc), v_new)
    o = o * lax.rsqrt(jnp.mean(o * o, -1, keepdims=True) + EPS) * o_norm
    return o * _silu(gate), st_new, t_inv


GDN_HP = 8
_GW = GDN_HP * GDN_D
_GB = GDN_H // GDN_HP


def _gdn_specs(n_chunks, rev):
    def tok(col):
        if rev:
            return pl.BlockSpec((GDN_C, _GW), lambda h, n: (n_chunks - 1 - n, col + h))
        return pl.BlockSpec((GDN_C, _GW), lambda h, n: (n, col + h))
    par = pl.BlockSpec((1, _GW), lambda h, n: (0, h))
    shared = pl.BlockSpec((1, GDN_D), lambda h, n: (0, 0))
    if rev:
        st = pl.BlockSpec((GDN_HP, None, GDN_D, GDN_D), lambda h, n: (h, n_chunks - 1 - n, 0, 0))
    else:
        st = pl.BlockSpec((GDN_HP, None, GDN_D, GDN_D), lambda h, n: (h, n, 0, 0))
    return tok, par, shared, st


def _heads(ref):
    return jnp.stack([ref[:, h * GDN_D:(h + 1) * GDN_D] for h in range(ref.shape[1] // GDN_D)])


def gdn_chunk_fwd(qkv, z, a_log_x, dt_bias_x, o_norm, name):
    T = qkv.shape[0]
    n_chunks = T // GDN_C
    H = GDN_H
    tok, par, shared, st_spec = _gdn_specs(n_chunks, False)

    def body(q_ref, k_ref, v_ref, g_ref, bl_ref, al_ref, a_ref, dt_ref, on_ref, o_ref, st_ref, ti_ref, state):
        @pl.when(pl.program_id(1) == 0)
        def _():
            state[...] = jnp.zeros_like(state)

        st = state[...]
        st_ref[...] = st
        o, st_new, t_inv = _gdn_chunk(_heads(q_ref), _heads(k_ref), _heads(v_ref), _heads(g_ref), _heads(bl_ref), _heads(al_ref),
                                      _heads(a_ref), _heads(dt_ref), on_ref[...], st)
        for hh in range(GDN_HP):
            o_ref[:, hh * GDN_D:(hh + 1) * GDN_D] = o[hh].astype(o_ref.dtype)
        ti_ref[...] = t_inv
        state[...] = st_new

    B = _GB
    return pl.pallas_call(
        body, name=name, grid=(B, n_chunks),
        in_specs=[tok(0), tok(B), tok(2 * B), tok(3 * B), tok(4 * B), tok(5 * B), par, par, shared],
        out_specs=[tok(0), st_spec, pl.BlockSpec((GDN_HP, None, GDN_C, GDN_C), lambda h, n: (h, n, 0, 0))],
        out_shape=[jax.ShapeDtypeStruct((T, H * GDN_D), BF16), jax.ShapeDtypeStruct((H, n_chunks, GDN_D, GDN_D), F32),
                   jax.ShapeDtypeStruct((H, n_chunks, GDN_C, GDN_C), F32)],
        scratch_shapes=[pltpu.VMEM((GDN_HP, GDN_D, GDN_D), F32)],
        compiler_params=_cparams(("parallel", "arbitrary")),
    )(qkv, qkv, qkv, z, z, z, a_log_x, dt_bias_x, o_norm)


def gdn_chunk_bwd(qkv, z, a_log_x, dt_bias_x, o_norm, states, t_invs, do, name):
    T = qkv.shape[0]
    n_chunks = T // GDN_C
    H = GDN_H
    tok, par, shared, st_spec = _gdn_specs(n_chunks, True)

    def body(q_ref, k_ref, v_ref, g_ref, bl_ref, al_ref, a_ref, dt_ref, on_ref, st_ref, ti_ref, do_ref,
             dq_ref, dk_ref, dv_ref, dg_ref, dbl_ref, dal_ref, da_ref, ddt_ref, don_ref, dstate):
        h, n = pl.program_id(0), pl.program_id(1)

        @pl.when(n == 0)
        def _():
            dstate[...] = jnp.zeros_like(dstate)
            da_ref[...] = jnp.zeros_like(da_ref)
            ddt_ref[...] = jnp.zeros_like(ddt_ref)

        @pl.when((n == 0) & (h == 0))
        def _():
            don_ref[...] = jnp.zeros_like(don_ref)

        t_known = ti_ref[...]
        _, vjp = jax.vjp(lambda *ins: _gdn_chunk(*ins, t_known=t_known)[:2],
                         _heads(q_ref), _heads(k_ref), _heads(v_ref), _heads(g_ref), _heads(bl_ref), _heads(al_ref),
                         _heads(a_ref), _heads(dt_ref), on_ref[...], st_ref[...])
        dq, dk, dv, dg, dbl, dal, da, ddt, don, dst = vjp((_heads(do_ref).astype(F32), dstate[...]))
        for hh in range(GDN_HP):
            cols = slice(hh * GDN_D, (hh + 1) * GDN_D)
            dq_ref[:, cols] = dq[hh]
            dk_ref[:, cols] = dk[hh]
            dv_ref[:, cols] = dv[hh]
            dg_ref[:, cols] = dg[hh].astype(dg_ref.dtype)
            dbl_ref[:, cols] = dbl[hh].astype(dbl_ref.dtype)
            dal_ref[:, cols] = dal[hh].astype(dal_ref.dtype)
            da_ref[:, cols] += da[hh]
            ddt_ref[:, cols] += ddt[hh]
        don_ref[...] += don
        dstate[...] = dst

    tok0 = tok(0)
    B = _GB
    f32_tok = jax.ShapeDtypeStruct((T, H * GDN_D), F32)
    bf_tok = jax.ShapeDtypeStruct((T, H * GDN_D), BF16)
    par_sh = jax.ShapeDtypeStruct((1, H * GDN_D), F32)
    return pl.pallas_call(
        body, name=name, grid=(B, n_chunks),
        in_specs=[tok(0), tok(B), tok(2 * B), tok(3 * B), tok(4 * B), tok(5 * B), par, par, shared, st_spec,
                  pl.BlockSpec((GDN_HP, None, GDN_C, GDN_C), lambda h, n: (h, n_chunks - 1 - n, 0, 0)), tok0],
        out_specs=[tok0] * 6 + [par, par, shared],
        out_shape=[f32_tok, f32_tok, f32_tok, bf_tok, bf_tok, bf_tok, par_sh, par_sh, jax.ShapeDtypeStruct((1, GDN_D), F32)],
        scratch_shapes=[pltpu.VMEM((GDN_HP, GDN_D, GDN_D), F32)],
        compiler_params=_cparams(("arbitrary", "arbitrary")),
    )(qkv, qkv, qkv, z, z, z, a_log_x, dt_bias_x, o_norm, states, t_invs, do)


def _prefetch_call(body, name, grid, in_specs, out_specs, out_shape, aliases=None):
    return pl.pallas_call(
        body, name=name,
        grid_spec=pltpu.PrefetchScalarGridSpec(num_scalar_prefetch=1, grid=grid, in_specs=in_specs, out_specs=out_specs),
        out_shape=out_shape, input_output_aliases=aliases or {},
        compiler_params=_cparams(("parallel",) * len(grid)))


def cast_into(src, src_row0, rows, slab, row0, me, name):
    width = src.shape[1]
    tr = _pick_rows(rows, 1024, row0, src_row0)
    assert rows % tr == 0 and row0 % tr == 0 and src_row0 % tr == 0

    def body(me_ref, s_ref, *refs):
        refs[-1][...] = s_ref[...].astype(refs[-1].dtype)

    in_specs = [pl.BlockSpec((tr, width), lambda r, me_ref: (src_row0 // tr + r, 0))]
    args = [src]
    aliases = {}
    if slab.arr is not None:
        in_specs.append(_ANY)
        args.append(slab.arr)
        aliases = {2: 0}
    slab.arr = _prefetch_call(
        body, name, (rows // tr,), in_specs,
        pl.BlockSpec((None, tr, width), lambda r, me_ref: (me_ref[0], row0 // tr + r, 0)),
        jax.ShapeDtypeStruct(slab.shape, slab.dtype), aliases)(me, *args)


def pair_add(g, b, c_idx, name):
    n, rh, w = b.shape
    tr = _pick_rows(rh, 1024)
    nb = rh // tr

    def body(c_ref, g_ref, b_ref, o_ref):
        o_ref[...] = (g_ref[...].astype(F32) + b_ref[...].astype(F32)).astype(o_ref.dtype)

    return _prefetch_call(
        body, name, (n, nb),
        [pl.BlockSpec((None, tr, w), lambda k, r, c: (k, c[0] * nb + r, 0)), pl.BlockSpec((None, tr, w), lambda k, r, c: (k, r, 0))],
        pl.BlockSpec((None, tr, w), lambda k, r, c: (k, r, 0)), jax.ShapeDtypeStruct(b.shape, BF16))(c_idx, g, b)


def chip_sum(p, rv, mc, name):
    n, rh, w = p.shape
    tr = _pick_rows(rh, 512)
    nb = rh // tr

    def body(mc_ref, p_ref, rv_ref, o_ref):
        me = mc_ref[0]
        acc = None
        for k in range(n):
            part = jnp.where(me == k, p_ref[...], rv_ref[k]).astype(F32)
            acc = part if acc is None else acc + part
        o_ref[...] = acc

    return _prefetch_call(
        body, name, (nb,),
        [pl.BlockSpec((None, tr, w), lambda r, mc_ref: (mc_ref[0], r, 0)), pl.BlockSpec((n, tr, w), lambda r, mc_ref: (0, r, 0))],
        pl.BlockSpec((tr, w), lambda r, mc_ref: (mc_ref[1] * nb + r, 0)), jax.ShapeDtypeStruct((2 * rh, w), F32))(mc, p, rv)


def adamw(red, row0, w, m, v, w_row0, rows, prev, name):
    cols = w.shape[1]
    tr = _pick_rows(rows, 512, row0, w_row0)
    assert rows % tr == 0 and row0 % tr == 0 and w_row0 % tr == 0

    def body(g_ref, w_ref, m_ref, v_ref, *refs):
        go_ref, d_ref, nm_ref, nv_ref = refs[-4:]
        gv = g_ref[...]
        nm = ADAM_B1 * m_ref[...] + (1.0 - ADAM_B1) * gv
        nv = ADAM_B2 * v_ref[...] + (1.0 - ADAM_B2) * (gv * gv)
        m_hat = nm / (1.0 - ADAM_B1 ** ADAM_STEP)
        v_hat = nv / (1.0 - ADAM_B2 ** ADAM_STEP)
        go_ref[...] = gv
        d_ref[...] = -ADAM_LR * (m_hat / (jnp.sqrt(v_hat) + ADAM_EPS) + ADAM_WD * w_ref[...])
        nm_ref[...] = nm
        nv_ref[...] = nv

    spec = pl.BlockSpec((tr, cols), lambda r: (w_row0 // tr + r, 0))
    sh = jax.ShapeDtypeStruct(w.shape, F32)
    in_specs = [pl.BlockSpec((tr, cols), lambda r: (row0 // tr + r, 0)), spec, spec, spec]
    args, aliases = [red, w, m, v], {}
    if prev is not None:
        in_specs += [_ANY] * 4
        args += list(prev)
        aliases = {4 + k: k for k in range(4)}
    return pl.pallas_call(
        body, name=name, grid=(rows // tr,), in_specs=in_specs, out_specs=[spec] * 4, out_shape=[sh] * 4,
        input_output_aliases=aliases, compiler_params=_cparams(("parallel",)),
    )(*args)


def _place():
    x, y, c = lax.axis_index("x"), lax.axis_index("y"), lax.axis_index("c")
    chips = [(1 - x, y), (x, 1 - y), (1 - x, 1 - y)]
    return x, y, c, chips


def _chip_index(cx, cy):
    return 2 * cx + cy


def _remote(src, dst, send_sem, recv_sem, to):
    return pltpu.make_async_remote_copy(src_ref=src, dst_ref=dst, send_sem=send_sem, recv_sem=recv_sem,
                                        device_id=to, device_id_type=MESH)


def _comm_call(body, name, ins, out_shapes, n_sems, aliases):
    return pl.pallas_call(
        body, name=name, in_specs=[_ANY] * len(ins), out_specs=[_ANY] * len(out_shapes), out_shape=out_shapes,
        scratch_shapes=[pltpu.SemaphoreType.DMA((n_sems,)), pltpu.SemaphoreType.DMA((n_sems,))],
        input_output_aliases=aliases,
    )(*ins)


def gather_slabs(slabs, name="weight_all_gather"):
    n = len(slabs)

    def body(*refs):
        in_refs, out_refs, send_sems, recv_sems = refs[:n], refs[n:2 * n], refs[-2], refs[-1]
        x, y, c, chips = _place()
        me = _chip_index(x, y)
        sib = (x, y, 1 - c)
        first, passed = [], []
        for a in range(n):
            rh = in_refs[a].shape[1] // 2
            mine = pl.ds(c * rh, rh)
            for j, chip in enumerate(chips):
                cp = _remote(in_refs[a].at[me, mine], out_refs[a].at[me, mine], send_sems.at[6 * a + j],
                             recv_sems.at[6 * a + j], (*chip, c))
                cp.start()
                first.append(cp)
        for a in range(n):
            rh = in_refs[a].shape[1] // 2
            mine = pl.ds(c * rh, rh)
            for j, chip in enumerate(chips):
                landed = out_refs[a].at[_chip_index(*chip), mine]
                _remote(landed, landed, send_sems.at[6 * a + j], recv_sems.at[6 * a + j], (*chip, c)).wait_recv()
                cp = _remote(landed, landed, send_sems.at[6 * a + 3 + j], recv_sems.at[6 * a + 3 + j], sib)
                cp.start()
                passed.append(cp)
        for a in range(n):
            rh = in_refs[a].shape[1] // 2
            theirs = pl.ds((1 - c) * rh, rh)
            for j, chip in enumerate(chips):
                got = out_refs[a].at[_chip_index(*chip), theirs]
                _remote(got, got, send_sems.at[6 * a + 3 + j], recv_sems.at[6 * a + 3 + j], sib).wait_recv()
        for cp in first + passed:
            cp.wait_send()

    return _comm_call(body, name, slabs, [jax.ShapeDtypeStruct(s.shape, s.dtype) for s in slabs], 6 * n,
                      {a: a for a in range(n)})


def pair_swap_halves(slabs, name="grad_pair_swap"):
    n = len(slabs)

    def body(*refs):
        in_refs, out_refs, send_sems, recv_sems = refs[:n], refs[n:2 * n], refs[-2], refs[-1]
        x, y, c, _ = _place()
        cps = []
        for a in range(n):
            rh = in_refs[a].shape[1] // 2
            cp = _remote(in_refs[a].at[:, pl.ds((1 - c) * rh, rh), :], out_refs[a], send_sems.at[a], recv_sems.at[a], (x, y, 1 - c))
            cp.start()
            cps.append(cp)
        for cp in cps:
            cp.wait()

    outs = [jax.ShapeDtypeStruct((s.shape[0], s.shape[1] // 2, s.shape[2]), s.dtype) for s in slabs]
    return _comm_call(body, name, slabs, outs, n, {})


def chip_exchange(parts, name="grad_chip_exchange"):
    n = len(parts)

    def body(*refs):
        in_refs, out_refs, send_sems, recv_sems = refs[:n], refs[n:2 * n], refs[-2], refs[-1]
        x, y, c, chips = _place()
        me = _chip_index(x, y)
        sends = []
        for a in range(n):
            for j, chip in enumerate(chips):
                cp = _remote(in_refs[a].at[_chip_index(*chip)], out_refs[a].at[me], send_sems.at[3 * a + j],
                             recv_sems.at[3 * a + j], (*chip, c))
                cp.start()
                sends.append(cp)
        for a in range(n):
            for j, chip in enumerate(chips):
                got = out_refs[a].at[_chip_index(*chip)]
                _remote(got, got, send_sems.at[3 * a + j], recv_sems.at[3 * a + j], (*chip, c)).wait_recv()
        for cp in sends:
            cp.wait_send()

    return _comm_call(body, name, parts, [jax.ShapeDtypeStruct(p.shape, p.dtype) for p in parts], 3 * n, {})


def pair_join_halves(reds, name="grad_pair_join"):
    n = len(reds)

    def body(*refs):
        in_refs, out_refs, send_sems, recv_sems = refs[:n], refs[n:2 * n], refs[-2], refs[-1]
        x, y, c, _ = _place()
        cps = []
        for a in range(n):
            rh = in_refs[a].shape[0] // 2
            mine = pl.ds(c * rh, rh)
            cp = _remote(in_refs[a].at[mine], out_refs[a].at[mine], send_sems.at[a], recv_sems.at[a], (x, y, 1 - c))
            cp.start()
            cps.append(cp)
        for a in range(n):
            rh = in_refs[a].shape[0] // 2
            got = out_refs[a].at[pl.ds((1 - c) * rh, rh)]
            _remote(got, got, send_sems.at[a], recv_sems.at[a], (x, y, 1 - c)).wait_recv()
        for cp in cps:
            cp.wait_send()

    return _comm_call(body, name, reds, [jax.ShapeDtypeStruct(r.shape, r.dtype) for r in reds], n, {a: a for a in range(n)})


_HBM = pl.BlockSpec(memory_space=pltpu.HBM)
_SEM = pl.BlockSpec(memory_space=pltpu.SEMAPHORE)
_EFFECT = pltpu.SideEffectType.DATAFLOW_SIDE_EFFECTING


def _in_hbm(a):
    return pltpu.with_memory_space_constraint(a, pltpu.HBM)


def _hbm_like(a):
    return pltpu.HBM(a.shape, a.dtype)


def _start_call(body, name, ins, n_sems, after):
    n = len(ins)
    res = pl.pallas_call(
        body, name=name, in_specs=[_HBM] * n + [_ANY],
        out_specs=[_SEM, _SEM] + [_HBM] * n + [pl.BlockSpec(memory_space=pltpu.VMEM)],
        out_shape=[pltpu.SemaphoreType.DMA((n_sems,)), pltpu.SemaphoreType.DMA((n_sems,))] + [_hbm_like(a) for a in ins]
        + [jax.ShapeDtypeStruct((8, LANES), F32)],
        input_output_aliases={a: 2 + a for a in range(n)},
        compiler_params=pltpu.CompilerParams(has_side_effects=_EFFECT),
    )(*[_in_hbm(a) for a in ins], after)
    return res[0], res[1], list(res[2:2 + n]), res[-1]


def _wait_call(body, name, thru, send_sems, recv_sems, after):
    n = len(thru)
    after = list(after) if isinstance(after, (list, tuple)) else [after]
    return pl.pallas_call(
        body, name=name, in_specs=[_HBM] * n + [_SEM, _SEM] + [_ANY] * len(after), out_specs=[_HBM] * n,
        out_shape=[_hbm_like(a) for a in thru], input_output_aliases={a: a for a in range(n)},
        compiler_params=pltpu.CompilerParams(has_side_effects=_EFFECT),
    )(*thru, send_sems, recv_sems, *after)


def gather_start(slabs, after, name="weight_gather_start"):
    n = len(slabs)

    def body(*refs):
        g_refs, send_sems, recv_sems, token = refs[:n], refs[n + 1], refs[n + 2], refs[-1]
        x, y, c, chips = _place()
        me = _chip_index(x, y)
        for a in range(n):
            rh = g_refs[a].shape[1] // 2
            mine = g_refs[a].at[me, pl.ds(c * rh, rh)]
            for j, chip in enumerate(chips):
                _remote(mine, mine, send_sems.at[3 * a + j], recv_sems.at[3 * a + j], (*chip, c)).start()
        token[...] = jnp.zeros_like(token)

    return _start_call(body, name, slabs, 3 * n, after)


def gather_wait(send_sems, recv_sems, thru, after, name="weight_gather_wait"):
    n = len(thru)

    def body(*refs):
        g_refs, send_sems, recv_sems = refs[:n], refs[n], refs[n + 1]
        x, y, c, chips = _place()
        me = _chip_index(x, y)
        for a in range(n):
            rh = g_refs[a].shape[1] // 2
            rows = pl.ds(c * rh, rh)
            for j, chip in enumerate(chips):
                mine, got = g_refs[a].at[me, rows], g_refs[a].at[_chip_index(*chip), rows]
                _remote(mine, mine, send_sems.at[3 * a + j], recv_sems.at[3 * a + j], (*chip, c)).wait_send()
                _remote(got, got, send_sems.at[3 * a + j], recv_sems.at[3 * a + j], (*chip, c)).wait_recv()

    return _wait_call(body, name, thru, send_sems, recv_sems, after)


def gather_forward(slabs, name="weight_gather_forward"):
    n = len(slabs)

    def body(*refs):
        in_refs, out_refs, send_sems, recv_sems = refs[:n], refs[n:2 * n], refs[-2], refs[-1]
        x, y, c, chips = _place()
        sib = (x, y, 1 - c)
        sends = []
        for a in range(n):
            rh = in_refs[a].shape[1] // 2
            for j, chip in enumerate(chips):
                k = _chip_index(*chip)
                cp = _remote(in_refs[a].at[k, pl.ds(c * rh, rh)], out_refs[a].at[k, pl.ds(c * rh, rh)], send_sems.at[3 * a + j],
                             recv_sems.at[3 * a + j], sib)
                cp.start()
                sends.append(cp)
        for a in range(n):
            rh = in_refs[a].shape[1] // 2
            for j, chip in enumerate(chips):
                got = out_refs[a].at[_chip_index(*chip), pl.ds((1 - c) * rh, rh)]
                _remote(got, got, send_sems.at[3 * a + j], recv_sems.at[3 * a + j], sib).wait_recv()
        for cp in sends:
            cp.wait_send()

    return _comm_call(body, name, slabs, [jax.ShapeDtypeStruct(s.shape, s.dtype) for s in slabs], 3 * n, {a: a for a in range(n)})


def exchange_start(parts, after, name="grad_exchange_start"):
    n = len(parts)

    def body(*refs):
        p_refs, land_refs, send_sems, recv_sems, token = refs[:n], refs[n:2 * n], refs[2 * n + 1], refs[2 * n + 2], refs[-1]
        x, y, c, chips = _place()
        me = _chip_index(x, y)
        for a in range(n):
            for j, chip in enumerate(chips):
                _remote(p_refs[a].at[_chip_index(*chip)], land_refs[a].at[me], send_sems.at[3 * a + j], recv_sems.at[3 * a + j],
                        (*chip, c)).start()
        token[...] = jnp.zeros_like(token)

    return _start_call(body, name, list(parts) + [lax.empty(p.shape, p.dtype) for p in parts], 3 * n, after)


def exchange_wait(send_sems, recv_sems, thru, after, name="grad_exchange_wait"):
    n = len(thru) // 2

    def body(*refs):
        p_refs, land_refs, send_sems, recv_sems = refs[:n], refs[n:2 * n], refs[2 * n], refs[2 * n + 1]
        x, y, c, chips = _place()
        me = _chip_index(x, y)
        for a in range(n):
            for j, chip in enumerate(chips):
                k = _chip_index(*chip)
                _remote(p_refs[a].at[k], land_refs[a].at[me], send_sems.at[3 * a + j], recv_sems.at[3 * a + j], (*chip, c)).wait_send()
                _remote(land_refs[a].at[k], land_refs[a].at[k], send_sems.at[3 * a + j], recv_sems.at[3 * a + j], (*chip, c)).wait_recv()

    res = _wait_call(body, name, thru, send_sems, recv_sems, after)
    return res[:n], res[n:]


_SLABS = {
    "mla_w_in": [("mla_w_in", 1, 0, 2)], "mla_w_uq": [("mla_w_uq", 2, 0, 2)], "mla_w_ukv": [("mla_w_ukv", 2, 0, 2)],
    "l0_mla_w_o": [("mla_w_o", 1, 0, 1)],
    "l0_w1024": [("mlp_w1", 2, 0, 1), ("mlp_w2", 1, 0, 1), ("xa_w_q", 1, 0, 1), ("xa_w_o", 1, 0, 1)],
    "l0_xa_w_kv": [("xa_w_kv", 2, 0, 1)],
    "l1_w1024": [("mlp_w1", 2, 1, 2), ("mlp_w2", 1, 1, 2), ("xa_w_q", 1, 1, 2), ("xa_w_o", 1, 1, 2), ("gdn_w_o", 1, 0, 1)],
    "l1_xa_w_kv": [("xa_w_kv", 2, 1, 2)], "gdn_w_in": [("gdn_w_in", 2, 0, 1)],
    "l23_w1024": [("mlp_w1", 2, 2, 4), ("mlp_w2", 1, 2, 4), ("xa_w_q", 1, 2, 4), ("xa_w_o", 1, 2, 4), ("mla_w_o", 1, 1, 2),
                  ("sc_w_o", 1, 0, 1)],
    "l23_xa_w_kv": [("xa_w_kv", 2, 2, 4)], "sc_w_in": [("sc_w_in", 2, 0, 1)],
}
_GROUPS = [(["mla_w_in", "mla_w_uq", "mla_w_ukv", "l0_mla_w_o"], None),
           (["l0_w1024", "l0_xa_w_kv"], (0, "xa")),
           (["l1_w1024", "l1_xa_w_kv", "gdn_w_in"], (1, "mix")),
           (["l23_w1024", "l23_xa_w_kv", "sc_w_in"], (2, "mix"))]
_RELAID = ("mla_w_in", "mla_w_uq", "mla_w_ukv", "gdn_w_in")
_SMALL = [("mla_q_norm", 1), ("mla_kv_norm", 1), ("gdn_conv_w", 2), ("sc_conv_w", 2)]
_REPL = ["gdn_a_log", "gdn_dt_bias", "gdn_o_norm", "norm_mix", "norm_mem", "norm_mlp", "mem_norm", "final_norm"]
_WEIGHTS = ['mla_w_in', 'mla_q_norm', 'mla_kv_norm', 'mla_w_uq', 'mla_w_ukv', 'mla_w_o', 'gdn_w_in', 'gdn_conv_w',
            'gdn_a_log', 'gdn_dt_bias', 'gdn_o_norm', 'gdn_w_o', 'sc_w_in', 'sc_conv_w', 'sc_w_o', 'norm_mix',
            'norm_mem', 'norm_mlp', 'xa_w_q', 'xa_w_kv', 'xa_w_o', 'mlp_w1', 'mlp_w2', 'mem_norm', 'final_norm']


class Layout:
    def __init__(self, shard_shapes):
        self.members, self.where, self.slab_dims = {}, {}, {}
        for slab, members in _SLABS.items():
            off, rows = 0, []
            for name, axis, l0, l1 in members:
                _, rpl, width = shard_shapes[name]
                rows.append((name, off, l0, l1, rpl))
                for layer in range(l0, l1):
                    self.where[(name, layer)] = (slab, off + (layer - l0) * rpl, rpl, width, axis)
                off += (l1 - l0) * rpl
            self.members[slab], self.slab_dims[slab] = rows, (off, width)

    def new_slabs(self, dtype):
        return {s: Slab(rows, width, dtype) for s, (rows, width) in self.slab_dims.items()}

    def loc(self, slabs, name, layer):
        slab, row0, rpl, width, axis = self.where[(name, layer)]
        if axis == 1:
            return Loc(slabs[slab], row0, N_CHIPS * rpl, width, 0)
        return Loc(slabs[slab], row0, rpl, N_CHIPS * width, 1)

    def _whole(self, name):
        (member,) = self.members[name]
        _, off, l0, l1, rpl = member
        assert off == 0 and l0 == 0
        return l1, rpl, self.slab_dims[name][1], dict((n, a) for n, a, _, _ in _SLABS[name])[name]

    def full(self, slabs, name):
        layers, rpl, width, axis = self._whole(name)
        blocks = slabs[name].arr.reshape(N_CHIPS, layers, rpl, width)
        return jnp.concatenate([blocks[s] for s in range(N_CHIPS)], axis=axis)

    def put_full(self, slabs, name, grad):
        layers, rpl, width, axis = self._whole(name)
        parts = jnp.stack(jnp.split(grad, N_CHIPS, axis=axis)).reshape(N_CHIPS, layers * rpl, width)
        slabs[name].arr = parts.astype(slabs[name].dtype)


def _small_pack(vals, names):
    flat = jnp.concatenate([vals[n].astype(F32).reshape(-1) for n in names])
    return jnp.pad(flat, (0, SMALL_ROWS * SMALL_COLS - flat.shape[0])).reshape(SMALL_ROWS, SMALL_COLS)


def _small_unpack(flat, like, names):
    out, off = {}, 0
    flat = flat.reshape(-1)
    for n in names:
        out[n] = flat[off:off + like[n].size].reshape(like[n].shape)
        off += like[n].size
    return out


_MLA_CFG = _Attn(MLA_H, 2 * LANES, MLA_NOPE, MLA_V, True, (MLA_NOPE + MLA_ROPE) ** -0.5, hp=8, hp_kv=4)
_XA_CFG = _Attn(XA_H, XA_D, XA_D, XA_D, False, XA_D ** -0.5, hp=4, hp_kv=4)


def _mla_weights(w_in, w_uq, w_ukv):
    w_in_p = jnp.pad(w_in, ((0, 0), (0, MLA_ZPAD - w_in.shape[1])))
    w_uq_p = jnp.pad(w_uq.reshape(MLA_QR, MLA_H, MLA_NOPE + MLA_ROPE), ((0, 0), (0, 0), (0, 2 * LANES - MLA_NOPE - MLA_ROPE)))
    w_uq_p = w_uq_p.reshape(MLA_QR, MLA_H * 2 * LANES)
    kv = w_ukv.reshape(MLA_KVR, MLA_H, MLA_NOPE + MLA_V)
    w_ukv_p = jnp.concatenate([kv[:, :, :MLA_NOPE].reshape(MLA_KVR, -1), kv[:, :, MLA_NOPE:].reshape(MLA_KVR, -1)], axis=1)
    return w_in_p, w_uq_p, w_ukv_p


def _mla_weight_grads(d_in_p, d_uq_p, d_ukv_p):
    d_in = d_in_p[:, :MLA_QR + MLA_KVR + MLA_ROPE]
    d_uq = d_uq_p.reshape(MLA_QR, MLA_H, 2 * LANES)[:, :, :MLA_NOPE + MLA_ROPE].reshape(MLA_QR, -1)
    half = MLA_H * MLA_NOPE
    d_ukv = jnp.concatenate([d_ukv_p[:, :half].reshape(MLA_KVR, MLA_H, MLA_NOPE),
                             d_ukv_p[:, half:].reshape(MLA_KVR, MLA_H, MLA_V)], axis=2).reshape(MLA_KVR, -1)
    return d_in, d_uq, d_ukv


def _mla_fwd(xs, h, wts, w_o, qn, kvn, tabs, g_next, tag):
    w_in_p, w_uq_p, w_ukv_p = wts
    z = mm(h, w_in_p, "nn", f"{tag}_in")
    cq, ckv, kr = mla_mid_fwd(z, qn, kvn, tabs, f"{tag}_mid")
    q = rope_q(mm(cq, w_uq_p, "nn", f"{tag}_uq"), tabs, False, f"{tag}_ropeq")
    kv = mm(ckv, w_ukv_p, "nn", f"{tag}_ukv", outs=(BF16,))
    o, lse = flash_fwd(_MLA_CFG, q, kv, kv, kr, f"{tag}_attn")
    xs, h_next = residual_norm(o, w_o, xs, g_next, f"{tag}_out")
    return xs, h_next, (z, cq, ckv, kr, q, kv, o, lse)


def _mla_bwd(dx, h, wts, w_o, g_wo, qn, kvn, tabs, saved, tag):
    w_in_p, w_uq_p, w_ukv_p = wts
    z, cq, ckv, kr, q, kv, o, lse = saved
    mm(o, dx, "tn", f"{tag}_dwo", outs=(BF16,), out_loc=g_wo)
    do = mm(dx, w_o, "nt", f"{tag}_do", outs=(BF16,))
    dq, delta = flash_dq(_MLA_CFG, q, kv, kv, kr, o, do, lse, F32, f"{tag}_attn_dq")
    dk1, dv, dkr = flash_dkv(_MLA_CFG, q, kv, kv, kr, do, lse, delta, BF16, f"{tag}_attn_dkv")
    dqp = rope_q(dq, tabs, True, f"{tag}_ropeq_t")
    d_uq_p = mm(cq, dqp, "tn", f"{tag}_duq")
    dcq = mm(dqp, w_uq_p, "nt", f"{tag}_dcq")
    dkv = jnp.concatenate([dk1, dv], axis=1)
    d_ukv_p = mm(ckv, dkv, "tn", f"{tag}_dukv")
    dckv = mm(dkv, w_ukv_p, "nt", f"{tag}_dckv")
    dz, dqn, dkvn = mla_mid_bwd(z, qn, kvn, tabs, dcq, dckv, dkr, f"{tag}_mid_bwd")
    d_in_p = mm(h, dz, "tn", f"{tag}_din")
    dh = (dz, w_in_p)
    d_in, d_uq, d_ukv = _mla_weight_grads(d_in_p, d_uq_p, d_ukv_p)
    return dh, dict(mla_w_in=d_in, mla_w_uq=d_uq, mla_w_ukv=d_ukv, mla_q_norm=dqn, mla_kv_norm=dkvn)


_GDN_QKV = 3 * GDN_H * GDN_D
_GDN_GATE_END = _GDN_QKV + GDN_H * GDN_D


def _gdn_weights(w_in):
    rep = lambda cols: jnp.repeat(cols, GDN_D, axis=1)
    return jnp.concatenate([w_in[:, :_GDN_GATE_END], rep(w_in[:, _GDN_GATE_END:_GDN_GATE_END + GDN_H]),
                            rep(w_in[:, _GDN_GATE_END + GDN_H:])], axis=1)


def _fold(x):
    return x.reshape(x.shape[0], -1, GDN_D).sum(-1)


def _gdn_fwd(xs, h, w_in_x, conv_w, a_log, dt_bias, o_norm, w_o, g_next, tag):
    z = mm(h, w_in_x, "nn", f"{tag}_in")
    qkv = gdn_conv_fwd(z, conv_w, f"{tag}_conv")
    a_x, dt_x = jnp.repeat(a_log.reshape(1, -1), GDN_D, axis=1), jnp.repeat(dt_bias.reshape(1, -1), GDN_D, axis=1)
    og, states, t_invs = gdn_chunk_fwd(qkv, z, a_x, dt_x, o_norm.reshape(1, -1), f"{tag}_chunks")
    xs, h_next = residual_norm(og, w_o, xs, g_next, f"{tag}_out")
    return xs, h_next, (z, qkv, a_x, dt_x, og, states, t_invs)


def _gdn_bwd(dx, h, w_in_x, conv_w, o_norm, w_o, g_wo, saved, tag):
    z, qkv, a_x, dt_x, og, states, t_invs = saved
    mm(og, dx, "tn", f"{tag}_dwo", outs=(BF16,), out_loc=g_wo)
    dog = mm(dx, w_o, "nt", f"{tag}_dog")
    dq, dk, dv, dgate, dbl, dal, da_x, ddt_x, don = gdn_chunk_bwd(qkv, z, a_x, dt_x, o_norm.reshape(1, -1), states, t_invs, dog,
                                                                  f"{tag}_chunks_bwd")
    dpre, dconv = gdn_conv_bwd(z, conv_w, jnp.concatenate([dq, dk, dv], axis=1), f"{tag}_conv_bwd")
    dz = jnp.concatenate([dpre, dgate, dbl, dal], axis=1)
    d_in_x = mm(h, dz, "tn", f"{tag}_din")
    dh = (dz, w_in_x)
    ge = _GDN_GATE_END
    d_in = jnp.concatenate([d_in_x[:, :ge], _fold(d_in_x[:, ge:ge + GDN_H * GDN_D]), _fold(d_in_x[:, ge + GDN_H * GDN_D:])], axis=1)
    return dh, dict(gdn_w_in=d_in, gdn_conv_w=dconv, gdn_a_log=_fold(da_x).reshape(-1), gdn_dt_bias=_fold(ddt_x).reshape(-1),
                    gdn_o_norm=don.reshape(-1))


def _sc_fwd(xs, h, w_in, conv_w, w_o, g_next, tag):
    z = mm(h, w_in, "nn", f"{tag}_in")
    y = sc_fwd(z, conv_w, f"{tag}_conv")
    xs, h_next = residual_norm(y, w_o, xs, g_next, f"{tag}_out")
    return xs, h_next, (z, y)


def _sc_bwd(dx, h, w_in, g_win, conv_w, w_o, g_wo, saved, tag):
    z, y = saved
    mm(y, dx, "tn", f"{tag}_dwo", outs=(BF16,), out_loc=g_wo)
    dy = mm(dx, w_o, "nt", f"{tag}_dy")
    db, dc, du, dconv = sc_bwd(z, conv_w, dy, f"{tag}_conv_bwd")
    dz = jnp.concatenate([db, dc, du], axis=1)
    mm(h, dz, "tn", f"{tag}_din", outs=(BF16,), out_loc=g_win)
    dh = (dz, w_in)
    return dh, dict(sc_conv_w=dconv)


def local_step(x, mem, pos, target, lay, wslabs, gslabs, small, before=None, after_bwd=None):
    depth = small["norm_mix"].shape[0]
    W = lambda name, layer: lay.loc(wslabs, name, layer)
    G = lambda name, layer: lay.loc(gslabs, name, layer)
    tabs = rope_tables(pos)
    mem_n = rmsnorm_fwd(mem, small["mem_norm"], "mem_norm")
    full = {n: lay.full(wslabs, n) for n in ("mla_w_in", "mla_w_uq", "mla_w_ukv")}
    mla_w = [_mla_weights(full["mla_w_in"][j], full["mla_w_uq"][j], full["mla_w_ukv"][j]) for j in range(full["mla_w_in"].shape[0])]
    gdn_in_x = {}

    xs, h_pre = x, None
    saved = []
    for i in range(depth):
        j, kind = i // 3, i % 3
        tag = f"l{i}"
        if before is not None:
            xs = before(i, "mix", xs)
        if kind == 1:
            gdn_in_x[j] = _gdn_weights(lay.full(wslabs, "gdn_w_in")[j])
        x_a = xs
        h = h_pre if h_pre is not None else rmsnorm_fwd(xs, small["norm_mix"][i], f"{tag}_norm_mix")
        g_mem = small["norm_mem"][i]
        if kind == 0:
            xs, hn, mix = _mla_fwd(xs, h, mla_w[j], W("mla_w_o", j), small["mla_q_norm"][j], small["mla_kv_norm"][j], tabs, g_mem,
                                   f"{tag}_mla")
        elif kind == 1:
            xs, hn, mix = _gdn_fwd(xs, h, gdn_in_x[j], small["gdn_conv_w"][j], small["gdn_a_log"][j], small["gdn_dt_bias"][j],
                                   small["gdn_o_norm"][j], W("gdn_w_o", j), g_mem, f"{tag}_gdn")
        else:
            xs, hn, mix = _sc_fwd(xs, h, W("sc_w_in", j), small["sc_conv_w"][j], W("sc_w_o", j), g_mem, f"{tag}_sc")
        if before is not None:
            xs = before(i, "xa", xs)
        x_b = xs
        xq = mm(hn, W("xa_w_q", i), "nn", f"{tag}_xa_q", outs=(BF16,))
        xkv = mm(mem_n, W("xa_w_kv", i), "nn", f"{tag}_xa_kv", outs=(BF16,))
        xo, xlse = flash_fwd(_XA_CFG, xq, xkv, xkv, None, f"{tag}_xa_attn")
        xs, hm = residual_norm(xo, W("xa_w_o", i), xs, small["norm_mlp"][i], f"{tag}_xa_out")
        x_c = xs
        h1, act = mm(hm, W("mlp_w1", i), "nn", f"{tag}_mlp_up", outs=(BF16, BF16), epi=_epi_relu2)
        xs, h_pre = residual_norm(act, W("mlp_w2", i), xs, small["norm_mix"][i + 1] if i + 1 < depth else None,
                                  f"{tag}_mlp_down", tm=512)
        saved.append((x_a, h, mix, x_b, hn, xq, xkv, xo, xlse, x_c, hm, h1, act))

    se, dx, d_final = loss_head(xs, small["final_norm"], target)

    per_layer = {n: [None] * depth for n in ("norm_mix", "norm_mem", "norm_mlp")}
    mixer = {}
    dmem_n = jnp.zeros(mem.shape, F32)
    for i in reversed(range(depth)):
        j, kind = i // 3, i % 3
        tag = f"l{i}"
        x_a, h, mix, x_b, hn, xq, xkv, xo, xlse, x_c, hm, h1, act = saved[i]
        mm(act, dx, "tn", f"{tag}_mlp_dw2", outs=(BF16,), out_loc=G("mlp_w2", i))
        dh1 = mm(dx, W("mlp_w2", i), "nt", f"{tag}_mlp_dh1", outs=(BF16,), epi=_epi_relu2_bwd, extras=(h1,))
        mm(hm, dh1, "tn", f"{tag}_mlp_dw1", outs=(BF16,), out_loc=G("mlp_w1", i))
        dx, dg = mm(dh1, W("mlp_w1", i), "nt", f"{tag}_mlp_dhm", epi=_epi_norm_bwd, extras=(x_c, dx), vecs=(small["norm_mlp"][i],),
                    row_outs=1, tm=512)
        per_layer["norm_mlp"][i] = dg.reshape(-1)
        mm(xo, dx, "tn", f"{tag}_xa_dwo", outs=(BF16,), out_loc=G("xa_w_o", i))
        dxo = mm(dx, W("xa_w_o", i), "nt", f"{tag}_xa_do", outs=(BF16,))
        dxq, xdelta = flash_dq(_XA_CFG, xq, xkv, xkv, None, xo, dxo, xlse, BF16, f"{tag}_xa_attn_dq")
        dxk, dxv = flash_dkv(_XA_CFG, xq, xkv, xkv, None, dxo, xlse, xdelta, BF16, f"{tag}_xa_attn_dkv")
        dxkv = jnp.concatenate([dxk, dxv], axis=1)
        mm(hn, dxq, "tn", f"{tag}_xa_dwq", outs=(BF16,), out_loc=G("xa_w_q", i))
        dx, dg = mm(dxq, W("xa_w_q", i), "nt", f"{tag}_xa_dhn", epi=_epi_norm_bwd, extras=(x_b, dx), vecs=(small["norm_mem"][i],),
                    row_outs=1, tm=512)
        per_layer["norm_mem"][i] = dg.reshape(-1)
        mm(mem_n, dxkv, "tn", f"{tag}_xa_dwkv", outs=(BF16,), out_loc=G("xa_w_kv", i))
        dmem_n = mm(dxkv, W("xa_w_kv", i), "nt", f"{tag}_xa_dmem", epi=_epi_add, extras=(dmem_n,))
        if after_bwd is not None:
            dx = after_bwd(i, "xa", dx)
        if kind == 0:
            dh, gr = _mla_bwd(dx, h, mla_w[j], W("mla_w_o", j), G("mla_w_o", j), small["mla_q_norm"][j], small["mla_kv_norm"][j],
                              tabs, mix, f"{tag}_mla")
        elif kind == 1:
            dh, gr = _gdn_bwd(dx, h, gdn_in_x[j], small["gdn_conv_w"][j], small["gdn_o_norm"][j], W("gdn_w_o", j), G("gdn_w_o", j),
                              mix, f"{tag}_gdn")
        else:
            dh, gr = _sc_bwd(dx, h, W("sc_w_in", j), G("sc_w_in", j), small["sc_conv_w"][j], W("sc_w_o", j), G("sc_w_o", j),
                             mix, f"{tag}_sc")
        if kind == 1:
            lay.put_full(gslabs, "gdn_w_in", gr.pop("gdn_w_in")[None])
        for n, g in gr.items():
            mixer.setdefault(n, {})[j] = g
        dz_mix, w_mix = dh
        dx, dg = mm(dz_mix, w_mix, "nt", f"{tag}_mix_dh", epi=_epi_norm_bwd, extras=(x_a, dx), vecs=(small["norm_mix"][i],),
                    row_outs=1, tm=256 if kind == 1 else 512)
        per_layer["norm_mix"][i] = dg.reshape(-1)
        if after_bwd is not None:
            dx = after_bwd(i, "mix", dx)

    _, d_mem_norm = rmsnorm_bwd(mem, small["mem_norm"], dmem_n, jnp.zeros(mem.shape, F32), "mem_norm_bwd")
    grads = {n: jnp.stack(v) for n, v in per_layer.items()}
    for n, by_j in mixer.items():
        grads[n] = jnp.stack([by_j[j] for j in sorted(by_j)])
    grads["mem_norm"] = d_mem_norm
    grads["final_norm"] = d_final
    for n in ("mla_w_in", "mla_w_uq", "mla_w_ukv"):
        lay.put_full(gslabs, n, grads.pop(n))
    return se, dx, grads


def kernel(x, mem, positions, mla_w_in, mla_q_norm, mla_kv_norm, mla_w_uq, mla_w_ukv, mla_w_o, gdn_w_in, gdn_conv_w, gdn_a_log, gdn_dt_bias, gdn_o_norm, gdn_w_o, sc_w_in, sc_conv_w, sc_w_o, norm_mix, norm_mem, norm_mlp, xa_w_q, xa_w_kv, xa_w_o, mlp_w1, mlp_w2, mem_norm, final_norm, loss_target, m_mla_w_in, m_mla_q_norm, m_mla_kv_norm, m_mla_w_uq, m_mla_w_ukv, m_mla_w_o, m_gdn_w_in, m_gdn_conv_w, m_gdn_a_log, m_gdn_dt_bias, m_gdn_o_norm, m_gdn_w_o, m_sc_w_in, m_sc_conv_w, m_sc_w_o, m_norm_mix, m_norm_mem, m_norm_mlp, m_xa_w_q, m_xa_w_kv, m_xa_w_o, m_mlp_w1, m_mlp_w2, m_mem_norm, m_final_norm, v_mla_w_in, v_mla_q_norm, v_mla_kv_norm, v_mla_w_uq, v_mla_w_ukv, v_mla_w_o, v_gdn_w_in, v_gdn_conv_w, v_gdn_a_log, v_gdn_dt_bias, v_gdn_o_norm, v_gdn_w_o, v_sc_w_in, v_sc_conv_w, v_sc_w_o, v_norm_mix, v_norm_mem, v_norm_mlp, v_xa_w_q, v_xa_w_kv, v_xa_w_o, v_mlp_w1, v_mlp_w2, v_mem_norm, v_final_norm):
    given = dict(locals())
    p = {n: given[n] for n in _WEIGHTS}
    mom = {n: given["m_" + n] for n in _WEIGHTS}
    var = {n: given["v_" + n] for n in _WEIGHTS}
    split = sorted({n for members in _SLABS.values() for n, _, _, _ in members})
    lay = Layout({n: p[n].shape for n in split})
    flat2d = lambda a: a.reshape(-1, a.shape[-1])

    me = (2 * lax.axis_index("x") + lax.axis_index("y")).astype(jnp.int32)
    core = lax.axis_index("c").astype(jnp.int32)
    me1, c1, mc = me.reshape(1), core.reshape(1), jnp.stack([me, core])

    wslabs = lay.new_slabs(BF16)

    def cast_group(slabs, chip):
        for slab in slabs:
            for name, off, l0, l1, rpl in lay.members[slab]:
                cast_into(flat2d(p[name]), l0 * rpl, (l1 - l0) * rpl, wslabs[slab], off, chip, f"cast_{slab}_{name}")

    first = _GROUPS[0][0]
    cast_group(first, me1)
    small_names = [n for n, _ in _SMALL]
    words = lax.bitcast_convert_type(jnp.concatenate([p[n].reshape(-1) for n in small_names]), BF16).reshape(-1)
    words = jnp.pad(words, (0, SMALL_ROWS * SMALL_COLS - words.shape[0])).reshape(1, SMALL_ROWS, SMALL_COLS)
    small_slab = lax.dynamic_update_slice(jnp.zeros((N_CHIPS, SMALL_ROWS, SMALL_COLS), BF16), words, (me, 0, 0))

    send0, recv0, thru0, token = gather_start([wslabs[s].arr for s in first] + [small_slab], me1, "weight_gather_start_first")
    in_flight = {}
    for slabs, point in _GROUPS[1:]:
        cast_group(slabs, me1 + token[0, 0].astype(jnp.int32))
        send, recv, thru, token = gather_start([wslabs[s].arr for s in slabs], token, f"weight_gather_start_{slabs[0]}")
        in_flight[point] = (send, recv, thru, slabs)
    started_token = token
    landed = gather_wait(send0, recv0, thru0, started_token, "weight_gather_wait_first")
    gathered = gather_forward(landed, "weight_gather_forward_first")
    for s, arr in zip(first, gathered):
        wslabs[s].arr = arr

    def before(i, stage, xs):
        if (i, stage) == (0, "mix"):
            return xs + started_token[0, 0]
        if (i, stage) in in_flight:
            send, recv, thru, slabs = in_flight[(i, stage)]
            landed = gather_wait(send, recv, thru, xs, f"weight_gather_wait_{slabs[0]}")
            for s, arr in zip(slabs, gather_forward(landed, f"weight_gather_forward_{slabs[0]}")):
                wslabs[s].arr = arr
        return xs

    small = {n: p[n] for n in _REPL}
    got, off = gathered[-1].reshape(N_CHIPS, -1), 0
    for n, ax in _SMALL:
        vals = lax.bitcast_convert_type(got[:, off:off + 2 * p[n].size].reshape(N_CHIPS, p[n].size, 2), F32)
        vals = vals.reshape((N_CHIPS,) + p[n].shape)
        small[n] = jnp.concatenate([vals[s] for s in range(N_CHIPS)], axis=ax)
        off += 2 * p[n].size

    gslabs = lay.new_slabs(BF16)
    complete_at = {point: slabs for slabs, point in _GROUPS[1:]}
    exchanging = []

    def after_bwd(i, stage, dx):
        if (i, stage) not in complete_at:
            return dx
        slabs = complete_at[(i, stage)]
        g = [gslabs[s].arr for s in slabs]
        swapped = pair_swap_halves(g, f"grad_pair_swap_{slabs[0]}")
        part = [pair_add(a, b, c1, f"pair_add_{s}") for a, b, s in zip(g, swapped, slabs)]
        send, recv, thru, token = exchange_start(part, c1, f"grad_exchange_start_{slabs[0]}")
        exchanging.append((slabs, send, recv, thru))
        return dx + token[0, 0]

    se, dx, sgrads = local_step(x[0], mem[0], positions.reshape(-1, 1), loss_target[0], lay, wslabs, gslabs, small,
                                before, after_bwd)
    loss = lax.psum(0.5 * jnp.sum(se) / x.shape[-1], ("x", "y", "c"))
    names, parts, received = [], [], []
    for slabs, send, recv, thru in exchanging:
        part, got = exchange_wait(send, recv, thru, dx, f"grad_exchange_wait_{slabs[0]}")
        names, parts, received = names + slabs, parts + list(part), received + list(got)

    axes = dict(_SMALL)
    small_order = small_names + _REPL
    slots = []
    for s in range(N_CHIPS):
        vals = {n: (lax.slice_in_dim(g, s * p[n].shape[axes[n]], (s + 1) * p[n].shape[axes[n]], axis=axes[n]) if n in axes else g)
                for n, g in sgrads.items()}
        slots.append(_small_pack(vals, small_order))
    g_last = [gslabs[s].arr for s in first] + [jnp.stack(slots).astype(BF16)]
    names_last = first + ["small"]
    swapped_last = pair_swap_halves(g_last, "grad_pair_swap_last")
    part_last = [pair_add(g, b, c1, f"pair_add_{s}") for g, b, s in zip(g_last, swapped_last, names_last)]
    send, recv, thru, token = exchange_start(part_last, c1, "grad_exchange_start_last")
    mc_after = mc + token[0, 0].astype(jnp.int32)
    halves = [chip_sum(q, r, mc_after, f"chip_sum_{s}") for q, r, s in zip(parts, received, names)]
    part_last, got_last = exchange_wait(send, recv, thru, list(halves), "grad_exchange_wait_last")
    halves += [chip_sum(q, r, mc, f"chip_sum_{s}") for q, r, s in zip(part_last, got_last, names_last)]
    reduced = dict(zip(names + names_last, pair_join_halves(halves)))

    res = {}
    for slab in _SLABS:
        for name, off, l0, l1, rpl in lay.members[slab]:
            res[name] = adamw(reduced[slab], off, flat2d(p[name]), flat2d(mom[name]), flat2d(var[name]), l0 * rpl, (l1 - l0) * rpl,
                              res.get(name), f"adamw_{slab}_{name}")
    for name in split:
        res[name] = [o.reshape(p[name].shape) for o in res[name]]
    sp = {k: _small_pack(d, small_order) for k, d in (("w", p), ("m", mom), ("v", var))}
    outs = adamw(reduced["small"], 0, sp["w"], sp["m"], sp["v"], 0, SMALL_ROWS, None, "adamw_small")
    unpacked = [_small_unpack(o, p, small_order) for o in outs]
    for n in small_order:
        res[n] = [u[n] for u in unpacked]
    return (loss, dx[None], *[res[n][k] for k in range(4) for n in _WEIGHTS])
```

```python
import jax
import jax.numpy as jnp
from jax import lax
from jax.experimental import pallas as pl
from jax.experimental.pallas import tpu as pltpu

F32 = jnp.float32
BF16 = jnp.bfloat16
MESH = pl.DeviceIdType.MESH

EPS = 1e-6
ROPE_THETA = 10000.0
N_CHIPS = 4
LANES = 128
VMEM_LIMIT = 56 * 1024 * 1024
NEG = -1e30

MLA_H, MLA_NOPE, MLA_ROPE, MLA_V = 8, 128, 64, 128
MLA_QR, MLA_KVR = 384, 256
MLA_ZPAD = 768
GDN_H, GDN_D, GDN_C = 8, 128, 64
XA_H, XA_D = 4, 256

ADAM_LR, ADAM_B1, ADAM_B2, ADAM_EPS, ADAM_WD, ADAM_STEP = 0.001, 0.9, 0.999, 1e-08, 0.01, 10

SMALL_ROWS, SMALL_COLS = 32, 1024


def _cparams(sem=None):
    return pltpu.CompilerParams(dimension_semantics=sem, vmem_limit_bytes=VMEM_LIMIT)


def _pick(dim, pref):
    t = (min(pref, dim) // LANES) * LANES
    while t >= LANES:
        if dim % t == 0:
            return t
        t -= LANES
    return dim


def _pick_rows(rows, pref, *offsets):
    t = (min(pref, rows) // 16) * 16
    while t > 16 and (rows % t or any(o % t for o in offsets)):
        t -= 16
    return t


class Slab:
    def __init__(self, rows, width, dtype, arr=None):
        self.shape, self.dtype, self.arr = (N_CHIPS, rows, width), dtype, arr


class Loc:
    def __init__(self, slab, row0, K, N, axis):
        self.slab, self.row0, self.K, self.N, self.axis = slab, row0, K, N, axis
        self.Ks = K // N_CHIPS if axis == 0 else K
        self.Ns = N // N_CHIPS if axis == 1 else N

    def tile_spec(self, tr, tc, rc):
        assert self.row0 % tr == 0 and self.Ks % tr == 0 and self.Ns % tc == 0, (self.row0, self.Ks, self.Ns, tr, tc)
        r0, rb, cb = self.row0 // tr, self.Ks // tr, self.Ns // tc
        if self.axis == 0:
            return pl.BlockSpec((None, tr, tc), lambda i, j: (rc(i, j)[0] // rb, r0 + rc(i, j)[0] % rb, rc(i, j)[1]))
        return pl.BlockSpec((None, tr, tc), lambda i, j: (rc(i, j)[1] // cb, r0 + rc(i, j)[0], rc(i, j)[1] % cb))

    def slot_spec(self, slot, tr, tc, rc):
        assert self.row0 % tr == 0, (self.row0, tr)
        r0 = self.row0 // tr
        return pl.BlockSpec((None, tr, tc), lambda i, j: (slot, r0 + rc(i, j)[0], rc(i, j)[1]))


_DIMS = {"nn": ((1,), (0,)), "nt": ((1,), (1,)), "tn": ((0,), (0,))}
_ANY = pl.BlockSpec(memory_space=pl.ANY)


def mm(a, b, mode, name, outs=(F32,), epi=None, extras=(), tm=1024, tn=1024, out_loc=None, vecs=(), row_outs=0, per_row=()):
    full_rows = bool(vecs) or row_outs > 0 or bool(per_row)
    b_loc = b if isinstance(b, Loc) else None
    if mode == "nn":
        M, K = a.shape
        K2, N = (b_loc.K, b_loc.N) if b_loc else b.shape
    elif mode == "nt":
        M, K = a.shape
        N, K2 = (b_loc.K, b_loc.N) if b_loc else b.shape
    else:
        K, M = a.shape
        K2, N = b.shape
    assert K == K2, (name, a.shape, K2, N)
    tm = _pick(out_loc.Ks if (out_loc and out_loc.axis == 0) else M, tm)
    n_split = full_rows and b_loc is not None and mode == "nt" and b_loc.axis == 0
    if out_loc is not None and out_loc.axis == 1:
        tn = _pick(out_loc.Ns, tn)
    elif n_split:
        tn = N
    elif b_loc is not None and ((mode == "nn" and b_loc.axis == 1) or (mode == "nt" and b_loc.axis == 0)):
        tn = _pick(b_loc.Ns if mode == "nn" else b_loc.Ks, tn)
    elif b_loc is not None:
        tn = N if full_rows else _pick(N, min(tn, 512))
    else:
        tn = N if full_rows else _pick(N, tn)
    assert tn == N or not full_rows, name

    parts = 1
    if mode == "tn":
        a_spec = pl.BlockSpec((K, tm), lambda i, j: (0, i))
        b_specs, b_args = [pl.BlockSpec((K, tn), lambda i, j: (0, j))], [b]
    else:
        a_spec = pl.BlockSpec((tm, K), lambda i, j: (i, 0))
        if b_loc is None:
            b_specs = [pl.BlockSpec((K, tn), lambda i, j: (0, j)) if mode == "nn" else pl.BlockSpec((tn, K), lambda i, j: (j, 0))]
            b_args = [b]
        elif mode == "nn" and b_loc.axis == 1:
            b_specs, b_args = [b_loc.tile_spec(K, tn, lambda i, j: (0, j))], [b_loc.slab.arr]
        elif n_split:
            b_specs = [b_loc.slot_spec(s, b_loc.Ks, K, lambda i, j: (0, 0)) for s in range(N_CHIPS)]
            b_args = [b_loc.slab.arr] * N_CHIPS
        elif mode == "nt" and b_loc.axis == 0:
            b_specs, b_args = [b_loc.tile_spec(tn, K, lambda i, j: (j, 0))], [b_loc.slab.arr]
        elif mode == "nn":
            parts = N_CHIPS
            b_specs = [b_loc.slot_spec(s, b_loc.Ks, tn, lambda i, j: (0, j)) for s in range(parts)]
            b_args = [b_loc.slab.arr] * parts
        else:
            parts = N_CHIPS
            b_specs = [b_loc.slot_spec(s, tn, b_loc.Ns, lambda i, j: (j, 0)) for s in range(parts)]
            b_args = [b_loc.slab.arr] * parts
    kp = K // parts
    n_b = N_CHIPS if n_split else parts
    n_ex, n_out = len(extras) + len(per_row) + len(vecs), len(outs)
    dims = (_DIMS[mode], ((), ()))

    def body(*refs):
        a_ref = refs[0]
        b_refs = refs[1:1 + n_b]
        ex_refs = refs[1 + n_b:1 + n_b + n_ex]
        o_refs = refs[len(refs) - n_out - row_outs:len(refs) - row_outs]
        r_refs = refs[len(refs) - row_outs:]
        acc = None
        if n_split:
            av = a_ref[...].astype(BF16)
            acc = jnp.concatenate([lax.dot_general(av, b_ref[...].astype(BF16), dims, preferred_element_type=F32)
                                   for b_ref in b_refs], axis=1)
        for s in range(0 if n_split else parts):
            av = a_ref[...] if parts == 1 else a_ref[:, s * kp:(s + 1) * kp]
            d = lax.dot_general(av.astype(BF16), b_refs[s][...].astype(BF16), dims, preferred_element_type=F32)
            acc = d if acc is None else acc + d
        res = epi(acc, *[e[...] for e in ex_refs]) if epi is not None else (acc,)
        for o_ref, v in zip(o_refs, res[:n_out]):
            o_ref[...] = v.astype(o_ref.dtype)
        for r_ref, v in zip(r_refs, res[n_out:]):
            @pl.when(pl.program_id(0) == 0)
            def _():
                r_ref[...] = jnp.zeros_like(r_ref)

            r_ref[...] += v

    mn_spec = pl.BlockSpec((tm, tn), lambda i, j: (i, j))
    row_spec = pl.BlockSpec((1, tn), lambda i, j: (0, j))
    in_specs = ([a_spec] + b_specs + [mn_spec] * len(extras) + [pl.BlockSpec((tm, r.shape[1]), lambda i, j: (i, 0)) for r in per_row]
                + [row_spec] * len(vecs))
    args = [a] + b_args + list(extras) + list(per_row) + [v.reshape(1, N) for v in vecs]
    aliases = {}
    if out_loc is None:
        out_specs = [mn_spec] * n_out + [row_spec] * row_outs
        out_shape = [jax.ShapeDtypeStruct((M, N), d) for d in outs] + [jax.ShapeDtypeStruct((1, N), F32)] * row_outs
    else:
        assert n_out == 1 and mode == "tn"
        out_specs = [out_loc.tile_spec(tm, tn, lambda i, j: (i, j))]
        out_shape = [jax.ShapeDtypeStruct(out_loc.slab.shape, out_loc.slab.dtype)]
        if out_loc.slab.arr is not None:
            in_specs.append(_ANY)
            args.append(out_loc.slab.arr)
            aliases = {len(args) - 1: 0}

    res = pl.pallas_call(
        body, name=name, grid=(M // tm, N // tn), in_specs=in_specs, out_specs=out_specs, out_shape=out_shape,
        input_output_aliases=aliases, compiler_params=_cparams(("arbitrary" if row_outs else "parallel", "parallel")),
    )(*args)
    if out_loc is not None:
        out_loc.slab.arr = res[0]
        return None
    return res[0] if len(res) == 1 else tuple(res)


def _epi_add(acc, r):
    return (acc + r,)


def _epi_add_norm(acc, r, g):
    x = acc + r
    return x, _rms(x, g)


def _epi_norm_bwd(acc, x, dx_in, g):
    r = lax.rsqrt(jnp.mean(x * x, axis=-1, keepdims=True) + EPS)
    xh = x * r
    dxh = acc * g
    dx = dx_in + r * (dxh - xh * jnp.mean(dxh * xh, axis=-1, keepdims=True))
    return dx, jnp.sum(acc * xh, axis=0, keepdims=True)


def residual_norm(a, w, xs, g, name, tm=1024):
    if g is None:
        return mm(a, w, "nn", name, epi=_epi_add, extras=(xs,), tm=tm), None
    return mm(a, w, "nn", name, outs=(F32, BF16), epi=_epi_add_norm, extras=(xs,), vecs=(g,), tm=tm)


def _epi_relu2(acc):
    r = jnp.maximum(acc, 0.0)
    return acc, r * r


def _epi_relu2_bwd(acc, h1):
    return (acc * (2.0 * jnp.maximum(h1.astype(F32), 0.0)),)


def _rms(x, g):
    return x * lax.rsqrt(jnp.mean(x * x, axis=-1, keepdims=True) + EPS) * g


def _row_spec(ts, cols):
    return pl.BlockSpec((ts, cols), lambda i: (i, 0))


def _par_spec(cols):
    return pl.BlockSpec((1, cols), lambda i: (0, 0))


def rmsnorm_fwd(x, g, name, ts=256):
    T, D = x.shape
    ts = min(ts, T)

    def body(x_ref, g_ref, o_ref):
        o_ref[...] = _rms(x_ref[...], g_ref[...]).astype(o_ref.dtype)

    return pl.pallas_call(
        body, name=name, grid=(T // ts,),
        in_specs=[_row_spec(ts, D), _par_spec(D)], out_specs=_row_spec(ts, D),
        out_shape=jax.ShapeDtypeStruct((T, D), BF16), compiler_params=_cparams(("parallel",)),
    )(x, g.reshape(1, D))


def rmsnorm_bwd(x, g, dy, dx_in, name, ts=256):
    T, D = x.shape
    ts = min(ts, T)

    def body(x_ref, g_ref, dy_ref, dxi_ref, dx_ref, dg_ref):
        xv = x_ref[...]
        r = lax.rsqrt(jnp.mean(xv * xv, axis=-1, keepdims=True) + EPS)
        xh = xv * r
        dyv = dy_ref[...].astype(F32)
        dxh = dyv * g_ref[...]
        dx_ref[...] = dxi_ref[...] + r * (dxh - xh * jnp.mean(dxh * xh, axis=-1, keepdims=True))
        dg = jnp.sum(dyv * xh, axis=0, keepdims=True)

        @pl.when(pl.program_id(0) == 0)
        def _():
            dg_ref[...] = jnp.zeros_like(dg_ref)

        dg_ref[...] += dg

    dx, dg = pl.pallas_call(
        body, name=name, grid=(T // ts,),
        in_specs=[_row_spec(ts, D), _par_spec(D), _row_spec(ts, D), _row_spec(ts, D)],
        out_specs=[_row_spec(ts, D), _par_spec(D)],
        out_shape=[jax.ShapeDtypeStruct((T, D), F32), jax.ShapeDtypeStruct((1, D), F32)],
        compiler_params=_cparams(("arbitrary",)),
    )(x, g.reshape(1, D), dy, dx_in)
    return dx, dg.reshape(D)


def rope_tables(pos, name="rope_tables"):
    T = pos.shape[0]
    half = MLA_ROPE // 2
    inv = ROPE_THETA ** (-jnp.arange(0, MLA_ROPE, 2, dtype=F32) / MLA_ROPE)
    inv_row = jnp.concatenate([inv, inv, jnp.zeros((LANES - MLA_ROPE,), F32)]).reshape(1, LANES)

    def body(p_ref, f_ref, c_ref, a_ref, b_ref):
        ang = p_ref[...].astype(F32) * f_ref[...]
        lane = lax.broadcasted_iota(jnp.int32, ang.shape, 1)
        c, s = jnp.cos(ang), jnp.sin(ang)
        c_ref[...] = jnp.where(lane < MLA_ROPE, c, 0.0)
        a_ref[...] = jnp.where(lane < half, -s, 0.0)
        b_ref[...] = jnp.where((lane >= half) & (lane < MLA_ROPE), s, 0.0)

    sh = jax.ShapeDtypeStruct((T, LANES), F32)
    return pl.pallas_call(body, name=name, out_shape=[sh, sh, sh], compiler_params=_cparams())(pos, inv_row)


def _roll_l(x):
    return pltpu.roll(x, LANES - MLA_ROPE // 2, 1)


def _roll_r(x):
    return pltpu.roll(x, MLA_ROPE // 2, 1)


def _rope(r, c, sa, sb):
    return r * c + _roll_l(r) * sa + _roll_r(r) * sb


def _rope_t(d, c, sa, sb):
    return d * c + _roll_r(d * sa) + _roll_l(d * sb)


def _epi_rope_q(acc, c, sa, sb):
    hw = 2 * LANES
    parts = []
    for h in range(acc.shape[1] // hw):
        parts += [acc[:, h * hw:h * hw + LANES], _rope(acc[:, h * hw + LANES:(h + 1) * hw], c, sa, sb)]
    return (jnp.concatenate(parts, axis=1),)


def mla_mid_fwd(z, qn, kvn, tabs, name, ts=256):
    T = z.shape[0]
    ts = min(ts, T)
    a0, a1 = MLA_QR, MLA_QR + MLA_KVR

    def body(z_ref, qn_ref, kvn_ref, c_ref, sa_ref, sb_ref, cq_ref, ckv_ref, kr_ref):
        cq_ref[...] = _rms(z_ref[:, 0:a0], qn_ref[...]).astype(BF16)
        ckv_ref[...] = _rms(z_ref[:, a0:a1], kvn_ref[...]).astype(BF16)
        kr_ref[...] = _rope(z_ref[:, a1:MLA_ZPAD], c_ref[...], sa_ref[...], sb_ref[...]).astype(BF16)

    return pl.pallas_call(
        body, name=name, grid=(T // ts,),
        in_specs=[_row_spec(ts, MLA_ZPAD), _par_spec(MLA_QR), _par_spec(MLA_KVR)] + [_row_spec(ts, LANES)] * 3,
        out_specs=[_row_spec(ts, MLA_QR), _row_spec(ts, MLA_KVR), _row_spec(ts, LANES)],
        out_shape=[jax.ShapeDtypeStruct((T, MLA_QR), BF16), jax.ShapeDtypeStruct((T, MLA_KVR), BF16),
                   jax.ShapeDtypeStruct((T, LANES), BF16)],
        compiler_params=_cparams(("parallel",)),
    )(z, qn.reshape(1, -1), kvn.reshape(1, -1), *tabs)


def mla_mid_bwd(z, qn, kvn, tabs, dcq, dckv, dkr, name, ts=256):
    T = z.shape[0]
    ts = min(ts, T)
    a0, a1 = MLA_QR, MLA_QR + MLA_KVR

    def body(z_ref, qn_ref, kvn_ref, c_ref, sa_ref, sb_ref, dcq_ref, dckv_ref, dkr_ref, dz_ref, dqn_ref, dkvn_ref):
        _, vq = jax.vjp(_rms, z_ref[:, 0:a0], qn_ref[...])
        dzq, dqn = vq(dcq_ref[...].astype(F32))
        _, vk = jax.vjp(_rms, z_ref[:, a0:a1], kvn_ref[...])
        dzk, dkvn = vk(dckv_ref[...].astype(F32))
        dz_ref[:, 0:a0] = dzq.astype(dz_ref.dtype)
        dz_ref[:, a0:a1] = dzk.astype(dz_ref.dtype)
        dz_ref[:, a1:MLA_ZPAD] = _rope_t(dkr_ref[...].astype(F32), c_ref[...], sa_ref[...], sb_ref[...]).astype(dz_ref.dtype)

        @pl.when(pl.program_id(0) == 0)
        def _():
            dqn_ref[...] = jnp.zeros_like(dqn_ref)
            dkvn_ref[...] = jnp.zeros_like(dkvn_ref)

        dqn_ref[...] += dqn
        dkvn_ref[...] += dkvn

    dz, dqn, dkvn = pl.pallas_call(
        body, name=name, grid=(T // ts,),
        in_specs=[_row_spec(ts, MLA_ZPAD), _par_spec(MLA_QR), _par_spec(MLA_KVR)] + [_row_spec(ts, LANES)] * 3
        + [_row_spec(ts, MLA_QR), _row_spec(ts, MLA_KVR), _row_spec(ts, LANES)],
        out_specs=[_row_spec(ts, MLA_ZPAD), _par_spec(MLA_QR), _par_spec(MLA_KVR)],
        out_shape=[jax.ShapeDtypeStruct((T, MLA_ZPAD), BF16), jax.ShapeDtypeStruct((1, MLA_QR), F32),
                   jax.ShapeDtypeStruct((1, MLA_KVR), F32)],
        compiler_params=_cparams(("arbitrary",)),
    )(z, qn.reshape(1, -1), kvn.reshape(1, -1), *tabs, dcq, dckv, dkr)
    return dz, dqn.reshape(-1), dkvn.reshape(-1)


def loss_head(x, g, target, name="loss_head", ts=256):
    T, D = x.shape
    ts = min(ts, T)

    def body(x_ref, g_ref, t_ref, se_ref, dx_ref, dg_ref):
        xv = x_ref[...]
        r = lax.rsqrt(jnp.mean(xv * xv, axis=-1, keepdims=True) + EPS)
        xh = xv * r
        err = xh * g_ref[...] - t_ref[...]
        dy = err * (1.0 / D)
        dxh = dy * g_ref[...]
        dx_ref[...] = r * (dxh - xh * jnp.mean(dxh * xh, axis=-1, keepdims=True))

        @pl.when(pl.program_id(0) == 0)
        def _():
            se_ref[...] = jnp.zeros_like(se_ref)
            dg_ref[...] = jnp.zeros_like(dg_ref)

        se_ref[...] += jnp.sum(err * err, axis=0, keepdims=True)
        dg_ref[...] += jnp.sum(dy * xh, axis=0, keepdims=True)

    se, dx, dg = pl.pallas_call(
        body, name=name, grid=(T // ts,),
        in_specs=[_row_spec(ts, D), _par_spec(D), _row_spec(ts, D)],
        out_specs=[_par_spec(D), _row_spec(ts, D), _par_spec(D)],
        out_shape=[jax.ShapeDtypeStruct((1, D), F32), jax.ShapeDtypeStruct((T, D), F32), jax.ShapeDtypeStruct((1, D), F32)],
        compiler_params=_cparams(("arbitrary",)),
    )(x, g.reshape(1, D), target)
    return se, dx, dg.reshape(D)


def _dot_nt(a, b):
    return lax.dot_general(a, b, (((1,), (1,)), ((), ())), preferred_element_type=F32)


def _dot_nn(a, b):
    return lax.dot_general(a, b, (((1,), (0,)), ((), ())), preferred_element_type=F32)


class _Attn:
    def __init__(self, H, dq, dk1, dv, causal, scale, hp, hp_kv, blk=256):
        self.H, self.dq, self.dk1, self.dv, self.causal, self.scale, self.blk = H, dq, dk1, dv, causal, scale, blk
        self.hp, self.hp_kv = hp, hp_kv


def _cols(ref, rows, hh, width):
    return ref[rows, hh * width:(hh + 1) * width]


def _keys(cfg, k1_ref, k2_ref, rows, hh):
    ks = _cols(k1_ref, rows, hh, cfg.dk1)
    if k2_ref is not None:
        ks = jnp.concatenate([ks, k2_ref[rows, :]], axis=1)
    return ks


def _attn_specs(cfg, hp, t, Tk, has_k2, by_q):
    g = cfg.H // hp
    if by_q:
        specs = [pl.BlockSpec((t, hp * cfg.dq), lambda h, i: (i, h)),
                 pl.BlockSpec((Tk, hp * cfg.dk1), lambda h, i: (0, h)),
                 pl.BlockSpec((Tk, hp * cfg.dv), lambda h, i: (0, g + h))]
        if has_k2:
            specs.append(pl.BlockSpec((Tk, LANES), lambda h, i: (0, 0)))
    else:
        specs = [None,
                 pl.BlockSpec((t, hp * cfg.dk1), lambda j, h: (j, h)),
                 pl.BlockSpec((t, hp * cfg.dv), lambda j, h: (j, g + h))]
        if has_k2:
            specs.append(pl.BlockSpec((t, LANES), lambda j, h: (j, 0)))
    return specs


def _mask(s, diagonal):
    if not diagonal:
        return s
    return jnp.where(lax.broadcasted_iota(jnp.int32, s.shape, 0) >= lax.broadcasted_iota(jnp.int32, s.shape, 1), s, NEG)


def flash_fwd(cfg, q, k1, v, k2, name):
    Tq, Tk = q.shape[0], k1.shape[0]
    t = min(cfg.blk, Tq, Tk)
    nkb = Tk // t
    has_k2 = k2 is not None
    hp = cfg.hp

    def body(*refs):
        q_ref, k1_ref, v_ref = refs[:3]
        k2_ref = refs[3] if has_k2 else None
        o_ref, lse_ref = refs[-2], refs[-1]
        i = pl.program_id(1)
        qs = [_cols(q_ref, slice(None), hh, cfg.dq) for hh in range(hp)]

        def step(j, carry, diagonal=False):
            rows = pl.ds(pl.multiple_of(j * t, t), t)
            out = []
            for hh in range(hp):
                m, l, acc = carry[hh]
                s = _mask(_dot_nt(qs[hh], _keys(cfg, k1_ref, k2_ref, rows, hh)) * cfg.scale, diagonal)
                m2 = jnp.maximum(m, jnp.max(s, axis=-1, keepdims=True))
                p = jnp.exp(s - m2)
                alpha = jnp.exp(m - m2)
                l2 = alpha * l + jnp.sum(p, axis=-1, keepdims=True)
                acc2 = alpha * acc + _dot_nn(p.astype(BF16), _cols(v_ref, rows, hh, cfg.dv))
                out.append((m2, l2, acc2))
            return tuple(out)

        init = tuple((jnp.full((t, 1), NEG, F32), jnp.zeros((t, 1), F32), jnp.zeros((t, cfg.dv), F32)) for _ in range(hp))
        res = lax.fori_loop(0, i if cfg.causal else nkb, step, init)
        if cfg.causal:
            res = step(i, res, True)
        for hh in range(hp):
            m, l, acc = res[hh]
            o_ref[:, hh * cfg.dv:(hh + 1) * cfg.dv] = (acc / l).astype(o_ref.dtype)
            lse_ref[hh] = m + jnp.log(l)

    args = [q, k1, v] + ([k2] if has_k2 else [])
    return pl.pallas_call(
        body, name=name, grid=(cfg.H // hp, Tq // t), in_specs=_attn_specs(cfg, hp, t, Tk, has_k2, True),
        out_specs=[pl.BlockSpec((t, hp * cfg.dv), lambda h, i: (i, h)), pl.BlockSpec((hp, t, 1), lambda h, i: (h, i, 0))],
        out_shape=[jax.ShapeDtypeStruct((Tq, cfg.H * cfg.dv), BF16), jax.ShapeDtypeStruct((cfg.H, Tq, 1), F32)],
        compiler_params=_cparams(("parallel", "parallel")),
    )(*args)


def flash_dq(cfg, q, k1, v, k2, o, do, lse, out_dtype, name, rope_tabs=None):
    Tq, Tk = q.shape[0], k1.shape[0]
    t = min(cfg.blk, Tq, Tk)
    nkb = Tk // t
    has_k2 = k2 is not None
    hp = cfg.hp
    n_tab = 0 if rope_tabs is None else len(rope_tabs)

    def body(*refs):
        q_ref, k1_ref, v_ref = refs[:3]
        k2_ref = refs[3] if has_k2 else None
        tab_refs = refs[len(refs) - 5 - n_tab:len(refs) - 5]
        o_ref, do_ref, lse_ref, dq_ref, dl_ref = refs[-5:]
        i = pl.program_id(1)
        qs = [_cols(q_ref, slice(None), hh, cfg.dq) for hh in range(hp)]
        dos = [_cols(do_ref, slice(None), hh, cfg.dv) for hh in range(hp)]
        lses = [lse_ref[hh] for hh in range(hp)]
        deltas = []
        for hh in range(hp):
            d = jnp.sum(dos[hh].astype(F32) * _cols(o_ref, slice(None), hh, cfg.dv).astype(F32), axis=-1, keepdims=True)
            dl_ref[hh] = d
            deltas.append(d)

        def step(j, dqs, diagonal=False):
            rows = pl.ds(pl.multiple_of(j * t, t), t)
            out = []
            for hh in range(hp):
                ks = _keys(cfg, k1_ref, k2_ref, rows, hh)
                s = _mask(_dot_nt(qs[hh], ks) * cfg.scale, diagonal)
                p = jnp.exp(s - lses[hh])
                dp = _dot_nt(dos[hh], _cols(v_ref, rows, hh, cfg.dv))
                ds = p * (dp - deltas[hh]) * cfg.scale
                out.append(dqs[hh] + _dot_nn(ds.astype(BF16), ks))
            return tuple(out)

        dqs = lax.fori_loop(0, i if cfg.causal else nkb, step, tuple(jnp.zeros((t, cfg.dq), F32) for _ in range(hp)))
        if cfg.causal:
            dqs = step(i, dqs, True)
        tabs = [r[...] for r in tab_refs]
        for hh in range(hp):
            dq = dqs[hh]
            if tabs:
                dq = jnp.concatenate([dq[:, :LANES], _rope_t(dq[:, LANES:], *tabs)], axis=1)
            dq_ref[:, hh * cfg.dq:(hh + 1) * cfg.dq] = dq.astype(dq_ref.dtype)

    ov = pl.BlockSpec((t, hp * cfg.dv), lambda h, i: (i, h))
    row1 = pl.BlockSpec((hp, t, 1), lambda h, i: (h, i, 0))
    tab_specs = [pl.BlockSpec((t, LANES), lambda h, i: (i, 0))] * n_tab
    args = [q, k1, v] + ([k2] if has_k2 else []) + list(rope_tabs or ()) + [o, do, lse]
    return pl.pallas_call(
        body, name=name, grid=(cfg.H // hp, Tq // t),
        in_specs=_attn_specs(cfg, hp, t, Tk, has_k2, True) + tab_specs + [ov, ov, row1],
        out_specs=[pl.BlockSpec((t, hp * cfg.dq), lambda h, i: (i, h)), row1],
        out_shape=[jax.ShapeDtypeStruct((Tq, cfg.H * cfg.dq), out_dtype), jax.ShapeDtypeStruct((cfg.H, Tq, 1), F32)],
        compiler_params=_cparams(("parallel", "parallel")),
    )(*args)


def flash_dkv(cfg, q, k1, v, k2, do, lse, delta, out_dtype, name):
    Tq, Tk = q.shape[0], k1.shape[0]
    t = min(cfg.blk, Tq, Tk)
    nqb = Tq // t
    has_k2 = k2 is not None
    hp = cfg.hp_kv

    def body(*refs):
        q_ref, k1_ref, v_ref = refs[:3]
        k2_ref = refs[3] if has_k2 else None
        n_in = 4 if has_k2 else 3
        do_ref, lse_ref, dl_ref = refs[n_in:n_in + 3]
        dk1_ref, dv_ref = refs[n_in + 3], refs[n_in + 4]
        j, h = pl.program_id(0), pl.program_id(1)
        kss = [_keys(cfg, k1_ref, k2_ref, slice(None), hh) for hh in range(hp)]
        vss = [_cols(v_ref, slice(None), hh, cfg.dv) for hh in range(hp)]

        def step(i, carry, diagonal=False):
            rows = pl.ds(pl.multiple_of(i * t, t), t)
            out = []
            for hh in range(hp):
                dk, dv = carry[hh]
                qi, doi = _cols(q_ref, rows, hh, cfg.dq), _cols(do_ref, rows, hh, cfg.dv)
                s = _dot_nt(kss[hh], qi) * cfg.scale
                if diagonal:
                    s = jnp.where(lax.broadcasted_iota(jnp.int32, s.shape, 0) <= lax.broadcasted_iota(jnp.int32, s.shape, 1), s, NEG)
                p = jnp.exp(s - lse_ref[hh, :, rows])
                dv = dv + _dot_nn(p.astype(BF16), doi)
                ds = p * (_dot_nt(vss[hh], doi) - dl_ref[hh, :, rows]) * cfg.scale
                dk = dk + _dot_nn(ds.astype(BF16), qi)
                out.append((dk, dv))
            return tuple(out)

        init = tuple((jnp.zeros((t, cfg.dq), F32), jnp.zeros((t, cfg.dv), F32)) for _ in range(hp))
        if cfg.causal:
            res = lax.fori_loop(j + 1, nqb, step, step(j, init, True))
        else:
            res = lax.fori_loop(0, nqb, step, init)
        for hh in range(hp):
            dk, dv = res[hh]
            dv_ref[:, hh * cfg.dv:(hh + 1) * cfg.dv] = dv.astype(dv_ref.dtype)
            dk1_ref[:, hh * cfg.dk1:(hh + 1) * cfg.dk1] = dk[:, 0:cfg.dk1].astype(dk1_ref.dtype)
        if has_k2:
            dk2_ref = refs[n_in + 5]

            @pl.when(h == 0)
            def _():
                dk2_ref[...] = jnp.zeros_like(dk2_ref)

            for hh in range(hp):
                dk2_ref[...] += res[hh][0][:, cfg.dk1:]

    specs = _attn_specs(cfg, hp, t, Tk, has_k2, False)
    specs[0] = pl.BlockSpec((Tq, hp * cfg.dq), lambda j, h: (0, h))
    rows_all = pl.BlockSpec((hp, 1, Tq), lambda j, h: (h, 0, 0))
    specs += [pl.BlockSpec((Tq, hp * cfg.dv), lambda j, h: (0, h)), rows_all, rows_all]
    args = [q, k1, v] + ([k2] if has_k2 else []) + [do, lse.reshape(cfg.H, 1, Tq), delta.reshape(cfg.H, 1, Tq)]
    out_specs = [pl.BlockSpec((t, hp * cfg.dk1), lambda j, h: (j, h)), pl.BlockSpec((t, hp * cfg.dv), lambda j, h: (j, h))]
    out_shape = [jax.ShapeDtypeStruct((Tk, cfg.H * cfg.dk1), out_dtype), jax.ShapeDtypeStruct((Tk, cfg.H * cfg.dv), out_dtype)]
    if has_k2:
        out_specs.append(pl.BlockSpec((t, LANES), lambda j, h: (j, 0)))
        out_shape.append(jax.ShapeDtypeStruct((Tk, LANES), F32))
    return pl.pallas_call(
        body, name=name, grid=(Tk // t, cfg.H // hp), in_specs=specs, out_specs=out_specs, out_shape=out_shape,
        compiler_params=_cparams(("parallel", "arbitrary")),
    )(*args)


def _shift_down(x, s):
    if s == 0:
        return x
    t = lax.broadcasted_iota(jnp.int32, x.shape, 0)
    return jnp.where(t >= s, pltpu.roll(x, s, 0), 0.0)


def _shift_up(x, s):
    if s == 0:
        return x
    n = x.shape[0]
    t = lax.broadcasted_iota(jnp.int32, x.shape, 0)
    return jnp.where(t < n - s, pltpu.roll(x, n - s, 0), 0.0)


def _conv(x, w_ref, kw):
    y = x * w_ref[kw - 1:kw, :]
    for j in range(kw - 1):
        y = y + _shift_down(x, kw - 1 - j) * w_ref[j:j + 1, :]
    return y


def _conv_t(d, w_ref, kw):
    y = d * w_ref[kw - 1:kw, :]
    for j in range(kw - 1):
        y = y + _shift_up(d, kw - 1 - j) * w_ref[j:j + 1, :]
    return y


def _conv_dw(d, x, kw):
    rows = lax.broadcasted_iota(jnp.int32, (kw, d.shape[1]), 0)
    dw = jnp.zeros((kw, d.shape[1]), F32)
    for j in range(kw):
        r = jnp.sum(d * _shift_down(x, kw - 1 - j), axis=0, keepdims=True)
        dw = jnp.where(rows == j, r, dw)
    return dw


def _silu(x):
    return x * jax.nn.sigmoid(x)


def _silu_grad(x):
    s = jax.nn.sigmoid(x)
    return s * (1.0 + x * (1.0 - s))


def gdn_conv_fwd(z, w, name, tc=256):
    T, C = z.shape[0], w.shape[1]
    kw = w.shape[0]

    def body(x_ref, w_ref, o_ref):
        o_ref[...] = _silu(_conv(x_ref[...], w_ref, kw))

    return pl.pallas_call(
        body, name=name, grid=(C // tc,),
        in_specs=[pl.BlockSpec((T, tc), lambda j: (0, j)), pl.BlockSpec((kw, tc), lambda j: (0, j))],
        out_specs=pl.BlockSpec((T, tc), lambda j: (0, j)),
        out_shape=jax.ShapeDtypeStruct((T, C), F32), compiler_params=_cparams(("parallel",)),
    )(z, w)


def gdn_conv_bwd(z, w, dy, name, tc=256):
    T, C = z.shape[0], w.shape[1]
    kw = w.shape[0]

    def body(x_ref, w_ref, dy_ref, dx_ref, dw_ref):
        xv = x_ref[...]
        dc = dy_ref[...] * _silu_grad(_conv(xv, w_ref, kw))
        dx_ref[...] = _conv_t(dc, w_ref, kw).astype(dx_ref.dtype)
        dw_ref[...] = _conv_dw(dc, xv, kw)

    col = lambda j: (0, j)
    return pl.pallas_call(
        body, name=name, grid=(C // tc,),
        in_specs=[pl.BlockSpec((T, tc), col), pl.BlockSpec((kw, tc), col), pl.BlockSpec((T, tc), col)],
        out_specs=[pl.BlockSpec((T, tc), col), pl.BlockSpec((kw, tc), col)],
        out_shape=[jax.ShapeDtypeStruct((T, C), BF16), jax.ShapeDtypeStruct((kw, C), F32)],
        compiler_params=_cparams(("parallel",)),
    )(z, w, dy)


def sc_fwd(z, w, name, tc=256):
    T, C = z.shape[0], w.shape[1]
    kw, nb = w.shape[0], C // tc

    def body(b_ref, c_ref, u_ref, w_ref, o_ref):
        o_ref[...] = (b_ref[...] * _conv(c_ref[...] * u_ref[...], w_ref, kw)).astype(o_ref.dtype)

    return pl.pallas_call(
        body, name=name, grid=(nb,),
        in_specs=[pl.BlockSpec((T, tc), lambda j: (0, j)), pl.BlockSpec((T, tc), lambda j: (0, nb + j)),
                  pl.BlockSpec((T, tc), lambda j: (0, 2 * nb + j)), pl.BlockSpec((kw, tc), lambda j: (0, j))],
        out_specs=pl.BlockSpec((T, tc), lambda j: (0, j)),
        out_shape=jax.ShapeDtypeStruct((T, C), BF16), compiler_params=_cparams(("parallel",)),
    )(z, z, z, w)


def sc_bwd(z, w, dy, name, tc=256):
    T, C = z.shape[0], w.shape[1]
    kw, nb = w.shape[0], C // tc

    def body(b_ref, c_ref, u_ref, w_ref, dy_ref, db_ref, dc_ref, du_ref, dw_ref):
        cv, uv, dyv = c_ref[...], u_ref[...], dy_ref[...]
        cu = cv * uv
        db_ref[...] = (dyv * _conv(cu, w_ref, kw)).astype(db_ref.dtype)
        dcv = dyv * b_ref[...]
        dcu = _conv_t(dcv, w_ref, kw)
        dc_ref[...] = (dcu * uv).astype(dc_ref.dtype)
        du_ref[...] = (dcu * cv).astype(du_ref.dtype)
        dw_ref[...] = _conv_dw(dcv, cu, kw)

    col = lambda j: (0, j)
    act = jax.ShapeDtypeStruct((T, C), BF16)
    return pl.pallas_call(
        body, name=name, grid=(nb,),
        in_specs=[pl.BlockSpec((T, tc), col), pl.BlockSpec((T, tc), lambda j: (0, nb + j)),
                  pl.BlockSpec((T, tc), lambda j: (0, 2 * nb + j)), pl.BlockSpec((kw, tc), col), pl.BlockSpec((T, tc), col)],
        out_specs=[pl.BlockSpec((T, tc), col)] * 3 + [pl.BlockSpec((kw, tc), col)],
        out_shape=[act, act, act, jax.ShapeDtypeStruct((kw, C), F32)],
        compiler_params=_cparams(("parallel",)),
    )(z, z, z, w, dy)


def _hdot(a, b, dims):
    a_hi, b_hi = a.astype(BF16), b.astype(BF16)
    a_lo, b_lo = (a - a_hi.astype(F32)).astype(BF16), (b - b_hi.astype(F32)).astype(BF16)
    dot = lambda x, y: lax.dot_general(x, y, (dims, ((), ())), preferred_element_type=F32)
    return dot(a_hi, b_hi) + (dot(a_hi, b_lo) + dot(a_lo, b_hi))


def _bdot(a, b, dims):
    return lax.dot_general(a.astype(BF16), b.astype(BF16), (dims, ((), ())), preferred_element_type=F32)


_NN, _NT, _TN = ((1,), (0,)), ((1,), (1,)), ((0,), (0,))


def _per_head_dots(dot2d):
    def stacked(a, b, dims):
        return jnp.stack([dot2d(a[h], b[h], dims) for h in range(a.shape[0])])

    @jax.custom_vjp
    def nn(a, b):
        return stacked(a, b, _NN)

    @jax.custom_vjp
    def nt(a, b):
        return stacked(a, b, _NT)

    @jax.custom_vjp
    def tn(a, b):
        return stacked(a, b, _TN)

    nn.defvjp(lambda a, b: (nn(a, b), (a, b)), lambda r, d: (stacked(d, r[1], _NT), stacked(r[0], d, _TN)))
    nt.defvjp(lambda a, b: (nt(a, b), (a, b)), lambda r, d: (stacked(d, r[1], _NN), stacked(d, r[0], _TN)))
    tn.defvjp(lambda a, b: (tn(a, b), (a, b)), lambda r, d: (stacked(r[1], d, _NT), stacked(r[0], d, _NN)))
    return nn, nt, tn


_hnn, _hnt, _htn = _per_head_dots(_hdot)
_bnn, _bnt, _btn = _per_head_dots(_bdot)


@jax.custom_vjp
def _unit_lower_inverse(m):
    c = m.shape[-1]
    eye = (lax.broadcasted_iota(jnp.int32, (c, c), 0) == lax.broadcasted_iota(jnp.int32, (c, c), 1)).astype(F32)
    t = eye - m
    p = _hnn(m, m)
    n = 2
    while n < c:
        t = t + _hnn(t, p)
        n *= 2
        if n < c:
            p = _hnn(p, p)
    return t


def _uli_fwd(m):
    t = _unit_lower_inverse(m)
    return t, t


def _uli_bwd(t, dt):
    return (-_htn(t, _hnt(dt, t)),)


_unit_lower_inverse.defvjp(_uli_fwd, _uli_bwd)


@jax.custom_vjp
def _known_inverse(m, t):
    return t


_known_inverse.defvjp(lambda m, t: (t, t), lambda t, dt: (_uli_bwd(t, dt)[0], jnp.zeros_like(t)))


def _gdn_chunk(q, k, v, gate, bl, al, a_log, dt_bias, o_norm, st, t_known=None):
    nh, c = q.shape[0], q.shape[1]
    ii = lax.broadcasted_iota(jnp.int32, (c, c), 0)
    jj = lax.broadcasted_iota(jnp.int32, (c, c), 1)
    tri, strict = ii >= jj, ii > jj
    q = q * lax.rsqrt(jnp.sum(q * q, -1, keepdims=True) + EPS) * (GDN_D ** -0.5)
    k = k * lax.rsqrt(jnp.sum(k * k, -1, keepdims=True) + EPS)
    beta = jax.nn.sigmoid(bl)
    g = -jnp.exp(a_log) * jax.nn.softplus(al + dt_bias)
    gc = _hnn(jnp.broadcast_to(tri.astype(F32), (nh, c, c)), g)
    gcol = _hnn(gc, jnp.full((nh, LANES, c), 1.0 / LANES, F32))
    grow = _hnt(jnp.full((nh, c, LANES), 1.0 / LANES, F32), gc)
    decay = jnp.where(tri, jnp.exp(jnp.where(tri, gcol - grow, 0.0)), 0.0)
    kb = k * beta
    m = jnp.where(strict, _bnt(kb, k) * decay, 0.0)
    t_inv = _unit_lower_inverse(m) if t_known is None else _known_inverse(m, t_known)
    eg = jnp.exp(gc)
    u = _bnn(t_inv, v * beta)
    w = _bnn(t_inv, kb * eg)
    attn = _bnt(q, k) * decay
    v_new = u - _bnn(w, st)
    o = _bnn(q * eg, st) + _bnn(attn, v_new)
    g_last = jnp.sum(g, axis=1, keepdims=True)
    st_new = st * jnp.exp(g_last) + _btn(k * jnp.exp(g_last - gc), v_new)
    o = o * lax.rsqrt(jnp.mean(o * o, -1, keepdims=True) + EPS) * o_norm
    return o * _silu(gate), st_new, t_inv


GDN_HP = 8
_GW = GDN_HP * GDN_D
_GB = GDN_H // GDN_HP


def _gdn_specs(n_chunks, rev):
    def tok(col):
        if rev:
            return pl.BlockSpec((GDN_C, _GW), lambda h, n: (n_chunks - 1 - n, col + h))
        return pl.BlockSpec((GDN_C, _GW), lambda h, n: (n, col + h))
    par = pl.BlockSpec((1, _GW), lambda h, n: (0, h))
    shared = pl.BlockSpec((1, GDN_D), lambda h, n: (0, 0))
    if rev:
        st = pl.BlockSpec((GDN_HP, None, GDN_D, GDN_D), lambda h, n: (h, n_chunks - 1 - n, 0, 0))
    else:
        st = pl.BlockSpec((GDN_HP, None, GDN_D, GDN_D), lambda h, n: (h, n, 0, 0))
    return tok, par, shared, st


def _heads(ref):
    return jnp.stack([ref[:, h * GDN_D:(h + 1) * GDN_D] for h in range(ref.shape[1] // GDN_D)])


def gdn_chunk_fwd(qkv, z, a_log_x, dt_bias_x, o_norm, name):
    T = qkv.shape[0]
    n_chunks = T // GDN_C
    H = GDN_H
    tok, par, shared, st_spec = _gdn_specs(n_chunks, False)

    def body(q_ref, k_ref, v_ref, g_ref, bl_ref, al_ref, a_ref, dt_ref, on_ref, o_ref, st_ref, ti_ref, state):
        @pl.when(pl.program_id(1) == 0)
        def _():
            state[...] = jnp.zeros_like(state)

        st = state[...]
        st_ref[...] = st
        o, st_new, t_inv = _gdn_chunk(_heads(q_ref), _heads(k_ref), _heads(v_ref), _heads(g_ref), _heads(bl_ref), _heads(al_ref),
                                      _heads(a_ref), _heads(dt_ref), on_ref[...], st)
        for hh in range(GDN_HP):
            o_ref[:, hh * GDN_D:(hh + 1) * GDN_D] = o[hh].astype(o_ref.dtype)
        ti_ref[...] = t_inv
        state[...] = st_new

    B = _GB
    return pl.pallas_call(
        body, name=name, grid=(B, n_chunks),
        in_specs=[tok(0), tok(B), tok(2 * B), tok(3 * B), tok(4 * B), tok(5 * B), par, par, shared],
        out_specs=[tok(0), st_spec, pl.BlockSpec((GDN_HP, None, GDN_C, GDN_C), lambda h, n: (h, n, 0, 0))],
        out_shape=[jax.ShapeDtypeStruct((T, H * GDN_D), BF16), jax.ShapeDtypeStruct((H, n_chunks, GDN_D, GDN_D), F32),
                   jax.ShapeDtypeStruct((H, n_chunks, GDN_C, GDN_C), F32)],
        scratch_shapes=[pltpu.VMEM((GDN_HP, GDN_D, GDN_D), F32)],
        compiler_params=_cparams(("parallel", "arbitrary")),
    )(qkv, qkv, qkv, z, z, z, a_log_x, dt_bias_x, o_norm)


def gdn_chunk_bwd(qkv, z, a_log_x, dt_bias_x, o_norm, states, t_invs, do, name):
    T = qkv.shape[0]
    n_chunks = T // GDN_C
    H = GDN_H
    tok, par, shared, st_spec = _gdn_specs(n_chunks, True)

    def body(q_ref, k_ref, v_ref, g_ref, bl_ref, al_ref, a_ref, dt_ref, on_ref, st_ref, ti_ref, do_ref,
             dq_ref, dk_ref, dv_ref, dg_ref, dba_ref, da_ref, ddt_ref, don_ref, dstate):
        h, n = pl.program_id(0), pl.program_id(1)

        @pl.when(n == 0)
        def _():
            dstate[...] = jnp.zeros_like(dstate)
            da_ref[...] = jnp.zeros_like(da_ref)
            ddt_ref[...] = jnp.zeros_like(ddt_ref)

        @pl.when((n == 0) & (h == 0))
        def _():
            don_ref[...] = jnp.zeros_like(don_ref)

        t_known = ti_ref[...]
        _, vjp = jax.vjp(lambda *ins: _gdn_chunk(*ins, t_known=t_known)[:2],
                         _heads(q_ref), _heads(k_ref), _heads(v_ref), _heads(g_ref), _heads(bl_ref), _heads(al_ref),
                         _heads(a_ref), _heads(dt_ref), on_ref[...], st_ref[...])
        dq, dk, dv, dg, dbl, dal, da, ddt, don, dst = vjp((_heads(do_ref).astype(F32), dstate[...]))
        lane = lax.broadcasted_iota(jnp.int32, (GDN_C, LANES), 1)
        dba = jnp.zeros((GDN_C, LANES), F32)
        for hh in range(GDN_HP):
            cols = slice(hh * GDN_D, (hh + 1) * GDN_D)
            dq_ref[:, cols] = dq[hh]
            dk_ref[:, cols] = dk[hh]
            dv_ref[:, cols] = dv[hh]
            dg_ref[:, cols] = dg[hh].astype(dg_ref.dtype)
            dba = jnp.where(lane == hh, jnp.sum(dbl[hh], axis=-1, keepdims=True), dba)
            dba = jnp.where(lane == H + hh, jnp.sum(dal[hh], axis=-1, keepdims=True), dba)
            da_ref[:, cols] += da[hh]
            ddt_ref[:, cols] += ddt[hh]
        dba_ref[...] = dba.astype(dba_ref.dtype)
        don_ref[...] += don
        dstate[...] = dst

    tok0 = tok(0)
    B = _GB
    f32_tok = jax.ShapeDtypeStruct((T, H * GDN_D), F32)
    bf_tok = jax.ShapeDtypeStruct((T, H * GDN_D), BF16)
    par_sh = jax.ShapeDtypeStruct((1, H * GDN_D), F32)
    return pl.pallas_call(
        body, name=name, grid=(B, n_chunks),
        in_specs=[tok(0), tok(B), tok(2 * B), tok(3 * B), tok(4 * B), tok(5 * B), par, par, shared, st_spec,
                  pl.BlockSpec((GDN_HP, None, GDN_C, GDN_C), lambda h, n: (h, n_chunks - 1 - n, 0, 0)), tok0],
        out_specs=[tok0] * 4 + [pl.BlockSpec((GDN_C, LANES), lambda h, n: (n_chunks - 1 - n, 0)), par, par, shared],
        out_shape=[f32_tok, f32_tok, f32_tok, bf_tok, jax.ShapeDtypeStruct((T, LANES), BF16), par_sh, par_sh,
                   jax.ShapeDtypeStruct((1, GDN_D), F32)],
        scratch_shapes=[pltpu.VMEM((GDN_HP, GDN_D, GDN_D), F32)],
        compiler_params=_cparams(("arbitrary", "arbitrary")),
    )(qkv, qkv, qkv, z, z, z, a_log_x, dt_bias_x, o_norm, states, t_invs, do)


def _prefetch_call(body, name, grid, in_specs, out_specs, out_shape, aliases=None):
    return pl.pallas_call(
        body, name=name,
        grid_spec=pltpu.PrefetchScalarGridSpec(num_scalar_prefetch=1, grid=grid, in_specs=in_specs, out_specs=out_specs),
        out_shape=out_shape, input_output_aliases=aliases or {},
        compiler_params=_cparams(("parallel",) * len(grid)))


def cast_into(src, src_row0, rows, slab, row0, me, name):
    width = src.shape[1]
    tr = _pick_rows(rows, 1024, row0, src_row0)
    assert rows % tr == 0 and row0 % tr == 0 and src_row0 % tr == 0

    def body(me_ref, s_ref, *refs):
        refs[-1][...] = s_ref[...].astype(refs[-1].dtype)

    in_specs = [pl.BlockSpec((tr, width), lambda r, me_ref: (src_row0 // tr + r, 0))]
    args = [src]
    aliases = {}
    if slab.arr is not None:
        in_specs.append(_ANY)
        args.append(slab.arr)
        aliases = {2: 0}
    slab.arr = _prefetch_call(
        body, name, (rows // tr,), in_specs,
        pl.BlockSpec((None, tr, width), lambda r, me_ref: (me_ref[0], row0 // tr + r, 0)),
        jax.ShapeDtypeStruct(slab.shape, slab.dtype), aliases)(me, *args)


def pair_add(g, b, c_idx, name):
    n, rh, w = b.shape
    tr = _pick_rows(rh, 1024)
    nb = rh // tr

    def body(c_ref, g_ref, b_ref, o_ref):
        o_ref[...] = (g_ref[...].astype(F32) + b_ref[...].astype(F32)).astype(o_ref.dtype)

    return _prefetch_call(
        body, name, (n, nb),
        [pl.BlockSpec((None, tr, w), lambda k, r, c: (k, c[0] * nb + r, 0)), pl.BlockSpec((None, tr, w), lambda k, r, c: (k, r, 0))],
        pl.BlockSpec((None, tr, w), lambda k, r, c: (k, r, 0)), jax.ShapeDtypeStruct(b.shape, BF16))(c_idx, g, b)


def chip_sum(p, rv, mc, name):
    n, rh, w = p.shape
    tr = _pick_rows(rh, 512)
    nb = rh // tr

    def body(mc_ref, p_ref, rv_ref, o_ref):
        me = mc_ref[0]
        acc = None
        for k in range(n):
            part = jnp.where(me == k, p_ref[...], rv_ref[k]).astype(F32)
            acc = part if acc is None else acc + part
        o_ref[...] = acc

    return _prefetch_call(
        body, name, (nb,),
        [pl.BlockSpec((None, tr, w), lambda r, mc_ref: (mc_ref[0], r, 0)), pl.BlockSpec((n, tr, w), lambda r, mc_ref: (0, r, 0))],
        pl.BlockSpec((tr, w), lambda r, mc_ref: (mc_ref[1] * nb + r, 0)), jax.ShapeDtypeStruct((2 * rh, w), F32))(mc, p, rv)


def adamw(red, row0, w, m, v, w_row0, rows, prev, name):
    cols = w.shape[1]
    tr = _pick_rows(rows, 512, row0, w_row0)
    assert rows % tr == 0 and row0 % tr == 0 and w_row0 % tr == 0

    def body(g_ref, w_ref, m_ref, v_ref, *refs):
        go_ref, d_ref, nm_ref, nv_ref = refs[-4:]
        gv = g_ref[...]
        nm = ADAM_B1 * m_ref[...] + (1.0 - ADAM_B1) * gv
        nv = ADAM_B2 * v_ref[...] + (1.0 - ADAM_B2) * (gv * gv)
        m_hat = nm / (1.0 - ADAM_B1 ** ADAM_STEP)
        v_hat = nv / (1.0 - ADAM_B2 ** ADAM_STEP)
        go_ref[...] = gv
        d_ref[...] = -ADAM_LR * (m_hat / (jnp.sqrt(v_hat) + ADAM_EPS) + ADAM_WD * w_ref[...])
        nm_ref[...] = nm
        nv_ref[...] = nv

    spec = pl.BlockSpec((tr, cols), lambda r: (w_row0 // tr + r, 0))
    sh = jax.ShapeDtypeStruct(w.shape, F32)
    in_specs = [pl.BlockSpec((tr, cols), lambda r: (row0 // tr + r, 0)), spec, spec, spec]
    args, aliases = [red, w, m, v], {}
    if prev is not None:
        in_specs += [_ANY] * 4
        args += list(prev)
        aliases = {4 + k: k for k in range(4)}
    return pl.pallas_call(
        body, name=name, grid=(rows // tr,), in_specs=in_specs, out_specs=[spec] * 4, out_shape=[sh] * 4,
        input_output_aliases=aliases, compiler_params=_cparams(("parallel",)),
    )(*args)


def _place():
    x, y, c = lax.axis_index("x"), lax.axis_index("y"), lax.axis_index("c")
    chips = [(1 - x, y), (x, 1 - y), (1 - x, 1 - y)]
    return x, y, c, chips


def _chip_index(cx, cy):
    return 2 * cx + cy


def _remote(src, dst, send_sem, recv_sem, to):
    return pltpu.make_async_remote_copy(src_ref=src, dst_ref=dst, send_sem=send_sem, recv_sem=recv_sem,
                                        device_id=to, device_id_type=MESH)


def _comm_call(body, name, ins, out_shapes, n_sems, aliases):
    return pl.pallas_call(
        body, name=name, in_specs=[_ANY] * len(ins), out_specs=[_ANY] * len(out_shapes), out_shape=out_shapes,
        scratch_shapes=[pltpu.SemaphoreType.DMA((n_sems,)), pltpu.SemaphoreType.DMA((n_sems,))],
        input_output_aliases=aliases,
    )(*ins)


def pair_swap_halves(slabs, name="grad_pair_swap"):
    n = len(slabs)

    def body(*refs):
        in_refs, out_refs, send_sems, recv_sems = refs[:n], refs[n:2 * n], refs[-2], refs[-1]
        x, y, c, _ = _place()
        cps = []
        for a in range(n):
            rh = in_refs[a].shape[1] // 2
            cp = _remote(in_refs[a].at[:, pl.ds((1 - c) * rh, rh), :], out_refs[a], send_sems.at[a], recv_sems.at[a], (x, y, 1 - c))
            cp.start()
            cps.append(cp)
        for cp in cps:
            cp.wait()

    outs = [jax.ShapeDtypeStruct((s.shape[0], s.shape[1] // 2, s.shape[2]), s.dtype) for s in slabs]
    return _comm_call(body, name, slabs, outs, n, {})


def pair_join_halves(reds, name="grad_pair_join"):
    n = len(reds)

    def body(*refs):
        in_refs, out_refs, send_sems, recv_sems = refs[:n], refs[n:2 * n], refs[-2], refs[-1]
        x, y, c, _ = _place()
        cps = []
        for a in range(n):
            rh = in_refs[a].shape[0] // 2
            mine = pl.ds(c * rh, rh)
            cp = _remote(in_refs[a].at[mine], out_refs[a].at[mine], send_sems.at[a], recv_sems.at[a], (x, y, 1 - c))
            cp.start()
            cps.append(cp)
        for a in range(n):
            rh = in_refs[a].shape[0] // 2
            got = out_refs[a].at[pl.ds((1 - c) * rh, rh)]
            _remote(got, got, send_sems.at[a], recv_sems.at[a], (x, y, 1 - c)).wait_recv()
        for cp in cps:
            cp.wait_send()

    return _comm_call(body, name, reds, [jax.ShapeDtypeStruct(r.shape, r.dtype) for r in reds], n, {a: a for a in range(n)})


_HBM = pl.BlockSpec(memory_space=pltpu.HBM)
_SEM = pl.BlockSpec(memory_space=pltpu.SEMAPHORE)
_EFFECT = pltpu.SideEffectType.DATAFLOW_SIDE_EFFECTING


def _in_hbm(a):
    return pltpu.with_memory_space_constraint(a, pltpu.HBM)


def _hbm_like(a):
    return pltpu.HBM(a.shape, a.dtype)


def _start_call(body, name, ins, n_sems, after):
    n = len(ins)
    res = pl.pallas_call(
        body, name=name, in_specs=[_HBM] * n + [_ANY],
        out_specs=[_SEM, _SEM] + [_HBM] * n + [pl.BlockSpec(memory_space=pltpu.VMEM)],
        out_shape=[pltpu.SemaphoreType.DMA((n_sems,)), pltpu.SemaphoreType.DMA((n_sems,))] + [_hbm_like(a) for a in ins]
        + [jax.ShapeDtypeStruct((8, LANES), F32)],
        input_output_aliases={a: 2 + a for a in range(n)},
        compiler_params=pltpu.CompilerParams(has_side_effects=_EFFECT),
    )(*[_in_hbm(a) for a in ins], after)
    return res[0], res[1], list(res[2:2 + n]), res[-1]


def _wait_call(body, name, thru, send_sems, recv_sems, after):
    n = len(thru)
    after = list(after) if isinstance(after, (list, tuple)) else [after]
    return pl.pallas_call(
        body, name=name, in_specs=[_HBM] * n + [_SEM, _SEM] + [_ANY] * len(after), out_specs=[_HBM] * n,
        out_shape=[_hbm_like(a) for a in thru], input_output_aliases={a: a for a in range(n)},
        compiler_params=pltpu.CompilerParams(has_side_effects=_EFFECT),
    )(*thru, send_sems, recv_sems, *after)


def gather_start(slabs, after, name="weight_gather_start"):
    n = len(slabs)

    def body(*refs):
        g_refs, send_sems, recv_sems, token = refs[:n], refs[n + 1], refs[n + 2], refs[-1]
        x, y, c, chips = _place()
        me = _chip_index(x, y)
        for a in range(n):
            rh = g_refs[a].shape[1] // 2
            mine = g_refs[a].at[me, pl.ds(c * rh, rh)]
            for j, chip in enumerate(chips):
                _remote(mine, mine, send_sems.at[3 * a + j], recv_sems.at[3 * a + j], (*chip, c)).start()
        token[...] = jnp.zeros_like(token)

    return _start_call(body, name, slabs, 3 * n, after)


def gather_wait(send_sems, recv_sems, thru, after, name="weight_gather_wait"):
    n = len(thru)

    def body(*refs):
        g_refs, send_sems, recv_sems = refs[:n], refs[n], refs[n + 1]
        x, y, c, chips = _place()
        me = _chip_index(x, y)
        for a in range(n):
            rh = g_refs[a].shape[1] // 2
            rows = pl.ds(c * rh, rh)
            for j, chip in enumerate(chips):
                mine, got = g_refs[a].at[me, rows], g_refs[a].at[_chip_index(*chip), rows]
                _remote(mine, mine, send_sems.at[3 * a + j], recv_sems.at[3 * a + j], (*chip, c)).wait_send()
                _remote(got, got, send_sems.at[3 * a + j], recv_sems.at[3 * a + j], (*chip, c)).wait_recv()

    return _wait_call(body, name, thru, send_sems, recv_sems, after)


def gather_forward(slabs, name="weight_gather_forward"):
    n = len(slabs)

    def body(*refs):
        in_refs, out_refs, send_sems, recv_sems = refs[:n], refs[n:2 * n], refs[-2], refs[-1]
        x, y, c, chips = _place()
        sib = (x, y, 1 - c)
        sends = []
        for a in range(n):
            rh = in_refs[a].shape[1] // 2
            for j, chip in enumerate(chips):
                k = _chip_index(*chip)
                cp = _remote(in_refs[a].at[k, pl.ds(c * rh, rh)], out_refs[a].at[k, pl.ds(c * rh, rh)], send_sems.at[3 * a + j],
                             recv_sems.at[3 * a + j], sib)
                cp.start()
                sends.append(cp)
        for a in range(n):
            rh = in_refs[a].shape[1] // 2
            for j, chip in enumerate(chips):
                got = out_refs[a].at[_chip_index(*chip), pl.ds((1 - c) * rh, rh)]
                _remote(got, got, send_sems.at[3 * a + j], recv_sems.at[3 * a + j], sib).wait_recv()
        for cp in sends:
            cp.wait_send()

    return _comm_call(body, name, slabs, [jax.ShapeDtypeStruct(s.shape, s.dtype) for s in slabs], 3 * n, {a: a for a in range(n)})


def exchange_start(parts, after, name="grad_exchange_start"):
    n = len(parts)

    def body(*refs):
        p_refs, land_refs, send_sems, recv_sems, token = refs[:n], refs[n:2 * n], refs[2 * n + 1], refs[2 * n + 2], refs[-1]
        x, y, c, chips = _place()
        me = _chip_index(x, y)
        for a in range(n):
            for j, chip in enumerate(chips):
                _remote(p_refs[a].at[_chip_index(*chip)], land_refs[a].at[me], send_sems.at[3 * a + j], recv_sems.at[3 * a + j],
                        (*chip, c)).start()
        token[...] = jnp.zeros_like(token)

    return _start_call(body, name, list(parts) + [lax.empty(p.shape, p.dtype) for p in parts], 3 * n, after)


def exchange_wait(send_sems, recv_sems, thru, after, name="grad_exchange_wait"):
    n = len(thru) // 2

    def body(*refs):
        p_refs, land_refs, send_sems, recv_sems = refs[:n], refs[n:2 * n], refs[2 * n], refs[2 * n + 1]
        x, y, c, chips = _place()
        me = _chip_index(x, y)
        for a in range(n):
            for j, chip in enumerate(chips):
                k = _chip_index(*chip)
                _remote(p_refs[a].at[k], land_refs[a].at[me], send_sems.at[3 * a + j], recv_sems.at[3 * a + j], (*chip, c)).wait_send()
                _remote(land_refs[a].at[k], land_refs[a].at[k], send_sems.at[3 * a + j], recv_sems.at[3 * a + j], (*chip, c)).wait_recv()

    res = _wait_call(body, name, thru, send_sems, recv_sems, after)
    return res[:n], res[n:]


_SLABS = {
    "mla_w_in": [("mla_w_in", 1, 0, 2)], "mla_w_uq": [("mla_w_uq", 2, 0, 2)], "mla_w_ukv": [("mla_w_ukv", 2, 0, 2)],
    "l0_mla_w_o": [("mla_w_o", 1, 0, 1)],
    "l0_w1024": [("mlp_w1", 2, 0, 1), ("mlp_w2", 1, 0, 1), ("xa_w_q", 1, 0, 1), ("xa_w_o", 1, 0, 1)],
    "l0_xa_w_kv": [("xa_w_kv", 2, 0, 1)],
    "l1_w1024": [("mlp_w1", 2, 1, 2), ("mlp_w2", 1, 1, 2), ("xa_w_q", 1, 1, 2), ("xa_w_o", 1, 1, 2), ("gdn_w_o", 1, 0, 1)],
    "l1_xa_w_kv": [("xa_w_kv", 2, 1, 2)], "gdn_w_in": [("gdn_w_in", 2, 0, 1)],
    "l23_w1024": [("mlp_w1", 2, 2, 4), ("mlp_w2", 1, 2, 4), ("xa_w_q", 1, 2, 4), ("xa_w_o", 1, 2, 4), ("mla_w_o", 1, 1, 2),
                  ("sc_w_o", 1, 0, 1)],
    "l23_xa_w_kv": [("xa_w_kv", 2, 2, 4)], "sc_w_in": [("sc_w_in", 2, 0, 1)],
}
_GROUPS = [(["mla_w_in", "mla_w_uq", "mla_w_ukv", "l0_mla_w_o"], None),
           (["l0_w1024", "l0_xa_w_kv"], (0, "xa")),
           (["l1_w1024", "l1_xa_w_kv", "gdn_w_in"], (1, "mix")),
           (["l23_w1024", "l23_xa_w_kv", "sc_w_in"], (2, "mix"))]
_RELAID = ("mla_w_in", "mla_w_uq", "mla_w_ukv", "gdn_w_in")
_SMALL = [("mla_q_norm", 1), ("mla_kv_norm", 1), ("gdn_conv_w", 2), ("sc_conv_w", 2)]
_REPL = ["gdn_a_log", "gdn_dt_bias", "gdn_o_norm", "norm_mix", "norm_mem", "norm_mlp", "mem_norm", "final_norm"]
_WEIGHTS = ['mla_w_in', 'mla_q_norm', 'mla_kv_norm', 'mla_w_uq', 'mla_w_ukv', 'mla_w_o', 'gdn_w_in', 'gdn_conv_w',
            'gdn_a_log', 'gdn_dt_bias', 'gdn_o_norm', 'gdn_w_o', 'sc_w_in', 'sc_conv_w', 'sc_w_o', 'norm_mix',
            'norm_mem', 'norm_mlp', 'xa_w_q', 'xa_w_kv', 'xa_w_o', 'mlp_w1', 'mlp_w2', 'mem_norm', 'final_norm']


class Layout:
    def __init__(self, shard_shapes):
        self.members, self.where, self.slab_dims = {}, {}, {}
        for slab, members in _SLABS.items():
            off, rows = 0, []
            for name, axis, l0, l1 in members:
                _, rpl, width = shard_shapes[name]
                rows.append((name, off, l0, l1, rpl))
                for layer in range(l0, l1):
                    self.where[(name, layer)] = (slab, off + (layer - l0) * rpl, rpl, width, axis)
                off += (l1 - l0) * rpl
            self.members[slab], self.slab_dims[slab] = rows, (off, width)

    def new_slabs(self, dtype):
        return {s: Slab(rows, width, dtype) for s, (rows, width) in self.slab_dims.items()}

    def loc(self, slabs, name, layer):
        slab, row0, rpl, width, axis = self.where[(name, layer)]
        if axis == 1:
            return Loc(slabs[slab], row0, N_CHIPS * rpl, width, 0)
        return Loc(slabs[slab], row0, rpl, N_CHIPS * width, 1)

    def _whole(self, name):
        (member,) = self.members[name]
        _, off, l0, l1, rpl = member
        assert off == 0 and l0 == 0
        return l1, rpl, self.slab_dims[name][1], dict((n, a) for n, a, _, _ in _SLABS[name])[name]

    def full(self, slabs, name):
        layers, rpl, width, axis = self._whole(name)
        blocks = slabs[name].arr.reshape(N_CHIPS, layers, rpl, width)
        return jnp.concatenate([blocks[s] for s in range(N_CHIPS)], axis=axis)

    def put_full(self, slabs, name, grad):
        layers, rpl, width, axis = self._whole(name)
        parts = jnp.stack(jnp.split(grad, N_CHIPS, axis=axis)).reshape(N_CHIPS, layers * rpl, width)
        slabs[name].arr = parts.astype(slabs[name].dtype)


def _small_pack(vals, names):
    flat = jnp.concatenate([vals[n].astype(F32).reshape(-1) for n in names])
    return jnp.pad(flat, (0, SMALL_ROWS * SMALL_COLS - flat.shape[0])).reshape(SMALL_ROWS, SMALL_COLS)


def _small_unpack(flat, like, names):
    out, off = {}, 0
    flat = flat.reshape(-1)
    for n in names:
        out[n] = flat[off:off + like[n].size].reshape(like[n].shape)
        off += like[n].size
    return out


_MLA_CFG = _Attn(MLA_H, 2 * LANES, MLA_NOPE, MLA_V, True, (MLA_NOPE + MLA_ROPE) ** -0.5, hp=8, hp_kv=8)
_XA_CFG = _Attn(XA_H, XA_D, XA_D, XA_D, False, XA_D ** -0.5, hp=4, hp_kv=4)


def _mla_weights(w_in, w_uq, w_ukv):
    w_in_p = jnp.pad(w_in, ((0, 0), (0, MLA_ZPAD - w_in.shape[1])))
    w_uq_p = jnp.pad(w_uq.reshape(MLA_QR, MLA_H, MLA_NOPE + MLA_ROPE), ((0, 0), (0, 0), (0, 2 * LANES - MLA_NOPE - MLA_ROPE)))
    w_uq_p = w_uq_p.reshape(MLA_QR, MLA_H * 2 * LANES)
    kv = w_ukv.reshape(MLA_KVR, MLA_H, MLA_NOPE + MLA_V)
    w_ukv_p = jnp.concatenate([kv[:, :, :MLA_NOPE].reshape(MLA_KVR, -1), kv[:, :, MLA_NOPE:].reshape(MLA_KVR, -1)], axis=1)
    return w_in_p, w_uq_p, w_ukv_p


def _mla_weight_grads(d_in_p, d_uq_p, d_ukv_p):
    d_in = d_in_p[:, :MLA_QR + MLA_KVR + MLA_ROPE]
    d_uq = d_uq_p.reshape(MLA_QR, MLA_H, 2 * LANES)[:, :, :MLA_NOPE + MLA_ROPE].reshape(MLA_QR, -1)
    half = MLA_H * MLA_NOPE
    d_ukv = jnp.concatenate([d_ukv_p[:, :half].reshape(MLA_KVR, MLA_H, MLA_NOPE),
                             d_ukv_p[:, half:].reshape(MLA_KVR, MLA_H, MLA_V)], axis=2).reshape(MLA_KVR, -1)
    return d_in, d_uq, d_ukv


def _mla_fwd(xs, h, wts, w_o, qn, kvn, tabs, g_next, tag):
    w_in_p, w_uq_p, w_ukv_p = wts
    z = mm(h, w_in_p, "nn", f"{tag}_in")
    cq, ckv, kr = mla_mid_fwd(z, qn, kvn, tabs, f"{tag}_mid")
    q = mm(cq, w_uq_p, "nn", f"{tag}_uq", outs=(BF16,), epi=_epi_rope_q, per_row=tabs, tm=512)
    kv = mm(ckv, w_ukv_p, "nn", f"{tag}_ukv", outs=(BF16,))
    o, lse = flash_fwd(_MLA_CFG, q, kv, kv, kr, f"{tag}_attn")
    xs, h_next = residual_norm(o, w_o, xs, g_next, f"{tag}_out")
    return xs, h_next, (z, cq, ckv, kr, q, kv, o, lse)


def _mla_bwd(dx, h, wts, w_o, g_wo, qn, kvn, tabs, saved, tag):
    w_in_p, w_uq_p, w_ukv_p = wts
    z, cq, ckv, kr, q, kv, o, lse = saved
    mm(o, dx, "tn", f"{tag}_dwo", outs=(BF16,), out_loc=g_wo)
    do = mm(dx, w_o, "nt", f"{tag}_do", outs=(BF16,))
    dqp, delta = flash_dq(_MLA_CFG, q, kv, kv, kr, o, do, lse, BF16, f"{tag}_attn_dq", rope_tabs=tabs)
    dk1, dv, dkr = flash_dkv(_MLA_CFG, q, kv, kv, kr, do, lse, delta, BF16, f"{tag}_attn_dkv")
    d_uq_p = mm(cq, dqp, "tn", f"{tag}_duq")
    dcq = mm(dqp, w_uq_p, "nt", f"{tag}_dcq")
    dkv = jnp.concatenate([dk1, dv], axis=1)
    d_ukv_p = mm(ckv, dkv, "tn", f"{tag}_dukv")
    dckv = mm(dkv, w_ukv_p, "nt", f"{tag}_dckv")
    dz, dqn, dkvn = mla_mid_bwd(z, qn, kvn, tabs, dcq, dckv, dkr, f"{tag}_mid_bwd")
    d_in_p = mm(h, dz, "tn", f"{tag}_din")
    dh = (dz, w_in_p)
    d_in, d_uq, d_ukv = _mla_weight_grads(d_in_p, d_uq_p, d_ukv_p)
    return dh, dict(mla_w_in=d_in, mla_w_uq=d_uq, mla_w_ukv=d_ukv, mla_q_norm=dqn, mla_kv_norm=dkvn)


_GDN_QKV = 3 * GDN_H * GDN_D
_GDN_GATE_END = _GDN_QKV + GDN_H * GDN_D


def _gdn_weights(w_in):
    rep = lambda cols: jnp.repeat(cols, GDN_D, axis=1)
    return jnp.concatenate([w_in[:, :_GDN_GATE_END], rep(w_in[:, _GDN_GATE_END:_GDN_GATE_END + GDN_H]),
                            rep(w_in[:, _GDN_GATE_END + GDN_H:])], axis=1)


def _fold(x):
    return x.reshape(x.shape[0], -1, GDN_D).sum(-1)


def _gdn_fwd(xs, h, w_in_x, conv_w, a_log, dt_bias, o_norm, w_o, g_next, tag):
    z = mm(h, w_in_x, "nn", f"{tag}_in")
    qkv = gdn_conv_fwd(z, conv_w, f"{tag}_conv")
    a_x, dt_x = jnp.repeat(a_log.reshape(1, -1), GDN_D, axis=1), jnp.repeat(dt_bias.reshape(1, -1), GDN_D, axis=1)
    og, states, t_invs = gdn_chunk_fwd(qkv, z, a_x, dt_x, o_norm.reshape(1, -1), f"{tag}_chunks")
    xs, h_next = residual_norm(og, w_o, xs, g_next, f"{tag}_out")
    return xs, h_next, (z, qkv, a_x, dt_x, og, states, t_invs)


def _gdn_weights_compact(w_in):
    return jnp.pad(w_in, ((0, 0), (0, LANES - 2 * GDN_H)))


def _gdn_bwd(dx, h, w_in_c, conv_w, o_norm, w_o, g_wo, saved, tag):
    z, qkv, a_x, dt_x, og, states, t_invs = saved
    mm(og, dx, "tn", f"{tag}_dwo", outs=(BF16,), out_loc=g_wo)
    dog = mm(dx, w_o, "nt", f"{tag}_dog")
    dq, dk, dv, dgate, dba, da_x, ddt_x, don = gdn_chunk_bwd(qkv, z, a_x, dt_x, o_norm.reshape(1, -1), states, t_invs, dog,
                                                             f"{tag}_chunks_bwd")
    dpre, dconv = gdn_conv_bwd(z, conv_w, jnp.concatenate([dq, dk, dv], axis=1), f"{tag}_conv_bwd")
    dz = jnp.concatenate([dpre, dgate, dba], axis=1)
    d_in_c = mm(h, dz, "tn", f"{tag}_din")
    dh = (dz, w_in_c)
    return dh, dict(gdn_w_in=d_in_c[:, :_GDN_GATE_END + 2 * GDN_H], gdn_conv_w=dconv, gdn_a_log=_fold(da_x).reshape(-1),
                    gdn_dt_bias=_fold(ddt_x).reshape(-1), gdn_o_norm=don.reshape(-1))


def _sc_fwd(xs, h, w_in, conv_w, w_o, g_next, tag):
    z = mm(h, w_in, "nn", f"{tag}_in")
    y = sc_fwd(z, conv_w, f"{tag}_conv")
    xs, h_next = residual_norm(y, w_o, xs, g_next, f"{tag}_out")
    return xs, h_next, (z, y)


def _sc_bwd(dx, h, w_in, g_win, conv_w, w_o, g_wo, saved, tag):
    z, y = saved
    mm(y, dx, "tn", f"{tag}_dwo", outs=(BF16,), out_loc=g_wo)
    dy = mm(dx, w_o, "nt", f"{tag}_dy")
    db, dc, du, dconv = sc_bwd(z, conv_w, dy, f"{tag}_conv_bwd")
    dz = jnp.concatenate([db, dc, du], axis=1)
    mm(h, dz, "tn", f"{tag}_din", outs=(BF16,), out_loc=g_win)
    dh = (dz, w_in)
    return dh, dict(sc_conv_w=dconv)


def local_step(x, mem, pos, target, lay, wslabs, gslabs, small, before=None, after_bwd=None):
    depth = small["norm_mix"].shape[0]
    W = lambda name, layer: lay.loc(wslabs, name, layer)
    G = lambda name, layer: lay.loc(gslabs, name, layer)
    tabs = rope_tables(pos)
    mem_n = rmsnorm_fwd(mem, small["mem_norm"], "mem_norm")
    full = {n: lay.full(wslabs, n) for n in ("mla_w_in", "mla_w_uq", "mla_w_ukv")}
    mla_w = [_mla_weights(full["mla_w_in"][j], full["mla_w_uq"][j], full["mla_w_ukv"][j]) for j in range(full["mla_w_in"].shape[0])]
    gdn_in_x, gdn_in_c = {}, {}

    xs, h_pre = x, None
    saved = []
    for i in range(depth):
        j, kind = i // 3, i % 3
        tag = f"l{i}"
        if before is not None:
            xs = before(i, "mix", xs)
        if kind == 1:
            gdn_full = lay.full(wslabs, "gdn_w_in")[j]
            gdn_in_x[j], gdn_in_c[j] = _gdn_weights(gdn_full), _gdn_weights_compact(gdn_full)
        x_a = xs
        h = h_pre if h_pre is not None else rmsnorm_fwd(xs, small["norm_mix"][i], f"{tag}_norm_mix")
        g_mem = small["norm_mem"][i]
        if kind == 0:
            xs, hn, mix = _mla_fwd(xs, h, mla_w[j], W("mla_w_o", j), small["mla_q_norm"][j], small["mla_kv_norm"][j], tabs, g_mem,
                                   f"{tag}_mla")
        elif kind == 1:
            xs, hn, mix = _gdn_fwd(xs, h, gdn_in_x[j], small["gdn_conv_w"][j], small["gdn_a_log"][j], small["gdn_dt_bias"][j],
                                   small["gdn_o_norm"][j], W("gdn_w_o", j), g_mem, f"{tag}_gdn")
        else:
            xs, hn, mix = _sc_fwd(xs, h, W("sc_w_in", j), small["sc_conv_w"][j], W("sc_w_o", j), g_mem, f"{tag}_sc")
        if before is not None:
            xs = before(i, "xa", xs)
        x_b = xs
        xq = mm(hn, W("xa_w_q", i), "nn", f"{tag}_xa_q", outs=(BF16,))
        xkv = mm(mem_n, W("xa_w_kv", i), "nn", f"{tag}_xa_kv", outs=(BF16,))
        xo, xlse = flash_fwd(_XA_CFG, xq, xkv, xkv, None, f"{tag}_xa_attn")
        xs, hm = residual_norm(xo, W("xa_w_o", i), xs, small["norm_mlp"][i], f"{tag}_xa_out")
        x_c = xs
        h1, act = mm(hm, W("mlp_w1", i), "nn", f"{tag}_mlp_up", outs=(BF16, BF16), epi=_epi_relu2)
        xs, h_pre = residual_norm(act, W("mlp_w2", i), xs, small["norm_mix"][i + 1] if i + 1 < depth else None,
                                  f"{tag}_mlp_down", tm=512)
        saved.append((x_a, h, mix, x_b, hn, xq, xkv, xo, xlse, x_c, hm, h1, act))

    se, dx, d_final = loss_head(xs, small["final_norm"], target)

    per_layer = {n: [None] * depth for n in ("norm_mix", "norm_mem", "norm_mlp")}
    mixer = {}
    dmem_n = jnp.zeros(mem.shape, F32)
    for i in reversed(range(depth)):
        j, kind = i // 3, i % 3
        tag = f"l{i}"
        x_a, h, mix, x_b, hn, xq, xkv, xo, xlse, x_c, hm, h1, act = saved[i]
        mm(act, dx, "tn", f"{tag}_mlp_dw2", outs=(BF16,), out_loc=G("mlp_w2", i))
        dh1 = mm(dx, W("mlp_w2", i), "nt", f"{tag}_mlp_dh1", outs=(BF16,), epi=_epi_relu2_bwd, extras=(h1,))
        mm(hm, dh1, "tn", f"{tag}_mlp_dw1", outs=(BF16,), out_loc=G("mlp_w1", i))
        dx, dg = mm(dh1, W("mlp_w1", i), "nt", f"{tag}_mlp_dhm", epi=_epi_norm_bwd, extras=(x_c, dx), vecs=(small["norm_mlp"][i],),
                    row_outs=1, tm=512)
        per_layer["norm_mlp"][i] = dg.reshape(-1)
        mm(xo, dx, "tn", f"{tag}_xa_dwo", outs=(BF16,), out_loc=G("xa_w_o", i))
        dxo = mm(dx, W("xa_w_o", i), "nt", f"{tag}_xa_do", outs=(BF16,))
        dxq, xdelta = flash_dq(_XA_CFG, xq, xkv, xkv, None, xo, dxo, xlse, BF16, f"{tag}_xa_attn_dq")
        dxk, dxv = flash_dkv(_XA_CFG, xq, xkv, xkv, None, dxo, xlse, xdelta, BF16, f"{tag}_xa_attn_dkv")
        dxkv = jnp.concatenate([dxk, dxv], axis=1)
        mm(hn, dxq, "tn", f"{tag}_xa_dwq", outs=(BF16,), out_loc=G("xa_w_q", i))
        dx, dg = mm(dxq, W("xa_w_q", i), "nt", f"{tag}_xa_dhn", epi=_epi_norm_bwd, extras=(x_b, dx), vecs=(small["norm_mem"][i],),
                    row_outs=1, tm=512)
        per_layer["norm_mem"][i] = dg.reshape(-1)
        mm(mem_n, dxkv, "tn", f"{tag}_xa_dwkv", outs=(BF16,), out_loc=G("xa_w_kv", i))
        dmem_n = mm(dxkv, W("xa_w_kv", i), "nt", f"{tag}_xa_dmem", epi=_epi_add, extras=(dmem_n,))
        if after_bwd is not None:
            dx = after_bwd(i, "xa", dx)
        if kind == 0:
            dh, gr = _mla_bwd(dx, h, mla_w[j], W("mla_w_o", j), G("mla_w_o", j), small["mla_q_norm"][j], small["mla_kv_norm"][j],
                              tabs, mix, f"{tag}_mla")
        elif kind == 1:
            dh, gr = _gdn_bwd(dx, h, gdn_in_c[j], small["gdn_conv_w"][j], small["gdn_o_norm"][j], W("gdn_w_o", j), G("gdn_w_o", j),
                              mix, f"{tag}_gdn")
        else:
            dh, gr = _sc_bwd(dx, h, W("sc_w_in", j), G("sc_w_in", j), small["sc_conv_w"][j], W("sc_w_o", j), G("sc_w_o", j),
                             mix, f"{tag}_sc")
        if kind == 1:
            lay.put_full(gslabs, "gdn_w_in", gr.pop("gdn_w_in")[None])
        for n, g in gr.items():
            mixer.setdefault(n, {})[j] = g
        dz_mix, w_mix = dh
        dx, dg = mm(dz_mix, w_mix, "nt", f"{tag}_mix_dh", epi=_epi_norm_bwd, extras=(x_a, dx), vecs=(small["norm_mix"][i],),
                    row_outs=1, tm=256 if kind == 1 else 512)
        per_layer["norm_mix"][i] = dg.reshape(-1)
        if after_bwd is not None:
            dx = after_bwd(i, "mix", dx)

    _, d_mem_norm = rmsnorm_bwd(mem, small["mem_norm"], dmem_n, jnp.zeros(mem.shape, F32), "mem_norm_bwd")
    grads = {n: jnp.stack(v) for n, v in per_layer.items()}
    for n, by_j in mixer.items():
        grads[n] = jnp.stack([by_j[j] for j in sorted(by_j)])
    grads["mem_norm"] = d_mem_norm
    grads["final_norm"] = d_final
    for n in ("mla_w_in", "mla_w_uq", "mla_w_ukv"):
        lay.put_full(gslabs, n, grads.pop(n))
    return se, dx, grads


def kernel(x, mem, positions, mla_w_in, mla_q_norm, mla_kv_norm, mla_w_uq, mla_w_ukv, mla_w_o, gdn_w_in, gdn_conv_w, gdn_a_log, gdn_dt_bias, gdn_o_norm, gdn_w_o, sc_w_in, sc_conv_w, sc_w_o, norm_mix, norm_mem, norm_mlp, xa_w_q, xa_w_kv, xa_w_o, mlp_w1, mlp_w2, mem_norm, final_norm, loss_target, m_mla_w_in, m_mla_q_norm, m_mla_kv_norm, m_mla_w_uq, m_mla_w_ukv, m_mla_w_o, m_gdn_w_in, m_gdn_conv_w, m_gdn_a_log, m_gdn_dt_bias, m_gdn_o_norm, m_gdn_w_o, m_sc_w_in, m_sc_conv_w, m_sc_w_o, m_norm_mix, m_norm_mem, m_norm_mlp, m_xa_w_q, m_xa_w_kv, m_xa_w_o, m_mlp_w1, m_mlp_w2, m_mem_norm, m_final_norm, v_mla_w_in, v_mla_q_norm, v_mla_kv_norm, v_mla_w_uq, v_mla_w_ukv, v_mla_w_o, v_gdn_w_in, v_gdn_conv_w, v_gdn_a_log, v_gdn_dt_bias, v_gdn_o_norm, v_gdn_w_o, v_sc_w_in, v_sc_conv_w, v_sc_w_o, v_norm_mix, v_norm_mem, v_norm_mlp, v_xa_w_q, v_xa_w_kv, v_xa_w_o, v_mlp_w1, v_mlp_w2, v_mem_norm, v_final_norm):
    given = dict(locals())
    p = {n: given[n] for n in _WEIGHTS}
    mom = {n: given["m_" + n] for n in _WEIGHTS}
    var = {n: given["v_" + n] for n in _WEIGHTS}
    split = sorted({n for members in _SLABS.values() for n, _, _, _ in members})
    lay = Layout({n: p[n].shape for n in split})
    flat2d = lambda a: a.reshape(-1, a.shape[-1])

    me = (2 * lax.axis_index("x") + lax.axis_index("y")).astype(jnp.int32)
    core = lax.axis_index("c").astype(jnp.int32)
    me1, c1, mc = me.reshape(1), core.reshape(1), jnp.stack([me, core])

    wslabs = lay.new_slabs(BF16)

    def cast_group(slabs, chip):
        for slab in slabs:
            for name, off, l0, l1, rpl in lay.members[slab]:
                cast_into(flat2d(p[name]), l0 * rpl, (l1 - l0) * rpl, wslabs[slab], off, chip, f"cast_{slab}_{name}")

    first = _GROUPS[0][0]
    cast_group(first, me1)
    small_names = [n for n, _ in _SMALL]
    words = lax.bitcast_convert_type(jnp.concatenate([p[n].reshape(-1) for n in small_names]), BF16).reshape(-1)
    words = jnp.pad(words, (0, SMALL_ROWS * SMALL_COLS - words.shape[0])).reshape(1, SMALL_ROWS, SMALL_COLS)
    small_slab = lax.dynamic_update_slice(jnp.zeros((N_CHIPS, SMALL_ROWS, SMALL_COLS), BF16), words, (me, 0, 0))

    send0, recv0, thru0, token = gather_start([wslabs[s].arr for s in first] + [small_slab], me1, "weight_gather_start_first")
    in_flight = {}
    for slabs, point in _GROUPS[1:]:
        cast_group(slabs, me1 + token[0, 0].astype(jnp.int32))
        send, recv, thru, token = gather_start([wslabs[s].arr for s in slabs], token, f"weight_gather_start_{slabs[0]}")
        in_flight[point] = (send, recv, thru, slabs)
    started_token = token
    landed = gather_wait(send0, recv0, thru0, started_token, "weight_gather_wait_first")
    gathered = gather_forward(landed, "weight_gather_forward_first")
    for s, arr in zip(first, gathered):
        wslabs[s].arr = arr

    def before(i, stage, xs):
        if (i, stage) in in_flight:
            send, recv, thru, slabs = in_flight[(i, stage)]
            landed = gather_wait(send, recv, thru, xs, f"weight_gather_wait_{slabs[0]}")
            for s, arr in zip(slabs, gather_forward(landed, f"weight_gather_forward_{slabs[0]}")):
                wslabs[s].arr = arr
        return xs

    small = {n: p[n] for n in _REPL}
    got, off = gathered[-1].reshape(N_CHIPS, -1), 0
    for n, ax in _SMALL:
        vals = lax.bitcast_convert_type(got[:, off:off + 2 * p[n].size].reshape(N_CHIPS, p[n].size, 2), F32)
        vals = vals.reshape((N_CHIPS,) + p[n].shape)
        small[n] = jnp.concatenate([vals[s] for s in range(N_CHIPS)], axis=ax)
        off += 2 * p[n].size

    gslabs = lay.new_slabs(BF16)
    complete_at = {point: slabs for slabs, point in _GROUPS[1:]}
    exchanging = []

    def after_bwd(i, stage, dx):
        if (i, stage) not in complete_at:
            return dx
        slabs = complete_at[(i, stage)]
        g = [gslabs[s].arr for s in slabs]
        swapped = pair_swap_halves(g, f"grad_pair_swap_{slabs[0]}")
        part = [pair_add(a, b, c1, f"pair_add_{s}") for a, b, s in zip(g, swapped, slabs)]
        send, recv, thru, token = exchange_start(part, c1, f"grad_exchange_start_{slabs[0]}")
        exchanging.append((slabs, send, recv, thru))
        return dx + token[0, 0]

    se, dx, sgrads = local_step(x[0], mem[0], positions.reshape(-1, 1), loss_target[0], lay, wslabs, gslabs, small,
                                before, after_bwd)
    loss = lax.psum(0.5 * jnp.sum(se) / x.shape[-1], ("x", "y", "c"))
    names, parts, received = [], [], []
    for slabs, send, recv, thru in exchanging:
        part, got = exchange_wait(send, recv, thru, dx, f"grad_exchange_wait_{slabs[0]}")
        names, parts, received = names + slabs, parts + list(part), received + list(got)

    axes = dict(_SMALL)
    small_order = small_names + _REPL
    slots = []
    for s in range(N_CHIPS):
        vals = {n: (lax.slice_in_dim(g, s * p[n].shape[axes[n]], (s + 1) * p[n].shape[axes[n]], axis=axes[n]) if n in axes else g)
                for n, g in sgrads.items()}
        slots.append(_small_pack(vals, small_order))
    g_last = [gslabs[s].arr for s in first] + [jnp.stack(slots).astype(BF16)]
    names_last = first + ["small"]
    swapped_last = pair_swap_halves(g_last, "grad_pair_swap_last")
    part_last = [pair_add(g, b, c1, f"pair_add_{s}") for g, b, s in zip(g_last, swapped_last, names_last)]
    send, recv, thru, token = exchange_start(part_last, c1, "grad_exchange_start_last")
    mc_after = mc + token[0, 0].astype(jnp.int32)
    halves = [chip_sum(q, r, mc_after, f"chip_sum_{s}") for q, r, s in zip(parts, received, names)]
    part_last, got_last = exchange_wait(send, recv, thru, list(halves), "grad_exchange_wait_last")
    halves += [chip_sum(q, r, mc, f"chip_sum_{s}") for q, r, s in zip(part_last, got_last, names_last)]
    reduced = dict(zip(names + names_last, pair_join_halves(halves)))

    res = {}
    for slab in _SLABS:
        for name, off, l0, l1, rpl in lay.members[slab]:
            res[name] = adamw(reduced[slab], off, flat2d(p[name]), flat2d(mom[name]), flat2d(var[name]), l0 * rpl, (l1 - l0) * rpl,
                              res.get(name), f"adamw_{slab}_{name}")
    for name in split:
        res[name] = [o.reshape(p[name].shape) for o in res[name]]
    sp = {k: _small_pack(d, small_order) for k, d in (("w", p), ("m", mom), ("v", var))}
    outs = adamw(reduced["small"], 0, sp["w"], sp["m"], sp["v"], 0, SMALL_ROWS, None, "adamw_small")
    unpacked = [_small_unpack(o, p, small_order) for o in outs]
    for n in small_order:
        res[n] = [u[n] for u in unpacked]
    return (loss, dx[None], *[res[n][k] for k in range(4) for n in _WEIGHTS])
```

```python
import jax
import jax.numpy as jnp
from jax import lax
from jax.experimental import pallas as pl
from jax.experimental.pallas import tpu as pltpu

F32 = jnp.float32
BF16 = jnp.bfloat16
MESH = pl.DeviceIdType.MESH

EPS = 1e-6
ROPE_THETA = 10000.0
N_CHIPS = 4
LANES = 128
VMEM_LIMIT = 56 * 1024 * 1024
NEG = -1e30

MLA_H, MLA_NOPE, MLA_ROPE, MLA_V = 8, 128, 64, 128
MLA_QR, MLA_KVR = 384, 256
MLA_ZPAD = 768
GDN_H, GDN_D, GDN_C = 8, 128, 64
XA_H, XA_D = 4, 256

ADAM_LR, ADAM_B1, ADAM_B2, ADAM_EPS, ADAM_WD, ADAM_STEP = 0.001, 0.9, 0.999, 1e-08, 0.01, 10

SMALL_ROWS, SMALL_COLS = 32, 1024


def _cparams(sem=None):
    return pltpu.CompilerParams(dimension_semantics=sem, vmem_limit_bytes=VMEM_LIMIT)


def _pick(dim, pref):
    t = (min(pref, dim) // LANES) * LANES
    while t >= LANES:
        if dim % t == 0:
            return t
        t -= LANES
    return dim


def _pick_rows(rows, pref, *offsets):
    t = (min(pref, rows) // 16) * 16
    while t > 16 and (rows % t or any(o % t for o in offsets)):
        t -= 16
    return t


class Slab:
    def __init__(self, rows, width, dtype, arr=None):
        self.shape, self.dtype, self.arr = (N_CHIPS, rows, width), dtype, arr


class Loc:
    def __init__(self, slab, row0, K, N, axis):
        self.slab, self.row0, self.K, self.N, self.axis = slab, row0, K, N, axis
        self.Ks = K // N_CHIPS if axis == 0 else K
        self.Ns = N // N_CHIPS if axis == 1 else N

    def tile_spec(self, tr, tc, rc):
        assert self.row0 % tr == 0 and self.Ks % tr == 0 and self.Ns % tc == 0, (self.row0, self.Ks, self.Ns, tr, tc)
        r0, rb, cb = self.row0 // tr, self.Ks // tr, self.Ns // tc
        if self.axis == 0:
            return pl.BlockSpec((None, tr, tc), lambda i, j: (rc(i, j)[0] // rb, r0 + rc(i, j)[0] % rb, rc(i, j)[1]))
        return pl.BlockSpec((None, tr, tc), lambda i, j: (rc(i, j)[1] // cb, r0 + rc(i, j)[0], rc(i, j)[1] % cb))

    def slot_spec(self, slot, tr, tc, rc):
        assert self.row0 % tr == 0, (self.row0, tr)
        r0 = self.row0 // tr
        return pl.BlockSpec((None, tr, tc), lambda i, j: (slot, r0 + rc(i, j)[0], rc(i, j)[1]))


_DIMS = {"nn": ((1,), (0,)), "nt": ((1,), (1,)), "tn": ((0,), (0,))}
_ANY = pl.BlockSpec(memory_space=pl.ANY)


def mm(a, b, mode, name, outs=(F32,), epi=None, extras=(), tm=1024, tn=1024, out_loc=None, vecs=(), row_outs=0, per_row=()):
    full_rows = bool(vecs) or row_outs > 0 or bool(per_row)
    b_loc = b if isinstance(b, Loc) else None
    if mode == "nn":
        M, K = a.shape
        K2, N = (b_loc.K, b_loc.N) if b_loc else b.shape
    elif mode == "nt":
        M, K = a.shape
        N, K2 = (b_loc.K, b_loc.N) if b_loc else b.shape
    else:
        K, M = a.shape
        K2, N = b.shape
    assert K == K2, (name, a.shape, K2, N)
    tm = _pick(out_loc.Ks if (out_loc and out_loc.axis == 0) else M, tm)
    n_split = full_rows and b_loc is not None and mode == "nt" and b_loc.axis == 0
    if out_loc is not None and out_loc.axis == 1:
        tn = _pick(out_loc.Ns, tn)
    elif n_split:
        tn = N
    elif b_loc is not None and ((mode == "nn" and b_loc.axis == 1) or (mode == "nt" and b_loc.axis == 0)):
        tn = _pick(b_loc.Ns if mode == "nn" else b_loc.Ks, tn)
    elif b_loc is not None:
        tn = N if full_rows else _pick(N, min(tn, 512))
    else:
        tn = N if full_rows else _pick(N, tn)
    assert tn == N or not full_rows, name

    parts = 1
    if mode == "tn":
        a_spec = pl.BlockSpec((K, tm), lambda i, j: (0, i))
        b_specs, b_args = [pl.BlockSpec((K, tn), lambda i, j: (0, j))], [b]
    else:
        a_spec = pl.BlockSpec((tm, K), lambda i, j: (i, 0))
        if b_loc is None:
            b_specs = [pl.BlockSpec((K, tn), lambda i, j: (0, j)) if mode == "nn" else pl.BlockSpec((tn, K), lambda i, j: (j, 0))]
            b_args = [b]
        elif mode == "nn" and b_loc.axis == 1:
            b_specs, b_args = [b_loc.tile_spec(K, tn, lambda i, j: (0, j))], [b_loc.slab.arr]
        elif n_split:
            b_specs = [b_loc.slot_spec(s, b_loc.Ks, K, lambda i, j: (0, 0)) for s in range(N_CHIPS)]
            b_args = [b_loc.slab.arr] * N_CHIPS
        elif mode == "nt" and b_loc.axis == 0:
            b_specs, b_args = [b_loc.tile_spec(tn, K, lambda i, j: (j, 0))], [b_loc.slab.arr]
        elif mode == "nn":
            parts = N_CHIPS
            b_specs = [b_loc.slot_spec(s, b_loc.Ks, tn, lambda i, j: (0, j)) for s in range(parts)]
            b_args = [b_loc.slab.arr] * parts
        else:
            parts = N_CHIPS
            b_specs = [b_loc.slot_spec(s, tn, b_loc.Ns, lambda i, j: (j, 0)) for s in range(parts)]
            b_args = [b_loc.slab.arr] * parts
    kp = K // parts
    n_b = N_CHIPS if n_split else parts
    n_ex, n_out = len(extras) + len(per_row) + len(vecs), len(outs)
    dims = (_DIMS[mode], ((), ()))

    def body(*refs):
        a_ref = refs[0]
        b_refs = refs[1:1 + n_b]
        ex_refs = refs[1 + n_b:1 + n_b + n_ex]
        o_refs = refs[len(refs) - n_out - row_outs:len(refs) - row_outs]
        r_refs = refs[len(refs) - row_outs:]
        acc = None
        if n_split:
            av = a_ref[...].astype(BF16)
            acc = jnp.concatenate([lax.dot_general(av, b_ref[...].astype(BF16), dims, preferred_element_type=F32)
                                   for b_ref in b_refs], axis=1)
        for s in range(0 if n_split else parts):
            av = a_ref[...] if parts == 1 else a_ref[:, s * kp:(s + 1) * kp]
            d = lax.dot_general(av.astype(BF16), b_refs[s][...].astype(BF16), dims, preferred_element_type=F32)
            acc = d if acc is None else acc + d
        res = epi(acc, *[e[...] for e in ex_refs]) if epi is not None else (acc,)
        for o_ref, v in zip(o_refs, res[:n_out]):
            o_ref[...] = v.astype(o_ref.dtype)
        for r_ref, v in zip(r_refs, res[n_out:]):
            @pl.when(pl.program_id(0) == 0)
            def _():
                r_ref[...] = jnp.zeros_like(r_ref)

            r_ref[...] += v

    mn_spec = pl.BlockSpec((tm, tn), lambda i, j: (i, j))
    row_spec = pl.BlockSpec((1, tn), lambda i, j: (0, j))
    in_specs = ([a_spec] + b_specs + [mn_spec] * len(extras) + [pl.BlockSpec((tm, r.shape[1]), lambda i, j: (i, 0)) for r in per_row]
                + [row_spec] * len(vecs))
    args = [a] + b_args + list(extras) + list(per_row) + [v.reshape(1, N) for v in vecs]
    aliases = {}
    if out_loc is None:
        out_specs = [mn_spec] * n_out + [row_spec] * row_outs
        out_shape = [jax.ShapeDtypeStruct((M, N), d) for d in outs] + [jax.ShapeDtypeStruct((1, N), F32)] * row_outs
    else:
        assert n_out == 1 and mode == "tn"
        out_specs = [out_loc.tile_spec(tm, tn, lambda i, j: (i, j))]
        out_shape = [jax.ShapeDtypeStruct(out_loc.slab.shape, out_loc.slab.dtype)]
        if out_loc.slab.arr is not None:
            in_specs.append(_ANY)
            args.append(out_loc.slab.arr)
            aliases = {len(args) - 1: 0}

    res = pl.pallas_call(
        body, name=name, grid=(M // tm, N // tn), in_specs=in_specs, out_specs=out_specs, out_shape=out_shape,
        input_output_aliases=aliases, compiler_params=_cparams(("arbitrary" if row_outs else "parallel", "parallel")),
    )(*args)
    if out_loc is not None:
        out_loc.slab.arr = res[0]
        return None
    return res[0] if len(res) == 1 else tuple(res)


def _epi_add(acc, r):
    return (acc + r,)


def _epi_add_norm(acc, r, g):
    x = acc + r
    return x, _rms(x, g)


def _epi_norm_bwd(acc, x, dx_in, g):
    r = lax.rsqrt(jnp.mean(x * x, axis=-1, keepdims=True) + EPS)
    xh = x * r
    dxh = acc * g
    dx = dx_in + r * (dxh - xh * jnp.mean(dxh * xh, axis=-1, keepdims=True))
    return dx, jnp.sum(acc * xh, axis=0, keepdims=True)


def residual_norm(a, w, xs, g, name, tm=1024):
    if g is None:
        return mm(a, w, "nn", name, epi=_epi_add, extras=(xs,), tm=tm), None
    return mm(a, w, "nn", name, outs=(F32, BF16), epi=_epi_add_norm, extras=(xs,), vecs=(g,), tm=tm)


def _epi_relu2(acc):
    r = jnp.maximum(acc, 0.0)
    return acc, r * r


def _epi_relu2_bwd(acc, h1):
    return (acc * (2.0 * jnp.maximum(h1.astype(F32), 0.0)),)


def _rms(x, g):
    return x * lax.rsqrt(jnp.mean(x * x, axis=-1, keepdims=True) + EPS) * g


def _row_spec(ts, cols):
    return pl.BlockSpec((ts, cols), lambda i: (i, 0))


def _par_spec(cols):
    return pl.BlockSpec((1, cols), lambda i: (0, 0))


def rmsnorm_fwd(x, g, name, ts=256):
    T, D = x.shape
    ts = min(ts, T)

    def body(x_ref, g_ref, o_ref):
        o_ref[...] = _rms(x_ref[...], g_ref[...]).astype(o_ref.dtype)

    return pl.pallas_call(
        body, name=name, grid=(T // ts,),
        in_specs=[_row_spec(ts, D), _par_spec(D)], out_specs=_row_spec(ts, D),
        out_shape=jax.ShapeDtypeStruct((T, D), BF16), compiler_params=_cparams(("parallel",)),
    )(x, g.reshape(1, D))


def rmsnorm_bwd(x, g, dy, dx_in, name, ts=256):
    T, D = x.shape
    ts = min(ts, T)

    def body(x_ref, g_ref, dy_ref, dxi_ref, dx_ref, dg_ref):
        xv = x_ref[...]
        r = lax.rsqrt(jnp.mean(xv * xv, axis=-1, keepdims=True) + EPS)
        xh = xv * r
        dyv = dy_ref[...].astype(F32)
        dxh = dyv * g_ref[...]
        dx_ref[...] = dxi_ref[...] + r * (dxh - xh * jnp.mean(dxh * xh, axis=-1, keepdims=True))
        dg = jnp.sum(dyv * xh, axis=0, keepdims=True)

        @pl.when(pl.program_id(0) == 0)
        def _():
            dg_ref[...] = jnp.zeros_like(dg_ref)

        dg_ref[...] += dg

    dx, dg = pl.pallas_call(
        body, name=name, grid=(T // ts,),
        in_specs=[_row_spec(ts, D), _par_spec(D), _row_spec(ts, D), _row_spec(ts, D)],
        out_specs=[_row_spec(ts, D), _par_spec(D)],
        out_shape=[jax.ShapeDtypeStruct((T, D), F32), jax.ShapeDtypeStruct((1, D), F32)],
        compiler_params=_cparams(("arbitrary",)),
    )(x, g.reshape(1, D), dy, dx_in)
    return dx, dg.reshape(D)


def rope_tables(pos, name="rope_tables"):
    T = pos.shape[0]
    half = MLA_ROPE // 2
    inv = ROPE_THETA ** (-jnp.arange(0, MLA_ROPE, 2, dtype=F32) / MLA_ROPE)
    inv_row = jnp.concatenate([inv, inv, jnp.zeros((LANES - MLA_ROPE,), F32)]).reshape(1, LANES)

    def body(p_ref, f_ref, c_ref, a_ref, b_ref):
        ang = p_ref[...].astype(F32) * f_ref[...]
        lane = lax.broadcasted_iota(jnp.int32, ang.shape, 1)
        c, s = jnp.cos(ang), jnp.sin(ang)
        c_ref[...] = jnp.where(lane < MLA_ROPE, c, 0.0)
        a_ref[...] = jnp.where(lane < half, -s, 0.0)
        b_ref[...] = jnp.where((lane >= half) & (lane < MLA_ROPE), s, 0.0)

    sh = jax.ShapeDtypeStruct((T, LANES), F32)
    return pl.pallas_call(body, name=name, out_shape=[sh, sh, sh], compiler_params=_cparams())(pos, inv_row)


def _roll_l(x):
    return pltpu.roll(x, LANES - MLA_ROPE // 2, 1)


def _roll_r(x):
    return pltpu.roll(x, MLA_ROPE // 2, 1)


def _rope(r, c, sa, sb):
    return r * c + _roll_l(r) * sa + _roll_r(r) * sb


def _rope_t(d, c, sa, sb):
    return d * c + _roll_r(d * sa) + _roll_l(d * sb)


def _epi_rope_q(acc, c, sa, sb):
    hw = 2 * LANES
    parts = []
    for h in range(acc.shape[1] // hw):
        parts += [acc[:, h * hw:h * hw + LANES], _rope(acc[:, h * hw + LANES:(h + 1) * hw], c, sa, sb)]
    return (jnp.concatenate(parts, axis=1),)


def mla_mid_fwd(z, qn, kvn, tabs, name, ts=256):
    T = z.shape[0]
    ts = min(ts, T)
    a0, a1 = MLA_QR, MLA_QR + MLA_KVR

    def body(z_ref, qn_ref, kvn_ref, c_ref, sa_ref, sb_ref, cq_ref, ckv_ref, kr_ref):
        cq_ref[...] = _rms(z_ref[:, 0:a0], qn_ref[...]).astype(BF16)
        ckv_ref[...] = _rms(z_ref[:, a0:a1], kvn_ref[...]).astype(BF16)
        kr_ref[...] = _rope(z_ref[:, a1:MLA_ZPAD], c_ref[...], sa_ref[...], sb_ref[...]).astype(BF16)

    return pl.pallas_call(
        body, name=name, grid=(T // ts,),
        in_specs=[_row_spec(ts, MLA_ZPAD), _par_spec(MLA_QR), _par_spec(MLA_KVR)] + [_row_spec(ts, LANES)] * 3,
        out_specs=[_row_spec(ts, MLA_QR), _row_spec(ts, MLA_KVR), _row_spec(ts, LANES)],
        out_shape=[jax.ShapeDtypeStruct((T, MLA_QR), BF16), jax.ShapeDtypeStruct((T, MLA_KVR), BF16),
                   jax.ShapeDtypeStruct((T, LANES), BF16)],
        compiler_params=_cparams(("parallel",)),
    )(z, qn.reshape(1, -1), kvn.reshape(1, -1), *tabs)


def mla_mid_bwd(z, qn, kvn, tabs, dcq, dckv, dkr, name, ts=256):
    T = z.shape[0]
    ts = min(ts, T)
    a0, a1 = MLA_QR, MLA_QR + MLA_KVR

    def body(z_ref, qn_ref, kvn_ref, c_ref, sa_ref, sb_ref, dcq_ref, dckv_ref, dkr_ref, dz_ref, dqn_ref, dkvn_ref):
        _, vq = jax.vjp(_rms, z_ref[:, 0:a0], qn_ref[...])
        dzq, dqn = vq(dcq_ref[...].astype(F32))
        _, vk = jax.vjp(_rms, z_ref[:, a0:a1], kvn_ref[...])
        dzk, dkvn = vk(dckv_ref[...].astype(F32))
        dz_ref[:, 0:a0] = dzq.astype(dz_ref.dtype)
        dz_ref[:, a0:a1] = dzk.astype(dz_ref.dtype)
        dz_ref[:, a1:MLA_ZPAD] = _rope_t(dkr_ref[...].astype(F32), c_ref[...], sa_ref[...], sb_ref[...]).astype(dz_ref.dtype)

        @pl.when(pl.program_id(0) == 0)
        def _():
            dqn_ref[...] = jnp.zeros_like(dqn_ref)
            dkvn_ref[...] = jnp.zeros_like(dkvn_ref)

        dqn_ref[...] += dqn
        dkvn_ref[...] += dkvn

    dz, dqn, dkvn = pl.pallas_call(
        body, name=name, grid=(T // ts,),
        in_specs=[_row_spec(ts, MLA_ZPAD), _par_spec(MLA_QR), _par_spec(MLA_KVR)] + [_row_spec(ts, LANES)] * 3
        + [_row_spec(ts, MLA_QR), _row_spec(ts, MLA_KVR), _row_spec(ts, LANES)],
        out_specs=[_row_spec(ts, MLA_ZPAD), _par_spec(MLA_QR), _par_spec(MLA_KVR)],
        out_shape=[jax.ShapeDtypeStruct((T, MLA_ZPAD), BF16), jax.ShapeDtypeStruct((1, MLA_QR), F32),
                   jax.ShapeDtypeStruct((1, MLA_KVR), F32)],
        compiler_params=_cparams(("arbitrary",)),
    )(z, qn.reshape(1, -1), kvn.reshape(1, -1), *tabs, dcq, dckv, dkr)
    return dz, dqn.reshape(-1), dkvn.reshape(-1)


def loss_head(x, g, target, name="loss_head", ts=256):
    T, D = x.shape
    ts = min(ts, T)

    def body(x_ref, g_ref, t_ref, se_ref, dx_ref, dg_ref):
        xv = x_ref[...]
        r = lax.rsqrt(jnp.mean(xv * xv, axis=-1, keepdims=True) + EPS)
        xh = xv * r
        err = xh * g_ref[...] - t_ref[...]
        dy = err * (1.0 / D)
        dxh = dy * g_ref[...]
        dx_ref[...] = r * (dxh - xh * jnp.mean(dxh * xh, axis=-1, keepdims=True))

        @pl.when(pl.program_id(0) == 0)
        def _():
            se_ref[...] = jnp.zeros_like(se_ref)
            dg_ref[...] = jnp.zeros_like(dg_ref)

        se_ref[...] += jnp.sum(err * err, axis=0, keepdims=True)
        dg_ref[...] += jnp.sum(dy * xh, axis=0, keepdims=True)

    se, dx, dg = pl.pallas_call(
        body, name=name, grid=(T // ts,),
        in_specs=[_row_spec(ts, D), _par_spec(D), _row_spec(ts, D)],
        out_specs=[_par_spec(D), _row_spec(ts, D), _par_spec(D)],
        out_shape=[jax.ShapeDtypeStruct((1, D), F32), jax.ShapeDtypeStruct((T, D), F32), jax.ShapeDtypeStruct((1, D), F32)],
        compiler_params=_cparams(("arbitrary",)),
    )(x, g.reshape(1, D), target)
    return se, dx, dg.reshape(D)


def _dot_nt(a, b):
    return lax.dot_general(a, b, (((1,), (1,)), ((), ())), preferred_element_type=F32)


def _dot_nn(a, b):
    return lax.dot_general(a, b, (((1,), (0,)), ((), ())), preferred_element_type=F32)


class _Attn:
    def __init__(self, H, dq, dk1, dv, causal, scale, hp, hp_kv, blk=256):
        self.H, self.dq, self.dk1, self.dv, self.causal, self.scale, self.blk = H, dq, dk1, dv, causal, scale, blk
        self.hp, self.hp_kv = hp, hp_kv


def _cols(ref, rows, hh, width):
    return ref[rows, hh * width:(hh + 1) * width]


def _keys(cfg, k1_ref, k2_ref, rows, hh):
    ks = _cols(k1_ref, rows, hh, cfg.dk1)
    if k2_ref is not None:
        ks = jnp.concatenate([ks, k2_ref[rows, :]], axis=1)
    return ks


def _attn_specs(cfg, hp, t, Tk, has_k2, by_q):
    g = cfg.H // hp
    if by_q:
        specs = [pl.BlockSpec((t, hp * cfg.dq), lambda h, i: (i, h)),
                 pl.BlockSpec((Tk, hp * cfg.dk1), lambda h, i: (0, h)),
                 pl.BlockSpec((Tk, hp * cfg.dv), lambda h, i: (0, g + h))]
        if has_k2:
            specs.append(pl.BlockSpec((Tk, LANES), lambda h, i: (0, 0)))
    else:
        specs = [None,
                 pl.BlockSpec((t, hp * cfg.dk1), lambda j, h: (j, h)),
                 pl.BlockSpec((t, hp * cfg.dv), lambda j, h: (j, g + h))]
        if has_k2:
            specs.append(pl.BlockSpec((t, LANES), lambda j, h: (j, 0)))
    return specs


def _mask(s, diagonal):
    if not diagonal:
        return s
    return jnp.where(lax.broadcasted_iota(jnp.int32, s.shape, 0) >= lax.broadcasted_iota(jnp.int32, s.shape, 1), s, NEG)


def flash_fwd(cfg, q, k1, v, k2, name):
    Tq, Tk = q.shape[0], k1.shape[0]
    t = min(cfg.blk, Tq, Tk)
    nkb = Tk // t
    has_k2 = k2 is not None
    hp = cfg.hp

    def body(*refs):
        q_ref, k1_ref, v_ref = refs[:3]
        k2_ref = refs[3] if has_k2 else None
        o_ref, lse_ref = refs[-2], refs[-1]
        i = pl.program_id(1)
        qs = [_cols(q_ref, slice(None), hh, cfg.dq) for hh in range(hp)]

        def step(j, carry, diagonal=False):
            rows = pl.ds(pl.multiple_of(j * t, t), t)
            out = []
            for hh in range(hp):
                m, l, acc = carry[hh]
                s = _mask(_dot_nt(qs[hh], _keys(cfg, k1_ref, k2_ref, rows, hh)) * cfg.scale, diagonal)
                m2 = jnp.maximum(m, jnp.max(s, axis=-1, keepdims=True))
                p = jnp.exp(s - m2)
                alpha = jnp.exp(m - m2)
                l2 = alpha * l + jnp.sum(p, axis=-1, keepdims=True)
                acc2 = alpha * acc + _dot_nn(p.astype(BF16), _cols(v_ref, rows, hh, cfg.dv))
                out.append((m2, l2, acc2))
            return tuple(out)

        init = tuple((jnp.full((t, 1), NEG, F32), jnp.zeros((t, 1), F32), jnp.zeros((t, cfg.dv), F32)) for _ in range(hp))
        res = lax.fori_loop(0, i if cfg.causal else nkb, step, init)
        if cfg.causal:
            res = step(i, res, True)
        for hh in range(hp):
            m, l, acc = res[hh]
            o_ref[:, hh * cfg.dv:(hh + 1) * cfg.dv] = (acc / l).astype(o_ref.dtype)
            lse_ref[hh] = m + jnp.log(l)

    args = [q, k1, v] + ([k2] if has_k2 else [])
    return pl.pallas_call(
        body, name=name, grid=(cfg.H // hp, Tq // t), in_specs=_attn_specs(cfg, hp, t, Tk, has_k2, True),
        out_specs=[pl.BlockSpec((t, hp * cfg.dv), lambda h, i: (i, h)), pl.BlockSpec((hp, t, 1), lambda h, i: (h, i, 0))],
        out_shape=[jax.ShapeDtypeStruct((Tq, cfg.H * cfg.dv), BF16), jax.ShapeDtypeStruct((cfg.H, Tq, 1), F32)],
        compiler_params=_cparams(("parallel", "parallel")),
    )(*args)


def flash_dq(cfg, q, k1, v, k2, o, do, lse, out_dtype, name, rope_tabs=None):
    Tq, Tk = q.shape[0], k1.shape[0]
    t = min(cfg.blk, Tq, Tk)
    nkb = Tk // t
    has_k2 = k2 is not None
    hp = cfg.hp
    n_tab = 0 if rope_tabs is None else len(rope_tabs)

    def body(*refs):
        q_ref, k1_ref, v_ref = refs[:3]
        k2_ref = refs[3] if has_k2 else None
        tab_refs = refs[len(refs) - 5 - n_tab:len(refs) - 5]
        o_ref, do_ref, lse_ref, dq_ref, dl_ref = refs[-5:]
        i = pl.program_id(1)
        qs = [_cols(q_ref, slice(None), hh, cfg.dq) for hh in range(hp)]
        dos = [_cols(do_ref, slice(None), hh, cfg.dv) for hh in range(hp)]
        lses = [lse_ref[hh] for hh in range(hp)]
        deltas = []
        for hh in range(hp):
            d = jnp.sum(dos[hh].astype(F32) * _cols(o_ref, slice(None), hh, cfg.dv).astype(F32), axis=-1, keepdims=True)
            dl_ref[hh] = d
            deltas.append(d)

        def step(j, dqs, diagonal=False):
            rows = pl.ds(pl.multiple_of(j * t, t), t)
            out = []
            for hh in range(hp):
                ks = _keys(cfg, k1_ref, k2_ref, rows, hh)
                s = _mask(_dot_nt(qs[hh], ks) * cfg.scale, diagonal)
                p = jnp.exp(s - lses[hh])
                dp = _dot_nt(dos[hh], _cols(v_ref, rows, hh, cfg.dv))
                ds = p * (dp - deltas[hh]) * cfg.scale
                out.append(dqs[hh] + _dot_nn(ds.astype(BF16), ks))
            return tuple(out)

        dqs = lax.fori_loop(0, i if cfg.causal else nkb, step, tuple(jnp.zeros((t, cfg.dq), F32) for _ in range(hp)))
        if cfg.causal:
            dqs = step(i, dqs, True)
        tabs = [r[...] for r in tab_refs]
        for hh in range(hp):
            dq = dqs[hh]
            if tabs:
                dq = jnp.concatenate([dq[:, :LANES], _rope_t(dq[:, LANES:], *tabs)], axis=1)
            dq_ref[:, hh * cfg.dq:(hh + 1) * cfg.dq] = dq.astype(dq_ref.dtype)

    ov = pl.BlockSpec((t, hp * cfg.dv), lambda h, i: (i, h))
    row1 = pl.BlockSpec((hp, t, 1), lambda h, i: (h, i, 0))
    tab_specs = [pl.BlockSpec((t, LANES), lambda h, i: (i, 0))] * n_tab
    args = [q, k1, v] + ([k2] if has_k2 else []) + list(rope_tabs or ()) + [o, do, lse]
    return pl.pallas_call(
        body, name=name, grid=(cfg.H // hp, Tq // t),
        in_specs=_attn_specs(cfg, hp, t, Tk, has_k2, True) + tab_specs + [ov, ov, row1],
        out_specs=[pl.BlockSpec((t, hp * cfg.dq), lambda h, i: (i, h)), row1],
        out_shape=[jax.ShapeDtypeStruct((Tq, cfg.H * cfg.dq), out_dtype), jax.ShapeDtypeStruct((cfg.H, Tq, 1), F32)],
        compiler_params=_cparams(("parallel", "parallel")),
    )(*args)


def flash_dkv(cfg, q, k1, v, k2, do, lse, delta, out_dtype, name):
    Tq, Tk = q.shape[0], k1.shape[0]
    t = min(cfg.blk, Tq, Tk)
    nqb = Tq // t
    has_k2 = k2 is not None
    hp = cfg.hp_kv
    assert hp == cfg.H
    v0 = cfg.H * cfg.dk1

    def body(*refs):
        q_ref, k1_ref, v_ref = refs[:3]
        k2_ref = refs[3] if has_k2 else None
        n_in = 4 if has_k2 else 3
        do_ref, lse_ref, dl_ref = refs[n_in:n_in + 3]
        dkv_ref = refs[n_in + 3]
        j, h = pl.program_id(0), pl.program_id(1)
        kss = [_keys(cfg, k1_ref, k2_ref, slice(None), hh) for hh in range(hp)]
        vss = [_cols(v_ref, slice(None), hh, cfg.dv) for hh in range(hp)]

        def step(i, carry, diagonal=False):
            rows = pl.ds(pl.multiple_of(i * t, t), t)
            out = []
            for hh in range(hp):
                dk, dv = carry[hh]
                qi, doi = _cols(q_ref, rows, hh, cfg.dq), _cols(do_ref, rows, hh, cfg.dv)
                s = _dot_nt(kss[hh], qi) * cfg.scale
                if diagonal:
                    s = jnp.where(lax.broadcasted_iota(jnp.int32, s.shape, 0) <= lax.broadcasted_iota(jnp.int32, s.shape, 1), s, NEG)
                p = jnp.exp(s - lse_ref[hh, :, rows])
                dv = dv + _dot_nn(p.astype(BF16), doi)
                ds = p * (_dot_nt(vss[hh], doi) - dl_ref[hh, :, rows]) * cfg.scale
                dk = dk + _dot_nn(ds.astype(BF16), qi)
                out.append((dk, dv))
            return tuple(out)

        init = tuple((jnp.zeros((t, cfg.dq), F32), jnp.zeros((t, cfg.dv), F32)) for _ in range(hp))
        if cfg.causal:
            res = lax.fori_loop(j + 1, nqb, step, step(j, init, True))
        else:
            res = lax.fori_loop(0, nqb, step, init)
        for hh in range(hp):
            dk, dv = res[hh]
            dkv_ref[:, hh * cfg.dk1:(hh + 1) * cfg.dk1] = dk[:, 0:cfg.dk1].astype(dkv_ref.dtype)
            dkv_ref[:, v0 + hh * cfg.dv:v0 + (hh + 1) * cfg.dv] = dv.astype(dkv_ref.dtype)
        if has_k2:
            dk2_ref = refs[n_in + 4]

            @pl.when(h == 0)
            def _():
                dk2_ref[...] = jnp.zeros_like(dk2_ref)

            for hh in range(hp):
                dk2_ref[...] += res[hh][0][:, cfg.dk1:]

    specs = _attn_specs(cfg, hp, t, Tk, has_k2, False)
    specs[0] = pl.BlockSpec((Tq, hp * cfg.dq), lambda j, h: (0, h))
    rows_all = pl.BlockSpec((hp, 1, Tq), lambda j, h: (h, 0, 0))
    specs += [pl.BlockSpec((Tq, hp * cfg.dv), lambda j, h: (0, h)), rows_all, rows_all]
    args = [q, k1, v] + ([k2] if has_k2 else []) + [do, lse.reshape(cfg.H, 1, Tq), delta.reshape(cfg.H, 1, Tq)]
    out_specs = [pl.BlockSpec((t, v0 + cfg.H * cfg.dv), lambda j, h: (j, 0))]
    out_shape = [jax.ShapeDtypeStruct((Tk, v0 + cfg.H * cfg.dv), out_dtype)]
    if has_k2:
        out_specs.append(pl.BlockSpec((t, LANES), lambda j, h: (j, 0)))
        out_shape.append(jax.ShapeDtypeStruct((Tk, LANES), F32))
    return pl.pallas_call(
        body, name=name, grid=(Tk // t, cfg.H // hp), in_specs=specs, out_specs=out_specs, out_shape=out_shape,
        compiler_params=_cparams(("parallel", "arbitrary")),
    )(*args)


def _shift_down(x, s):
    if s == 0:
        return x
    t = lax.broadcasted_iota(jnp.int32, x.shape, 0)
    return jnp.where(t >= s, pltpu.roll(x, s, 0), 0.0)


def _shift_up(x, s):
    if s == 0:
        return x
    n = x.shape[0]
    t = lax.broadcasted_iota(jnp.int32, x.shape, 0)
    return jnp.where(t < n - s, pltpu.roll(x, n - s, 0), 0.0)


def _conv(x, w_ref, kw):
    y = x * w_ref[kw - 1:kw, :]
    for j in range(kw - 1):
        y = y + _shift_down(x, kw - 1 - j) * w_ref[j:j + 1, :]
    return y


def _conv_t(d, w_ref, kw):
    y = d * w_ref[kw - 1:kw, :]
    for j in range(kw - 1):
        y = y + _shift_up(d, kw - 1 - j) * w_ref[j:j + 1, :]
    return y


def _conv_dw(d, x, kw):
    rows = lax.broadcasted_iota(jnp.int32, (kw, d.shape[1]), 0)
    dw = jnp.zeros((kw, d.shape[1]), F32)
    for j in range(kw):
        r = jnp.sum(d * _shift_down(x, kw - 1 - j), axis=0, keepdims=True)
        dw = jnp.where(rows == j, r, dw)
    return dw


def _silu(x):
    return x * jax.nn.sigmoid(x)


def _silu_grad(x):
    s = jax.nn.sigmoid(x)
    return s * (1.0 + x * (1.0 - s))


def gdn_conv_fwd(z, w, name, tc=256):
    T, C = z.shape[0], w.shape[1]
    kw = w.shape[0]

    def body(x_ref, w_ref, o_ref):
        o_ref[...] = _silu(_conv(x_ref[...], w_ref, kw))

    return pl.pallas_call(
        body, name=name, grid=(C // tc,),
        in_specs=[pl.BlockSpec((T, tc), lambda j: (0, j)), pl.BlockSpec((kw, tc), lambda j: (0, j))],
        out_specs=pl.BlockSpec((T, tc), lambda j: (0, j)),
        out_shape=jax.ShapeDtypeStruct((T, C), F32), compiler_params=_cparams(("parallel",)),
    )(z, w)


def gdn_conv_bwd(z, w, dy, name, tc=256):
    T, C = z.shape[0], w.shape[1]
    kw = w.shape[0]

    def body(x_ref, w_ref, dy_ref, dx_ref, dw_ref):
        xv = x_ref[...]
        dc = dy_ref[...] * _silu_grad(_conv(xv, w_ref, kw))
        dx_ref[...] = _conv_t(dc, w_ref, kw).astype(dx_ref.dtype)
        dw_ref[...] = _conv_dw(dc, xv, kw)

    col = lambda j: (0, j)
    return pl.pallas_call(
        body, name=name, grid=(C // tc,),
        in_specs=[pl.BlockSpec((T, tc), col), pl.BlockSpec((kw, tc), col), pl.BlockSpec((T, tc), col)],
        out_specs=[pl.BlockSpec((T, tc), col), pl.BlockSpec((kw, tc), col)],
        out_shape=[jax.ShapeDtypeStruct((T, C), BF16), jax.ShapeDtypeStruct((kw, C), F32)],
        compiler_params=_cparams(("parallel",)),
    )(z, w, dy)


def sc_fwd(z, w, name, tc=256):
    T, C = z.shape[0], w.shape[1]
    kw, nb = w.shape[0], C // tc

    def body(b_ref, c_ref, u_ref, w_ref, o_ref):
        o_ref[...] = (b_ref[...] * _conv(c_ref[...] * u_ref[...], w_ref, kw)).astype(o_ref.dtype)

    return pl.pallas_call(
        body, name=name, grid=(nb,),
        in_specs=[pl.BlockSpec((T, tc), lambda j: (0, j)), pl.BlockSpec((T, tc), lambda j: (0, nb + j)),
                  pl.BlockSpec((T, tc), lambda j: (0, 2 * nb + j)), pl.BlockSpec((kw, tc), lambda j: (0, j))],
        out_specs=pl.BlockSpec((T, tc), lambda j: (0, j)),
        out_shape=jax.ShapeDtypeStruct((T, C), BF16), compiler_params=_cparams(("parallel",)),
    )(z, z, z, w)


def sc_bwd(z, w, dy, name, tc=256):
    T, C = z.shape[0], w.shape[1]
    kw, nb = w.shape[0], C // tc

    def body(b_ref, c_ref, u_ref, w_ref, dy_ref, db_ref, dc_ref, du_ref, dw_ref):
        cv, uv, dyv = c_ref[...], u_ref[...], dy_ref[...]
        cu = cv * uv
        db_ref[...] = (dyv * _conv(cu, w_ref, kw)).astype(db_ref.dtype)
        dcv = dyv * b_ref[...]
        dcu = _conv_t(dcv, w_ref, kw)
        dc_ref[...] = (dcu * uv).astype(dc_ref.dtype)
        du_ref[...] = (dcu * cv).astype(du_ref.dtype)
        dw_ref[...] = _conv_dw(dcv, cu, kw)

    col = lambda j: (0, j)
    act = jax.ShapeDtypeStruct((T, C), BF16)
    return pl.pallas_call(
        body, name=name, grid=(nb,),
        in_specs=[pl.BlockSpec((T, tc), col), pl.BlockSpec((T, tc), lambda j: (0, nb + j)),
                  pl.BlockSpec((T, tc), lambda j: (0, 2 * nb + j)), pl.BlockSpec((kw, tc), col), pl.BlockSpec((T, tc), col)],
        out_specs=[pl.BlockSpec((T, tc), col)] * 3 + [pl.BlockSpec((kw, tc), col)],
        out_shape=[act, act, act, jax.ShapeDtypeStruct((kw, C), F32)],
        compiler_params=_cparams(("parallel",)),
    )(z, z, z, w, dy)


def _hdot(a, b, dims):
    a_hi, b_hi = a.astype(BF16), b.astype(BF16)
    a_lo, b_lo = (a - a_hi.astype(F32)).astype(BF16), (b - b_hi.astype(F32)).astype(BF16)
    dot = lambda x, y: lax.dot_general(x, y, (dims, ((), ())), preferred_element_type=F32)
    return dot(a_hi, b_hi) + (dot(a_hi, b_lo) + dot(a_lo, b_hi))


def _bdot(a, b, dims):
    return lax.dot_general(a.astype(BF16), b.astype(BF16), (dims, ((), ())), preferred_element_type=F32)


_NN, _NT, _TN = ((1,), (0,)), ((1,), (1,)), ((0,), (0,))


def _per_head_dots(dot2d):
    def stacked(a, b, dims):
        return jnp.stack([dot2d(a[h], b[h], dims) for h in range(a.shape[0])])

    @jax.custom_vjp
    def nn(a, b):
        return stacked(a, b, _NN)

    @jax.custom_vjp
    def nt(a, b):
        return stacked(a, b, _NT)

    @jax.custom_vjp
    def tn(a, b):
        return stacked(a, b, _TN)

    nn.defvjp(lambda a, b: (nn(a, b), (a, b)), lambda r, d: (stacked(d, r[1], _NT), stacked(r[0], d, _TN)))
    nt.defvjp(lambda a, b: (nt(a, b), (a, b)), lambda r, d: (stacked(d, r[1], _NN), stacked(d, r[0], _TN)))
    tn.defvjp(lambda a, b: (tn(a, b), (a, b)), lambda r, d: (stacked(r[1], d, _NT), stacked(r[0], d, _NN)))
    return nn, nt, tn


_hnn, _hnt, _htn = _per_head_dots(_hdot)
_bnn, _bnt, _btn = _per_head_dots(_bdot)


@jax.custom_vjp
def _unit_lower_inverse(m):
    c = m.shape[-1]
    eye = (lax.broadcasted_iota(jnp.int32, (c, c), 0) == lax.broadcasted_iota(jnp.int32, (c, c), 1)).astype(F32)
    t = eye - m
    p = _hnn(m, m)
    n = 2
    while n < c:
        t = t + _hnn(t, p)
        n *= 2
        if n < c:
            p = _hnn(p, p)
    return t


def _uli_fwd(m):
    t = _unit_lower_inverse(m)
    return t, t


def _uli_bwd(t, dt):
    return (-_htn(t, _hnt(dt, t)),)


_unit_lower_inverse.defvjp(_uli_fwd, _uli_bwd)


@jax.custom_vjp
def _known_inverse(m, t):
    return t


_known_inverse.defvjp(lambda m, t: (t, t), lambda t, dt: (_uli_bwd(t, dt)[0], jnp.zeros_like(t)))


def _gdn_chunk(q, k, v, gate, bl, al, a_log, dt_bias, o_norm, st, t_known=None):
    nh, c = q.shape[0], q.shape[1]
    ii = lax.broadcasted_iota(jnp.int32, (c, c), 0)
    jj = lax.broadcasted_iota(jnp.int32, (c, c), 1)
    tri, strict = ii >= jj, ii > jj
    q = q * lax.rsqrt(jnp.sum(q * q, -1, keepdims=True) + EPS) * (GDN_D ** -0.5)
    k = k * lax.rsqrt(jnp.sum(k * k, -1, keepdims=True) + EPS)
    beta = jax.nn.sigmoid(bl)
    g = -jnp.exp(a_log) * jax.nn.softplus(al + dt_bias)
    gc = _hnn(jnp.broadcast_to(tri.astype(F32), (nh, c, c)), g)
    gcol = _hnn(gc, jnp.full((nh, LANES, c), 1.0 / LANES, F32))
    grow = _hnt(jnp.full((nh, c, LANES), 1.0 / LANES, F32), gc)
    decay = jnp.where(tri, jnp.exp(jnp.where(tri, gcol - grow, 0.0)), 0.0)
    kb = k * beta
    m = jnp.where(strict, _bnt(kb, k) * decay, 0.0)
    t_inv = _unit_lower_inverse(m) if t_known is None else _known_inverse(m, t_known)
    eg = jnp.exp(gc)
    u = _bnn(t_inv, v * beta)
    w = _bnn(t_inv, kb * eg)
    attn = _bnt(q, k) * decay
    v_new = u - _bnn(w, st)
    o = _bnn(q * eg, st) + _bnn(attn, v_new)
    g_last = jnp.sum(g, axis=1, keepdims=True)
    st_new = st * jnp.exp(g_last) + _btn(k * jnp.exp(g_last - gc), v_new)
    o = o * lax.rsqrt(jnp.mean(o * o, -1, keepdims=True) + EPS) * o_norm
    return o * _silu(gate), st_new, t_inv


GDN_HP = 8
_GW = GDN_HP * GDN_D
_GB = GDN_H // GDN_HP


def _gdn_specs(n_chunks, rev):
    def tok(col):
        if rev:
            return pl.BlockSpec((GDN_C, _GW), lambda h, n: (n_chunks - 1 - n, col + h))
        return pl.BlockSpec((GDN_C, _GW), lambda h, n: (n, col + h))
    par = pl.BlockSpec((1, _GW), lambda h, n: (0, h))
    shared = pl.BlockSpec((1, GDN_D), lambda h, n: (0, 0))
    if rev:
        st = pl.BlockSpec((GDN_HP, None, GDN_D, GDN_D), lambda h, n: (h, n_chunks - 1 - n, 0, 0))
    else:
        st = pl.BlockSpec((GDN_HP, None, GDN_D, GDN_D), lambda h, n: (h, n, 0, 0))
    return tok, par, shared, st


def _heads(ref):
    return jnp.stack([ref[:, h * GDN_D:(h + 1) * GDN_D] for h in range(ref.shape[1] // GDN_D)])


def gdn_chunk_fwd(qkv, z, a_log_x, dt_bias_x, o_norm, name):
    T = qkv.shape[0]
    n_chunks = T // GDN_C
    H = GDN_H
    tok, par, shared, st_spec = _gdn_specs(n_chunks, False)

    def body(q_ref, k_ref, v_ref, g_ref, bl_ref, al_ref, a_ref, dt_ref, on_ref, o_ref, st_ref, ti_ref, state):
        @pl.when(pl.program_id(1) == 0)
        def _():
            state[...] = jnp.zeros_like(state)

        st = state[...]
        st_ref[...] = st
        o, st_new, t_inv = _gdn_chunk(_heads(q_ref), _heads(k_ref), _heads(v_ref), _heads(g_ref), _heads(bl_ref), _heads(al_ref),
                                      _heads(a_ref), _heads(dt_ref), on_ref[...], st)
        for hh in range(GDN_HP):
            o_ref[:, hh * GDN_D:(hh + 1) * GDN_D] = o[hh].astype(o_ref.dtype)
        ti_ref[...] = t_inv
        state[...] = st_new

    B = _GB
    return pl.pallas_call(
        body, name=name, grid=(B, n_chunks),
        in_specs=[tok(0), tok(B), tok(2 * B), tok(3 * B), tok(4 * B), tok(5 * B), par, par, shared],
        out_specs=[tok(0), st_spec, pl.BlockSpec((GDN_HP, None, GDN_C, GDN_C), lambda h, n: (h, n, 0, 0))],
        out_shape=[jax.ShapeDtypeStruct((T, H * GDN_D), BF16), jax.ShapeDtypeStruct((H, n_chunks, GDN_D, GDN_D), F32),
                   jax.ShapeDtypeStruct((H, n_chunks, GDN_C, GDN_C), F32)],
        scratch_shapes=[pltpu.VMEM((GDN_HP, GDN_D, GDN_D), F32)],
        compiler_params=_cparams(("parallel", "arbitrary")),
    )(qkv, qkv, qkv, z, z, z, a_log_x, dt_bias_x, o_norm)


def gdn_chunk_bwd(qkv, z, a_log_x, dt_bias_x, o_norm, states, t_invs, do, name):
    T = qkv.shape[0]
    n_chunks = T // GDN_C
    H = GDN_H
    tok, par, shared, st_spec = _gdn_specs(n_chunks, True)

    def body(q_ref, k_ref, v_ref, g_ref, bl_ref, al_ref, a_ref, dt_ref, on_ref, st_ref, ti_ref, do_ref,
             dqkv_ref, dg_ref, dba_ref, da_ref, ddt_ref, don_ref, dstate):
        h, n = pl.program_id(0), pl.program_id(1)

        @pl.when(n == 0)
        def _():
            dstate[...] = jnp.zeros_like(dstate)
            da_ref[...] = jnp.zeros_like(da_ref)
            ddt_ref[...] = jnp.zeros_like(ddt_ref)

        @pl.when((n == 0) & (h == 0))
        def _():
            don_ref[...] = jnp.zeros_like(don_ref)

        t_known = ti_ref[...]
        _, vjp = jax.vjp(lambda *ins: _gdn_chunk(*ins, t_known=t_known)[:2],
                         _heads(q_ref), _heads(k_ref), _heads(v_ref), _heads(g_ref), _heads(bl_ref), _heads(al_ref),
                         _heads(a_ref), _heads(dt_ref), on_ref[...], st_ref[...])
        dq, dk, dv, dg, dbl, dal, da, ddt, don, dst = vjp((_heads(do_ref).astype(F32), dstate[...]))
        lane = lax.broadcasted_iota(jnp.int32, (GDN_C, LANES), 1)
        dba = jnp.zeros((GDN_C, LANES), F32)
        for hh in range(GDN_HP):
            cols = slice(hh * GDN_D, (hh + 1) * GDN_D)
            for part, d in enumerate((dq, dk, dv)):
                dqkv_ref[:, part * H * GDN_D + hh * GDN_D:part * H * GDN_D + (hh + 1) * GDN_D] = d[hh]
            dg_ref[:, cols] = dg[hh].astype(dg_ref.dtype)
            dba = jnp.where(lane == hh, jnp.sum(dbl[hh], axis=-1, keepdims=True), dba)
            dba = jnp.where(lane == H + hh, jnp.sum(dal[hh], axis=-1, keepdims=True), dba)
            da_ref[:, cols] += da[hh]
            ddt_ref[:, cols] += ddt[hh]
        dba_ref[...] = dba.astype(dba_ref.dtype)
        don_ref[...] += don
        dstate[...] = dst

    tok0 = tok(0)
    B = _GB
    f32_tok = jax.ShapeDtypeStruct((T, H * GDN_D), F32)
    bf_tok = jax.ShapeDtypeStruct((T, H * GDN_D), BF16)
    par_sh = jax.ShapeDtypeStruct((1, H * GDN_D), F32)
    return pl.pallas_call(
        body, name=name, grid=(B, n_chunks),
        in_specs=[tok(0), tok(B), tok(2 * B), tok(3 * B), tok(4 * B), tok(5 * B), par, par, shared, st_spec,
                  pl.BlockSpec((GDN_HP, None, GDN_C, GDN_C), lambda h, n: (h, n_chunks - 1 - n, 0, 0)), tok0],
        out_specs=[pl.BlockSpec((GDN_C, 3 * H * GDN_D), lambda h, n: (n_chunks - 1 - n, 0)), tok0,
                   pl.BlockSpec((GDN_C, LANES), lambda h, n: (n_chunks - 1 - n, 0)), par, par, shared],
        out_shape=[jax.ShapeDtypeStruct((T, 3 * H * GDN_D), F32), bf_tok, jax.ShapeDtypeStruct((T, LANES), BF16), par_sh, par_sh,
                   jax.ShapeDtypeStruct((1, GDN_D), F32)],
        scratch_shapes=[pltpu.VMEM((GDN_HP, GDN_D, GDN_D), F32)],
        compiler_params=_cparams(("arbitrary", "arbitrary")),
    )(qkv, qkv, qkv, z, z, z, a_log_x, dt_bias_x, o_norm, states, t_invs, do)


def _prefetch_call(body, name, grid, in_specs, out_specs, out_shape, aliases=None):
    return pl.pallas_call(
        body, name=name,
        grid_spec=pltpu.PrefetchScalarGridSpec(num_scalar_prefetch=1, grid=grid, in_specs=in_specs, out_specs=out_specs),
        out_shape=out_shape, input_output_aliases=aliases or {},
        compiler_params=_cparams(("parallel",) * len(grid)))


def cast_into(src, src_row0, rows, slab, row0, me, name):
    width = src.shape[1]
    tr = _pick_rows(rows, 1024, row0, src_row0)
    assert rows % tr == 0 and row0 % tr == 0 and src_row0 % tr == 0

    def body(me_ref, s_ref, *refs):
        refs[-1][...] = s_ref[...].astype(refs[-1].dtype)

    in_specs = [pl.BlockSpec((tr, width), lambda r, me_ref: (src_row0 // tr + r, 0))]
    args = [src]
    aliases = {}
    if slab.arr is not None:
        in_specs.append(_ANY)
        args.append(slab.arr)
        aliases = {2: 0}
    slab.arr = _prefetch_call(
        body, name, (rows // tr,), in_specs,
        pl.BlockSpec((None, tr, width), lambda r, me_ref: (me_ref[0], row0 // tr + r, 0)),
        jax.ShapeDtypeStruct(slab.shape, slab.dtype), aliases)(me, *args)


def pair_add(g, b, c_idx, name):
    n, rh, w = b.shape
    tr = _pick_rows(rh, 1024)
    nb = rh // tr

    def body(c_ref, g_ref, b_ref, o_ref):
        o_ref[...] = (g_ref[...].astype(F32) + b_ref[...].astype(F32)).astype(o_ref.dtype)

    return _prefetch_call(
        body, name, (n, nb),
        [pl.BlockSpec((None, tr, w), lambda k, r, c: (k, c[0] * nb + r, 0)), pl.BlockSpec((None, tr, w), lambda k, r, c: (k, r, 0))],
        pl.BlockSpec((None, tr, w), lambda k, r, c: (k, r, 0)), jax.ShapeDtypeStruct(b.shape, BF16))(c_idx, g, b)


def chip_sum(p, rv, mc, name):
    n, rh, w = p.shape
    tr = _pick_rows(rh, 512)
    nb = rh // tr

    def body(mc_ref, p_ref, rv_ref, o_ref):
        me = mc_ref[0]
        acc = None
        for k in range(n):
            part = jnp.where(me == k, p_ref[...], rv_ref[k]).astype(F32)
            acc = part if acc is None else acc + part
        o_ref[...] = acc

    return _prefetch_call(
        body, name, (nb,),
        [pl.BlockSpec((None, tr, w), lambda r, mc_ref: (mc_ref[0], r, 0)), pl.BlockSpec((n, tr, w), lambda r, mc_ref: (0, r, 0))],
        pl.BlockSpec((tr, w), lambda r, mc_ref: (mc_ref[1] * nb + r, 0)), jax.ShapeDtypeStruct((2 * rh, w), F32))(mc, p, rv)


def adamw(red, row0, w, m, v, w_row0, rows, prev, name):
    cols = w.shape[1]
    tr = _pick_rows(rows, 512, row0, w_row0)
    assert rows % tr == 0 and row0 % tr == 0 and w_row0 % tr == 0

    def body(g_ref, w_ref, m_ref, v_ref, *refs):
        go_ref, d_ref, nm_ref, nv_ref = refs[-4:]
        gv = g_ref[...]
        nm = ADAM_B1 * m_ref[...] + (1.0 - ADAM_B1) * gv
        nv = ADAM_B2 * v_ref[...] + (1.0 - ADAM_B2) * (gv * gv)
        m_hat = nm / (1.0 - ADAM_B1 ** ADAM_STEP)
        v_hat = nv / (1.0 - ADAM_B2 ** ADAM_STEP)
        go_ref[...] = gv
        d_ref[...] = -ADAM_LR * (m_hat / (jnp.sqrt(v_hat) + ADAM_EPS) + ADAM_WD * w_ref[...])
        nm_ref[...] = nm
        nv_ref[...] = nv

    spec = pl.BlockSpec((tr, cols), lambda r: (w_row0 // tr + r, 0))
    sh = jax.ShapeDtypeStruct(w.shape, F32)
    in_specs = [pl.BlockSpec((tr, cols), lambda r: (row0 // tr + r, 0)), spec, spec, spec]
    args, aliases = [red, w, m, v], {}
    if prev is not None:
        in_specs += [_ANY] * 4
        args += list(prev)
        aliases = {4 + k: k for k in range(4)}
    return pl.pallas_call(
        body, name=name, grid=(rows // tr,), in_specs=in_specs, out_specs=[spec] * 4, out_shape=[sh] * 4,
        input_output_aliases=aliases, compiler_params=_cparams(("parallel",)),
    )(*args)


def _place():
    x, y, c = lax.axis_index("x"), lax.axis_index("y"), lax.axis_index("c")
    chips = [(1 - x, y), (x, 1 - y), (1 - x, 1 - y)]
    return x, y, c, chips


def _chip_index(cx, cy):
    return 2 * cx + cy


def _remote(src, dst, send_sem, recv_sem, to):
    return pltpu.make_async_remote_copy(src_ref=src, dst_ref=dst, send_sem=send_sem, recv_sem=recv_sem,
                                        device_id=to, device_id_type=MESH)


def _comm_call(body, name, ins, out_shapes, n_sems, aliases):
    return pl.pallas_call(
        body, name=name, in_specs=[_ANY] * len(ins), out_specs=[_ANY] * len(out_shapes), out_shape=out_shapes,
        scratch_shapes=[pltpu.SemaphoreType.DMA((n_sems,)), pltpu.SemaphoreType.DMA((n_sems,))],
        input_output_aliases=aliases,
    )(*ins)


def pair_swap_halves(slabs, name="grad_pair_swap"):
    n = len(slabs)

    def body(*refs):
        in_refs, out_refs, send_sems, recv_sems = refs[:n], refs[n:2 * n], refs[-2], refs[-1]
        x, y, c, _ = _place()
        cps = []
        for a in range(n):
            rh = in_refs[a].shape[1] // 2
            cp = _remote(in_refs[a].at[:, pl.ds((1 - c) * rh, rh), :], out_refs[a], send_sems.at[a], recv_sems.at[a], (x, y, 1 - c))
            cp.start()
            cps.append(cp)
        for cp in cps:
            cp.wait()

    outs = [jax.ShapeDtypeStruct((s.shape[0], s.shape[1] // 2, s.shape[2]), s.dtype) for s in slabs]
    return _comm_call(body, name, slabs, outs, n, {})


def pair_join_halves(reds, name="grad_pair_join"):
    n = len(reds)

    def body(*refs):
        in_refs, out_refs, send_sems, recv_sems = refs[:n], refs[n:2 * n], refs[-2], refs[-1]
        x, y, c, _ = _place()
        cps = []
        for a in range(n):
            rh = in_refs[a].shape[0] // 2
            mine = pl.ds(c * rh, rh)
            cp = _remote(in_refs[a].at[mine], out_refs[a].at[mine], send_sems.at[a], recv_sems.at[a], (x, y, 1 - c))
            cp.start()
            cps.append(cp)
        for a in range(n):
            rh = in_refs[a].shape[0] // 2
            got = out_refs[a].at[pl.ds((1 - c) * rh, rh)]
            _remote(got, got, send_sems.at[a], recv_sems.at[a], (x, y, 1 - c)).wait_recv()
        for cp in cps:
            cp.wait_send()

    return _comm_call(body, name, reds, [jax.ShapeDtypeStruct(r.shape, r.dtype) for r in reds], n, {a: a for a in range(n)})


_HBM = pl.BlockSpec(memory_space=pltpu.HBM)
_SEM = pl.BlockSpec(memory_space=pltpu.SEMAPHORE)
_EFFECT = pltpu.SideEffectType.DATAFLOW_SIDE_EFFECTING


def _in_hbm(a):
    return pltpu.with_memory_space_constraint(a, pltpu.HBM)


def _hbm_like(a):
    return pltpu.HBM(a.shape, a.dtype)


def _start_call(body, name, ins, n_sems, after):
    n = len(ins)
    res = pl.pallas_call(
        body, name=name, in_specs=[_HBM] * n + [_ANY],
        out_specs=[_SEM, _SEM] + [_HBM] * n + [pl.BlockSpec(memory_space=pltpu.VMEM)],
        out_shape=[pltpu.SemaphoreType.DMA((n_sems,)), pltpu.SemaphoreType.DMA((n_sems,))] + [_hbm_like(a) for a in ins]
        + [jax.ShapeDtypeStruct((8, LANES), F32)],
        input_output_aliases={a: 2 + a for a in range(n)},
        compiler_params=pltpu.CompilerParams(has_side_effects=_EFFECT),
    )(*[_in_hbm(a) for a in ins], after)
    return res[0], res[1], list(res[2:2 + n]), res[-1]


def _wait_call(body, name, thru, send_sems, recv_sems, after):
    n = len(thru)
    after = list(after) if isinstance(after, (list, tuple)) else [after]
    return pl.pallas_call(
        body, name=name, in_specs=[_HBM] * n + [_SEM, _SEM] + [_ANY] * len(after), out_specs=[_HBM] * n,
        out_shape=[_hbm_like(a) for a in thru], input_output_aliases={a: a for a in range(n)},
        compiler_params=pltpu.CompilerParams(has_side_effects=_EFFECT),
    )(*thru, send_sems, recv_sems, *after)


def gather_start(slabs, after, name="weight_gather_start"):
    n = len(slabs)

    def body(*refs):
        g_refs, send_sems, recv_sems, token = refs[:n], refs[n + 1], refs[n + 2], refs[-1]
        x, y, c, chips = _place()
        me = _chip_index(x, y)
        for a in range(n):
            rh = g_refs[a].shape[1] // 2
            mine = g_refs[a].at[me, pl.ds(c * rh, rh)]
            for j, chip in enumerate(chips):
                _remote(mine, mine, send_sems.at[3 * a + j], recv_sems.at[3 * a + j], (*chip, c)).start()
        token[...] = jnp.zeros_like(token)

    return _start_call(body, name, slabs, 3 * n, after)


def gather_wait(send_sems, recv_sems, thru, after, name="weight_gather_wait"):
    n = len(thru)

    def body(*refs):
        g_refs, send_sems, recv_sems = refs[:n], refs[n], refs[n + 1]
        x, y, c, chips = _place()
        me = _chip_index(x, y)
        for a in range(n):
            rh = g_refs[a].shape[1] // 2
            rows = pl.ds(c * rh, rh)
            for j, chip in enumerate(chips):
                mine, got = g_refs[a].at[me, rows], g_refs[a].at[_chip_index(*chip), rows]
                _remote(mine, mine, send_sems.at[3 * a + j], recv_sems.at[3 * a + j], (*chip, c)).wait_send()
                _remote(got, got, send_sems.at[3 * a + j], recv_sems.at[3 * a + j], (*chip, c)).wait_recv()

    return _wait_call(body, name, thru, send_sems, recv_sems, after)


def gather_forward(slabs, name="weight_gather_forward"):
    n = len(slabs)

    def body(*refs):
        in_refs, out_refs, send_sems, recv_sems = refs[:n], refs[n:2 * n], refs[-2], refs[-1]
        x, y, c, chips = _place()
        sib = (x, y, 1 - c)
        sends = []
        for a in range(n):
            rh = in_refs[a].shape[1] // 2
            for j, chip in enumerate(chips):
                k = _chip_index(*chip)
                cp = _remote(in_refs[a].at[k, pl.ds(c * rh, rh)], out_refs[a].at[k, pl.ds(c * rh, rh)], send_sems.at[3 * a + j],
                             recv_sems.at[3 * a + j], sib)
                cp.start()
                sends.append(cp)
        for a in range(n):
            rh = in_refs[a].shape[1] // 2
            for j, chip in enumerate(chips):
                got = out_refs[a].at[_chip_index(*chip), pl.ds((1 - c) * rh, rh)]
                _remote(got, got, send_sems.at[3 * a + j], recv_sems.at[3 * a + j], sib).wait_recv()
        for cp in sends:
            cp.wait_send()

    return _comm_call(body, name, slabs, [jax.ShapeDtypeStruct(s.shape, s.dtype) for s in slabs], 3 * n, {a: a for a in range(n)})


def exchange_start(parts, after, name="grad_exchange_start"):
    n = len(parts)

    def body(*refs):
        p_refs, land_refs, send_sems, recv_sems, token = refs[:n], refs[n:2 * n], refs[2 * n + 1], refs[2 * n + 2], refs[-1]
        x, y, c, chips = _place()
        me = _chip_index(x, y)
        for a in range(n):
            for j, chip in enumerate(chips):
                _remote(p_refs[a].at[_chip_index(*chip)], land_refs[a].at[me], send_sems.at[3 * a + j], recv_sems.at[3 * a + j],
                        (*chip, c)).start()
        token[...] = jnp.zeros_like(token)

    return _start_call(body, name, list(parts) + [lax.empty(p.shape, p.dtype) for p in parts], 3 * n, after)


def exchange_wait(send_sems, recv_sems, thru, after, name="grad_exchange_wait"):
    n = len(thru) // 2

    def body(*refs):
        p_refs, land_refs, send_sems, recv_sems = refs[:n], refs[n:2 * n], refs[2 * n], refs[2 * n + 1]
        x, y, c, chips = _place()
        me = _chip_index(x, y)
        for a in range(n):
            for j, chip in enumerate(chips):
                k = _chip_index(*chip)
                _remote(p_refs[a].at[k], land_refs[a].at[me], send_sems.at[3 * a + j], recv_sems.at[3 * a + j], (*chip, c)).wait_send()
                _remote(land_refs[a].at[k], land_refs[a].at[k], send_sems.at[3 * a + j], recv_sems.at[3 * a + j], (*chip, c)).wait_recv()

    res = _wait_call(body, name, thru, send_sems, recv_sems, after)
    return res[:n], res[n:]


_SLABS = {
    "mla_w_in": [("mla_w_in", 1, 0, 2)], "mla_w_uq": [("mla_w_uq", 2, 0, 2)], "mla_w_ukv": [("mla_w_ukv", 2, 0, 2)],
    "l0_mla_w_o": [("mla_w_o", 1, 0, 1)],
    "l0_w1024": [("mlp_w1", 2, 0, 1), ("mlp_w2", 1, 0, 1), ("xa_w_q", 1, 0, 1), ("xa_w_o", 1, 0, 1)],
    "l0_xa_w_kv": [("xa_w_kv", 2, 0, 1)],
    "l1_w1024": [("mlp_w1", 2, 1, 2), ("mlp_w2", 1, 1, 2), ("xa_w_q", 1, 1, 2), ("xa_w_o", 1, 1, 2), ("gdn_w_o", 1, 0, 1)],
    "l1_xa_w_kv": [("xa_w_kv", 2, 1, 2)], "gdn_w_in": [("gdn_w_in", 2, 0, 1)],
    "l23_w1024": [("mlp_w1", 2, 2, 4), ("mlp_w2", 1, 2, 4), ("xa_w_q", 1, 2, 4), ("xa_w_o", 1, 2, 4), ("mla_w_o", 1, 1, 2),
                  ("sc_w_o", 1, 0, 1)],
    "l23_xa_w_kv": [("xa_w_kv", 2, 2, 4)], "sc_w_in": [("sc_w_in", 2, 0, 1)],
}
_GROUPS = [(["mla_w_in", "mla_w_uq", "mla_w_ukv", "l0_mla_w_o"], None),
           (["l0_w1024", "l0_xa_w_kv"], (0, "xa")),
           (["l1_w1024", "l1_xa_w_kv", "gdn_w_in"], (1, "mix")),
           (["l23_w1024", "l23_xa_w_kv", "sc_w_in"], (2, "mix"))]
_RELAID = ("mla_w_in", "mla_w_uq", "mla_w_ukv", "gdn_w_in")
_SMALL = [("mla_q_norm", 1), ("mla_kv_norm", 1), ("gdn_conv_w", 2), ("sc_conv_w", 2)]
_REPL = ["gdn_a_log", "gdn_dt_bias", "gdn_o_norm", "norm_mix", "norm_mem", "norm_mlp", "mem_norm", "final_norm"]
_WEIGHTS = ['mla_w_in', 'mla_q_norm', 'mla_kv_norm', 'mla_w_uq', 'mla_w_ukv', 'mla_w_o', 'gdn_w_in', 'gdn_conv_w',
            'gdn_a_log', 'gdn_dt_bias', 'gdn_o_norm', 'gdn_w_o', 'sc_w_in', 'sc_conv_w', 'sc_w_o', 'norm_mix',
            'norm_mem', 'norm_mlp', 'xa_w_q', 'xa_w_kv', 'xa_w_o', 'mlp_w1', 'mlp_w2', 'mem_norm', 'final_norm']


class Layout:
    def __init__(self, shard_shapes):
        self.members, self.where, self.slab_dims = {}, {}, {}
        for slab, members in _SLABS.items():
            off, rows = 0, []
            for name, axis, l0, l1 in members:
                _, rpl, width = shard_shapes[name]
                rows.append((name, off, l0, l1, rpl))
                for layer in range(l0, l1):
                    self.where[(name, layer)] = (slab, off + (layer - l0) * rpl, rpl, width, axis)
                off += (l1 - l0) * rpl
            self.members[slab], self.slab_dims[slab] = rows, (off, width)

    def new_slabs(self, dtype):
        return {s: Slab(rows, width, dtype) for s, (rows, width) in self.slab_dims.items()}

    def loc(self, slabs, name, layer):
        slab, row0, rpl, width, axis = self.where[(name, layer)]
        if axis == 1:
            return Loc(slabs[slab], row0, N_CHIPS * rpl, width, 0)
        return Loc(slabs[slab], row0, rpl, N_CHIPS * width, 1)

    def _whole(self, name):
        (member,) = self.members[name]
        _, off, l0, l1, rpl = member
        assert off == 0 and l0 == 0
        return l1, rpl, self.slab_dims[name][1], dict((n, a) for n, a, _, _ in _SLABS[name])[name]

    def full(self, slabs, name):
        layers, rpl, width, axis = self._whole(name)
        blocks = slabs[name].arr.reshape(N_CHIPS, layers, rpl, width)
        return jnp.concatenate([blocks[s] for s in range(N_CHIPS)], axis=axis)

    def put_full(self, slabs, name, grad):
        layers, rpl, width, axis = self._whole(name)
        parts = jnp.stack(jnp.split(grad, N_CHIPS, axis=axis)).reshape(N_CHIPS, layers * rpl, width)
        slabs[name].arr = parts.astype(slabs[name].dtype)


def _small_pack(vals, names):
    flat = jnp.concatenate([vals[n].astype(F32).reshape(-1) for n in names])
    return jnp.pad(flat, (0, SMALL_ROWS * SMALL_COLS - flat.shape[0])).reshape(SMALL_ROWS, SMALL_COLS)


def _small_unpack(flat, like, names):
    out, off = {}, 0
    flat = flat.reshape(-1)
    for n in names:
        out[n] = flat[off:off + like[n].size].reshape(like[n].shape)
        off += like[n].size
    return out


_MLA_CFG = _Attn(MLA_H, 2 * LANES, MLA_NOPE, MLA_V, True, (MLA_NOPE + MLA_ROPE) ** -0.5, hp=8, hp_kv=8)
_XA_CFG = _Attn(XA_H, XA_D, XA_D, XA_D, False, XA_D ** -0.5, hp=4, hp_kv=4)


def _mla_weights(w_in, w_uq, w_ukv):
    w_in_p = jnp.pad(w_in, ((0, 0), (0, MLA_ZPAD - w_in.shape[1])))
    w_uq_p = jnp.pad(w_uq.reshape(MLA_QR, MLA_H, MLA_NOPE + MLA_ROPE), ((0, 0), (0, 0), (0, 2 * LANES - MLA_NOPE - MLA_ROPE)))
    w_uq_p = w_uq_p.reshape(MLA_QR, MLA_H * 2 * LANES)
    kv = w_ukv.reshape(MLA_KVR, MLA_H, MLA_NOPE + MLA_V)
    w_ukv_p = jnp.concatenate([kv[:, :, :MLA_NOPE].reshape(MLA_KVR, -1), kv[:, :, MLA_NOPE:].reshape(MLA_KVR, -1)], axis=1)
    return w_in_p, w_uq_p, w_ukv_p


def _mla_weight_grads(d_in_p, d_uq_p, d_ukv_p):
    d_in = d_in_p[:, :MLA_QR + MLA_KVR + MLA_ROPE]
    d_uq = d_uq_p.reshape(MLA_QR, MLA_H, 2 * LANES)[:, :, :MLA_NOPE + MLA_ROPE].reshape(MLA_QR, -1)
    half = MLA_H * MLA_NOPE
    d_ukv = jnp.concatenate([d_ukv_p[:, :half].reshape(MLA_KVR, MLA_H, MLA_NOPE),
                             d_ukv_p[:, half:].reshape(MLA_KVR, MLA_H, MLA_V)], axis=2).reshape(MLA_KVR, -1)
    return d_in, d_uq, d_ukv


def _mla_fwd(xs, h, wts, w_o, qn, kvn, tabs, g_next, tag):
    w_in_p, w_uq_p, w_ukv_p = wts
    z = mm(h, w_in_p, "nn", f"{tag}_in")
    cq, ckv, kr = mla_mid_fwd(z, qn, kvn, tabs, f"{tag}_mid")
    q = mm(cq, w_uq_p, "nn", f"{tag}_uq", outs=(BF16,), epi=_epi_rope_q, per_row=tabs, tm=512)
    kv = mm(ckv, w_ukv_p, "nn", f"{tag}_ukv", outs=(BF16,))
    o, lse = flash_fwd(_MLA_CFG, q, kv, kv, kr, f"{tag}_attn")
    xs, h_next = residual_norm(o, w_o, xs, g_next, f"{tag}_out")
    return xs, h_next, (z, cq, ckv, kr, q, kv, o, lse)


def _mla_bwd(dx, h, wts, w_o, g_wo, qn, kvn, tabs, saved, tag):
    w_in_p, w_uq_p, w_ukv_p = wts
    z, cq, ckv, kr, q, kv, o, lse = saved
    mm(o, dx, "tn", f"{tag}_dwo", outs=(BF16,), out_loc=g_wo)
    do = mm(dx, w_o, "nt", f"{tag}_do", outs=(BF16,))
    dqp, delta = flash_dq(_MLA_CFG, q, kv, kv, kr, o, do, lse, BF16, f"{tag}_attn_dq", rope_tabs=tabs)
    dkv, dkr = flash_dkv(_MLA_CFG, q, kv, kv, kr, do, lse, delta, BF16, f"{tag}_attn_dkv")
    d_uq_p = mm(cq, dqp, "tn", f"{tag}_duq")
    dcq = mm(dqp, w_uq_p, "nt", f"{tag}_dcq")
    d_ukv_p = mm(ckv, dkv, "tn", f"{tag}_dukv")
    dckv = mm(dkv, w_ukv_p, "nt", f"{tag}_dckv")
    dz, dqn, dkvn = mla_mid_bwd(z, qn, kvn, tabs, dcq, dckv, dkr, f"{tag}_mid_bwd")
    d_in_p = mm(h, dz, "tn", f"{tag}_din")
    dh = (dz, w_in_p)
    d_in, d_uq, d_ukv = _mla_weight_grads(d_in_p, d_uq_p, d_ukv_p)
    return dh, dict(mla_w_in=d_in, mla_w_uq=d_uq, mla_w_ukv=d_ukv, mla_q_norm=dqn, mla_kv_norm=dkvn)


_GDN_QKV = 3 * GDN_H * GDN_D
_GDN_GATE_END = _GDN_QKV + GDN_H * GDN_D


def _gdn_weights(w_in):
    rep = lambda cols: jnp.repeat(cols, GDN_D, axis=1)
    return jnp.concatenate([w_in[:, :_GDN_GATE_END], rep(w_in[:, _GDN_GATE_END:_GDN_GATE_END + GDN_H]),
                            rep(w_in[:, _GDN_GATE_END + GDN_H:])], axis=1)


def _fold(x):
    return x.reshape(x.shape[0], -1, GDN_D).sum(-1)


def _gdn_fwd(xs, h, w_in_x, conv_w, a_log, dt_bias, o_norm, w_o, g_next, tag):
    z = mm(h, w_in_x, "nn", f"{tag}_in")
    qkv = gdn_conv_fwd(z, conv_w, f"{tag}_conv")
    a_x, dt_x = jnp.repeat(a_log.reshape(1, -1), GDN_D, axis=1), jnp.repeat(dt_bias.reshape(1, -1), GDN_D, axis=1)
    og, states, t_invs = gdn_chunk_fwd(qkv, z, a_x, dt_x, o_norm.reshape(1, -1), f"{tag}_chunks")
    xs, h_next = residual_norm(og, w_o, xs, g_next, f"{tag}_out")
    return xs, h_next, (z, qkv, a_x, dt_x, og, states, t_invs)


def _gdn_weights_compact(w_in):
    return jnp.pad(w_in, ((0, 0), (0, LANES - 2 * GDN_H)))


def _gdn_bwd(dx, h, w_in_c, conv_w, o_norm, w_o, g_wo, saved, tag):
    z, qkv, a_x, dt_x, og, states, t_invs = saved
    mm(og, dx, "tn", f"{tag}_dwo", outs=(BF16,), out_loc=g_wo)
    dog = mm(dx, w_o, "nt", f"{tag}_dog")
    dqkv, dgate, dba, da_x, ddt_x, don = gdn_chunk_bwd(qkv, z, a_x, dt_x, o_norm.reshape(1, -1), states, t_invs, dog,
                                                       f"{tag}_chunks_bwd")
    dpre, dconv = gdn_conv_bwd(z, conv_w, dqkv, f"{tag}_conv_bwd")
    dz = jnp.concatenate([dpre, dgate, dba], axis=1)
    d_in_c = mm(h, dz, "tn", f"{tag}_din")
    dh = (dz, w_in_c)
    return dh, dict(gdn_w_in=d_in_c[:, :_GDN_GATE_END + 2 * GDN_H], gdn_conv_w=dconv, gdn_a_log=_fold(da_x).reshape(-1),
                    gdn_dt_bias=_fold(ddt_x).reshape(-1), gdn_o_norm=don.reshape(-1))


def _sc_fwd(xs, h, w_in, conv_w, w_o, g_next, tag):
    z = mm(h, w_in, "nn", f"{tag}_in")
    y = sc_fwd(z, conv_w, f"{tag}_conv")
    xs, h_next = residual_norm(y, w_o, xs, g_next, f"{tag}_out")
    return xs, h_next, (z, y)


def _sc_bwd(dx, h, w_in, g_win, conv_w, w_o, g_wo, saved, tag):
    z, y = saved
    mm(y, dx, "tn", f"{tag}_dwo", outs=(BF16,), out_loc=g_wo)
    dy = mm(dx, w_o, "nt", f"{tag}_dy")
    db, dc, du, dconv = sc_bwd(z, conv_w, dy, f"{tag}_conv_bwd")
    dz = jnp.concatenate([db, dc, du], axis=1)
    mm(h, dz, "tn", f"{tag}_din", outs=(BF16,), out_loc=g_win)
    dh = (dz, w_in)
    return dh, dict(sc_conv_w=dconv)


def local_step(x, mem, pos, target, lay, wslabs, gslabs, small, before=None, after_bwd=None):
    depth = small["norm_mix"].shape[0]
    W = lambda name, layer: lay.loc(wslabs, name, layer)
    G = lambda name, layer: lay.loc(gslabs, name, layer)
    tabs = rope_tables(pos)
    mem_n = rmsnorm_fwd(mem, small["mem_norm"], "mem_norm")
    full = {n: lay.full(wslabs, n) for n in ("mla_w_in", "mla_w_uq", "mla_w_ukv")}
    mla_w = [_mla_weights(full["mla_w_in"][j], full["mla_w_uq"][j], full["mla_w_ukv"][j]) for j in range(full["mla_w_in"].shape[0])]
    gdn_in_x, gdn_in_c = {}, {}

    xs, h_pre = x, None
    saved = []
    for i in range(depth):
        j, kind = i // 3, i % 3
        tag = f"l{i}"
        if before is not None:
            xs = before(i, "mix", xs)
        if kind == 1:
            gdn_full = lay.full(wslabs, "gdn_w_in")[j]
            gdn_in_x[j], gdn_in_c[j] = _gdn_weights(gdn_full), _gdn_weights_compact(gdn_full)
        x_a = xs
        h = h_pre if h_pre is not None else rmsnorm_fwd(xs, small["norm_mix"][i], f"{tag}_norm_mix")
        g_mem = small["norm_mem"][i]
        if kind == 0:
            xs, hn, mix = _mla_fwd(xs, h, mla_w[j], W("mla_w_o", j), small["mla_q_norm"][j], small["mla_kv_norm"][j], tabs, g_mem,
                                   f"{tag}_mla")
        elif kind == 1:
            xs, hn, mix = _gdn_fwd(xs, h, gdn_in_x[j], small["gdn_conv_w"][j], small["gdn_a_log"][j], small["gdn_dt_bias"][j],
                                   small["gdn_o_norm"][j], W("gdn_w_o", j), g_mem, f"{tag}_gdn")
        else:
            xs, hn, mix = _sc_fwd(xs, h, W("sc_w_in", j), small["sc_conv_w"][j], W("sc_w_o", j), g_mem, f"{tag}_sc")
        if before is not None:
            xs = before(i, "xa", xs)
        x_b = xs
        xq = mm(hn, W("xa_w_q", i), "nn", f"{tag}_xa_q", outs=(BF16,))
        xkv = mm(mem_n, W("xa_w_kv", i), "nn", f"{tag}_xa_kv", outs=(BF16,))
        xo, xlse = flash_fwd(_XA_CFG, xq, xkv, xkv, None, f"{tag}_xa_attn")
        xs, hm = residual_norm(xo, W("xa_w_o", i), xs, small["norm_mlp"][i], f"{tag}_xa_out")
        x_c = xs
        h1, act = mm(hm, W("mlp_w1", i), "nn", f"{tag}_mlp_up", outs=(BF16, BF16), epi=_epi_relu2)
        xs, h_pre = residual_norm(act, W("mlp_w2", i), xs, small["norm_mix"][i + 1] if i + 1 < depth else None,
                                  f"{tag}_mlp_down", tm=512)
        saved.append((x_a, h, mix, x_b, hn, xq, xkv, xo, xlse, x_c, hm, h1, act))

    se, dx, d_final = loss_head(xs, small["final_norm"], target)

    per_layer = {n: [None] * depth for n in ("norm_mix", "norm_mem", "norm_mlp")}
    mixer = {}
    dmem_n = jnp.zeros(mem.shape, F32)
    for i in reversed(range(depth)):
        j, kind = i // 3, i % 3
        tag = f"l{i}"
        x_a, h, mix, x_b, hn, xq, xkv, xo, xlse, x_c, hm, h1, act = saved[i]
        mm(act, dx, "tn", f"{tag}_mlp_dw2", outs=(BF16,), out_loc=G("mlp_w2", i))
        dh1 = mm(dx, W("mlp_w2", i), "nt", f"{tag}_mlp_dh1", outs=(BF16,), epi=_epi_relu2_bwd, extras=(h1,))
        mm(hm, dh1, "tn", f"{tag}_mlp_dw1", outs=(BF16,), out_loc=G("mlp_w1", i))
        dx, dg = mm(dh1, W("mlp_w1", i), "nt", f"{tag}_mlp_dhm", epi=_epi_norm_bwd, extras=(x_c, dx), vecs=(small["norm_mlp"][i],),
                    row_outs=1, tm=512)
        per_layer["norm_mlp"][i] = dg.reshape(-1)
        mm(xo, dx, "tn", f"{tag}_xa_dwo", outs=(BF16,), out_loc=G("xa_w_o", i))
        dxo = mm(dx, W("xa_w_o", i), "nt", f"{tag}_xa_do", outs=(BF16,))
        dxq, xdelta = flash_dq(_XA_CFG, xq, xkv, xkv, None, xo, dxo, xlse, BF16, f"{tag}_xa_attn_dq")
        (dxkv,) = flash_dkv(_XA_CFG, xq, xkv, xkv, None, dxo, xlse, xdelta, BF16, f"{tag}_xa_attn_dkv")
        mm(hn, dxq, "tn", f"{tag}_xa_dwq", outs=(BF16,), out_loc=G("xa_w_q", i))
        dx, dg = mm(dxq, W("xa_w_q", i), "nt", f"{tag}_xa_dhn", epi=_epi_norm_bwd, extras=(x_b, dx), vecs=(small["norm_mem"][i],),
                    row_outs=1, tm=512)
        per_layer["norm_mem"][i] = dg.reshape(-1)
        mm(mem_n, dxkv, "tn", f"{tag}_xa_dwkv", outs=(BF16,), out_loc=G("xa_w_kv", i))
        dmem_n = mm(dxkv, W("xa_w_kv", i), "nt", f"{tag}_xa_dmem", epi=_epi_add, extras=(dmem_n,))
        if after_bwd is not None:
            dx = after_bwd(i, "xa", dx)
        if kind == 0:
            dh, gr = _mla_bwd(dx, h, mla_w[j], W("mla_w_o", j), G("mla_w_o", j), small["mla_q_norm"][j], small["mla_kv_norm"][j],
                              tabs, mix, f"{tag}_mla")
        elif kind == 1:
            dh, gr = _gdn_bwd(dx, h, gdn_in_c[j], small["gdn_conv_w"][j], small["gdn_o_norm"][j], W("gdn_w_o", j), G("gdn_w_o", j),
                              mix, f"{tag}_gdn")
        else:
            dh, gr = _sc_bwd(dx, h, W("sc_w_in", j), G("sc_w_in", j), small["sc_conv_w"][j], W("sc_w_o", j), G("sc_w_o", j),
                             mix, f"{tag}_sc")
        if kind == 1:
            lay.put_full(gslabs, "gdn_w_in", gr.pop("gdn_w_in")[None])
        for n, g in gr.items():
            mixer.setdefault(n, {})[j] = g
        dz_mix, w_mix = dh
        dx, dg = mm(dz_mix, w_mix, "nt", f"{tag}_mix_dh", epi=_epi_norm_bwd, extras=(x_a, dx), vecs=(small["norm_mix"][i],),
                    row_outs=1, tm=256 if kind == 1 else 512)
        per_layer["norm_mix"][i] = dg.reshape(-1)
        if after_bwd is not None:
            dx = after_bwd(i, "mix", dx)

    _, d_mem_norm = rmsnorm_bwd(mem, small["mem_norm"], dmem_n, jnp.zeros(mem.shape, F32), "mem_norm_bwd")
    grads = {n: jnp.stack(v) for n, v in per_layer.items()}
    for n, by_j in mixer.items():
        grads[n] = jnp.stack([by_j[j] for j in sorted(by_j)])
    grads["mem_norm"] = d_mem_norm
    grads["final_norm"] = d_final
    for n in ("mla_w_in", "mla_w_uq", "mla_w_ukv"):
        lay.put_full(gslabs, n, grads.pop(n))
    return se, dx, grads


def kernel(x, mem, positions, mla_w_in, mla_q_norm, mla_kv_norm, mla_w_uq, mla_w_ukv, mla_w_o, gdn_w_in, gdn_conv_w, gdn_a_log, gdn_dt_bias, gdn_o_norm, gdn_w_o, sc_w_in, sc_conv_w, sc_w_o, norm_mix, norm_mem, norm_mlp, xa_w_q, xa_w_kv, xa_w_o, mlp_w1, mlp_w2, mem_norm, final_norm, loss_target, m_mla_w_in, m_mla_q_norm, m_mla_kv_norm, m_mla_w_uq, m_mla_w_ukv, m_mla_w_o, m_gdn_w_in, m_gdn_conv_w, m_gdn_a_log, m_gdn_dt_bias, m_gdn_o_norm, m_gdn_w_o, m_sc_w_in, m_sc_conv_w, m_sc_w_o, m_norm_mix, m_norm_mem, m_norm_mlp, m_xa_w_q, m_xa_w_kv, m_xa_w_o, m_mlp_w1, m_mlp_w2, m_mem_norm, m_final_norm, v_mla_w_in, v_mla_q_norm, v_mla_kv_norm, v_mla_w_uq, v_mla_w_ukv, v_mla_w_o, v_gdn_w_in, v_gdn_conv_w, v_gdn_a_log, v_gdn_dt_bias, v_gdn_o_norm, v_gdn_w_o, v_sc_w_in, v_sc_conv_w, v_sc_w_o, v_norm_mix, v_norm_mem, v_norm_mlp, v_xa_w_q, v_xa_w_kv, v_xa_w_o, v_mlp_w1, v_mlp_w2, v_mem_norm, v_final_norm):
    given = dict(locals())
    p = {n: given[n] for n in _WEIGHTS}
    mom = {n: given["m_" + n] for n in _WEIGHTS}
    var = {n: given["v_" + n] for n in _WEIGHTS}
    split = sorted({n for members in _SLABS.values() for n, _, _, _ in members})
    lay = Layout({n: p[n].shape for n in split})
    flat2d = lambda a: a.reshape(-1, a.shape[-1])

    me = (2 * lax.axis_index("x") + lax.axis_index("y")).astype(jnp.int32)
    core = lax.axis_index("c").astype(jnp.int32)
    me1, c1, mc = me.reshape(1), core.reshape(1), jnp.stack([me, core])

    wslabs = lay.new_slabs(BF16)

    def cast_group(slabs, chip):
        for slab in slabs:
            for name, off, l0, l1, rpl in lay.members[slab]:
                cast_into(flat2d(p[name]), l0 * rpl, (l1 - l0) * rpl, wslabs[slab], off, chip, f"cast_{slab}_{name}")

    first = _GROUPS[0][0]
    cast_group(first, me1)
    small_names = [n for n, _ in _SMALL]
    words = lax.bitcast_convert_type(jnp.concatenate([p[n].reshape(-1) for n in small_names]), BF16).reshape(-1)
    words = jnp.pad(words, (0, SMALL_ROWS * SMALL_COLS - words.shape[0])).reshape(1, SMALL_ROWS, SMALL_COLS)
    small_slab = lax.dynamic_update_slice(jnp.zeros((N_CHIPS, SMALL_ROWS, SMALL_COLS), BF16), words, (me, 0, 0))

    send0, recv0, thru0, token = gather_start([wslabs[s].arr for s in first] + [small_slab], me1, "weight_gather_start_first")
    in_flight = {}
    for slabs, point in _GROUPS[1:]:
        cast_group(slabs, me1 + token[0, 0].astype(jnp.int32))
        send, recv, thru, token = gather_start([wslabs[s].arr for s in slabs], token, f"weight_gather_start_{slabs[0]}")
        in_flight[point] = (send, recv, thru, slabs)
    started_token = token
    landed = gather_wait(send0, recv0, thru0, started_token, "weight_gather_wait_first")
    gathered = gather_forward(landed, "weight_gather_forward_first")
    for s, arr in zip(first, gathered):
        wslabs[s].arr = arr

    def before(i, stage, xs):
        if (i, stage) in in_flight:
            send, recv, thru, slabs = in_flight[(i, stage)]
            landed = gather_wait(send, recv, thru, xs, f"weight_gather_wait_{slabs[0]}")
            for s, arr in zip(slabs, gather_forward(landed, f"weight_gather_forward_{slabs[0]}")):
                wslabs[s].arr = arr
        return xs

    small = {n: p[n] for n in _REPL}
    got, off = gathered[-1].reshape(N_CHIPS, -1), 0
    for n, ax in _SMALL:
        vals = lax.bitcast_convert_type(got[:, off:off + 2 * p[n].size].reshape(N_CHIPS, p[n].size, 2), F32)
        vals = vals.reshape((N_CHIPS,) + p[n].shape)
        small[n] = jnp.concatenate([vals[s] for s in range(N_CHIPS)], axis=ax)
        off += 2 * p[n].size

    gslabs = lay.new_slabs(BF16)
    complete_at = {point: slabs for slabs, point in _GROUPS[1:]}
    exchanging = []

    def after_bwd(i, stage, dx):
        if (i, stage) not in complete_at:
            return dx
        slabs = complete_at[(i, stage)]
        g = [gslabs[s].arr for s in slabs]
        swapped = pair_swap_halves(g, f"grad_pair_swap_{slabs[0]}")
        part = [pair_add(a, b, c1, f"pair_add_{s}") for a, b, s in zip(g, swapped, slabs)]
        send, recv, thru, token = exchange_start(part, c1, f"grad_exchange_start_{slabs[0]}")
        exchanging.append((slabs, send, recv, thru))
        return dx + token[0, 0]

    se, dx, sgrads = local_step(x[0], mem[0], positions.reshape(-1, 1), loss_target[0], lay, wslabs, gslabs, small,
                                before, after_bwd)
    loss = lax.psum(0.5 * jnp.sum(se) / x.shape[-1], ("x", "y", "c"))
    names, parts, received = [], [], []
    for slabs, send, recv, thru in exchanging:
        part, got = exchange_wait(send, recv, thru, dx, f"grad_exchange_wait_{slabs[0]}")
        names, parts, received = names + slabs, parts + list(part), received + list(got)

    axes = dict(_SMALL)
    small_order = small_names + _REPL
    slots = []
    for s in range(N_CHIPS):
        vals = {n: (lax.slice_in_dim(g, s * p[n].shape[axes[n]], (s + 1) * p[n].shape[axes[n]], axis=axes[n]) if n in axes else g)
                for n, g in sgrads.items()}
        slots.append(_small_pack(vals, small_order))
    g_last = [gslabs[s].arr for s in first] + [jnp.stack(slots).astype(BF16)]
    names_last = first + ["small"]
    swapped_last = pair_swap_halves(g_last, "grad_pair_swap_last")
    part_last = [pair_add(g, b, c1, f"pair_add_{s}") for g, b, s in zip(g_last, swapped_last, names_last)]
    send, recv, thru, token = exchange_start(part_last, c1, "grad_exchange_start_last")
    mc_after = mc + token[0, 0].astype(jnp.int32)
    halves = [chip_sum(q, r, mc_after, f"chip_sum_{s}") for q, r, s in zip(parts, received, names)]
    part_last, got_last = exchange_wait(send, recv, thru, list(halves), "grad_exchange_wait_last")
    halves += [chip_sum(q, r, mc, f"chip_sum_{s}") for q, r, s in zip(part_last, got_last, names_last)]
    reduced = dict(zip(names + names_last, pair_join_halves(halves)))

    res = {}
    for slab in _SLABS:
        for name, off, l0, l1, rpl in lay.members[slab]:
            res[name] = adamw(reduced[slab], off, flat2d(p[name]), flat2d(mom[name]), flat2d(var[name]), l0 * rpl, (l1 - l0) * rpl,
                              res.get(name), f"adamw_{slab}_{name}")
    for name in split:
        res[name] = [o.reshape(p[name].shape) for o in res[name]]
    sp = {k: _small_pack(d, small_order) for k, d in (("w", p), ("m", mom), ("v", var))}
    outs = adamw(reduced["small"], 0, sp["w"], sp["m"], sp["v"], 0, SMALL_ROWS, None, "adamw_small")
    unpacked = [_small_unpack(o, p, small_order) for o in outs]
    for n in small_order:
        res[n] = [u[n] for u in unpacked]
    return (loss, dx[None], *[res[n][k] for k in range(4) for n in _WEIGHTS])
```

```python
import jax
import jax.numpy as jnp
from jax import lax
from jax.experimental import pallas as pl
from jax.experimental.pallas import tpu as pltpu

F32 = jnp.float32
BF16 = jnp.bfloat16
MESH = pl.DeviceIdType.MESH

EPS = 1e-6
ROPE_THETA = 10000.0
N_CHIPS = 4
LANES = 128
VMEM_LIMIT = 56 * 1024 * 1024
NEG = -1e30

MLA_H, MLA_NOPE, MLA_ROPE, MLA_V = 8, 128, 64, 128
MLA_QR, MLA_KVR = 384, 256
MLA_ZPAD = 768
GDN_H, GDN_D, GDN_C = 8, 128, 64
XA_H, XA_D = 4, 256

ADAM_LR, ADAM_B1, ADAM_B2, ADAM_EPS, ADAM_WD, ADAM_STEP = 0.001, 0.9, 0.999, 1e-08, 0.01, 10

SMALL_ROWS, SMALL_COLS = 32, 1024


def _cparams(sem=None):
    return pltpu.CompilerParams(dimension_semantics=sem, vmem_limit_bytes=VMEM_LIMIT)


def _pick(dim, pref):
    t = (min(pref, dim) // LANES) * LANES
    while t >= LANES:
        if dim % t == 0:
            return t
        t -= LANES
    return dim


def _pick_rows(rows, pref, *offsets):
    t = (min(pref, rows) // 16) * 16
    while t > 16 and (rows % t or any(o % t for o in offsets)):
        t -= 16
    return t


class Slab:
    def __init__(self, rows, width, dtype, arr=None):
        self.shape, self.dtype, self.arr = (N_CHIPS, rows, width), dtype, arr


class Loc:
    def __init__(self, slab, row0, K, N, axis):
        self.slab, self.row0, self.K, self.N, self.axis = slab, row0, K, N, axis
        self.Ks = K // N_CHIPS if axis == 0 else K
        self.Ns = N // N_CHIPS if axis == 1 else N

    def tile_spec(self, tr, tc, rc):
        assert self.row0 % tr == 0 and self.Ks % tr == 0 and self.Ns % tc == 0, (self.row0, self.Ks, self.Ns, tr, tc)
        r0, rb, cb = self.row0 // tr, self.Ks // tr, self.Ns // tc
        if self.axis == 0:
            return pl.BlockSpec((None, tr, tc), lambda i, j: (rc(i, j)[0] // rb, r0 + rc(i, j)[0] % rb, rc(i, j)[1]))
        return pl.BlockSpec((None, tr, tc), lambda i, j: (rc(i, j)[1] // cb, r0 + rc(i, j)[0], rc(i, j)[1] % cb))

    def slot_spec(self, slot, tr, tc, rc):
        assert self.row0 % tr == 0, (self.row0, tr)
        r0 = self.row0 // tr
        return pl.BlockSpec((None, tr, tc), lambda i, j: (slot, r0 + rc(i, j)[0], rc(i, j)[1]))


_DIMS = {"nn": ((1,), (0,)), "nt": ((1,), (1,)), "tn": ((0,), (0,))}
_ANY = pl.BlockSpec(memory_space=pl.ANY)


def mm(a, b, mode, name, outs=(F32,), epi=None, extras=(), tm=1024, tn=1024, out_loc=None, vecs=(), row_outs=0, per_row=()):
    full_rows = bool(vecs) or row_outs > 0 or bool(per_row)
    b_loc = b if isinstance(b, Loc) else None
    if mode == "nn":
        M, K = a.shape
        K2, N = (b_loc.K, b_loc.N) if b_loc else b.shape
    elif mode == "nt":
        M, K = a.shape
        N, K2 = (b_loc.K, b_loc.N) if b_loc else b.shape
    else:
        K, M = a.shape
        K2, N = b.shape
    assert K == K2, (name, a.shape, K2, N)
    tm = _pick(out_loc.Ks if (out_loc and out_loc.axis == 0) else M, tm)
    n_split = full_rows and b_loc is not None and mode == "nt" and b_loc.axis == 0
    if out_loc is not None and out_loc.axis == 1:
        tn = _pick(out_loc.Ns, tn)
    elif n_split:
        tn = N
    elif b_loc is not None and ((mode == "nn" and b_loc.axis == 1) or (mode == "nt" and b_loc.axis == 0)):
        tn = _pick(b_loc.Ns if mode == "nn" else b_loc.Ks, tn)
    elif b_loc is not None:
        tn = N if full_rows else _pick(N, min(tn, 512))
    else:
        tn = N if full_rows else _pick(N, tn)
    assert tn == N or not full_rows, name

    parts = 1
    if mode == "tn":
        a_spec = pl.BlockSpec((K, tm), lambda i, j: (0, i))
        b_specs, b_args = [pl.BlockSpec((K, tn), lambda i, j: (0, j))], [b]
    else:
        a_spec = pl.BlockSpec((tm, K), lambda i, j: (i, 0))
        if b_loc is None:
            b_specs = [pl.BlockSpec((K, tn), lambda i, j: (0, j)) if mode == "nn" else pl.BlockSpec((tn, K), lambda i, j: (j, 0))]
            b_args = [b]
        elif mode == "nn" and b_loc.axis == 1:
            b_specs, b_args = [b_loc.tile_spec(K, tn, lambda i, j: (0, j))], [b_loc.slab.arr]
        elif n_split:
            b_specs = [b_loc.slot_spec(s, b_loc.Ks, K, lambda i, j: (0, 0)) for s in range(N_CHIPS)]
            b_args = [b_loc.slab.arr] * N_CHIPS
        elif mode == "nt" and b_loc.axis == 0:
            b_specs, b_args = [b_loc.tile_spec(tn, K, lambda i, j: (j, 0))], [b_loc.slab.arr]
        elif mode == "nn":
            parts = N_CHIPS
            b_specs = [b_loc.slot_spec(s, b_loc.Ks, tn, lambda i, j: (0, j)) for s in range(parts)]
            b_args = [b_loc.slab.arr] * parts
        else:
            parts = N_CHIPS
            b_specs = [b_loc.slot_spec(s, tn, b_loc.Ns, lambda i, j: (j, 0)) for s in range(parts)]
            b_args = [b_loc.slab.arr] * parts
    kp = K // parts
    n_b = N_CHIPS if n_split else parts
    n_ex, n_out = len(extras) + len(per_row) + len(vecs), len(outs)
    dims = (_DIMS[mode], ((), ()))

    def body(*refs):
        a_ref = refs[0]
        b_refs = refs[1:1 + n_b]
        ex_refs = refs[1 + n_b:1 + n_b + n_ex]
        o_refs = refs[len(refs) - n_out - row_outs:len(refs) - row_outs]
        r_refs = refs[len(refs) - row_outs:]
        acc = None
        if n_split:
            av = a_ref[...].astype(BF16)
            acc = jnp.concatenate([lax.dot_general(av, b_ref[...].astype(BF16), dims, preferred_element_type=F32)
                                   for b_ref in b_refs], axis=1)
        for s in range(0 if n_split else parts):
            av = a_ref[...] if parts == 1 else a_ref[:, s * kp:(s + 1) * kp]
            d = lax.dot_general(av.astype(BF16), b_refs[s][...].astype(BF16), dims, preferred_element_type=F32)
            acc = d if acc is None else acc + d
        res = epi(acc, *[e[...] for e in ex_refs]) if epi is not None else (acc,)
        for o_ref, v in zip(o_refs, res[:n_out]):
            o_ref[...] = v.astype(o_ref.dtype)
        for r_ref, v in zip(r_refs, res[n_out:]):
            @pl.when(pl.program_id(0) == 0)
            def _():
                r_ref[...] = jnp.zeros_like(r_ref)

            r_ref[...] += v

    mn_spec = pl.BlockSpec((tm, tn), lambda i, j: (i, j))
    row_spec = pl.BlockSpec((1, tn), lambda i, j: (0, j))
    in_specs = ([a_spec] + b_specs + [mn_spec] * len(extras) + [pl.BlockSpec((tm, r.shape[1]), lambda i, j: (i, 0)) for r in per_row]
                + [row_spec] * len(vecs))
    args = [a] + b_args + list(extras) + list(per_row) + [v.reshape(1, N) for v in vecs]
    aliases = {}
    if out_loc is None:
        out_specs = [mn_spec] * n_out + [row_spec] * row_outs
        out_shape = [jax.ShapeDtypeStruct((M, N), d) for d in outs] + [jax.ShapeDtypeStruct((1, N), F32)] * row_outs
    else:
        assert n_out == 1 and mode == "tn"
        out_specs = [out_loc.tile_spec(tm, tn, lambda i, j: (i, j))]
        out_shape = [jax.ShapeDtypeStruct(out_loc.slab.shape, out_loc.slab.dtype)]
        if out_loc.slab.arr is not None:
            in_specs.append(_ANY)
            args.append(out_loc.slab.arr)
            aliases = {len(args) - 1: 0}

    res = pl.pallas_call(
        body, name=name, grid=(M // tm, N // tn), in_specs=in_specs, out_specs=out_specs, out_shape=out_shape,
        input_output_aliases=aliases, compiler_params=_cparams(("arbitrary" if row_outs else "parallel", "parallel")),
    )(*args)
    if out_loc is not None:
        out_loc.slab.arr = res[0]
        return None
    return res[0] if len(res) == 1 else tuple(res)


def _epi_add(acc, r):
    return (acc + r,)


def _epi_add_norm(acc, r, g):
    x = acc + r
    return x, _rms(x, g)


def _epi_norm_bwd(acc, x, dx_in, g):
    r = lax.rsqrt(jnp.mean(x * x, axis=-1, keepdims=True) + EPS)
    xh = x * r
    dxh = acc * g
    dx = dx_in + r * (dxh - xh * jnp.mean(dxh * xh, axis=-1, keepdims=True))
    return dx, jnp.sum(acc * xh, axis=0, keepdims=True)


def residual_norm(a, w, xs, g, name, tm=1024):
    if g is None:
        return mm(a, w, "nn", name, epi=_epi_add, extras=(xs,), tm=tm), None
    return mm(a, w, "nn", name, outs=(F32, BF16), epi=_epi_add_norm, extras=(xs,), vecs=(g,), tm=tm)


def _epi_relu2(acc):
    r = jnp.maximum(acc, 0.0)
    return acc, r * r


def _epi_relu2_bwd(acc, h1):
    return (acc * (2.0 * jnp.maximum(h1.astype(F32), 0.0)),)


def _rms(x, g):
    return x * lax.rsqrt(jnp.mean(x * x, axis=-1, keepdims=True) + EPS) * g


def _row_spec(ts, cols):
    return pl.BlockSpec((ts, cols), lambda i: (i, 0))


def _par_spec(cols):
    return pl.BlockSpec((1, cols), lambda i: (0, 0))


def rmsnorm_fwd(x, g, name, ts=256):
    T, D = x.shape
    ts = min(ts, T)

    def body(x_ref, g_ref, o_ref):
        o_ref[...] = _rms(x_ref[...], g_ref[...]).astype(o_ref.dtype)

    return pl.pallas_call(
        body, name=name, grid=(T // ts,),
        in_specs=[_row_spec(ts, D), _par_spec(D)], out_specs=_row_spec(ts, D),
        out_shape=jax.ShapeDtypeStruct((T, D), BF16), compiler_params=_cparams(("parallel",)),
    )(x, g.reshape(1, D))


def rmsnorm_bwd(x, g, dy, dx_in, name, ts=256):
    T, D = x.shape
    ts = min(ts, T)

    def body(x_ref, g_ref, dy_ref, dxi_ref, dx_ref, dg_ref):
        xv = x_ref[...]
        r = lax.rsqrt(jnp.mean(xv * xv, axis=-1, keepdims=True) + EPS)
        xh = xv * r
        dyv = dy_ref[...].astype(F32)
        dxh = dyv * g_ref[...]
        dx_ref[...] = dxi_ref[...] + r * (dxh - xh * jnp.mean(dxh * xh, axis=-1, keepdims=True))
        dg = jnp.sum(dyv * xh, axis=0, keepdims=True)

        @pl.when(pl.program_id(0) == 0)
        def _():
            dg_ref[...] = jnp.zeros_like(dg_ref)

        dg_ref[...] += dg

    dx, dg = pl.pallas_call(
        body, name=name, grid=(T // ts,),
        in_specs=[_row_spec(ts, D), _par_spec(D), _row_spec(ts, D), _row_spec(ts, D)],
        out_specs=[_row_spec(ts, D), _par_spec(D)],
        out_shape=[jax.ShapeDtypeStruct((T, D), F32), jax.ShapeDtypeStruct((1, D), F32)],
        compiler_params=_cparams(("arbitrary",)),
    )(x, g.reshape(1, D), dy, dx_in)
    return dx, dg.reshape(D)


def rope_tables(pos, name="rope_tables"):
    T = pos.shape[0]
    half = MLA_ROPE // 2
    inv = ROPE_THETA ** (-jnp.arange(0, MLA_ROPE, 2, dtype=F32) / MLA_ROPE)
    inv_row = jnp.concatenate([inv, inv, jnp.zeros((LANES - MLA_ROPE,), F32)]).reshape(1, LANES)

    def body(p_ref, f_ref, c_ref, a_ref, b_ref):
        ang = p_ref[...].astype(F32) * f_ref[...]
        lane = lax.broadcasted_iota(jnp.int32, ang.shape, 1)
        c, s = jnp.cos(ang), jnp.sin(ang)
        c_ref[...] = jnp.where(lane < MLA_ROPE, c, 0.0)
        a_ref[...] = jnp.where(lane < half, -s, 0.0)
        b_ref[...] = jnp.where((lane >= half) & (lane < MLA_ROPE), s, 0.0)

    sh = jax.ShapeDtypeStruct((T, LANES), F32)
    return pl.pallas_call(body, name=name, out_shape=[sh, sh, sh], compiler_params=_cparams())(pos, inv_row)


def _roll_l(x):
    return pltpu.roll(x, LANES - MLA_ROPE // 2, 1)


def _roll_r(x):
    return pltpu.roll(x, MLA_ROPE // 2, 1)


def _rope(r, c, sa, sb):
    return r * c + _roll_l(r) * sa + _roll_r(r) * sb


def _rope_t(d, c, sa, sb):
    return d * c + _roll_r(d * sa) + _roll_l(d * sb)


def _epi_rope_q(acc, c, sa, sb):
    hw = 2 * LANES
    parts = []
    for h in range(acc.shape[1] // hw):
        parts += [acc[:, h * hw:h * hw + LANES], _rope(acc[:, h * hw + LANES:(h + 1) * hw], c, sa, sb)]
    return (jnp.concatenate(parts, axis=1),)


def mla_mid_fwd(z, qn, kvn, tabs, name, ts=256):
    T = z.shape[0]
    ts = min(ts, T)
    a0, a1 = MLA_QR, MLA_QR + MLA_KVR

    def body(z_ref, qn_ref, kvn_ref, c_ref, sa_ref, sb_ref, cq_ref, ckv_ref, kr_ref):
        cq_ref[...] = _rms(z_ref[:, 0:a0], qn_ref[...]).astype(BF16)
        ckv_ref[...] = _rms(z_ref[:, a0:a1], kvn_ref[...]).astype(BF16)
        kr_ref[...] = _rope(z_ref[:, a1:MLA_ZPAD], c_ref[...], sa_ref[...], sb_ref[...]).astype(BF16)

    return pl.pallas_call(
        body, name=name, grid=(T // ts,),
        in_specs=[_row_spec(ts, MLA_ZPAD), _par_spec(MLA_QR), _par_spec(MLA_KVR)] + [_row_spec(ts, LANES)] * 3,
        out_specs=[_row_spec(ts, MLA_QR), _row_spec(ts, MLA_KVR), _row_spec(ts, LANES)],
        out_shape=[jax.ShapeDtypeStruct((T, MLA_QR), BF16), jax.ShapeDtypeStruct((T, MLA_KVR), BF16),
                   jax.ShapeDtypeStruct((T, LANES), BF16)],
        compiler_params=_cparams(("parallel",)),
    )(z, qn.reshape(1, -1), kvn.reshape(1, -1), *tabs)


def mla_mid_bwd(z, qn, kvn, tabs, dcq, dckv, dkr, name, ts=256):
    T = z.shape[0]
    ts = min(ts, T)
    a0, a1 = MLA_QR, MLA_QR + MLA_KVR

    def body(z_ref, qn_ref, kvn_ref, c_ref, sa_ref, sb_ref, dcq_ref, dckv_ref, dkr_ref, dz_ref, dqn_ref, dkvn_ref):
        _, vq = jax.vjp(_rms, z_ref[:, 0:a0], qn_ref[...])
        dzq, dqn = vq(dcq_ref[...].astype(F32))
        _, vk = jax.vjp(_rms, z_ref[:, a0:a1], kvn_ref[...])
        dzk, dkvn = vk(dckv_ref[...].astype(F32))
        dz_ref[:, 0:a0] = dzq.astype(dz_ref.dtype)
        dz_ref[:, a0:a1] = dzk.astype(dz_ref.dtype)
        dz_ref[:, a1:MLA_ZPAD] = _rope_t(dkr_ref[...].astype(F32), c_ref[...], sa_ref[...], sb_ref[...]).astype(dz_ref.dtype)

        @pl.when(pl.program_id(0) == 0)
        def _():
            dqn_ref[...] = jnp.zeros_like(dqn_ref)
            dkvn_ref[...] = jnp.zeros_like(dkvn_ref)

        dqn_ref[...] += dqn
        dkvn_ref[...] += dkvn

    dz, dqn, dkvn = pl.pallas_call(
        body, name=name, grid=(T // ts,),
        in_specs=[_row_spec(ts, MLA_ZPAD), _par_spec(MLA_QR), _par_spec(MLA_KVR)] + [_row_spec(ts, LANES)] * 3
        + [_row_spec(ts, MLA_QR), _row_spec(ts, MLA_KVR), _row_spec(ts, LANES)],
        out_specs=[_row_spec(ts, MLA_ZPAD), _par_spec(MLA_QR), _par_spec(MLA_KVR)],
        out_shape=[jax.ShapeDtypeStruct((T, MLA_ZPAD), BF16), jax.ShapeDtypeStruct((1, MLA_QR), F32),
                   jax.ShapeDtypeStruct((1, MLA_KVR), F32)],
        compiler_params=_cparams(("arbitrary",)),
    )(z, qn.reshape(1, -1), kvn.reshape(1, -1), *tabs, dcq, dckv, dkr)
    return dz, dqn.reshape(-1), dkvn.reshape(-1)


def loss_head(x, g, target, name="loss_head", ts=256):
    T, D = x.shape
    ts = min(ts, T)

    def body(x_ref, g_ref, t_ref, se_ref, dx_ref, dg_ref):
        xv = x_ref[...]
        r = lax.rsqrt(jnp.mean(xv * xv, axis=-1, keepdims=True) + EPS)
        xh = xv * r
        err = xh * g_ref[...] - t_ref[...]
        dy = err * (1.0 / D)
        dxh = dy * g_ref[...]
        dx_ref[...] = r * (dxh - xh * jnp.mean(dxh * xh, axis=-1, keepdims=True))

        @pl.when(pl.program_id(0) == 0)
        def _():
            se_ref[...] = jnp.zeros_like(se_ref)
            dg_ref[...] = jnp.zeros_like(dg_ref)

        se_ref[...] += jnp.sum(err * err, axis=0, keepdims=True)
        dg_ref[...] += jnp.sum(dy * xh, axis=0, keepdims=True)

    se, dx, dg = pl.pallas_call(
        body, name=name, grid=(T // ts,),
        in_specs=[_row_spec(ts, D), _par_spec(D), _row_spec(ts, D)],
        out_specs=[_par_spec(D), _row_spec(ts, D), _par_spec(D)],
        out_shape=[jax.ShapeDtypeStruct((1, D), F32), jax.ShapeDtypeStruct((T, D), F32), jax.ShapeDtypeStruct((1, D), F32)],
        compiler_params=_cparams(("arbitrary",)),
    )(x, g.reshape(1, D), target)
    return se, dx, dg.reshape(D)


def _dot_nt(a, b):
    return lax.dot_general(a, b, (((1,), (1,)), ((), ())), preferred_element_type=F32)


def _dot_nn(a, b):
    return lax.dot_general(a, b, (((1,), (0,)), ((), ())), preferred_element_type=F32)


class _Attn:
    def __init__(self, H, dq, dk1, dv, causal, scale, hp, hp_kv, blk=256):
        self.H, self.dq, self.dk1, self.dv, self.causal, self.scale, self.blk = H, dq, dk1, dv, causal, scale, blk
        self.hp, self.hp_kv = hp, hp_kv


def _cols(ref, rows, hh, width):
    return ref[rows, hh * width:(hh + 1) * width]


def _keys(cfg, k1_ref, k2_ref, rows, hh):
    ks = _cols(k1_ref, rows, hh, cfg.dk1)
    if k2_ref is not None:
        ks = jnp.concatenate([ks, k2_ref[rows, :]], axis=1)
    return ks


def _attn_specs(cfg, hp, t, Tk, has_k2, by_q):
    g = cfg.H // hp
    if by_q:
        specs = [pl.BlockSpec((t, hp * cfg.dq), lambda h, i: (i, h)),
                 pl.BlockSpec((Tk, hp * cfg.dk1), lambda h, i: (0, h)),
                 pl.BlockSpec((Tk, hp * cfg.dv), lambda h, i: (0, g + h))]
        if has_k2:
            specs.append(pl.BlockSpec((Tk, LANES), lambda h, i: (0, 0)))
    else:
        specs = [None,
                 pl.BlockSpec((t, hp * cfg.dk1), lambda j, h: (j, h)),
                 pl.BlockSpec((t, hp * cfg.dv), lambda j, h: (j, g + h))]
        if has_k2:
            specs.append(pl.BlockSpec((t, LANES), lambda j, h: (j, 0)))
    return specs


def _mask(s, diagonal):
    if not diagonal:
        return s
    return jnp.where(lax.broadcasted_iota(jnp.int32, s.shape, 0) >= lax.broadcasted_iota(jnp.int32, s.shape, 1), s, NEG)


def flash_fwd(cfg, q, k1, v, k2, name):
    Tq, Tk = q.shape[0], k1.shape[0]
    t = min(cfg.blk, Tq, Tk)
    nkb = Tk // t
    has_k2 = k2 is not None
    hp = cfg.hp

    def body(*refs):
        q_ref, k1_ref, v_ref = refs[:3]
        k2_ref = refs[3] if has_k2 else None
        o_ref, lse_ref = refs[-2], refs[-1]
        i = pl.program_id(1)
        qs = [_cols(q_ref, slice(None), hh, cfg.dq) for hh in range(hp)]

        def step(j, carry, diagonal=False):
            rows = pl.ds(pl.multiple_of(j * t, t), t)
            out = []
            for hh in range(hp):
                m, l, acc = carry[hh]
                s = _mask(_dot_nt(qs[hh], _keys(cfg, k1_ref, k2_ref, rows, hh)) * cfg.scale, diagonal)
                m2 = jnp.maximum(m, jnp.max(s, axis=-1, keepdims=True))
                p = jnp.exp(s - m2)
                alpha = jnp.exp(m - m2)
                l2 = alpha * l + jnp.sum(p, axis=-1, keepdims=True)
                acc2 = alpha * acc + _dot_nn(p.astype(BF16), _cols(v_ref, rows, hh, cfg.dv))
                out.append((m2, l2, acc2))
            return tuple(out)

        init = tuple((jnp.full((t, 1), NEG, F32), jnp.zeros((t, 1), F32), jnp.zeros((t, cfg.dv), F32)) for _ in range(hp))
        res = lax.fori_loop(0, i if cfg.causal else nkb, step, init)
        if cfg.causal:
            res = step(i, res, True)
        for hh in range(hp):
            m, l, acc = res[hh]
            o_ref[:, hh * cfg.dv:(hh + 1) * cfg.dv] = (acc / l).astype(o_ref.dtype)
            lse_ref[hh] = m + jnp.log(l)

    args = [q, k1, v] + ([k2] if has_k2 else [])
    return pl.pallas_call(
        body, name=name, grid=(cfg.H // hp, Tq // t), in_specs=_attn_specs(cfg, hp, t, Tk, has_k2, True),
        out_specs=[pl.BlockSpec((t, hp * cfg.dv), lambda h, i: (i, h)), pl.BlockSpec((hp, t, 1), lambda h, i: (h, i, 0))],
        out_shape=[jax.ShapeDtypeStruct((Tq, cfg.H * cfg.dv), BF16), jax.ShapeDtypeStruct((cfg.H, Tq, 1), F32)],
        compiler_params=_cparams(("parallel", "parallel")),
    )(*args)


def flash_dq(cfg, q, k1, v, k2, o, do, lse, out_dtype, name, rope_tabs=None):
    Tq, Tk = q.shape[0], k1.shape[0]
    t = min(cfg.blk, Tq, Tk)
    nkb = Tk // t
    has_k2 = k2 is not None
    hp = cfg.hp
    n_tab = 0 if rope_tabs is None else len(rope_tabs)

    def body(*refs):
        q_ref, k1_ref, v_ref = refs[:3]
        k2_ref = refs[3] if has_k2 else None
        tab_refs = refs[len(refs) - 5 - n_tab:len(refs) - 5]
        o_ref, do_ref, lse_ref, dq_ref, dl_ref = refs[-5:]
        i = pl.program_id(1)
        qs = [_cols(q_ref, slice(None), hh, cfg.dq) for hh in range(hp)]
        dos = [_cols(do_ref, slice(None), hh, cfg.dv) for hh in range(hp)]
        lses = [lse_ref[hh] for hh in range(hp)]
        deltas = []
        for hh in range(hp):
            d = jnp.sum(dos[hh].astype(F32) * _cols(o_ref, slice(None), hh, cfg.dv).astype(F32), axis=-1, keepdims=True)
            dl_ref[hh] = d
            deltas.append(d)

        def step(j, dqs, diagonal=False):
            rows = pl.ds(pl.multiple_of(j * t, t), t)
            out = []
            for hh in range(hp):
                ks = _keys(cfg, k1_ref, k2_ref, rows, hh)
                s = _mask(_dot_nt(qs[hh], ks) * cfg.scale, diagonal)
                p = jnp.exp(s - lses[hh])
                dp = _dot_nt(dos[hh], _cols(v_ref, rows, hh, cfg.dv))
                ds = p * (dp - deltas[hh]) * cfg.scale
                out.append(dqs[hh] + _dot_nn(ds.astype(BF16), ks))
            return tuple(out)

        dqs = lax.fori_loop(0, i if cfg.causal else nkb, step, tuple(jnp.zeros((t, cfg.dq), F32) for _ in range(hp)))
        if cfg.causal:
            dqs = step(i, dqs, True)
        tabs = [r[...] for r in tab_refs]
        for hh in range(hp):
            dq = dqs[hh]
            if tabs:
                dq = jnp.concatenate([dq[:, :LANES], _rope_t(dq[:, LANES:], *tabs)], axis=1)
            dq_ref[:, hh * cfg.dq:(hh + 1) * cfg.dq] = dq.astype(dq_ref.dtype)

    ov = pl.BlockSpec((t, hp * cfg.dv), lambda h, i: (i, h))
    row1 = pl.BlockSpec((hp, t, 1), lambda h, i: (h, i, 0))
    tab_specs = [pl.BlockSpec((t, LANES), lambda h, i: (i, 0))] * n_tab
    args = [q, k1, v] + ([k2] if has_k2 else []) + list(rope_tabs or ()) + [o, do, lse]
    return pl.pallas_call(
        body, name=name, grid=(cfg.H // hp, Tq // t),
        in_specs=_attn_specs(cfg, hp, t, Tk, has_k2, True) + tab_specs + [ov, ov, row1],
        out_specs=[pl.BlockSpec((t, hp * cfg.dq), lambda h, i: (i, h)), row1],
        out_shape=[jax.ShapeDtypeStruct((Tq, cfg.H * cfg.dq), out_dtype), jax.ShapeDtypeStruct((cfg.H, Tq, 1), F32)],
        compiler_params=_cparams(("parallel", "parallel")),
    )(*args)


def flash_dkv(cfg, q, k1, v, k2, do, lse, delta, out_dtype, name):
    Tq, Tk = q.shape[0], k1.shape[0]
    t = min(cfg.blk, Tq, Tk)
    nqb = Tq // t
    has_k2 = k2 is not None
    hp = cfg.hp_kv
    assert hp == cfg.H
    v0 = cfg.H * cfg.dk1

    def body(*refs):
        q_ref, k1_ref, v_ref = refs[:3]
        k2_ref = refs[3] if has_k2 else None
        n_in = 4 if has_k2 else 3
        do_ref, lse_ref, dl_ref = refs[n_in:n_in + 3]
        dkv_ref = refs[n_in + 3]
        j, h = pl.program_id(0), pl.program_id(1)
        kss = [_keys(cfg, k1_ref, k2_ref, slice(None), hh) for hh in range(hp)]
        vss = [_cols(v_ref, slice(None), hh, cfg.dv) for hh in range(hp)]

        def step(i, carry, diagonal=False):
            rows = pl.ds(pl.multiple_of(i * t, t), t)
            out = []
            for hh in range(hp):
                dk, dv = carry[hh]
                qi, doi = _cols(q_ref, rows, hh, cfg.dq), _cols(do_ref, rows, hh, cfg.dv)
                s = _dot_nt(kss[hh], qi) * cfg.scale
                if diagonal:
                    s = jnp.where(lax.broadcasted_iota(jnp.int32, s.shape, 0) <= lax.broadcasted_iota(jnp.int32, s.shape, 1), s, NEG)
                p = jnp.exp(s - lse_ref[hh, :, rows])
                dv = dv + _dot_nn(p.astype(BF16), doi)
                ds = p * (_dot_nt(vss[hh], doi) - dl_ref[hh, :, rows]) * cfg.scale
                dk = dk + _dot_nn(ds.astype(BF16), qi)
                out.append((dk, dv))
            return tuple(out)

        init = tuple((jnp.zeros((t, cfg.dq), F32), jnp.zeros((t, cfg.dv), F32)) for _ in range(hp))
        if cfg.causal:
            res = lax.fori_loop(j + 1, nqb, step, step(j, init, True))
        else:
            res = lax.fori_loop(0, nqb, step, init)
        for hh in range(hp):
            dk, dv = res[hh]
            dkv_ref[:, hh * cfg.dk1:(hh + 1) * cfg.dk1] = dk[:, 0:cfg.dk1].astype(dkv_ref.dtype)
            dkv_ref[:, v0 + hh * cfg.dv:v0 + (hh + 1) * cfg.dv] = dv.astype(dkv_ref.dtype)
        if has_k2:
            dk2_ref = refs[n_in + 4]

            @pl.when(h == 0)
            def _():
                dk2_ref[...] = jnp.zeros_like(dk2_ref)

            for hh in range(hp):
                dk2_ref[...] += res[hh][0][:, cfg.dk1:]

    specs = _attn_specs(cfg, hp, t, Tk, has_k2, False)
    specs[0] = pl.BlockSpec((Tq, hp * cfg.dq), lambda j, h: (0, h))
    rows_all = pl.BlockSpec((hp, 1, Tq), lambda j, h: (h, 0, 0))
    specs += [pl.BlockSpec((Tq, hp * cfg.dv), lambda j, h: (0, h)), rows_all, rows_all]
    args = [q, k1, v] + ([k2] if has_k2 else []) + [do, lse.reshape(cfg.H, 1, Tq), delta.reshape(cfg.H, 1, Tq)]
    out_specs = [pl.BlockSpec((t, v0 + cfg.H * cfg.dv), lambda j, h: (j, 0))]
    out_shape = [jax.ShapeDtypeStruct((Tk, v0 + cfg.H * cfg.dv), out_dtype)]
    if has_k2:
        out_specs.append(pl.BlockSpec((t, LANES), lambda j, h: (j, 0)))
        out_shape.append(jax.ShapeDtypeStruct((Tk, LANES), F32))
    return pl.pallas_call(
        body, name=name, grid=(Tk // t, cfg.H // hp), in_specs=specs, out_specs=out_specs, out_shape=out_shape,
        compiler_params=_cparams(("parallel", "arbitrary")),
    )(*args)


def _shift_down(x, s):
    if s == 0:
        return x
    t = lax.broadcasted_iota(jnp.int32, x.shape, 0)
    return jnp.where(t >= s, pltpu.roll(x, s, 0), 0.0)


def _shift_up(x, s):
    if s == 0:
        return x
    n = x.shape[0]
    t = lax.broadcasted_iota(jnp.int32, x.shape, 0)
    return jnp.where(t < n - s, pltpu.roll(x, n - s, 0), 0.0)


def _conv(x, w_ref, kw):
    y = x * w_ref[kw - 1:kw, :]
    for j in range(kw - 1):
        y = y + _shift_down(x, kw - 1 - j) * w_ref[j:j + 1, :]
    return y


def _conv_t(d, w_ref, kw):
    y = d * w_ref[kw - 1:kw, :]
    for j in range(kw - 1):
        y = y + _shift_up(d, kw - 1 - j) * w_ref[j:j + 1, :]
    return y


def _conv_dw(d, x, kw):
    rows = lax.broadcasted_iota(jnp.int32, (kw, d.shape[1]), 0)
    dw = jnp.zeros((kw, d.shape[1]), F32)
    for j in range(kw):
        r = jnp.sum(d * _shift_down(x, kw - 1 - j), axis=0, keepdims=True)
        dw = jnp.where(rows == j, r, dw)
    return dw


def _silu(x):
    return x * jax.nn.sigmoid(x)


def _silu_grad(x):
    s = jax.nn.sigmoid(x)
    return s * (1.0 + x * (1.0 - s))


def gdn_conv_fwd(z, w, name, tc=256):
    T, C = z.shape[0], w.shape[1]
    kw = w.shape[0]

    def body(x_ref, w_ref, o_ref):
        o_ref[...] = _silu(_conv(x_ref[...], w_ref, kw))

    return pl.pallas_call(
        body, name=name, grid=(C // tc,),
        in_specs=[pl.BlockSpec((T, tc), lambda j: (0, j)), pl.BlockSpec((kw, tc), lambda j: (0, j))],
        out_specs=pl.BlockSpec((T, tc), lambda j: (0, j)),
        out_shape=jax.ShapeDtypeStruct((T, C), F32), compiler_params=_cparams(("parallel",)),
    )(z, w)


def gdn_conv_bwd(z, w, dy, name, tc=256):
    T, C = z.shape[0], w.shape[1]
    kw = w.shape[0]

    def body(x_ref, w_ref, dy_ref, dx_ref, dw_ref):
        xv = x_ref[...]
        dc = dy_ref[...] * _silu_grad(_conv(xv, w_ref, kw))
        dx_ref[...] = _conv_t(dc, w_ref, kw).astype(dx_ref.dtype)
        dw_ref[...] = _conv_dw(dc, xv, kw)

    col = lambda j: (0, j)
    return pl.pallas_call(
        body, name=name, grid=(C // tc,),
        in_specs=[pl.BlockSpec((T, tc), col), pl.BlockSpec((kw, tc), col), pl.BlockSpec((T, tc), col)],
        out_specs=[pl.BlockSpec((T, tc), col), pl.BlockSpec((kw, tc), col)],
        out_shape=[jax.ShapeDtypeStruct((T, C), BF16), jax.ShapeDtypeStruct((kw, C), F32)],
        compiler_params=_cparams(("parallel",)),
    )(z, w, dy)


def sc_fwd(z, w, name, tc=256):
    T, C = z.shape[0], w.shape[1]
    kw, nb = w.shape[0], C // tc

    def body(b_ref, c_ref, u_ref, w_ref, o_ref):
        o_ref[...] = (b_ref[...] * _conv(c_ref[...] * u_ref[...], w_ref, kw)).astype(o_ref.dtype)

    return pl.pallas_call(
        body, name=name, grid=(nb,),
        in_specs=[pl.BlockSpec((T, tc), lambda j: (0, j)), pl.BlockSpec((T, tc), lambda j: (0, nb + j)),
                  pl.BlockSpec((T, tc), lambda j: (0, 2 * nb + j)), pl.BlockSpec((kw, tc), lambda j: (0, j))],
        out_specs=pl.BlockSpec((T, tc), lambda j: (0, j)),
        out_shape=jax.ShapeDtypeStruct((T, C), BF16), compiler_params=_cparams(("parallel",)),
    )(z, z, z, w)


def sc_bwd(z, w, dy, name, tc=256):
    T, C = z.shape[0], w.shape[1]
    kw, nb = w.shape[0], C // tc

    def body(b_ref, c_ref, u_ref, w_ref, dy_ref, db_ref, dc_ref, du_ref, dw_ref):
        cv, uv, dyv = c_ref[...], u_ref[...], dy_ref[...]
        cu = cv * uv
        db_ref[...] = (dyv * _conv(cu, w_ref, kw)).astype(db_ref.dtype)
        dcv = dyv * b_ref[...]
        dcu = _conv_t(dcv, w_ref, kw)
        dc_ref[...] = (dcu * uv).astype(dc_ref.dtype)
        du_ref[...] = (dcu * cv).astype(du_ref.dtype)
        dw_ref[...] = _conv_dw(dcv, cu, kw)

    col = lambda j: (0, j)
    act = jax.ShapeDtypeStruct((T, C), BF16)
    return pl.pallas_call(
        body, name=name, grid=(nb,),
        in_specs=[pl.BlockSpec((T, tc), col), pl.BlockSpec((T, tc), lambda j: (0, nb + j)),
                  pl.BlockSpec((T, tc), lambda j: (0, 2 * nb + j)), pl.BlockSpec((kw, tc), col), pl.BlockSpec((T, tc), col)],
        out_specs=[pl.BlockSpec((T, tc), col)] * 3 + [pl.BlockSpec((kw, tc), col)],
        out_shape=[act, act, act, jax.ShapeDtypeStruct((kw, C), F32)],
        compiler_params=_cparams(("parallel",)),
    )(z, z, z, w, dy)


def _hdot(a, b, dims):
    a_hi, b_hi = a.astype(BF16), b.astype(BF16)
    a_lo, b_lo = (a - a_hi.astype(F32)).astype(BF16), (b - b_hi.astype(F32)).astype(BF16)
    dot = lambda x, y: lax.dot_general(x, y, (dims, ((), ())), preferred_element_type=F32)
    return dot(a_hi, b_hi) + (dot(a_hi, b_lo) + dot(a_lo, b_hi))


def _bdot(a, b, dims):
    return lax.dot_general(a.astype(BF16), b.astype(BF16), (dims, ((), ())), preferred_element_type=F32)


_NN, _NT, _TN = ((1,), (0,)), ((1,), (1,)), ((0,), (0,))


def _per_head_dots(dot2d):
    def stacked(a, b, dims):
        return jnp.stack([dot2d(a[h], b[h], dims) for h in range(a.shape[0])])

    @jax.custom_vjp
    def nn(a, b):
        return stacked(a, b, _NN)

    @jax.custom_vjp
    def nt(a, b):
        return stacked(a, b, _NT)

    @jax.custom_vjp
    def tn(a, b):
        return stacked(a, b, _TN)

    nn.defvjp(lambda a, b: (nn(a, b), (a, b)), lambda r, d: (stacked(d, r[1], _NT), stacked(r[0], d, _TN)))
    nt.defvjp(lambda a, b: (nt(a, b), (a, b)), lambda r, d: (stacked(d, r[1], _NN), stacked(d, r[0], _TN)))
    tn.defvjp(lambda a, b: (tn(a, b), (a, b)), lambda r, d: (stacked(r[1], d, _NT), stacked(r[0], d, _NN)))
    return nn, nt, tn


_hnn, _hnt, _htn = _per_head_dots(_hdot)
_bnn, _bnt, _btn = _per_head_dots(_bdot)


@jax.custom_vjp
def _unit_lower_inverse(m):
    c = m.shape[-1]
    eye = (lax.broadcasted_iota(jnp.int32, (c, c), 0) == lax.broadcasted_iota(jnp.int32, (c, c), 1)).astype(F32)
    t = eye - m
    p = _hnn(m, m)
    n = 2
    while n < c:
        t = t + _hnn(t, p)
        n *= 2
        if n < c:
            p = _hnn(p, p)
    return t


def _uli_fwd(m):
    t = _unit_lower_inverse(m)
    return t, t


def _uli_bwd(t, dt):
    return (-_htn(t, _hnt(dt, t)),)


_unit_lower_inverse.defvjp(_uli_fwd, _uli_bwd)


@jax.custom_vjp
def _known_inverse(m, t):
    return t


_known_inverse.defvjp(lambda m, t: (t, t), lambda t, dt: (_uli_bwd(t, dt)[0], jnp.zeros_like(t)))


def _gdn_chunk(q, k, v, gate, bl, al, a_log, dt_bias, o_norm, st, t_known=None):
    nh, c = q.shape[0], q.shape[1]
    ii = lax.broadcasted_iota(jnp.int32, (c, c), 0)
    jj = lax.broadcasted_iota(jnp.int32, (c, c), 1)
    tri, strict = ii >= jj, ii > jj
    q = q * lax.rsqrt(jnp.sum(q * q, -1, keepdims=True) + EPS) * (GDN_D ** -0.5)
    k = k * lax.rsqrt(jnp.sum(k * k, -1, keepdims=True) + EPS)
    beta = jax.nn.sigmoid(bl)
    g = -jnp.exp(a_log) * jax.nn.softplus(al + dt_bias)
    gc = _hnn(jnp.broadcast_to(tri.astype(F32), (nh, c, c)), g)
    gcol = _hnn(gc, jnp.full((nh, LANES, c), 1.0 / LANES, F32))
    grow = _hnt(jnp.full((nh, c, LANES), 1.0 / LANES, F32), gc)
    decay = jnp.where(tri, jnp.exp(jnp.where(tri, gcol - grow, 0.0)), 0.0)
    kb = k * beta
    m = jnp.where(strict, _bnt(kb, k) * decay, 0.0)
    t_inv = _unit_lower_inverse(m) if t_known is None else _known_inverse(m, t_known)
    eg = jnp.exp(gc)
    u = _bnn(t_inv, v * beta)
    w = _bnn(t_inv, kb * eg)
    attn = _bnt(q, k) * decay
    v_new = u - _bnn(w, st)
    o = _bnn(q * eg, st) + _bnn(attn, v_new)
    g_last = jnp.sum(g, axis=1, keepdims=True)
    st_new = st * jnp.exp(g_last) + _btn(k * jnp.exp(g_last - gc), v_new)
    o = o * lax.rsqrt(jnp.mean(o * o, -1, keepdims=True) + EPS) * o_norm
    return o * _silu(gate), st_new, t_inv


GDN_HP = 8
_GW = GDN_HP * GDN_D
_GB = GDN_H // GDN_HP


def _gdn_specs(n_chunks, rev):
    def tok(col):
        if rev:
            return pl.BlockSpec((GDN_C, _GW), lambda h, n: (n_chunks - 1 - n, col + h))
        return pl.BlockSpec((GDN_C, _GW), lambda h, n: (n, col + h))
    par = pl.BlockSpec((1, _GW), lambda h, n: (0, h))
    shared = pl.BlockSpec((1, GDN_D), lambda h, n: (0, 0))
    if rev:
        st = pl.BlockSpec((GDN_HP, None, GDN_D, GDN_D), lambda h, n: (h, n_chunks - 1 - n, 0, 0))
    else:
        st = pl.BlockSpec((GDN_HP, None, GDN_D, GDN_D), lambda h, n: (h, n, 0, 0))
    return tok, par, shared, st


def _heads(ref):
    return jnp.stack([ref[:, h * GDN_D:(h + 1) * GDN_D] for h in range(ref.shape[1] // GDN_D)])


def gdn_chunk_fwd(qkv, z, a_log_x, dt_bias_x, o_norm, name):
    T = qkv.shape[0]
    n_chunks = T // GDN_C
    H = GDN_H
    tok, par, shared, st_spec = _gdn_specs(n_chunks, False)

    def body(q_ref, k_ref, v_ref, g_ref, bl_ref, al_ref, a_ref, dt_ref, on_ref, o_ref, st_ref, ti_ref, state):
        @pl.when(pl.program_id(1) == 0)
        def _():
            state[...] = jnp.zeros_like(state)

        st = state[...]
        st_ref[...] = st
        o, st_new, t_inv = _gdn_chunk(_heads(q_ref), _heads(k_ref), _heads(v_ref), _heads(g_ref), _heads(bl_ref), _heads(al_ref),
                                      _heads(a_ref), _heads(dt_ref), on_ref[...], st)
        for hh in range(GDN_HP):
            o_ref[:, hh * GDN_D:(hh + 1) * GDN_D] = o[hh].astype(o_ref.dtype)
        ti_ref[...] = t_inv
        state[...] = st_new

    B = _GB
    return pl.pallas_call(
        body, name=name, grid=(B, n_chunks),
        in_specs=[tok(0), tok(B), tok(2 * B), tok(3 * B), tok(4 * B), tok(5 * B), par, par, shared],
        out_specs=[tok(0), st_spec, pl.BlockSpec((GDN_HP, None, GDN_C, GDN_C), lambda h, n: (h, n, 0, 0))],
        out_shape=[jax.ShapeDtypeStruct((T, H * GDN_D), BF16), jax.ShapeDtypeStruct((H, n_chunks, GDN_D, GDN_D), F32),
                   jax.ShapeDtypeStruct((H, n_chunks, GDN_C, GDN_C), F32)],
        scratch_shapes=[pltpu.VMEM((GDN_HP, GDN_D, GDN_D), F32)],
        compiler_params=_cparams(("parallel", "arbitrary")),
    )(qkv, qkv, qkv, z, z, z, a_log_x, dt_bias_x, o_norm)


def gdn_chunk_bwd(qkv, z, a_log_x, dt_bias_x, o_norm, states, t_invs, do, name):
    T = qkv.shape[0]
    n_chunks = T // GDN_C
    H = GDN_H
    tok, par, shared, st_spec = _gdn_specs(n_chunks, True)

    def body(q_ref, k_ref, v_ref, g_ref, bl_ref, al_ref, a_ref, dt_ref, on_ref, st_ref, ti_ref, do_ref,
             dqkv_ref, dg_ref, dba_ref, da_ref, ddt_ref, don_ref, dstate):
        h, n = pl.program_id(0), pl.program_id(1)

        @pl.when(n == 0)
        def _():
            dstate[...] = jnp.zeros_like(dstate)
            da_ref[...] = jnp.zeros_like(da_ref)
            ddt_ref[...] = jnp.zeros_like(ddt_ref)

        @pl.when((n == 0) & (h == 0))
        def _():
            don_ref[...] = jnp.zeros_like(don_ref)

        t_known = ti_ref[...]
        _, vjp = jax.vjp(lambda *ins: _gdn_chunk(*ins, t_known=t_known)[:2],
                         _heads(q_ref), _heads(k_ref), _heads(v_ref), _heads(g_ref), _heads(bl_ref), _heads(al_ref),
                         _heads(a_ref), _heads(dt_ref), on_ref[...], st_ref[...])
        dq, dk, dv, dg, dbl, dal, da, ddt, don, dst = vjp((_heads(do_ref).astype(F32), dstate[...]))
        lane = lax.broadcasted_iota(jnp.int32, (GDN_C, LANES), 1)
        dba = jnp.zeros((GDN_C, LANES), F32)
        for hh in range(GDN_HP):
            cols = slice(hh * GDN_D, (hh + 1) * GDN_D)
            for part, d in enumerate((dq, dk, dv)):
                dqkv_ref[:, part * H * GDN_D + hh * GDN_D:part * H * GDN_D + (hh + 1) * GDN_D] = d[hh]
            dg_ref[:, cols] = dg[hh].astype(dg_ref.dtype)
            dba = jnp.where(lane == hh, jnp.sum(dbl[hh], axis=-1, keepdims=True), dba)
            dba = jnp.where(lane == H + hh, jnp.sum(dal[hh], axis=-1, keepdims=True), dba)
            da_ref[:, cols] += da[hh]
            ddt_ref[:, cols] += ddt[hh]
        dba_ref[...] = dba.astype(dba_ref.dtype)
        don_ref[...] += don
        dstate[...] = dst

    tok0 = tok(0)
    B = _GB
    assert B == 1
    bf_tok = jax.ShapeDtypeStruct((T, H * GDN_D), BF16)
    par_sh = jax.ShapeDtypeStruct((1, H * GDN_D), F32)
    return pl.pallas_call(
        body, name=name, grid=(B, n_chunks),
        in_specs=[tok(0), tok(B), tok(2 * B), tok(3 * B), tok(4 * B), tok(5 * B), par, par, shared, st_spec,
                  pl.BlockSpec((GDN_HP, None, GDN_C, GDN_C), lambda h, n: (h, n_chunks - 1 - n, 0, 0)), tok0],
        out_specs=[pl.BlockSpec((GDN_C, 3 * H * GDN_D), lambda h, n: (n_chunks - 1 - n, 0)), tok0,
                   pl.BlockSpec((GDN_C, LANES), lambda h, n: (n_chunks - 1 - n, 0)), par, par, shared],
        out_shape=[jax.ShapeDtypeStruct((T, 3 * H * GDN_D), F32), bf_tok, jax.ShapeDtypeStruct((T, LANES), BF16), par_sh, par_sh,
                   jax.ShapeDtypeStruct((1, GDN_D), F32)],
        scratch_shapes=[pltpu.VMEM((GDN_HP, GDN_D, GDN_D), F32)],
        compiler_params=_cparams(("arbitrary", "arbitrary")),
    )(qkv, qkv, qkv, z, z, z, a_log_x, dt_bias_x, o_norm, states, t_invs, do)


def _prefetch_call(body, name, grid, in_specs, out_specs, out_shape, aliases=None):
    return pl.pallas_call(
        body, name=name,
        grid_spec=pltpu.PrefetchScalarGridSpec(num_scalar_prefetch=1, grid=grid, in_specs=in_specs, out_specs=out_specs),
        out_shape=out_shape, input_output_aliases=aliases or {},
        compiler_params=_cparams(("parallel",) * len(grid)))


def cast_into(src, src_row0, rows, slab, row0, me, name):
    width = src.shape[1]
    tr = _pick_rows(rows, 1024, row0, src_row0)
    assert rows % tr == 0 and row0 % tr == 0 and src_row0 % tr == 0

    def body(me_ref, s_ref, *refs):
        refs[-1][...] = s_ref[...].astype(refs[-1].dtype)

    in_specs = [pl.BlockSpec((tr, width), lambda r, me_ref: (src_row0 // tr + r, 0))]
    args = [src]
    aliases = {}
    if slab.arr is not None:
        in_specs.append(_ANY)
        args.append(slab.arr)
        aliases = {2: 0}
    slab.arr = _prefetch_call(
        body, name, (rows // tr,), in_specs,
        pl.BlockSpec((None, tr, width), lambda r, me_ref: (me_ref[0], row0 // tr + r, 0)),
        jax.ShapeDtypeStruct(slab.shape, slab.dtype), aliases)(me, *args)


def pair_add(g, b, c_idx, name):
    n, rh, w = b.shape
    tr = _pick_rows(rh, 1024)
    nb = rh // tr

    def body(c_ref, g_ref, b_ref, o_ref):
        o_ref[...] = (g_ref[...].astype(F32) + b_ref[...].astype(F32)).astype(o_ref.dtype)

    return _prefetch_call(
        body, name, (n, nb),
        [pl.BlockSpec((None, tr, w), lambda k, r, c: (k, c[0] * nb + r, 0)), pl.BlockSpec((None, tr, w), lambda k, r, c: (k, r, 0))],
        pl.BlockSpec((None, tr, w), lambda k, r, c: (k, r, 0)), jax.ShapeDtypeStruct(b.shape, BF16))(c_idx, g, b)


def chip_sum(p, rv, mc, name):
    n, rh, w = p.shape
    tr = _pick_rows(rh, 512)
    nb = rh // tr

    def body(mc_ref, p_ref, rv_ref, o_ref):
        me = mc_ref[0]
        acc = None
        for k in range(n):
            part = jnp.where(me == k, p_ref[...], rv_ref[k]).astype(F32)
            acc = part if acc is None else acc + part
        o_ref[...] = acc.astype(o_ref.dtype)

    return _prefetch_call(
        body, name, (nb,),
        [pl.BlockSpec((None, tr, w), lambda r, mc_ref: (mc_ref[0], r, 0)), pl.BlockSpec((n, tr, w), lambda r, mc_ref: (0, r, 0))],
        pl.BlockSpec((tr, w), lambda r, mc_ref: (mc_ref[1] * nb + r, 0)), jax.ShapeDtypeStruct((2 * rh, w), BF16))(mc, p, rv)


def adamw(red, row0, w, m, v, w_row0, rows, prev, name):
    cols = w.shape[1]
    tr = _pick_rows(rows, 512, row0, w_row0)
    assert rows % tr == 0 and row0 % tr == 0 and w_row0 % tr == 0

    def body(g_ref, w_ref, m_ref, v_ref, *refs):
        go_ref, d_ref, nm_ref, nv_ref = refs[-4:]
        gv = g_ref[...].astype(F32)
        nm = ADAM_B1 * m_ref[...] + (1.0 - ADAM_B1) * gv
        nv = ADAM_B2 * v_ref[...] + (1.0 - ADAM_B2) * (gv * gv)
        m_hat = nm / (1.0 - ADAM_B1 ** ADAM_STEP)
        v_hat = nv / (1.0 - ADAM_B2 ** ADAM_STEP)
        go_ref[...] = gv
        d_ref[...] = -ADAM_LR * (m_hat / (jnp.sqrt(v_hat) + ADAM_EPS) + ADAM_WD * w_ref[...])
        nm_ref[...] = nm
        nv_ref[...] = nv

    spec = pl.BlockSpec((tr, cols), lambda r: (w_row0 // tr + r, 0))
    sh = jax.ShapeDtypeStruct(w.shape, F32)
    in_specs = [pl.BlockSpec((tr, cols), lambda r: (row0 // tr + r, 0)), spec, spec, spec]
    args, aliases = [red, w, m, v], {}
    if prev is not None:
        in_specs += [_ANY] * 4
        args += list(prev)
        aliases = {4 + k: k for k in range(4)}
    return pl.pallas_call(
        body, name=name, grid=(rows // tr,), in_specs=in_specs, out_specs=[spec] * 4, out_shape=[sh] * 4,
        input_output_aliases=aliases, compiler_params=_cparams(("parallel",)),
    )(*args)


def _place():
    x, y, c = lax.axis_index("x"), lax.axis_index("y"), lax.axis_index("c")
    chips = [(1 - x, y), (x, 1 - y), (1 - x, 1 - y)]
    return x, y, c, chips


def _chip_index(cx, cy):
    return 2 * cx + cy


def _remote(src, dst, send_sem, recv_sem, to):
    return pltpu.make_async_remote_copy(src_ref=src, dst_ref=dst, send_sem=send_sem, recv_sem=recv_sem,
                                        device_id=to, device_id_type=MESH)


def _comm_call(body, name, ins, out_shapes, n_sems, aliases):
    return pl.pallas_call(
        body, name=name, in_specs=[_ANY] * len(ins), out_specs=[_ANY] * len(out_shapes), out_shape=out_shapes,
        scratch_shapes=[pltpu.SemaphoreType.DMA((n_sems,)), pltpu.SemaphoreType.DMA((n_sems,))],
        input_output_aliases=aliases,
    )(*ins)


def pair_swap_halves(slabs, name="grad_pair_swap"):
    n = len(slabs)

    def body(*refs):
        in_refs, out_refs, send_sems, recv_sems = refs[:n], refs[n:2 * n], refs[-2], refs[-1]
        x, y, c, _ = _place()
        cps = []
        for a in range(n):
            rh = in_refs[a].shape[1] // 2
            cp = _remote(in_refs[a].at[:, pl.ds((1 - c) * rh, rh), :], out_refs[a], send_sems.at[a], recv_sems.at[a], (x, y, 1 - c))
            cp.start()
            cps.append(cp)
        for cp in cps:
            cp.wait()

    outs = [jax.ShapeDtypeStruct((s.shape[0], s.shape[1] // 2, s.shape[2]), s.dtype) for s in slabs]
    return _comm_call(body, name, slabs, outs, n, {})


def pair_join_halves(reds, name="grad_pair_join"):
    n = len(reds)

    def body(*refs):
        in_refs, out_refs, send_sems, recv_sems = refs[:n], refs[n:2 * n], refs[-2], refs[-1]
        x, y, c, _ = _place()
        cps = []
        for a in range(n):
            rh = in_refs[a].shape[0] // 2
            mine = pl.ds(c * rh, rh)
            cp = _remote(in_refs[a].at[mine], out_refs[a].at[mine], send_sems.at[a], recv_sems.at[a], (x, y, 1 - c))
            cp.start()
            cps.append(cp)
        for a in range(n):
            rh = in_refs[a].shape[0] // 2
            got = out_refs[a].at[pl.ds((1 - c) * rh, rh)]
            _remote(got, got, send_sems.at[a], recv_sems.at[a], (x, y, 1 - c)).wait_recv()
        for cp in cps:
            cp.wait_send()

    return _comm_call(body, name, reds, [jax.ShapeDtypeStruct(r.shape, r.dtype) for r in reds], n, {a: a for a in range(n)})


_HBM = pl.BlockSpec(memory_space=pltpu.HBM)
_SEM = pl.BlockSpec(memory_space=pltpu.SEMAPHORE)
_EFFECT = pltpu.SideEffectType.DATAFLOW_SIDE_EFFECTING


def _in_hbm(a):
    return pltpu.with_memory_space_constraint(a, pltpu.HBM)


def _hbm_like(a):
    return pltpu.HBM(a.shape, a.dtype)


def _start_call(body, name, ins, n_sems, after):
    n = len(ins)
    res = pl.pallas_call(
        body, name=name, in_specs=[_HBM] * n + [_ANY],
        out_specs=[_SEM, _SEM] + [_HBM] * n + [pl.BlockSpec(memory_space=pltpu.VMEM)],
        out_shape=[pltpu.SemaphoreType.DMA((n_sems,)), pltpu.SemaphoreType.DMA((n_sems,))] + [_hbm_like(a) for a in ins]
        + [jax.ShapeDtypeStruct((8, LANES), F32)],
        input_output_aliases={a: 2 + a for a in range(n)},
        compiler_params=pltpu.CompilerParams(has_side_effects=_EFFECT),
    )(*[_in_hbm(a) for a in ins], after)
    return res[0], res[1], list(res[2:2 + n]), res[-1]


def _wait_call(body, name, thru, send_sems, recv_sems, after):
    n = len(thru)
    after = list(after) if isinstance(after, (list, tuple)) else [after]
    return pl.pallas_call(
        body, name=name, in_specs=[_HBM] * n + [_SEM, _SEM] + [_ANY] * len(after), out_specs=[_HBM] * n,
        out_shape=[_hbm_like(a) for a in thru], input_output_aliases={a: a for a in range(n)},
        compiler_params=pltpu.CompilerParams(has_side_effects=_EFFECT),
    )(*thru, send_sems, recv_sems, *after)


def gather_start(slabs, after, name="weight_gather_start"):
    n = len(slabs)

    def body(*refs):
        g_refs, send_sems, recv_sems, token = refs[:n], refs[n + 1], refs[n + 2], refs[-1]
        x, y, c, chips = _place()
        me = _chip_index(x, y)
        for a in range(n):
            rh = g_refs[a].shape[1] // 2
            mine = g_refs[a].at[me, pl.ds(c * rh, rh)]
            for j, chip in enumerate(chips):
                _remote(mine, mine, send_sems.at[3 * a + j], recv_sems.at[3 * a + j], (*chip, c)).start()
        token[...] = jnp.zeros_like(token)

    return _start_call(body, name, slabs, 3 * n, after)


def gather_wait(send_sems, recv_sems, thru, after, name="weight_gather_wait"):
    n = len(thru)

    def body(*refs):
        g_refs, send_sems, recv_sems = refs[:n], refs[n], refs[n + 1]
        x, y, c, chips = _place()
        me = _chip_index(x, y)
        for a in range(n):
            rh = g_refs[a].shape[1] // 2
            rows = pl.ds(c * rh, rh)
            for j, chip in enumerate(chips):
                mine, got = g_refs[a].at[me, rows], g_refs[a].at[_chip_index(*chip), rows]
                _remote(mine, mine, send_sems.at[3 * a + j], recv_sems.at[3 * a + j], (*chip, c)).wait_send()
                _remote(got, got, send_sems.at[3 * a + j], recv_sems.at[3 * a + j], (*chip, c)).wait_recv()

    return _wait_call(body, name, thru, send_sems, recv_sems, after)


def gather_forward(slabs, name="weight_gather_forward"):
    n = len(slabs)

    def body(*refs):
        in_refs, out_refs, send_sems, recv_sems = refs[:n], refs[n:2 * n], refs[-2], refs[-1]
        x, y, c, chips = _place()
        sib = (x, y, 1 - c)
        sends = []
        for a in range(n):
            rh = in_refs[a].shape[1] // 2
            for j, chip in enumerate(chips):
                k = _chip_index(*chip)
                cp = _remote(in_refs[a].at[k, pl.ds(c * rh, rh)], out_refs[a].at[k, pl.ds(c * rh, rh)], send_sems.at[3 * a + j],
                             recv_sems.at[3 * a + j], sib)
                cp.start()
                sends.append(cp)
        for a in range(n):
            rh = in_refs[a].shape[1] // 2
            for j, chip in enumerate(chips):
                got = out_refs[a].at[_chip_index(*chip), pl.ds((1 - c) * rh, rh)]
                _remote(got, got, send_sems.at[3 * a + j], recv_sems.at[3 * a + j], sib).wait_recv()
        for cp in sends:
            cp.wait_send()

    return _comm_call(body, name, slabs, [jax.ShapeDtypeStruct(s.shape, s.dtype) for s in slabs], 3 * n, {a: a for a in range(n)})


def exchange_start(parts, after, name="grad_exchange_start"):
    n = len(parts)

    def body(*refs):
        p_refs, land_refs, send_sems, recv_sems, token = refs[:n], refs[n:2 * n], refs[2 * n + 1], refs[2 * n + 2], refs[-1]
        x, y, c, chips = _place()
        me = _chip_index(x, y)
        for a in range(n):
            for j, chip in enumerate(chips):
                _remote(p_refs[a].at[_chip_index(*chip)], land_refs[a].at[me], send_sems.at[3 * a + j], recv_sems.at[3 * a + j],
                        (*chip, c)).start()
        token[...] = jnp.zeros_like(token)

    return _start_call(body, name, list(parts) + [lax.empty(p.shape, p.dtype) for p in parts], 3 * n, after)


def exchange_wait(send_sems, recv_sems, thru, after, name="grad_exchange_wait"):
    n = len(thru) // 2

    def body(*refs):
        p_refs, land_refs, send_sems, recv_sems = refs[:n], refs[n:2 * n], refs[2 * n], refs[2 * n + 1]
        x, y, c, chips = _place()
        me = _chip_index(x, y)
        for a in range(n):
            for j, chip in enumerate(chips):
                k = _chip_index(*chip)
                _remote(p_refs[a].at[k], land_refs[a].at[me], send_sems.at[3 * a + j], recv_sems.at[3 * a + j], (*chip, c)).wait_send()
                _remote(land_refs[a].at[k], land_refs[a].at[k], send_sems.at[3 * a + j], recv_sems.at[3 * a + j], (*chip, c)).wait_recv()

    res = _wait_call(body, name, thru, send_sems, recv_sems, after)
    return res[:n], res[n:]


_SLABS = {
    "mla_w_in": [("mla_w_in", 1, 0, 2)], "mla_w_uq": [("mla_w_uq", 2, 0, 2)], "mla_w_ukv": [("mla_w_ukv", 2, 0, 2)],
    "l0_mla_w_o": [("mla_w_o", 1, 0, 1)],
    "l0_w1024": [("mlp_w1", 2, 0, 1), ("mlp_w2", 1, 0, 1), ("xa_w_q", 1, 0, 1), ("xa_w_o", 1, 0, 1)],
    "l0_xa_w_kv": [("xa_w_kv", 2, 0, 1)],
    "l1_w1024": [("mlp_w1", 2, 1, 2), ("mlp_w2", 1, 1, 2), ("xa_w_q", 1, 1, 2), ("xa_w_o", 1, 1, 2), ("gdn_w_o", 1, 0, 1)],
    "l1_xa_w_kv": [("xa_w_kv", 2, 1, 2)], "gdn_w_in": [("gdn_w_in", 2, 0, 1)],
    "l23_w1024": [("mlp_w1", 2, 2, 4), ("mlp_w2", 1, 2, 4), ("xa_w_q", 1, 2, 4), ("xa_w_o", 1, 2, 4), ("mla_w_o", 1, 1, 2),
                  ("sc_w_o", 1, 0, 1)],
    "l23_xa_w_kv": [("xa_w_kv", 2, 2, 4)], "sc_w_in": [("sc_w_in", 2, 0, 1)],
}
_GROUPS = [(["mla_w_in", "mla_w_uq", "mla_w_ukv", "l0_mla_w_o"], None),
           (["l0_w1024", "l0_xa_w_kv"], (0, "xa")),
           (["l1_w1024", "l1_xa_w_kv", "gdn_w_in"], (1, "mix")),
           (["l23_w1024", "l23_xa_w_kv", "sc_w_in"], (2, "mix"))]
_RELAID = ("mla_w_in", "mla_w_uq", "mla_w_ukv", "gdn_w_in")
_SMALL = [("mla_q_norm", 1), ("mla_kv_norm", 1), ("gdn_conv_w", 2), ("sc_conv_w", 2)]
_REPL = ["gdn_a_log", "gdn_dt_bias", "gdn_o_norm", "norm_mix", "norm_mem", "norm_mlp", "mem_norm", "final_norm"]
_WEIGHTS = ['mla_w_in', 'mla_q_norm', 'mla_kv_norm', 'mla_w_uq', 'mla_w_ukv', 'mla_w_o', 'gdn_w_in', 'gdn_conv_w',
            'gdn_a_log', 'gdn_dt_bias', 'gdn_o_norm', 'gdn_w_o', 'sc_w_in', 'sc_conv_w', 'sc_w_o', 'norm_mix',
            'norm_mem', 'norm_mlp', 'xa_w_q', 'xa_w_kv', 'xa_w_o', 'mlp_w1', 'mlp_w2', 'mem_norm', 'final_norm']


class Layout:
    def __init__(self, shard_shapes):
        self.members, self.where, self.slab_dims = {}, {}, {}
        for slab, members in _SLABS.items():
            off, rows = 0, []
            for name, axis, l0, l1 in members:
                _, rpl, width = shard_shapes[name]
                rows.append((name, off, l0, l1, rpl))
                for layer in range(l0, l1):
                    self.where[(name, layer)] = (slab, off + (layer - l0) * rpl, rpl, width, axis)
                off += (l1 - l0) * rpl
            self.members[slab], self.slab_dims[slab] = rows, (off, width)

    def new_slabs(self, dtype):
        return {s: Slab(rows, width, dtype) for s, (rows, width) in self.slab_dims.items()}

    def loc(self, slabs, name, layer):
        slab, row0, rpl, width, axis = self.where[(name, layer)]
        if axis == 1:
            return Loc(slabs[slab], row0, N_CHIPS * rpl, width, 0)
        return Loc(slabs[slab], row0, rpl, N_CHIPS * width, 1)

    def _whole(self, name):
        (member,) = self.members[name]
        _, off, l0, l1, rpl = member
        assert off == 0 and l0 == 0
        return l1, rpl, self.slab_dims[name][1], dict((n, a) for n, a, _, _ in _SLABS[name])[name]

    def full(self, slabs, name):
        layers, rpl, width, axis = self._whole(name)
        blocks = slabs[name].arr.reshape(N_CHIPS, layers, rpl, width)
        return jnp.concatenate([blocks[s] for s in range(N_CHIPS)], axis=axis)

    def put_full(self, slabs, name, grad):
        layers, rpl, width, axis = self._whole(name)
        parts = jnp.stack(jnp.split(grad, N_CHIPS, axis=axis)).reshape(N_CHIPS, layers * rpl, width)
        slabs[name].arr = parts.astype(slabs[name].dtype)


def _small_pack(vals, names):
    flat = jnp.concatenate([vals[n].astype(F32).reshape(-1) for n in names])
    return jnp.pad(flat, (0, SMALL_ROWS * SMALL_COLS - flat.shape[0])).reshape(SMALL_ROWS, SMALL_COLS)


def _small_unpack(flat, like, names):
    out, off = {}, 0
    flat = flat.reshape(-1)
    for n in names:
        out[n] = flat[off:off + like[n].size].reshape(like[n].shape)
        off += like[n].size
    return out


_MLA_CFG = _Attn(MLA_H, 2 * LANES, MLA_NOPE, MLA_V, True, (MLA_NOPE + MLA_ROPE) ** -0.5, hp=8, hp_kv=8)
_XA_CFG = _Attn(XA_H, XA_D, XA_D, XA_D, False, XA_D ** -0.5, hp=4, hp_kv=4)


def _mla_weights(w_in, w_uq, w_ukv):
    w_in_p = jnp.pad(w_in, ((0, 0), (0, MLA_ZPAD - w_in.shape[1])))
    w_uq_p = jnp.pad(w_uq.reshape(MLA_QR, MLA_H, MLA_NOPE + MLA_ROPE), ((0, 0), (0, 0), (0, 2 * LANES - MLA_NOPE - MLA_ROPE)))
    w_uq_p = w_uq_p.reshape(MLA_QR, MLA_H * 2 * LANES)
    kv = w_ukv.reshape(MLA_KVR, MLA_H, MLA_NOPE + MLA_V)
    w_ukv_p = jnp.concatenate([kv[:, :, :MLA_NOPE].reshape(MLA_KVR, -1), kv[:, :, MLA_NOPE:].reshape(MLA_KVR, -1)], axis=1)
    return w_in_p, w_uq_p, w_ukv_p


def _mla_weight_grads(d_in_p, d_uq_p, d_ukv_p):
    d_in = d_in_p[:, :MLA_QR + MLA_KVR + MLA_ROPE]
    d_uq = d_uq_p.reshape(MLA_QR, MLA_H, 2 * LANES)[:, :, :MLA_NOPE + MLA_ROPE].reshape(MLA_QR, -1)
    half = MLA_H * MLA_NOPE
    d_ukv = jnp.concatenate([d_ukv_p[:, :half].reshape(MLA_KVR, MLA_H, MLA_NOPE),
                             d_ukv_p[:, half:].reshape(MLA_KVR, MLA_H, MLA_V)], axis=2).reshape(MLA_KVR, -1)
    return d_in, d_uq, d_ukv


def _mla_fwd(xs, h, wts, w_o, qn, kvn, tabs, g_next, tag):
    w_in_p, w_uq_p, w_ukv_p = wts
    z = mm(h, w_in_p, "nn", f"{tag}_in")
    cq, ckv, kr = mla_mid_fwd(z, qn, kvn, tabs, f"{tag}_mid")
    q = mm(cq, w_uq_p, "nn", f"{tag}_uq", outs=(BF16,), epi=_epi_rope_q, per_row=tabs, tm=512)
    kv = mm(ckv, w_ukv_p, "nn", f"{tag}_ukv", outs=(BF16,))
    o, lse = flash_fwd(_MLA_CFG, q, kv, kv, kr, f"{tag}_attn")
    xs, h_next = residual_norm(o, w_o, xs, g_next, f"{tag}_out")
    return xs, h_next, (z, cq, ckv, kr, q, kv, o, lse)


def _mla_bwd(dx, h, wts, w_o, g_wo, qn, kvn, tabs, saved, tag):
    w_in_p, w_uq_p, w_ukv_p = wts
    z, cq, ckv, kr, q, kv, o, lse = saved
    mm(o, dx, "tn", f"{tag}_dwo", outs=(BF16,), out_loc=g_wo)
    do = mm(dx, w_o, "nt", f"{tag}_do", outs=(BF16,))
    dqp, delta = flash_dq(_MLA_CFG, q, kv, kv, kr, o, do, lse, BF16, f"{tag}_attn_dq", rope_tabs=tabs)
    dkv, dkr = flash_dkv(_MLA_CFG, q, kv, kv, kr, do, lse, delta, BF16, f"{tag}_attn_dkv")
    d_uq_p = mm(cq, dqp, "tn", f"{tag}_duq")
    dcq = mm(dqp, w_uq_p, "nt", f"{tag}_dcq")
    d_ukv_p = mm(ckv, dkv, "tn", f"{tag}_dukv")
    dckv = mm(dkv, w_ukv_p, "nt", f"{tag}_dckv")
    dz, dqn, dkvn = mla_mid_bwd(z, qn, kvn, tabs, dcq, dckv, dkr, f"{tag}_mid_bwd")
    d_in_p = mm(h, dz, "tn", f"{tag}_din")
    dh = (dz, w_in_p)
    d_in, d_uq, d_ukv = _mla_weight_grads(d_in_p, d_uq_p, d_ukv_p)
    return dh, dict(mla_w_in=d_in, mla_w_uq=d_uq, mla_w_ukv=d_ukv, mla_q_norm=dqn, mla_kv_norm=dkvn)


_GDN_QKV = 3 * GDN_H * GDN_D
_GDN_GATE_END = _GDN_QKV + GDN_H * GDN_D


def _gdn_weights(w_in):
    rep = lambda cols: jnp.repeat(cols, GDN_D, axis=1)
    return jnp.concatenate([w_in[:, :_GDN_GATE_END], rep(w_in[:, _GDN_GATE_END:_GDN_GATE_END + GDN_H]),
                            rep(w_in[:, _GDN_GATE_END + GDN_H:])], axis=1)


def _fold(x):
    return x.reshape(x.shape[0], -1, GDN_D).sum(-1)


def _gdn_fwd(xs, h, w_in_x, conv_w, a_log, dt_bias, o_norm, w_o, g_next, tag):
    z = mm(h, w_in_x, "nn", f"{tag}_in")
    qkv = gdn_conv_fwd(z, conv_w, f"{tag}_conv")
    a_x, dt_x = jnp.repeat(a_log.reshape(1, -1), GDN_D, axis=1), jnp.repeat(dt_bias.reshape(1, -1), GDN_D, axis=1)
    og, states, t_invs = gdn_chunk_fwd(qkv, z, a_x, dt_x, o_norm.reshape(1, -1), f"{tag}_chunks")
    xs, h_next = residual_norm(og, w_o, xs, g_next, f"{tag}_out")
    return xs, h_next, (z, qkv, a_x, dt_x, og, states, t_invs)


def _gdn_weights_compact(w_in):
    return jnp.pad(w_in, ((0, 0), (0, LANES - 2 * GDN_H)))


def _gdn_bwd(dx, h, w_in_c, conv_w, o_norm, w_o, g_wo, saved, tag):
    z, qkv, a_x, dt_x, og, states, t_invs = saved
    mm(og, dx, "tn", f"{tag}_dwo", outs=(BF16,), out_loc=g_wo)
    dog = mm(dx, w_o, "nt", f"{tag}_dog")
    dqkv, dgate, dba, da_x, ddt_x, don = gdn_chunk_bwd(qkv, z, a_x, dt_x, o_norm.reshape(1, -1), states, t_invs, dog,
                                                       f"{tag}_chunks_bwd")
    dpre, dconv = gdn_conv_bwd(z, conv_w, dqkv, f"{tag}_conv_bwd")
    dz = jnp.concatenate([dpre, dgate, dba], axis=1)
    d_in_c = mm(h, dz, "tn", f"{tag}_din")
    dh = (dz, w_in_c)
    return dh, dict(gdn_w_in=d_in_c[:, :_GDN_GATE_END + 2 * GDN_H], gdn_conv_w=dconv, gdn_a_log=_fold(da_x).reshape(-1),
                    gdn_dt_bias=_fold(ddt_x).reshape(-1), gdn_o_norm=don.reshape(-1))


def _sc_fwd(xs, h, w_in, conv_w, w_o, g_next, tag):
    z = mm(h, w_in, "nn", f"{tag}_in")
    y = sc_fwd(z, conv_w, f"{tag}_conv")
    xs, h_next = residual_norm(y, w_o, xs, g_next, f"{tag}_out")
    return xs, h_next, (z, y)


def _sc_bwd(dx, h, w_in, g_win, conv_w, w_o, g_wo, saved, tag):
    z, y = saved
    mm(y, dx, "tn", f"{tag}_dwo", outs=(BF16,), out_loc=g_wo)
    dy = mm(dx, w_o, "nt", f"{tag}_dy")
    db, dc, du, dconv = sc_bwd(z, conv_w, dy, f"{tag}_conv_bwd")
    dz = jnp.concatenate([db, dc, du], axis=1)
    mm(h, dz, "tn", f"{tag}_din", outs=(BF16,), out_loc=g_win)
    dh = (dz, w_in)
    return dh, dict(sc_conv_w=dconv)


def local_step(x, mem, pos, target, lay, wslabs, gslabs, small, before=None, after_bwd=None):
    depth = small["norm_mix"].shape[0]
    W = lambda name, layer: lay.loc(wslabs, name, layer)
    G = lambda name, layer: lay.loc(gslabs, name, layer)
    tabs = rope_tables(pos)
    mem_n = rmsnorm_fwd(mem, small["mem_norm"], "mem_norm")
    full = {n: lay.full(wslabs, n) for n in ("mla_w_in", "mla_w_uq", "mla_w_ukv")}
    mla_w = [_mla_weights(full["mla_w_in"][j], full["mla_w_uq"][j], full["mla_w_ukv"][j]) for j in range(full["mla_w_in"].shape[0])]
    gdn_in_x, gdn_in_c = {}, {}

    xs, h_pre = x, None
    saved = []
    for i in range(depth):
        j, kind = i // 3, i % 3
        tag = f"l{i}"
        if before is not None:
            xs = before(i, "mix", xs)
        if kind == 1:
            gdn_full = lay.full(wslabs, "gdn_w_in")[j]
            gdn_in_x[j], gdn_in_c[j] = _gdn_weights(gdn_full), _gdn_weights_compact(gdn_full)
        x_a = xs
        h = h_pre if h_pre is not None else rmsnorm_fwd(xs, small["norm_mix"][i], f"{tag}_norm_mix")
        g_mem = small["norm_mem"][i]
        if kind == 0:
            xs, hn, mix = _mla_fwd(xs, h, mla_w[j], W("mla_w_o", j), small["mla_q_norm"][j], small["mla_kv_norm"][j], tabs, g_mem,
                                   f"{tag}_mla")
        elif kind == 1:
            xs, hn, mix = _gdn_fwd(xs, h, gdn_in_x[j], small["gdn_conv_w"][j], small["gdn_a_log"][j], small["gdn_dt_bias"][j],
                                   small["gdn_o_norm"][j], W("gdn_w_o", j), g_mem, f"{tag}_gdn")
        else:
            xs, hn, mix = _sc_fwd(xs, h, W("sc_w_in", j), small["sc_conv_w"][j], W("sc_w_o", j), g_mem, f"{tag}_sc")
        if before is not None:
            xs = before(i, "xa", xs)
        x_b = xs
        xq = mm(hn, W("xa_w_q", i), "nn", f"{tag}_xa_q", outs=(BF16,))
        xkv = mm(mem_n, W("xa_w_kv", i), "nn", f"{tag}_xa_kv", outs=(BF16,))
        xo, xlse = flash_fwd(_XA_CFG, xq, xkv, xkv, None, f"{tag}_xa_attn")
        xs, hm = residual_norm(xo, W("xa_w_o", i), xs, small["norm_mlp"][i], f"{tag}_xa_out")
        x_c = xs
        h1, act = mm(hm, W("mlp_w1", i), "nn", f"{tag}_mlp_up", outs=(BF16, BF16), epi=_epi_relu2)
        xs, h_pre = residual_norm(act, W("mlp_w2", i), xs, small["norm_mix"][i + 1] if i + 1 < depth else None,
                                  f"{tag}_mlp_down", tm=512)
        saved.append((x_a, h, mix, x_b, hn, xq, xkv, xo, xlse, x_c, hm, h1, act))

    se, dx, d_final = loss_head(xs, small["final_norm"], target)

    per_layer = {n: [None] * depth for n in ("norm_mix", "norm_mem", "norm_mlp")}
    mixer = {}
    dmem_n = jnp.zeros(mem.shape, F32)
    for i in reversed(range(depth)):
        j, kind = i // 3, i % 3
        tag = f"l{i}"
        x_a, h, mix, x_b, hn, xq, xkv, xo, xlse, x_c, hm, h1, act = saved[i]
        mm(act, dx, "tn", f"{tag}_mlp_dw2", outs=(BF16,), out_loc=G("mlp_w2", i))
        dh1 = mm(dx, W("mlp_w2", i), "nt", f"{tag}_mlp_dh1", outs=(BF16,), epi=_epi_relu2_bwd, extras=(h1,))
        mm(hm, dh1, "tn", f"{tag}_mlp_dw1", outs=(BF16,), out_loc=G("mlp_w1", i))
        dx, dg = mm(dh1, W("mlp_w1", i), "nt", f"{tag}_mlp_dhm", epi=_epi_norm_bwd, extras=(x_c, dx), vecs=(small["norm_mlp"][i],),
                    row_outs=1, tm=512)
        per_layer["norm_mlp"][i] = dg.reshape(-1)
        mm(xo, dx, "tn", f"{tag}_xa_dwo", outs=(BF16,), out_loc=G("xa_w_o", i))
        dxo = mm(dx, W("xa_w_o", i), "nt", f"{tag}_xa_do", outs=(BF16,))
        dxq, xdelta = flash_dq(_XA_CFG, xq, xkv, xkv, None, xo, dxo, xlse, BF16, f"{tag}_xa_attn_dq")
        (dxkv,) = flash_dkv(_XA_CFG, xq, xkv, xkv, None, dxo, xlse, xdelta, BF16, f"{tag}_xa_attn_dkv")
        mm(hn, dxq, "tn", f"{tag}_xa_dwq", outs=(BF16,), out_loc=G("xa_w_q", i))
        dx, dg = mm(dxq, W("xa_w_q", i), "nt", f"{tag}_xa_dhn", epi=_epi_norm_bwd, extras=(x_b, dx), vecs=(small["norm_mem"][i],),
                    row_outs=1, tm=512)
        per_layer["norm_mem"][i] = dg.reshape(-1)
        mm(mem_n, dxkv, "tn", f"{tag}_xa_dwkv", outs=(BF16,), out_loc=G("xa_w_kv", i))
        dmem_n = mm(dxkv, W("xa_w_kv", i), "nt", f"{tag}_xa_dmem", epi=_epi_add, extras=(dmem_n,))
        if after_bwd is not None:
            dx = after_bwd(i, "xa", dx)
        if kind == 0:
            dh, gr = _mla_bwd(dx, h, mla_w[j], W("mla_w_o", j), G("mla_w_o", j), small["mla_q_norm"][j], small["mla_kv_norm"][j],
                              tabs, mix, f"{tag}_mla")
        elif kind == 1:
            dh, gr = _gdn_bwd(dx, h, gdn_in_c[j], small["gdn_conv_w"][j], small["gdn_o_norm"][j], W("gdn_w_o", j), G("gdn_w_o", j),
                              mix, f"{tag}_gdn")
        else:
            dh, gr = _sc_bwd(dx, h, W("sc_w_in", j), G("sc_w_in", j), small["sc_conv_w"][j], W("sc_w_o", j), G("sc_w_o", j),
                             mix, f"{tag}_sc")
        if kind == 1:
            lay.put_full(gslabs, "gdn_w_in", gr.pop("gdn_w_in")[None])
        for n, g in gr.items():
            mixer.setdefault(n, {})[j] = g
        dz_mix, w_mix = dh
        dx, dg = mm(dz_mix, w_mix, "nt", f"{tag}_mix_dh", epi=_epi_norm_bwd, extras=(x_a, dx), vecs=(small["norm_mix"][i],),
                    row_outs=1, tm=256 if kind == 1 else 512)
        per_layer["norm_mix"][i] = dg.reshape(-1)
        if after_bwd is not None:
            dx = after_bwd(i, "mix", dx)

    _, d_mem_norm = rmsnorm_bwd(mem, small["mem_norm"], dmem_n, jnp.zeros(mem.shape, F32), "mem_norm_bwd")
    grads = {n: jnp.stack(v) for n, v in per_layer.items()}
    for n, by_j in mixer.items():
        grads[n] = jnp.stack([by_j[j] for j in sorted(by_j)])
    grads["mem_norm"] = d_mem_norm
    grads["final_norm"] = d_final
    for n in ("mla_w_in", "mla_w_uq", "mla_w_ukv"):
        lay.put_full(gslabs, n, grads.pop(n))
    return se, dx, grads


def kernel(x, mem, positions, mla_w_in, mla_q_norm, mla_kv_norm, mla_w_uq, mla_w_ukv, mla_w_o, gdn_w_in, gdn_conv_w, gdn_a_log, gdn_dt_bias, gdn_o_norm, gdn_w_o, sc_w_in, sc_conv_w, sc_w_o, norm_mix, norm_mem, norm_mlp, xa_w_q, xa_w_kv, xa_w_o, mlp_w1, mlp_w2, mem_norm, final_norm, loss_target, m_mla_w_in, m_mla_q_norm, m_mla_kv_norm, m_mla_w_uq, m_mla_w_ukv, m_mla_w_o, m_gdn_w_in, m_gdn_conv_w, m_gdn_a_log, m_gdn_dt_bias, m_gdn_o_norm, m_gdn_w_o, m_sc_w_in, m_sc_conv_w, m_sc_w_o, m_norm_mix, m_norm_mem, m_norm_mlp, m_xa_w_q, m_xa_w_kv, m_xa_w_o, m_mlp_w1, m_mlp_w2, m_mem_norm, m_final_norm, v_mla_w_in, v_mla_q_norm, v_mla_kv_norm, v_mla_w_uq, v_mla_w_ukv, v_mla_w_o, v_gdn_w_in, v_gdn_conv_w, v_gdn_a_log, v_gdn_dt_bias, v_gdn_o_norm, v_gdn_w_o, v_sc_w_in, v_sc_conv_w, v_sc_w_o, v_norm_mix, v_norm_mem, v_norm_mlp, v_xa_w_q, v_xa_w_kv, v_xa_w_o, v_mlp_w1, v_mlp_w2, v_mem_norm, v_final_norm):
    given = dict(locals())
    p = {n: given[n] for n in _WEIGHTS}
    mom = {n: given["m_" + n] for n in _WEIGHTS}
    var = {n: given["v_" + n] for n in _WEIGHTS}
    split = sorted({n for members in _SLABS.values() for n, _, _, _ in members})
    lay = Layout({n: p[n].shape for n in split})
    flat2d = lambda a: a.reshape(-1, a.shape[-1])

    me = (2 * lax.axis_index("x") + lax.axis_index("y")).astype(jnp.int32)
    core = lax.axis_index("c").astype(jnp.int32)
    me1, c1, mc = me.reshape(1), core.reshape(1), jnp.stack([me, core])

    wslabs = lay.new_slabs(BF16)

    def cast_group(slabs, chip):
        for slab in slabs:
            for name, off, l0, l1, rpl in lay.members[slab]:
                cast_into(flat2d(p[name]), l0 * rpl, (l1 - l0) * rpl, wslabs[slab], off, chip, f"cast_{slab}_{name}")

    first = _GROUPS[0][0]
    cast_group(first, me1)
    small_names = [n for n, _ in _SMALL]
    words = lax.bitcast_convert_type(jnp.concatenate([p[n].reshape(-1) for n in small_names]), BF16).reshape(-1)
    words = jnp.pad(words, (0, SMALL_ROWS * SMALL_COLS - words.shape[0])).reshape(1, SMALL_ROWS, SMALL_COLS)
    small_slab = lax.dynamic_update_slice(jnp.zeros((N_CHIPS, SMALL_ROWS, SMALL_COLS), BF16), words, (me, 0, 0))

    send0, recv0, thru0, token = gather_start([wslabs[s].arr for s in first] + [small_slab], me1, "weight_gather_start_first")
    in_flight = {}
    for slabs, point in _GROUPS[1:]:
        cast_group(slabs, me1 + token[0, 0].astype(jnp.int32))
        send, recv, thru, token = gather_start([wslabs[s].arr for s in slabs], token, f"weight_gather_start_{slabs[0]}")
        in_flight[point] = (send, recv, thru, slabs)
    started_token = token
    landed = gather_wait(send0, recv0, thru0, started_token, "weight_gather_wait_first")
    gathered = gather_forward(landed, "weight_gather_forward_first")
    for s, arr in zip(first, gathered):
        wslabs[s].arr = arr

    def before(i, stage, xs):
        if (i, stage) in in_flight:
            send, recv, thru, slabs = in_flight[(i, stage)]
            landed = gather_wait(send, recv, thru, xs, f"weight_gather_wait_{slabs[0]}")
            for s, arr in zip(slabs, gather_forward(landed, f"weight_gather_forward_{slabs[0]}")):
                wslabs[s].arr = arr
        return xs

    small = {n: p[n] for n in _REPL}
    got, off = gathered[-1].reshape(N_CHIPS, -1), 0
    for n, ax in _SMALL:
        vals = lax.bitcast_convert_type(got[:, off:off + 2 * p[n].size].reshape(N_CHIPS, p[n].size, 2), F32)
        vals = vals.reshape((N_CHIPS,) + p[n].shape)
        small[n] = jnp.concatenate([vals[s] for s in range(N_CHIPS)], axis=ax)
        off += 2 * p[n].size

    gslabs = lay.new_slabs(BF16)
    complete_at = {point: slabs for slabs, point in _GROUPS[1:]}
    exchanging = []

    def after_bwd(i, stage, dx):
        if (i, stage) not in complete_at:
            return dx
        slabs = complete_at[(i, stage)]
        g = [gslabs[s].arr for s in slabs]
        swapped = pair_swap_halves(g, f"grad_pair_swap_{slabs[0]}")
        part = [pair_add(a, b, c1, f"pair_add_{s}") for a, b, s in zip(g, swapped, slabs)]
        send, recv, thru, token = exchange_start(part, c1, f"grad_exchange_start_{slabs[0]}")
        exchanging.append((slabs, send, recv, thru))
        return dx + token[0, 0]

    se, dx, sgrads = local_step(x[0], mem[0], positions.reshape(-1, 1), loss_target[0], lay, wslabs, gslabs, small,
                                before, after_bwd)
    loss = lax.psum(0.5 * jnp.sum(se) / x.shape[-1], ("x", "y", "c"))
    names, parts, received = [], [], []
    for slabs, send, recv, thru in exchanging:
        part, got = exchange_wait(send, recv, thru, dx, f"grad_exchange_wait_{slabs[0]}")
        names, parts, received = names + slabs, parts + list(part), received + list(got)

    axes = dict(_SMALL)
    small_order = small_names + _REPL
    slots = []
    for s in range(N_CHIPS):
        vals = {n: (lax.slice_in_dim(g, s * p[n].shape[axes[n]], (s + 1) * p[n].shape[axes[n]], axis=axes[n]) if n in axes else g)
                for n, g in sgrads.items()}
        slots.append(_small_pack(vals, small_order))
    g_last = [gslabs[s].arr for s in first] + [jnp.stack(slots).astype(BF16)]
    names_last = first + ["small"]
    swapped_last = pair_swap_halves(g_last, "grad_pair_swap_last")
    part_last = [pair_add(g, b, c1, f"pair_add_{s}") for g, b, s in zip(g_last, swapped_last, names_last)]
    send, recv, thru, token = exchange_start(part_last, c1, "grad_exchange_start_last")
    mc_after = mc + token[0, 0].astype(jnp.int32)
    halves = [chip_sum(q, r, mc_after, f"chip_sum_{s}") for q, r, s in zip(parts, received, names)]
    part_last, got_last = exchange_wait(send, recv, thru, list(halves), "grad_exchange_wait_last")
    halves += [chip_sum(q, r, mc, f"chip_sum_{s}") for q, r, s in zip(part_last, got_last, names_last)]
    reduced = dict(zip(names + names_last, pair_join_halves(halves)))

    res = {}
    for slab in _SLABS:
        for name, off, l0, l1, rpl in lay.members[slab]:
            res[name] = adamw(reduced[slab], off, flat2d(p[name]), flat2d(mom[name]), flat2d(var[name]), l0 * rpl, (l1 - l0) * rpl,
                              res.get(name), f"adamw_{slab}_{name}")
    for name in split:
        res[name] = [o.reshape(p[name].shape) for o in res[name]]
    sp = {k: _small_pack(d, small_order) for k, d in (("w", p), ("m", mom), ("v", var))}
    outs = adamw(reduced["small"], 0, sp["w"], sp["m"], sp["v"], 0, SMALL_ROWS, None, "adamw_small")
    unpacked = [_small_unpack(o, p, small_order) for o in outs]
    for n in small_order:
        res[n] = [u[n] for u in unpacked]
    return (loss, dx[None], *[res[n][k] for k in range(4) for n in _WEIGHTS])
```

```python
import jax
import jax.numpy as jnp
from jax import lax
from jax.experimental import pallas as pl
from jax.experimental.pallas import tpu as pltpu

F32 = jnp.float32
BF16 = jnp.bfloat16
MESH = pl.DeviceIdType.MESH

EPS = 1e-6
ROPE_THETA = 10000.0
N_CHIPS = 4
LANES = 128
VMEM_LIMIT = 56 * 1024 * 1024
NEG = -1e30

MLA_H, MLA_NOPE, MLA_ROPE, MLA_V = 8, 128, 64, 128
MLA_QR, MLA_KVR = 384, 256
MLA_ZPAD = 768
GDN_H, GDN_D, GDN_C = 8, 128, 64
XA_H, XA_D = 4, 256

ADAM_LR, ADAM_B1, ADAM_B2, ADAM_EPS, ADAM_WD, ADAM_STEP = 0.001, 0.9, 0.999, 1e-08, 0.01, 10

SMALL_ROWS, SMALL_COLS = 32, 1024


def _cparams(sem=None):
    return pltpu.CompilerParams(dimension_semantics=sem, vmem_limit_bytes=VMEM_LIMIT)


def _pick(dim, pref):
    t = (min(pref, dim) // LANES) * LANES
    while t >= LANES:
        if dim % t == 0:
            return t
        t -= LANES
    return dim


def _pick_rows(rows, pref, *offsets):
    t = (min(pref, rows) // 16) * 16
    while t > 16 and (rows % t or any(o % t for o in offsets)):
        t -= 16
    return t


class Slab:
    def __init__(self, rows, width, dtype, arr=None):
        self.shape, self.dtype, self.arr = (N_CHIPS, rows, width), dtype, arr


class Loc:
    def __init__(self, slab, row0, K, N, axis):
        self.slab, self.row0, self.K, self.N, self.axis = slab, row0, K, N, axis
        self.Ks = K // N_CHIPS if axis == 0 else K
        self.Ns = N // N_CHIPS if axis == 1 else N

    def tile_spec(self, tr, tc, rc):
        assert self.row0 % tr == 0 and self.Ks % tr == 0 and self.Ns % tc == 0, (self.row0, self.Ks, self.Ns, tr, tc)
        r0, rb, cb = self.row0 // tr, self.Ks // tr, self.Ns // tc
        if self.axis == 0:
            return pl.BlockSpec((None, tr, tc), lambda i, j: (rc(i, j)[0] // rb, r0 + rc(i, j)[0] % rb, rc(i, j)[1]))
        return pl.BlockSpec((None, tr, tc), lambda i, j: (rc(i, j)[1] // cb, r0 + rc(i, j)[0], rc(i, j)[1] % cb))

    def slot_spec(self, slot, tr, tc, rc):
        assert self.row0 % tr == 0, (self.row0, tr)
        r0 = self.row0 // tr
        return pl.BlockSpec((None, tr, tc), lambda i, j: (slot, r0 + rc(i, j)[0], rc(i, j)[1]))


_DIMS = {"nn": ((1,), (0,)), "nt": ((1,), (1,)), "tn": ((0,), (0,))}
_ANY = pl.BlockSpec(memory_space=pl.ANY)


def mm(a, b, mode, name, outs=(F32,), epi=None, extras=(), tm=1024, tn=1024, out_loc=None, vecs=(), row_outs=0, per_row=()):
    full_rows = bool(vecs) or row_outs > 0 or bool(per_row)
    b_loc = b if isinstance(b, Loc) else None
    if mode == "nn":
        M, K = a.shape
        K2, N = (b_loc.K, b_loc.N) if b_loc else b.shape
    elif mode == "nt":
        M, K = a.shape
        N, K2 = (b_loc.K, b_loc.N) if b_loc else b.shape
    else:
        K, M = a.shape
        K2, N = b.shape
    assert K == K2, (name, a.shape, K2, N)
    tm = _pick(out_loc.Ks if (out_loc and out_loc.axis == 0) else M, tm)
    n_split = full_rows and b_loc is not None and mode == "nt" and b_loc.axis == 0
    if out_loc is not None and out_loc.axis == 1:
        tn = _pick(out_loc.Ns, tn)
    elif n_split:
        tn = N
    elif b_loc is not None and ((mode == "nn" and b_loc.axis == 1) or (mode == "nt" and b_loc.axis == 0)):
        tn = _pick(b_loc.Ns if mode == "nn" else b_loc.Ks, tn)
    elif b_loc is not None:
        tn = N if full_rows else _pick(N, min(tn, 512))
    else:
        tn = N if full_rows else _pick(N, tn)
    assert tn == N or not full_rows, name

    parts = 1
    if mode == "tn":
        a_spec = pl.BlockSpec((K, tm), lambda i, j: (0, i))
        b_specs, b_args = [pl.BlockSpec((K, tn), lambda i, j: (0, j))], [b]
    else:
        a_spec = pl.BlockSpec((tm, K), lambda i, j: (i, 0))
        if b_loc is None:
            b_specs = [pl.BlockSpec((K, tn), lambda i, j: (0, j)) if mode == "nn" else pl.BlockSpec((tn, K), lambda i, j: (j, 0))]
            b_args = [b]
        elif mode == "nn" and b_loc.axis == 1:
            b_specs, b_args = [b_loc.tile_spec(K, tn, lambda i, j: (0, j))], [b_loc.slab.arr]
        elif n_split:
            b_specs = [b_loc.slot_spec(s, b_loc.Ks, K, lambda i, j: (0, 0)) for s in range(N_CHIPS)]
            b_args = [b_loc.slab.arr] * N_CHIPS
        elif mode == "nt" and b_loc.axis == 0:
            b_specs, b_args = [b_loc.tile_spec(tn, K, lambda i, j: (j, 0))], [b_loc.slab.arr]
        elif mode == "nn":
            parts = N_CHIPS
            b_specs = [b_loc.slot_spec(s, b_loc.Ks, tn, lambda i, j: (0, j)) for s in range(parts)]
            b_args = [b_loc.slab.arr] * parts
        else:
            parts = N_CHIPS
            b_specs = [b_loc.slot_spec(s, tn, b_loc.Ns, lambda i, j: (j, 0)) for s in range(parts)]
            b_args = [b_loc.slab.arr] * parts
    kp = K // parts
    n_b = N_CHIPS if n_split else parts
    n_ex, n_out = len(extras) + len(per_row) + len(vecs), len(outs)
    dims = (_DIMS[mode], ((), ()))

    def body(*refs):
        a_ref = refs[0]
        b_refs = refs[1:1 + n_b]
        ex_refs = refs[1 + n_b:1 + n_b + n_ex]
        o_refs = refs[len(refs) - n_out - row_outs:len(refs) - row_outs]
        r_refs = refs[len(refs) - row_outs:]
        acc = None
        if n_split:
            av = a_ref[...].astype(BF16)
            acc = jnp.concatenate([lax.dot_general(av, b_ref[...].astype(BF16), dims, preferred_element_type=F32)
                                   for b_ref in b_refs], axis=1)
        for s in range(0 if n_split else parts):
            av = a_ref[...] if parts == 1 else a_ref[:, s * kp:(s + 1) * kp]
            d = lax.dot_general(av.astype(BF16), b_refs[s][...].astype(BF16), dims, preferred_element_type=F32)
            acc = d if acc is None else acc + d
        res = epi(acc, *[e[...] for e in ex_refs]) if epi is not None else (acc,)
        for o_ref, v in zip(o_refs, res[:n_out]):
            o_ref[...] = v.astype(o_ref.dtype)
        for r_ref, v in zip(r_refs, res[n_out:]):
            @pl.when(pl.program_id(0) == 0)
            def _():
                r_ref[...] = jnp.zeros_like(r_ref)

            r_ref[...] += v

    mn_spec = pl.BlockSpec((tm, tn), lambda i, j: (i, j))
    row_spec = pl.BlockSpec((1, tn), lambda i, j: (0, j))
    in_specs = ([a_spec] + b_specs + [mn_spec] * len(extras) + [pl.BlockSpec((tm, r.shape[1]), lambda i, j: (i, 0)) for r in per_row]
                + [row_spec] * len(vecs))
    args = [a] + b_args + list(extras) + list(per_row) + [v.reshape(1, N) for v in vecs]
    aliases = {}
    if out_loc is None:
        out_specs = [mn_spec] * n_out + [row_spec] * row_outs
        out_shape = [jax.ShapeDtypeStruct((M, N), d) for d in outs] + [jax.ShapeDtypeStruct((1, N), F32)] * row_outs
    else:
        assert n_out == 1 and mode == "tn"
        out_specs = [out_loc.tile_spec(tm, tn, lambda i, j: (i, j))]
        out_shape = [jax.ShapeDtypeStruct(out_loc.slab.shape, out_loc.slab.dtype)]
        if out_loc.slab.arr is not None:
            in_specs.append(_ANY)
            args.append(out_loc.slab.arr)
            aliases = {len(args) - 1: 0}

    res = pl.pallas_call(
        body, name=name, grid=(M // tm, N // tn), in_specs=in_specs, out_specs=out_specs, out_shape=out_shape,
        input_output_aliases=aliases, compiler_params=_cparams(("arbitrary" if row_outs else "parallel", "parallel")),
    )(*args)
    if out_loc is not None:
        out_loc.slab.arr = res[0]
        return None
    return res[0] if len(res) == 1 else tuple(res)


def _epi_add(acc, r):
    return (acc + r,)


def _epi_add_norm(acc, r, g):
    x = acc + r
    return x, _rms(x, g)


def _epi_norm_bwd(acc, x, dx_in, g):
    r = lax.rsqrt(jnp.mean(x * x, axis=-1, keepdims=True) + EPS)
    xh = x * r
    dxh = acc * g
    dx = dx_in + r * (dxh - xh * jnp.mean(dxh * xh, axis=-1, keepdims=True))
    return dx, jnp.sum(acc * xh, axis=0, keepdims=True)


def residual_norm(a, w, xs, g, name, tm=1024):
    if g is None:
        return mm(a, w, "nn", name, epi=_epi_add, extras=(xs,), tm=tm), None
    return mm(a, w, "nn", name, outs=(F32, BF16), epi=_epi_add_norm, extras=(xs,), vecs=(g,), tm=tm)


def _epi_relu2(acc):
    r = jnp.maximum(acc, 0.0)
    return acc, r * r


def _epi_relu2_bwd(acc, h1):
    return (acc * (2.0 * jnp.maximum(h1.astype(F32), 0.0)),)


def _rms(x, g):
    return x * lax.rsqrt(jnp.mean(x * x, axis=-1, keepdims=True) + EPS) * g


def _row_spec(ts, cols):
    return pl.BlockSpec((ts, cols), lambda i: (i, 0))


def _par_spec(cols):
    return pl.BlockSpec((1, cols), lambda i: (0, 0))


def rmsnorm_fwd(x, g, name, ts=256):
    T, D = x.shape
    ts = min(ts, T)

    def body(x_ref, g_ref, o_ref):
        o_ref[...] = _rms(x_ref[...], g_ref[...]).astype(o_ref.dtype)

    return pl.pallas_call(
        body, name=name, grid=(T // ts,),
        in_specs=[_row_spec(ts, D), _par_spec(D)], out_specs=_row_spec(ts, D),
        out_shape=jax.ShapeDtypeStruct((T, D), BF16), compiler_params=_cparams(("parallel",)),
    )(x, g.reshape(1, D))


def rmsnorm_bwd(x, g, dy, dx_in, name, ts=256):
    T, D = x.shape
    ts = min(ts, T)

    def body(x_ref, g_ref, dy_ref, dxi_ref, dx_ref, dg_ref):
        xv = x_ref[...]
        r = lax.rsqrt(jnp.mean(xv * xv, axis=-1, keepdims=True) + EPS)
        xh = xv * r
        dyv = dy_ref[...].astype(F32)
        dxh = dyv * g_ref[...]
        dx_ref[...] = dxi_ref[...] + r * (dxh - xh * jnp.mean(dxh * xh, axis=-1, keepdims=True))
        dg = jnp.sum(dyv * xh, axis=0, keepdims=True)

        @pl.when(pl.program_id(0) == 0)
        def _():
            dg_ref[...] = jnp.zeros_like(dg_ref)

        dg_ref[...] += dg

    dx, dg = pl.pallas_call(
        body, name=name, grid=(T // ts,),
        in_specs=[_row_spec(ts, D), _par_spec(D), _row_spec(ts, D), _row_spec(ts, D)],
        out_specs=[_row_spec(ts, D), _par_spec(D)],
        out_shape=[jax.ShapeDtypeStruct((T, D), F32), jax.ShapeDtypeStruct((1, D), F32)],
        compiler_params=_cparams(("arbitrary",)),
    )(x, g.reshape(1, D), dy, dx_in)
    return dx, dg.reshape(D)


def rope_tables(pos, name="rope_tables"):
    T = pos.shape[0]
    half = MLA_ROPE // 2
    inv = ROPE_THETA ** (-jnp.arange(0, MLA_ROPE, 2, dtype=F32) / MLA_ROPE)
    inv_row = jnp.concatenate([inv, inv, jnp.zeros((LANES - MLA_ROPE,), F32)]).reshape(1, LANES)

    def body(p_ref, f_ref, c_ref, a_ref, b_ref):
        ang = p_ref[...].astype(F32) * f_ref[...]
        lane = lax.broadcasted_iota(jnp.int32, ang.shape, 1)
        c, s = jnp.cos(ang), jnp.sin(ang)
        c_ref[...] = jnp.where(lane < MLA_ROPE, c, 0.0)
        a_ref[...] = jnp.where(lane < half, -s, 0.0)
        b_ref[...] = jnp.where((lane >= half) & (lane < MLA_ROPE), s, 0.0)

    sh = jax.ShapeDtypeStruct((T, LANES), F32)
    return pl.pallas_call(body, name=name, out_shape=[sh, sh, sh], compiler_params=_cparams())(pos, inv_row)


def _roll_l(x):
    return pltpu.roll(x, LANES - MLA_ROPE // 2, 1)


def _roll_r(x):
    return pltpu.roll(x, MLA_ROPE // 2, 1)


def _rope(r, c, sa, sb):
    return r * c + _roll_l(r) * sa + _roll_r(r) * sb


def _rope_t(d, c, sa, sb):
    return d * c + _roll_r(d * sa) + _roll_l(d * sb)


def _epi_rope_q(acc, c, sa, sb):
    hw = 2 * LANES
    parts = []
    for h in range(acc.shape[1] // hw):
        parts += [acc[:, h * hw:h * hw + LANES], _rope(acc[:, h * hw + LANES:(h + 1) * hw], c, sa, sb)]
    return (jnp.concatenate(parts, axis=1),)


def mla_mid_fwd(z, qn, kvn, tabs, name, ts=256):
    T = z.shape[0]
    ts = min(ts, T)
    a0, a1 = MLA_QR, MLA_QR + MLA_KVR

    def body(z_ref, qn_ref, kvn_ref, c_ref, sa_ref, sb_ref, cq_ref, ckv_ref, kr_ref):
        cq_ref[...] = _rms(z_ref[:, 0:a0], qn_ref[...]).astype(BF16)
        ckv_ref[...] = _rms(z_ref[:, a0:a1], kvn_ref[...]).astype(BF16)
        kr_ref[...] = _rope(z_ref[:, a1:MLA_ZPAD], c_ref[...], sa_ref[...], sb_ref[...]).astype(BF16)

    return pl.pallas_call(
        body, name=name, grid=(T // ts,),
        in_specs=[_row_spec(ts, MLA_ZPAD), _par_spec(MLA_QR), _par_spec(MLA_KVR)] + [_row_spec(ts, LANES)] * 3,
        out_specs=[_row_spec(ts, MLA_QR), _row_spec(ts, MLA_KVR), _row_spec(ts, LANES)],
        out_shape=[jax.ShapeDtypeStruct((T, MLA_QR), BF16), jax.ShapeDtypeStruct((T, MLA_KVR), BF16),
                   jax.ShapeDtypeStruct((T, LANES), BF16)],
        compiler_params=_cparams(("parallel",)),
    )(z, qn.reshape(1, -1), kvn.reshape(1, -1), *tabs)


def mla_mid_bwd(z, qn, kvn, tabs, dcq, dckv, dkr, name, ts=256):
    T = z.shape[0]
    ts = min(ts, T)
    a0, a1 = MLA_QR, MLA_QR + MLA_KVR

    def body(z_ref, qn_ref, kvn_ref, c_ref, sa_ref, sb_ref, dcq_ref, dckv_ref, dkr_ref, dz_ref, dqn_ref, dkvn_ref):
        _, vq = jax.vjp(_rms, z_ref[:, 0:a0], qn_ref[...])
        dzq, dqn = vq(dcq_ref[...].astype(F32))
        _, vk = jax.vjp(_rms, z_ref[:, a0:a1], kvn_ref[...])
        dzk, dkvn = vk(dckv_ref[...].astype(F32))
        dz_ref[:, 0:a0] = dzq.astype(dz_ref.dtype)
        dz_ref[:, a0:a1] = dzk.astype(dz_ref.dtype)
        dz_ref[:, a1:MLA_ZPAD] = _rope_t(dkr_ref[...].astype(F32), c_ref[...], sa_ref[...], sb_ref[...]).astype(dz_ref.dtype)

        @pl.when(pl.program_id(0) == 0)
        def _():
            dqn_ref[...] = jnp.zeros_like(dqn_ref)
            dkvn_ref[...] = jnp.zeros_like(dkvn_ref)

        dqn_ref[...] += dqn
        dkvn_ref[...] += dkvn

    dz, dqn, dkvn = pl.pallas_call(
        body, name=name, grid=(T // ts,),
        in_specs=[_row_spec(ts, MLA_ZPAD), _par_spec(MLA_QR), _par_spec(MLA_KVR)] + [_row_spec(ts, LANES)] * 3
        + [_row_spec(ts, MLA_QR), _row_spec(ts, MLA_KVR), _row_spec(ts, LANES)],
        out_specs=[_row_spec(ts, MLA_ZPAD), _par_spec(MLA_QR), _par_spec(MLA_KVR)],
        out_shape=[jax.ShapeDtypeStruct((T, MLA_ZPAD), BF16), jax.ShapeDtypeStruct((1, MLA_QR), F32),
                   jax.ShapeDtypeStruct((1, MLA_KVR), F32)],
        compiler_params=_cparams(("arbitrary",)),
    )(z, qn.reshape(1, -1), kvn.reshape(1, -1), *tabs, dcq, dckv, dkr)
    return dz, dqn.reshape(-1), dkvn.reshape(-1)


def loss_head(x, g, target, name="loss_head", ts=256):
    T, D = x.shape
    ts = min(ts, T)

    def body(x_ref, g_ref, t_ref, se_ref, dx_ref, dg_ref):
        xv = x_ref[...]
        r = lax.rsqrt(jnp.mean(xv * xv, axis=-1, keepdims=True) + EPS)
        xh = xv * r
        err = xh * g_ref[...] - t_ref[...]
        dy = err * (1.0 / D)
        dxh = dy * g_ref[...]
        dx_ref[...] = r * (dxh - xh * jnp.mean(dxh * xh, axis=-1, keepdims=True))

        @pl.when(pl.program_id(0) == 0)
        def _():
            se_ref[...] = jnp.zeros_like(se_ref)
            dg_ref[...] = jnp.zeros_like(dg_ref)

        se_ref[...] += jnp.sum(err * err, axis=0, keepdims=True)
        dg_ref[...] += jnp.sum(dy * xh, axis=0, keepdims=True)

    se, dx, dg = pl.pallas_call(
        body, name=name, grid=(T // ts,),
        in_specs=[_row_spec(ts, D), _par_spec(D), _row_spec(ts, D)],
        out_specs=[_par_spec(D), _row_spec(ts, D), _par_spec(D)],
        out_shape=[jax.ShapeDtypeStruct((1, D), F32), jax.ShapeDtypeStruct((T, D), F32), jax.ShapeDtypeStruct((1, D), F32)],
        compiler_params=_cparams(("arbitrary",)),
    )(x, g.reshape(1, D), target)
    return se, dx, dg.reshape(D)


def _dot_nt(a, b):
    return lax.dot_general(a, b, (((1,), (1,)), ((), ())), preferred_element_type=F32)


def _dot_nn(a, b):
    return lax.dot_general(a, b, (((1,), (0,)), ((), ())), preferred_element_type=F32)


class _Attn:
    def __init__(self, H, dq, dk1, dv, causal, scale, hp, hp_kv, blk=256):
        self.H, self.dq, self.dk1, self.dv, self.causal, self.scale, self.blk = H, dq, dk1, dv, causal, scale, blk
        self.hp, self.hp_kv = hp, hp_kv


def _cols(ref, rows, hh, width):
    return ref[rows, hh * width:(hh + 1) * width]


def _keys(cfg, k1_ref, k2_ref, rows, hh):
    ks = _cols(k1_ref, rows, hh, cfg.dk1)
    if k2_ref is not None:
        ks = jnp.concatenate([ks, k2_ref[rows, :]], axis=1)
    return ks


def _attn_specs(cfg, hp, t, Tk, has_k2, by_q):
    g = cfg.H // hp
    if by_q:
        specs = [pl.BlockSpec((t, hp * cfg.dq), lambda h, i: (i, h)),
                 pl.BlockSpec((Tk, hp * cfg.dk1), lambda h, i: (0, h)),
                 pl.BlockSpec((Tk, hp * cfg.dv), lambda h, i: (0, g + h))]
        if has_k2:
            specs.append(pl.BlockSpec((Tk, LANES), lambda h, i: (0, 0)))
    else:
        specs = [None,
                 pl.BlockSpec((t, hp * cfg.dk1), lambda j, h: (j, h)),
                 pl.BlockSpec((t, hp * cfg.dv), lambda j, h: (j, g + h))]
        if has_k2:
            specs.append(pl.BlockSpec((t, LANES), lambda j, h: (j, 0)))
    return specs


def _mask(s, diagonal):
    if not diagonal:
        return s
    return jnp.where(lax.broadcasted_iota(jnp.int32, s.shape, 0) >= lax.broadcasted_iota(jnp.int32, s.shape, 1), s, NEG)


def flash_fwd(cfg, q, k1, v, k2, name):
    Tq, Tk = q.shape[0], k1.shape[0]
    t = min(cfg.blk, Tq, Tk)
    nkb = Tk // t
    has_k2 = k2 is not None
    hp = cfg.hp

    def body(*refs):
        q_ref, k1_ref, v_ref = refs[:3]
        k2_ref = refs[3] if has_k2 else None
        o_ref, lse_ref = refs[-2], refs[-1]
        i = pl.program_id(1)
        qs = [_cols(q_ref, slice(None), hh, cfg.dq) for hh in range(hp)]

        def step(j, carry, diagonal=False):
            rows = pl.ds(pl.multiple_of(j * t, t), t)
            out = []
            for hh in range(hp):
                m, l, acc = carry[hh]
                s = _mask(_dot_nt(qs[hh], _keys(cfg, k1_ref, k2_ref, rows, hh)) * cfg.scale, diagonal)
                m2 = jnp.maximum(m, jnp.max(s, axis=-1, keepdims=True))
                p = jnp.exp(s - m2)
                alpha = jnp.exp(m - m2)
                l2 = alpha * l + jnp.sum(p, axis=-1, keepdims=True)
                acc2 = alpha * acc + _dot_nn(p.astype(BF16), _cols(v_ref, rows, hh, cfg.dv))
                out.append((m2, l2, acc2))
            return tuple(out)

        init = tuple((jnp.full((t, 1), NEG, F32), jnp.zeros((t, 1), F32), jnp.zeros((t, cfg.dv), F32)) for _ in range(hp))
        res = lax.fori_loop(0, i if cfg.causal else nkb, step, init)
        if cfg.causal:
            res = step(i, res, True)
        for hh in range(hp):
            m, l, acc = res[hh]
            o_ref[:, hh * cfg.dv:(hh + 1) * cfg.dv] = (acc / l).astype(o_ref.dtype)
            lse_ref[hh] = m + jnp.log(l)

    args = [q, k1, v] + ([k2] if has_k2 else [])
    return pl.pallas_call(
        body, name=name, grid=(cfg.H // hp, Tq // t), in_specs=_attn_specs(cfg, hp, t, Tk, has_k2, True),
        out_specs=[pl.BlockSpec((t, hp * cfg.dv), lambda h, i: (i, h)), pl.BlockSpec((hp, t, 1), lambda h, i: (h, i, 0))],
        out_shape=[jax.ShapeDtypeStruct((Tq, cfg.H * cfg.dv), BF16), jax.ShapeDtypeStruct((cfg.H, Tq, 1), F32)],
        compiler_params=_cparams(("parallel", "parallel")),
    )(*args)


def flash_dq(cfg, q, k1, v, k2, o, do, lse, out_dtype, name, rope_tabs=None):
    Tq, Tk = q.shape[0], k1.shape[0]
    t = min(cfg.blk, Tq, Tk)
    nkb = Tk // t
    has_k2 = k2 is not None
    hp = cfg.hp
    n_tab = 0 if rope_tabs is None else len(rope_tabs)

    def body(*refs):
        q_ref, k1_ref, v_ref = refs[:3]
        k2_ref = refs[3] if has_k2 else None
        tab_refs = refs[len(refs) - 5 - n_tab:len(refs) - 5]
        o_ref, do_ref, lse_ref, dq_ref, dl_ref = refs[-5:]
        i = pl.program_id(1)
        qs = [_cols(q_ref, slice(None), hh, cfg.dq) for hh in range(hp)]
        dos = [_cols(do_ref, slice(None), hh, cfg.dv) for hh in range(hp)]
        lses = [lse_ref[hh] for hh in range(hp)]
        deltas = []
        for hh in range(hp):
            d = jnp.sum(dos[hh].astype(F32) * _cols(o_ref, slice(None), hh, cfg.dv).astype(F32), axis=-1, keepdims=True)
            dl_ref[hh] = d
            deltas.append(d)

        def step(j, dqs, diagonal=False):
            rows = pl.ds(pl.multiple_of(j * t, t), t)
            out = []
            for hh in range(hp):
                ks = _keys(cfg, k1_ref, k2_ref, rows, hh)
                s = _mask(_dot_nt(qs[hh], ks) * cfg.scale, diagonal)
                p = jnp.exp(s - lses[hh])
                dp = _dot_nt(dos[hh], _cols(v_ref, rows, hh, cfg.dv))
                ds = p * (dp - deltas[hh]) * cfg.scale
                out.append(dqs[hh] + _dot_nn(ds.astype(BF16), ks))
            return tuple(out)

        dqs = lax.fori_loop(0, i if cfg.causal else nkb, step, tuple(jnp.zeros((t, cfg.dq), F32) for _ in range(hp)))
        if cfg.causal:
            dqs = step(i, dqs, True)
        tabs = [r[...] for r in tab_refs]
        for hh in range(hp):
            dq = dqs[hh]
            if tabs:
                dq = jnp.concatenate([dq[:, :LANES], _rope_t(dq[:, LANES:], *tabs)], axis=1)
            dq_ref[:, hh * cfg.dq:(hh + 1) * cfg.dq] = dq.astype(dq_ref.dtype)

    ov = pl.BlockSpec((t, hp * cfg.dv), lambda h, i: (i, h))
    row1 = pl.BlockSpec((hp, t, 1), lambda h, i: (h, i, 0))
    tab_specs = [pl.BlockSpec((t, LANES), lambda h, i: (i, 0))] * n_tab
    args = [q, k1, v] + ([k2] if has_k2 else []) + list(rope_tabs or ()) + [o, do, lse]
    return pl.pallas_call(
        body, name=name, grid=(cfg.H // hp, Tq // t),
        in_specs=_attn_specs(cfg, hp, t, Tk, has_k2, True) + tab_specs + [ov, ov, row1],
        out_specs=[pl.BlockSpec((t, hp * cfg.dq), lambda h, i: (i, h)), row1],
        out_shape=[jax.ShapeDtypeStruct((Tq, cfg.H * cfg.dq), out_dtype), jax.ShapeDtypeStruct((cfg.H, Tq, 1), F32)],
        compiler_params=_cparams(("parallel", "parallel")),
    )(*args)


def flash_dkv(cfg, q, k1, v, k2, do, lse, delta, out_dtype, name):
    Tq, Tk = q.shape[0], k1.shape[0]
    t = min(cfg.blk, Tq, Tk)
    nqb = Tq // t
    has_k2 = k2 is not None
    hp = cfg.hp_kv
    assert hp == cfg.H
    v0 = cfg.H * cfg.dk1

    def body(*refs):
        q_ref, k1_ref, v_ref = refs[:3]
        k2_ref = refs[3] if has_k2 else None
        n_in = 4 if has_k2 else 3
        do_ref, lse_ref, dl_ref = refs[n_in:n_in + 3]
        dkv_ref = refs[n_in + 3]
        j, h = pl.program_id(0), pl.program_id(1)
        kss = [_keys(cfg, k1_ref, k2_ref, slice(None), hh) for hh in range(hp)]
        vss = [_cols(v_ref, slice(None), hh, cfg.dv) for hh in range(hp)]

        def step(i, carry, diagonal=False):
            rows = pl.ds(pl.multiple_of(i * t, t), t)
            out = []
            for hh in range(hp):
                dk, dv = carry[hh]
                qi, doi = _cols(q_ref, rows, hh, cfg.dq), _cols(do_ref, rows, hh, cfg.dv)
                s = _dot_nt(kss[hh], qi) * cfg.scale
                if diagonal:
                    s = jnp.where(lax.broadcasted_iota(jnp.int32, s.shape, 0) <= lax.broadcasted_iota(jnp.int32, s.shape, 1), s, NEG)
                p = jnp.exp(s - lse_ref[hh, :, rows])
                dv = dv + _dot_nn(p.astype(BF16), doi)
                ds = p * (_dot_nt(vss[hh], doi) - dl_ref[hh, :, rows]) * cfg.scale
                dk = dk + _dot_nn(ds.astype(BF16), qi)
                out.append((dk, dv))
            return tuple(out)

        init = tuple((jnp.zeros((t, cfg.dq), F32), jnp.zeros((t, cfg.dv), F32)) for _ in range(hp))
        if cfg.causal:
            res = lax.fori_loop(j + 1, nqb, step, step(j, init, True))
        else:
            res = lax.fori_loop(0, nqb, step, init)
        for hh in range(hp):
            dk, dv = res[hh]
            dkv_ref[:, hh * cfg.dk1:(hh + 1) * cfg.dk1] = dk[:, 0:cfg.dk1].astype(dkv_ref.dtype)
            dkv_ref[:, v0 + hh * cfg.dv:v0 + (hh + 1) * cfg.dv] = dv.astype(dkv_ref.dtype)
        if has_k2:
            dk2_ref = refs[n_in + 4]

            @pl.when(h == 0)
            def _():
                dk2_ref[...] = jnp.zeros_like(dk2_ref)

            for hh in range(hp):
                dk2_ref[...] += res[hh][0][:, cfg.dk1:]

    specs = _attn_specs(cfg, hp, t, Tk, has_k2, False)
    specs[0] = pl.BlockSpec((Tq, hp * cfg.dq), lambda j, h: (0, h))
    rows_all = pl.BlockSpec((hp, 1, Tq), lambda j, h: (h, 0, 0))
    specs += [pl.BlockSpec((Tq, hp * cfg.dv), lambda j, h: (0, h)), rows_all, rows_all]
    args = [q, k1, v] + ([k2] if has_k2 else []) + [do, lse.reshape(cfg.H, 1, Tq), delta.reshape(cfg.H, 1, Tq)]
    out_specs = [pl.BlockSpec((t, v0 + cfg.H * cfg.dv), lambda j, h: (j, 0))]
    out_shape = [jax.ShapeDtypeStruct((Tk, v0 + cfg.H * cfg.dv), out_dtype)]
    if has_k2:
        out_specs.append(pl.BlockSpec((t, LANES), lambda j, h: (j, 0)))
        out_shape.append(jax.ShapeDtypeStruct((Tk, LANES), F32))
    return pl.pallas_call(
        body, name=name, grid=(Tk // t, cfg.H // hp), in_specs=specs, out_specs=out_specs, out_shape=out_shape,
        compiler_params=_cparams(("parallel", "arbitrary")),
    )(*args)


def _shift_down(x, s):
    if s == 0:
        return x
    t = lax.broadcasted_iota(jnp.int32, x.shape, 0)
    return jnp.where(t >= s, pltpu.roll(x, s, 0), 0.0)


def _shift_up(x, s):
    if s == 0:
        return x
    n = x.shape[0]
    t = lax.broadcasted_iota(jnp.int32, x.shape, 0)
    return jnp.where(t < n - s, pltpu.roll(x, n - s, 0), 0.0)


def _conv(x, w_ref, kw):
    y = x * w_ref[kw - 1:kw, :]
    for j in range(kw - 1):
        y = y + _shift_down(x, kw - 1 - j) * w_ref[j:j + 1, :]
    return y


def _conv_t(d, w_ref, kw):
    y = d * w_ref[kw - 1:kw, :]
    for j in range(kw - 1):
        y = y + _shift_up(d, kw - 1 - j) * w_ref[j:j + 1, :]
    return y


def _conv_dw(d, x, kw):
    rows = lax.broadcasted_iota(jnp.int32, (kw, d.shape[1]), 0)
    dw = jnp.zeros((kw, d.shape[1]), F32)
    for j in range(kw):
        r = jnp.sum(d * _shift_down(x, kw - 1 - j), axis=0, keepdims=True)
        dw = jnp.where(rows == j, r, dw)
    return dw


def _silu(x):
    return x * jax.nn.sigmoid(x)


def _silu_grad(x):
    s = jax.nn.sigmoid(x)
    return s * (1.0 + x * (1.0 - s))


def gdn_conv_fwd(z, w, name, tc=256):
    T, C = z.shape[0], w.shape[1]
    kw = w.shape[0]

    def body(x_ref, w_ref, o_ref):
        o_ref[...] = _silu(_conv(x_ref[...], w_ref, kw))

    return pl.pallas_call(
        body, name=name, grid=(C // tc,),
        in_specs=[pl.BlockSpec((T, tc), lambda j: (0, j)), pl.BlockSpec((kw, tc), lambda j: (0, j))],
        out_specs=pl.BlockSpec((T, tc), lambda j: (0, j)),
        out_shape=jax.ShapeDtypeStruct((T, C), F32), compiler_params=_cparams(("parallel",)),
    )(z, w)


def gdn_conv_bwd(z, w, dy, name, tc=256):
    T, C = z.shape[0], w.shape[1]
    kw = w.shape[0]

    def body(x_ref, w_ref, dy_ref, dx_ref, dw_ref):
        xv = x_ref[...]
        dc = dy_ref[...] * _silu_grad(_conv(xv, w_ref, kw))
        dx_ref[...] = _conv_t(dc, w_ref, kw).astype(dx_ref.dtype)
        dw_ref[...] = _conv_dw(dc, xv, kw)

    col = lambda j: (0, j)
    return pl.pallas_call(
        body, name=name, grid=(C // tc,),
        in_specs=[pl.BlockSpec((T, tc), col), pl.BlockSpec((kw, tc), col), pl.BlockSpec((T, tc), col)],
        out_specs=[pl.BlockSpec((T, tc), col), pl.BlockSpec((kw, tc), col)],
        out_shape=[jax.ShapeDtypeStruct((T, C), BF16), jax.ShapeDtypeStruct((kw, C), F32)],
        compiler_params=_cparams(("parallel",)),
    )(z, w, dy)


def sc_fwd(z, w, name, tc=256):
    T, C = z.shape[0], w.shape[1]
    kw, nb = w.shape[0], C // tc

    def body(b_ref, c_ref, u_ref, w_ref, o_ref):
        o_ref[...] = (b_ref[...] * _conv(c_ref[...] * u_ref[...], w_ref, kw)).astype(o_ref.dtype)

    return pl.pallas_call(
        body, name=name, grid=(nb,),
        in_specs=[pl.BlockSpec((T, tc), lambda j: (0, j)), pl.BlockSpec((T, tc), lambda j: (0, nb + j)),
                  pl.BlockSpec((T, tc), lambda j: (0, 2 * nb + j)), pl.BlockSpec((kw, tc), lambda j: (0, j))],
        out_specs=pl.BlockSpec((T, tc), lambda j: (0, j)),
        out_shape=jax.ShapeDtypeStruct((T, C), BF16), compiler_params=_cparams(("parallel",)),
    )(z, z, z, w)


def sc_bwd(z, w, dy, name, tc=256):
    T, C = z.shape[0], w.shape[1]
    kw, nb = w.shape[0], C // tc

    def body(b_ref, c_ref, u_ref, w_ref, dy_ref, db_ref, dc_ref, du_ref, dw_ref):
        cv, uv, dyv = c_ref[...], u_ref[...], dy_ref[...]
        cu = cv * uv
        db_ref[...] = (dyv * _conv(cu, w_ref, kw)).astype(db_ref.dtype)
        dcv = dyv * b_ref[...]
        dcu = _conv_t(dcv, w_ref, kw)
        dc_ref[...] = (dcu * uv).astype(dc_ref.dtype)
        du_ref[...] = (dcu * cv).astype(du_ref.dtype)
        dw_ref[...] = _conv_dw(dcv, cu, kw)

    col = lambda j: (0, j)
    act = jax.ShapeDtypeStruct((T, C), BF16)
    return pl.pallas_call(
        body, name=name, grid=(nb,),
        in_specs=[pl.BlockSpec((T, tc), col), pl.BlockSpec((T, tc), lambda j: (0, nb + j)),
                  pl.BlockSpec((T, tc), lambda j: (0, 2 * nb + j)), pl.BlockSpec((kw, tc), col), pl.BlockSpec((T, tc), col)],
        out_specs=[pl.BlockSpec((T, tc), col)] * 3 + [pl.BlockSpec((kw, tc), col)],
        out_shape=[act, act, act, jax.ShapeDtypeStruct((kw, C), F32)],
        compiler_params=_cparams(("parallel",)),
    )(z, z, z, w, dy)


def _hdot(a, b, dims):
    a_hi, b_hi = a.astype(BF16), b.astype(BF16)
    a_lo, b_lo = (a - a_hi.astype(F32)).astype(BF16), (b - b_hi.astype(F32)).astype(BF16)
    dot = lambda x, y: lax.dot_general(x, y, (dims, ((), ())), preferred_element_type=F32)
    return dot(a_hi, b_hi) + (dot(a_hi, b_lo) + dot(a_lo, b_hi))


def _bdot(a, b, dims):
    return lax.dot_general(a.astype(BF16), b.astype(BF16), (dims, ((), ())), preferred_element_type=F32)


_NN, _NT, _TN = ((1,), (0,)), ((1,), (1,)), ((0,), (0,))


def _per_head_dots(dot2d):
    def stacked(a, b, dims):
        return jnp.stack([dot2d(a[h], b[h], dims) for h in range(a.shape[0])])

    @jax.custom_vjp
    def nn(a, b):
        return stacked(a, b, _NN)

    @jax.custom_vjp
    def nt(a, b):
        return stacked(a, b, _NT)

    @jax.custom_vjp
    def tn(a, b):
        return stacked(a, b, _TN)

    nn.defvjp(lambda a, b: (nn(a, b), (a, b)), lambda r, d: (stacked(d, r[1], _NT), stacked(r[0], d, _TN)))
    nt.defvjp(lambda a, b: (nt(a, b), (a, b)), lambda r, d: (stacked(d, r[1], _NN), stacked(d, r[0], _TN)))
    tn.defvjp(lambda a, b: (tn(a, b), (a, b)), lambda r, d: (stacked(r[1], d, _NT), stacked(r[0], d, _NN)))
    return nn, nt, tn


_hnn, _hnt, _htn = _per_head_dots(_hdot)
_bnn, _bnt, _btn = _per_head_dots(_bdot)


@jax.custom_vjp
def _unit_lower_inverse(m):
    c = m.shape[-1]
    eye = (lax.broadcasted_iota(jnp.int32, (c, c), 0) == lax.broadcasted_iota(jnp.int32, (c, c), 1)).astype(F32)
    t = eye - m
    p = _hnn(m, m)
    n = 2
    while n < c:
        t = t + _hnn(t, p)
        n *= 2
        if n < c:
            p = _hnn(p, p)
    return t


def _uli_fwd(m):
    t = _unit_lower_inverse(m)
    return t, t


def _uli_bwd(t, dt):
    return (-_htn(t, _hnt(dt, t)),)


_unit_lower_inverse.defvjp(_uli_fwd, _uli_bwd)


@jax.custom_vjp
def _known_inverse(m, t):
    return t


_known_inverse.defvjp(lambda m, t: (t, t), lambda t, dt: (_uli_bwd(t, dt)[0], jnp.zeros_like(t)))


def _gdn_chunk(q, k, v, gate, bl, al, a_log, dt_bias, o_norm, st, t_known=None):
    nh, c = q.shape[0], q.shape[1]
    ii = lax.broadcasted_iota(jnp.int32, (c, c), 0)
    jj = lax.broadcasted_iota(jnp.int32, (c, c), 1)
    tri, strict = ii >= jj, ii > jj
    q = q * lax.rsqrt(jnp.sum(q * q, -1, keepdims=True) + EPS) * (GDN_D ** -0.5)
    k = k * lax.rsqrt(jnp.sum(k * k, -1, keepdims=True) + EPS)
    beta = jax.nn.sigmoid(bl)
    g = -jnp.exp(a_log) * jax.nn.softplus(al + dt_bias)
    gc = _hnn(jnp.broadcast_to(tri.astype(F32), (nh, c, c)), g)
    gcol = _hnn(gc, jnp.full((nh, LANES, c), 1.0 / LANES, F32))
    grow = _hnt(jnp.full((nh, c, LANES), 1.0 / LANES, F32), gc)
    decay = jnp.where(tri, jnp.exp(jnp.where(tri, gcol - grow, 0.0)), 0.0)
    kb = k * beta
    m = jnp.where(strict, _bnt(kb, k) * decay, 0.0)
    t_inv = _unit_lower_inverse(m) if t_known is None else _known_inverse(m, t_known)
    eg = jnp.exp(gc)
    u = _bnn(t_inv, v * beta)
    w = _bnn(t_inv, kb * eg)
    attn = _bnt(q, k) * decay
    v_new = u - _bnn(w, st)
    o = _bnn(q * eg, st) + _bnn(attn, v_new)
    g_last = jnp.sum(g, axis=1, keepdims=True)
    st_new = st * jnp.exp(g_last) + _btn(k * jnp.exp(g_last - gc), v_new)
    o = o * lax.rsqrt(jnp.mean(o * o, -1, keepdims=True) + EPS) * o_norm
    return o * _silu(gate), st_new, t_inv


GDN_HP = 8
_GW = GDN_HP * GDN_D
_GB = GDN_H // GDN_HP


def _gdn_specs(n_chunks, rev):
    def tok(col):
        if rev:
            return pl.BlockSpec((GDN_C, _GW), lambda h, n: (n_chunks - 1 - n, col + h))
        return pl.BlockSpec((GDN_C, _GW), lambda h, n: (n, col + h))
    par = pl.BlockSpec((1, _GW), lambda h, n: (0, h))
    shared = pl.BlockSpec((1, GDN_D), lambda h, n: (0, 0))
    if rev:
        st = pl.BlockSpec((GDN_HP, None, GDN_D, GDN_D), lambda h, n: (h, n_chunks - 1 - n, 0, 0))
    else:
        st = pl.BlockSpec((GDN_HP, None, GDN_D, GDN_D), lambda h, n: (h, n, 0, 0))
    return tok, par, shared, st


def _heads(ref):
    return jnp.stack([ref[:, h * GDN_D:(h + 1) * GDN_D] for h in range(ref.shape[1] // GDN_D)])


def gdn_chunk_fwd(qkv, z, a_log_x, dt_bias_x, o_norm, name):
    T = qkv.shape[0]
    n_chunks = T // GDN_C
    H = GDN_H
    tok, par, shared, st_spec = _gdn_specs(n_chunks, False)

    def body(q_ref, k_ref, v_ref, g_ref, bl_ref, al_ref, a_ref, dt_ref, on_ref, o_ref, st_ref, ti_ref, state):
        @pl.when(pl.program_id(1) == 0)
        def _():
            state[...] = jnp.zeros_like(state)

        st = state[...]
        st_ref[...] = st
        o, st_new, t_inv = _gdn_chunk(_heads(q_ref), _heads(k_ref), _heads(v_ref), _heads(g_ref), _heads(bl_ref), _heads(al_ref),
                                      _heads(a_ref), _heads(dt_ref), on_ref[...], st)
        for hh in range(GDN_HP):
            o_ref[:, hh * GDN_D:(hh + 1) * GDN_D] = o[hh].astype(o_ref.dtype)
        ti_ref[...] = t_inv
        state[...] = st_new

    B = _GB
    return pl.pallas_call(
        body, name=name, grid=(B, n_chunks),
        in_specs=[tok(0), tok(B), tok(2 * B), tok(3 * B), tok(4 * B), tok(5 * B), par, par, shared],
        out_specs=[tok(0), st_spec, pl.BlockSpec((GDN_HP, None, GDN_C, GDN_C), lambda h, n: (h, n, 0, 0))],
        out_shape=[jax.ShapeDtypeStruct((T, H * GDN_D), BF16), jax.ShapeDtypeStruct((H, n_chunks, GDN_D, GDN_D), F32),
                   jax.ShapeDtypeStruct((H, n_chunks, GDN_C, GDN_C), F32)],
        scratch_shapes=[pltpu.VMEM((GDN_HP, GDN_D, GDN_D), F32)],
        compiler_params=_cparams(("parallel", "arbitrary")),
    )(qkv, qkv, qkv, z, z, z, a_log_x, dt_bias_x, o_norm)


def gdn_chunk_bwd(qkv, z, a_log_x, dt_bias_x, o_norm, states, t_invs, do, name):
    T = qkv.shape[0]
    n_chunks = T // GDN_C
    H = GDN_H
    tok, par, shared, st_spec = _gdn_specs(n_chunks, True)

    def body(q_ref, k_ref, v_ref, g_ref, bl_ref, al_ref, a_ref, dt_ref, on_ref, st_ref, ti_ref, do_ref,
             dqkv_ref, dg_ref, dba_ref, da_ref, ddt_ref, don_ref, dstate):
        h, n = pl.program_id(0), pl.program_id(1)

        @pl.when(n == 0)
        def _():
            dstate[...] = jnp.zeros_like(dstate)
            da_ref[...] = jnp.zeros_like(da_ref)
            ddt_ref[...] = jnp.zeros_like(ddt_ref)

        @pl.when((n == 0) & (h == 0))
        def _():
            don_ref[...] = jnp.zeros_like(don_ref)

        t_known = ti_ref[...]
        _, vjp = jax.vjp(lambda *ins: _gdn_chunk(*ins, t_known=t_known)[:2],
                         _heads(q_ref), _heads(k_ref), _heads(v_ref), _heads(g_ref), _heads(bl_ref), _heads(al_ref),
                         _heads(a_ref), _heads(dt_ref), on_ref[...], st_ref[...])
        dq, dk, dv, dg, dbl, dal, da, ddt, don, dst = vjp((_heads(do_ref).astype(F32), dstate[...]))
        lane = lax.broadcasted_iota(jnp.int32, (GDN_C, LANES), 1)
        dba = jnp.zeros((GDN_C, LANES), F32)
        for hh in range(GDN_HP):
            cols = slice(hh * GDN_D, (hh + 1) * GDN_D)
            for part, d in enumerate((dq, dk, dv)):
                dqkv_ref[:, part * H * GDN_D + hh * GDN_D:part * H * GDN_D + (hh + 1) * GDN_D] = d[hh]
            dg_ref[:, cols] = dg[hh].astype(dg_ref.dtype)
            dba = jnp.where(lane == hh, jnp.sum(dbl[hh], axis=-1, keepdims=True), dba)
            dba = jnp.where(lane == H + hh, jnp.sum(dal[hh], axis=-1, keepdims=True), dba)
            da_ref[:, cols] += da[hh]
            ddt_ref[:, cols] += ddt[hh]
        dba_ref[...] = dba.astype(dba_ref.dtype)
        don_ref[...] += don
        dstate[...] = dst

    tok0 = tok(0)
    B = _GB
    assert B == 1
    bf_tok = jax.ShapeDtypeStruct((T, H * GDN_D), BF16)
    par_sh = jax.ShapeDtypeStruct((1, H * GDN_D), F32)
    return pl.pallas_call(
        body, name=name, grid=(B, n_chunks),
        in_specs=[tok(0), tok(B), tok(2 * B), tok(3 * B), tok(4 * B), tok(5 * B), par, par, shared, st_spec,
                  pl.BlockSpec((GDN_HP, None, GDN_C, GDN_C), lambda h, n: (h, n_chunks - 1 - n, 0, 0)), tok0],
        out_specs=[pl.BlockSpec((GDN_C, 3 * H * GDN_D), lambda h, n: (n_chunks - 1 - n, 0)), tok0,
                   pl.BlockSpec((GDN_C, LANES), lambda h, n: (n_chunks - 1 - n, 0)), par, par, shared],
        out_shape=[jax.ShapeDtypeStruct((T, 3 * H * GDN_D), F32), bf_tok, jax.ShapeDtypeStruct((T, LANES), BF16), par_sh, par_sh,
                   jax.ShapeDtypeStruct((1, GDN_D), F32)],
        scratch_shapes=[pltpu.VMEM((GDN_HP, GDN_D, GDN_D), F32)],
        compiler_params=_cparams(("arbitrary", "arbitrary")),
    )(qkv, qkv, qkv, z, z, z, a_log_x, dt_bias_x, o_norm, states, t_invs, do)


def _prefetch_call(body, name, grid, in_specs, out_specs, out_shape, aliases=None):
    return pl.pallas_call(
        body, name=name,
        grid_spec=pltpu.PrefetchScalarGridSpec(num_scalar_prefetch=1, grid=grid, in_specs=in_specs, out_specs=out_specs),
        out_shape=out_shape, input_output_aliases=aliases or {},
        compiler_params=_cparams(("parallel",) * len(grid)))


def cast_into(src, src_row0, rows, slab, row0, me, name):
    width = src.shape[1]
    tr = _pick_rows(rows, 1024, row0, src_row0)
    assert rows % tr == 0 and row0 % tr == 0 and src_row0 % tr == 0

    def body(me_ref, s_ref, *refs):
        refs[-1][...] = s_ref[...].astype(refs[-1].dtype)

    in_specs = [pl.BlockSpec((tr, width), lambda r, me_ref: (src_row0 // tr + r, 0))]
    args = [src]
    aliases = {}
    if slab.arr is not None:
        in_specs.append(_ANY)
        args.append(slab.arr)
        aliases = {2: 0}
    slab.arr = _prefetch_call(
        body, name, (rows // tr,), in_specs,
        pl.BlockSpec((None, tr, width), lambda r, me_ref: (me_ref[0], row0 // tr + r, 0)),
        jax.ShapeDtypeStruct(slab.shape, slab.dtype), aliases)(me, *args)


def pair_add(g, b, c_idx, name):
    n, rh, w = b.shape
    tr = _pick_rows(rh, 1024)
    nb = rh // tr

    def body(c_ref, g_ref, b_ref, o_ref):
        o_ref[...] = (g_ref[...].astype(F32) + b_ref[...].astype(F32)).astype(o_ref.dtype)

    return _prefetch_call(
        body, name, (n, nb),
        [pl.BlockSpec((None, tr, w), lambda k, r, c: (k, c[0] * nb + r, 0)), pl.BlockSpec((None, tr, w), lambda k, r, c: (k, r, 0))],
        pl.BlockSpec((None, tr, w), lambda k, r, c: (k, r, 0)), jax.ShapeDtypeStruct(b.shape, BF16))(c_idx, g, b)


def chip_sum(p, rv, mc, name):
    n, rh, w = p.shape
    tr = _pick_rows(rh, 512)
    nb = rh // tr

    def body(mc_ref, p_ref, rv_ref, o_ref):
        me = mc_ref[0]
        acc = None
        for k in range(n):
            part = jnp.where(me == k, p_ref[...], rv_ref[k]).astype(F32)
            acc = part if acc is None else acc + part
        o_ref[...] = acc.astype(o_ref.dtype)

    return _prefetch_call(
        body, name, (nb,),
        [pl.BlockSpec((None, tr, w), lambda r, mc_ref: (mc_ref[0], r, 0)), pl.BlockSpec((n, tr, w), lambda r, mc_ref: (0, r, 0))],
        pl.BlockSpec((tr, w), lambda r, mc_ref: (mc_ref[1] * nb + r, 0)), jax.ShapeDtypeStruct((2 * rh, w), BF16))(mc, p, rv)


def adamw(red, row0, w, m, v, w_row0, rows, prev, name):
    cols = w.shape[1]
    tr = _pick_rows(rows, 512, row0, w_row0)
    assert rows % tr == 0 and row0 % tr == 0 and w_row0 % tr == 0

    def body(g_ref, w_ref, m_ref, v_ref, *refs):
        go_ref, d_ref, nm_ref, nv_ref = refs[-4:]
        gv = g_ref[...].astype(F32)
        nm = ADAM_B1 * m_ref[...] + (1.0 - ADAM_B1) * gv
        nv = ADAM_B2 * v_ref[...] + (1.0 - ADAM_B2) * (gv * gv)
        m_hat = nm / (1.0 - ADAM_B1 ** ADAM_STEP)
        v_hat = nv / (1.0 - ADAM_B2 ** ADAM_STEP)
        go_ref[...] = gv
        d_ref[...] = -ADAM_LR * (m_hat / (jnp.sqrt(v_hat) + ADAM_EPS) + ADAM_WD * w_ref[...])
        nm_ref[...] = nm
        nv_ref[...] = nv

    spec = pl.BlockSpec((tr, cols), lambda r: (w_row0 // tr + r, 0))
    sh = jax.ShapeDtypeStruct(w.shape, F32)
    in_specs = [pl.BlockSpec((tr, cols), lambda r: (row0 // tr + r, 0)), spec, spec, spec]
    args, aliases = [red, w, m, v], {}
    if prev is not None:
        in_specs += [_ANY] * 4
        args += list(prev)
        aliases = {4 + k: k for k in range(4)}
    return pl.pallas_call(
        body, name=name, grid=(rows // tr,), in_specs=in_specs, out_specs=[spec] * 4, out_shape=[sh] * 4,
        input_output_aliases=aliases, compiler_params=_cparams(("parallel",)),
    )(*args)


def _place():
    x, y, c = lax.axis_index("x"), lax.axis_index("y"), lax.axis_index("c")
    chips = [(1 - x, y), (x, 1 - y), (1 - x, 1 - y)]
    return x, y, c, chips


def _chip_index(cx, cy):
    return 2 * cx + cy


def _remote(src, dst, send_sem, recv_sem, to):
    return pltpu.make_async_remote_copy(src_ref=src, dst_ref=dst, send_sem=send_sem, recv_sem=recv_sem,
                                        device_id=to, device_id_type=MESH)


def _comm_call(body, name, ins, out_shapes, n_sems, aliases):
    return pl.pallas_call(
        body, name=name, in_specs=[_ANY] * len(ins), out_specs=[_ANY] * len(out_shapes), out_shape=out_shapes,
        scratch_shapes=[pltpu.SemaphoreType.DMA((n_sems,)), pltpu.SemaphoreType.DMA((n_sems,))],
        input_output_aliases=aliases,
    )(*ins)


def pair_swap_halves(slabs, name="grad_pair_swap"):
    n = len(slabs)

    def body(*refs):
        in_refs, out_refs, send_sems, recv_sems = refs[:n], refs[n:2 * n], refs[-2], refs[-1]
        x, y, c, _ = _place()
        cps = []
        for a in range(n):
            rh = in_refs[a].shape[1] // 2
            cp = _remote(in_refs[a].at[:, pl.ds((1 - c) * rh, rh), :], out_refs[a], send_sems.at[a], recv_sems.at[a], (x, y, 1 - c))
            cp.start()
            cps.append(cp)
        for cp in cps:
            cp.wait()

    outs = [jax.ShapeDtypeStruct((s.shape[0], s.shape[1] // 2, s.shape[2]), s.dtype) for s in slabs]
    return _comm_call(body, name, slabs, outs, n, {})


def pair_join_halves(reds, name="grad_pair_join"):
    n = len(reds)

    def body(*refs):
        in_refs, out_refs, send_sems, recv_sems = refs[:n], refs[n:2 * n], refs[-2], refs[-1]
        x, y, c, _ = _place()
        cps = []
        for a in range(n):
            rh = in_refs[a].shape[0] // 2
            mine = pl.ds(c * rh, rh)
            cp = _remote(in_refs[a].at[mine], out_refs[a].at[mine], send_sems.at[a], recv_sems.at[a], (x, y, 1 - c))
            cp.start()
            cps.append(cp)
        for a in range(n):
            rh = in_refs[a].shape[0] // 2
            got = out_refs[a].at[pl.ds((1 - c) * rh, rh)]
            _remote(got, got, send_sems.at[a], recv_sems.at[a], (x, y, 1 - c)).wait_recv()
        for cp in cps:
            cp.wait_send()

    return _comm_call(body, name, reds, [jax.ShapeDtypeStruct(r.shape, r.dtype) for r in reds], n, {a: a for a in range(n)})


_HBM = pl.BlockSpec(memory_space=pltpu.HBM)
_SEM = pl.BlockSpec(memory_space=pltpu.SEMAPHORE)
_EFFECT = pltpu.SideEffectType.DATAFLOW_SIDE_EFFECTING


def _in_hbm(a):
    return pltpu.with_memory_space_constraint(a, pltpu.HBM)


def _hbm_like(a):
    return pltpu.HBM(a.shape, a.dtype)


def _start_call(body, name, ins, n_sems, after):
    n = len(ins)
    res = pl.pallas_call(
        body, name=name, in_specs=[_HBM] * n + [_ANY],
        out_specs=[_SEM, _SEM] + [_HBM] * n + [pl.BlockSpec(memory_space=pltpu.VMEM)],
        out_shape=[pltpu.SemaphoreType.DMA((n_sems,)), pltpu.SemaphoreType.DMA((n_sems,))] + [_hbm_like(a) for a in ins]
        + [jax.ShapeDtypeStruct((8, LANES), F32)],
        input_output_aliases={a: 2 + a for a in range(n)},
        compiler_params=pltpu.CompilerParams(has_side_effects=_EFFECT),
    )(*[_in_hbm(a) for a in ins], after)
    return res[0], res[1], list(res[2:2 + n]), res[-1]


def _wait_call(body, name, thru, send_sems, recv_sems, after):
    n = len(thru)
    after = list(after) if isinstance(after, (list, tuple)) else [after]
    return pl.pallas_call(
        body, name=name, in_specs=[_HBM] * n + [_SEM, _SEM] + [_ANY] * len(after), out_specs=[_HBM] * n,
        out_shape=[_hbm_like(a) for a in thru], input_output_aliases={a: a for a in range(n)},
        compiler_params=pltpu.CompilerParams(has_side_effects=_EFFECT),
    )(*thru, send_sems, recv_sems, *after)


def gather_start(slabs, after, name="weight_gather_start"):
    n = len(slabs)

    def body(*refs):
        g_refs, send_sems, recv_sems, token = refs[:n], refs[n + 1], refs[n + 2], refs[-1]
        x, y, c, chips = _place()
        me = _chip_index(x, y)
        for a in range(n):
            rh = g_refs[a].shape[1] // 2
            mine = g_refs[a].at[me, pl.ds(c * rh, rh)]
            for j, chip in enumerate(chips):
                _remote(mine, mine, send_sems.at[3 * a + j], recv_sems.at[3 * a + j], (*chip, c)).start()
        token[...] = jnp.zeros_like(token)

    return _start_call(body, name, slabs, 3 * n, after)


def gather_wait(send_sems, recv_sems, thru, after, name="weight_gather_wait"):
    n = len(thru)

    def body(*refs):
        g_refs, send_sems, recv_sems = refs[:n], refs[n], refs[n + 1]
        x, y, c, chips = _place()
        me = _chip_index(x, y)
        for a in range(n):
            rh = g_refs[a].shape[1] // 2
            rows = pl.ds(c * rh, rh)
            for j, chip in enumerate(chips):
                mine, got = g_refs[a].at[me, rows], g_refs[a].at[_chip_index(*chip), rows]
                _remote(mine, mine, send_sems.at[3 * a + j], recv_sems.at[3 * a + j], (*chip, c)).wait_send()
                _remote(got, got, send_sems.at[3 * a + j], recv_sems.at[3 * a + j], (*chip, c)).wait_recv()

    return _wait_call(body, name, thru, send_sems, recv_sems, after)


def gather_forward(slabs, name="weight_gather_forward"):
    n = len(slabs)

    def body(*refs):
        in_refs, out_refs, send_sems, recv_sems = refs[:n], refs[n:2 * n], refs[-2], refs[-1]
        x, y, c, chips = _place()
        sib = (x, y, 1 - c)
        sends = []
        for a in range(n):
            rh = in_refs[a].shape[1] // 2
            for j, chip in enumerate(chips):
                k = _chip_index(*chip)
                cp = _remote(in_refs[a].at[k, pl.ds(c * rh, rh)], out_refs[a].at[k, pl.ds(c * rh, rh)], send_sems.at[3 * a + j],
                             recv_sems.at[3 * a + j], sib)
                cp.start()
                sends.append(cp)
        for a in range(n):
            rh = in_refs[a].shape[1] // 2
            for j, chip in enumerate(chips):
                got = out_refs[a].at[_chip_index(*chip), pl.ds((1 - c) * rh, rh)]
                _remote(got, got, send_sems.at[3 * a + j], recv_sems.at[3 * a + j], sib).wait_recv()
        for cp in sends:
            cp.wait_send()

    return _comm_call(body, name, slabs, [jax.ShapeDtypeStruct(s.shape, s.dtype) for s in slabs], 3 * n, {a: a for a in range(n)})


def exchange_start(parts, after, name="grad_exchange_start"):
    n = len(parts)

    def body(*refs):
        p_refs, land_refs, send_sems, recv_sems, token = refs[:n], refs[n:2 * n], refs[2 * n + 1], refs[2 * n + 2], refs[-1]
        x, y, c, chips = _place()
        me = _chip_index(x, y)
        for a in range(n):
            for j, chip in enumerate(chips):
                _remote(p_refs[a].at[_chip_index(*chip)], land_refs[a].at[me], send_sems.at[3 * a + j], recv_sems.at[3 * a + j],
                        (*chip, c)).start()
        token[...] = jnp.zeros_like(token)

    return _start_call(body, name, list(parts) + [lax.empty(p.shape, p.dtype) for p in parts], 3 * n, after)


def swap_start(slabs, after, name="grad_swap_start"):
    n = len(slabs)

    def body(*refs):
        g_refs, land_refs, send_sems, recv_sems, token = refs[:n], refs[n:2 * n], refs[2 * n + 1], refs[2 * n + 2], refs[-1]
        x, y, c, _ = _place()
        for a in range(n):
            rh = g_refs[a].shape[1] // 2
            _remote(g_refs[a].at[:, pl.ds((1 - c) * rh, rh), :], land_refs[a], send_sems.at[a], recv_sems.at[a], (x, y, 1 - c)).start()
        token[...] = jnp.zeros_like(token)

    lands = [lax.empty((s.shape[0], s.shape[1] // 2, s.shape[2]), s.dtype) for s in slabs]
    return _start_call(body, name, list(slabs) + lands, n, after)


def swap_wait(send_sems, recv_sems, thru, after, name="grad_swap_wait"):
    n = len(thru) // 2

    def body(*refs):
        g_refs, land_refs, send_sems, recv_sems = refs[:n], refs[n:2 * n], refs[2 * n], refs[2 * n + 1]
        x, y, c, _ = _place()
        for a in range(n):
            rh = g_refs[a].shape[1] // 2
            cp = _remote(g_refs[a].at[:, pl.ds((1 - c) * rh, rh), :], land_refs[a], send_sems.at[a], recv_sems.at[a], (x, y, 1 - c))
            cp.wait_send()
            cp.wait_recv()

    res = _wait_call(body, name, thru, send_sems, recv_sems, after)
    return res[:n], res[n:]


def exchange_wait(send_sems, recv_sems, thru, after, name="grad_exchange_wait"):
    n = len(thru) // 2

    def body(*refs):
        p_refs, land_refs, send_sems, recv_sems = refs[:n], refs[n:2 * n], refs[2 * n], refs[2 * n + 1]
        x, y, c, chips = _place()
        me = _chip_index(x, y)
        for a in range(n):
            for j, chip in enumerate(chips):
                k = _chip_index(*chip)
                _remote(p_refs[a].at[k], land_refs[a].at[me], send_sems.at[3 * a + j], recv_sems.at[3 * a + j], (*chip, c)).wait_send()
                _remote(land_refs[a].at[k], land_refs[a].at[k], send_sems.at[3 * a + j], recv_sems.at[3 * a + j], (*chip, c)).wait_recv()

    res = _wait_call(body, name, thru, send_sems, recv_sems, after)
    return res[:n], res[n:]


_SLABS = {
    "mla_w_in": [("mla_w_in", 1, 0, 2)], "mla_w_uq": [("mla_w_uq", 2, 0, 2)], "mla_w_ukv": [("mla_w_ukv", 2, 0, 2)],
    "l0_mla_w_o": [("mla_w_o", 1, 0, 1)],
    "l0_w1024": [("mlp_w1", 2, 0, 1), ("mlp_w2", 1, 0, 1), ("xa_w_q", 1, 0, 1), ("xa_w_o", 1, 0, 1)],
    "l0_xa_w_kv": [("xa_w_kv", 2, 0, 1)],
    "l1_w1024": [("mlp_w1", 2, 1, 2), ("mlp_w2", 1, 1, 2), ("xa_w_q", 1, 1, 2), ("xa_w_o", 1, 1, 2), ("gdn_w_o", 1, 0, 1)],
    "l1_xa_w_kv": [("xa_w_kv", 2, 1, 2)], "gdn_w_in": [("gdn_w_in", 2, 0, 1)],
    "l23_w1024": [("mlp_w1", 2, 2, 4), ("mlp_w2", 1, 2, 4), ("xa_w_q", 1, 2, 4), ("xa_w_o", 1, 2, 4), ("mla_w_o", 1, 1, 2),
                  ("sc_w_o", 1, 0, 1)],
    "l23_xa_w_kv": [("xa_w_kv", 2, 2, 4)], "sc_w_in": [("sc_w_in", 2, 0, 1)],
}
_GROUPS = [(["mla_w_in", "mla_w_uq", "mla_w_ukv", "l0_mla_w_o"], None),
           (["l0_w1024", "l0_xa_w_kv"], (0, "xa")),
           (["l1_w1024", "l1_xa_w_kv", "gdn_w_in"], (1, "mix")),
           (["l23_w1024", "l23_xa_w_kv", "sc_w_in"], (2, "mix"))]
_SWAP_DONE = {(2, "mix"): (1, "mlp"), (1, "mix"): (0, "mlp")}
_RELAID = ("mla_w_in", "mla_w_uq", "mla_w_ukv", "gdn_w_in")
_SMALL = [("mla_q_norm", 1), ("mla_kv_norm", 1), ("gdn_conv_w", 2), ("sc_conv_w", 2)]
_REPL = ["gdn_a_log", "gdn_dt_bias", "gdn_o_norm", "norm_mix", "norm_mem", "norm_mlp", "mem_norm", "final_norm"]
_WEIGHTS = ['mla_w_in', 'mla_q_norm', 'mla_kv_norm', 'mla_w_uq', 'mla_w_ukv', 'mla_w_o', 'gdn_w_in', 'gdn_conv_w',
            'gdn_a_log', 'gdn_dt_bias', 'gdn_o_norm', 'gdn_w_o', 'sc_w_in', 'sc_conv_w', 'sc_w_o', 'norm_mix',
            'norm_mem', 'norm_mlp', 'xa_w_q', 'xa_w_kv', 'xa_w_o', 'mlp_w1', 'mlp_w2', 'mem_norm', 'final_norm']


class Layout:
    def __init__(self, shard_shapes):
        self.members, self.where, self.slab_dims = {}, {}, {}
        for slab, members in _SLABS.items():
            off, rows = 0, []
            for name, axis, l0, l1 in members:
                _, rpl, width = shard_shapes[name]
                rows.append((name, off, l0, l1, rpl))
                for layer in range(l0, l1):
                    self.where[(name, layer)] = (slab, off + (layer - l0) * rpl, rpl, width, axis)
                off += (l1 - l0) * rpl
            self.members[slab], self.slab_dims[slab] = rows, (off, width)

    def new_slabs(self, dtype):
        return {s: Slab(rows, width, dtype) for s, (rows, width) in self.slab_dims.items()}

    def loc(self, slabs, name, layer):
        slab, row0, rpl, width, axis = self.where[(name, layer)]
        if axis == 1:
            return Loc(slabs[slab], row0, N_CHIPS * rpl, width, 0)
        return Loc(slabs[slab], row0, rpl, N_CHIPS * width, 1)

    def _whole(self, name):
        (member,) = self.members[name]
        _, off, l0, l1, rpl = member
        assert off == 0 and l0 == 0
        return l1, rpl, self.slab_dims[name][1], dict((n, a) for n, a, _, _ in _SLABS[name])[name]

    def full(self, slabs, name):
        layers, rpl, width, axis = self._whole(name)
        blocks = slabs[name].arr.reshape(N_CHIPS, layers, rpl, width)
        return jnp.concatenate([blocks[s] for s in range(N_CHIPS)], axis=axis)

    def put_full(self, slabs, name, grad):
        layers, rpl, width, axis = self._whole(name)
        parts = jnp.stack(jnp.split(grad, N_CHIPS, axis=axis)).reshape(N_CHIPS, layers * rpl, width)
        slabs[name].arr = parts.astype(slabs[name].dtype)


def _small_pack(vals, names):
    flat = jnp.concatenate([vals[n].astype(F32).reshape(-1) for n in names])
    return jnp.pad(flat, (0, SMALL_ROWS * SMALL_COLS - flat.shape[0])).reshape(SMALL_ROWS, SMALL_COLS)


def _small_unpack(flat, like, names):
    out, off = {}, 0
    flat = flat.reshape(-1)
    for n in names:
        out[n] = flat[off:off + like[n].size].reshape(like[n].shape)
        off += like[n].size
    return out


_MLA_CFG = _Attn(MLA_H, 2 * LANES, MLA_NOPE, MLA_V, True, (MLA_NOPE + MLA_ROPE) ** -0.5, hp=8, hp_kv=8)
_XA_CFG = _Attn(XA_H, XA_D, XA_D, XA_D, False, XA_D ** -0.5, hp=4, hp_kv=4)


def _mla_weights(w_in, w_uq, w_ukv):
    w_in_p = jnp.pad(w_in, ((0, 0), (0, MLA_ZPAD - w_in.shape[1])))
    w_uq_p = jnp.pad(w_uq.reshape(MLA_QR, MLA_H, MLA_NOPE + MLA_ROPE), ((0, 0), (0, 0), (0, 2 * LANES - MLA_NOPE - MLA_ROPE)))
    w_uq_p = w_uq_p.reshape(MLA_QR, MLA_H * 2 * LANES)
    kv = w_ukv.reshape(MLA_KVR, MLA_H, MLA_NOPE + MLA_V)
    w_ukv_p = jnp.concatenate([kv[:, :, :MLA_NOPE].reshape(MLA_KVR, -1), kv[:, :, MLA_NOPE:].reshape(MLA_KVR, -1)], axis=1)
    return w_in_p, w_uq_p, w_ukv_p


def _mla_weight_grads(d_in_p, d_uq_p, d_ukv_p):
    d_in = d_in_p[:, :MLA_QR + MLA_KVR + MLA_ROPE]
    d_uq = d_uq_p.reshape(MLA_QR, MLA_H, 2 * LANES)[:, :, :MLA_NOPE + MLA_ROPE].reshape(MLA_QR, -1)
    half = MLA_H * MLA_NOPE
    d_ukv = jnp.concatenate([d_ukv_p[:, :half].reshape(MLA_KVR, MLA_H, MLA_NOPE),
                             d_ukv_p[:, half:].reshape(MLA_KVR, MLA_H, MLA_V)], axis=2).reshape(MLA_KVR, -1)
    return d_in, d_uq, d_ukv


def _mla_fwd(xs, h, wts, w_o, qn, kvn, tabs, g_next, tag):
    w_in_p, w_uq_p, w_ukv_p = wts
    z = mm(h, w_in_p, "nn", f"{tag}_in")
    cq, ckv, kr = mla_mid_fwd(z, qn, kvn, tabs, f"{tag}_mid")
    q = mm(cq, w_uq_p, "nn", f"{tag}_uq", outs=(BF16,), epi=_epi_rope_q, per_row=tabs, tm=512)
    kv = mm(ckv, w_ukv_p, "nn", f"{tag}_ukv", outs=(BF16,))
    o, lse = flash_fwd(_MLA_CFG, q, kv, kv, kr, f"{tag}_attn")
    xs, h_next = residual_norm(o, w_o, xs, g_next, f"{tag}_out")
    return xs, h_next, (z, cq, ckv, kr, q, kv, o, lse)


def _mla_bwd(dx, h, wts, w_o, g_wo, qn, kvn, tabs, saved, tag):
    w_in_p, w_uq_p, w_ukv_p = wts
    z, cq, ckv, kr, q, kv, o, lse = saved
    mm(o, dx, "tn", f"{tag}_dwo", outs=(BF16,), out_loc=g_wo)
    do = mm(dx, w_o, "nt", f"{tag}_do", outs=(BF16,))
    dqp, delta = flash_dq(_MLA_CFG, q, kv, kv, kr, o, do, lse, BF16, f"{tag}_attn_dq", rope_tabs=tabs)
    dkv, dkr = flash_dkv(_MLA_CFG, q, kv, kv, kr, do, lse, delta, BF16, f"{tag}_attn_dkv")
    d_uq_p = mm(cq, dqp, "tn", f"{tag}_duq")
    dcq = mm(dqp, w_uq_p, "nt", f"{tag}_dcq")
    d_ukv_p = mm(ckv, dkv, "tn", f"{tag}_dukv")
    dckv = mm(dkv, w_ukv_p, "nt", f"{tag}_dckv")
    dz, dqn, dkvn = mla_mid_bwd(z, qn, kvn, tabs, dcq, dckv, dkr, f"{tag}_mid_bwd")
    d_in_p = mm(h, dz, "tn", f"{tag}_din")
    dh = (dz, w_in_p)
    d_in, d_uq, d_ukv = _mla_weight_grads(d_in_p, d_uq_p, d_ukv_p)
    return dh, dict(mla_w_in=d_in, mla_w_uq=d_uq, mla_w_ukv=d_ukv, mla_q_norm=dqn, mla_kv_norm=dkvn)


_GDN_QKV = 3 * GDN_H * GDN_D
_GDN_GATE_END = _GDN_QKV + GDN_H * GDN_D


def _gdn_weights(w_in):
    rep = lambda cols: jnp.repeat(cols, GDN_D, axis=1)
    return jnp.concatenate([w_in[:, :_GDN_GATE_END], rep(w_in[:, _GDN_GATE_END:_GDN_GATE_END + GDN_H]),
                            rep(w_in[:, _GDN_GATE_END + GDN_H:])], axis=1)


def _fold(x):
    return x.reshape(x.shape[0], -1, GDN_D).sum(-1)


def _gdn_fwd(xs, h, w_in_x, conv_w, a_log, dt_bias, o_norm, w_o, g_next, tag):
    z = mm(h, w_in_x, "nn", f"{tag}_in")
    qkv = gdn_conv_fwd(z, conv_w, f"{tag}_conv")
    a_x, dt_x = jnp.repeat(a_log.reshape(1, -1), GDN_D, axis=1), jnp.repeat(dt_bias.reshape(1, -1), GDN_D, axis=1)
    og, states, t_invs = gdn_chunk_fwd(qkv, z, a_x, dt_x, o_norm.reshape(1, -1), f"{tag}_chunks")
    xs, h_next = residual_norm(og, w_o, xs, g_next, f"{tag}_out")
    return xs, h_next, (z, qkv, a_x, dt_x, og, states, t_invs)


def _gdn_weights_compact(w_in):
    return jnp.pad(w_in, ((0, 0), (0, LANES - 2 * GDN_H)))


def _gdn_bwd(dx, h, w_in_c, conv_w, o_norm, w_o, g_wo, saved, tag):
    z, qkv, a_x, dt_x, og, states, t_invs = saved
    mm(og, dx, "tn", f"{tag}_dwo", outs=(BF16,), out_loc=g_wo)
    dog = mm(dx, w_o, "nt", f"{tag}_dog")
    dqkv, dgate, dba, da_x, ddt_x, don = gdn_chunk_bwd(qkv, z, a_x, dt_x, o_norm.reshape(1, -1), states, t_invs, dog,
                                                       f"{tag}_chunks_bwd")
    dpre, dconv = gdn_conv_bwd(z, conv_w, dqkv, f"{tag}_conv_bwd")
    dz = jnp.concatenate([dpre, dgate, dba], axis=1)
    d_in_c = mm(h, dz, "tn", f"{tag}_din")
    dh = (dz, w_in_c)
    return dh, dict(gdn_w_in=d_in_c[:, :_GDN_GATE_END + 2 * GDN_H], gdn_conv_w=dconv, gdn_a_log=_fold(da_x).reshape(-1),
                    gdn_dt_bias=_fold(ddt_x).reshape(-1), gdn_o_norm=don.reshape(-1))


def _sc_fwd(xs, h, w_in, conv_w, w_o, g_next, tag):
    z = mm(h, w_in, "nn", f"{tag}_in")
    y = sc_fwd(z, conv_w, f"{tag}_conv")
    xs, h_next = residual_norm(y, w_o, xs, g_next, f"{tag}_out")
    return xs, h_next, (z, y)


def _sc_bwd(dx, h, w_in, g_win, conv_w, w_o, g_wo, saved, tag):
    z, y = saved
    mm(y, dx, "tn", f"{tag}_dwo", outs=(BF16,), out_loc=g_wo)
    dy = mm(dx, w_o, "nt", f"{tag}_dy")
    db, dc, du, dconv = sc_bwd(z, conv_w, dy, f"{tag}_conv_bwd")
    dz = jnp.concatenate([db, dc, du], axis=1)
    mm(h, dz, "tn", f"{tag}_din", outs=(BF16,), out_loc=g_win)
    dh = (dz, w_in)
    return dh, dict(sc_conv_w=dconv)


def local_step(x, mem, pos, target, lay, wslabs, gslabs, small, before=None, after_bwd=None):
    depth = small["norm_mix"].shape[0]
    W = lambda name, layer: lay.loc(wslabs, name, layer)
    G = lambda name, layer: lay.loc(gslabs, name, layer)
    tabs = rope_tables(pos)
    mem_n = rmsnorm_fwd(mem, small["mem_norm"], "mem_norm")
    full = {n: lay.full(wslabs, n) for n in ("mla_w_in", "mla_w_uq", "mla_w_ukv")}
    mla_w = [_mla_weights(full["mla_w_in"][j], full["mla_w_uq"][j], full["mla_w_ukv"][j]) for j in range(full["mla_w_in"].shape[0])]
    gdn_in_x, gdn_in_c = {}, {}

    xs, h_pre = x, None
    saved = []
    for i in range(depth):
        j, kind = i // 3, i % 3
        tag = f"l{i}"
        if before is not None:
            xs = before(i, "mix", xs)
        if kind == 1:
            gdn_full = lay.full(wslabs, "gdn_w_in")[j]
            gdn_in_x[j], gdn_in_c[j] = _gdn_weights(gdn_full), _gdn_weights_compact(gdn_full)
        x_a = xs
        h = h_pre if h_pre is not None else rmsnorm_fwd(xs, small["norm_mix"][i], f"{tag}_norm_mix")
        g_mem = small["norm_mem"][i]
        if kind == 0:
            xs, hn, mix = _mla_fwd(xs, h, mla_w[j], W("mla_w_o", j), small["mla_q_norm"][j], small["mla_kv_norm"][j], tabs, g_mem,
                                   f"{tag}_mla")
        elif kind == 1:
            xs, hn, mix = _gdn_fwd(xs, h, gdn_in_x[j], small["gdn_conv_w"][j], small["gdn_a_log"][j], small["gdn_dt_bias"][j],
                                   small["gdn_o_norm"][j], W("gdn_w_o", j), g_mem, f"{tag}_gdn")
        else:
            xs, hn, mix = _sc_fwd(xs, h, W("sc_w_in", j), small["sc_conv_w"][j], W("sc_w_o", j), g_mem, f"{tag}_sc")
        if before is not None:
            xs = before(i, "xa", xs)
        x_b = xs
        xq = mm(hn, W("xa_w_q", i), "nn", f"{tag}_xa_q", outs=(BF16,))
        xkv = mm(mem_n, W("xa_w_kv", i), "nn", f"{tag}_xa_kv", outs=(BF16,))
        xo, xlse = flash_fwd(_XA_CFG, xq, xkv, xkv, None, f"{tag}_xa_attn")
        xs, hm = residual_norm(xo, W("xa_w_o", i), xs, small["norm_mlp"][i], f"{tag}_xa_out")
        x_c = xs
        h1, act = mm(hm, W("mlp_w1", i), "nn", f"{tag}_mlp_up", outs=(BF16, BF16), epi=_epi_relu2)
        xs, h_pre = residual_norm(act, W("mlp_w2", i), xs, small["norm_mix"][i + 1] if i + 1 < depth else None,
                                  f"{tag}_mlp_down", tm=512)
        saved.append((x_a, h, mix, x_b, hn, xq, xkv, xo, xlse, x_c, hm, h1, act))

    se, dx, d_final = loss_head(xs, small["final_norm"], target)

    per_layer = {n: [None] * depth for n in ("norm_mix", "norm_mem", "norm_mlp")}
    mixer = {}
    dmem_n = jnp.zeros(mem.shape, F32)
    for i in reversed(range(depth)):
        j, kind = i // 3, i % 3
        tag = f"l{i}"
        x_a, h, mix, x_b, hn, xq, xkv, xo, xlse, x_c, hm, h1, act = saved[i]
        mm(act, dx, "tn", f"{tag}_mlp_dw2", outs=(BF16,), out_loc=G("mlp_w2", i))
        dh1 = mm(dx, W("mlp_w2", i), "nt", f"{tag}_mlp_dh1", outs=(BF16,), epi=_epi_relu2_bwd, extras=(h1,))
        mm(hm, dh1, "tn", f"{tag}_mlp_dw1", outs=(BF16,), out_loc=G("mlp_w1", i))
        dx, dg = mm(dh1, W("mlp_w1", i), "nt", f"{tag}_mlp_dhm", epi=_epi_norm_bwd, extras=(x_c, dx), vecs=(small["norm_mlp"][i],),
                    row_outs=1, tm=512)
        per_layer["norm_mlp"][i] = dg.reshape(-1)
        if after_bwd is not None:
            dx = after_bwd(i, "mlp", dx)
        mm(xo, dx, "tn", f"{tag}_xa_dwo", outs=(BF16,), out_loc=G("xa_w_o", i))
        dxo = mm(dx, W("xa_w_o", i), "nt", f"{tag}_xa_do", outs=(BF16,))
        dxq, xdelta = flash_dq(_XA_CFG, xq, xkv, xkv, None, xo, dxo, xlse, BF16, f"{tag}_xa_attn_dq")
        (dxkv,) = flash_dkv(_XA_CFG, xq, xkv, xkv, None, dxo, xlse, xdelta, BF16, f"{tag}_xa_attn_dkv")
        mm(hn, dxq, "tn", f"{tag}_xa_dwq", outs=(BF16,), out_loc=G("xa_w_q", i))
        dx, dg = mm(dxq, W("xa_w_q", i), "nt", f"{tag}_xa_dhn", epi=_epi_norm_bwd, extras=(x_b, dx), vecs=(small["norm_mem"][i],),
                    row_outs=1, tm=512)
        per_layer["norm_mem"][i] = dg.reshape(-1)
        mm(mem_n, dxkv, "tn", f"{tag}_xa_dwkv", outs=(BF16,), out_loc=G("xa_w_kv", i))
        dmem_n = mm(dxkv, W("xa_w_kv", i), "nt", f"{tag}_xa_dmem", epi=_epi_add, extras=(dmem_n,))
        if after_bwd is not None:
            dx = after_bwd(i, "xa", dx)
        if kind == 0:
            dh, gr = _mla_bwd(dx, h, mla_w[j], W("mla_w_o", j), G("mla_w_o", j), small["mla_q_norm"][j], small["mla_kv_norm"][j],
                              tabs, mix, f"{tag}_mla")
        elif kind == 1:
            dh, gr = _gdn_bwd(dx, h, gdn_in_c[j], small["gdn_conv_w"][j], small["gdn_o_norm"][j], W("gdn_w_o", j), G("gdn_w_o", j),
                              mix, f"{tag}_gdn")
        else:
            dh, gr = _sc_bwd(dx, h, W("sc_w_in", j), G("sc_w_in", j), small["sc_conv_w"][j], W("sc_w_o", j), G("sc_w_o", j),
                             mix, f"{tag}_sc")
        if kind == 1:
            lay.put_full(gslabs, "gdn_w_in", gr.pop("gdn_w_in")[None])
        for n, g in gr.items():
            mixer.setdefault(n, {})[j] = g
        dz_mix, w_mix = dh
        dx, dg = mm(dz_mix, w_mix, "nt", f"{tag}_mix_dh", epi=_epi_norm_bwd, extras=(x_a, dx), vecs=(small["norm_mix"][i],),
                    row_outs=1, tm=256 if kind == 1 else 512)
        per_layer["norm_mix"][i] = dg.reshape(-1)
        if after_bwd is not None:
            dx = after_bwd(i, "mix", dx)

    _, d_mem_norm = rmsnorm_bwd(mem, small["mem_norm"], dmem_n, jnp.zeros(mem.shape, F32), "mem_norm_bwd")
    grads = {n: jnp.stack(v) for n, v in per_layer.items()}
    for n, by_j in mixer.items():
        grads[n] = jnp.stack([by_j[j] for j in sorted(by_j)])
    grads["mem_norm"] = d_mem_norm
    grads["final_norm"] = d_final
    for n in ("mla_w_in", "mla_w_uq", "mla_w_ukv"):
        lay.put_full(gslabs, n, grads.pop(n))
    return se, dx, grads


def kernel(x, mem, positions, mla_w_in, mla_q_norm, mla_kv_norm, mla_w_uq, mla_w_ukv, mla_w_o, gdn_w_in, gdn_conv_w, gdn_a_log, gdn_dt_bias, gdn_o_norm, gdn_w_o, sc_w_in, sc_conv_w, sc_w_o, norm_mix, norm_mem, norm_mlp, xa_w_q, xa_w_kv, xa_w_o, mlp_w1, mlp_w2, mem_norm, final_norm, loss_target, m_mla_w_in, m_mla_q_norm, m_mla_kv_norm, m_mla_w_uq, m_mla_w_ukv, m_mla_w_o, m_gdn_w_in, m_gdn_conv_w, m_gdn_a_log, m_gdn_dt_bias, m_gdn_o_norm, m_gdn_w_o, m_sc_w_in, m_sc_conv_w, m_sc_w_o, m_norm_mix, m_norm_mem, m_norm_mlp, m_xa_w_q, m_xa_w_kv, m_xa_w_o, m_mlp_w1, m_mlp_w2, m_mem_norm, m_final_norm, v_mla_w_in, v_mla_q_norm, v_mla_kv_norm, v_mla_w_uq, v_mla_w_ukv, v_mla_w_o, v_gdn_w_in, v_gdn_conv_w, v_gdn_a_log, v_gdn_dt_bias, v_gdn_o_norm, v_gdn_w_o, v_sc_w_in, v_sc_conv_w, v_sc_w_o, v_norm_mix, v_norm_mem, v_norm_mlp, v_xa_w_q, v_xa_w_kv, v_xa_w_o, v_mlp_w1, v_mlp_w2, v_mem_norm, v_final_norm):
    given = dict(locals())
    p = {n: given[n] for n in _WEIGHTS}
    mom = {n: given["m_" + n] for n in _WEIGHTS}
    var = {n: given["v_" + n] for n in _WEIGHTS}
    split = sorted({n for members in _SLABS.values() for n, _, _, _ in members})
    lay = Layout({n: p[n].shape for n in split})
    flat2d = lambda a: a.reshape(-1, a.shape[-1])

    me = (2 * lax.axis_index("x") + lax.axis_index("y")).astype(jnp.int32)
    core = lax.axis_index("c").astype(jnp.int32)
    me1, c1, mc = me.reshape(1), core.reshape(1), jnp.stack([me, core])

    wslabs = lay.new_slabs(BF16)

    def cast_group(slabs, chip):
        for slab in slabs:
            for name, off, l0, l1, rpl in lay.members[slab]:
                cast_into(flat2d(p[name]), l0 * rpl, (l1 - l0) * rpl, wslabs[slab], off, chip, f"cast_{slab}_{name}")

    first = _GROUPS[0][0]
    cast_group(first, me1)
    small_names = [n for n, _ in _SMALL]
    words = lax.bitcast_convert_type(jnp.concatenate([p[n].reshape(-1) for n in small_names]), BF16).reshape(-1)
    words = jnp.pad(words, (0, SMALL_ROWS * SMALL_COLS - words.shape[0])).reshape(1, SMALL_ROWS, SMALL_COLS)
    small_slab = lax.dynamic_update_slice(jnp.zeros((N_CHIPS, SMALL_ROWS, SMALL_COLS), BF16), words, (me, 0, 0))

    send0, recv0, thru0, token = gather_start([wslabs[s].arr for s in first] + [small_slab], me1, "weight_gather_start_first")
    in_flight = {}
    for slabs, point in _GROUPS[1:]:
        cast_group(slabs, me1 + token[0, 0].astype(jnp.int32))
        send, recv, thru, token = gather_start([wslabs[s].arr for s in slabs], token, f"weight_gather_start_{slabs[0]}")
        in_flight[point] = (send, recv, thru, slabs)
    started_token = token
    landed = gather_wait(send0, recv0, thru0, started_token, "weight_gather_wait_first")
    gathered = gather_forward(landed, "weight_gather_forward_first")
    for s, arr in zip(first, gathered):
        wslabs[s].arr = arr

    def before(i, stage, xs):
        if (i, stage) in in_flight:
            send, recv, thru, slabs = in_flight[(i, stage)]
            landed = gather_wait(send, recv, thru, xs, f"weight_gather_wait_{slabs[0]}")
            for s, arr in zip(slabs, gather_forward(landed, f"weight_gather_forward_{slabs[0]}")):
                wslabs[s].arr = arr
        return xs

    small = {n: p[n] for n in _REPL}
    got, off = gathered[-1].reshape(N_CHIPS, -1), 0
    for n, ax in _SMALL:
        vals = lax.bitcast_convert_type(got[:, off:off + 2 * p[n].size].reshape(N_CHIPS, p[n].size, 2), F32)
        vals = vals.reshape((N_CHIPS,) + p[n].shape)
        small[n] = jnp.concatenate([vals[s] for s in range(N_CHIPS)], axis=ax)
        off += 2 * p[n].size

    gslabs = lay.new_slabs(BF16)
    complete_at = {point: slabs for slabs, point in _GROUPS[1:]}
    swapping, exchanging = {}, []

    def after_bwd(i, stage, dx):
        if (i, stage) in swapping:
            slabs, send, recv, thru = swapping.pop((i, stage))
            g, swapped = swap_wait(send, recv, thru, dx, f"grad_swap_wait_{slabs[0]}")
        elif (i, stage) in complete_at:
            slabs = complete_at[(i, stage)]
            g = [gslabs[s].arr for s in slabs]
            if (i, stage) in _SWAP_DONE:
                send, recv, thru, token = swap_start(g, c1, f"grad_swap_start_{slabs[0]}")
                swapping[_SWAP_DONE[(i, stage)]] = (slabs, send, recv, thru)
                return dx + token[0, 0]
            swapped = pair_swap_halves(g, f"grad_pair_swap_{slabs[0]}")
        else:
            return dx
        part = [pair_add(a, b, c1, f"pair_add_{s}") for a, b, s in zip(g, swapped, slabs)]
        send, recv, thru, token = exchange_start(part, c1, f"grad_exchange_start_{slabs[0]}")
        exchanging.append((slabs, send, recv, thru))
        return dx + token[0, 0]

    se, dx, sgrads = local_step(x[0], mem[0], positions.reshape(-1, 1), loss_target[0], lay, wslabs, gslabs, small,
                                before, after_bwd)
    loss = lax.psum(0.5 * jnp.sum(se) / x.shape[-1], ("x", "y", "c"))
    names, parts, received = [], [], []
    for slabs, send, recv, thru in exchanging:
        part, got = exchange_wait(send, recv, thru, dx, f"grad_exchange_wait_{slabs[0]}")
        names, parts, received = names + slabs, parts + list(part), received + list(got)

    axes = dict(_SMALL)
    small_order = small_names + _REPL
    slots = []
    for s in range(N_CHIPS):
        vals = {n: (lax.slice_in_dim(g, s * p[n].shape[axes[n]], (s + 1) * p[n].shape[axes[n]], axis=axes[n]) if n in axes else g)
                for n, g in sgrads.items()}
        slots.append(_small_pack(vals, small_order))
    g_last = [gslabs[s].arr for s in first] + [jnp.stack(slots).astype(BF16)]
    names_last = first + ["small"]
    swapped_last = pair_swap_halves(g_last, "grad_pair_swap_last")
    part_last = [pair_add(g, b, c1, f"pair_add_{s}") for g, b, s in zip(g_last, swapped_last, names_last)]
    send, recv, thru, token = exchange_start(part_last, c1, "grad_exchange_start_last")
    mc_after = mc + token[0, 0].astype(jnp.int32)
    halves = [chip_sum(q, r, mc_after, f"chip_sum_{s}") for q, r, s in zip(parts, received, names)]
    part_last, got_last = exchange_wait(send, recv, thru, list(halves), "grad_exchange_wait_last")
    halves += [chip_sum(q, r, mc, f"chip_sum_{s}") for q, r, s in zip(part_last, got_last, names_last)]
    reduced = dict(zip(names + names_last, pair_join_halves(halves)))

    res = {}
    for slab in _SLABS:
        for name, off, l0, l1, rpl in lay.members[slab]:
            res[name] = adamw(reduced[slab], off, flat2d(p[name]), flat2d(mom[name]), flat2d(var[name]), l0 * rpl, (l1 - l0) * rpl,
                              res.get(name), f"adamw_{slab}_{name}")
    for name in split:
        res[name] = [o.reshape(p[name].shape) for o in res[name]]
    sp = {k: _small_pack(d, small_order) for k, d in (("w", p), ("m", mom), ("v", var))}
    outs = adamw(reduced["small"], 0, sp["w"], sp["m"], sp["v"], 0, SMALL_ROWS, None, "adamw_small")
    unpacked = [_small_unpack(o, p, small_order) for o in outs]
    for n in small_order:
        res[n] = [u[n] for u in unpacked]
    return (loss, dx[None], *[res[n][k] for k in range(4) for n in _WEIGHTS])
```

```python
import jax
import jax.numpy as jnp
from jax import lax
from jax.experimental import pallas as pl
from jax.experimental.pallas import tpu as pltpu

F32 = jnp.float32
BF16 = jnp.bfloat16
MESH = pl.DeviceIdType.MESH

EPS = 1e-6
ROPE_THETA = 10000.0
N_CHIPS = 4
LANES = 128
VMEM_LIMIT = 56 * 1024 * 1024
NEG = -1e30

MLA_H, MLA_NOPE, MLA_ROPE, MLA_V = 8, 128, 64, 128
MLA_QR, MLA_KVR = 384, 256
MLA_ZPAD = 768
GDN_H, GDN_D, GDN_C = 8, 128, 64
XA_H, XA_D = 4, 256

ADAM_LR, ADAM_B1, ADAM_B2, ADAM_EPS, ADAM_WD, ADAM_STEP = 0.001, 0.9, 0.999, 1e-08, 0.01, 10

SMALL_ROWS, SMALL_COLS = 32, 1024


def _cparams(sem=None):
    return pltpu.CompilerParams(dimension_semantics=sem, vmem_limit_bytes=VMEM_LIMIT)


def _pick(dim, pref):
    t = (min(pref, dim) // LANES) * LANES
    while t >= LANES:
        if dim % t == 0:
            return t
        t -= LANES
    return dim


def _pick_rows(rows, pref, *offsets):
    t = (min(pref, rows) // 16) * 16
    while t > 16 and (rows % t or any(o % t for o in offsets)):
        t -= 16
    return t


class Slab:
    def __init__(self, rows, width, dtype, arr=None):
        self.shape, self.dtype, self.arr = (N_CHIPS, rows, width), dtype, arr


class Loc:
    def __init__(self, slab, row0, K, N, axis):
        self.slab, self.row0, self.K, self.N, self.axis = slab, row0, K, N, axis
        self.Ks = K // N_CHIPS if axis == 0 else K
        self.Ns = N // N_CHIPS if axis == 1 else N

    def tile_spec(self, tr, tc, rc):
        assert self.row0 % tr == 0 and self.Ks % tr == 0 and self.Ns % tc == 0, (self.row0, self.Ks, self.Ns, tr, tc)
        r0, rb, cb = self.row0 // tr, self.Ks // tr, self.Ns // tc
        if self.axis == 0:
            return pl.BlockSpec((None, tr, tc), lambda i, j: (rc(i, j)[0] // rb, r0 + rc(i, j)[0] % rb, rc(i, j)[1]))
        return pl.BlockSpec((None, tr, tc), lambda i, j: (rc(i, j)[1] // cb, r0 + rc(i, j)[0], rc(i, j)[1] % cb))

    def slot_spec(self, slot, tr, tc, rc):
        assert self.row0 % tr == 0, (self.row0, tr)
        r0 = self.row0 // tr
        return pl.BlockSpec((None, tr, tc), lambda i, j: (slot, r0 + rc(i, j)[0], rc(i, j)[1]))


_DIMS = {"nn": ((1,), (0,)), "nt": ((1,), (1,)), "tn": ((0,), (0,))}
_ANY = pl.BlockSpec(memory_space=pl.ANY)


def mm(a, b, mode, name, outs=(F32,), epi=None, extras=(), tm=1024, tn=1024, out_loc=None, vecs=(), row_outs=0, per_row=()):
    full_rows = bool(vecs) or row_outs > 0 or bool(per_row)
    b_loc = b if isinstance(b, Loc) else None
    if mode == "nn":
        M, K = a.shape
        K2, N = (b_loc.K, b_loc.N) if b_loc else b.shape
    elif mode == "nt":
        M, K = a.shape
        N, K2 = (b_loc.K, b_loc.N) if b_loc else b.shape
    else:
        K, M = a.shape
        K2, N = b.shape
    assert K == K2, (name, a.shape, K2, N)
    tm = _pick(out_loc.Ks if (out_loc and out_loc.axis == 0) else M, tm)
    n_split = full_rows and b_loc is not None and mode == "nt" and b_loc.axis == 0
    if out_loc is not None and out_loc.axis == 1:
        tn = _pick(out_loc.Ns, tn)
    elif n_split:
        tn = N
    elif b_loc is not None and ((mode == "nn" and b_loc.axis == 1) or (mode == "nt" and b_loc.axis == 0)):
        tn = _pick(b_loc.Ns if mode == "nn" else b_loc.Ks, tn)
    elif b_loc is not None:
        tn = N if full_rows else _pick(N, min(tn, 512))
    else:
        tn = N if full_rows else _pick(N, tn)
    assert tn == N or not full_rows, name

    parts = 1
    if mode == "tn":
        a_spec = pl.BlockSpec((K, tm), lambda i, j: (0, i))
        b_specs, b_args = [pl.BlockSpec((K, tn), lambda i, j: (0, j))], [b]
    else:
        a_spec = pl.BlockSpec((tm, K), lambda i, j: (i, 0))
        if b_loc is None:
            b_specs = [pl.BlockSpec((K, tn), lambda i, j: (0, j)) if mode == "nn" else pl.BlockSpec((tn, K), lambda i, j: (j, 0))]
            b_args = [b]
        elif mode == "nn" and b_loc.axis == 1:
            b_specs, b_args = [b_loc.tile_spec(K, tn, lambda i, j: (0, j))], [b_loc.slab.arr]
        elif n_split:
            b_specs = [b_loc.slot_spec(s, b_loc.Ks, K, lambda i, j: (0, 0)) for s in range(N_CHIPS)]
            b_args = [b_loc.slab.arr] * N_CHIPS
        elif mode == "nt" and b_loc.axis == 0:
            b_specs, b_args = [b_loc.tile_spec(tn, K, lambda i, j: (j, 0))], [b_loc.slab.arr]
        elif mode == "nn":
            parts = N_CHIPS
            b_specs = [b_loc.slot_spec(s, b_loc.Ks, tn, lambda i, j: (0, j)) for s in range(parts)]
            b_args = [b_loc.slab.arr] * parts
        else:
            parts = N_CHIPS
            b_specs = [b_loc.slot_spec(s, tn, b_loc.Ns, lambda i, j: (j, 0)) for s in range(parts)]
            b_args = [b_loc.slab.arr] * parts
    kp = K // parts
    n_b = N_CHIPS if n_split else parts
    n_ex, n_out = len(extras) + len(per_row) + len(vecs), len(outs)
    dims = (_DIMS[mode], ((), ()))

    def body(*refs):
        a_ref = refs[0]
        b_refs = refs[1:1 + n_b]
        ex_refs = refs[1 + n_b:1 + n_b + n_ex]
        o_refs = refs[len(refs) - n_out - row_outs:len(refs) - row_outs]
        r_refs = refs[len(refs) - row_outs:]
        acc = None
        if n_split:
            av = a_ref[...].astype(BF16)
            acc = jnp.concatenate([lax.dot_general(av, b_ref[...].astype(BF16), dims, preferred_element_type=F32)
                                   for b_ref in b_refs], axis=1)
        for s in range(0 if n_split else parts):
            av = a_ref[...] if parts == 1 else a_ref[:, s * kp:(s + 1) * kp]
            d = lax.dot_general(av.astype(BF16), b_refs[s][...].astype(BF16), dims, preferred_element_type=F32)
            acc = d if acc is None else acc + d
        res = epi(acc, *[e[...] for e in ex_refs]) if epi is not None else (acc,)
        for o_ref, v in zip(o_refs, res[:n_out]):
            o_ref[...] = v.astype(o_ref.dtype)
        for r_ref, v in zip(r_refs, res[n_out:]):
            @pl.when(pl.program_id(0) == 0)
            def _():
                r_ref[...] = jnp.zeros_like(r_ref)

            r_ref[...] += v

    mn_spec = pl.BlockSpec((tm, tn), lambda i, j: (i, j))
    row_spec = pl.BlockSpec((1, tn), lambda i, j: (0, j))
    in_specs = ([a_spec] + b_specs + [mn_spec] * len(extras) + [pl.BlockSpec((tm, r.shape[1]), lambda i, j: (i, 0)) for r in per_row]
                + [row_spec] * len(vecs))
    args = [a] + b_args + list(extras) + list(per_row) + [v.reshape(1, N) for v in vecs]
    aliases = {}
    if out_loc is None:
        out_specs = [mn_spec] * n_out + [row_spec] * row_outs
        out_shape = [jax.ShapeDtypeStruct((M, N), d) for d in outs] + [jax.ShapeDtypeStruct((1, N), F32)] * row_outs
    else:
        assert n_out == 1 and mode == "tn"
        out_specs = [out_loc.tile_spec(tm, tn, lambda i, j: (i, j))]
        out_shape = [jax.ShapeDtypeStruct(out_loc.slab.shape, out_loc.slab.dtype)]
        if out_loc.slab.arr is not None:
            in_specs.append(_ANY)
            args.append(out_loc.slab.arr)
            aliases = {len(args) - 1: 0}

    res = pl.pallas_call(
        body, name=name, grid=(M // tm, N // tn), in_specs=in_specs, out_specs=out_specs, out_shape=out_shape,
        input_output_aliases=aliases, compiler_params=_cparams(("arbitrary" if row_outs else "parallel", "parallel")),
    )(*args)
    if out_loc is not None:
        out_loc.slab.arr = res[0]
        return None
    return res[0] if len(res) == 1 else tuple(res)


def _epi_add(acc, r):
    return (acc + r,)


def _epi_add_norm(acc, r, g):
    x = acc + r
    return x, _rms(x, g)


def _epi_norm_bwd(acc, x, dx_in, g):
    r = lax.rsqrt(jnp.mean(x * x, axis=-1, keepdims=True) + EPS)
    xh = x * r
    dxh = acc * g
    dx = dx_in + r * (dxh - xh * jnp.mean(dxh * xh, axis=-1, keepdims=True))
    return dx, jnp.sum(acc * xh, axis=0, keepdims=True)


def residual_norm(a, w, xs, g, name, tm=1024):
    if g is None:
        return mm(a, w, "nn", name, epi=_epi_add, extras=(xs,), tm=tm), None
    return mm(a, w, "nn", name, outs=(F32, BF16), epi=_epi_add_norm, extras=(xs,), vecs=(g,), tm=tm)


def _epi_relu2(acc):
    r = jnp.maximum(acc, 0.0)
    return acc, r * r


def _epi_relu2_bwd(acc, h1):
    return (acc * (2.0 * jnp.maximum(h1.astype(F32), 0.0)),)


def _rms(x, g):
    return x * lax.rsqrt(jnp.mean(x * x, axis=-1, keepdims=True) + EPS) * g


def _row_spec(ts, cols):
    return pl.BlockSpec((ts, cols), lambda i: (i, 0))


def _par_spec(cols):
    return pl.BlockSpec((1, cols), lambda i: (0, 0))


def rmsnorm_fwd(x, g, name, ts=256):
    T, D = x.shape
    ts = min(ts, T)

    def body(x_ref, g_ref, o_ref):
        o_ref[...] = _rms(x_ref[...], g_ref[...]).astype(o_ref.dtype)

    return pl.pallas_call(
        body, name=name, grid=(T // ts,),
        in_specs=[_row_spec(ts, D), _par_spec(D)], out_specs=_row_spec(ts, D),
        out_shape=jax.ShapeDtypeStruct((T, D), BF16), compiler_params=_cparams(("parallel",)),
    )(x, g.reshape(1, D))


def rmsnorm_bwd(x, g, dy, dx_in, name, ts=256):
    T, D = x.shape
    ts = min(ts, T)

    def body(x_ref, g_ref, dy_ref, dxi_ref, dx_ref, dg_ref):
        xv = x_ref[...]
        r = lax.rsqrt(jnp.mean(xv * xv, axis=-1, keepdims=True) + EPS)
        xh = xv * r
        dyv = dy_ref[...].astype(F32)
        dxh = dyv * g_ref[...]
        dx_ref[...] = dxi_ref[...] + r * (dxh - xh * jnp.mean(dxh * xh, axis=-1, keepdims=True))
        dg = jnp.sum(dyv * xh, axis=0, keepdims=True)

        @pl.when(pl.program_id(0) == 0)
        def _():
            dg_ref[...] = jnp.zeros_like(dg_ref)

        dg_ref[...] += dg

    dx, dg = pl.pallas_call(
        body, name=name, grid=(T // ts,),
        in_specs=[_row_spec(ts, D), _par_spec(D), _row_spec(ts, D), _row_spec(ts, D)],
        out_specs=[_row_spec(ts, D), _par_spec(D)],
        out_shape=[jax.ShapeDtypeStruct((T, D), F32), jax.ShapeDtypeStruct((1, D), F32)],
        compiler_params=_cparams(("arbitrary",)),
    )(x, g.reshape(1, D), dy, dx_in)
    return dx, dg.reshape(D)


def rope_tables(pos, name="rope_tables"):
    T = pos.shape[0]
    half = MLA_ROPE // 2
    inv = ROPE_THETA ** (-jnp.arange(0, MLA_ROPE, 2, dtype=F32) / MLA_ROPE)
    inv_row = jnp.concatenate([inv, inv, jnp.zeros((LANES - MLA_ROPE,), F32)]).reshape(1, LANES)

    def body(p_ref, f_ref, c_ref, a_ref, b_ref):
        ang = p_ref[...].astype(F32) * f_ref[...]
        lane = lax.broadcasted_iota(jnp.int32, ang.shape, 1)
        c, s = jnp.cos(ang), jnp.sin(ang)
        c_ref[...] = jnp.where(lane < MLA_ROPE, c, 0.0)
        a_ref[...] = jnp.where(lane < half, -s, 0.0)
        b_ref[...] = jnp.where((lane >= half) & (lane < MLA_ROPE), s, 0.0)

    sh = jax.ShapeDtypeStruct((T, LANES), F32)
    return pl.pallas_call(body, name=name, out_shape=[sh, sh, sh], compiler_params=_cparams())(pos, inv_row)


def _roll_l(x):
    return pltpu.roll(x, LANES - MLA_ROPE // 2, 1)


def _roll_r(x):
    return pltpu.roll(x, MLA_ROPE // 2, 1)


def _rope(r, c, sa, sb):
    return r * c + _roll_l(r) * sa + _roll_r(r) * sb


def _rope_t(d, c, sa, sb):
    return d * c + _roll_r(d * sa) + _roll_l(d * sb)


def _epi_rope_q(acc, c, sa, sb):
    hw = 2 * LANES
    parts = []
    for h in range(acc.shape[1] // hw):
        parts += [acc[:, h * hw:h * hw + LANES], _rope(acc[:, h * hw + LANES:(h + 1) * hw], c, sa, sb)]
    return (jnp.concatenate(parts, axis=1),)


def mla_mid_fwd(z, qn, kvn, tabs, name, ts=256):
    T = z.shape[0]
    ts = min(ts, T)
    a0, a1 = MLA_QR, MLA_QR + MLA_KVR

    def body(z_ref, qn_ref, kvn_ref, c_ref, sa_ref, sb_ref, cq_ref, ckv_ref, kr_ref):
        cq_ref[...] = _rms(z_ref[:, 0:a0], qn_ref[...]).astype(BF16)
        ckv_ref[...] = _rms(z_ref[:, a0:a1], kvn_ref[...]).astype(BF16)
        kr_ref[...] = _rope(z_ref[:, a1:MLA_ZPAD], c_ref[...], sa_ref[...], sb_ref[...]).astype(BF16)

    return pl.pallas_call(
        body, name=name, grid=(T // ts,),
        in_specs=[_row_spec(ts, MLA_ZPAD), _par_spec(MLA_QR), _par_spec(MLA_KVR)] + [_row_spec(ts, LANES)] * 3,
        out_specs=[_row_spec(ts, MLA_QR), _row_spec(ts, MLA_KVR), _row_spec(ts, LANES)],
        out_shape=[jax.ShapeDtypeStruct((T, MLA_QR), BF16), jax.ShapeDtypeStruct((T, MLA_KVR), BF16),
                   jax.ShapeDtypeStruct((T, LANES), BF16)],
        compiler_params=_cparams(("parallel",)),
    )(z, qn.reshape(1, -1), kvn.reshape(1, -1), *tabs)


def mla_mid_bwd(z, qn, kvn, tabs, dcq, dckv, dkr, name, ts=256):
    T = z.shape[0]
    ts = min(ts, T)
    a0, a1 = MLA_QR, MLA_QR + MLA_KVR

    def body(z_ref, qn_ref, kvn_ref, c_ref, sa_ref, sb_ref, dcq_ref, dckv_ref, dkr_ref, dz_ref, dqn_ref, dkvn_ref):
        _, vq = jax.vjp(_rms, z_ref[:, 0:a0], qn_ref[...])
        dzq, dqn = vq(dcq_ref[...].astype(F32))
        _, vk = jax.vjp(_rms, z_ref[:, a0:a1], kvn_ref[...])
        dzk, dkvn = vk(dckv_ref[...].astype(F32))
        dz_ref[:, 0:a0] = dzq.astype(dz_ref.dtype)
        dz_ref[:, a0:a1] = dzk.astype(dz_ref.dtype)
        dz_ref[:, a1:MLA_ZPAD] = _rope_t(dkr_ref[...].astype(F32), c_ref[...], sa_ref[...], sb_ref[...]).astype(dz_ref.dtype)

        @pl.when(pl.program_id(0) == 0)
        def _():
            dqn_ref[...] = jnp.zeros_like(dqn_ref)
            dkvn_ref[...] = jnp.zeros_like(dkvn_ref)

        dqn_ref[...] += dqn
        dkvn_ref[...] += dkvn

    dz, dqn, dkvn = pl.pallas_call(
        body, name=name, grid=(T // ts,),
        in_specs=[_row_spec(ts, MLA_ZPAD), _par_spec(MLA_QR), _par_spec(MLA_KVR)] + [_row_spec(ts, LANES)] * 3
        + [_row_spec(ts, MLA_QR), _row_spec(ts, MLA_KVR), _row_spec(ts, LANES)],
        out_specs=[_row_spec(ts, MLA_ZPAD), _par_spec(MLA_QR), _par_spec(MLA_KVR)],
        out_shape=[jax.ShapeDtypeStruct((T, MLA_ZPAD), BF16), jax.ShapeDtypeStruct((1, MLA_QR), F32),
                   jax.ShapeDtypeStruct((1, MLA_KVR), F32)],
        compiler_params=_cparams(("arbitrary",)),
    )(z, qn.reshape(1, -1), kvn.reshape(1, -1), *tabs, dcq, dckv, dkr)
    return dz, dqn.reshape(-1), dkvn.reshape(-1)


def loss_head(x, g, target, name="loss_head", ts=256):
    T, D = x.shape
    ts = min(ts, T)

    def body(x_ref, g_ref, t_ref, se_ref, dx_ref, dg_ref):
        xv = x_ref[...]
        r = lax.rsqrt(jnp.mean(xv * xv, axis=-1, keepdims=True) + EPS)
        xh = xv * r
        err = xh * g_ref[...] - t_ref[...]
        dy = err * (1.0 / D)
        dxh = dy * g_ref[...]
        dx_ref[...] = r * (dxh - xh * jnp.mean(dxh * xh, axis=-1, keepdims=True))

        @pl.when(pl.program_id(0) == 0)
        def _():
            se_ref[...] = jnp.zeros_like(se_ref)
            dg_ref[...] = jnp.zeros_like(dg_ref)

        se_ref[...] += jnp.sum(err * err, axis=0, keepdims=True)
        dg_ref[...] += jnp.sum(dy * xh, axis=0, keepdims=True)

    se, dx, dg = pl.pallas_call(
        body, name=name, grid=(T // ts,),
        in_specs=[_row_spec(ts, D), _par_spec(D), _row_spec(ts, D)],
        out_specs=[_par_spec(D), _row_spec(ts, D), _par_spec(D)],
        out_shape=[jax.ShapeDtypeStruct((1, D), F32), jax.ShapeDtypeStruct((T, D), F32), jax.ShapeDtypeStruct((1, D), F32)],
        compiler_params=_cparams(("arbitrary",)),
    )(x, g.reshape(1, D), target)
    return se, dx, dg.reshape(D)


def _dot_nt(a, b):
    return lax.dot_general(a, b, (((1,), (1,)), ((), ())), preferred_element_type=F32)


def _dot_nn(a, b):
    return lax.dot_general(a, b, (((1,), (0,)), ((), ())), preferred_element_type=F32)


class _Attn:
    def __init__(self, H, dq, dk1, dv, causal, scale, hp, hp_kv, blk=256):
        self.H, self.dq, self.dk1, self.dv, self.causal, self.scale, self.blk = H, dq, dk1, dv, causal, scale, blk
        self.hp, self.hp_kv = hp, hp_kv


def _cols(ref, rows, hh, width):
    return ref[rows, hh * width:(hh + 1) * width]


def _keys(cfg, k1_ref, k2_ref, rows, hh):
    ks = _cols(k1_ref, rows, hh, cfg.dk1)
    if k2_ref is not None:
        ks = jnp.concatenate([ks, k2_ref[rows, :]], axis=1)
    return ks


def _attn_specs(cfg, hp, t, Tk, has_k2, by_q):
    g = cfg.H // hp
    if by_q:
        specs = [pl.BlockSpec((t, hp * cfg.dq), lambda h, i: (i, h)),
                 pl.BlockSpec((Tk, hp * cfg.dk1), lambda h, i: (0, h)),
                 pl.BlockSpec((Tk, hp * cfg.dv), lambda h, i: (0, g + h))]
        if has_k2:
            specs.append(pl.BlockSpec((Tk, LANES), lambda h, i: (0, 0)))
    else:
        specs = [None,
                 pl.BlockSpec((t, hp * cfg.dk1), lambda j, h: (j, h)),
                 pl.BlockSpec((t, hp * cfg.dv), lambda j, h: (j, g + h))]
        if has_k2:
            specs.append(pl.BlockSpec((t, LANES), lambda j, h: (j, 0)))
    return specs


def _mask(s, diagonal):
    if not diagonal:
        return s
    return jnp.where(lax.broadcasted_iota(jnp.int32, s.shape, 0) >= lax.broadcasted_iota(jnp.int32, s.shape, 1), s, NEG)


def flash_fwd(cfg, q, k1, v, k2, name):
    Tq, Tk = q.shape[0], k1.shape[0]
    t = min(cfg.blk, Tq, Tk)
    nkb = Tk // t
    has_k2 = k2 is not None
    hp = cfg.hp

    def body(*refs):
        q_ref, k1_ref, v_ref = refs[:3]
        k2_ref = refs[3] if has_k2 else None
        o_ref, lse_ref = refs[-2], refs[-1]
        i = pl.program_id(1)
        qs = [_cols(q_ref, slice(None), hh, cfg.dq) for hh in range(hp)]

        def step(j, carry, diagonal=False):
            rows = pl.ds(pl.multiple_of(j * t, t), t)
            out = []
            for hh in range(hp):
                m, l, acc = carry[hh]
                s = _mask(_dot_nt(qs[hh], _keys(cfg, k1_ref, k2_ref, rows, hh)) * cfg.scale, diagonal)
                m2 = jnp.maximum(m, jnp.max(s, axis=-1, keepdims=True))
                p = jnp.exp(s - m2)
                alpha = jnp.exp(m - m2)
                l2 = alpha * l + jnp.sum(p, axis=-1, keepdims=True)
                acc2 = alpha * acc + _dot_nn(p.astype(BF16), _cols(v_ref, rows, hh, cfg.dv))
                out.append((m2, l2, acc2))
            return tuple(out)

        init = tuple((jnp.full((t, 1), NEG, F32), jnp.zeros((t, 1), F32), jnp.zeros((t, cfg.dv), F32)) for _ in range(hp))
        res = lax.fori_loop(0, i if cfg.causal else nkb, step, init)
        if cfg.causal:
            res = step(i, res, True)
        for hh in range(hp):
            m, l, acc = res[hh]
            o_ref[:, hh * cfg.dv:(hh + 1) * cfg.dv] = (acc / l).astype(o_ref.dtype)
            lse_ref[hh] = m + jnp.log(l)

    args = [q, k1, v] + ([k2] if has_k2 else [])
    return pl.pallas_call(
        body, name=name, grid=(cfg.H // hp, Tq // t), in_specs=_attn_specs(cfg, hp, t, Tk, has_k2, True),
        out_specs=[pl.BlockSpec((t, hp * cfg.dv), lambda h, i: (i, h)), pl.BlockSpec((hp, t, 1), lambda h, i: (h, i, 0))],
        out_shape=[jax.ShapeDtypeStruct((Tq, cfg.H * cfg.dv), BF16), jax.ShapeDtypeStruct((cfg.H, Tq, 1), F32)],
        compiler_params=_cparams(("parallel", "parallel")),
    )(*args)


def flash_dq(cfg, q, k1, v, k2, o, do, lse, out_dtype, name, rope_tabs=None):
    Tq, Tk = q.shape[0], k1.shape[0]
    t = min(cfg.blk, Tq, Tk)
    nkb = Tk // t
    has_k2 = k2 is not None
    hp = cfg.hp
    n_tab = 0 if rope_tabs is None else len(rope_tabs)

    def body(*refs):
        q_ref, k1_ref, v_ref = refs[:3]
        k2_ref = refs[3] if has_k2 else None
        tab_refs = refs[len(refs) - 5 - n_tab:len(refs) - 5]
        o_ref, do_ref, lse_ref, dq_ref, dl_ref = refs[-5:]
        i = pl.program_id(1)
        qs = [_cols(q_ref, slice(None), hh, cfg.dq) for hh in range(hp)]
        dos = [_cols(do_ref, slice(None), hh, cfg.dv) for hh in range(hp)]
        lses = [lse_ref[hh] for hh in range(hp)]
        deltas = []
        for hh in range(hp):
            d = jnp.sum(dos[hh].astype(F32) * _cols(o_ref, slice(None), hh, cfg.dv).astype(F32), axis=-1, keepdims=True)
            dl_ref[hh] = d
            deltas.append(d)

        def step(j, dqs, diagonal=False):
            rows = pl.ds(pl.multiple_of(j * t, t), t)
            out = []
            for hh in range(hp):
                ks = _keys(cfg, k1_ref, k2_ref, rows, hh)
                s = _mask(_dot_nt(qs[hh], ks) * cfg.scale, diagonal)
                p = jnp.exp(s - lses[hh])
                dp = _dot_nt(dos[hh], _cols(v_ref, rows, hh, cfg.dv))
                ds = p * (dp - deltas[hh]) * cfg.scale
                out.append(dqs[hh] + _dot_nn(ds.astype(BF16), ks))
            return tuple(out)

        dqs = lax.fori_loop(0, i if cfg.causal else nkb, step, tuple(jnp.zeros((t, cfg.dq), F32) for _ in range(hp)))
        if cfg.causal:
            dqs = step(i, dqs, True)
        tabs = [r[...] for r in tab_refs]
        for hh in range(hp):
            dq = dqs[hh]
            if tabs:
                dq = jnp.concatenate([dq[:, :LANES], _rope_t(dq[:, LANES:], *tabs)], axis=1)
            dq_ref[:, hh * cfg.dq:(hh + 1) * cfg.dq] = dq.astype(dq_ref.dtype)

    ov = pl.BlockSpec((t, hp * cfg.dv), lambda h, i: (i, h))
    row1 = pl.BlockSpec((hp, t, 1), lambda h, i: (h, i, 0))
    tab_specs = [pl.BlockSpec((t, LANES), lambda h, i: (i, 0))] * n_tab
    args = [q, k1, v] + ([k2] if has_k2 else []) + list(rope_tabs or ()) + [o, do, lse]
    return pl.pallas_call(
        body, name=name, grid=(cfg.H // hp, Tq // t),
        in_specs=_attn_specs(cfg, hp, t, Tk, has_k2, True) + tab_specs + [ov, ov, row1],
        out_specs=[pl.BlockSpec((t, hp * cfg.dq), lambda h, i: (i, h)), row1],
        out_shape=[jax.ShapeDtypeStruct((Tq, cfg.H * cfg.dq), out_dtype), jax.ShapeDtypeStruct((cfg.H, Tq, 1), F32)],
        compiler_params=_cparams(("parallel", "parallel")),
    )(*args)


def flash_dkv(cfg, q, k1, v, k2, do, lse, delta, out_dtype, name):
    Tq, Tk = q.shape[0], k1.shape[0]
    t = min(cfg.blk, Tq, Tk)
    nqb = Tq // t
    has_k2 = k2 is not None
    hp = cfg.hp_kv
    assert hp == cfg.H
    v0 = cfg.H * cfg.dk1

    def body(*refs):
        q_ref, k1_ref, v_ref = refs[:3]
        k2_ref = refs[3] if has_k2 else None
        n_in = 4 if has_k2 else 3
        do_ref, lse_ref, dl_ref = refs[n_in:n_in + 3]
        dkv_ref = refs[n_in + 3]
        j, h = pl.program_id(0), pl.program_id(1)
        kss = [_keys(cfg, k1_ref, k2_ref, slice(None), hh) for hh in range(hp)]
        vss = [_cols(v_ref, slice(None), hh, cfg.dv) for hh in range(hp)]

        def step(i, carry, diagonal=False):
            rows = pl.ds(pl.multiple_of(i * t, t), t)
            out = []
            for hh in range(hp):
                dk, dv = carry[hh]
                qi, doi = _cols(q_ref, rows, hh, cfg.dq), _cols(do_ref, rows, hh, cfg.dv)
                s = _dot_nt(kss[hh], qi) * cfg.scale
                if diagonal:
                    s = jnp.where(lax.broadcasted_iota(jnp.int32, s.shape, 0) <= lax.broadcasted_iota(jnp.int32, s.shape, 1), s, NEG)
                p = jnp.exp(s - lse_ref[hh, :, rows])
                dv = dv + _dot_nn(p.astype(BF16), doi)
                ds = p * (_dot_nt(vss[hh], doi) - dl_ref[hh, :, rows]) * cfg.scale
                dk = dk + _dot_nn(ds.astype(BF16), qi)
                out.append((dk, dv))
            return tuple(out)

        init = tuple((jnp.zeros((t, cfg.dq), F32), jnp.zeros((t, cfg.dv), F32)) for _ in range(hp))
        if cfg.causal:
            res = lax.fori_loop(j + 1, nqb, step, step(j, init, True))
        else:
            res = lax.fori_loop(0, nqb, step, init)
        for hh in range(hp):
            dk, dv = res[hh]
            dkv_ref[:, hh * cfg.dk1:(hh + 1) * cfg.dk1] = dk[:, 0:cfg.dk1].astype(dkv_ref.dtype)
            dkv_ref[:, v0 + hh * cfg.dv:v0 + (hh + 1) * cfg.dv] = dv.astype(dkv_ref.dtype)
        if has_k2:
            dk2_ref = refs[n_in + 4]

            @pl.when(h == 0)
            def _():
                dk2_ref[...] = jnp.zeros_like(dk2_ref)

            for hh in range(hp):
                dk2_ref[...] += res[hh][0][:, cfg.dk1:]

    specs = _attn_specs(cfg, hp, t, Tk, has_k2, False)
    specs[0] = pl.BlockSpec((Tq, hp * cfg.dq), lambda j, h: (0, h))
    rows_all = pl.BlockSpec((hp, 1, Tq), lambda j, h: (h, 0, 0))
    specs += [pl.BlockSpec((Tq, hp * cfg.dv), lambda j, h: (0, h)), rows_all, rows_all]
    args = [q, k1, v] + ([k2] if has_k2 else []) + [do, lse.reshape(cfg.H, 1, Tq), delta.reshape(cfg.H, 1, Tq)]
    out_specs = [pl.BlockSpec((t, v0 + cfg.H * cfg.dv), lambda j, h: (j, 0))]
    out_shape = [jax.ShapeDtypeStruct((Tk, v0 + cfg.H * cfg.dv), out_dtype)]
    if has_k2:
        out_specs.append(pl.BlockSpec((t, LANES), lambda j, h: (j, 0)))
        out_shape.append(jax.ShapeDtypeStruct((Tk, LANES), F32))
    return pl.pallas_call(
        body, name=name, grid=(Tk // t, cfg.H // hp), in_specs=specs, out_specs=out_specs, out_shape=out_shape,
        compiler_params=_cparams(("parallel", "arbitrary")),
    )(*args)


def _shift_down(x, s):
    if s == 0:
        return x
    t = lax.broadcasted_iota(jnp.int32, x.shape, 0)
    return jnp.where(t >= s, pltpu.roll(x, s, 0), 0.0)


def _shift_up(x, s):
    if s == 0:
        return x
    n = x.shape[0]
    t = lax.broadcasted_iota(jnp.int32, x.shape, 0)
    return jnp.where(t < n - s, pltpu.roll(x, n - s, 0), 0.0)


def _conv(x, w_ref, kw):
    y = x * w_ref[kw - 1:kw, :]
    for j in range(kw - 1):
        y = y + _shift_down(x, kw - 1 - j) * w_ref[j:j + 1, :]
    return y


def _conv_t(d, w_ref, kw):
    y = d * w_ref[kw - 1:kw, :]
    for j in range(kw - 1):
        y = y + _shift_up(d, kw - 1 - j) * w_ref[j:j + 1, :]
    return y


def _conv_dw(d, x, kw):
    rows = lax.broadcasted_iota(jnp.int32, (kw, d.shape[1]), 0)
    dw = jnp.zeros((kw, d.shape[1]), F32)
    for j in range(kw):
        r = jnp.sum(d * _shift_down(x, kw - 1 - j), axis=0, keepdims=True)
        dw = jnp.where(rows == j, r, dw)
    return dw


def _silu(x):
    return x * jax.nn.sigmoid(x)


def _silu_grad(x):
    s = jax.nn.sigmoid(x)
    return s * (1.0 + x * (1.0 - s))


def gdn_conv_fwd(z, w, name, tc=256):
    T, C = z.shape[0], w.shape[1]
    kw = w.shape[0]

    def body(x_ref, w_ref, o_ref):
        o_ref[...] = _silu(_conv(x_ref[...], w_ref, kw))

    return pl.pallas_call(
        body, name=name, grid=(C // tc,),
        in_specs=[pl.BlockSpec((T, tc), lambda j: (0, j)), pl.BlockSpec((kw, tc), lambda j: (0, j))],
        out_specs=pl.BlockSpec((T, tc), lambda j: (0, j)),
        out_shape=jax.ShapeDtypeStruct((T, C), F32), compiler_params=_cparams(("parallel",)),
    )(z, w)


def gdn_conv_bwd(z, w, dy, name, tc=256):
    T, C = z.shape[0], w.shape[1]
    kw = w.shape[0]

    def body(x_ref, w_ref, dy_ref, dx_ref, dw_ref):
        xv = x_ref[...]
        dc = dy_ref[...] * _silu_grad(_conv(xv, w_ref, kw))
        dx_ref[...] = _conv_t(dc, w_ref, kw).astype(dx_ref.dtype)
        dw_ref[...] = _conv_dw(dc, xv, kw)

    col = lambda j: (0, j)
    return pl.pallas_call(
        body, name=name, grid=(C // tc,),
        in_specs=[pl.BlockSpec((T, tc), col), pl.BlockSpec((kw, tc), col), pl.BlockSpec((T, tc), col)],
        out_specs=[pl.BlockSpec((T, tc), col), pl.BlockSpec((kw, tc), col)],
        out_shape=[jax.ShapeDtypeStruct((T, C), BF16), jax.ShapeDtypeStruct((kw, C), F32)],
        compiler_params=_cparams(("parallel",)),
    )(z, w, dy)


def sc_fwd(z, w, name, tc=256):
    T, C = z.shape[0], w.shape[1]
    kw, nb = w.shape[0], C // tc

    def body(b_ref, c_ref, u_ref, w_ref, o_ref):
        o_ref[...] = (b_ref[...] * _conv(c_ref[...] * u_ref[...], w_ref, kw)).astype(o_ref.dtype)

    return pl.pallas_call(
        body, name=name, grid=(nb,),
        in_specs=[pl.BlockSpec((T, tc), lambda j: (0, j)), pl.BlockSpec((T, tc), lambda j: (0, nb + j)),
                  pl.BlockSpec((T, tc), lambda j: (0, 2 * nb + j)), pl.BlockSpec((kw, tc), lambda j: (0, j))],
        out_specs=pl.BlockSpec((T, tc), lambda j: (0, j)),
        out_shape=jax.ShapeDtypeStruct((T, C), BF16), compiler_params=_cparams(("parallel",)),
    )(z, z, z, w)


def sc_bwd(z, w, dy, name, tc=256):
    T, C = z.shape[0], w.shape[1]
    kw, nb = w.shape[0], C // tc

    def body(b_ref, c_ref, u_ref, w_ref, dy_ref, db_ref, dc_ref, du_ref, dw_ref):
        cv, uv, dyv = c_ref[...], u_ref[...], dy_ref[...]
        cu = cv * uv
        db_ref[...] = (dyv * _conv(cu, w_ref, kw)).astype(db_ref.dtype)
        dcv = dyv * b_ref[...]
        dcu = _conv_t(dcv, w_ref, kw)
        dc_ref[...] = (dcu * uv).astype(dc_ref.dtype)
        du_ref[...] = (dcu * cv).astype(du_ref.dtype)
        dw_ref[...] = _conv_dw(dcv, cu, kw)

    col = lambda j: (0, j)
    act = jax.ShapeDtypeStruct((T, C), BF16)
    return pl.pallas_call(
        body, name=name, grid=(nb,),
        in_specs=[pl.BlockSpec((T, tc), col), pl.BlockSpec((T, tc), lambda j: (0, nb + j)),
                  pl.BlockSpec((T, tc), lambda j: (0, 2 * nb + j)), pl.BlockSpec((kw, tc), col), pl.BlockSpec((T, tc), col)],
        out_specs=[pl.BlockSpec((T, tc), col)] * 3 + [pl.BlockSpec((kw, tc), col)],
        out_shape=[act, act, act, jax.ShapeDtypeStruct((kw, C), F32)],
        compiler_params=_cparams(("parallel",)),
    )(z, z, z, w, dy)


def _hdot(a, b, dims):
    a_hi, b_hi = a.astype(BF16), b.astype(BF16)
    a_lo, b_lo = (a - a_hi.astype(F32)).astype(BF16), (b - b_hi.astype(F32)).astype(BF16)
    dot = lambda x, y: lax.dot_general(x, y, (dims, ((), ())), preferred_element_type=F32)
    return dot(a_hi, b_hi) + (dot(a_hi, b_lo) + dot(a_lo, b_hi))


def _bdot(a, b, dims):
    return lax.dot_general(a.astype(BF16), b.astype(BF16), (dims, ((), ())), preferred_element_type=F32)


_NN, _NT, _TN = ((1,), (0,)), ((1,), (1,)), ((0,), (0,))


def _per_head_dots(dot2d):
    def stacked(a, b, dims):
        return jnp.stack([dot2d(a[h], b[h], dims) for h in range(a.shape[0])])

    @jax.custom_vjp
    def nn(a, b):
        return stacked(a, b, _NN)

    @jax.custom_vjp
    def nt(a, b):
        return stacked(a, b, _NT)

    @jax.custom_vjp
    def tn(a, b):
        return stacked(a, b, _TN)

    nn.defvjp(lambda a, b: (nn(a, b), (a, b)), lambda r, d: (stacked(d, r[1], _NT), stacked(r[0], d, _TN)))
    nt.defvjp(lambda a, b: (nt(a, b), (a, b)), lambda r, d: (stacked(d, r[1], _NN), stacked(d, r[0], _TN)))
    tn.defvjp(lambda a, b: (tn(a, b), (a, b)), lambda r, d: (stacked(r[1], d, _NT), stacked(r[0], d, _NN)))
    return nn, nt, tn


_hnn, _hnt, _htn = _per_head_dots(_hdot)
_bnn, _bnt, _btn = _per_head_dots(_bdot)


@jax.custom_vjp
def _unit_lower_inverse(m):
    c = m.shape[-1]
    eye = (lax.broadcasted_iota(jnp.int32, (c, c), 0) == lax.broadcasted_iota(jnp.int32, (c, c), 1)).astype(F32)
    t = eye - m
    p = _hnn(m, m)
    n = 2
    while n < c:
        t = t + _hnn(t, p)
        n *= 2
        if n < c:
            p = _hnn(p, p)
    return t


def _uli_fwd(m):
    t = _unit_lower_inverse(m)
    return t, t


def _uli_bwd(t, dt):
    return (-_htn(t, _hnt(dt, t)),)


_unit_lower_inverse.defvjp(_uli_fwd, _uli_bwd)


@jax.custom_vjp
def _known_inverse(m, t):
    return t


_known_inverse.defvjp(lambda m, t: (t, t), lambda t, dt: (_uli_bwd(t, dt)[0], jnp.zeros_like(t)))


def _gdn_chunk(q, k, v, gate, bl, al, a_log, dt_bias, o_norm, st, t_known=None):
    nh, c = q.shape[0], q.shape[1]
    ii = lax.broadcasted_iota(jnp.int32, (c, c), 0)
    jj = lax.broadcasted_iota(jnp.int32, (c, c), 1)
    tri, strict = ii >= jj, ii > jj
    q = q * lax.rsqrt(jnp.sum(q * q, -1, keepdims=True) + EPS) * (GDN_D ** -0.5)
    k = k * lax.rsqrt(jnp.sum(k * k, -1, keepdims=True) + EPS)
    beta = jax.nn.sigmoid(bl)
    g = -jnp.exp(a_log) * jax.nn.softplus(al + dt_bias)
    gc = _hnn(jnp.broadcast_to(tri.astype(F32), (nh, c, c)), g)
    gcol = _hnn(gc, jnp.full((nh, LANES, c), 1.0 / LANES, F32))
    grow = _hnt(jnp.full((nh, c, LANES), 1.0 / LANES, F32), gc)
    decay = jnp.where(tri, jnp.exp(jnp.where(tri, gcol - grow, 0.0)), 0.0)
    kb = k * beta
    m = jnp.where(strict, _bnt(kb, k) * decay, 0.0)
    t_inv = _unit_lower_inverse(m) if t_known is None else _known_inverse(m, t_known)
    eg = jnp.exp(gc)
    u = _bnn(t_inv, v * beta)
    w = _bnn(t_inv, kb * eg)
    attn = _bnt(q, k) * decay
    v_new = u - _bnn(w, st)
    o = _bnn(q * eg, st) + _bnn(attn, v_new)
    g_last = jnp.sum(g, axis=1, keepdims=True)
    st_new = st * jnp.exp(g_last) + _btn(k * jnp.exp(g_last - gc), v_new)
    o = o * lax.rsqrt(jnp.mean(o * o, -1, keepdims=True) + EPS) * o_norm
    return o * _silu(gate), st_new, t_inv


GDN_HP = 8
_GW = GDN_HP * GDN_D
_GB = GDN_H // GDN_HP


def _gdn_specs(n_chunks, rev):
    def tok(col):
        if rev:
            return pl.BlockSpec((GDN_C, _GW), lambda h, n: (n_chunks - 1 - n, col + h))
        return pl.BlockSpec((GDN_C, _GW), lambda h, n: (n, col + h))
    par = pl.BlockSpec((1, _GW), lambda h, n: (0, h))
    shared = pl.BlockSpec((1, GDN_D), lambda h, n: (0, 0))
    if rev:
        st = pl.BlockSpec((GDN_HP, None, GDN_D, GDN_D), lambda h, n: (h, n_chunks - 1 - n, 0, 0))
    else:
        st = pl.BlockSpec((GDN_HP, None, GDN_D, GDN_D), lambda h, n: (h, n, 0, 0))
    return tok, par, shared, st


def _heads(ref):
    return jnp.stack([ref[:, h * GDN_D:(h + 1) * GDN_D] for h in range(ref.shape[1] // GDN_D)])


def gdn_chunk_fwd(qkv, z, a_log_x, dt_bias_x, o_norm, name):
    T = qkv.shape[0]
    n_chunks = T // GDN_C
    H = GDN_H
    tok, par, shared, st_spec = _gdn_specs(n_chunks, False)

    def body(q_ref, k_ref, v_ref, g_ref, bl_ref, al_ref, a_ref, dt_ref, on_ref, o_ref, st_ref, ti_ref, state):
        @pl.when(pl.program_id(1) == 0)
        def _():
            state[...] = jnp.zeros_like(state)

        st = state[...]
        st_ref[...] = st
        o, st_new, t_inv = _gdn_chunk(_heads(q_ref), _heads(k_ref), _heads(v_ref), _heads(g_ref), _heads(bl_ref), _heads(al_ref),
                                      _heads(a_ref), _heads(dt_ref), on_ref[...], st)
        for hh in range(GDN_HP):
            o_ref[:, hh * GDN_D:(hh + 1) * GDN_D] = o[hh].astype(o_ref.dtype)
        ti_ref[...] = t_inv
        state[...] = st_new

    B = _GB
    return pl.pallas_call(
        body, name=name, grid=(B, n_chunks),
        in_specs=[tok(0), tok(B), tok(2 * B), tok(3 * B), tok(4 * B), tok(5 * B), par, par, shared],
        out_specs=[tok(0), st_spec, pl.BlockSpec((GDN_HP, None, GDN_C, GDN_C), lambda h, n: (h, n, 0, 0))],
        out_shape=[jax.ShapeDtypeStruct((T, H * GDN_D), BF16), jax.ShapeDtypeStruct((H, n_chunks, GDN_D, GDN_D), F32),
                   jax.ShapeDtypeStruct((H, n_chunks, GDN_C, GDN_C), F32)],
        scratch_shapes=[pltpu.VMEM((GDN_HP, GDN_D, GDN_D), F32)],
        compiler_params=_cparams(("parallel", "arbitrary")),
    )(qkv, qkv, qkv, z, z, z, a_log_x, dt_bias_x, o_norm)


def gdn_chunk_bwd(qkv, z, a_log_x, dt_bias_x, o_norm, states, t_invs, do, name):
    T = qkv.shape[0]
    n_chunks = T // GDN_C
    H = GDN_H
    tok, par, shared, st_spec = _gdn_specs(n_chunks, True)

    def body(q_ref, k_ref, v_ref, g_ref, bl_ref, al_ref, a_ref, dt_ref, on_ref, st_ref, ti_ref, do_ref,
             dqkv_ref, dg_ref, dba_ref, da_ref, ddt_ref, don_ref, dstate):
        h, n = pl.program_id(0), pl.program_id(1)

        @pl.when(n == 0)
        def _():
            dstate[...] = jnp.zeros_like(dstate)
            da_ref[...] = jnp.zeros_like(da_ref)
            ddt_ref[...] = jnp.zeros_like(ddt_ref)

        @pl.when((n == 0) & (h == 0))
        def _():
            don_ref[...] = jnp.zeros_like(don_ref)

        t_known = ti_ref[...]
        _, vjp = jax.vjp(lambda *ins: _gdn_chunk(*ins, t_known=t_known)[:2],
                         _heads(q_ref), _heads(k_ref), _heads(v_ref), _heads(g_ref), _heads(bl_ref), _heads(al_ref),
                         _heads(a_ref), _heads(dt_ref), on_ref[...], st_ref[...])
        dq, dk, dv, dg, dbl, dal, da, ddt, don, dst = vjp((_heads(do_ref).astype(F32), dstate[...]))
        lane = lax.broadcasted_iota(jnp.int32, (GDN_C, LANES), 1)
        dba = jnp.zeros((GDN_C, LANES), F32)
        for hh in range(GDN_HP):
            cols = slice(hh * GDN_D, (hh + 1) * GDN_D)
            for part, d in enumerate((dq, dk, dv)):
                dqkv_ref[:, part * H * GDN_D + hh * GDN_D:part * H * GDN_D + (hh + 1) * GDN_D] = d[hh]
            dg_ref[:, cols] = dg[hh].astype(dg_ref.dtype)
            dba = jnp.where(lane == hh, jnp.sum(dbl[hh], axis=-1, keepdims=True), dba)
            dba = jnp.where(lane == H + hh, jnp.sum(dal[hh], axis=-1, keepdims=True), dba)
            da_ref[:, cols] += da[hh]
            ddt_ref[:, cols] += ddt[hh]
        dba_ref[...] = dba.astype(dba_ref.dtype)
        don_ref[...] += don
        dstate[...] = dst

    tok0 = tok(0)
    B = _GB
    assert B == 1
    bf_tok = jax.ShapeDtypeStruct((T, H * GDN_D), BF16)
    par_sh = jax.ShapeDtypeStruct((1, H * GDN_D), F32)
    return pl.pallas_call(
        body, name=name, grid=(B, n_chunks),
        in_specs=[tok(0), tok(B), tok(2 * B), tok(3 * B), tok(4 * B), tok(5 * B), par, par, shared, st_spec,
                  pl.BlockSpec((GDN_HP, None, GDN_C, GDN_C), lambda h, n: (h, n_chunks - 1 - n, 0, 0)), tok0],
        out_specs=[pl.BlockSpec((GDN_C, 3 * H * GDN_D), lambda h, n: (n_chunks - 1 - n, 0)), tok0,
                   pl.BlockSpec((GDN_C, LANES), lambda h, n: (n_chunks - 1 - n, 0)), par, par, shared],
        out_shape=[jax.ShapeDtypeStruct((T, 3 * H * GDN_D), F32), bf_tok, jax.ShapeDtypeStruct((T, LANES), BF16), par_sh, par_sh,
                   jax.ShapeDtypeStruct((1, GDN_D), F32)],
        scratch_shapes=[pltpu.VMEM((GDN_HP, GDN_D, GDN_D), F32)],
        compiler_params=_cparams(("arbitrary", "arbitrary")),
    )(qkv, qkv, qkv, z, z, z, a_log_x, dt_bias_x, o_norm, states, t_invs, do)


def _prefetch_call(body, name, grid, in_specs, out_specs, out_shape, aliases=None):
    return pl.pallas_call(
        body, name=name,
        grid_spec=pltpu.PrefetchScalarGridSpec(num_scalar_prefetch=1, grid=grid, in_specs=in_specs, out_specs=out_specs),
        out_shape=out_shape, input_output_aliases=aliases or {},
        compiler_params=_cparams(("parallel",) * len(grid)))


def cast_into(src, src_row0, rows, slab, row0, me, name):
    width = src.shape[1]
    tr = _pick_rows(rows, 1024, row0, src_row0)
    assert rows % tr == 0 and row0 % tr == 0 and src_row0 % tr == 0

    def body(me_ref, s_ref, *refs):
        refs[-1][...] = s_ref[...].astype(refs[-1].dtype)

    in_specs = [pl.BlockSpec((tr, width), lambda r, me_ref: (src_row0 // tr + r, 0))]
    args = [src]
    aliases = {}
    if slab.arr is not None:
        in_specs.append(_ANY)
        args.append(slab.arr)
        aliases = {2: 0}
    slab.arr = _prefetch_call(
        body, name, (rows // tr,), in_specs,
        pl.BlockSpec((None, tr, width), lambda r, me_ref: (me_ref[0], row0 // tr + r, 0)),
        jax.ShapeDtypeStruct(slab.shape, slab.dtype), aliases)(me, *args)


def pair_add(g, b, c_idx, name):
    n, rh, w = b.shape
    tr = _pick_rows(rh, 1024)
    nb = rh // tr

    def body(c_ref, g_ref, b_ref, o_ref):
        o_ref[...] = (g_ref[...].astype(F32) + b_ref[...].astype(F32)).astype(o_ref.dtype)

    return _prefetch_call(
        body, name, (n, nb),
        [pl.BlockSpec((None, tr, w), lambda k, r, c: (k, c[0] * nb + r, 0)), pl.BlockSpec((None, tr, w), lambda k, r, c: (k, r, 0))],
        pl.BlockSpec((None, tr, w), lambda k, r, c: (k, r, 0)), jax.ShapeDtypeStruct(b.shape, BF16))(c_idx, g, b)


def chip_sum(p, rv, mc, name):
    n, rh, w = p.shape
    tr = _pick_rows(rh, 512)
    nb = rh // tr

    def body(mc_ref, p_ref, rv_ref, o_ref):
        me = mc_ref[0]
        acc = None
        for k in range(n):
            part = jnp.where(me == k, p_ref[...], rv_ref[k]).astype(F32)
            acc = part if acc is None else acc + part
        o_ref[...] = acc.astype(o_ref.dtype)

    return _prefetch_call(
        body, name, (nb,),
        [pl.BlockSpec((None, tr, w), lambda r, mc_ref: (mc_ref[0], r, 0)), pl.BlockSpec((n, tr, w), lambda r, mc_ref: (0, r, 0))],
        pl.BlockSpec((tr, w), lambda r, mc_ref: (mc_ref[1] * nb + r, 0)), jax.ShapeDtypeStruct((2 * rh, w), BF16))(mc, p, rv)


def adamw(red, row0, w, m, v, w_row0, rows, prev, name):
    cols = w.shape[1]
    tr = _pick_rows(rows, 512, row0, w_row0)
    assert rows % tr == 0 and row0 % tr == 0 and w_row0 % tr == 0

    def body(g_ref, w_ref, m_ref, v_ref, *refs):
        go_ref, d_ref, nm_ref, nv_ref = refs[-4:]
        gv = g_ref[...].astype(F32)
        nm = ADAM_B1 * m_ref[...] + (1.0 - ADAM_B1) * gv
        nv = ADAM_B2 * v_ref[...] + (1.0 - ADAM_B2) * (gv * gv)
        m_hat = nm / (1.0 - ADAM_B1 ** ADAM_STEP)
        v_hat = nv / (1.0 - ADAM_B2 ** ADAM_STEP)
        go_ref[...] = gv
        d_ref[...] = -ADAM_LR * (m_hat / (jnp.sqrt(v_hat) + ADAM_EPS) + ADAM_WD * w_ref[...])
        nm_ref[...] = nm
        nv_ref[...] = nv

    spec = pl.BlockSpec((tr, cols), lambda r: (w_row0 // tr + r, 0))
    sh = jax.ShapeDtypeStruct(w.shape, F32)
    in_specs = [pl.BlockSpec((tr, cols), lambda r: (row0 // tr + r, 0)), spec, spec, spec]
    args, aliases = [red, w, m, v], {}
    if prev is not None:
        in_specs += [_ANY] * 4
        args += list(prev)
        aliases = {4 + k: k for k in range(4)}
    return pl.pallas_call(
        body, name=name, grid=(rows // tr,), in_specs=in_specs, out_specs=[spec] * 4, out_shape=[sh] * 4,
        input_output_aliases=aliases, compiler_params=_cparams(("parallel",)),
    )(*args)


def _place():
    x, y, c = lax.axis_index("x"), lax.axis_index("y"), lax.axis_index("c")
    chips = [(1 - x, y), (x, 1 - y), (1 - x, 1 - y)]
    return x, y, c, chips


def _chip_index(cx, cy):
    return 2 * cx + cy


def _remote(src, dst, send_sem, recv_sem, to):
    return pltpu.make_async_remote_copy(src_ref=src, dst_ref=dst, send_sem=send_sem, recv_sem=recv_sem,
                                        device_id=to, device_id_type=MESH)


def _comm_call(body, name, ins, out_shapes, n_sems, aliases):
    return pl.pallas_call(
        body, name=name, in_specs=[_ANY] * len(ins), out_specs=[_ANY] * len(out_shapes), out_shape=out_shapes,
        scratch_shapes=[pltpu.SemaphoreType.DMA((n_sems,)), pltpu.SemaphoreType.DMA((n_sems,))],
        input_output_aliases=aliases,
    )(*ins)


def pair_swap_halves(slabs, name="grad_pair_swap"):
    n = len(slabs)

    def body(*refs):
        in_refs, out_refs, send_sems, recv_sems = refs[:n], refs[n:2 * n], refs[-2], refs[-1]
        x, y, c, _ = _place()
        cps = []
        for a in range(n):
            rh = in_refs[a].shape[1] // 2
            cp = _remote(in_refs[a].at[:, pl.ds((1 - c) * rh, rh), :], out_refs[a], send_sems.at[a], recv_sems.at[a], (x, y, 1 - c))
            cp.start()
            cps.append(cp)
        for cp in cps:
            cp.wait()

    outs = [jax.ShapeDtypeStruct((s.shape[0], s.shape[1] // 2, s.shape[2]), s.dtype) for s in slabs]
    return _comm_call(body, name, slabs, outs, n, {})


def pair_join_halves(reds, name="grad_pair_join"):
    n = len(reds)

    def body(*refs):
        in_refs, out_refs, send_sems, recv_sems = refs[:n], refs[n:2 * n], refs[-2], refs[-1]
        x, y, c, _ = _place()
        cps = []
        for a in range(n):
            rh = in_refs[a].shape[0] // 2
            mine = pl.ds(c * rh, rh)
            cp = _remote(in_refs[a].at[mine], out_refs[a].at[mine], send_sems.at[a], recv_sems.at[a], (x, y, 1 - c))
            cp.start()
            cps.append(cp)
        for a in range(n):
            rh = in_refs[a].shape[0] // 2
            got = out_refs[a].at[pl.ds((1 - c) * rh, rh)]
            _remote(got, got, send_sems.at[a], recv_sems.at[a], (x, y, 1 - c)).wait_recv()
        for cp in cps:
            cp.wait_send()

    return _comm_call(body, name, reds, [jax.ShapeDtypeStruct(r.shape, r.dtype) for r in reds], n, {a: a for a in range(n)})


_HBM = pl.BlockSpec(memory_space=pltpu.HBM)
_SEM = pl.BlockSpec(memory_space=pltpu.SEMAPHORE)
_EFFECT = pltpu.SideEffectType.DATAFLOW_SIDE_EFFECTING


def _in_hbm(a):
    return pltpu.with_memory_space_constraint(a, pltpu.HBM)


def _hbm_like(a):
    return pltpu.HBM(a.shape, a.dtype)


def _start_call(body, name, ins, n_sems, after):
    n = len(ins)
    res = pl.pallas_call(
        body, name=name, in_specs=[_HBM] * n + [_ANY],
        out_specs=[_SEM, _SEM] + [_HBM] * n + [pl.BlockSpec(memory_space=pltpu.VMEM)],
        out_shape=[pltpu.SemaphoreType.DMA((n_sems,)), pltpu.SemaphoreType.DMA((n_sems,))] + [_hbm_like(a) for a in ins]
        + [jax.ShapeDtypeStruct((8, LANES), F32)],
        input_output_aliases={a: 2 + a for a in range(n)},
        compiler_params=pltpu.CompilerParams(has_side_effects=_EFFECT),
    )(*[_in_hbm(a) for a in ins], after)
    return res[0], res[1], list(res[2:2 + n]), res[-1]


def _wait_call(body, name, thru, send_sems, recv_sems, after):
    n = len(thru)
    after = list(after) if isinstance(after, (list, tuple)) else [after]
    return pl.pallas_call(
        body, name=name, in_specs=[_HBM] * n + [_SEM, _SEM] + [_ANY] * len(after), out_specs=[_HBM] * n,
        out_shape=[_hbm_like(a) for a in thru], input_output_aliases={a: a for a in range(n)},
        compiler_params=pltpu.CompilerParams(has_side_effects=_EFFECT),
    )(*thru, send_sems, recv_sems, *after)


def gather_start(slabs, after, name="weight_gather_start"):
    n = len(slabs)

    def body(*refs):
        g_refs, send_sems, recv_sems, token = refs[:n], refs[n + 1], refs[n + 2], refs[-1]
        x, y, c, chips = _place()
        me = _chip_index(x, y)
        for a in range(n):
            rh = g_refs[a].shape[1] // 2
            mine = g_refs[a].at[me, pl.ds(c * rh, rh)]
            for j, chip in enumerate(chips):
                _remote(mine, mine, send_sems.at[3 * a + j], recv_sems.at[3 * a + j], (*chip, c)).start()
        token[...] = jnp.zeros_like(token)

    return _start_call(body, name, slabs, 3 * n, after)


def gather_wait(send_sems, recv_sems, thru, after, name="weight_gather_wait"):
    n = len(thru)

    def body(*refs):
        g_refs, send_sems, recv_sems = refs[:n], refs[n], refs[n + 1]
        x, y, c, chips = _place()
        me = _chip_index(x, y)
        for a in range(n):
            rh = g_refs[a].shape[1] // 2
            rows = pl.ds(c * rh, rh)
            for j, chip in enumerate(chips):
                mine, got = g_refs[a].at[me, rows], g_refs[a].at[_chip_index(*chip), rows]
                _remote(mine, mine, send_sems.at[3 * a + j], recv_sems.at[3 * a + j], (*chip, c)).wait_send()
                _remote(got, got, send_sems.at[3 * a + j], recv_sems.at[3 * a + j], (*chip, c)).wait_recv()

    return _wait_call(body, name, thru, send_sems, recv_sems, after)


def gather_forward(slabs, name="weight_gather_forward"):
    n = len(slabs)

    def body(*refs):
        in_refs, out_refs, send_sems, recv_sems = refs[:n], refs[n:2 * n], refs[-2], refs[-1]
        x, y, c, chips = _place()
        sib = (x, y, 1 - c)
        sends = []
        for a in range(n):
            rh = in_refs[a].shape[1] // 2
            for j, chip in enumerate(chips):
                k = _chip_index(*chip)
                cp = _remote(in_refs[a].at[k, pl.ds(c * rh, rh)], out_refs[a].at[k, pl.ds(c * rh, rh)], send_sems.at[3 * a + j],
                             recv_sems.at[3 * a + j], sib)
                cp.start()
                sends.append(cp)
        for a in range(n):
            rh = in_refs[a].shape[1] // 2
            for j, chip in enumerate(chips):
                got = out_refs[a].at[_chip_index(*chip), pl.ds((1 - c) * rh, rh)]
                _remote(got, got, send_sems.at[3 * a + j], recv_sems.at[3 * a + j], sib).wait_recv()
        for cp in sends:
            cp.wait_send()

    return _comm_call(body, name, slabs, [jax.ShapeDtypeStruct(s.shape, s.dtype) for s in slabs], 3 * n, {a: a for a in range(n)})


def exchange_start(parts, after, name="grad_exchange_start"):
    n = len(parts)

    def body(*refs):
        p_refs, land_refs, send_sems, recv_sems, token = refs[:n], refs[n:2 * n], refs[2 * n + 1], refs[2 * n + 2], refs[-1]
        x, y, c, chips = _place()
        me = _chip_index(x, y)
        for a in range(n):
            for j, chip in enumerate(chips):
                _remote(p_refs[a].at[_chip_index(*chip)], land_refs[a].at[me], send_sems.at[3 * a + j], recv_sems.at[3 * a + j],
                        (*chip, c)).start()
        token[...] = jnp.zeros_like(token)

    return _start_call(body, name, list(parts) + [lax.empty(p.shape, p.dtype) for p in parts], 3 * n, after)


def swap_start(slabs, after, name="grad_swap_start"):
    n = len(slabs)

    def body(*refs):
        g_refs, land_refs, send_sems, recv_sems, token = refs[:n], refs[n:2 * n], refs[2 * n + 1], refs[2 * n + 2], refs[-1]
        x, y, c, _ = _place()
        for a in range(n):
            rh = g_refs[a].shape[1] // 2
            _remote(g_refs[a].at[:, pl.ds((1 - c) * rh, rh), :], land_refs[a], send_sems.at[a], recv_sems.at[a], (x, y, 1 - c)).start()
        token[...] = jnp.zeros_like(token)

    lands = [lax.empty((s.shape[0], s.shape[1] // 2, s.shape[2]), s.dtype) for s in slabs]
    return _start_call(body, name, list(slabs) + lands, n, after)


def swap_wait(send_sems, recv_sems, thru, after, name="grad_swap_wait"):
    n = len(thru) // 2

    def body(*refs):
        g_refs, land_refs, send_sems, recv_sems = refs[:n], refs[n:2 * n], refs[2 * n], refs[2 * n + 1]
        x, y, c, _ = _place()
        for a in range(n):
            rh = g_refs[a].shape[1] // 2
            cp = _remote(g_refs[a].at[:, pl.ds((1 - c) * rh, rh), :], land_refs[a], send_sems.at[a], recv_sems.at[a], (x, y, 1 - c))
            cp.wait_send()
            cp.wait_recv()

    res = _wait_call(body, name, thru, send_sems, recv_sems, after)
    return res[:n], res[n:]


def exchange_wait(send_sems, recv_sems, thru, after, name="grad_exchange_wait"):
    n = len(thru) // 2

    def body(*refs):
        p_refs, land_refs, send_sems, recv_sems = refs[:n], refs[n:2 * n], refs[2 * n], refs[2 * n + 1]
        x, y, c, chips = _place()
        me = _chip_index(x, y)
        for a in range(n):
            for j, chip in enumerate(chips):
                k = _chip_index(*chip)
                _remote(p_refs[a].at[k], land_refs[a].at[me], send_sems.at[3 * a + j], recv_sems.at[3 * a + j], (*chip, c)).wait_send()
                _remote(land_refs[a].at[k], land_refs[a].at[k], send_sems.at[3 * a + j], recv_sems.at[3 * a + j], (*chip, c)).wait_recv()

    res = _wait_call(body, name, thru, send_sems, recv_sems, after)
    return res[:n], res[n:]


_SLABS = {
    "mla_w_in": [("mla_w_in", 1, 0, 2)], "mla_w_uq": [("mla_w_uq", 2, 0, 2)], "mla_w_ukv": [("mla_w_ukv", 2, 0, 2)],
    "l0_mla_w_o": [("mla_w_o", 1, 0, 1)],
    "l0_w1024": [("mlp_w1", 2, 0, 1), ("mlp_w2", 1, 0, 1), ("xa_w_q", 1, 0, 1), ("xa_w_o", 1, 0, 1)],
    "l0_xa_w_kv": [("xa_w_kv", 2, 0, 1)],
    "l1_w1024": [("mlp_w1", 2, 1, 2), ("mlp_w2", 1, 1, 2), ("xa_w_q", 1, 1, 2), ("xa_w_o", 1, 1, 2), ("gdn_w_o", 1, 0, 1)],
    "l1_xa_w_kv": [("xa_w_kv", 2, 1, 2)], "gdn_w_in": [("gdn_w_in", 2, 0, 1)],
    "l23_w1024": [("mlp_w1", 2, 2, 4), ("mlp_w2", 1, 2, 4), ("xa_w_q", 1, 2, 4), ("xa_w_o", 1, 2, 4), ("mla_w_o", 1, 1, 2),
                  ("sc_w_o", 1, 0, 1)],
    "l23_xa_w_kv": [("xa_w_kv", 2, 2, 4)], "sc_w_in": [("sc_w_in", 2, 0, 1)],
}
_GROUPS = [(["mla_w_in", "mla_w_uq", "mla_w_ukv", "l0_mla_w_o"], None),
           (["l0_w1024", "l0_xa_w_kv"], (0, "xa")),
           (["l1_w1024", "l1_xa_w_kv", "gdn_w_in"], (1, "mix")),
           (["l23_w1024", "l23_xa_w_kv", "sc_w_in"], (2, "mix"))]
_SWAP_DONE = {(2, "mix"): (1, "mlp"), (1, "mix"): (0, "mlp")}
_RELAID = ("mla_w_in", "mla_w_uq", "mla_w_ukv", "gdn_w_in")
_SMALL = [("mla_q_norm", 1), ("mla_kv_norm", 1), ("gdn_conv_w", 2), ("sc_conv_w", 2)]
_REPL = ["gdn_a_log", "gdn_dt_bias", "gdn_o_norm", "norm_mix", "norm_mem", "norm_mlp", "mem_norm", "final_norm"]
_WEIGHTS = ['mla_w_in', 'mla_q_norm', 'mla_kv_norm', 'mla_w_uq', 'mla_w_ukv', 'mla_w_o', 'gdn_w_in', 'gdn_conv_w',
            'gdn_a_log', 'gdn_dt_bias', 'gdn_o_norm', 'gdn_w_o', 'sc_w_in', 'sc_conv_w', 'sc_w_o', 'norm_mix',
            'norm_mem', 'norm_mlp', 'xa_w_q', 'xa_w_kv', 'xa_w_o', 'mlp_w1', 'mlp_w2', 'mem_norm', 'final_norm']


class Layout:
    def __init__(self, shard_shapes):
        self.members, self.where, self.slab_dims = {}, {}, {}
        for slab, members in _SLABS.items():
            off, rows = 0, []
            for name, axis, l0, l1 in members:
                _, rpl, width = shard_shapes[name]
                rows.append((name, off, l0, l1, rpl))
                for layer in range(l0, l1):
                    self.where[(name, layer)] = (slab, off + (layer - l0) * rpl, rpl, width, axis)
                off += (l1 - l0) * rpl
            self.members[slab], self.slab_dims[slab] = rows, (off, width)

    def new_slabs(self, dtype):
        return {s: Slab(rows, width, dtype) for s, (rows, width) in self.slab_dims.items()}

    def loc(self, slabs, name, layer):
        slab, row0, rpl, width, axis = self.where[(name, layer)]
        if axis == 1:
            return Loc(slabs[slab], row0, N_CHIPS * rpl, width, 0)
        return Loc(slabs[slab], row0, rpl, N_CHIPS * width, 1)

    def _whole(self, name):
        (member,) = self.members[name]
        _, off, l0, l1, rpl = member
        assert off == 0 and l0 == 0
        return l1, rpl, self.slab_dims[name][1], dict((n, a) for n, a, _, _ in _SLABS[name])[name]

    def full(self, slabs, name):
        layers, rpl, width, axis = self._whole(name)
        blocks = slabs[name].arr.reshape(N_CHIPS, layers, rpl, width)
        return jnp.concatenate([blocks[s] for s in range(N_CHIPS)], axis=axis)

    def put_full(self, slabs, name, grad):
        layers, rpl, width, axis = self._whole(name)
        parts = jnp.stack(jnp.split(grad, N_CHIPS, axis=axis)).reshape(N_CHIPS, layers * rpl, width)
        slabs[name].arr = parts.astype(slabs[name].dtype)


def _small_pack(vals, names):
    flat = jnp.concatenate([vals[n].astype(F32).reshape(-1) for n in names])
    return jnp.pad(flat, (0, SMALL_ROWS * SMALL_COLS - flat.shape[0])).reshape(SMALL_ROWS, SMALL_COLS)


def _small_unpack(flat, like, names):
    out, off = {}, 0
    flat = flat.reshape(-1)
    for n in names:
        out[n] = flat[off:off + like[n].size].reshape(like[n].shape)
        off += like[n].size
    return out


_MLA_CFG = _Attn(MLA_H, 2 * LANES, MLA_NOPE, MLA_V, True, (MLA_NOPE + MLA_ROPE) ** -0.5, hp=8, hp_kv=8, blk=512)
_XA_CFG = _Attn(XA_H, XA_D, XA_D, XA_D, False, XA_D ** -0.5, hp=4, hp_kv=4)


def _mla_weights(w_in, w_uq, w_ukv):
    w_in_p = jnp.pad(w_in, ((0, 0), (0, MLA_ZPAD - w_in.shape[1])))
    w_uq_p = jnp.pad(w_uq.reshape(MLA_QR, MLA_H, MLA_NOPE + MLA_ROPE), ((0, 0), (0, 0), (0, 2 * LANES - MLA_NOPE - MLA_ROPE)))
    w_uq_p = w_uq_p.reshape(MLA_QR, MLA_H * 2 * LANES)
    kv = w_ukv.reshape(MLA_KVR, MLA_H, MLA_NOPE + MLA_V)
    w_ukv_p = jnp.concatenate([kv[:, :, :MLA_NOPE].reshape(MLA_KVR, -1), kv[:, :, MLA_NOPE:].reshape(MLA_KVR, -1)], axis=1)
    return w_in_p, w_uq_p, w_ukv_p


def _mla_weight_grads(d_in_p, d_uq_p, d_ukv_p):
    d_in = d_in_p[:, :MLA_QR + MLA_KVR + MLA_ROPE]
    d_uq = d_uq_p.reshape(MLA_QR, MLA_H, 2 * LANES)[:, :, :MLA_NOPE + MLA_ROPE].reshape(MLA_QR, -1)
    half = MLA_H * MLA_NOPE
    d_ukv = jnp.concatenate([d_ukv_p[:, :half].reshape(MLA_KVR, MLA_H, MLA_NOPE),
                             d_ukv_p[:, half:].reshape(MLA_KVR, MLA_H, MLA_V)], axis=2).reshape(MLA_KVR, -1)
    return d_in, d_uq, d_ukv


def _mla_fwd(xs, h, wts, w_o, qn, kvn, tabs, g_next, tag):
    w_in_p, w_uq_p, w_ukv_p = wts
    z = mm(h, w_in_p, "nn", f"{tag}_in")
    cq, ckv, kr = mla_mid_fwd(z, qn, kvn, tabs, f"{tag}_mid")
    q = mm(cq, w_uq_p, "nn", f"{tag}_uq", outs=(BF16,), epi=_epi_rope_q, per_row=tabs, tm=512)
    kv = mm(ckv, w_ukv_p, "nn", f"{tag}_ukv", outs=(BF16,))
    o, lse = flash_fwd(_MLA_CFG, q, kv, kv, kr, f"{tag}_attn")
    xs, h_next = residual_norm(o, w_o, xs, g_next, f"{tag}_out")
    return xs, h_next, (z, cq, ckv, kr, q, kv, o, lse)


def _mla_bwd(dx, h, wts, w_o, g_wo, qn, kvn, tabs, saved, tag):
    w_in_p, w_uq_p, w_ukv_p = wts
    z, cq, ckv, kr, q, kv, o, lse = saved
    mm(o, dx, "tn", f"{tag}_dwo", outs=(BF16,), out_loc=g_wo)
    do = mm(dx, w_o, "nt", f"{tag}_do", outs=(BF16,))
    dqp, delta = flash_dq(_MLA_CFG, q, kv, kv, kr, o, do, lse, BF16, f"{tag}_attn_dq", rope_tabs=tabs)
    dkv, dkr = flash_dkv(_MLA_CFG, q, kv, kv, kr, do, lse, delta, BF16, f"{tag}_attn_dkv")
    d_uq_p = mm(cq, dqp, "tn", f"{tag}_duq")
    dcq = mm(dqp, w_uq_p, "nt", f"{tag}_dcq")
    d_ukv_p = mm(ckv, dkv, "tn", f"{tag}_dukv")
    dckv = mm(dkv, w_ukv_p, "nt", f"{tag}_dckv")
    dz, dqn, dkvn = mla_mid_bwd(z, qn, kvn, tabs, dcq, dckv, dkr, f"{tag}_mid_bwd")
    d_in_p = mm(h, dz, "tn", f"{tag}_din")
    dh = (dz, w_in_p)
    d_in, d_uq, d_ukv = _mla_weight_grads(d_in_p, d_uq_p, d_ukv_p)
    return dh, dict(mla_w_in=d_in, mla_w_uq=d_uq, mla_w_ukv=d_ukv, mla_q_norm=dqn, mla_kv_norm=dkvn)


_GDN_QKV = 3 * GDN_H * GDN_D
_GDN_GATE_END = _GDN_QKV + GDN_H * GDN_D


def _gdn_weights(w_in):
    rep = lambda cols: jnp.repeat(cols, GDN_D, axis=1)
    return jnp.concatenate([w_in[:, :_GDN_GATE_END], rep(w_in[:, _GDN_GATE_END:_GDN_GATE_END + GDN_H]),
                            rep(w_in[:, _GDN_GATE_END + GDN_H:])], axis=1)


def _fold(x):
    return x.reshape(x.shape[0], -1, GDN_D).sum(-1)


def _gdn_fwd(xs, h, w_in_x, conv_w, a_log, dt_bias, o_norm, w_o, g_next, tag):
    z = mm(h, w_in_x, "nn", f"{tag}_in")
    qkv = gdn_conv_fwd(z, conv_w, f"{tag}_conv")
    a_x, dt_x = jnp.repeat(a_log.reshape(1, -1), GDN_D, axis=1), jnp.repeat(dt_bias.reshape(1, -1), GDN_D, axis=1)
    og, states, t_invs = gdn_chunk_fwd(qkv, z, a_x, dt_x, o_norm.reshape(1, -1), f"{tag}_chunks")
    xs, h_next = residual_norm(og, w_o, xs, g_next, f"{tag}_out")
    return xs, h_next, (z, qkv, a_x, dt_x, og, states, t_invs)


def _gdn_weights_compact(w_in):
    return jnp.pad(w_in, ((0, 0), (0, LANES - 2 * GDN_H)))


def _gdn_bwd(dx, h, w_in_c, conv_w, o_norm, w_o, g_wo, saved, tag):
    z, qkv, a_x, dt_x, og, states, t_invs = saved
    mm(og, dx, "tn", f"{tag}_dwo", outs=(BF16,), out_loc=g_wo)
    dog = mm(dx, w_o, "nt", f"{tag}_dog")
    dqkv, dgate, dba, da_x, ddt_x, don = gdn_chunk_bwd(qkv, z, a_x, dt_x, o_norm.reshape(1, -1), states, t_invs, dog,
                                                       f"{tag}_chunks_bwd")
    dpre, dconv = gdn_conv_bwd(z, conv_w, dqkv, f"{tag}_conv_bwd")
    dz = jnp.concatenate([dpre, dgate, dba], axis=1)
    d_in_c = mm(h, dz, "tn", f"{tag}_din")
    dh = (dz, w_in_c)
    return dh, dict(gdn_w_in=d_in_c[:, :_GDN_GATE_END + 2 * GDN_H], gdn_conv_w=dconv, gdn_a_log=_fold(da_x).reshape(-1),
                    gdn_dt_bias=_fold(ddt_x).reshape(-1), gdn_o_norm=don.reshape(-1))


def _sc_fwd(xs, h, w_in, conv_w, w_o, g_next, tag):
    z = mm(h, w_in, "nn", f"{tag}_in")
    y = sc_fwd(z, conv_w, f"{tag}_conv")
    xs, h_next = residual_norm(y, w_o, xs, g_next, f"{tag}_out")
    return xs, h_next, (z, y)


def _sc_bwd(dx, h, w_in, g_win, conv_w, w_o, g_wo, saved, tag):
    z, y = saved
    mm(y, dx, "tn", f"{tag}_dwo", outs=(BF16,), out_loc=g_wo)
    dy = mm(dx, w_o, "nt", f"{tag}_dy")
    db, dc, du, dconv = sc_bwd(z, conv_w, dy, f"{tag}_conv_bwd")
    dz = jnp.concatenate([db, dc, du], axis=1)
    mm(h, dz, "tn", f"{tag}_din", outs=(BF16,), out_loc=g_win)
    dh = (dz, w_in)
    return dh, dict(sc_conv_w=dconv)


def local_step(x, mem, pos, target, lay, wslabs, gslabs, small, before=None, after_bwd=None):
    depth = small["norm_mix"].shape[0]
    W = lambda name, layer: lay.loc(wslabs, name, layer)
    G = lambda name, layer: lay.loc(gslabs, name, layer)
    tabs = rope_tables(pos)
    mem_n = rmsnorm_fwd(mem, small["mem_norm"], "mem_norm")
    full = {n: lay.full(wslabs, n) for n in ("mla_w_in", "mla_w_uq", "mla_w_ukv")}
    mla_w = [_mla_weights(full["mla_w_in"][j], full["mla_w_uq"][j], full["mla_w_ukv"][j]) for j in range(full["mla_w_in"].shape[0])]
    gdn_in_x, gdn_in_c = {}, {}

    xs, h_pre = x, None
    saved = []
    for i in range(depth):
        j, kind = i // 3, i % 3
        tag = f"l{i}"
        if before is not None:
            xs = before(i, "mix", xs)
        if kind == 1:
            gdn_full = lay.full(wslabs, "gdn_w_in")[j]
            gdn_in_x[j], gdn_in_c[j] = _gdn_weights(gdn_full), _gdn_weights_compact(gdn_full)
        x_a = xs
        h = h_pre if h_pre is not None else rmsnorm_fwd(xs, small["norm_mix"][i], f"{tag}_norm_mix")
        g_mem = small["norm_mem"][i]
        if kind == 0:
            xs, hn, mix = _mla_fwd(xs, h, mla_w[j], W("mla_w_o", j), small["mla_q_norm"][j], small["mla_kv_norm"][j], tabs, g_mem,
                                   f"{tag}_mla")
        elif kind == 1:
            xs, hn, mix = _gdn_fwd(xs, h, gdn_in_x[j], small["gdn_conv_w"][j], small["gdn_a_log"][j], small["gdn_dt_bias"][j],
                                   small["gdn_o_norm"][j], W("gdn_w_o", j), g_mem, f"{tag}_gdn")
        else:
            xs, hn, mix = _sc_fwd(xs, h, W("sc_w_in", j), small["sc_conv_w"][j], W("sc_w_o", j), g_mem, f"{tag}_sc")
        if before is not None:
            xs = before(i, "xa", xs)
        x_b = xs
        xq = mm(hn, W("xa_w_q", i), "nn", f"{tag}_xa_q", outs=(BF16,))
        xkv = mm(mem_n, W("xa_w_kv", i), "nn", f"{tag}_xa_kv", outs=(BF16,))
        xo, xlse = flash_fwd(_XA_CFG, xq, xkv, xkv, None, f"{tag}_xa_attn")
        xs, hm = residual_norm(xo, W("xa_w_o", i), xs, small["norm_mlp"][i], f"{tag}_xa_out")
        x_c = xs
        h1, act = mm(hm, W("mlp_w1", i), "nn", f"{tag}_mlp_up", outs=(BF16, BF16), epi=_epi_relu2)
        xs, h_pre = residual_norm(act, W("mlp_w2", i), xs, small["norm_mix"][i + 1] if i + 1 < depth else None,
                                  f"{tag}_mlp_down", tm=512)
        saved.append((x_a, h, mix, x_b, hn, xq, xkv, xo, xlse, x_c, hm, h1, act))

    se, dx, d_final = loss_head(xs, small["final_norm"], target)

    per_layer = {n: [None] * depth for n in ("norm_mix", "norm_mem", "norm_mlp")}
    mixer = {}
    dmem_n = jnp.zeros(mem.shape, F32)
    for i in reversed(range(depth)):
        j, kind = i // 3, i % 3
        tag = f"l{i}"
        x_a, h, mix, x_b, hn, xq, xkv, xo, xlse, x_c, hm, h1, act = saved[i]
        mm(act, dx, "tn", f"{tag}_mlp_dw2", outs=(BF16,), out_loc=G("mlp_w2", i))
        dh1 = mm(dx, W("mlp_w2", i), "nt", f"{tag}_mlp_dh1", outs=(BF16,), epi=_epi_relu2_bwd, extras=(h1,))
        mm(hm, dh1, "tn", f"{tag}_mlp_dw1", outs=(BF16,), out_loc=G("mlp_w1", i))
        dx, dg = mm(dh1, W("mlp_w1", i), "nt", f"{tag}_mlp_dhm", epi=_epi_norm_bwd, extras=(x_c, dx), vecs=(small["norm_mlp"][i],),
                    row_outs=1, tm=512)
        per_layer["norm_mlp"][i] = dg.reshape(-1)
        if after_bwd is not None:
            dx = after_bwd(i, "mlp", dx)
        mm(xo, dx, "tn", f"{tag}_xa_dwo", outs=(BF16,), out_loc=G("xa_w_o", i))
        dxo = mm(dx, W("xa_w_o", i), "nt", f"{tag}_xa_do", outs=(BF16,))
        dxq, xdelta = flash_dq(_XA_CFG, xq, xkv, xkv, None, xo, dxo, xlse, BF16, f"{tag}_xa_attn_dq")
        (dxkv,) = flash_dkv(_XA_CFG, xq, xkv, xkv, None, dxo, xlse, xdelta, BF16, f"{tag}_xa_attn_dkv")
        mm(hn, dxq, "tn", f"{tag}_xa_dwq", outs=(BF16,), out_loc=G("xa_w_q", i))
        dx, dg = mm(dxq, W("xa_w_q", i), "nt", f"{tag}_xa_dhn", epi=_epi_norm_bwd, extras=(x_b, dx), vecs=(small["norm_mem"][i],),
                    row_outs=1, tm=512)
        per_layer["norm_mem"][i] = dg.reshape(-1)
        mm(mem_n, dxkv, "tn", f"{tag}_xa_dwkv", outs=(BF16,), out_loc=G("xa_w_kv", i))
        dmem_n = mm(dxkv, W("xa_w_kv", i), "nt", f"{tag}_xa_dmem", epi=_epi_add, extras=(dmem_n,))
        if after_bwd is not None:
            dx = after_bwd(i, "xa", dx)
        if kind == 0:
            dh, gr = _mla_bwd(dx, h, mla_w[j], W("mla_w_o", j), G("mla_w_o", j), small["mla_q_norm"][j], small["mla_kv_norm"][j],
                              tabs, mix, f"{tag}_mla")
        elif kind == 1:
            dh, gr = _gdn_bwd(dx, h, gdn_in_c[j], small["gdn_conv_w"][j], small["gdn_o_norm"][j], W("gdn_w_o", j), G("gdn_w_o", j),
                              mix, f"{tag}_gdn")
        else:
            dh, gr = _sc_bwd(dx, h, W("sc_w_in", j), G("sc_w_in", j), small["sc_conv_w"][j], W("sc_w_o", j), G("sc_w_o", j),
                             mix, f"{tag}_sc")
        if kind == 1:
            lay.put_full(gslabs, "gdn_w_in", gr.pop("gdn_w_in")[None])
        for n, g in gr.items():
            mixer.setdefault(n, {})[j] = g
        dz_mix, w_mix = dh
        dx, dg = mm(dz_mix, w_mix, "nt", f"{tag}_mix_dh", epi=_epi_norm_bwd, extras=(x_a, dx), vecs=(small["norm_mix"][i],),
                    row_outs=1, tm=256 if kind == 1 else 512)
        per_layer["norm_mix"][i] = dg.reshape(-1)
        if after_bwd is not None:
            dx = after_bwd(i, "mix", dx)

    _, d_mem_norm = rmsnorm_bwd(mem, small["mem_norm"], dmem_n, jnp.zeros(mem.shape, F32), "mem_norm_bwd")
    grads = {n: jnp.stack(v) for n, v in per_layer.items()}
    for n, by_j in mixer.items():
        grads[n] = jnp.stack([by_j[j] for j in sorted(by_j)])
    grads["mem_norm"] = d_mem_norm
    grads["final_norm"] = d_final
    for n in ("mla_w_in", "mla_w_uq", "mla_w_ukv"):
        lay.put_full(gslabs, n, grads.pop(n))
    return se, dx, grads


def kernel(x, mem, positions, mla_w_in, mla_q_norm, mla_kv_norm, mla_w_uq, mla_w_ukv, mla_w_o, gdn_w_in, gdn_conv_w, gdn_a_log, gdn_dt_bias, gdn_o_norm, gdn_w_o, sc_w_in, sc_conv_w, sc_w_o, norm_mix, norm_mem, norm_mlp, xa_w_q, xa_w_kv, xa_w_o, mlp_w1, mlp_w2, mem_norm, final_norm, loss_target, m_mla_w_in, m_mla_q_norm, m_mla_kv_norm, m_mla_w_uq, m_mla_w_ukv, m_mla_w_o, m_gdn_w_in, m_gdn_conv_w, m_gdn_a_log, m_gdn_dt_bias, m_gdn_o_norm, m_gdn_w_o, m_sc_w_in, m_sc_conv_w, m_sc_w_o, m_norm_mix, m_norm_mem, m_norm_mlp, m_xa_w_q, m_xa_w_kv, m_xa_w_o, m_mlp_w1, m_mlp_w2, m_mem_norm, m_final_norm, v_mla_w_in, v_mla_q_norm, v_mla_kv_norm, v_mla_w_uq, v_mla_w_ukv, v_mla_w_o, v_gdn_w_in, v_gdn_conv_w, v_gdn_a_log, v_gdn_dt_bias, v_gdn_o_norm, v_gdn_w_o, v_sc_w_in, v_sc_conv_w, v_sc_w_o, v_norm_mix, v_norm_mem, v_norm_mlp, v_xa_w_q, v_xa_w_kv, v_xa_w_o, v_mlp_w1, v_mlp_w2, v_mem_norm, v_final_norm):
    given = dict(locals())
    p = {n: given[n] for n in _WEIGHTS}
    mom = {n: given["m_" + n] for n in _WEIGHTS}
    var = {n: given["v_" + n] for n in _WEIGHTS}
    split = sorted({n for members in _SLABS.values() for n, _, _, _ in members})
    lay = Layout({n: p[n].shape for n in split})
    flat2d = lambda a: a.reshape(-1, a.shape[-1])

    me = (2 * lax.axis_index("x") + lax.axis_index("y")).astype(jnp.int32)
    core = lax.axis_index("c").astype(jnp.int32)
    me1, c1, mc = me.reshape(1), core.reshape(1), jnp.stack([me, core])

    wslabs = lay.new_slabs(BF16)

    def cast_group(slabs, chip):
        for slab in slabs:
            for name, off, l0, l1, rpl in lay.members[slab]:
                cast_into(flat2d(p[name]), l0 * rpl, (l1 - l0) * rpl, wslabs[slab], off, chip, f"cast_{slab}_{name}")

    first = _GROUPS[0][0]
    cast_group(first, me1)
    small_names = [n for n, _ in _SMALL]
    words = lax.bitcast_convert_type(jnp.concatenate([p[n].reshape(-1) for n in small_names]), BF16).reshape(-1)
    words = jnp.pad(words, (0, SMALL_ROWS * SMALL_COLS - words.shape[0])).reshape(1, SMALL_ROWS, SMALL_COLS)
    small_slab = lax.dynamic_update_slice(jnp.zeros((N_CHIPS, SMALL_ROWS, SMALL_COLS), BF16), words, (me, 0, 0))

    send0, recv0, thru0, token = gather_start([wslabs[s].arr for s in first] + [small_slab], me1, "weight_gather_start_first")
    in_flight = {}
    for slabs, point in _GROUPS[1:]:
        cast_group(slabs, me1 + token[0, 0].astype(jnp.int32))
        send, recv, thru, token = gather_start([wslabs[s].arr for s in slabs], token, f"weight_gather_start_{slabs[0]}")
        in_flight[point] = (send, recv, thru, slabs)
    started_token = token
    landed = gather_wait(send0, recv0, thru0, started_token, "weight_gather_wait_first")
    gathered = gather_forward(landed, "weight_gather_forward_first")
    for s, arr in zip(first, gathered):
        wslabs[s].arr = arr

    def before(i, stage, xs):
        if (i, stage) in in_flight:
            send, recv, thru, slabs = in_flight[(i, stage)]
            landed = gather_wait(send, recv, thru, xs, f"weight_gather_wait_{slabs[0]}")
            for s, arr in zip(slabs, gather_forward(landed, f"weight_gather_forward_{slabs[0]}")):
                wslabs[s].arr = arr
        return xs

    small = {n: p[n] for n in _REPL}
    got, off = gathered[-1].reshape(N_CHIPS, -1), 0
    for n, ax in _SMALL:
        vals = lax.bitcast_convert_type(got[:, off:off + 2 * p[n].size].reshape(N_CHIPS, p[n].size, 2), F32)
        vals = vals.reshape((N_CHIPS,) + p[n].shape)
        small[n] = jnp.concatenate([vals[s] for s in range(N_CHIPS)], axis=ax)
        off += 2 * p[n].size

    gslabs = lay.new_slabs(BF16)
    complete_at = {point: slabs for slabs, point in _GROUPS[1:]}
    swapping, exchanging = {}, []

    def after_bwd(i, stage, dx):
        if (i, stage) in swapping:
            slabs, send, recv, thru = swapping.pop((i, stage))
            g, swapped = swap_wait(send, recv, thru, dx, f"grad_swap_wait_{slabs[0]}")
        elif (i, stage) in complete_at:
            slabs = complete_at[(i, stage)]
            g = [gslabs[s].arr for s in slabs]
            if (i, stage) in _SWAP_DONE:
                send, recv, thru, token = swap_start(g, c1, f"grad_swap_start_{slabs[0]}")
                swapping[_SWAP_DONE[(i, stage)]] = (slabs, send, recv, thru)
                return dx + token[0, 0]
            swapped = pair_swap_halves(g, f"grad_pair_swap_{slabs[0]}")
        else:
            return dx
        part = [pair_add(a, b, c1, f"pair_add_{s}") for a, b, s in zip(g, swapped, slabs)]
        send, recv, thru, token = exchange_start(part, c1, f"grad_exchange_start_{slabs[0]}")
        exchanging.append((slabs, send, recv, thru))
        return dx + token[0, 0]

    se, dx, sgrads = local_step(x[0], mem[0], positions.reshape(-1, 1), loss_target[0], lay, wslabs, gslabs, small,
                                before, after_bwd)
    loss = lax.psum(0.5 * jnp.sum(se) / x.shape[-1], ("x", "y", "c"))
    names, parts, received = [], [], []
    for slabs, send, recv, thru in exchanging:
        part, got = exchange_wait(send, recv, thru, dx, f"grad_exchange_wait_{slabs[0]}")
        names, parts, received = names + slabs, parts + list(part), received + list(got)

    axes = dict(_SMALL)
    small_order = small_names + _REPL
    slots = []
    for s in range(N_CHIPS):
        vals = {n: (lax.slice_in_dim(g, s * p[n].shape[axes[n]], (s + 1) * p[n].shape[axes[n]], axis=axes[n]) if n in axes else g)
                for n, g in sgrads.items()}
        slots.append(_small_pack(vals, small_order))
    g_last = [gslabs[s].arr for s in first] + [jnp.stack(slots).astype(BF16)]
    names_last = first + ["small"]
    swapped_last = pair_swap_halves(g_last, "grad_pair_swap_last")
    part_last = [pair_add(g, b, c1, f"pair_add_{s}") for g, b, s in zip(g_last, swapped_last, names_last)]
    send, recv, thru, token = exchange_start(part_last, c1, "grad_exchange_start_last")
    mc_after = mc + token[0, 0].astype(jnp.int32)
    halves = [chip_sum(q, r, mc_after, f"chip_sum_{s}") for q, r, s in zip(parts, received, names)]
    part_last, got_last = exchange_wait(send, recv, thru, list(halves), "grad_exchange_wait_last")
    halves += [chip_sum(q, r, mc, f"chip_sum_{s}") for q, r, s in zip(part_last, got_last, names_last)]
    reduced = dict(zip(names + names_last, pair_join_halves(halves)))

    res = {}
    for slab in _SLABS:
        for name, off, l0, l1, rpl in lay.members[slab]:
            res[name] = adamw(reduced[slab], off, flat2d(p[name]), flat2d(mom[name]), flat2d(var[name]), l0 * rpl, (l1 - l0) * rpl,
                              res.get(name), f"adamw_{slab}_{name}")
    for name in split:
        res[name] = [o.reshape(p[name].shape) for o in res[name]]
    sp = {k: _small_pack(d, small_order) for k, d in (("w", p), ("m", mom), ("v", var))}
    outs = adamw(reduced["small"], 0, sp["w"], sp["m"], sp["v"], 0, SMALL_ROWS, None, "adamw_small")
    unpacked = [_small_unpack(o, p, small_order) for o in outs]
    for n in small_order:
        res[n] = [u[n] for u in unpacked]
    return (loss, dx[None], *[res[n][k] for k in range(4) for n in _WEIGHTS])
```

```python
import jax
import jax.numpy as jnp
from jax import lax
from jax.experimental import pallas as pl
from jax.experimental.pallas import tpu as pltpu

F32 = jnp.float32
BF16 = jnp.bfloat16
MESH = pl.DeviceIdType.MESH

EPS = 1e-6
ROPE_THETA = 10000.0
N_CHIPS = 4
LANES = 128
VMEM_LIMIT = 56 * 1024 * 1024
NEG = -1e30

MLA_H, MLA_NOPE, MLA_ROPE, MLA_V = 8, 128, 64, 128
MLA_QR, MLA_KVR = 384, 256
MLA_ZPAD = 768
GDN_H, GDN_D, GDN_C = 8, 128, 64
XA_H, XA_D = 4, 256

ADAM_LR, ADAM_B1, ADAM_B2, ADAM_EPS, ADAM_WD, ADAM_STEP = 0.001, 0.9, 0.999, 1e-08, 0.01, 10

SMALL_ROWS, SMALL_COLS = 32, 1024


def _cparams(sem=None):
    return pltpu.CompilerParams(dimension_semantics=sem, vmem_limit_bytes=VMEM_LIMIT)


def _pick(dim, pref):
    t = (min(pref, dim) // LANES) * LANES
    while t >= LANES:
        if dim % t == 0:
            return t
        t -= LANES
    return dim


def _pick_rows(rows, pref, *offsets):
    t = (min(pref, rows) // 16) * 16
    while t > 16 and (rows % t or any(o % t for o in offsets)):
        t -= 16
    return t


class Slab:
    def __init__(self, rows, width, dtype, arr=None):
        self.shape, self.dtype, self.arr = (N_CHIPS, rows, width), dtype, arr


class Loc:
    def __init__(self, slab, row0, K, N, axis):
        self.slab, self.row0, self.K, self.N, self.axis = slab, row0, K, N, axis
        self.Ks = K // N_CHIPS if axis == 0 else K
        self.Ns = N // N_CHIPS if axis == 1 else N

    def tile_spec(self, tr, tc, rc):
        assert self.row0 % tr == 0 and self.Ks % tr == 0 and self.Ns % tc == 0, (self.row0, self.Ks, self.Ns, tr, tc)
        r0, rb, cb = self.row0 // tr, self.Ks // tr, self.Ns // tc
        if self.axis == 0:
            return pl.BlockSpec((None, tr, tc), lambda i, j: (rc(i, j)[0] // rb, r0 + rc(i, j)[0] % rb, rc(i, j)[1]))
        return pl.BlockSpec((None, tr, tc), lambda i, j: (rc(i, j)[1] // cb, r0 + rc(i, j)[0], rc(i, j)[1] % cb))

    def slot_spec(self, slot, tr, tc, rc):
        assert self.row0 % tr == 0, (self.row0, tr)
        r0 = self.row0 // tr
        return pl.BlockSpec((None, tr, tc), lambda i, j: (slot, r0 + rc(i, j)[0], rc(i, j)[1]))


_DIMS = {"nn": ((1,), (0,)), "nt": ((1,), (1,)), "tn": ((0,), (0,))}
_ANY = pl.BlockSpec(memory_space=pl.ANY)


def mm(a, b, mode, name, outs=(F32,), epi=None, extras=(), tm=1024, tn=1024, out_loc=None, vecs=(), row_outs=0, per_row=()):
    full_rows = bool(vecs) or row_outs > 0 or bool(per_row)
    b_loc = b if isinstance(b, Loc) else None
    if mode == "nn":
        M, K = a.shape
        K2, N = (b_loc.K, b_loc.N) if b_loc else b.shape
    elif mode == "nt":
        M, K = a.shape
        N, K2 = (b_loc.K, b_loc.N) if b_loc else b.shape
    else:
        K, M = a.shape
        K2, N = b.shape
    assert K == K2, (name, a.shape, K2, N)
    tm = _pick(out_loc.Ks if (out_loc and out_loc.axis == 0) else M, tm)
    n_split = full_rows and b_loc is not None and mode == "nt" and b_loc.axis == 0
    if out_loc is not None and out_loc.axis == 1:
        tn = _pick(out_loc.Ns, tn)
    elif n_split:
        tn = N
    elif b_loc is not None and ((mode == "nn" and b_loc.axis == 1) or (mode == "nt" and b_loc.axis == 0)):
        tn = _pick(b_loc.Ns if mode == "nn" else b_loc.Ks, tn)
    elif b_loc is not None:
        tn = N if full_rows else _pick(N, min(tn, 512))
    else:
        tn = N if full_rows else _pick(N, tn)
    assert tn == N or not full_rows, name

    parts = 1
    if mode == "tn":
        a_spec = pl.BlockSpec((K, tm), lambda i, j: (0, i))
        b_specs, b_args = [pl.BlockSpec((K, tn), lambda i, j: (0, j))], [b]
    else:
        a_spec = pl.BlockSpec((tm, K), lambda i, j: (i, 0))
        if b_loc is None:
            b_specs = [pl.BlockSpec((K, tn), lambda i, j: (0, j)) if mode == "nn" else pl.BlockSpec((tn, K), lambda i, j: (j, 0))]
            b_args = [b]
        elif mode == "nn" and b_loc.axis == 1:
            b_specs, b_args = [b_loc.tile_spec(K, tn, lambda i, j: (0, j))], [b_loc.slab.arr]
        elif n_split:
            b_specs = [b_loc.slot_spec(s, b_loc.Ks, K, lambda i, j: (0, 0)) for s in range(N_CHIPS)]
            b_args = [b_loc.slab.arr] * N_CHIPS
        elif mode == "nt" and b_loc.axis == 0:
            b_specs, b_args = [b_loc.tile_spec(tn, K, lambda i, j: (j, 0))], [b_loc.slab.arr]
        elif mode == "nn":
            parts = N_CHIPS
            b_specs = [b_loc.slot_spec(s, b_loc.Ks, tn, lambda i, j: (0, j)) for s in range(parts)]
            b_args = [b_loc.slab.arr] * parts
        else:
            parts = N_CHIPS
            b_specs = [b_loc.slot_spec(s, tn, b_loc.Ns, lambda i, j: (j, 0)) for s in range(parts)]
            b_args = [b_loc.slab.arr] * parts
    kp = K // parts
    n_b = N_CHIPS if n_split else parts
    n_ex, n_out = len(extras) + len(per_row) + len(vecs), len(outs)
    dims = (_DIMS[mode], ((), ()))

    def body(*refs):
        a_ref = refs[0]
        b_refs = refs[1:1 + n_b]
        ex_refs = refs[1 + n_b:1 + n_b + n_ex]
        o_refs = refs[len(refs) - n_out - row_outs:len(refs) - row_outs]
        r_refs = refs[len(refs) - row_outs:]
        acc = None
        if n_split:
            av = a_ref[...].astype(BF16)
            acc = jnp.concatenate([lax.dot_general(av, b_ref[...].astype(BF16), dims, preferred_element_type=F32)
                                   for b_ref in b_refs], axis=1)
        for s in range(0 if n_split else parts):
            av = a_ref[...] if parts == 1 else a_ref[:, s * kp:(s + 1) * kp]
            d = lax.dot_general(av.astype(BF16), b_refs[s][...].astype(BF16), dims, preferred_element_type=F32)
            acc = d if acc is None else acc + d
        res = epi(acc, *[e[...] for e in ex_refs]) if epi is not None else (acc,)
        for o_ref, v in zip(o_refs, res[:n_out]):
            o_ref[...] = v.astype(o_ref.dtype)
        for r_ref, v in zip(r_refs, res[n_out:]):
            @pl.when(pl.program_id(0) == 0)
            def _():
                r_ref[...] = jnp.zeros_like(r_ref)

            r_ref[...] += v

    mn_spec = pl.BlockSpec((tm, tn), lambda i, j: (i, j))
    row_spec = pl.BlockSpec((1, tn), lambda i, j: (0, j))
    in_specs = ([a_spec] + b_specs + [mn_spec] * len(extras) + [pl.BlockSpec((tm, r.shape[1]), lambda i, j: (i, 0)) for r in per_row]
                + [row_spec] * len(vecs))
    args = [a] + b_args + list(extras) + list(per_row) + [v.reshape(1, N) for v in vecs]
    aliases = {}
    if out_loc is None:
        out_specs = [mn_spec] * n_out + [row_spec] * row_outs
        out_shape = [jax.ShapeDtypeStruct((M, N), d) for d in outs] + [jax.ShapeDtypeStruct((1, N), F32)] * row_outs
    else:
        assert n_out == 1 and mode == "tn"
        out_specs = [out_loc.tile_spec(tm, tn, lambda i, j: (i, j))]
        out_shape = [jax.ShapeDtypeStruct(out_loc.slab.shape, out_loc.slab.dtype)]
        if out_loc.slab.arr is not None:
            in_specs.append(_ANY)
            args.append(out_loc.slab.arr)
            aliases = {len(args) - 1: 0}

    res = pl.pallas_call(
        body, name=name, grid=(M // tm, N // tn), in_specs=in_specs, out_specs=out_specs, out_shape=out_shape,
        input_output_aliases=aliases, compiler_params=_cparams(("arbitrary" if row_outs else "parallel", "parallel")),
    )(*args)
    if out_loc is not None:
        out_loc.slab.arr = res[0]
        return None
    return res[0] if len(res) == 1 else tuple(res)


def _epi_add(acc, r):
    return (acc + r,)


def _epi_add_norm(acc, r, g):
    x = acc + r
    return x, _rms(x, g)


def _epi_norm_bwd(acc, x, dx_in, g):
    r = lax.rsqrt(jnp.mean(x * x, axis=-1, keepdims=True) + EPS)
    xh = x * r
    dxh = acc * g
    dx = dx_in + r * (dxh - xh * jnp.mean(dxh * xh, axis=-1, keepdims=True))
    return dx, jnp.sum(acc * xh, axis=0, keepdims=True)


def residual_norm(a, w, xs, g, name, tm=1024):
    if g is None:
        return mm(a, w, "nn", name, epi=_epi_add, extras=(xs,), tm=tm), None
    return mm(a, w, "nn", name, outs=(F32, BF16), epi=_epi_add_norm, extras=(xs,), vecs=(g,), tm=tm)


def _epi_relu2(acc):
    r = jnp.maximum(acc, 0.0)
    return acc, r * r


def _epi_relu2_bwd(acc, h1):
    return (acc * (2.0 * jnp.maximum(h1.astype(F32), 0.0)),)


def _rms(x, g):
    return x * lax.rsqrt(jnp.mean(x * x, axis=-1, keepdims=True) + EPS) * g


def _row_spec(ts, cols):
    return pl.BlockSpec((ts, cols), lambda i: (i, 0))


def _par_spec(cols):
    return pl.BlockSpec((1, cols), lambda i: (0, 0))


def rmsnorm_fwd(x, g, name, ts=256):
    T, D = x.shape
    ts = min(ts, T)

    def body(x_ref, g_ref, o_ref):
        o_ref[...] = _rms(x_ref[...], g_ref[...]).astype(o_ref.dtype)

    return pl.pallas_call(
        body, name=name, grid=(T // ts,),
        in_specs=[_row_spec(ts, D), _par_spec(D)], out_specs=_row_spec(ts, D),
        out_shape=jax.ShapeDtypeStruct((T, D), BF16), compiler_params=_cparams(("parallel",)),
    )(x, g.reshape(1, D))


def rmsnorm_bwd(x, g, dy, dx_in, name, ts=256):
    T, D = x.shape
    ts = min(ts, T)

    def body(x_ref, g_ref, dy_ref, dxi_ref, dx_ref, dg_ref):
        xv = x_ref[...]
        r = lax.rsqrt(jnp.mean(xv * xv, axis=-1, keepdims=True) + EPS)
        xh = xv * r
        dyv = dy_ref[...].astype(F32)
        dxh = dyv * g_ref[...]
        dx_ref[...] = dxi_ref[...] + r * (dxh - xh * jnp.mean(dxh * xh, axis=-1, keepdims=True))
        dg = jnp.sum(dyv * xh, axis=0, keepdims=True)

        @pl.when(pl.program_id(0) == 0)
        def _():
            dg_ref[...] = jnp.zeros_like(dg_ref)

        dg_ref[...] += dg

    dx, dg = pl.pallas_call(
        body, name=name, grid=(T // ts,),
        in_specs=[_row_spec(ts, D), _par_spec(D), _row_spec(ts, D), _row_spec(ts, D)],
        out_specs=[_row_spec(ts, D), _par_spec(D)],
        out_shape=[jax.ShapeDtypeStruct((T, D), F32), jax.ShapeDtypeStruct((1, D), F32)],
        compiler_params=_cparams(("arbitrary",)),
    )(x, g.reshape(1, D), dy, dx_in)
    return dx, dg.reshape(D)


def rope_tables(pos, name="rope_tables"):
    T = pos.shape[0]
    half = MLA_ROPE // 2
    inv = ROPE_THETA ** (-jnp.arange(0, MLA_ROPE, 2, dtype=F32) / MLA_ROPE)
    inv_row = jnp.concatenate([inv, inv, jnp.zeros((LANES - MLA_ROPE,), F32)]).reshape(1, LANES)

    def body(p_ref, f_ref, c_ref, a_ref, b_ref):
        ang = p_ref[...].astype(F32) * f_ref[...]
        lane = lax.broadcasted_iota(jnp.int32, ang.shape, 1)
        c, s = jnp.cos(ang), jnp.sin(ang)
        c_ref[...] = jnp.where(lane < MLA_ROPE, c, 0.0)
        a_ref[...] = jnp.where(lane < half, -s, 0.0)
        b_ref[...] = jnp.where((lane >= half) & (lane < MLA_ROPE), s, 0.0)

    sh = jax.ShapeDtypeStruct((T, LANES), F32)
    return pl.pallas_call(body, name=name, out_shape=[sh, sh, sh], compiler_params=_cparams())(pos, inv_row)


def _roll_l(x):
    return pltpu.roll(x, LANES - MLA_ROPE // 2, 1)


def _roll_r(x):
    return pltpu.roll(x, MLA_ROPE // 2, 1)


def _rope(r, c, sa, sb):
    return r * c + _roll_l(r) * sa + _roll_r(r) * sb


def _rope_t(d, c, sa, sb):
    return d * c + _roll_r(d * sa) + _roll_l(d * sb)


def _epi_rope_q(acc, c, sa, sb):
    hw = 2 * LANES
    parts = []
    for h in range(acc.shape[1] // hw):
        parts += [acc[:, h * hw:h * hw + LANES], _rope(acc[:, h * hw + LANES:(h + 1) * hw], c, sa, sb)]
    return (jnp.concatenate(parts, axis=1),)


def mla_mid_fwd(z, qn, kvn, tabs, name, ts=256):
    T = z.shape[0]
    ts = min(ts, T)
    a0, a1 = MLA_QR, MLA_QR + MLA_KVR

    def body(z_ref, qn_ref, kvn_ref, c_ref, sa_ref, sb_ref, cq_ref, ckv_ref, kr_ref):
        cq_ref[...] = _rms(z_ref[:, 0:a0], qn_ref[...]).astype(BF16)
        ckv_ref[...] = _rms(z_ref[:, a0:a1], kvn_ref[...]).astype(BF16)
        kr_ref[...] = _rope(z_ref[:, a1:MLA_ZPAD], c_ref[...], sa_ref[...], sb_ref[...]).astype(BF16)

    return pl.pallas_call(
        body, name=name, grid=(T // ts,),
        in_specs=[_row_spec(ts, MLA_ZPAD), _par_spec(MLA_QR), _par_spec(MLA_KVR)] + [_row_spec(ts, LANES)] * 3,
        out_specs=[_row_spec(ts, MLA_QR), _row_spec(ts, MLA_KVR), _row_spec(ts, LANES)],
        out_shape=[jax.ShapeDtypeStruct((T, MLA_QR), BF16), jax.ShapeDtypeStruct((T, MLA_KVR), BF16),
                   jax.ShapeDtypeStruct((T, LANES), BF16)],
        compiler_params=_cparams(("parallel",)),
    )(z, qn.reshape(1, -1), kvn.reshape(1, -1), *tabs)


def mla_mid_bwd(z, qn, kvn, tabs, dcq, dckv, dkr, name, ts=256):
    T = z.shape[0]
    ts = min(ts, T)
    a0, a1 = MLA_QR, MLA_QR + MLA_KVR

    def body(z_ref, qn_ref, kvn_ref, c_ref, sa_ref, sb_ref, dcq_ref, dckv_ref, dkr_ref, dz_ref, dqn_ref, dkvn_ref):
        _, vq = jax.vjp(_rms, z_ref[:, 0:a0], qn_ref[...])
        dzq, dqn = vq(dcq_ref[...].astype(F32))
        _, vk = jax.vjp(_rms, z_ref[:, a0:a1], kvn_ref[...])
        dzk, dkvn = vk(dckv_ref[...].astype(F32))
        dz_ref[:, 0:a0] = dzq.astype(dz_ref.dtype)
        dz_ref[:, a0:a1] = dzk.astype(dz_ref.dtype)
        dz_ref[:, a1:MLA_ZPAD] = _rope_t(dkr_ref[...].astype(F32), c_ref[...], sa_ref[...], sb_ref[...]).astype(dz_ref.dtype)

        @pl.when(pl.program_id(0) == 0)
        def _():
            dqn_ref[...] = jnp.zeros_like(dqn_ref)
            dkvn_ref[...] = jnp.zeros_like(dkvn_ref)

        dqn_ref[...] += dqn
        dkvn_ref[...] += dkvn

    dz, dqn, dkvn = pl.pallas_call(
        body, name=name, grid=(T // ts,),
        in_specs=[_row_spec(ts, MLA_ZPAD), _par_spec(MLA_QR), _par_spec(MLA_KVR)] + [_row_spec(ts, LANES)] * 3
        + [_row_spec(ts, MLA_QR), _row_spec(ts, MLA_KVR), _row_spec(ts, LANES)],
        out_specs=[_row_spec(ts, MLA_ZPAD), _par_spec(MLA_QR), _par_spec(MLA_KVR)],
        out_shape=[jax.ShapeDtypeStruct((T, MLA_ZPAD), BF16), jax.ShapeDtypeStruct((1, MLA_QR), F32),
                   jax.ShapeDtypeStruct((1, MLA_KVR), F32)],
        compiler_params=_cparams(("arbitrary",)),
    )(z, qn.reshape(1, -1), kvn.reshape(1, -1), *tabs, dcq, dckv, dkr)
    return dz, dqn.reshape(-1), dkvn.reshape(-1)


def loss_head(x, g, target, name="loss_head", ts=256):
    T, D = x.shape
    ts = min(ts, T)

    def body(x_ref, g_ref, t_ref, se_ref, dx_ref, dg_ref):
        xv = x_ref[...]
        r = lax.rsqrt(jnp.mean(xv * xv, axis=-1, keepdims=True) + EPS)
        xh = xv * r
        err = xh * g_ref[...] - t_ref[...]
        dy = err * (1.0 / D)
        dxh = dy * g_ref[...]
        dx_ref[...] = r * (dxh - xh * jnp.mean(dxh * xh, axis=-1, keepdims=True))

        @pl.when(pl.program_id(0) == 0)
        def _():
            se_ref[...] = jnp.zeros_like(se_ref)
            dg_ref[...] = jnp.zeros_like(dg_ref)

        se_ref[...] += jnp.sum(err * err, axis=0, keepdims=True)
        dg_ref[...] += jnp.sum(dy * xh, axis=0, keepdims=True)

    se, dx, dg = pl.pallas_call(
        body, name=name, grid=(T // ts,),
        in_specs=[_row_spec(ts, D), _par_spec(D), _row_spec(ts, D)],
        out_specs=[_par_spec(D), _row_spec(ts, D), _par_spec(D)],
        out_shape=[jax.ShapeDtypeStruct((1, D), F32), jax.ShapeDtypeStruct((T, D), F32), jax.ShapeDtypeStruct((1, D), F32)],
        compiler_params=_cparams(("arbitrary",)),
    )(x, g.reshape(1, D), target)
    return se, dx, dg.reshape(D)


def _dot_nt(a, b):
    return lax.dot_general(a, b, (((1,), (1,)), ((), ())), preferred_element_type=F32)


def _dot_nn(a, b):
    return lax.dot_general(a, b, (((1,), (0,)), ((), ())), preferred_element_type=F32)


class _Attn:
    def __init__(self, H, dq, dk1, dv, causal, scale, hp, hp_kv, blk=256):
        self.H, self.dq, self.dk1, self.dv, self.causal, self.scale, self.blk = H, dq, dk1, dv, causal, scale, blk
        self.hp, self.hp_kv = hp, hp_kv


def _cols(ref, rows, hh, width):
    return ref[rows, hh * width:(hh + 1) * width]


def _keys(cfg, k1_ref, k2_ref, rows, hh):
    ks = _cols(k1_ref, rows, hh, cfg.dk1)
    if k2_ref is not None:
        ks = jnp.concatenate([ks, k2_ref[rows, :]], axis=1)
    return ks


def _blocks(cfg, Tq, Tk):
    tq, tk = min(cfg.blk, Tq), min(cfg.blk, Tk)
    assert tq == tk or not cfg.causal
    return tq, tk


def _attn_specs(cfg, hp, t, Tk, has_k2, by_q):
    g = cfg.H // hp
    if by_q:
        specs = [pl.BlockSpec((t, hp * cfg.dq), lambda h, i: (i, h)),
                 pl.BlockSpec((Tk, hp * cfg.dk1), lambda h, i: (0, h)),
                 pl.BlockSpec((Tk, hp * cfg.dv), lambda h, i: (0, g + h))]
        if has_k2:
            specs.append(pl.BlockSpec((Tk, LANES), lambda h, i: (0, 0)))
    else:
        specs = [None,
                 pl.BlockSpec((t, hp * cfg.dk1), lambda j, h: (j, h)),
                 pl.BlockSpec((t, hp * cfg.dv), lambda j, h: (j, g + h))]
        if has_k2:
            specs.append(pl.BlockSpec((t, LANES), lambda j, h: (j, 0)))
    return specs


def _mask(s, diagonal):
    if not diagonal:
        return s
    return jnp.where(lax.broadcasted_iota(jnp.int32, s.shape, 0) >= lax.broadcasted_iota(jnp.int32, s.shape, 1), s, NEG)


def flash_fwd(cfg, q, k1, v, k2, name):
    Tq, Tk = q.shape[0], k1.shape[0]
    t, tk = _blocks(cfg, Tq, Tk)
    nkb = Tk // tk
    has_k2 = k2 is not None
    hp = cfg.hp

    def body(*refs):
        q_ref, k1_ref, v_ref = refs[:3]
        k2_ref = refs[3] if has_k2 else None
        o_ref, lse_ref = refs[-2], refs[-1]
        i = pl.program_id(1)
        qs = [_cols(q_ref, slice(None), hh, cfg.dq) for hh in range(hp)]

        def step(j, carry, diagonal=False):
            rows = pl.ds(pl.multiple_of(j * tk, tk), tk)
            out = []
            for hh in range(hp):
                m, l, acc = carry[hh]
                s = _mask(_dot_nt(qs[hh], _keys(cfg, k1_ref, k2_ref, rows, hh)) * cfg.scale, diagonal)
                m2 = jnp.maximum(m, jnp.max(s, axis=-1, keepdims=True))
                p = jnp.exp(s - m2)
                alpha = jnp.exp(m - m2)
                l2 = alpha * l + jnp.sum(p, axis=-1, keepdims=True)
                acc2 = alpha * acc + _dot_nn(p.astype(BF16), _cols(v_ref, rows, hh, cfg.dv))
                out.append((m2, l2, acc2))
            return tuple(out)

        init = tuple((jnp.full((t, 1), NEG, F32), jnp.zeros((t, 1), F32), jnp.zeros((t, cfg.dv), F32)) for _ in range(hp))
        res = lax.fori_loop(0, i if cfg.causal else nkb, step, init)
        if cfg.causal:
            res = step(i, res, True)
        for hh in range(hp):
            m, l, acc = res[hh]
            o_ref[:, hh * cfg.dv:(hh + 1) * cfg.dv] = (acc / l).astype(o_ref.dtype)
            lse_ref[hh] = m + jnp.log(l)

    args = [q, k1, v] + ([k2] if has_k2 else [])
    return pl.pallas_call(
        body, name=name, grid=(cfg.H // hp, Tq // t), in_specs=_attn_specs(cfg, hp, t, Tk, has_k2, True),
        out_specs=[pl.BlockSpec((t, hp * cfg.dv), lambda h, i: (i, h)), pl.BlockSpec((hp, t, 1), lambda h, i: (h, i, 0))],
        out_shape=[jax.ShapeDtypeStruct((Tq, cfg.H * cfg.dv), BF16), jax.ShapeDtypeStruct((cfg.H, Tq, 1), F32)],
        compiler_params=_cparams(("parallel", "parallel")),
    )(*args)


def flash_dq(cfg, q, k1, v, k2, o, do, lse, out_dtype, name, rope_tabs=None):
    Tq, Tk = q.shape[0], k1.shape[0]
    t, tk = _blocks(cfg, Tq, Tk)
    nkb = Tk // tk
    has_k2 = k2 is not None
    hp = cfg.hp
    n_tab = 0 if rope_tabs is None else len(rope_tabs)

    def body(*refs):
        q_ref, k1_ref, v_ref = refs[:3]
        k2_ref = refs[3] if has_k2 else None
        tab_refs = refs[len(refs) - 5 - n_tab:len(refs) - 5]
        o_ref, do_ref, lse_ref, dq_ref, dl_ref = refs[-5:]
        i = pl.program_id(1)
        qs = [_cols(q_ref, slice(None), hh, cfg.dq) for hh in range(hp)]
        dos = [_cols(do_ref, slice(None), hh, cfg.dv) for hh in range(hp)]
        lses = [lse_ref[hh] for hh in range(hp)]
        deltas = []
        for hh in range(hp):
            d = jnp.sum(dos[hh].astype(F32) * _cols(o_ref, slice(None), hh, cfg.dv).astype(F32), axis=-1, keepdims=True)
            dl_ref[hh] = d
            deltas.append(d)

        def step(j, dqs, diagonal=False):
            rows = pl.ds(pl.multiple_of(j * tk, tk), tk)
            out = []
            for hh in range(hp):
                ks = _keys(cfg, k1_ref, k2_ref, rows, hh)
                s = _mask(_dot_nt(qs[hh], ks) * cfg.scale, diagonal)
                p = jnp.exp(s - lses[hh])
                dp = _dot_nt(dos[hh], _cols(v_ref, rows, hh, cfg.dv))
                ds = p * (dp - deltas[hh]) * cfg.scale
                out.append(dqs[hh] + _dot_nn(ds.astype(BF16), ks))
            return tuple(out)

        dqs = lax.fori_loop(0, i if cfg.causal else nkb, step, tuple(jnp.zeros((t, cfg.dq), F32) for _ in range(hp)))
        if cfg.causal:
            dqs = step(i, dqs, True)
        tabs = [r[...] for r in tab_refs]
        for hh in range(hp):
            dq = dqs[hh]
            if tabs:
                dq = jnp.concatenate([dq[:, :LANES], _rope_t(dq[:, LANES:], *tabs)], axis=1)
            dq_ref[:, hh * cfg.dq:(hh + 1) * cfg.dq] = dq.astype(dq_ref.dtype)

    ov = pl.BlockSpec((t, hp * cfg.dv), lambda h, i: (i, h))
    row1 = pl.BlockSpec((hp, t, 1), lambda h, i: (h, i, 0))
    tab_specs = [pl.BlockSpec((t, LANES), lambda h, i: (i, 0))] * n_tab
    args = [q, k1, v] + ([k2] if has_k2 else []) + list(rope_tabs or ()) + [o, do, lse]
    return pl.pallas_call(
        body, name=name, grid=(cfg.H // hp, Tq // t),
        in_specs=_attn_specs(cfg, hp, t, Tk, has_k2, True) + tab_specs + [ov, ov, row1],
        out_specs=[pl.BlockSpec((t, hp * cfg.dq), lambda h, i: (i, h)), row1],
        out_shape=[jax.ShapeDtypeStruct((Tq, cfg.H * cfg.dq), out_dtype), jax.ShapeDtypeStruct((cfg.H, Tq, 1), F32)],
        compiler_params=_cparams(("parallel", "parallel")),
    )(*args)


def flash_dkv(cfg, q, k1, v, k2, do, lse, delta, out_dtype, name):
    Tq, Tk = q.shape[0], k1.shape[0]
    tq, t = _blocks(cfg, Tq, Tk)
    nqb = Tq // tq
    has_k2 = k2 is not None
    hp = cfg.hp_kv
    assert hp == cfg.H
    v0 = cfg.H * cfg.dk1

    def body(*refs):
        q_ref, k1_ref, v_ref = refs[:3]
        k2_ref = refs[3] if has_k2 else None
        n_in = 4 if has_k2 else 3
        do_ref, lse_ref, dl_ref = refs[n_in:n_in + 3]
        dkv_ref = refs[n_in + 3]
        j, h = pl.program_id(0), pl.program_id(1)
        kss = [_keys(cfg, k1_ref, k2_ref, slice(None), hh) for hh in range(hp)]
        vss = [_cols(v_ref, slice(None), hh, cfg.dv) for hh in range(hp)]

        def step(i, carry, diagonal=False):
            rows = pl.ds(pl.multiple_of(i * tq, tq), tq)
            out = []
            for hh in range(hp):
                dk, dv = carry[hh]
                qi, doi = _cols(q_ref, rows, hh, cfg.dq), _cols(do_ref, rows, hh, cfg.dv)
                s = _dot_nt(kss[hh], qi) * cfg.scale
                if diagonal:
                    s = jnp.where(lax.broadcasted_iota(jnp.int32, s.shape, 0) <= lax.broadcasted_iota(jnp.int32, s.shape, 1), s, NEG)
                p = jnp.exp(s - lse_ref[hh, :, rows])
                dv = dv + _dot_nn(p.astype(BF16), doi)
                ds = p * (_dot_nt(vss[hh], doi) - dl_ref[hh, :, rows]) * cfg.scale
                dk = dk + _dot_nn(ds.astype(BF16), qi)
                out.append((dk, dv))
            return tuple(out)

        init = tuple((jnp.zeros((t, cfg.dq), F32), jnp.zeros((t, cfg.dv), F32)) for _ in range(hp))
        if cfg.causal:
            res = lax.fori_loop(j + 1, nqb, step, step(j, init, True))
        else:
            res = lax.fori_loop(0, nqb, step, init)
        for hh in range(hp):
            dk, dv = res[hh]
            dkv_ref[:, hh * cfg.dk1:(hh + 1) * cfg.dk1] = dk[:, 0:cfg.dk1].astype(dkv_ref.dtype)
            dkv_ref[:, v0 + hh * cfg.dv:v0 + (hh + 1) * cfg.dv] = dv.astype(dkv_ref.dtype)
        if has_k2:
            dk2_ref = refs[n_in + 4]

            @pl.when(h == 0)
            def _():
                dk2_ref[...] = jnp.zeros_like(dk2_ref)

            for hh in range(hp):
                dk2_ref[...] += res[hh][0][:, cfg.dk1:]

    specs = _attn_specs(cfg, hp, t, Tk, has_k2, False)
    specs[0] = pl.BlockSpec((Tq, hp * cfg.dq), lambda j, h: (0, h))
    rows_all = pl.BlockSpec((hp, 1, Tq), lambda j, h: (h, 0, 0))
    specs += [pl.BlockSpec((Tq, hp * cfg.dv), lambda j, h: (0, h)), rows_all, rows_all]
    args = [q, k1, v] + ([k2] if has_k2 else []) + [do, lse.reshape(cfg.H, 1, Tq), delta.reshape(cfg.H, 1, Tq)]
    out_specs = [pl.BlockSpec((t, v0 + cfg.H * cfg.dv), lambda j, h: (j, 0))]
    out_shape = [jax.ShapeDtypeStruct((Tk, v0 + cfg.H * cfg.dv), out_dtype)]
    if has_k2:
        out_specs.append(pl.BlockSpec((t, LANES), lambda j, h: (j, 0)))
        out_shape.append(jax.ShapeDtypeStruct((Tk, LANES), F32))
    return pl.pallas_call(
        body, name=name, grid=(Tk // t, cfg.H // hp), in_specs=specs, out_specs=out_specs, out_shape=out_shape,
        compiler_params=_cparams(("parallel", "arbitrary")),
    )(*args)


def _shift_down(x, s):
    if s == 0:
        return x
    t = lax.broadcasted_iota(jnp.int32, x.shape, 0)
    return jnp.where(t >= s, pltpu.roll(x, s, 0), 0.0)


def _shift_up(x, s):
    if s == 0:
        return x
    n = x.shape[0]
    t = lax.broadcasted_iota(jnp.int32, x.shape, 0)
    return jnp.where(t < n - s, pltpu.roll(x, n - s, 0), 0.0)


def _conv(x, w_ref, kw):
    y = x * w_ref[kw - 1:kw, :]
    for j in range(kw - 1):
        y = y + _shift_down(x, kw - 1 - j) * w_ref[j:j + 1, :]
    return y


def _conv_t(d, w_ref, kw):
    y = d * w_ref[kw - 1:kw, :]
    for j in range(kw - 1):
        y = y + _shift_up(d, kw - 1 - j) * w_ref[j:j + 1, :]
    return y


def _conv_dw(d, x, kw):
    rows = lax.broadcasted_iota(jnp.int32, (kw, d.shape[1]), 0)
    dw = jnp.zeros((kw, d.shape[1]), F32)
    for j in range(kw):
        r = jnp.sum(d * _shift_down(x, kw - 1 - j), axis=0, keepdims=True)
        dw = jnp.where(rows == j, r, dw)
    return dw


def _silu(x):
    return x * jax.nn.sigmoid(x)


def _silu_grad(x):
    s = jax.nn.sigmoid(x)
    return s * (1.0 + x * (1.0 - s))


def gdn_conv_fwd(z, w, name, tc=256):
    T, C = z.shape[0], w.shape[1]
    kw = w.shape[0]

    def body(x_ref, w_ref, o_ref):
        o_ref[...] = _silu(_conv(x_ref[...], w_ref, kw))

    return pl.pallas_call(
        body, name=name, grid=(C // tc,),
        in_specs=[pl.BlockSpec((T, tc), lambda j: (0, j)), pl.BlockSpec((kw, tc), lambda j: (0, j))],
        out_specs=pl.BlockSpec((T, tc), lambda j: (0, j)),
        out_shape=jax.ShapeDtypeStruct((T, C), F32), compiler_params=_cparams(("parallel",)),
    )(z, w)


def gdn_conv_bwd(z, w, dy, name, tc=256):
    T, C = z.shape[0], w.shape[1]
    kw = w.shape[0]

    def body(x_ref, w_ref, dy_ref, dx_ref, dw_ref):
        xv = x_ref[...]
        dc = dy_ref[...] * _silu_grad(_conv(xv, w_ref, kw))
        dx_ref[...] = _conv_t(dc, w_ref, kw).astype(dx_ref.dtype)
        dw_ref[...] = _conv_dw(dc, xv, kw)

    col = lambda j: (0, j)
    return pl.pallas_call(
        body, name=name, grid=(C // tc,),
        in_specs=[pl.BlockSpec((T, tc), col), pl.BlockSpec((kw, tc), col), pl.BlockSpec((T, tc), col)],
        out_specs=[pl.BlockSpec((T, tc), col), pl.BlockSpec((kw, tc), col)],
        out_shape=[jax.ShapeDtypeStruct((T, C), BF16), jax.ShapeDtypeStruct((kw, C), F32)],
        compiler_params=_cparams(("parallel",)),
    )(z, w, dy)


def sc_fwd(z, w, name, tc=256):
    T, C = z.shape[0], w.shape[1]
    kw, nb = w.shape[0], C // tc

    def body(b_ref, c_ref, u_ref, w_ref, o_ref):
        o_ref[...] = (b_ref[...] * _conv(c_ref[...] * u_ref[...], w_ref, kw)).astype(o_ref.dtype)

    return pl.pallas_call(
        body, name=name, grid=(nb,),
        in_specs=[pl.BlockSpec((T, tc), lambda j: (0, j)), pl.BlockSpec((T, tc), lambda j: (0, nb + j)),
                  pl.BlockSpec((T, tc), lambda j: (0, 2 * nb + j)), pl.BlockSpec((kw, tc), lambda j: (0, j))],
        out_specs=pl.BlockSpec((T, tc), lambda j: (0, j)),
        out_shape=jax.ShapeDtypeStruct((T, C), BF16), compiler_params=_cparams(("parallel",)),
    )(z, z, z, w)


def sc_bwd(z, w, dy, name, tc=256):
    T, C = z.shape[0], w.shape[1]
    kw, nb = w.shape[0], C // tc

    def body(b_ref, c_ref, u_ref, w_ref, dy_ref, db_ref, dc_ref, du_ref, dw_ref):
        cv, uv, dyv = c_ref[...], u_ref[...], dy_ref[...]
        cu = cv * uv
        db_ref[...] = (dyv * _conv(cu, w_ref, kw)).astype(db_ref.dtype)
        dcv = dyv * b_ref[...]
        dcu = _conv_t(dcv, w_ref, kw)
        dc_ref[...] = (dcu * uv).astype(dc_ref.dtype)
        du_ref[...] = (dcu * cv).astype(du_ref.dtype)
        dw_ref[...] = _conv_dw(dcv, cu, kw)

    col = lambda j: (0, j)
    act = jax.ShapeDtypeStruct((T, C), BF16)
    return pl.pallas_call(
        body, name=name, grid=(nb,),
        in_specs=[pl.BlockSpec((T, tc), col), pl.BlockSpec((T, tc), lambda j: (0, nb + j)),
                  pl.BlockSpec((T, tc), lambda j: (0, 2 * nb + j)), pl.BlockSpec((kw, tc), col), pl.BlockSpec((T, tc), col)],
        out_specs=[pl.BlockSpec((T, tc), col)] * 3 + [pl.BlockSpec((kw, tc), col)],
        out_shape=[act, act, act, jax.ShapeDtypeStruct((kw, C), F32)],
        compiler_params=_cparams(("parallel",)),
    )(z, z, z, w, dy)


def _hdot(a, b, dims):
    a_hi, b_hi = a.astype(BF16), b.astype(BF16)
    a_lo, b_lo = (a - a_hi.astype(F32)).astype(BF16), (b - b_hi.astype(F32)).astype(BF16)
    dot = lambda x, y: lax.dot_general(x, y, (dims, ((), ())), preferred_element_type=F32)
    return dot(a_hi, b_hi) + (dot(a_hi, b_lo) + dot(a_lo, b_hi))


def _bdot(a, b, dims):
    return lax.dot_general(a.astype(BF16), b.astype(BF16), (dims, ((), ())), preferred_element_type=F32)


_NN, _NT, _TN = ((1,), (0,)), ((1,), (1,)), ((0,), (0,))


def _per_head_dots(dot2d):
    def stacked(a, b, dims):
        return jnp.stack([dot2d(a[h], b[h], dims) for h in range(a.shape[0])])

    @jax.custom_vjp
    def nn(a, b):
        return stacked(a, b, _NN)

    @jax.custom_vjp
    def nt(a, b):
        return stacked(a, b, _NT)

    @jax.custom_vjp
    def tn(a, b):
        return stacked(a, b, _TN)

    nn.defvjp(lambda a, b: (nn(a, b), (a, b)), lambda r, d: (stacked(d, r[1], _NT), stacked(r[0], d, _TN)))
    nt.defvjp(lambda a, b: (nt(a, b), (a, b)), lambda r, d: (stacked(d, r[1], _NN), stacked(d, r[0], _TN)))
    tn.defvjp(lambda a, b: (tn(a, b), (a, b)), lambda r, d: (stacked(r[1], d, _NT), stacked(r[0], d, _NN)))
    return nn, nt, tn


_hnn, _hnt, _htn = _per_head_dots(_hdot)
_bnn, _bnt, _btn = _per_head_dots(_bdot)


@jax.custom_vjp
def _unit_lower_inverse(m):
    c = m.shape[-1]
    eye = (lax.broadcasted_iota(jnp.int32, (c, c), 0) == lax.broadcasted_iota(jnp.int32, (c, c), 1)).astype(F32)
    t = eye - m
    p = _hnn(m, m)
    n = 2
    while n < c:
        t = t + _hnn(t, p)
        n *= 2
        if n < c:
            p = _hnn(p, p)
    return t


def _uli_fwd(m):
    t = _unit_lower_inverse(m)
    return t, t


def _uli_bwd(t, dt):
    return (-_htn(t, _hnt(dt, t)),)


_unit_lower_inverse.defvjp(_uli_fwd, _uli_bwd)


@jax.custom_vjp
def _known_inverse(m, t):
    return t


_known_inverse.defvjp(lambda m, t: (t, t), lambda t, dt: (_uli_bwd(t, dt)[0], jnp.zeros_like(t)))


def _gdn_chunk(q, k, v, gate, bl, al, a_log, dt_bias, o_norm, st, t_known=None):
    nh, c = q.shape[0], q.shape[1]
    ii = lax.broadcasted_iota(jnp.int32, (c, c), 0)
    jj = lax.broadcasted_iota(jnp.int32, (c, c), 1)
    tri, strict = ii >= jj, ii > jj
    q = q * lax.rsqrt(jnp.sum(q * q, -1, keepdims=True) + EPS) * (GDN_D ** -0.5)
    k = k * lax.rsqrt(jnp.sum(k * k, -1, keepdims=True) + EPS)
    beta = jax.nn.sigmoid(bl)
    g = -jnp.exp(a_log) * jax.nn.softplus(al + dt_bias)
    gc = _hnn(jnp.broadcast_to(tri.astype(F32), (nh, c, c)), g)
    gcol = _hnn(gc, jnp.full((nh, LANES, c), 1.0 / LANES, F32))
    grow = _hnt(jnp.full((nh, c, LANES), 1.0 / LANES, F32), gc)
    decay = jnp.where(tri, jnp.exp(jnp.where(tri, gcol - grow, 0.0)), 0.0)
    kb = k * beta
    m = jnp.where(strict, _bnt(kb, k) * decay, 0.0)
    t_inv = _unit_lower_inverse(m) if t_known is None else _known_inverse(m, t_known)
    eg = jnp.exp(gc)
    u = _bnn(t_inv, v * beta)
    w = _bnn(t_inv, kb * eg)
    attn = _bnt(q, k) * decay
    v_new = u - _bnn(w, st)
    o = _bnn(q * eg, st) + _bnn(attn, v_new)
    g_last = jnp.sum(g, axis=1, keepdims=True)
    st_new = st * jnp.exp(g_last) + _btn(k * jnp.exp(g_last - gc), v_new)
    o = o * lax.rsqrt(jnp.mean(o * o, -1, keepdims=True) + EPS) * o_norm
    return o * _silu(gate), st_new, t_inv


GDN_HP = 8
_GW = GDN_HP * GDN_D
_GB = GDN_H // GDN_HP


def _gdn_specs(n_chunks, rev):
    def tok(col):
        if rev:
            return pl.BlockSpec((GDN_C, _GW), lambda h, n: (n_chunks - 1 - n, col + h))
        return pl.BlockSpec((GDN_C, _GW), lambda h, n: (n, col + h))
    par = pl.BlockSpec((1, _GW), lambda h, n: (0, h))
    shared = pl.BlockSpec((1, GDN_D), lambda h, n: (0, 0))
    if rev:
        st = pl.BlockSpec((GDN_HP, None, GDN_D, GDN_D), lambda h, n: (h, n_chunks - 1 - n, 0, 0))
    else:
        st = pl.BlockSpec((GDN_HP, None, GDN_D, GDN_D), lambda h, n: (h, n, 0, 0))
    return tok, par, shared, st


def _heads(ref):
    return jnp.stack([ref[:, h * GDN_D:(h + 1) * GDN_D] for h in range(ref.shape[1] // GDN_D)])


def gdn_chunk_fwd(qkv, z, a_log_x, dt_bias_x, o_norm, name):
    T = qkv.shape[0]
    n_chunks = T // GDN_C
    H = GDN_H
    tok, par, shared, st_spec = _gdn_specs(n_chunks, False)

    def body(q_ref, k_ref, v_ref, g_ref, bl_ref, al_ref, a_ref, dt_ref, on_ref, o_ref, st_ref, ti_ref, state):
        @pl.when(pl.program_id(1) == 0)
        def _():
            state[...] = jnp.zeros_like(state)

        st = state[...]
        st_ref[...] = st
        o, st_new, t_inv = _gdn_chunk(_heads(q_ref), _heads(k_ref), _heads(v_ref), _heads(g_ref), _heads(bl_ref), _heads(al_ref),
                                      _heads(a_ref), _heads(dt_ref), on_ref[...], st)
        for hh in range(GDN_HP):
            o_ref[:, hh * GDN_D:(hh + 1) * GDN_D] = o[hh].astype(o_ref.dtype)
        ti_ref[...] = t_inv
        state[...] = st_new

    B = _GB
    return pl.pallas_call(
        body, name=name, grid=(B, n_chunks),
        in_specs=[tok(0), tok(B), tok(2 * B), tok(3 * B), tok(4 * B), tok(5 * B), par, par, shared],
        out_specs=[tok(0), st_spec, pl.BlockSpec((GDN_HP, None, GDN_C, GDN_C), lambda h, n: (h, n, 0, 0))],
        out_shape=[jax.ShapeDtypeStruct((T, H * GDN_D), BF16), jax.ShapeDtypeStruct((H, n_chunks, GDN_D, GDN_D), F32),
                   jax.ShapeDtypeStruct((H, n_chunks, GDN_C, GDN_C), F32)],
        scratch_shapes=[pltpu.VMEM((GDN_HP, GDN_D, GDN_D), F32)],
        compiler_params=_cparams(("parallel", "arbitrary")),
    )(qkv, qkv, qkv, z, z, z, a_log_x, dt_bias_x, o_norm)


def gdn_chunk_bwd(qkv, z, a_log_x, dt_bias_x, o_norm, states, t_invs, do, name):
    T = qkv.shape[0]
    n_chunks = T // GDN_C
    H = GDN_H
    tok, par, shared, st_spec = _gdn_specs(n_chunks, True)

    def body(q_ref, k_ref, v_ref, g_ref, bl_ref, al_ref, a_ref, dt_ref, on_ref, st_ref, ti_ref, do_ref,
             dqkv_ref, dg_ref, dba_ref, da_ref, ddt_ref, don_ref, dstate):
        h, n = pl.program_id(0), pl.program_id(1)

        @pl.when(n == 0)
        def _():
            dstate[...] = jnp.zeros_like(dstate)
            da_ref[...] = jnp.zeros_like(da_ref)
            ddt_ref[...] = jnp.zeros_like(ddt_ref)

        @pl.when((n == 0) & (h == 0))
        def _():
            don_ref[...] = jnp.zeros_like(don_ref)

        t_known = ti_ref[...]
        _, vjp = jax.vjp(lambda *ins: _gdn_chunk(*ins, t_known=t_known)[:2],
                         _heads(q_ref), _heads(k_ref), _heads(v_ref), _heads(g_ref), _heads(bl_ref), _heads(al_ref),
                         _heads(a_ref), _heads(dt_ref), on_ref[...], st_ref[...])
        dq, dk, dv, dg, dbl, dal, da, ddt, don, dst = vjp((_heads(do_ref).astype(F32), dstate[...]))
        lane = lax.broadcasted_iota(jnp.int32, (GDN_C, LANES), 1)
        dba = jnp.zeros((GDN_C, LANES), F32)
        for hh in range(GDN_HP):
            cols = slice(hh * GDN_D, (hh + 1) * GDN_D)
            for part, d in enumerate((dq, dk, dv)):
                dqkv_ref[:, part * H * GDN_D + hh * GDN_D:part * H * GDN_D + (hh + 1) * GDN_D] = d[hh]
            dg_ref[:, cols] = dg[hh].astype(dg_ref.dtype)
            dba = jnp.where(lane == hh, jnp.sum(dbl[hh], axis=-1, keepdims=True), dba)
            dba = jnp.where(lane == H + hh, jnp.sum(dal[hh], axis=-1, keepdims=True), dba)
            da_ref[:, cols] += da[hh]
            ddt_ref[:, cols] += ddt[hh]
        dba_ref[...] = dba.astype(dba_ref.dtype)
        don_ref[...] += don
        dstate[...] = dst

    tok0 = tok(0)
    B = _GB
    assert B == 1
    bf_tok = jax.ShapeDtypeStruct((T, H * GDN_D), BF16)
    par_sh = jax.ShapeDtypeStruct((1, H * GDN_D), F32)
    return pl.pallas_call(
        body, name=name, grid=(B, n_chunks),
        in_specs=[tok(0), tok(B), tok(2 * B), tok(3 * B), tok(4 * B), tok(5 * B), par, par, shared, st_spec,
                  pl.BlockSpec((GDN_HP, None, GDN_C, GDN_C), lambda h, n: (h, n_chunks - 1 - n, 0, 0)), tok0],
        out_specs=[pl.BlockSpec((GDN_C, 3 * H * GDN_D), lambda h, n: (n_chunks - 1 - n, 0)), tok0,
                   pl.BlockSpec((GDN_C, LANES), lambda h, n: (n_chunks - 1 - n, 0)), par, par, shared],
        out_shape=[jax.ShapeDtypeStruct((T, 3 * H * GDN_D), F32), bf_tok, jax.ShapeDtypeStruct((T, LANES), BF16), par_sh, par_sh,
                   jax.ShapeDtypeStruct((1, GDN_D), F32)],
        scratch_shapes=[pltpu.VMEM((GDN_HP, GDN_D, GDN_D), F32)],
        compiler_params=_cparams(("arbitrary", "arbitrary")),
    )(qkv, qkv, qkv, z, z, z, a_log_x, dt_bias_x, o_norm, states, t_invs, do)


def _prefetch_call(body, name, grid, in_specs, out_specs, out_shape, aliases=None):
    return pl.pallas_call(
        body, name=name,
        grid_spec=pltpu.PrefetchScalarGridSpec(num_scalar_prefetch=1, grid=grid, in_specs=in_specs, out_specs=out_specs),
        out_shape=out_shape, input_output_aliases=aliases or {},
        compiler_params=_cparams(("parallel",) * len(grid)))


def cast_into(src, src_row0, rows, slab, row0, me, name):
    width = src.shape[1]
    tr = _pick_rows(rows, 1024, row0, src_row0)
    assert rows % tr == 0 and row0 % tr == 0 and src_row0 % tr == 0

    def body(me_ref, s_ref, *refs):
        refs[-1][...] = s_ref[...].astype(refs[-1].dtype)

    in_specs = [pl.BlockSpec((tr, width), lambda r, me_ref: (src_row0 // tr + r, 0))]
    args = [src]
    aliases = {}
    if slab.arr is not None:
        in_specs.append(_ANY)
        args.append(slab.arr)
        aliases = {2: 0}
    slab.arr = _prefetch_call(
        body, name, (rows // tr,), in_specs,
        pl.BlockSpec((None, tr, width), lambda r, me_ref: (me_ref[0], row0 // tr + r, 0)),
        jax.ShapeDtypeStruct(slab.shape, slab.dtype), aliases)(me, *args)


def pair_add(g, b, c_idx, name):
    n, rh, w = b.shape
    tr = _pick_rows(rh, 1024)
    nb = rh // tr

    def body(c_ref, g_ref, b_ref, o_ref):
        o_ref[...] = (g_ref[...].astype(F32) + b_ref[...].astype(F32)).astype(o_ref.dtype)

    return _prefetch_call(
        body, name, (n, nb),
        [pl.BlockSpec((None, tr, w), lambda k, r, c: (k, c[0] * nb + r, 0)), pl.BlockSpec((None, tr, w), lambda k, r, c: (k, r, 0))],
        pl.BlockSpec((None, tr, w), lambda k, r, c: (k, r, 0)), jax.ShapeDtypeStruct(b.shape, BF16))(c_idx, g, b)


def chip_sum(p, rv, mc, name):
    n, rh, w = p.shape
    tr = _pick_rows(rh, 512)
    nb = rh // tr

    def body(mc_ref, p_ref, rv_ref, o_ref):
        me = mc_ref[0]
        acc = None
        for k in range(n):
            part = jnp.where(me == k, p_ref[...], rv_ref[k]).astype(F32)
            acc = part if acc is None else acc + part
        o_ref[...] = acc.astype(o_ref.dtype)

    return _prefetch_call(
        body, name, (nb,),
        [pl.BlockSpec((None, tr, w), lambda r, mc_ref: (mc_ref[0], r, 0)), pl.BlockSpec((n, tr, w), lambda r, mc_ref: (0, r, 0))],
        pl.BlockSpec((tr, w), lambda r, mc_ref: (mc_ref[1] * nb + r, 0)), jax.ShapeDtypeStruct((2 * rh, w), BF16))(mc, p, rv)


def adamw(red, row0, w, m, v, w_row0, rows, prev, name):
    cols = w.shape[1]
    tr = _pick_rows(rows, 512, row0, w_row0)
    assert rows % tr == 0 and row0 % tr == 0 and w_row0 % tr == 0

    def body(g_ref, w_ref, m_ref, v_ref, *refs):
        go_ref, d_ref, nm_ref, nv_ref = refs[-4:]
        gv = g_ref[...].astype(F32)
        nm = ADAM_B1 * m_ref[...] + (1.0 - ADAM_B1) * gv
        nv = ADAM_B2 * v_ref[...] + (1.0 - ADAM_B2) * (gv * gv)
        m_hat = nm / (1.0 - ADAM_B1 ** ADAM_STEP)
        v_hat = nv / (1.0 - ADAM_B2 ** ADAM_STEP)
        go_ref[...] = gv
        d_ref[...] = -ADAM_LR * (m_hat / (jnp.sqrt(v_hat) + ADAM_EPS) + ADAM_WD * w_ref[...])
        nm_ref[...] = nm
        nv_ref[...] = nv

    spec = pl.BlockSpec((tr, cols), lambda r: (w_row0 // tr + r, 0))
    sh = jax.ShapeDtypeStruct(w.shape, F32)
    in_specs = [pl.BlockSpec((tr, cols), lambda r: (row0 // tr + r, 0)), spec, spec, spec]
    args, aliases = [red, w, m, v], {}
    if prev is not None:
        in_specs += [_ANY] * 4
        args += list(prev)
        aliases = {4 + k: k for k in range(4)}
    return pl.pallas_call(
        body, name=name, grid=(rows // tr,), in_specs=in_specs, out_specs=[spec] * 4, out_shape=[sh] * 4,
        input_output_aliases=aliases, compiler_params=_cparams(("parallel",)),
    )(*args)


def _place():
    x, y, c = lax.axis_index("x"), lax.axis_index("y"), lax.axis_index("c")
    chips = [(1 - x, y), (x, 1 - y), (1 - x, 1 - y)]
    return x, y, c, chips


def _chip_index(cx, cy):
    return 2 * cx + cy


def _remote(src, dst, send_sem, recv_sem, to):
    return pltpu.make_async_remote_copy(src_ref=src, dst_ref=dst, send_sem=send_sem, recv_sem=recv_sem,
                                        device_id=to, device_id_type=MESH)


def _comm_call(body, name, ins, out_shapes, n_sems, aliases):
    return pl.pallas_call(
        body, name=name, in_specs=[_ANY] * len(ins), out_specs=[_ANY] * len(out_shapes), out_shape=out_shapes,
        scratch_shapes=[pltpu.SemaphoreType.DMA((n_sems,)), pltpu.SemaphoreType.DMA((n_sems,))],
        input_output_aliases=aliases,
    )(*ins)


def pair_swap_halves(slabs, name="grad_pair_swap"):
    n = len(slabs)

    def body(*refs):
        in_refs, out_refs, send_sems, recv_sems = refs[:n], refs[n:2 * n], refs[-2], refs[-1]
        x, y, c, _ = _place()
        cps = []
        for a in range(n):
            rh = in_refs[a].shape[1] // 2
            cp = _remote(in_refs[a].at[:, pl.ds((1 - c) * rh, rh), :], out_refs[a], send_sems.at[a], recv_sems.at[a], (x, y, 1 - c))
            cp.start()
            cps.append(cp)
        for cp in cps:
            cp.wait()

    outs = [jax.ShapeDtypeStruct((s.shape[0], s.shape[1] // 2, s.shape[2]), s.dtype) for s in slabs]
    return _comm_call(body, name, slabs, outs, n, {})


def pair_join_halves(reds, name="grad_pair_join"):
    n = len(reds)

    def body(*refs):
        in_refs, out_refs, send_sems, recv_sems = refs[:n], refs[n:2 * n], refs[-2], refs[-1]
        x, y, c, _ = _place()
        cps = []
        for a in range(n):
            rh = in_refs[a].shape[0] // 2
            mine = pl.ds(c * rh, rh)
            cp = _remote(in_refs[a].at[mine], out_refs[a].at[mine], send_sems.at[a], recv_sems.at[a], (x, y, 1 - c))
            cp.start()
            cps.append(cp)
        for a in range(n):
            rh = in_refs[a].shape[0] // 2
            got = out_refs[a].at[pl.ds((1 - c) * rh, rh)]
            _remote(got, got, send_sems.at[a], recv_sems.at[a], (x, y, 1 - c)).wait_recv()
        for cp in cps:
            cp.wait_send()

    return _comm_call(body, name, reds, [jax.ShapeDtypeStruct(r.shape, r.dtype) for r in reds], n, {a: a for a in range(n)})


_HBM = pl.BlockSpec(memory_space=pltpu.HBM)
_SEM = pl.BlockSpec(memory_space=pltpu.SEMAPHORE)
_EFFECT = pltpu.SideEffectType.DATAFLOW_SIDE_EFFECTING


def _in_hbm(a):
    return pltpu.with_memory_space_constraint(a, pltpu.HBM)


def _hbm_like(a):
    return pltpu.HBM(a.shape, a.dtype)


def _start_call(body, name, ins, n_sems, after):
    n = len(ins)
    res = pl.pallas_call(
        body, name=name, in_specs=[_HBM] * n + [_ANY],
        out_specs=[_SEM, _SEM] + [_HBM] * n + [pl.BlockSpec(memory_space=pltpu.VMEM)],
        out_shape=[pltpu.SemaphoreType.DMA((n_sems,)), pltpu.SemaphoreType.DMA((n_sems,))] + [_hbm_like(a) for a in ins]
        + [jax.ShapeDtypeStruct((8, LANES), F32)],
        input_output_aliases={a: 2 + a for a in range(n)},
        compiler_params=pltpu.CompilerParams(has_side_effects=_EFFECT),
    )(*[_in_hbm(a) for a in ins], after)
    return res[0], res[1], list(res[2:2 + n]), res[-1]


def _wait_call(body, name, thru, send_sems, recv_sems, after):
    n = len(thru)
    after = list(after) if isinstance(after, (list, tuple)) else [after]
    return pl.pallas_call(
        body, name=name, in_specs=[_HBM] * n + [_SEM, _SEM] + [_ANY] * len(after), out_specs=[_HBM] * n,
        out_shape=[_hbm_like(a) for a in thru], input_output_aliases={a: a for a in range(n)},
        compiler_params=pltpu.CompilerParams(has_side_effects=_EFFECT),
    )(*thru, send_sems, recv_sems, *after)


def gather_start(slabs, after, name="weight_gather_start"):
    n = len(slabs)

    def body(*refs):
        g_refs, send_sems, recv_sems, token = refs[:n], refs[n + 1], refs[n + 2], refs[-1]
        x, y, c, chips = _place()
        me = _chip_index(x, y)
        for a in range(n):
            rh = g_refs[a].shape[1] // 2
            mine = g_refs[a].at[me, pl.ds(c * rh, rh)]
            for j, chip in enumerate(chips):
                _remote(mine, mine, send_sems.at[3 * a + j], recv_sems.at[3 * a + j], (*chip, c)).start()
        token[...] = jnp.zeros_like(token)

    return _start_call(body, name, slabs, 3 * n, after)


def gather_wait(send_sems, recv_sems, thru, after, name="weight_gather_wait"):
    n = len(thru)

    def body(*refs):
        g_refs, send_sems, recv_sems = refs[:n], refs[n], refs[n + 1]
        x, y, c, chips = _place()
        me = _chip_index(x, y)
        for a in range(n):
            rh = g_refs[a].shape[1] // 2
            rows = pl.ds(c * rh, rh)
            for j, chip in enumerate(chips):
                mine, got = g_refs[a].at[me, rows], g_refs[a].at[_chip_index(*chip), rows]
                _remote(mine, mine, send_sems.at[3 * a + j], recv_sems.at[3 * a + j], (*chip, c)).wait_send()
                _remote(got, got, send_sems.at[3 * a + j], recv_sems.at[3 * a + j], (*chip, c)).wait_recv()

    return _wait_call(body, name, thru, send_sems, recv_sems, after)


def gather_forward(slabs, name="weight_gather_forward"):
    n = len(slabs)

    def body(*refs):
        in_refs, out_refs, send_sems, recv_sems = refs[:n], refs[n:2 * n], refs[-2], refs[-1]
        x, y, c, chips = _place()
        sib = (x, y, 1 - c)
        sends = []
        for a in range(n):
            rh = in_refs[a].shape[1] // 2
            for j, chip in enumerate(chips):
                k = _chip_index(*chip)
                cp = _remote(in_refs[a].at[k, pl.ds(c * rh, rh)], out_refs[a].at[k, pl.ds(c * rh, rh)], send_sems.at[3 * a + j],
                             recv_sems.at[3 * a + j], sib)
                cp.start()
                sends.append(cp)
        for a in range(n):
            rh = in_refs[a].shape[1] // 2
            for j, chip in enumerate(chips):
                got = out_refs[a].at[_chip_index(*chip), pl.ds((1 - c) * rh, rh)]
                _remote(got, got, send_sems.at[3 * a + j], recv_sems.at[3 * a + j], sib).wait_recv()
        for cp in sends:
            cp.wait_send()

    return _comm_call(body, name, slabs, [jax.ShapeDtypeStruct(s.shape, s.dtype) for s in slabs], 3 * n, {a: a for a in range(n)})


def exchange_start(parts, after, name="grad_exchange_start"):
    n = len(parts)

    def body(*refs):
        p_refs, land_refs, send_sems, recv_sems, token = refs[:n], refs[n:2 * n], refs[2 * n + 1], refs[2 * n + 2], refs[-1]
        x, y, c, chips = _place()
        me = _chip_index(x, y)
        for a in range(n):
            for j, chip in enumerate(chips):
                _remote(p_refs[a].at[_chip_index(*chip)], land_refs[a].at[me], send_sems.at[3 * a + j], recv_sems.at[3 * a + j],
                        (*chip, c)).start()
        token[...] = jnp.zeros_like(token)

    return _start_call(body, name, list(parts) + [lax.empty(p.shape, p.dtype) for p in parts], 3 * n, after)


def swap_start(slabs, after, name="grad_swap_start"):
    n = len(slabs)

    def body(*refs):
        g_refs, land_refs, send_sems, recv_sems, token = refs[:n], refs[n:2 * n], refs[2 * n + 1], refs[2 * n + 2], refs[-1]
        x, y, c, _ = _place()
        for a in range(n):
            rh = g_refs[a].shape[1] // 2
            _remote(g_refs[a].at[:, pl.ds((1 - c) * rh, rh), :], land_refs[a], send_sems.at[a], recv_sems.at[a], (x, y, 1 - c)).start()
        token[...] = jnp.zeros_like(token)

    lands = [lax.empty((s.shape[0], s.shape[1] // 2, s.shape[2]), s.dtype) for s in slabs]
    return _start_call(body, name, list(slabs) + lands, n, after)


def swap_wait(send_sems, recv_sems, thru, after, name="grad_swap_wait"):
    n = len(thru) // 2

    def body(*refs):
        g_refs, land_refs, send_sems, recv_sems = refs[:n], refs[n:2 * n], refs[2 * n], refs[2 * n + 1]
        x, y, c, _ = _place()
        for a in range(n):
            rh = g_refs[a].shape[1] // 2
            cp = _remote(g_refs[a].at[:, pl.ds((1 - c) * rh, rh), :], land_refs[a], send_sems.at[a], recv_sems.at[a], (x, y, 1 - c))
            cp.wait_send()
            cp.wait_recv()

    res = _wait_call(body, name, thru, send_sems, recv_sems, after)
    return res[:n], res[n:]


def exchange_wait(send_sems, recv_sems, thru, after, name="grad_exchange_wait"):
    n = len(thru) // 2

    def body(*refs):
        p_refs, land_refs, send_sems, recv_sems = refs[:n], refs[n:2 * n], refs[2 * n], refs[2 * n + 1]
        x, y, c, chips = _place()
        me = _chip_index(x, y)
        for a in range(n):
            for j, chip in enumerate(chips):
                k = _chip_index(*chip)
                _remote(p_refs[a].at[k], land_refs[a].at[me], send_sems.at[3 * a + j], recv_sems.at[3 * a + j], (*chip, c)).wait_send()
                _remote(land_refs[a].at[k], land_refs[a].at[k], send_sems.at[3 * a + j], recv_sems.at[3 * a + j], (*chip, c)).wait_recv()

    res = _wait_call(body, name, thru, send_sems, recv_sems, after)
    return res[:n], res[n:]


_SLABS = {
    "mla_w_in": [("mla_w_in", 1, 0, 2)], "mla_w_uq": [("mla_w_uq", 2, 0, 2)], "mla_w_ukv": [("mla_w_ukv", 2, 0, 2)],
    "l0_mla_w_o": [("mla_w_o", 1, 0, 1)],
    "l0_w1024": [("mlp_w1", 2, 0, 1), ("mlp_w2", 1, 0, 1), ("xa_w_q", 1, 0, 1), ("xa_w_o", 1, 0, 1)],
    "l0_xa_w_kv": [("xa_w_kv", 2, 0, 1)],
    "l1_w1024": [("mlp_w1", 2, 1, 2), ("mlp_w2", 1, 1, 2), ("xa_w_q", 1, 1, 2), ("xa_w_o", 1, 1, 2), ("gdn_w_o", 1, 0, 1)],
    "l1_xa_w_kv": [("xa_w_kv", 2, 1, 2)], "gdn_w_in": [("gdn_w_in", 2, 0, 1)],
    "l23_w1024": [("mlp_w1", 2, 2, 4), ("mlp_w2", 1, 2, 4), ("xa_w_q", 1, 2, 4), ("xa_w_o", 1, 2, 4), ("mla_w_o", 1, 1, 2),
                  ("sc_w_o", 1, 0, 1)],
    "l23_xa_w_kv": [("xa_w_kv", 2, 2, 4)], "sc_w_in": [("sc_w_in", 2, 0, 1)],
}
_GROUPS = [(["mla_w_in", "mla_w_uq", "mla_w_ukv", "l0_mla_w_o"], None),
           (["l0_w1024", "l0_xa_w_kv"], (0, "xa")),
           (["l1_w1024", "l1_xa_w_kv", "gdn_w_in"], (1, "mix")),
           (["l23_w1024", "l23_xa_w_kv", "sc_w_in"], (2, "mix"))]
_SWAP_DONE = {(2, "mix"): (1, "mlp"), (1, "mix"): (0, "mlp")}
_RELAID = ("mla_w_in", "mla_w_uq", "mla_w_ukv", "gdn_w_in")
_SMALL = [("mla_q_norm", 1), ("mla_kv_norm", 1), ("gdn_conv_w", 2), ("sc_conv_w", 2)]
_REPL = ["gdn_a_log", "gdn_dt_bias", "gdn_o_norm", "norm_mix", "norm_mem", "norm_mlp", "mem_norm", "final_norm"]
_WEIGHTS = ['mla_w_in', 'mla_q_norm', 'mla_kv_norm', 'mla_w_uq', 'mla_w_ukv', 'mla_w_o', 'gdn_w_in', 'gdn_conv_w',
            'gdn_a_log', 'gdn_dt_bias', 'gdn_o_norm', 'gdn_w_o', 'sc_w_in', 'sc_conv_w', 'sc_w_o', 'norm_mix',
            'norm_mem', 'norm_mlp', 'xa_w_q', 'xa_w_kv', 'xa_w_o', 'mlp_w1', 'mlp_w2', 'mem_norm', 'final_norm']


class Layout:
    def __init__(self, shard_shapes):
        self.members, self.where, self.slab_dims = {}, {}, {}
        for slab, members in _SLABS.items():
            off, rows = 0, []
            for name, axis, l0, l1 in members:
                _, rpl, width = shard_shapes[name]
                rows.append((name, off, l0, l1, rpl))
                for layer in range(l0, l1):
                    self.where[(name, layer)] = (slab, off + (layer - l0) * rpl, rpl, width, axis)
                off += (l1 - l0) * rpl
            self.members[slab], self.slab_dims[slab] = rows, (off, width)

    def new_slabs(self, dtype):
        return {s: Slab(rows, width, dtype) for s, (rows, width) in self.slab_dims.items()}

    def loc(self, slabs, name, layer):
        slab, row0, rpl, width, axis = self.where[(name, layer)]
        if axis == 1:
            return Loc(slabs[slab], row0, N_CHIPS * rpl, width, 0)
        return Loc(slabs[slab], row0, rpl, N_CHIPS * width, 1)

    def _whole(self, name):
        (member,) = self.members[name]
        _, off, l0, l1, rpl = member
        assert off == 0 and l0 == 0
        return l1, rpl, self.slab_dims[name][1], dict((n, a) for n, a, _, _ in _SLABS[name])[name]

    def full(self, slabs, name):
        layers, rpl, width, axis = self._whole(name)
        blocks = slabs[name].arr.reshape(N_CHIPS, layers, rpl, width)
        return jnp.concatenate([blocks[s] for s in range(N_CHIPS)], axis=axis)

    def put_full(self, slabs, name, grad):
        layers, rpl, width, axis = self._whole(name)
        parts = jnp.stack(jnp.split(grad, N_CHIPS, axis=axis)).reshape(N_CHIPS, layers * rpl, width)
        slabs[name].arr = parts.astype(slabs[name].dtype)


def _small_pack(vals, names):
    flat = jnp.concatenate([vals[n].astype(F32).reshape(-1) for n in names])
    return jnp.pad(flat, (0, SMALL_ROWS * SMALL_COLS - flat.shape[0])).reshape(SMALL_ROWS, SMALL_COLS)


def _small_unpack(flat, like, names):
    out, off = {}, 0
    flat = flat.reshape(-1)
    for n in names:
        out[n] = flat[off:off + like[n].size].reshape(like[n].shape)
        off += like[n].size
    return out


_MLA_CFG = _Attn(MLA_H, 2 * LANES, MLA_NOPE, MLA_V, True, (MLA_NOPE + MLA_ROPE) ** -0.5, hp=8, hp_kv=8, blk=512)
_XA_CFG = _Attn(XA_H, XA_D, XA_D, XA_D, False, XA_D ** -0.5, hp=4, hp_kv=4, blk=1024)


def _mla_weights(w_in, w_uq, w_ukv):
    w_in_p = jnp.pad(w_in, ((0, 0), (0, MLA_ZPAD - w_in.shape[1])))
    w_uq_p = jnp.pad(w_uq.reshape(MLA_QR, MLA_H, MLA_NOPE + MLA_ROPE), ((0, 0), (0, 0), (0, 2 * LANES - MLA_NOPE - MLA_ROPE)))
    w_uq_p = w_uq_p.reshape(MLA_QR, MLA_H * 2 * LANES)
    kv = w_ukv.reshape(MLA_KVR, MLA_H, MLA_NOPE + MLA_V)
    w_ukv_p = jnp.concatenate([kv[:, :, :MLA_NOPE].reshape(MLA_KVR, -1), kv[:, :, MLA_NOPE:].reshape(MLA_KVR, -1)], axis=1)
    return w_in_p, w_uq_p, w_ukv_p


def _mla_weight_grads(d_in_p, d_uq_p, d_ukv_p):
    d_in = d_in_p[:, :MLA_QR + MLA_KVR + MLA_ROPE]
    d_uq = d_uq_p.reshape(MLA_QR, MLA_H, 2 * LANES)[:, :, :MLA_NOPE + MLA_ROPE].reshape(MLA_QR, -1)
    half = MLA_H * MLA_NOPE
    d_ukv = jnp.concatenate([d_ukv_p[:, :half].reshape(MLA_KVR, MLA_H, MLA_NOPE),
                             d_ukv_p[:, half:].reshape(MLA_KVR, MLA_H, MLA_V)], axis=2).reshape(MLA_KVR, -1)
    return d_in, d_uq, d_ukv


def _mla_fwd(xs, h, wts, w_o, qn, kvn, tabs, g_next, tag):
    w_in_p, w_uq_p, w_ukv_p = wts
    z = mm(h, w_in_p, "nn", f"{tag}_in")
    cq, ckv, kr = mla_mid_fwd(z, qn, kvn, tabs, f"{tag}_mid")
    q = mm(cq, w_uq_p, "nn", f"{tag}_uq", outs=(BF16,), epi=_epi_rope_q, per_row=tabs, tm=512)
    kv = mm(ckv, w_ukv_p, "nn", f"{tag}_ukv", outs=(BF16,))
    o, lse = flash_fwd(_MLA_CFG, q, kv, kv, kr, f"{tag}_attn")
    xs, h_next = residual_norm(o, w_o, xs, g_next, f"{tag}_out")
    return xs, h_next, (z, cq, ckv, kr, q, kv, o, lse)


def _mla_bwd(dx, h, wts, w_o, g_wo, qn, kvn, tabs, saved, tag):
    w_in_p, w_uq_p, w_ukv_p = wts
    z, cq, ckv, kr, q, kv, o, lse = saved
    mm(o, dx, "tn", f"{tag}_dwo", outs=(BF16,), out_loc=g_wo)
    do = mm(dx, w_o, "nt", f"{tag}_do", outs=(BF16,))
    dqp, delta = flash_dq(_MLA_CFG, q, kv, kv, kr, o, do, lse, BF16, f"{tag}_attn_dq", rope_tabs=tabs)
    dkv, dkr = flash_dkv(_MLA_CFG, q, kv, kv, kr, do, lse, delta, BF16, f"{tag}_attn_dkv")
    d_uq_p = mm(cq, dqp, "tn", f"{tag}_duq")
    dcq = mm(dqp, w_uq_p, "nt", f"{tag}_dcq")
    d_ukv_p = mm(ckv, dkv, "tn", f"{tag}_dukv")
    dckv = mm(dkv, w_ukv_p, "nt", f"{tag}_dckv")
    dz, dqn, dkvn = mla_mid_bwd(z, qn, kvn, tabs, dcq, dckv, dkr, f"{tag}_mid_bwd")
    d_in_p = mm(h, dz, "tn", f"{tag}_din")
    dh = (dz, w_in_p)
    d_in, d_uq, d_ukv = _mla_weight_grads(d_in_p, d_uq_p, d_ukv_p)
    return dh, dict(mla_w_in=d_in, mla_w_uq=d_uq, mla_w_ukv=d_ukv, mla_q_norm=dqn, mla_kv_norm=dkvn)


_GDN_QKV = 3 * GDN_H * GDN_D
_GDN_GATE_END = _GDN_QKV + GDN_H * GDN_D


def _gdn_weights(w_in):
    rep = lambda cols: jnp.repeat(cols, GDN_D, axis=1)
    return jnp.concatenate([w_in[:, :_GDN_GATE_END], rep(w_in[:, _GDN_GATE_END:_GDN_GATE_END + GDN_H]),
                            rep(w_in[:, _GDN_GATE_END + GDN_H:])], axis=1)


def _fold(x):
    return x.reshape(x.shape[0], -1, GDN_D).sum(-1)


def _gdn_fwd(xs, h, w_in_x, conv_w, a_log, dt_bias, o_norm, w_o, g_next, tag):
    z = mm(h, w_in_x, "nn", f"{tag}_in")
    qkv = gdn_conv_fwd(z, conv_w, f"{tag}_conv")
    a_x, dt_x = jnp.repeat(a_log.reshape(1, -1), GDN_D, axis=1), jnp.repeat(dt_bias.reshape(1, -1), GDN_D, axis=1)
    og, states, t_invs = gdn_chunk_fwd(qkv, z, a_x, dt_x, o_norm.reshape(1, -1), f"{tag}_chunks")
    xs, h_next = residual_norm(og, w_o, xs, g_next, f"{tag}_out")
    return xs, h_next, (z, qkv, a_x, dt_x, og, states, t_invs)


def _gdn_weights_compact(w_in):
    return jnp.pad(w_in, ((0, 0), (0, LANES - 2 * GDN_H)))


def _gdn_bwd(dx, h, w_in_c, conv_w, o_norm, w_o, g_wo, saved, tag):
    z, qkv, a_x, dt_x, og, states, t_invs = saved
    mm(og, dx, "tn", f"{tag}_dwo", outs=(BF16,), out_loc=g_wo)
    dog = mm(dx, w_o, "nt", f"{tag}_dog")
    dqkv, dgate, dba, da_x, ddt_x, don = gdn_chunk_bwd(qkv, z, a_x, dt_x, o_norm.reshape(1, -1), states, t_invs, dog,
                                                       f"{tag}_chunks_bwd")
    dpre, dconv = gdn_conv_bwd(z, conv_w, dqkv, f"{tag}_conv_bwd")
    dz = jnp.concatenate([dpre, dgate, dba], axis=1)
    d_in_c = mm(h, dz, "tn", f"{tag}_din")
    dh = (dz, w_in_c)
    return dh, dict(gdn_w_in=d_in_c[:, :_GDN_GATE_END + 2 * GDN_H], gdn_conv_w=dconv, gdn_a_log=_fold(da_x).reshape(-1),
                    gdn_dt_bias=_fold(ddt_x).reshape(-1), gdn_o_norm=don.reshape(-1))


def _sc_fwd(xs, h, w_in, conv_w, w_o, g_next, tag):
    z = mm(h, w_in, "nn", f"{tag}_in")
    y = sc_fwd(z, conv_w, f"{tag}_conv")
    xs, h_next = residual_norm(y, w_o, xs, g_next, f"{tag}_out")
    return xs, h_next, (z, y)


def _sc_bwd(dx, h, w_in, g_win, conv_w, w_o, g_wo, saved, tag):
    z, y = saved
    mm(y, dx, "tn", f"{tag}_dwo", outs=(BF16,), out_loc=g_wo)
    dy = mm(dx, w_o, "nt", f"{tag}_dy")
    db, dc, du, dconv = sc_bwd(z, conv_w, dy, f"{tag}_conv_bwd")
    dz = jnp.concatenate([db, dc, du], axis=1)
    mm(h, dz, "tn", f"{tag}_din", outs=(BF16,), out_loc=g_win)
    dh = (dz, w_in)
    return dh, dict(sc_conv_w=dconv)


def local_step(x, mem, pos, target, lay, wslabs, gslabs, small, before=None, after_bwd=None):
    depth = small["norm_mix"].shape[0]
    W = lambda name, layer: lay.loc(wslabs, name, layer)
    G = lambda name, layer: lay.loc(gslabs, name, layer)
    tabs = rope_tables(pos)
    mem_n = rmsnorm_fwd(mem, small["mem_norm"], "mem_norm")
    full = {n: lay.full(wslabs, n) for n in ("mla_w_in", "mla_w_uq", "mla_w_ukv")}
    mla_w = [_mla_weights(full["mla_w_in"][j], full["mla_w_uq"][j], full["mla_w_ukv"][j]) for j in range(full["mla_w_in"].shape[0])]
    gdn_in_x, gdn_in_c = {}, {}

    xs, h_pre = x, None
    saved = []
    for i in range(depth):
        j, kind = i // 3, i % 3
        tag = f"l{i}"
        if before is not None:
            xs = before(i, "mix", xs)
        if kind == 1:
            gdn_full = lay.full(wslabs, "gdn_w_in")[j]
            gdn_in_x[j], gdn_in_c[j] = _gdn_weights(gdn_full), _gdn_weights_compact(gdn_full)
        x_a = xs
        h = h_pre if h_pre is not None else rmsnorm_fwd(xs, small["norm_mix"][i], f"{tag}_norm_mix")
        g_mem = small["norm_mem"][i]
        if kind == 0:
            xs, hn, mix = _mla_fwd(xs, h, mla_w[j], W("mla_w_o", j), small["mla_q_norm"][j], small["mla_kv_norm"][j], tabs, g_mem,
                                   f"{tag}_mla")
        elif kind == 1:
            xs, hn, mix = _gdn_fwd(xs, h, gdn_in_x[j], small["gdn_conv_w"][j], small["gdn_a_log"][j], small["gdn_dt_bias"][j],
                                   small["gdn_o_norm"][j], W("gdn_w_o", j), g_mem, f"{tag}_gdn")
        else:
            xs, hn, mix = _sc_fwd(xs, h, W("sc_w_in", j), small["sc_conv_w"][j], W("sc_w_o", j), g_mem, f"{tag}_sc")
        if before is not None:
            xs = before(i, "xa", xs)
        x_b = xs
        xq = mm(hn, W("xa_w_q", i), "nn", f"{tag}_xa_q", outs=(BF16,))
        xkv = mm(mem_n, W("xa_w_kv", i), "nn", f"{tag}_xa_kv", outs=(BF16,))
        xo, xlse = flash_fwd(_XA_CFG, xq, xkv, xkv, None, f"{tag}_xa_attn")
        xs, hm = residual_norm(xo, W("xa_w_o", i), xs, small["norm_mlp"][i], f"{tag}_xa_out")
        x_c = xs
        h1, act = mm(hm, W("mlp_w1", i), "nn", f"{tag}_mlp_up", outs=(BF16, BF16), epi=_epi_relu2)
        xs, h_pre = residual_norm(act, W("mlp_w2", i), xs, small["norm_mix"][i + 1] if i + 1 < depth else None,
                                  f"{tag}_mlp_down", tm=512)
        saved.append((x_a, h, mix, x_b, hn, xq, xkv, xo, xlse, x_c, hm, h1, act))

    se, dx, d_final = loss_head(xs, small["final_norm"], target)

    per_layer = {n: [None] * depth for n in ("norm_mix", "norm_mem", "norm_mlp")}
    mixer = {}
    dmem_n = jnp.zeros(mem.shape, F32)
    for i in reversed(range(depth)):
        j, kind = i // 3, i % 3
        tag = f"l{i}"
        x_a, h, mix, x_b, hn, xq, xkv, xo, xlse, x_c, hm, h1, act = saved[i]
        mm(act, dx, "tn", f"{tag}_mlp_dw2", outs=(BF16,), out_loc=G("mlp_w2", i))
        dh1 = mm(dx, W("mlp_w2", i), "nt", f"{tag}_mlp_dh1", outs=(BF16,), epi=_epi_relu2_bwd, extras=(h1,))
        mm(hm, dh1, "tn", f"{tag}_mlp_dw1", outs=(BF16,), out_loc=G("mlp_w1", i))
        dx, dg = mm(dh1, W("mlp_w1", i), "nt", f"{tag}_mlp_dhm", epi=_epi_norm_bwd, extras=(x_c, dx), vecs=(small["norm_mlp"][i],),
                    row_outs=1, tm=512)
        per_layer["norm_mlp"][i] = dg.reshape(-1)
        if after_bwd is not None:
            dx = after_bwd(i, "mlp", dx)
        mm(xo, dx, "tn", f"{tag}_xa_dwo", outs=(BF16,), out_loc=G("xa_w_o", i))
        dxo = mm(dx, W("xa_w_o", i), "nt", f"{tag}_xa_do", outs=(BF16,))
        dxq, xdelta = flash_dq(_XA_CFG, xq, xkv, xkv, None, xo, dxo, xlse, BF16, f"{tag}_xa_attn_dq")
        (dxkv,) = flash_dkv(_XA_CFG, xq, xkv, xkv, None, dxo, xlse, xdelta, BF16, f"{tag}_xa_attn_dkv")
        mm(hn, dxq, "tn", f"{tag}_xa_dwq", outs=(BF16,), out_loc=G("xa_w_q", i))
        dx, dg = mm(dxq, W("xa_w_q", i), "nt", f"{tag}_xa_dhn", epi=_epi_norm_bwd, extras=(x_b, dx), vecs=(small["norm_mem"][i],),
                    row_outs=1, tm=512)
        per_layer["norm_mem"][i] = dg.reshape(-1)
        mm(mem_n, dxkv, "tn", f"{tag}_xa_dwkv", outs=(BF16,), out_loc=G("xa_w_kv", i))
        dmem_n = mm(dxkv, W("xa_w_kv", i), "nt", f"{tag}_xa_dmem", epi=_epi_add, extras=(dmem_n,))
        if after_bwd is not None:
            dx = after_bwd(i, "xa", dx)
        if kind == 0:
            dh, gr = _mla_bwd(dx, h, mla_w[j], W("mla_w_o", j), G("mla_w_o", j), small["mla_q_norm"][j], small["mla_kv_norm"][j],
                              tabs, mix, f"{tag}_mla")
        elif kind == 1:
            dh, gr = _gdn_bwd(dx, h, gdn_in_c[j], small["gdn_conv_w"][j], small["gdn_o_norm"][j], W("gdn_w_o", j), G("gdn_w_o", j),
                              mix, f"{tag}_gdn")
        else:
            dh, gr = _sc_bwd(dx, h, W("sc_w_in", j), G("sc_w_in", j), small["sc_conv_w"][j], W("sc_w_o", j), G("sc_w_o", j),
                             mix, f"{tag}_sc")
        if kind == 1:
            lay.put_full(gslabs, "gdn_w_in", gr.pop("gdn_w_in")[None])
        for n, g in gr.items():
            mixer.setdefault(n, {})[j] = g
        dz_mix, w_mix = dh
        dx, dg = mm(dz_mix, w_mix, "nt", f"{tag}_mix_dh", epi=_epi_norm_bwd, extras=(x_a, dx), vecs=(small["norm_mix"][i],),
                    row_outs=1, tm=256 if kind == 1 else 512)
        per_layer["norm_mix"][i] = dg.reshape(-1)
        if after_bwd is not None:
            dx = after_bwd(i, "mix", dx)

    _, d_mem_norm = rmsnorm_bwd(mem, small["mem_norm"], dmem_n, jnp.zeros(mem.shape, F32), "mem_norm_bwd")
    grads = {n: jnp.stack(v) for n, v in per_layer.items()}
    for n, by_j in mixer.items():
        grads[n] = jnp.stack([by_j[j] for j in sorted(by_j)])
    grads["mem_norm"] = d_mem_norm
    grads["final_norm"] = d_final
    for n in ("mla_w_in", "mla_w_uq", "mla_w_ukv"):
        lay.put_full(gslabs, n, grads.pop(n))
    return se, dx, grads


def kernel(x, mem, positions, mla_w_in, mla_q_norm, mla_kv_norm, mla_w_uq, mla_w_ukv, mla_w_o, gdn_w_in, gdn_conv_w, gdn_a_log, gdn_dt_bias, gdn_o_norm, gdn_w_o, sc_w_in, sc_conv_w, sc_w_o, norm_mix, norm_mem, norm_mlp, xa_w_q, xa_w_kv, xa_w_o, mlp_w1, mlp_w2, mem_norm, final_norm, loss_target, m_mla_w_in, m_mla_q_norm, m_mla_kv_norm, m_mla_w_uq, m_mla_w_ukv, m_mla_w_o, m_gdn_w_in, m_gdn_conv_w, m_gdn_a_log, m_gdn_dt_bias, m_gdn_o_norm, m_gdn_w_o, m_sc_w_in, m_sc_conv_w, m_sc_w_o, m_norm_mix, m_norm_mem, m_norm_mlp, m_xa_w_q, m_xa_w_kv, m_xa_w_o, m_mlp_w1, m_mlp_w2, m_mem_norm, m_final_norm, v_mla_w_in, v_mla_q_norm, v_mla_kv_norm, v_mla_w_uq, v_mla_w_ukv, v_mla_w_o, v_gdn_w_in, v_gdn_conv_w, v_gdn_a_log, v_gdn_dt_bias, v_gdn_o_norm, v_gdn_w_o, v_sc_w_in, v_sc_conv_w, v_sc_w_o, v_norm_mix, v_norm_mem, v_norm_mlp, v_xa_w_q, v_xa_w_kv, v_xa_w_o, v_mlp_w1, v_mlp_w2, v_mem_norm, v_final_norm):
    given = dict(locals())
    p = {n: given[n] for n in _WEIGHTS}
    mom = {n: given["m_" + n] for n in _WEIGHTS}
    var = {n: given["v_" + n] for n in _WEIGHTS}
    split = sorted({n for members in _SLABS.values() for n, _, _, _ in members})
    lay = Layout({n: p[n].shape for n in split})
    flat2d = lambda a: a.reshape(-1, a.shape[-1])

    me = (2 * lax.axis_index("x") + lax.axis_index("y")).astype(jnp.int32)
    core = lax.axis_index("c").astype(jnp.int32)
    me1, c1, mc = me.reshape(1), core.reshape(1), jnp.stack([me, core])

    wslabs = lay.new_slabs(BF16)

    def cast_group(slabs, chip):
        for slab in slabs:
            for name, off, l0, l1, rpl in lay.members[slab]:
                cast_into(flat2d(p[name]), l0 * rpl, (l1 - l0) * rpl, wslabs[slab], off, chip, f"cast_{slab}_{name}")

    first = _GROUPS[0][0]
    cast_group(first, me1)
    small_names = [n for n, _ in _SMALL]
    words = lax.bitcast_convert_type(jnp.concatenate([p[n].reshape(-1) for n in small_names]), BF16).reshape(-1)
    words = jnp.pad(words, (0, SMALL_ROWS * SMALL_COLS - words.shape[0])).reshape(1, SMALL_ROWS, SMALL_COLS)
    small_slab = lax.dynamic_update_slice(jnp.zeros((N_CHIPS, SMALL_ROWS, SMALL_COLS), BF16), words, (me, 0, 0))

    send0, recv0, thru0, token = gather_start([wslabs[s].arr for s in first] + [small_slab], me1, "weight_gather_start_first")
    in_flight = {}
    for slabs, point in _GROUPS[1:]:
        cast_group(slabs, me1 + token[0, 0].astype(jnp.int32))
        send, recv, thru, token = gather_start([wslabs[s].arr for s in slabs], token, f"weight_gather_start_{slabs[0]}")
        in_flight[point] = (send, recv, thru, slabs)
    started_token = token
    landed = gather_wait(send0, recv0, thru0, started_token, "weight_gather_wait_first")
    gathered = gather_forward(landed, "weight_gather_forward_first")
    for s, arr in zip(first, gathered):
        wslabs[s].arr = arr

    def before(i, stage, xs):
        if (i, stage) in in_flight:
            send, recv, thru, slabs = in_flight[(i, stage)]
            landed = gather_wait(send, recv, thru, xs, f"weight_gather_wait_{slabs[0]}")
            for s, arr in zip(slabs, gather_forward(landed, f"weight_gather_forward_{slabs[0]}")):
                wslabs[s].arr = arr
        return xs

    small = {n: p[n] for n in _REPL}
    got, off = gathered[-1].reshape(N_CHIPS, -1), 0
    for n, ax in _SMALL:
        vals = lax.bitcast_convert_type(got[:, off:off + 2 * p[n].size].reshape(N_CHIPS, p[n].size, 2), F32)
        vals = vals.reshape((N_CHIPS,) + p[n].shape)
        small[n] = jnp.concatenate([vals[s] for s in range(N_CHIPS)], axis=ax)
        off += 2 * p[n].size

    gslabs = lay.new_slabs(BF16)
    complete_at = {point: slabs for slabs, point in _GROUPS[1:]}
    swapping, exchanging = {}, []

    def after_bwd(i, stage, dx):
        if (i, stage) in swapping:
            slabs, send, recv, thru = swapping.pop((i, stage))
            g, swapped = swap_wait(send, recv, thru, dx, f"grad_swap_wait_{slabs[0]}")
        elif (i, stage) in complete_at:
            slabs = complete_at[(i, stage)]
            g = [gslabs[s].arr for s in slabs]
            if (i, stage) in _SWAP_DONE:
                send, recv, thru, token = swap_start(g, c1, f"grad_swap_start_{slabs[0]}")
                swapping[_SWAP_DONE[(i, stage)]] = (slabs, send, recv, thru)
                return dx + token[0, 0]
            swapped = pair_swap_halves(g, f"grad_pair_swap_{slabs[0]}")
        else:
            return dx
        part = [pair_add(a, b, c1, f"pair_add_{s}") for a, b, s in zip(g, swapped, slabs)]
        send, recv, thru, token = exchange_start(part, c1, f"grad_exchange_start_{slabs[0]}")
        exchanging.append((slabs, send, recv, thru))
        return dx + token[0, 0]

    se, dx, sgrads = local_step(x[0], mem[0], positions.reshape(-1, 1), loss_target[0], lay, wslabs, gslabs, small,
                                before, after_bwd)
    loss = lax.psum(0.5 * jnp.sum(se) / x.shape[-1], ("x", "y", "c"))
    names, parts, received = [], [], []
    for slabs, send, recv, thru in exchanging:
        part, got = exchange_wait(send, recv, thru, dx, f"grad_exchange_wait_{slabs[0]}")
        names, parts, received = names + slabs, parts + list(part), received + list(got)

    axes = dict(_SMALL)
    small_order = small_names + _REPL
    slots = []
    for s in range(N_CHIPS):
        vals = {n: (lax.slice_in_dim(g, s * p[n].shape[axes[n]], (s + 1) * p[n].shape[axes[n]], axis=axes[n]) if n in axes else g)
                for n, g in sgrads.items()}
        slots.append(_small_pack(vals, small_order))
    g_last = [gslabs[s].arr for s in first] + [jnp.stack(slots).astype(BF16)]
    names_last = first + ["small"]
    swapped_last = pair_swap_halves(g_last, "grad_pair_swap_last")
    part_last = [pair_add(g, b, c1, f"pair_add_{s}") for g, b, s in zip(g_last, swapped_last, names_last)]
    send, recv, thru, token = exchange_start(part_last, c1, "grad_exchange_start_last")
    mc_after = mc + token[0, 0].astype(jnp.int32)
    halves = [chip_sum(q, r, mc_after, f"chip_sum_{s}") for q, r, s in zip(parts, received, names)]
    part_last, got_last = exchange_wait(send, recv, thru, list(halves), "grad_exchange_wait_last")
    halves += [chip_sum(q, r, mc, f"chip_sum_{s}") for q, r, s in zip(part_last, got_last, names_last)]
    reduced = dict(zip(names + names_last, pair_join_halves(halves)))

    res = {}
    for slab in _SLABS:
        for name, off, l0, l1, rpl in lay.members[slab]:
            res[name] = adamw(reduced[slab], off, flat2d(p[name]), flat2d(mom[name]), flat2d(var[name]), l0 * rpl, (l1 - l0) * rpl,
                              res.get(name), f"adamw_{slab}_{name}")
    for name in split:
        res[name] = [o.reshape(p[name].shape) for o in res[name]]
    sp = {k: _small_pack(d, small_order) for k, d in (("w", p), ("m", mom), ("v", var))}
    outs = adamw(reduced["small"], 0, sp["w"], sp["m"], sp["v"], 0, SMALL_ROWS, None, "adamw_small")
    unpacked = [_small_unpack(o, p, small_order) for o in outs]
    for n in small_order:
        res[n] = [u[n] for u in unpacked]
    return (loss, dx[None], *[res[n][k] for k in range(4) for n in _WEIGHTS])
```

```python
import jax
import jax.numpy as jnp
from jax import lax
from jax.experimental import pallas as pl
from jax.experimental.pallas import tpu as pltpu

F32 = jnp.float32
BF16 = jnp.bfloat16
MESH = pl.DeviceIdType.MESH

EPS = 1e-6
ROPE_THETA = 10000.0
N_CHIPS = 4
LANES = 128
VMEM_LIMIT = 56 * 1024 * 1024
NEG = -1e30

MLA_H, MLA_NOPE, MLA_ROPE, MLA_V = 8, 128, 64, 128
MLA_QR, MLA_KVR = 384, 256
MLA_ZPAD = 768
GDN_H, GDN_D, GDN_C = 8, 128, 64
XA_H, XA_D = 4, 256

ADAM_LR, ADAM_B1, ADAM_B2, ADAM_EPS, ADAM_WD, ADAM_STEP = 0.001, 0.9, 0.999, 1e-08, 0.01, 10

SMALL_ROWS, SMALL_COLS = 32, 1024


def _cparams(sem=None):
    return pltpu.CompilerParams(dimension_semantics=sem, vmem_limit_bytes=VMEM_LIMIT)


def _pick(dim, pref):
    t = (min(pref, dim) // LANES) * LANES
    while t >= LANES:
        if dim % t == 0:
            return t
        t -= LANES
    return dim


def _pick_rows(rows, pref, *offsets):
    t = (min(pref, rows) // 16) * 16
    while t > 16 and (rows % t or any(o % t for o in offsets)):
        t -= 16
    return t


class Slab:
    def __init__(self, rows, width, dtype, arr=None):
        self.shape, self.dtype, self.arr = (N_CHIPS, rows, width), dtype, arr


class Loc:
    def __init__(self, slab, row0, K, N, axis):
        self.slab, self.row0, self.K, self.N, self.axis = slab, row0, K, N, axis
        self.Ks = K // N_CHIPS if axis == 0 else K
        self.Ns = N // N_CHIPS if axis == 1 else N

    def tile_spec(self, tr, tc, rc):
        assert self.row0 % tr == 0 and self.Ks % tr == 0 and self.Ns % tc == 0, (self.row0, self.Ks, self.Ns, tr, tc)
        r0, rb, cb = self.row0 // tr, self.Ks // tr, self.Ns // tc
        if self.axis == 0:
            return pl.BlockSpec((None, tr, tc), lambda i, j: (rc(i, j)[0] // rb, r0 + rc(i, j)[0] % rb, rc(i, j)[1]))
        return pl.BlockSpec((None, tr, tc), lambda i, j: (rc(i, j)[1] // cb, r0 + rc(i, j)[0], rc(i, j)[1] % cb))

    def slot_spec(self, slot, tr, tc, rc):
        assert self.row0 % tr == 0, (self.row0, tr)
        r0 = self.row0 // tr
        return pl.BlockSpec((None, tr, tc), lambda i, j: (slot, r0 + rc(i, j)[0], rc(i, j)[1]))


_DIMS = {"nn": ((1,), (0,)), "nt": ((1,), (1,)), "tn": ((0,), (0,))}
_ANY = pl.BlockSpec(memory_space=pl.ANY)


def mm(a, b, mode, name, outs=(F32,), epi=None, extras=(), tm=1024, tn=1024, out_loc=None, vecs=(), row_outs=0, per_row=()):
    full_rows = bool(vecs) or row_outs > 0 or bool(per_row)
    b_loc = b if isinstance(b, Loc) else None
    if mode == "nn":
        M, K = a.shape
        K2, N = (b_loc.K, b_loc.N) if b_loc else b.shape
    elif mode == "nt":
        M, K = a.shape
        N, K2 = (b_loc.K, b_loc.N) if b_loc else b.shape
    else:
        K, M = a.shape
        K2, N = b.shape
    assert K == K2, (name, a.shape, K2, N)
    tm = _pick(out_loc.Ks if (out_loc and out_loc.axis == 0) else M, tm)
    n_split = full_rows and b_loc is not None and mode == "nt" and b_loc.axis == 0
    if out_loc is not None and out_loc.axis == 1:
        tn = _pick(out_loc.Ns, tn)
    elif n_split:
        tn = N
    elif b_loc is not None and ((mode == "nn" and b_loc.axis == 1) or (mode == "nt" and b_loc.axis == 0)):
        tn = _pick(b_loc.Ns if mode == "nn" else b_loc.Ks, tn)
    elif b_loc is not None:
        tn = N if full_rows else _pick(N, min(tn, 512))
    else:
        tn = N if full_rows else _pick(N, tn)
    assert tn == N or not full_rows, name

    parts = 1
    if mode == "tn":
        a_spec = pl.BlockSpec((K, tm), lambda i, j: (0, i))
        b_specs, b_args = [pl.BlockSpec((K, tn), lambda i, j: (0, j))], [b]
    else:
        a_spec = pl.BlockSpec((tm, K), lambda i, j: (i, 0))
        if b_loc is None:
            b_specs = [pl.BlockSpec((K, tn), lambda i, j: (0, j)) if mode == "nn" else pl.BlockSpec((tn, K), lambda i, j: (j, 0))]
            b_args = [b]
        elif mode == "nn" and b_loc.axis == 1:
            b_specs, b_args = [b_loc.tile_spec(K, tn, lambda i, j: (0, j))], [b_loc.slab.arr]
        elif n_split:
            b_specs = [b_loc.slot_spec(s, b_loc.Ks, K, lambda i, j: (0, 0)) for s in range(N_CHIPS)]
            b_args = [b_loc.slab.arr] * N_CHIPS
        elif mode == "nt" and b_loc.axis == 0:
            b_specs, b_args = [b_loc.tile_spec(tn, K, lambda i, j: (j, 0))], [b_loc.slab.arr]
        elif mode == "nn":
            parts = N_CHIPS
            b_specs = [b_loc.slot_spec(s, b_loc.Ks, tn, lambda i, j: (0, j)) for s in range(parts)]
            b_args = [b_loc.slab.arr] * parts
        else:
            parts = N_CHIPS
            b_specs = [b_loc.slot_spec(s, tn, b_loc.Ns, lambda i, j: (j, 0)) for s in range(parts)]
            b_args = [b_loc.slab.arr] * parts
    kp = K // parts
    n_b = N_CHIPS if n_split else parts
    n_ex, n_out = len(extras) + len(per_row) + len(vecs), len(outs)
    dims = (_DIMS[mode], ((), ()))

    def body(*refs):
        a_ref = refs[0]
        b_refs = refs[1:1 + n_b]
        ex_refs = refs[1 + n_b:1 + n_b + n_ex]
        o_refs = refs[len(refs) - n_out - row_outs:len(refs) - row_outs]
        r_refs = refs[len(refs) - row_outs:]
        acc = None
        if n_split:
            av = a_ref[...].astype(BF16)
            acc = jnp.concatenate([lax.dot_general(av, b_ref[...].astype(BF16), dims, preferred_element_type=F32)
                                   for b_ref in b_refs], axis=1)
        for s in range(0 if n_split else parts):
            av = a_ref[...] if parts == 1 else a_ref[:, s * kp:(s + 1) * kp]
            d = lax.dot_general(av.astype(BF16), b_refs[s][...].astype(BF16), dims, preferred_element_type=F32)
            acc = d if acc is None else acc + d
        res = epi(acc, *[e[...] for e in ex_refs]) if epi is not None else (acc,)
        for o_ref, v in zip(o_refs, res[:n_out]):
            o_ref[...] = v.astype(o_ref.dtype)
        for r_ref, v in zip(r_refs, res[n_out:]):
            @pl.when(pl.program_id(0) == 0)
            def _():
                r_ref[...] = jnp.zeros_like(r_ref)

            r_ref[...] += v

    mn_spec = pl.BlockSpec((tm, tn), lambda i, j: (i, j))
    row_spec = pl.BlockSpec((1, tn), lambda i, j: (0, j))
    in_specs = ([a_spec] + b_specs + [mn_spec] * len(extras) + [pl.BlockSpec((tm, r.shape[1]), lambda i, j: (i, 0)) for r in per_row]
                + [row_spec] * len(vecs))
    args = [a] + b_args + list(extras) + list(per_row) + [v.reshape(1, N) for v in vecs]
    aliases = {}
    if out_loc is None:
        out_specs = [mn_spec] * n_out + [row_spec] * row_outs
        out_shape = [jax.ShapeDtypeStruct((M, N), d) for d in outs] + [jax.ShapeDtypeStruct((1, N), F32)] * row_outs
    else:
        assert n_out == 1 and mode == "tn"
        out_specs = [out_loc.tile_spec(tm, tn, lambda i, j: (i, j))]
        out_shape = [jax.ShapeDtypeStruct(out_loc.slab.shape, out_loc.slab.dtype)]
        if out_loc.slab.arr is not None:
            in_specs.append(_ANY)
            args.append(out_loc.slab.arr)
            aliases = {len(args) - 1: 0}

    res = pl.pallas_call(
        body, name=name, grid=(M // tm, N // tn), in_specs=in_specs, out_specs=out_specs, out_shape=out_shape,
        input_output_aliases=aliases, compiler_params=_cparams(("arbitrary" if row_outs else "parallel", "parallel")),
    )(*args)
    if out_loc is not None:
        out_loc.slab.arr = res[0]
        return None
    return res[0] if len(res) == 1 else tuple(res)


def _epi_add(acc, r):
    return (acc + r,)


def _epi_add_norm(acc, r, g):
    x = acc + r
    return x, _rms(x, g)


def _epi_norm_bwd(acc, x, dx_in, g):
    r = lax.rsqrt(jnp.mean(x * x, axis=-1, keepdims=True) + EPS)
    xh = x * r
    dxh = acc * g
    dx = dx_in + r * (dxh - xh * jnp.mean(dxh * xh, axis=-1, keepdims=True))
    return dx, dx, jnp.sum(acc * xh, axis=0, keepdims=True)


_NORM_BWD_OUTS = (F32, BF16)


def residual_norm(a, w, xs, g, name, tm=1024):
    if g is None:
        return mm(a, w, "nn", name, epi=_epi_add, extras=(xs,), tm=tm), None
    return mm(a, w, "nn", name, outs=(F32, BF16), epi=_epi_add_norm, extras=(xs,), vecs=(g,), tm=tm)


def _epi_relu2(acc):
    r = jnp.maximum(acc, 0.0)
    return acc, r * r


def _epi_relu2_bwd(acc, h1):
    return (acc * (2.0 * jnp.maximum(h1.astype(F32), 0.0)),)


def _rms(x, g):
    return x * lax.rsqrt(jnp.mean(x * x, axis=-1, keepdims=True) + EPS) * g


def _row_spec(ts, cols):
    return pl.BlockSpec((ts, cols), lambda i: (i, 0))


def _par_spec(cols):
    return pl.BlockSpec((1, cols), lambda i: (0, 0))


def rmsnorm_fwd(x, g, name, ts=256):
    T, D = x.shape
    ts = min(ts, T)

    def body(x_ref, g_ref, o_ref):
        o_ref[...] = _rms(x_ref[...], g_ref[...]).astype(o_ref.dtype)

    return pl.pallas_call(
        body, name=name, grid=(T // ts,),
        in_specs=[_row_spec(ts, D), _par_spec(D)], out_specs=_row_spec(ts, D),
        out_shape=jax.ShapeDtypeStruct((T, D), BF16), compiler_params=_cparams(("parallel",)),
    )(x, g.reshape(1, D))


def rmsnorm_bwd(x, g, dy, dx_in, name, ts=256):
    T, D = x.shape
    ts = min(ts, T)

    def body(x_ref, g_ref, dy_ref, dxi_ref, dx_ref, dg_ref):
        xv = x_ref[...]
        r = lax.rsqrt(jnp.mean(xv * xv, axis=-1, keepdims=True) + EPS)
        xh = xv * r
        dyv = dy_ref[...].astype(F32)
        dxh = dyv * g_ref[...]
        dx_ref[...] = dxi_ref[...] + r * (dxh - xh * jnp.mean(dxh * xh, axis=-1, keepdims=True))
        dg = jnp.sum(dyv * xh, axis=0, keepdims=True)

        @pl.when(pl.program_id(0) == 0)
        def _():
            dg_ref[...] = jnp.zeros_like(dg_ref)

        dg_ref[...] += dg

    dx, dg = pl.pallas_call(
        body, name=name, grid=(T // ts,),
        in_specs=[_row_spec(ts, D), _par_spec(D), _row_spec(ts, D), _row_spec(ts, D)],
        out_specs=[_row_spec(ts, D), _par_spec(D)],
        out_shape=[jax.ShapeDtypeStruct((T, D), F32), jax.ShapeDtypeStruct((1, D), F32)],
        compiler_params=_cparams(("arbitrary",)),
    )(x, g.reshape(1, D), dy, dx_in)
    return dx, dg.reshape(D)


def rope_tables(pos, name="rope_tables"):
    T = pos.shape[0]
    half = MLA_ROPE // 2
    inv = ROPE_THETA ** (-jnp.arange(0, MLA_ROPE, 2, dtype=F32) / MLA_ROPE)
    inv_row = jnp.concatenate([inv, inv, jnp.zeros((LANES - MLA_ROPE,), F32)]).reshape(1, LANES)

    def body(p_ref, f_ref, c_ref, a_ref, b_ref):
        ang = p_ref[...].astype(F32) * f_ref[...]
        lane = lax.broadcasted_iota(jnp.int32, ang.shape, 1)
        c, s = jnp.cos(ang), jnp.sin(ang)
        c_ref[...] = jnp.where(lane < MLA_ROPE, c, 0.0)
        a_ref[...] = jnp.where(lane < half, -s, 0.0)
        b_ref[...] = jnp.where((lane >= half) & (lane < MLA_ROPE), s, 0.0)

    sh = jax.ShapeDtypeStruct((T, LANES), F32)
    return pl.pallas_call(body, name=name, out_shape=[sh, sh, sh], compiler_params=_cparams())(pos, inv_row)


def _roll_l(x):
    return pltpu.roll(x, LANES - MLA_ROPE // 2, 1)


def _roll_r(x):
    return pltpu.roll(x, MLA_ROPE // 2, 1)


def _rope(r, c, sa, sb):
    return r * c + _roll_l(r) * sa + _roll_r(r) * sb


def _rope_t(d, c, sa, sb):
    return d * c + _roll_r(d * sa) + _roll_l(d * sb)


def _epi_rope_q(acc, c, sa, sb):
    hw = 2 * LANES
    parts = []
    for h in range(acc.shape[1] // hw):
        parts += [acc[:, h * hw:h * hw + LANES], _rope(acc[:, h * hw + LANES:(h + 1) * hw], c, sa, sb)]
    return (jnp.concatenate(parts, axis=1),)


def mla_mid_fwd(z, qn, kvn, tabs, name, ts=256):
    T = z.shape[0]
    ts = min(ts, T)
    a0, a1 = MLA_QR, MLA_QR + MLA_KVR

    def body(z_ref, qn_ref, kvn_ref, c_ref, sa_ref, sb_ref, cq_ref, ckv_ref, kr_ref):
        cq_ref[...] = _rms(z_ref[:, 0:a0], qn_ref[...]).astype(BF16)
        ckv_ref[...] = _rms(z_ref[:, a0:a1], kvn_ref[...]).astype(BF16)
        kr_ref[...] = _rope(z_ref[:, a1:MLA_ZPAD], c_ref[...], sa_ref[...], sb_ref[...]).astype(BF16)

    return pl.pallas_call(
        body, name=name, grid=(T // ts,),
        in_specs=[_row_spec(ts, MLA_ZPAD), _par_spec(MLA_QR), _par_spec(MLA_KVR)] + [_row_spec(ts, LANES)] * 3,
        out_specs=[_row_spec(ts, MLA_QR), _row_spec(ts, MLA_KVR), _row_spec(ts, LANES)],
        out_shape=[jax.ShapeDtypeStruct((T, MLA_QR), BF16), jax.ShapeDtypeStruct((T, MLA_KVR), BF16),
                   jax.ShapeDtypeStruct((T, LANES), BF16)],
        compiler_params=_cparams(("parallel",)),
    )(z, qn.reshape(1, -1), kvn.reshape(1, -1), *tabs)


def mla_mid_bwd(z, qn, kvn, tabs, dcq, dckv, dkr, name, ts=256):
    T = z.shape[0]
    ts = min(ts, T)
    a0, a1 = MLA_QR, MLA_QR + MLA_KVR

    def body(z_ref, qn_ref, kvn_ref, c_ref, sa_ref, sb_ref, dcq_ref, dckv_ref, dkr_ref, dz_ref, dqn_ref, dkvn_ref):
        _, vq = jax.vjp(_rms, z_ref[:, 0:a0], qn_ref[...])
        dzq, dqn = vq(dcq_ref[...].astype(F32))
        _, vk = jax.vjp(_rms, z_ref[:, a0:a1], kvn_ref[...])
        dzk, dkvn = vk(dckv_ref[...].astype(F32))
        dz_ref[:, 0:a0] = dzq.astype(dz_ref.dtype)
        dz_ref[:, a0:a1] = dzk.astype(dz_ref.dtype)
        dz_ref[:, a1:MLA_ZPAD] = _rope_t(dkr_ref[...].astype(F32), c_ref[...], sa_ref[...], sb_ref[...]).astype(dz_ref.dtype)

        @pl.when(pl.program_id(0) == 0)
        def _():
            dqn_ref[...] = jnp.zeros_like(dqn_ref)
            dkvn_ref[...] = jnp.zeros_like(dkvn_ref)

        dqn_ref[...] += dqn
        dkvn_ref[...] += dkvn

    dz, dqn, dkvn = pl.pallas_call(
        body, name=name, grid=(T // ts,),
        in_specs=[_row_spec(ts, MLA_ZPAD), _par_spec(MLA_QR), _par_spec(MLA_KVR)] + [_row_spec(ts, LANES)] * 3
        + [_row_spec(ts, MLA_QR), _row_spec(ts, MLA_KVR), _row_spec(ts, LANES)],
        out_specs=[_row_spec(ts, MLA_ZPAD), _par_spec(MLA_QR), _par_spec(MLA_KVR)],
        out_shape=[jax.ShapeDtypeStruct((T, MLA_ZPAD), BF16), jax.ShapeDtypeStruct((1, MLA_QR), F32),
                   jax.ShapeDtypeStruct((1, MLA_KVR), F32)],
        compiler_params=_cparams(("arbitrary",)),
    )(z, qn.reshape(1, -1), kvn.reshape(1, -1), *tabs, dcq, dckv, dkr)
    return dz, dqn.reshape(-1), dkvn.reshape(-1)


def loss_head(x, g, target, name="loss_head", ts=256):
    T, D = x.shape
    ts = min(ts, T)

    def body(x_ref, g_ref, t_ref, se_ref, dx_ref, dg_ref):
        xv = x_ref[...]
        r = lax.rsqrt(jnp.mean(xv * xv, axis=-1, keepdims=True) + EPS)
        xh = xv * r
        err = xh * g_ref[...] - t_ref[...]
        dy = err * (1.0 / D)
        dxh = dy * g_ref[...]
        dx_ref[...] = r * (dxh - xh * jnp.mean(dxh * xh, axis=-1, keepdims=True))

        @pl.when(pl.program_id(0) == 0)
        def _():
            se_ref[...] = jnp.zeros_like(se_ref)
            dg_ref[...] = jnp.zeros_like(dg_ref)

        se_ref[...] += jnp.sum(err * err, axis=0, keepdims=True)
        dg_ref[...] += jnp.sum(dy * xh, axis=0, keepdims=True)

    se, dx, dg = pl.pallas_call(
        body, name=name, grid=(T // ts,),
        in_specs=[_row_spec(ts, D), _par_spec(D), _row_spec(ts, D)],
        out_specs=[_par_spec(D), _row_spec(ts, D), _par_spec(D)],
        out_shape=[jax.ShapeDtypeStruct((1, D), F32), jax.ShapeDtypeStruct((T, D), F32), jax.ShapeDtypeStruct((1, D), F32)],
        compiler_params=_cparams(("arbitrary",)),
    )(x, g.reshape(1, D), target)
    return se, dx, dg.reshape(D)


def _dot_nt(a, b):
    return lax.dot_general(a, b, (((1,), (1,)), ((), ())), preferred_element_type=F32)


def _dot_nn(a, b):
    return lax.dot_general(a, b, (((1,), (0,)), ((), ())), preferred_element_type=F32)


class _Attn:
    def __init__(self, H, dq, dk1, dv, causal, scale, hp, hp_kv, blk=256):
        self.H, self.dq, self.dk1, self.dv, self.causal, self.scale, self.blk = H, dq, dk1, dv, causal, scale, blk
        self.hp, self.hp_kv = hp, hp_kv


def _cols(ref, rows, hh, width):
    return ref[rows, hh * width:(hh + 1) * width]


def _keys(cfg, k1_ref, k2_ref, rows, hh):
    ks = _cols(k1_ref, rows, hh, cfg.dk1)
    if k2_ref is not None:
        ks = jnp.concatenate([ks, k2_ref[rows, :]], axis=1)
    return ks


def _blocks(cfg, Tq, Tk):
    tq, tk = min(cfg.blk, Tq), min(cfg.blk, Tk)
    assert tq == tk or not cfg.causal
    return tq, tk


def _attn_specs(cfg, hp, t, Tk, has_k2, by_q):
    g = cfg.H // hp
    if by_q:
        specs = [pl.BlockSpec((t, hp * cfg.dq), lambda h, i: (i, h)),
                 pl.BlockSpec((Tk, hp * cfg.dk1), lambda h, i: (0, h)),
                 pl.BlockSpec((Tk, hp * cfg.dv), lambda h, i: (0, g + h))]
        if has_k2:
            specs.append(pl.BlockSpec((Tk, LANES), lambda h, i: (0, 0)))
    else:
        specs = [None,
                 pl.BlockSpec((t, hp * cfg.dk1), lambda j, h: (j, h)),
                 pl.BlockSpec((t, hp * cfg.dv), lambda j, h: (j, g + h))]
        if has_k2:
            specs.append(pl.BlockSpec((t, LANES), lambda j, h: (j, 0)))
    return specs


def _mask(s, diagonal):
    if not diagonal:
        return s
    return jnp.where(lax.broadcasted_iota(jnp.int32, s.shape, 0) >= lax.broadcasted_iota(jnp.int32, s.shape, 1), s, NEG)


def flash_fwd(cfg, q, k1, v, k2, name):
    Tq, Tk = q.shape[0], k1.shape[0]
    t, tk = _blocks(cfg, Tq, Tk)
    nkb = Tk // tk
    has_k2 = k2 is not None
    hp = cfg.hp

    def body(*refs):
        q_ref, k1_ref, v_ref = refs[:3]
        k2_ref = refs[3] if has_k2 else None
        o_ref, lse_ref = refs[-2], refs[-1]
        i = pl.program_id(1)
        qs = [_cols(q_ref, slice(None), hh, cfg.dq) for hh in range(hp)]

        def step(j, carry, diagonal=False):
            rows = pl.ds(pl.multiple_of(j * tk, tk), tk)
            out = []
            for hh in range(hp):
                m, l, acc = carry[hh]
                s = _mask(_dot_nt(qs[hh], _keys(cfg, k1_ref, k2_ref, rows, hh)) * cfg.scale, diagonal)
                m2 = jnp.maximum(m, jnp.max(s, axis=-1, keepdims=True))
                p = jnp.exp(s - m2)
                alpha = jnp.exp(m - m2)
                l2 = alpha * l + jnp.sum(p, axis=-1, keepdims=True)
                acc2 = alpha * acc + _dot_nn(p.astype(BF16), _cols(v_ref, rows, hh, cfg.dv))
                out.append((m2, l2, acc2))
            return tuple(out)

        init = tuple((jnp.full((t, 1), NEG, F32), jnp.zeros((t, 1), F32), jnp.zeros((t, cfg.dv), F32)) for _ in range(hp))
        res = lax.fori_loop(0, i if cfg.causal else nkb, step, init)
        if cfg.causal:
            res = step(i, res, True)
        for hh in range(hp):
            m, l, acc = res[hh]
            o_ref[:, hh * cfg.dv:(hh + 1) * cfg.dv] = (acc / l).astype(o_ref.dtype)
            lse_ref[hh] = m + jnp.log(l)

    args = [q, k1, v] + ([k2] if has_k2 else [])
    return pl.pallas_call(
        body, name=name, grid=(cfg.H // hp, Tq // t), in_specs=_attn_specs(cfg, hp, t, Tk, has_k2, True),
        out_specs=[pl.BlockSpec((t, hp * cfg.dv), lambda h, i: (i, h)), pl.BlockSpec((hp, t, 1), lambda h, i: (h, i, 0))],
        out_shape=[jax.ShapeDtypeStruct((Tq, cfg.H * cfg.dv), BF16), jax.ShapeDtypeStruct((cfg.H, Tq, 1), F32)],
        compiler_params=_cparams(("parallel", "parallel")),
    )(*args)


def flash_dq(cfg, q, k1, v, k2, o, do, lse, out_dtype, name, rope_tabs=None):
    Tq, Tk = q.shape[0], k1.shape[0]
    t, tk = _blocks(cfg, Tq, Tk)
    nkb = Tk // tk
    has_k2 = k2 is not None
    hp = cfg.hp
    n_tab = 0 if rope_tabs is None else len(rope_tabs)

    def body(*refs):
        q_ref, k1_ref, v_ref = refs[:3]
        k2_ref = refs[3] if has_k2 else None
        tab_refs = refs[len(refs) - 5 - n_tab:len(refs) - 5]
        o_ref, do_ref, lse_ref, dq_ref, dl_ref = refs[-5:]
        i = pl.program_id(1)
        qs = [_cols(q_ref, slice(None), hh, cfg.dq) for hh in range(hp)]
        dos = [_cols(do_ref, slice(None), hh, cfg.dv) for hh in range(hp)]
        lses = [lse_ref[hh] for hh in range(hp)]
        deltas = []
        for hh in range(hp):
            d = jnp.sum(dos[hh].astype(F32) * _cols(o_ref, slice(None), hh, cfg.dv).astype(F32), axis=-1, keepdims=True)
            dl_ref[hh] = d
            deltas.append(d)

        def step(j, dqs, diagonal=False):
            rows = pl.ds(pl.multiple_of(j * tk, tk), tk)
            out = []
            for hh in range(hp):
                ks = _keys(cfg, k1_ref, k2_ref, rows, hh)
                s = _mask(_dot_nt(qs[hh], ks) * cfg.scale, diagonal)
                p = jnp.exp(s - lses[hh])
                dp = _dot_nt(dos[hh], _cols(v_ref, rows, hh, cfg.dv))
                ds = p * (dp - deltas[hh]) * cfg.scale
                out.append(dqs[hh] + _dot_nn(ds.astype(BF16), ks))
            return tuple(out)

        dqs = lax.fori_loop(0, i if cfg.causal else nkb, step, tuple(jnp.zeros((t, cfg.dq), F32) for _ in range(hp)))
        if cfg.causal:
            dqs = step(i, dqs, True)
        tabs = [r[...] for r in tab_refs]
        for hh in range(hp):
            dq = dqs[hh]
            if tabs:
                dq = jnp.concatenate([dq[:, :LANES], _rope_t(dq[:, LANES:], *tabs)], axis=1)
            dq_ref[:, hh * cfg.dq:(hh + 1) * cfg.dq] = dq.astype(dq_ref.dtype)

    ov = pl.BlockSpec((t, hp * cfg.dv), lambda h, i: (i, h))
    row1 = pl.BlockSpec((hp, t, 1), lambda h, i: (h, i, 0))
    tab_specs = [pl.BlockSpec((t, LANES), lambda h, i: (i, 0))] * n_tab
    args = [q, k1, v] + ([k2] if has_k2 else []) + list(rope_tabs or ()) + [o, do, lse]
    return pl.pallas_call(
        body, name=name, grid=(cfg.H // hp, Tq // t),
        in_specs=_attn_specs(cfg, hp, t, Tk, has_k2, True) + tab_specs + [ov, ov, row1],
        out_specs=[pl.BlockSpec((t, hp * cfg.dq), lambda h, i: (i, h)), row1],
        out_shape=[jax.ShapeDtypeStruct((Tq, cfg.H * cfg.dq), out_dtype), jax.ShapeDtypeStruct((cfg.H, Tq, 1), F32)],
        compiler_params=_cparams(("parallel", "parallel")),
    )(*args)


def flash_dkv(cfg, q, k1, v, k2, do, lse, delta, out_dtype, name):
    Tq, Tk = q.shape[0], k1.shape[0]
    tq, t = _blocks(cfg, Tq, Tk)
    nqb = Tq // tq
    has_k2 = k2 is not None
    hp = cfg.hp_kv
    assert hp == cfg.H
    v0 = cfg.H * cfg.dk1

    def body(*refs):
        q_ref, k1_ref, v_ref = refs[:3]
        k2_ref = refs[3] if has_k2 else None
        n_in = 4 if has_k2 else 3
        do_ref, lse_ref, dl_ref = refs[n_in:n_in + 3]
        dkv_ref = refs[n_in + 3]
        j, h = pl.program_id(0), pl.program_id(1)
        kss = [_keys(cfg, k1_ref, k2_ref, slice(None), hh) for hh in range(hp)]
        vss = [_cols(v_ref, slice(None), hh, cfg.dv) for hh in range(hp)]

        def step(i, carry, diagonal=False):
            rows = pl.ds(pl.multiple_of(i * tq, tq), tq)
            out = []
            for hh in range(hp):
                dk, dv = carry[hh]
                qi, doi = _cols(q_ref, rows, hh, cfg.dq), _cols(do_ref, rows, hh, cfg.dv)
                s = _dot_nt(kss[hh], qi) * cfg.scale
                if diagonal:
                    s = jnp.where(lax.broadcasted_iota(jnp.int32, s.shape, 0) <= lax.broadcasted_iota(jnp.int32, s.shape, 1), s, NEG)
                p = jnp.exp(s - lse_ref[hh, :, rows])
                dv = dv + _dot_nn(p.astype(BF16), doi)
                ds = p * (_dot_nt(vss[hh], doi) - dl_ref[hh, :, rows]) * cfg.scale
                dk = dk + _dot_nn(ds.astype(BF16), qi)
                out.append((dk, dv))
            return tuple(out)

        init = tuple((jnp.zeros((t, cfg.dq), F32), jnp.zeros((t, cfg.dv), F32)) for _ in range(hp))
        if cfg.causal:
            res = lax.fori_loop(j + 1, nqb, step, step(j, init, True))
        else:
            res = lax.fori_loop(0, nqb, step, init)
        for hh in range(hp):
            dk, dv = res[hh]
            dkv_ref[:, hh * cfg.dk1:(hh + 1) * cfg.dk1] = dk[:, 0:cfg.dk1].astype(dkv_ref.dtype)
            dkv_ref[:, v0 + hh * cfg.dv:v0 + (hh + 1) * cfg.dv] = dv.astype(dkv_ref.dtype)
        if has_k2:
            dk2_ref = refs[n_in + 4]

            @pl.when(h == 0)
            def _():
                dk2_ref[...] = jnp.zeros_like(dk2_ref)

            for hh in range(hp):
                dk2_ref[...] += res[hh][0][:, cfg.dk1:]

    specs = _attn_specs(cfg, hp, t, Tk, has_k2, False)
    specs[0] = pl.BlockSpec((Tq, hp * cfg.dq), lambda j, h: (0, h))
    rows_all = pl.BlockSpec((hp, 1, Tq), lambda j, h: (h, 0, 0))
    specs += [pl.BlockSpec((Tq, hp * cfg.dv), lambda j, h: (0, h)), rows_all, rows_all]
    args = [q, k1, v] + ([k2] if has_k2 else []) + [do, lse.reshape(cfg.H, 1, Tq), delta.reshape(cfg.H, 1, Tq)]
    out_specs = [pl.BlockSpec((t, v0 + cfg.H * cfg.dv), lambda j, h: (j, 0))]
    out_shape = [jax.ShapeDtypeStruct((Tk, v0 + cfg.H * cfg.dv), out_dtype)]
    if has_k2:
        out_specs.append(pl.BlockSpec((t, LANES), lambda j, h: (j, 0)))
        out_shape.append(jax.ShapeDtypeStruct((Tk, LANES), F32))
    return pl.pallas_call(
        body, name=name, grid=(Tk // t, cfg.H // hp), in_specs=specs, out_specs=out_specs, out_shape=out_shape,
        compiler_params=_cparams(("parallel", "arbitrary")),
    )(*args)


def _shift_down(x, s):
    if s == 0:
        return x
    t = lax.broadcasted_iota(jnp.int32, x.shape, 0)
    return jnp.where(t >= s, pltpu.roll(x, s, 0), 0.0)


def _shift_up(x, s):
    if s == 0:
        return x
    n = x.shape[0]
    t = lax.broadcasted_iota(jnp.int32, x.shape, 0)
    return jnp.where(t < n - s, pltpu.roll(x, n - s, 0), 0.0)


def _conv(x, w_ref, kw):
    y = x * w_ref[kw - 1:kw, :]
    for j in range(kw - 1):
        y = y + _shift_down(x, kw - 1 - j) * w_ref[j:j + 1, :]
    return y


def _conv_t(d, w_ref, kw):
    y = d * w_ref[kw - 1:kw, :]
    for j in range(kw - 1):
        y = y + _shift_up(d, kw - 1 - j) * w_ref[j:j + 1, :]
    return y


def _conv_dw(d, x, kw):
    rows = lax.broadcasted_iota(jnp.int32, (kw, d.shape[1]), 0)
    dw = jnp.zeros((kw, d.shape[1]), F32)
    for j in range(kw):
        r = jnp.sum(d * _shift_down(x, kw - 1 - j), axis=0, keepdims=True)
        dw = jnp.where(rows == j, r, dw)
    return dw


def _silu(x):
    return x * jax.nn.sigmoid(x)


def _silu_grad(x):
    s = jax.nn.sigmoid(x)
    return s * (1.0 + x * (1.0 - s))


def gdn_conv_fwd(z, w, name, tc=256):
    T, C = z.shape[0], w.shape[1]
    kw = w.shape[0]

    def body(x_ref, w_ref, o_ref):
        o_ref[...] = _silu(_conv(x_ref[...], w_ref, kw))

    return pl.pallas_call(
        body, name=name, grid=(C // tc,),
        in_specs=[pl.BlockSpec((T, tc), lambda j: (0, j)), pl.BlockSpec((kw, tc), lambda j: (0, j))],
        out_specs=pl.BlockSpec((T, tc), lambda j: (0, j)),
        out_shape=jax.ShapeDtypeStruct((T, C), F32), compiler_params=_cparams(("parallel",)),
    )(z, w)


def gdn_conv_bwd(z, w, dy, name, tc=256):
    T, C = z.shape[0], w.shape[1]
    kw = w.shape[0]

    def body(x_ref, w_ref, dy_ref, dx_ref, dw_ref):
        xv = x_ref[...]
        dc = dy_ref[...] * _silu_grad(_conv(xv, w_ref, kw))
        dx_ref[...] = _conv_t(dc, w_ref, kw).astype(dx_ref.dtype)
        dw_ref[...] = _conv_dw(dc, xv, kw)

    col = lambda j: (0, j)
    return pl.pallas_call(
        body, name=name, grid=(C // tc,),
        in_specs=[pl.BlockSpec((T, tc), col), pl.BlockSpec((kw, tc), col), pl.BlockSpec((T, tc), col)],
        out_specs=[pl.BlockSpec((T, tc), col), pl.BlockSpec((kw, tc), col)],
        out_shape=[jax.ShapeDtypeStruct((T, C), BF16), jax.ShapeDtypeStruct((kw, C), F32)],
        compiler_params=_cparams(("parallel",)),
    )(z, w, dy)


def sc_fwd(z, w, name, tc=256):
    T, C = z.shape[0], w.shape[1]
    kw, nb = w.shape[0], C // tc

    def body(b_ref, c_ref, u_ref, w_ref, o_ref):
        o_ref[...] = (b_ref[...] * _conv(c_ref[...] * u_ref[...], w_ref, kw)).astype(o_ref.dtype)

    return pl.pallas_call(
        body, name=name, grid=(nb,),
        in_specs=[pl.BlockSpec((T, tc), lambda j: (0, j)), pl.BlockSpec((T, tc), lambda j: (0, nb + j)),
                  pl.BlockSpec((T, tc), lambda j: (0, 2 * nb + j)), pl.BlockSpec((kw, tc), lambda j: (0, j))],
        out_specs=pl.BlockSpec((T, tc), lambda j: (0, j)),
        out_shape=jax.ShapeDtypeStruct((T, C), BF16), compiler_params=_cparams(("parallel",)),
    )(z, z, z, w)


def sc_bwd(z, w, dy, name, tc=256):
    T, C = z.shape[0], w.shape[1]
    kw, nb = w.shape[0], C // tc

    def body(b_ref, c_ref, u_ref, w_ref, dy_ref, db_ref, dc_ref, du_ref, dw_ref):
        cv, uv, dyv = c_ref[...], u_ref[...], dy_ref[...]
        cu = cv * uv
        db_ref[...] = (dyv * _conv(cu, w_ref, kw)).astype(db_ref.dtype)
        dcv = dyv * b_ref[...]
        dcu = _conv_t(dcv, w_ref, kw)
        dc_ref[...] = (dcu * uv).astype(dc_ref.dtype)
        du_ref[...] = (dcu * cv).astype(du_ref.dtype)
        dw_ref[...] = _conv_dw(dcv, cu, kw)

    col = lambda j: (0, j)
    act = jax.ShapeDtypeStruct((T, C), BF16)
    return pl.pallas_call(
        body, name=name, grid=(nb,),
        in_specs=[pl.BlockSpec((T, tc), col), pl.BlockSpec((T, tc), lambda j: (0, nb + j)),
                  pl.BlockSpec((T, tc), lambda j: (0, 2 * nb + j)), pl.BlockSpec((kw, tc), col), pl.BlockSpec((T, tc), col)],
        out_specs=[pl.BlockSpec((T, tc), col)] * 3 + [pl.BlockSpec((kw, tc), col)],
        out_shape=[act, act, act, jax.ShapeDtypeStruct((kw, C), F32)],
        compiler_params=_cparams(("parallel",)),
    )(z, z, z, w, dy)


def _hdot(a, b, dims):
    a_hi, b_hi = a.astype(BF16), b.astype(BF16)
    a_lo, b_lo = (a - a_hi.astype(F32)).astype(BF16), (b - b_hi.astype(F32)).astype(BF16)
    dot = lambda x, y: lax.dot_general(x, y, (dims, ((), ())), preferred_element_type=F32)
    return dot(a_hi, b_hi) + (dot(a_hi, b_lo) + dot(a_lo, b_hi))


def _bdot(a, b, dims):
    return lax.dot_general(a.astype(BF16), b.astype(BF16), (dims, ((), ())), preferred_element_type=F32)


_NN, _NT, _TN = ((1,), (0,)), ((1,), (1,)), ((0,), (0,))


def _per_head_dots(dot2d):
    def stacked(a, b, dims):
        return jnp.stack([dot2d(a[h], b[h], dims) for h in range(a.shape[0])])

    @jax.custom_vjp
    def nn(a, b):
        return stacked(a, b, _NN)

    @jax.custom_vjp
    def nt(a, b):
        return stacked(a, b, _NT)

    @jax.custom_vjp
    def tn(a, b):
        return stacked(a, b, _TN)

    nn.defvjp(lambda a, b: (nn(a, b), (a, b)), lambda r, d: (stacked(d, r[1], _NT), stacked(r[0], d, _TN)))
    nt.defvjp(lambda a, b: (nt(a, b), (a, b)), lambda r, d: (stacked(d, r[1], _NN), stacked(d, r[0], _TN)))
    tn.defvjp(lambda a, b: (tn(a, b), (a, b)), lambda r, d: (stacked(r[1], d, _NT), stacked(r[0], d, _NN)))
    return nn, nt, tn


_hnn, _hnt, _htn = _per_head_dots(_hdot)
_bnn, _bnt, _btn = _per_head_dots(_bdot)


@jax.custom_vjp
def _unit_lower_inverse(m):
    c = m.shape[-1]
    eye = (lax.broadcasted_iota(jnp.int32, (c, c), 0) == lax.broadcasted_iota(jnp.int32, (c, c), 1)).astype(F32)
    t = eye - m
    p = _hnn(m, m)
    n = 2
    while n < c:
        t = t + _hnn(t, p)
        n *= 2
        if n < c:
            p = _hnn(p, p)
    return t


def _uli_fwd(m):
    t = _unit_lower_inverse(m)
    return t, t


def _uli_bwd(t, dt):
    return (-_htn(t, _hnt(dt, t)),)


_unit_lower_inverse.defvjp(_uli_fwd, _uli_bwd)


@jax.custom_vjp
def _known_inverse(m, t):
    return t


_known_inverse.defvjp(lambda m, t: (t, t), lambda t, dt: (_uli_bwd(t, dt)[0], jnp.zeros_like(t)))


def _gdn_chunk(q, k, v, gate, bl, al, a_log, dt_bias, o_norm, st, t_known=None):
    nh, c = q.shape[0], q.shape[1]
    ii = lax.broadcasted_iota(jnp.int32, (c, c), 0)
    jj = lax.broadcasted_iota(jnp.int32, (c, c), 1)
    tri, strict = ii >= jj, ii > jj
    q = q * lax.rsqrt(jnp.sum(q * q, -1, keepdims=True) + EPS) * (GDN_D ** -0.5)
    k = k * lax.rsqrt(jnp.sum(k * k, -1, keepdims=True) + EPS)
    beta = jax.nn.sigmoid(bl)
    g = -jnp.exp(a_log) * jax.nn.softplus(al + dt_bias)
    gc = _hnn(jnp.broadcast_to(tri.astype(F32), (nh, c, c)), g)
    gcol = _hnn(gc, jnp.full((nh, LANES, c), 1.0 / LANES, F32))
    grow = _hnt(jnp.full((nh, c, LANES), 1.0 / LANES, F32), gc)
    decay = jnp.where(tri, jnp.exp(jnp.where(tri, gcol - grow, 0.0)), 0.0)
    kb = k * beta
    m = jnp.where(strict, _bnt(kb, k) * decay, 0.0)
    t_inv = _unit_lower_inverse(m) if t_known is None else _known_inverse(m, t_known)
    eg = jnp.exp(gc)
    u = _bnn(t_inv, v * beta)
    w = _bnn(t_inv, kb * eg)
    attn = _bnt(q, k) * decay
    v_new = u - _bnn(w, st)
    o = _bnn(q * eg, st) + _bnn(attn, v_new)
    g_last = jnp.sum(g, axis=1, keepdims=True)
    st_new = st * jnp.exp(g_last) + _btn(k * jnp.exp(g_last - gc), v_new)
    o = o * lax.rsqrt(jnp.mean(o * o, -1, keepdims=True) + EPS) * o_norm
    return o * _silu(gate), st_new, t_inv


GDN_HP = 8
_GW = GDN_HP * GDN_D
_GB = GDN_H // GDN_HP


def _gdn_specs(n_chunks, rev):
    def tok(col):
        if rev:
            return pl.BlockSpec((GDN_C, _GW), lambda h, n: (n_chunks - 1 - n, col + h))
        return pl.BlockSpec((GDN_C, _GW), lambda h, n: (n, col + h))
    par = pl.BlockSpec((1, _GW), lambda h, n: (0, h))
    shared = pl.BlockSpec((1, GDN_D), lambda h, n: (0, 0))
    if rev:
        st = pl.BlockSpec((GDN_HP, None, GDN_D, GDN_D), lambda h, n: (h, n_chunks - 1 - n, 0, 0))
    else:
        st = pl.BlockSpec((GDN_HP, None, GDN_D, GDN_D), lambda h, n: (h, n, 0, 0))
    return tok, par, shared, st


def _heads(ref):
    return jnp.stack([ref[:, h * GDN_D:(h + 1) * GDN_D] for h in range(ref.shape[1] // GDN_D)])


def gdn_chunk_fwd(qkv, z, a_log_x, dt_bias_x, o_norm, name):
    T = qkv.shape[0]
    n_chunks = T // GDN_C
    H = GDN_H
    tok, par, shared, st_spec = _gdn_specs(n_chunks, False)

    def body(q_ref, k_ref, v_ref, g_ref, bl_ref, al_ref, a_ref, dt_ref, on_ref, o_ref, st_ref, ti_ref, state):
        @pl.when(pl.program_id(1) == 0)
        def _():
            state[...] = jnp.zeros_like(state)

        st = state[...]
        st_ref[...] = st
        o, st_new, t_inv = _gdn_chunk(_heads(q_ref), _heads(k_ref), _heads(v_ref), _heads(g_ref), _heads(bl_ref), _heads(al_ref),
                                      _heads(a_ref), _heads(dt_ref), on_ref[...], st)
        for hh in range(GDN_HP):
            o_ref[:, hh * GDN_D:(hh + 1) * GDN_D] = o[hh].astype(o_ref.dtype)
        ti_ref[...] = t_inv
        state[...] = st_new

    B = _GB
    return pl.pallas_call(
        body, name=name, grid=(B, n_chunks),
        in_specs=[tok(0), tok(B), tok(2 * B), tok(3 * B), tok(4 * B), tok(5 * B), par, par, shared],
        out_specs=[tok(0), st_spec, pl.BlockSpec((GDN_HP, None, GDN_C, GDN_C), lambda h, n: (h, n, 0, 0))],
        out_shape=[jax.ShapeDtypeStruct((T, H * GDN_D), BF16), jax.ShapeDtypeStruct((H, n_chunks, GDN_D, GDN_D), F32),
                   jax.ShapeDtypeStruct((H, n_chunks, GDN_C, GDN_C), F32)],
        scratch_shapes=[pltpu.VMEM((GDN_HP, GDN_D, GDN_D), F32)],
        compiler_params=_cparams(("parallel", "arbitrary")),
    )(qkv, qkv, qkv, z, z, z, a_log_x, dt_bias_x, o_norm)


def gdn_chunk_bwd(qkv, z, a_log_x, dt_bias_x, o_norm, states, t_invs, do, name):
    T = qkv.shape[0]
    n_chunks = T // GDN_C
    H = GDN_H
    tok, par, shared, st_spec = _gdn_specs(n_chunks, True)

    def body(q_ref, k_ref, v_ref, g_ref, bl_ref, al_ref, a_ref, dt_ref, on_ref, st_ref, ti_ref, do_ref,
             dqkv_ref, dg_ref, dba_ref, da_ref, ddt_ref, don_ref, dstate):
        h, n = pl.program_id(0), pl.program_id(1)

        @pl.when(n == 0)
        def _():
            dstate[...] = jnp.zeros_like(dstate)
            da_ref[...] = jnp.zeros_like(da_ref)
            ddt_ref[...] = jnp.zeros_like(ddt_ref)

        @pl.when((n == 0) & (h == 0))
        def _():
            don_ref[...] = jnp.zeros_like(don_ref)

        t_known = ti_ref[...]
        _, vjp = jax.vjp(lambda *ins: _gdn_chunk(*ins, t_known=t_known)[:2],
                         _heads(q_ref), _heads(k_ref), _heads(v_ref), _heads(g_ref), _heads(bl_ref), _heads(al_ref),
                         _heads(a_ref), _heads(dt_ref), on_ref[...], st_ref[...])
        dq, dk, dv, dg, dbl, dal, da, ddt, don, dst = vjp((_heads(do_ref).astype(F32), dstate[...]))
        lane = lax.broadcasted_iota(jnp.int32, (GDN_C, LANES), 1)
        dba = jnp.zeros((GDN_C, LANES), F32)
        for hh in range(GDN_HP):
            cols = slice(hh * GDN_D, (hh + 1) * GDN_D)
            for part, d in enumerate((dq, dk, dv)):
                dqkv_ref[:, part * H * GDN_D + hh * GDN_D:part * H * GDN_D + (hh + 1) * GDN_D] = d[hh]
            dg_ref[:, cols] = dg[hh].astype(dg_ref.dtype)
            dba = jnp.where(lane == hh, jnp.sum(dbl[hh], axis=-1, keepdims=True), dba)
            dba = jnp.where(lane == H + hh, jnp.sum(dal[hh], axis=-1, keepdims=True), dba)
            da_ref[:, cols] += da[hh]
            ddt_ref[:, cols] += ddt[hh]
        dba_ref[...] = dba.astype(dba_ref.dtype)
        don_ref[...] += don
        dstate[...] = dst

    tok0 = tok(0)
    B = _GB
    assert B == 1
    bf_tok = jax.ShapeDtypeStruct((T, H * GDN_D), BF16)
    par_sh = jax.ShapeDtypeStruct((1, H * GDN_D), F32)
    return pl.pallas_call(
        body, name=name, grid=(B, n_chunks),
        in_specs=[tok(0), tok(B), tok(2 * B), tok(3 * B), tok(4 * B), tok(5 * B), par, par, shared, st_spec,
                  pl.BlockSpec((GDN_HP, None, GDN_C, GDN_C), lambda h, n: (h, n_chunks - 1 - n, 0, 0)), tok0],
        out_specs=[pl.BlockSpec((GDN_C, 3 * H * GDN_D), lambda h, n: (n_chunks - 1 - n, 0)), tok0,
                   pl.BlockSpec((GDN_C, LANES), lambda h, n: (n_chunks - 1 - n, 0)), par, par, shared],
        out_shape=[jax.ShapeDtypeStruct((T, 3 * H * GDN_D), F32), bf_tok, jax.ShapeDtypeStruct((T, LANES), BF16), par_sh, par_sh,
                   jax.ShapeDtypeStruct((1, GDN_D), F32)],
        scratch_shapes=[pltpu.VMEM((GDN_HP, GDN_D, GDN_D), F32)],
        compiler_params=_cparams(("arbitrary", "arbitrary")),
    )(qkv, qkv, qkv, z, z, z, a_log_x, dt_bias_x, o_norm, states, t_invs, do)


def _prefetch_call(body, name, grid, in_specs, out_specs, out_shape, aliases=None):
    return pl.pallas_call(
        body, name=name,
        grid_spec=pltpu.PrefetchScalarGridSpec(num_scalar_prefetch=1, grid=grid, in_specs=in_specs, out_specs=out_specs),
        out_shape=out_shape, input_output_aliases=aliases or {},
        compiler_params=_cparams(("parallel",) * len(grid)))


def cast_into(src, src_row0, rows, slab, row0, me, name):
    width = src.shape[1]
    tr = _pick_rows(rows, 1024, row0, src_row0)
    assert rows % tr == 0 and row0 % tr == 0 and src_row0 % tr == 0

    def body(me_ref, s_ref, *refs):
        refs[-1][...] = s_ref[...].astype(refs[-1].dtype)

    in_specs = [pl.BlockSpec((tr, width), lambda r, me_ref: (src_row0 // tr + r, 0))]
    args = [src]
    aliases = {}
    if slab.arr is not None:
        in_specs.append(_ANY)
        args.append(slab.arr)
        aliases = {2: 0}
    slab.arr = _prefetch_call(
        body, name, (rows // tr,), in_specs,
        pl.BlockSpec((None, tr, width), lambda r, me_ref: (me_ref[0], row0 // tr + r, 0)),
        jax.ShapeDtypeStruct(slab.shape, slab.dtype), aliases)(me, *args)


def pair_add(g, b, c_idx, name):
    n, rh, w = b.shape
    tr = _pick_rows(rh, 1024)
    nb = rh // tr

    def body(c_ref, g_ref, b_ref, o_ref):
        o_ref[...] = (g_ref[...].astype(F32) + b_ref[...].astype(F32)).astype(o_ref.dtype)

    return _prefetch_call(
        body, name, (n, nb),
        [pl.BlockSpec((None, tr, w), lambda k, r, c: (k, c[0] * nb + r, 0)), pl.BlockSpec((None, tr, w), lambda k, r, c: (k, r, 0))],
        pl.BlockSpec((None, tr, w), lambda k, r, c: (k, r, 0)), jax.ShapeDtypeStruct(b.shape, BF16))(c_idx, g, b)


def chip_sum(p, rv, mc, name):
    n, rh, w = p.shape
    tr = _pick_rows(rh, 512)
    nb = rh // tr

    def body(mc_ref, p_ref, rv_ref, o_ref):
        me = mc_ref[0]
        acc = None
        for k in range(n):
            part = jnp.where(me == k, p_ref[...], rv_ref[k]).astype(F32)
            acc = part if acc is None else acc + part
        o_ref[...] = acc.astype(o_ref.dtype)

    return _prefetch_call(
        body, name, (nb,),
        [pl.BlockSpec((None, tr, w), lambda r, mc_ref: (mc_ref[0], r, 0)), pl.BlockSpec((n, tr, w), lambda r, mc_ref: (0, r, 0))],
        pl.BlockSpec((tr, w), lambda r, mc_ref: (mc_ref[1] * nb + r, 0)), jax.ShapeDtypeStruct((2 * rh, w), BF16))(mc, p, rv)


def adamw(red, row0, w, m, v, w_row0, rows, prev, name):
    cols = w.shape[1]
    tr = _pick_rows(rows, 512, row0, w_row0)
    assert rows % tr == 0 and row0 % tr == 0 and w_row0 % tr == 0

    def body(g_ref, w_ref, m_ref, v_ref, *refs):
        go_ref, d_ref, nm_ref, nv_ref = refs[-4:]
        gv = g_ref[...].astype(F32)
        nm = ADAM_B1 * m_ref[...] + (1.0 - ADAM_B1) * gv
        nv = ADAM_B2 * v_ref[...] + (1.0 - ADAM_B2) * (gv * gv)
        m_hat = nm / (1.0 - ADAM_B1 ** ADAM_STEP)
        v_hat = nv / (1.0 - ADAM_B2 ** ADAM_STEP)
        go_ref[...] = gv
        d_ref[...] = -ADAM_LR * (m_hat / (jnp.sqrt(v_hat) + ADAM_EPS) + ADAM_WD * w_ref[...])
        nm_ref[...] = nm
        nv_ref[...] = nv

    spec = pl.BlockSpec((tr, cols), lambda r: (w_row0 // tr + r, 0))
    sh = jax.ShapeDtypeStruct(w.shape, F32)
    in_specs = [pl.BlockSpec((tr, cols), lambda r: (row0 // tr + r, 0)), spec, spec, spec]
    args, aliases = [red, w, m, v], {}
    if prev is not None:
        in_specs += [_ANY] * 4
        args += list(prev)
        aliases = {4 + k: k for k in range(4)}
    return pl.pallas_call(
        body, name=name, grid=(rows // tr,), in_specs=in_specs, out_specs=[spec] * 4, out_shape=[sh] * 4,
        input_output_aliases=aliases, compiler_params=_cparams(("parallel",)),
    )(*args)


def _place():
    x, y, c = lax.axis_index("x"), lax.axis_index("y"), lax.axis_index("c")
    chips = [(1 - x, y), (x, 1 - y), (1 - x, 1 - y)]
    return x, y, c, chips


def _chip_index(cx, cy):
    return 2 * cx + cy


def _remote(src, dst, send_sem, recv_sem, to):
    return pltpu.make_async_remote_copy(src_ref=src, dst_ref=dst, send_sem=send_sem, recv_sem=recv_sem,
                                        device_id=to, device_id_type=MESH)


def _comm_call(body, name, ins, out_shapes, n_sems, aliases):
    return pl.pallas_call(
        body, name=name, in_specs=[_ANY] * len(ins), out_specs=[_ANY] * len(out_shapes), out_shape=out_shapes,
        scratch_shapes=[pltpu.SemaphoreType.DMA((n_sems,)), pltpu.SemaphoreType.DMA((n_sems,))],
        input_output_aliases=aliases,
    )(*ins)


def pair_swap_halves(slabs, name="grad_pair_swap"):
    n = len(slabs)

    def body(*refs):
        in_refs, out_refs, send_sems, recv_sems = refs[:n], refs[n:2 * n], refs[-2], refs[-1]
        x, y, c, _ = _place()
        cps = []
        for a in range(n):
            rh = in_refs[a].shape[1] // 2
            cp = _remote(in_refs[a].at[:, pl.ds((1 - c) * rh, rh), :], out_refs[a], send_sems.at[a], recv_sems.at[a], (x, y, 1 - c))
            cp.start()
            cps.append(cp)
        for cp in cps:
            cp.wait()

    outs = [jax.ShapeDtypeStruct((s.shape[0], s.shape[1] // 2, s.shape[2]), s.dtype) for s in slabs]
    return _comm_call(body, name, slabs, outs, n, {})


def pair_join_halves(reds, name="grad_pair_join"):
    n = len(reds)

    def body(*refs):
        in_refs, out_refs, send_sems, recv_sems = refs[:n], refs[n:2 * n], refs[-2], refs[-1]
        x, y, c, _ = _place()
        cps = []
        for a in range(n):
            rh = in_refs[a].shape[0] // 2
            mine = pl.ds(c * rh, rh)
            cp = _remote(in_refs[a].at[mine], out_refs[a].at[mine], send_sems.at[a], recv_sems.at[a], (x, y, 1 - c))
            cp.start()
            cps.append(cp)
        for a in range(n):
            rh = in_refs[a].shape[0] // 2
            got = out_refs[a].at[pl.ds((1 - c) * rh, rh)]
            _remote(got, got, send_sems.at[a], recv_sems.at[a], (x, y, 1 - c)).wait_recv()
        for cp in cps:
            cp.wait_send()

    return _comm_call(body, name, reds, [jax.ShapeDtypeStruct(r.shape, r.dtype) for r in reds], n, {a: a for a in range(n)})


_HBM = pl.BlockSpec(memory_space=pltpu.HBM)
_SEM = pl.BlockSpec(memory_space=pltpu.SEMAPHORE)
_EFFECT = pltpu.SideEffectType.DATAFLOW_SIDE_EFFECTING


def _in_hbm(a):
    return pltpu.with_memory_space_constraint(a, pltpu.HBM)


def _hbm_like(a):
    return pltpu.HBM(a.shape, a.dtype)


def _start_call(body, name, ins, n_sems, after):
    n = len(ins)
    res = pl.pallas_call(
        body, name=name, in_specs=[_HBM] * n + [_ANY],
        out_specs=[_SEM, _SEM] + [_HBM] * n + [pl.BlockSpec(memory_space=pltpu.VMEM)],
        out_shape=[pltpu.SemaphoreType.DMA((n_sems,)), pltpu.SemaphoreType.DMA((n_sems,))] + [_hbm_like(a) for a in ins]
        + [jax.ShapeDtypeStruct((8, LANES), F32)],
        input_output_aliases={a: 2 + a for a in range(n)},
        compiler_params=pltpu.CompilerParams(has_side_effects=_EFFECT),
    )(*[_in_hbm(a) for a in ins], after)
    return res[0], res[1], list(res[2:2 + n]), res[-1]


def _wait_call(body, name, thru, send_sems, recv_sems, after):
    n = len(thru)
    after = list(after) if isinstance(after, (list, tuple)) else [after]
    return pl.pallas_call(
        body, name=name, in_specs=[_HBM] * n + [_SEM, _SEM] + [_ANY] * len(after), out_specs=[_HBM] * n,
        out_shape=[_hbm_like(a) for a in thru], input_output_aliases={a: a for a in range(n)},
        compiler_params=pltpu.CompilerParams(has_side_effects=_EFFECT),
    )(*thru, send_sems, recv_sems, *after)


def gather_start(slabs, after, name="weight_gather_start"):
    n = len(slabs)

    def body(*refs):
        g_refs, send_sems, recv_sems, token = refs[:n], refs[n + 1], refs[n + 2], refs[-1]
        x, y, c, chips = _place()
        me = _chip_index(x, y)
        for a in range(n):
            rh = g_refs[a].shape[1] // 2
            mine = g_refs[a].at[me, pl.ds(c * rh, rh)]
            for j, chip in enumerate(chips):
                _remote(mine, mine, send_sems.at[3 * a + j], recv_sems.at[3 * a + j], (*chip, c)).start()
        token[...] = jnp.zeros_like(token)

    return _start_call(body, name, slabs, 3 * n, after)


def gather_wait(send_sems, recv_sems, thru, after, name="weight_gather_wait"):
    n = len(thru)

    def body(*refs):
        g_refs, send_sems, recv_sems = refs[:n], refs[n], refs[n + 1]
        x, y, c, chips = _place()
        me = _chip_index(x, y)
        for a in range(n):
            rh = g_refs[a].shape[1] // 2
            rows = pl.ds(c * rh, rh)
            for j, chip in enumerate(chips):
                mine, got = g_refs[a].at[me, rows], g_refs[a].at[_chip_index(*chip), rows]
                _remote(mine, mine, send_sems.at[3 * a + j], recv_sems.at[3 * a + j], (*chip, c)).wait_send()
                _remote(got, got, send_sems.at[3 * a + j], recv_sems.at[3 * a + j], (*chip, c)).wait_recv()

    return _wait_call(body, name, thru, send_sems, recv_sems, after)


def gather_forward(slabs, name="weight_gather_forward"):
    n = len(slabs)

    def body(*refs):
        in_refs, out_refs, send_sems, recv_sems = refs[:n], refs[n:2 * n], refs[-2], refs[-1]
        x, y, c, chips = _place()
        sib = (x, y, 1 - c)
        sends = []
        for a in range(n):
            rh = in_refs[a].shape[1] // 2
            for j, chip in enumerate(chips):
                k = _chip_index(*chip)
                cp = _remote(in_refs[a].at[k, pl.ds(c * rh, rh)], out_refs[a].at[k, pl.ds(c * rh, rh)], send_sems.at[3 * a + j],
                             recv_sems.at[3 * a + j], sib)
                cp.start()
                sends.append(cp)
        for a in range(n):
            rh = in_refs[a].shape[1] // 2
            for j, chip in enumerate(chips):
                got = out_refs[a].at[_chip_index(*chip), pl.ds((1 - c) * rh, rh)]
                _remote(got, got, send_sems.at[3 * a + j], recv_sems.at[3 * a + j], sib).wait_recv()
        for cp in sends:
            cp.wait_send()

    return _comm_call(body, name, slabs, [jax.ShapeDtypeStruct(s.shape, s.dtype) for s in slabs], 3 * n, {a: a for a in range(n)})


def exchange_start(parts, after, name="grad_exchange_start"):
    n = len(parts)

    def body(*refs):
        p_refs, land_refs, send_sems, recv_sems, token = refs[:n], refs[n:2 * n], refs[2 * n + 1], refs[2 * n + 2], refs[-1]
        x, y, c, chips = _place()
        me = _chip_index(x, y)
        for a in range(n):
            for j, chip in enumerate(chips):
                _remote(p_refs[a].at[_chip_index(*chip)], land_refs[a].at[me], send_sems.at[3 * a + j], recv_sems.at[3 * a + j],
                        (*chip, c)).start()
        token[...] = jnp.zeros_like(token)

    return _start_call(body, name, list(parts) + [lax.empty(p.shape, p.dtype) for p in parts], 3 * n, after)


def swap_start(slabs, after, name="grad_swap_start"):
    n = len(slabs)

    def body(*refs):
        g_refs, land_refs, send_sems, recv_sems, token = refs[:n], refs[n:2 * n], refs[2 * n + 1], refs[2 * n + 2], refs[-1]
        x, y, c, _ = _place()
        for a in range(n):
            rh = g_refs[a].shape[1] // 2
            _remote(g_refs[a].at[:, pl.ds((1 - c) * rh, rh), :], land_refs[a], send_sems.at[a], recv_sems.at[a], (x, y, 1 - c)).start()
        token[...] = jnp.zeros_like(token)

    lands = [lax.empty((s.shape[0], s.shape[1] // 2, s.shape[2]), s.dtype) for s in slabs]
    return _start_call(body, name, list(slabs) + lands, n, after)


def swap_wait(send_sems, recv_sems, thru, after, name="grad_swap_wait"):
    n = len(thru) // 2

    def body(*refs):
        g_refs, land_refs, send_sems, recv_sems = refs[:n], refs[n:2 * n], refs[2 * n], refs[2 * n + 1]
        x, y, c, _ = _place()
        for a in range(n):
            rh = g_refs[a].shape[1] // 2
            cp = _remote(g_refs[a].at[:, pl.ds((1 - c) * rh, rh), :], land_refs[a], send_sems.at[a], recv_sems.at[a], (x, y, 1 - c))
            cp.wait_send()
            cp.wait_recv()

    res = _wait_call(body, name, thru, send_sems, recv_sems, after)
    return res[:n], res[n:]


def exchange_wait(send_sems, recv_sems, thru, after, name="grad_exchange_wait"):
    n = len(thru) // 2

    def body(*refs):
        p_refs, land_refs, send_sems, recv_sems = refs[:n], refs[n:2 * n], refs[2 * n], refs[2 * n + 1]
        x, y, c, chips = _place()
        me = _chip_index(x, y)
        for a in range(n):
            for j, chip in enumerate(chips):
                k = _chip_index(*chip)
                _remote(p_refs[a].at[k], land_refs[a].at[me], send_sems.at[3 * a + j], recv_sems.at[3 * a + j], (*chip, c)).wait_send()
                _remote(land_refs[a].at[k], land_refs[a].at[k], send_sems.at[3 * a + j], recv_sems.at[3 * a + j], (*chip, c)).wait_recv()

    res = _wait_call(body, name, thru, send_sems, recv_sems, after)
    return res[:n], res[n:]


_SLABS = {
    "mla_w_in": [("mla_w_in", 1, 0, 2)], "mla_w_uq": [("mla_w_uq", 2, 0, 2)], "mla_w_ukv": [("mla_w_ukv", 2, 0, 2)],
    "l0_mla_w_o": [("mla_w_o", 1, 0, 1)],
    "l0_w1024": [("mlp_w1", 2, 0, 1), ("mlp_w2", 1, 0, 1), ("xa_w_q", 1, 0, 1), ("xa_w_o", 1, 0, 1)],
    "l0_xa_w_kv": [("xa_w_kv", 2, 0, 1)],
    "l1_w1024": [("mlp_w1", 2, 1, 2), ("mlp_w2", 1, 1, 2), ("xa_w_q", 1, 1, 2), ("xa_w_o", 1, 1, 2), ("gdn_w_o", 1, 0, 1)],
    "l1_xa_w_kv": [("xa_w_kv", 2, 1, 2)], "gdn_w_in": [("gdn_w_in", 2, 0, 1)],
    "l23_w1024": [("mlp_w1", 2, 2, 4), ("mlp_w2", 1, 2, 4), ("xa_w_q", 1, 2, 4), ("xa_w_o", 1, 2, 4), ("mla_w_o", 1, 1, 2),
                  ("sc_w_o", 1, 0, 1)],
    "l23_xa_w_kv": [("xa_w_kv", 2, 2, 4)], "sc_w_in": [("sc_w_in", 2, 0, 1)],
}
_GROUPS = [(["mla_w_in", "mla_w_uq", "mla_w_ukv", "l0_mla_w_o"], None),
           (["l0_w1024", "l0_xa_w_kv"], (0, "xa")),
           (["l1_w1024", "l1_xa_w_kv", "gdn_w_in"], (1, "mix")),
           (["l23_w1024", "l23_xa_w_kv", "sc_w_in"], (2, "mix"))]
_SWAP_DONE = {(2, "mix"): (1, "mlp"), (1, "mix"): (0, "mlp")}
_RELAID = ("mla_w_in", "mla_w_uq", "mla_w_ukv", "gdn_w_in")
_SMALL = [("mla_q_norm", 1), ("mla_kv_norm", 1), ("gdn_conv_w", 2), ("sc_conv_w", 2)]
_REPL = ["gdn_a_log", "gdn_dt_bias", "gdn_o_norm", "norm_mix", "norm_mem", "norm_mlp", "mem_norm", "final_norm"]
_WEIGHTS = ['mla_w_in', 'mla_q_norm', 'mla_kv_norm', 'mla_w_uq', 'mla_w_ukv', 'mla_w_o', 'gdn_w_in', 'gdn_conv_w',
            'gdn_a_log', 'gdn_dt_bias', 'gdn_o_norm', 'gdn_w_o', 'sc_w_in', 'sc_conv_w', 'sc_w_o', 'norm_mix',
            'norm_mem', 'norm_mlp', 'xa_w_q', 'xa_w_kv', 'xa_w_o', 'mlp_w1', 'mlp_w2', 'mem_norm', 'final_norm']


class Layout:
    def __init__(self, shard_shapes):
        self.members, self.where, self.slab_dims = {}, {}, {}
        for slab, members in _SLABS.items():
            off, rows = 0, []
            for name, axis, l0, l1 in members:
                _, rpl, width = shard_shapes[name]
                rows.append((name, off, l0, l1, rpl))
                for layer in range(l0, l1):
                    self.where[(name, layer)] = (slab, off + (layer - l0) * rpl, rpl, width, axis)
                off += (l1 - l0) * rpl
            self.members[slab], self.slab_dims[slab] = rows, (off, width)

    def new_slabs(self, dtype):
        return {s: Slab(rows, width, dtype) for s, (rows, width) in self.slab_dims.items()}

    def loc(self, slabs, name, layer):
        slab, row0, rpl, width, axis = self.where[(name, layer)]
        if axis == 1:
            return Loc(slabs[slab], row0, N_CHIPS * rpl, width, 0)
        return Loc(slabs[slab], row0, rpl, N_CHIPS * width, 1)

    def _whole(self, name):
        (member,) = self.members[name]
        _, off, l0, l1, rpl = member
        assert off == 0 and l0 == 0
        return l1, rpl, self.slab_dims[name][1], dict((n, a) for n, a, _, _ in _SLABS[name])[name]

    def full(self, slabs, name):
        layers, rpl, width, axis = self._whole(name)
        blocks = slabs[name].arr.reshape(N_CHIPS, layers, rpl, width)
        return jnp.concatenate([blocks[s] for s in range(N_CHIPS)], axis=axis)

    def put_full(self, slabs, name, grad):
        layers, rpl, width, axis = self._whole(name)
        parts = jnp.stack(jnp.split(grad, N_CHIPS, axis=axis)).reshape(N_CHIPS, layers * rpl, width)
        slabs[name].arr = parts.astype(slabs[name].dtype)


def _small_pack(vals, names):
    flat = jnp.concatenate([vals[n].astype(F32).reshape(-1) for n in names])
    return jnp.pad(flat, (0, SMALL_ROWS * SMALL_COLS - flat.shape[0])).reshape(SMALL_ROWS, SMALL_COLS)


def _small_unpack(flat, like, names):
    out, off = {}, 0
    flat = flat.reshape(-1)
    for n in names:
        out[n] = flat[off:off + like[n].size].reshape(like[n].shape)
        off += like[n].size
    return out


_MLA_CFG = _Attn(MLA_H, 2 * LANES, MLA_NOPE, MLA_V, True, (MLA_NOPE + MLA_ROPE) ** -0.5, hp=8, hp_kv=8, blk=512)
_XA_CFG = _Attn(XA_H, XA_D, XA_D, XA_D, False, XA_D ** -0.5, hp=4, hp_kv=4, blk=1024)


def _mla_weights(w_in, w_uq, w_ukv):
    w_in_p = jnp.pad(w_in, ((0, 0), (0, MLA_ZPAD - w_in.shape[1])))
    w_uq_p = jnp.pad(w_uq.reshape(MLA_QR, MLA_H, MLA_NOPE + MLA_ROPE), ((0, 0), (0, 0), (0, 2 * LANES - MLA_NOPE - MLA_ROPE)))
    w_uq_p = w_uq_p.reshape(MLA_QR, MLA_H * 2 * LANES)
    kv = w_ukv.reshape(MLA_KVR, MLA_H, MLA_NOPE + MLA_V)
    w_ukv_p = jnp.concatenate([kv[:, :, :MLA_NOPE].reshape(MLA_KVR, -1), kv[:, :, MLA_NOPE:].reshape(MLA_KVR, -1)], axis=1)
    return w_in_p, w_uq_p, w_ukv_p


def _mla_weight_grads(d_in_p, d_uq_p, d_ukv_p):
    d_in = d_in_p[:, :MLA_QR + MLA_KVR + MLA_ROPE]
    d_uq = d_uq_p.reshape(MLA_QR, MLA_H, 2 * LANES)[:, :, :MLA_NOPE + MLA_ROPE].reshape(MLA_QR, -1)
    half = MLA_H * MLA_NOPE
    d_ukv = jnp.concatenate([d_ukv_p[:, :half].reshape(MLA_KVR, MLA_H, MLA_NOPE),
                             d_ukv_p[:, half:].reshape(MLA_KVR, MLA_H, MLA_V)], axis=2).reshape(MLA_KVR, -1)
    return d_in, d_uq, d_ukv


def _mla_fwd(xs, h, wts, w_o, qn, kvn, tabs, g_next, tag):
    w_in_p, w_uq_p, w_ukv_p = wts
    z = mm(h, w_in_p, "nn", f"{tag}_in")
    cq, ckv, kr = mla_mid_fwd(z, qn, kvn, tabs, f"{tag}_mid")
    q = mm(cq, w_uq_p, "nn", f"{tag}_uq", outs=(BF16,), epi=_epi_rope_q, per_row=tabs, tm=512)
    kv = mm(ckv, w_ukv_p, "nn", f"{tag}_ukv", outs=(BF16,))
    o, lse = flash_fwd(_MLA_CFG, q, kv, kv, kr, f"{tag}_attn")
    xs, h_next = residual_norm(o, w_o, xs, g_next, f"{tag}_out")
    return xs, h_next, (z, cq, ckv, kr, q, kv, o, lse)


def _mla_bwd(dx, h, wts, w_o, g_wo, qn, kvn, tabs, saved, tag):
    w_in_p, w_uq_p, w_ukv_p = wts
    z, cq, ckv, kr, q, kv, o, lse = saved
    mm(o, dx, "tn", f"{tag}_dwo", outs=(BF16,), out_loc=g_wo)
    do = mm(dx, w_o, "nt", f"{tag}_do", outs=(BF16,))
    dqp, delta = flash_dq(_MLA_CFG, q, kv, kv, kr, o, do, lse, BF16, f"{tag}_attn_dq", rope_tabs=tabs)
    dkv, dkr = flash_dkv(_MLA_CFG, q, kv, kv, kr, do, lse, delta, BF16, f"{tag}_attn_dkv")
    d_uq_p = mm(cq, dqp, "tn", f"{tag}_duq")
    dcq = mm(dqp, w_uq_p, "nt", f"{tag}_dcq")
    d_ukv_p = mm(ckv, dkv, "tn", f"{tag}_dukv")
    dckv = mm(dkv, w_ukv_p, "nt", f"{tag}_dckv")
    dz, dqn, dkvn = mla_mid_bwd(z, qn, kvn, tabs, dcq, dckv, dkr, f"{tag}_mid_bwd")
    d_in_p = mm(h, dz, "tn", f"{tag}_din")
    dh = (dz, w_in_p)
    d_in, d_uq, d_ukv = _mla_weight_grads(d_in_p, d_uq_p, d_ukv_p)
    return dh, dict(mla_w_in=d_in, mla_w_uq=d_uq, mla_w_ukv=d_ukv, mla_q_norm=dqn, mla_kv_norm=dkvn)


_GDN_QKV = 3 * GDN_H * GDN_D
_GDN_GATE_END = _GDN_QKV + GDN_H * GDN_D


def _gdn_weights(w_in):
    rep = lambda cols: jnp.repeat(cols, GDN_D, axis=1)
    return jnp.concatenate([w_in[:, :_GDN_GATE_END], rep(w_in[:, _GDN_GATE_END:_GDN_GATE_END + GDN_H]),
                            rep(w_in[:, _GDN_GATE_END + GDN_H:])], axis=1)


def _fold(x):
    return x.reshape(x.shape[0], -1, GDN_D).sum(-1)


def _gdn_fwd(xs, h, w_in_x, conv_w, a_log, dt_bias, o_norm, w_o, g_next, tag):
    z = mm(h, w_in_x, "nn", f"{tag}_in")
    qkv = gdn_conv_fwd(z, conv_w, f"{tag}_conv")
    a_x, dt_x = jnp.repeat(a_log.reshape(1, -1), GDN_D, axis=1), jnp.repeat(dt_bias.reshape(1, -1), GDN_D, axis=1)
    og, states, t_invs = gdn_chunk_fwd(qkv, z, a_x, dt_x, o_norm.reshape(1, -1), f"{tag}_chunks")
    xs, h_next = residual_norm(og, w_o, xs, g_next, f"{tag}_out")
    return xs, h_next, (z, qkv, a_x, dt_x, og, states, t_invs)


def _gdn_weights_compact(w_in):
    return jnp.pad(w_in, ((0, 0), (0, LANES - 2 * GDN_H)))


def _gdn_bwd(dx, h, w_in_c, conv_w, o_norm, w_o, g_wo, saved, tag):
    z, qkv, a_x, dt_x, og, states, t_invs = saved
    mm(og, dx, "tn", f"{tag}_dwo", outs=(BF16,), out_loc=g_wo)
    dog = mm(dx, w_o, "nt", f"{tag}_dog")
    dqkv, dgate, dba, da_x, ddt_x, don = gdn_chunk_bwd(qkv, z, a_x, dt_x, o_norm.reshape(1, -1), states, t_invs, dog,
                                                       f"{tag}_chunks_bwd")
    dpre, dconv = gdn_conv_bwd(z, conv_w, dqkv, f"{tag}_conv_bwd")
    dz = jnp.concatenate([dpre, dgate, dba], axis=1)
    d_in_c = mm(h, dz, "tn", f"{tag}_din")
    dh = (dz, w_in_c)
    return dh, dict(gdn_w_in=d_in_c[:, :_GDN_GATE_END + 2 * GDN_H], gdn_conv_w=dconv, gdn_a_log=_fold(da_x).reshape(-1),
                    gdn_dt_bias=_fold(ddt_x).reshape(-1), gdn_o_norm=don.reshape(-1))


def _sc_fwd(xs, h, w_in, conv_w, w_o, g_next, tag):
    z = mm(h, w_in, "nn", f"{tag}_in")
    y = sc_fwd(z, conv_w, f"{tag}_conv")
    xs, h_next = residual_norm(y, w_o, xs, g_next, f"{tag}_out")
    return xs, h_next, (z, y)


def _sc_bwd(dx, h, w_in, g_win, conv_w, w_o, g_wo, saved, tag):
    z, y = saved
    mm(y, dx, "tn", f"{tag}_dwo", outs=(BF16,), out_loc=g_wo)
    dy = mm(dx, w_o, "nt", f"{tag}_dy")
    db, dc, du, dconv = sc_bwd(z, conv_w, dy, f"{tag}_conv_bwd")
    dz = jnp.concatenate([db, dc, du], axis=1)
    mm(h, dz, "tn", f"{tag}_din", outs=(BF16,), out_loc=g_win)
    dh = (dz, w_in)
    return dh, dict(sc_conv_w=dconv)


def local_step(x, mem, pos, target, lay, wslabs, gslabs, small, before=None, after_bwd=None):
    depth = small["norm_mix"].shape[0]
    W = lambda name, layer: lay.loc(wslabs, name, layer)
    G = lambda name, layer: lay.loc(gslabs, name, layer)
    tabs = rope_tables(pos)
    mem_n = rmsnorm_fwd(mem, small["mem_norm"], "mem_norm")
    full = {n: lay.full(wslabs, n) for n in ("mla_w_in", "mla_w_uq", "mla_w_ukv")}
    mla_w = [_mla_weights(full["mla_w_in"][j], full["mla_w_uq"][j], full["mla_w_ukv"][j]) for j in range(full["mla_w_in"].shape[0])]
    gdn_in_x, gdn_in_c = {}, {}

    xs, h_pre = x, None
    saved = []
    for i in range(depth):
        j, kind = i // 3, i % 3
        tag = f"l{i}"
        if before is not None:
            xs = before(i, "mix", xs)
        if kind == 1:
            gdn_full = lay.full(wslabs, "gdn_w_in")[j]
            gdn_in_x[j], gdn_in_c[j] = _gdn_weights(gdn_full), _gdn_weights_compact(gdn_full)
        x_a = xs
        h = h_pre if h_pre is not None else rmsnorm_fwd(xs, small["norm_mix"][i], f"{tag}_norm_mix")
        g_mem = small["norm_mem"][i]
        if kind == 0:
            xs, hn, mix = _mla_fwd(xs, h, mla_w[j], W("mla_w_o", j), small["mla_q_norm"][j], small["mla_kv_norm"][j], tabs, g_mem,
                                   f"{tag}_mla")
        elif kind == 1:
            xs, hn, mix = _gdn_fwd(xs, h, gdn_in_x[j], small["gdn_conv_w"][j], small["gdn_a_log"][j], small["gdn_dt_bias"][j],
                                   small["gdn_o_norm"][j], W("gdn_w_o", j), g_mem, f"{tag}_gdn")
        else:
            xs, hn, mix = _sc_fwd(xs, h, W("sc_w_in", j), small["sc_conv_w"][j], W("sc_w_o", j), g_mem, f"{tag}_sc")
        if before is not None:
            xs = before(i, "xa", xs)
        x_b = xs
        xq = mm(hn, W("xa_w_q", i), "nn", f"{tag}_xa_q", outs=(BF16,))
        xkv = mm(mem_n, W("xa_w_kv", i), "nn", f"{tag}_xa_kv", outs=(BF16,))
        xo, xlse = flash_fwd(_XA_CFG, xq, xkv, xkv, None, f"{tag}_xa_attn")
        xs, hm = residual_norm(xo, W("xa_w_o", i), xs, small["norm_mlp"][i], f"{tag}_xa_out")
        x_c = xs
        h1, act = mm(hm, W("mlp_w1", i), "nn", f"{tag}_mlp_up", outs=(BF16, BF16), epi=_epi_relu2)
        xs, h_pre = residual_norm(act, W("mlp_w2", i), xs, small["norm_mix"][i + 1] if i + 1 < depth else None,
                                  f"{tag}_mlp_down", tm=512)
        saved.append((x_a, h, mix, x_b, hn, xq, xkv, xo, xlse, x_c, hm, h1, act))

    se, dx, d_final = loss_head(xs, small["final_norm"], target)
    dxb = dx.astype(BF16)

    def hooked(i, stage, dx, dxb):
        new = dx if after_bwd is None else after_bwd(i, stage, dx)
        return (dx, dxb) if new is dx else (new, new.astype(BF16))

    per_layer = {n: [None] * depth for n in ("norm_mix", "norm_mem", "norm_mlp")}
    mixer = {}
    dmem_n = jnp.zeros(mem.shape, F32)
    for i in reversed(range(depth)):
        j, kind = i // 3, i % 3
        tag = f"l{i}"
        x_a, h, mix, x_b, hn, xq, xkv, xo, xlse, x_c, hm, h1, act = saved[i]
        mm(act, dxb, "tn", f"{tag}_mlp_dw2", outs=(BF16,), out_loc=G("mlp_w2", i))
        dh1 = mm(dxb, W("mlp_w2", i), "nt", f"{tag}_mlp_dh1", outs=(BF16,), epi=_epi_relu2_bwd, extras=(h1,))
        mm(hm, dh1, "tn", f"{tag}_mlp_dw1", outs=(BF16,), out_loc=G("mlp_w1", i))
        dx, dxb, dg = mm(dh1, W("mlp_w1", i), "nt", f"{tag}_mlp_dhm", outs=_NORM_BWD_OUTS, epi=_epi_norm_bwd, extras=(x_c, dx),
                         vecs=(small["norm_mlp"][i],), row_outs=1, tm=512)
        per_layer["norm_mlp"][i] = dg.reshape(-1)
        dx, dxb = hooked(i, "mlp", dx, dxb)
        mm(xo, dxb, "tn", f"{tag}_xa_dwo", outs=(BF16,), out_loc=G("xa_w_o", i))
        dxo = mm(dxb, W("xa_w_o", i), "nt", f"{tag}_xa_do", outs=(BF16,))
        dxq, xdelta = flash_dq(_XA_CFG, xq, xkv, xkv, None, xo, dxo, xlse, BF16, f"{tag}_xa_attn_dq")
        (dxkv,) = flash_dkv(_XA_CFG, xq, xkv, xkv, None, dxo, xlse, xdelta, BF16, f"{tag}_xa_attn_dkv")
        mm(hn, dxq, "tn", f"{tag}_xa_dwq", outs=(BF16,), out_loc=G("xa_w_q", i))
        dx, dxb, dg = mm(dxq, W("xa_w_q", i), "nt", f"{tag}_xa_dhn", outs=_NORM_BWD_OUTS, epi=_epi_norm_bwd, extras=(x_b, dx),
                         vecs=(small["norm_mem"][i],), row_outs=1, tm=512)
        per_layer["norm_mem"][i] = dg.reshape(-1)
        mm(mem_n, dxkv, "tn", f"{tag}_xa_dwkv", outs=(BF16,), out_loc=G("xa_w_kv", i))
        dmem_n = mm(dxkv, W("xa_w_kv", i), "nt", f"{tag}_xa_dmem", epi=_epi_add, extras=(dmem_n,))
        dx, dxb = hooked(i, "xa", dx, dxb)
        if kind == 0:
            dh, gr = _mla_bwd(dxb, h, mla_w[j], W("mla_w_o", j), G("mla_w_o", j), small["mla_q_norm"][j], small["mla_kv_norm"][j],
                              tabs, mix, f"{tag}_mla")
        elif kind == 1:
            dh, gr = _gdn_bwd(dxb, h, gdn_in_c[j], small["gdn_conv_w"][j], small["gdn_o_norm"][j], W("gdn_w_o", j), G("gdn_w_o", j),
                              mix, f"{tag}_gdn")
        else:
            dh, gr = _sc_bwd(dxb, h, W("sc_w_in", j), G("sc_w_in", j), small["sc_conv_w"][j], W("sc_w_o", j), G("sc_w_o", j),
                             mix, f"{tag}_sc")
        if kind == 1:
            lay.put_full(gslabs, "gdn_w_in", gr.pop("gdn_w_in")[None])
        for n, g in gr.items():
            mixer.setdefault(n, {})[j] = g
        dz_mix, w_mix = dh
        dx, dxb, dg = mm(dz_mix, w_mix, "nt", f"{tag}_mix_dh", outs=_NORM_BWD_OUTS, epi=_epi_norm_bwd, extras=(x_a, dx),
                         vecs=(small["norm_mix"][i],), row_outs=1, tm=256 if kind == 1 else 512)
        per_layer["norm_mix"][i] = dg.reshape(-1)
        dx, dxb = hooked(i, "mix", dx, dxb)

    _, d_mem_norm = rmsnorm_bwd(mem, small["mem_norm"], dmem_n, jnp.zeros(mem.shape, F32), "mem_norm_bwd")
    grads = {n: jnp.stack(v) for n, v in per_layer.items()}
    for n, by_j in mixer.items():
        grads[n] = jnp.stack([by_j[j] for j in sorted(by_j)])
    grads["mem_norm"] = d_mem_norm
    grads["final_norm"] = d_final
    for n in ("mla_w_in", "mla_w_uq", "mla_w_ukv"):
        lay.put_full(gslabs, n, grads.pop(n))
    return se, dx, grads


def kernel(x, mem, positions, mla_w_in, mla_q_norm, mla_kv_norm, mla_w_uq, mla_w_ukv, mla_w_o, gdn_w_in, gdn_conv_w, gdn_a_log, gdn_dt_bias, gdn_o_norm, gdn_w_o, sc_w_in, sc_conv_w, sc_w_o, norm_mix, norm_mem, norm_mlp, xa_w_q, xa_w_kv, xa_w_o, mlp_w1, mlp_w2, mem_norm, final_norm, loss_target, m_mla_w_in, m_mla_q_norm, m_mla_kv_norm, m_mla_w_uq, m_mla_w_ukv, m_mla_w_o, m_gdn_w_in, m_gdn_conv_w, m_gdn_a_log, m_gdn_dt_bias, m_gdn_o_norm, m_gdn_w_o, m_sc_w_in, m_sc_conv_w, m_sc_w_o, m_norm_mix, m_norm_mem, m_norm_mlp, m_xa_w_q, m_xa_w_kv, m_xa_w_o, m_mlp_w1, m_mlp_w2, m_mem_norm, m_final_norm, v_mla_w_in, v_mla_q_norm, v_mla_kv_norm, v_mla_w_uq, v_mla_w_ukv, v_mla_w_o, v_gdn_w_in, v_gdn_conv_w, v_gdn_a_log, v_gdn_dt_bias, v_gdn_o_norm, v_gdn_w_o, v_sc_w_in, v_sc_conv_w, v_sc_w_o, v_norm_mix, v_norm_mem, v_norm_mlp, v_xa_w_q, v_xa_w_kv, v_xa_w_o, v_mlp_w1, v_mlp_w2, v_mem_norm, v_final_norm):
    given = dict(locals())
    p = {n: given[n] for n in _WEIGHTS}
    mom = {n: given["m_" + n] for n in _WEIGHTS}
    var = {n: given["v_" + n] for n in _WEIGHTS}
    split = sorted({n for members in _SLABS.values() for n, _, _, _ in members})
    lay = Layout({n: p[n].shape for n in split})
    flat2d = lambda a: a.reshape(-1, a.shape[-1])

    me = (2 * lax.axis_index("x") + lax.axis_index("y")).astype(jnp.int32)
    core = lax.axis_index("c").astype(jnp.int32)
    me1, c1, mc = me.reshape(1), core.reshape(1), jnp.stack([me, core])

    wslabs = lay.new_slabs(BF16)

    def cast_group(slabs, chip):
        for slab in slabs:
            for name, off, l0, l1, rpl in lay.members[slab]:
                cast_into(flat2d(p[name]), l0 * rpl, (l1 - l0) * rpl, wslabs[slab], off, chip, f"cast_{slab}_{name}")

    first = _GROUPS[0][0]
    cast_group(first, me1)
    small_names = [n for n, _ in _SMALL]
    words = lax.bitcast_convert_type(jnp.concatenate([p[n].reshape(-1) for n in small_names]), BF16).reshape(-1)
    words = jnp.pad(words, (0, SMALL_ROWS * SMALL_COLS - words.shape[0])).reshape(1, SMALL_ROWS, SMALL_COLS)
    small_slab = lax.dynamic_update_slice(jnp.zeros((N_CHIPS, SMALL_ROWS, SMALL_COLS), BF16), words, (me, 0, 0))

    send0, recv0, thru0, token = gather_start([wslabs[s].arr for s in first] + [small_slab], me1, "weight_gather_start_first")
    in_flight = {}
    for slabs, point in _GROUPS[1:]:
        cast_group(slabs, me1 + token[0, 0].astype(jnp.int32))
        send, recv, thru, token = gather_start([wslabs[s].arr for s in slabs], token, f"weight_gather_start_{slabs[0]}")
        in_flight[point] = (send, recv, thru, slabs)
    started_token = token
    landed = gather_wait(send0, recv0, thru0, started_token, "weight_gather_wait_first")
    gathered = gather_forward(landed, "weight_gather_forward_first")
    for s, arr in zip(first, gathered):
        wslabs[s].arr = arr

    def before(i, stage, xs):
        if (i, stage) in in_flight:
            send, recv, thru, slabs = in_flight[(i, stage)]
            landed = gather_wait(send, recv, thru, xs, f"weight_gather_wait_{slabs[0]}")
            for s, arr in zip(slabs, gather_forward(landed, f"weight_gather_forward_{slabs[0]}")):
                wslabs[s].arr = arr
        return xs

    small = {n: p[n] for n in _REPL}
    got, off = gathered[-1].reshape(N_CHIPS, -1), 0
    for n, ax in _SMALL:
        vals = lax.bitcast_convert_type(got[:, off:off + 2 * p[n].size].reshape(N_CHIPS, p[n].size, 2), F32)
        vals = vals.reshape((N_CHIPS,) + p[n].shape)
        small[n] = jnp.concatenate([vals[s] for s in range(N_CHIPS)], axis=ax)
        off += 2 * p[n].size

    gslabs = lay.new_slabs(BF16)
    complete_at = {point: slabs for slabs, point in _GROUPS[1:]}
    swapping, exchanging = {}, []

    def after_bwd(i, stage, dx):
        if (i, stage) in swapping:
            slabs, send, recv, thru = swapping.pop((i, stage))
            g, swapped = swap_wait(send, recv, thru, dx, f"grad_swap_wait_{slabs[0]}")
        elif (i, stage) in complete_at:
            slabs = complete_at[(i, stage)]
            g = [gslabs[s].arr for s in slabs]
            if (i, stage) in _SWAP_DONE:
                send, recv, thru, token = swap_start(g, c1, f"grad_swap_start_{slabs[0]}")
                swapping[_SWAP_DONE[(i, stage)]] = (slabs, send, recv, thru)
                return dx + token[0, 0]
            swapped = pair_swap_halves(g, f"grad_pair_swap_{slabs[0]}")
        else:
            return dx
        part = [pair_add(a, b, c1, f"pair_add_{s}") for a, b, s in zip(g, swapped, slabs)]
        send, recv, thru, token = exchange_start(part, c1, f"grad_exchange_start_{slabs[0]}")
        exchanging.append((slabs, send, recv, thru))
        return dx + token[0, 0]

    se, dx, sgrads = local_step(x[0], mem[0], positions.reshape(-1, 1), loss_target[0], lay, wslabs, gslabs, small,
                                before, after_bwd)
    loss = lax.psum(0.5 * jnp.sum(se) / x.shape[-1], ("x", "y", "c"))
    names, parts, received = [], [], []
    for slabs, send, recv, thru in exchanging:
        part, got = exchange_wait(send, recv, thru, dx, f"grad_exchange_wait_{slabs[0]}")
        names, parts, received = names + slabs, parts + list(part), received + list(got)

    axes = dict(_SMALL)
    small_order = small_names + _REPL
    slots = []
    for s in range(N_CHIPS):
        vals = {n: (lax.slice_in_dim(g, s * p[n].shape[axes[n]], (s + 1) * p[n].shape[axes[n]], axis=axes[n]) if n in axes else g)
                for n, g in sgrads.items()}
        slots.append(_small_pack(vals, small_order))
    g_last = [gslabs[s].arr for s in first] + [jnp.stack(slots).astype(BF16)]
    names_last = first + ["small"]
    swapped_last = pair_swap_halves(g_last, "grad_pair_swap_last")
    part_last = [pair_add(g, b, c1, f"pair_add_{s}") for g, b, s in zip(g_last, swapped_last, names_last)]
    send, recv, thru, token = exchange_start(part_last, c1, "grad_exchange_start_last")
    mc_after = mc + token[0, 0].astype(jnp.int32)
    halves = [chip_sum(q, r, mc_after, f"chip_sum_{s}") for q, r, s in zip(parts, received, names)]
    part_last, got_last = exchange_wait(send, recv, thru, list(halves), "grad_exchange_wait_last")
    halves += [chip_sum(q, r, mc, f"chip_sum_{s}") for q, r, s in zip(part_last, got_last, names_last)]
    reduced = dict(zip(names + names_last, pair_join_halves(halves)))

    res = {}
    for slab in _SLABS:
        for name, off, l0, l1, rpl in lay.members[slab]:
            res[name] = adamw(reduced[slab], off, flat2d(p[name]), flat2d(mom[name]), flat2d(var[name]), l0 * rpl, (l1 - l0) * rpl,
                              res.get(name), f"adamw_{slab}_{name}")
    for name in split:
        res[name] = [o.reshape(p[name].shape) for o in res[name]]
    sp = {k: _small_pack(d, small_order) for k, d in (("w", p), ("m", mom), ("v", var))}
    outs = adamw(reduced["small"], 0, sp["w"], sp["m"], sp["v"], 0, SMALL_ROWS, None, "adamw_small")
    unpacked = [_small_unpack(o, p, small_order) for o in outs]
    for n in small_order:
        res[n] = [u[n] for u in unpacked]
    return (loss, dx[None], *[res[n][k] for k in range(4) for n in _WEIGHTS])
```

```python
import jax
import jax.numpy as jnp
from jax import lax
from jax.experimental import pallas as pl
from jax.experimental.pallas import tpu as pltpu

F32 = jnp.float32
BF16 = jnp.bfloat16
MESH = pl.DeviceIdType.MESH

EPS = 1e-6
ROPE_THETA = 10000.0
N_CHIPS = 4
LANES = 128
VMEM_LIMIT = 56 * 1024 * 1024
NEG = -1e30

MLA_H, MLA_NOPE, MLA_ROPE, MLA_V = 8, 128, 64, 128
MLA_QR, MLA_KVR = 384, 256
MLA_ZPAD = 768
GDN_H, GDN_D, GDN_C = 8, 128, 64
XA_H, XA_D = 4, 256

ADAM_LR, ADAM_B1, ADAM_B2, ADAM_EPS, ADAM_WD, ADAM_STEP = 0.001, 0.9, 0.999, 1e-08, 0.01, 10

SMALL_ROWS, SMALL_COLS = 32, 1024


def _cparams(sem=None):
    return pltpu.CompilerParams(dimension_semantics=sem, vmem_limit_bytes=VMEM_LIMIT)


def _pick(dim, pref):
    t = (min(pref, dim) // LANES) * LANES
    while t >= LANES:
        if dim % t == 0:
            return t
        t -= LANES
    return dim


def _pick_rows(rows, pref, *offsets):
    t = (min(pref, rows) // 16) * 16
    while t > 16 and (rows % t or any(o % t for o in offsets)):
        t -= 16
    return t


class Slab:
    def __init__(self, rows, width, dtype, arr=None):
        self.shape, self.dtype, self.arr = (N_CHIPS, rows, width), dtype, arr


class Loc:
    def __init__(self, slab, row0, K, N, axis):
        self.slab, self.row0, self.K, self.N, self.axis = slab, row0, K, N, axis
        self.Ks = K // N_CHIPS if axis == 0 else K
        self.Ns = N // N_CHIPS if axis == 1 else N

    def tile_spec(self, tr, tc, rc):
        assert self.row0 % tr == 0 and self.Ks % tr == 0 and self.Ns % tc == 0, (self.row0, self.Ks, self.Ns, tr, tc)
        r0, rb, cb = self.row0 // tr, self.Ks // tr, self.Ns // tc
        if self.axis == 0:
            return pl.BlockSpec((None, tr, tc), lambda i, j: (rc(i, j)[0] // rb, r0 + rc(i, j)[0] % rb, rc(i, j)[1]))
        return pl.BlockSpec((None, tr, tc), lambda i, j: (rc(i, j)[1] // cb, r0 + rc(i, j)[0], rc(i, j)[1] % cb))

    def slot_spec(self, slot, tr, tc, rc):
        assert self.row0 % tr == 0, (self.row0, tr)
        r0 = self.row0 // tr
        return pl.BlockSpec((None, tr, tc), lambda i, j: (slot, r0 + rc(i, j)[0], rc(i, j)[1]))


_DIMS = {"nn": ((1,), (0,)), "nt": ((1,), (1,)), "tn": ((0,), (0,))}
_ANY = pl.BlockSpec(memory_space=pl.ANY)


def mm(a, b, mode, name, outs=(F32,), epi=None, extras=(), tm=1024, tn=1024, out_loc=None, vecs=(), row_outs=0, per_row=()):
    full_rows = bool(vecs) or row_outs > 0 or bool(per_row)
    b_loc = b if isinstance(b, Loc) else None
    if mode == "nn":
        M, K = a.shape
        K2, N = (b_loc.K, b_loc.N) if b_loc else b.shape
    elif mode == "nt":
        M, K = a.shape
        N, K2 = (b_loc.K, b_loc.N) if b_loc else b.shape
    else:
        K, M = a.shape
        K2, N = b.shape
    assert K == K2, (name, a.shape, K2, N)
    tm = _pick(out_loc.Ks if (out_loc and out_loc.axis == 0) else M, tm)
    n_split = full_rows and b_loc is not None and mode == "nt" and b_loc.axis == 0
    if out_loc is not None and out_loc.axis == 1:
        tn = _pick(out_loc.Ns, tn)
    elif n_split:
        tn = N
    elif b_loc is not None and ((mode == "nn" and b_loc.axis == 1) or (mode == "nt" and b_loc.axis == 0)):
        tn = _pick(b_loc.Ns if mode == "nn" else b_loc.Ks, tn)
    elif b_loc is not None:
        tn = N if full_rows else _pick(N, min(tn, 512))
    else:
        tn = N if full_rows else _pick(N, tn)
    assert tn == N or not full_rows, name

    parts = 1
    if mode == "tn":
        a_spec = pl.BlockSpec((K, tm), lambda i, j: (0, i))
        b_specs, b_args = [pl.BlockSpec((K, tn), lambda i, j: (0, j))], [b]
    else:
        a_spec = pl.BlockSpec((tm, K), lambda i, j: (i, 0))
        if b_loc is None:
            b_specs = [pl.BlockSpec((K, tn), lambda i, j: (0, j)) if mode == "nn" else pl.BlockSpec((tn, K), lambda i, j: (j, 0))]
            b_args = [b]
        elif mode == "nn" and b_loc.axis == 1:
            b_specs, b_args = [b_loc.tile_spec(K, tn, lambda i, j: (0, j))], [b_loc.slab.arr]
        elif n_split:
            b_specs = [b_loc.slot_spec(s, b_loc.Ks, K, lambda i, j: (0, 0)) for s in range(N_CHIPS)]
            b_args = [b_loc.slab.arr] * N_CHIPS
        elif mode == "nt" and b_loc.axis == 0:
            b_specs, b_args = [b_loc.tile_spec(tn, K, lambda i, j: (j, 0))], [b_loc.slab.arr]
        elif mode == "nn":
            parts = N_CHIPS
            b_specs = [b_loc.slot_spec(s, b_loc.Ks, tn, lambda i, j: (0, j)) for s in range(parts)]
            b_args = [b_loc.slab.arr] * parts
        else:
            parts = N_CHIPS
            b_specs = [b_loc.slot_spec(s, tn, b_loc.Ns, lambda i, j: (j, 0)) for s in range(parts)]
            b_args = [b_loc.slab.arr] * parts
    kp = K // parts
    n_b = N_CHIPS if n_split else parts
    n_ex, n_out = len(extras) + len(per_row) + len(vecs), len(outs)
    dims = (_DIMS[mode], ((), ()))

    def body(*refs):
        a_ref = refs[0]
        b_refs = refs[1:1 + n_b]
        ex_refs = refs[1 + n_b:1 + n_b + n_ex]
        o_refs = refs[len(refs) - n_out - row_outs:len(refs) - row_outs]
        r_refs = refs[len(refs) - row_outs:]
        acc = None
        if n_split:
            av = a_ref[...].astype(BF16)
            acc = jnp.concatenate([lax.dot_general(av, b_ref[...].astype(BF16), dims, preferred_element_type=F32)
                                   for b_ref in b_refs], axis=1)
        for s in range(0 if n_split else parts):
            av = a_ref[...] if parts == 1 else a_ref[:, s * kp:(s + 1) * kp]
            d = lax.dot_general(av.astype(BF16), b_refs[s][...].astype(BF16), dims, preferred_element_type=F32)
            acc = d if acc is None else acc + d
        res = epi(acc, *[e[...] for e in ex_refs]) if epi is not None else (acc,)
        for o_ref, v in zip(o_refs, res[:n_out]):
            o_ref[...] = v.astype(o_ref.dtype)
        for r_ref, v in zip(r_refs, res[n_out:]):
            @pl.when(pl.program_id(0) == 0)
            def _():
                r_ref[...] = jnp.zeros_like(r_ref)

            r_ref[...] += v

    mn_spec = pl.BlockSpec((tm, tn), lambda i, j: (i, j))
    row_spec = pl.BlockSpec((1, tn), lambda i, j: (0, j))
    in_specs = ([a_spec] + b_specs + [mn_spec] * len(extras) + [pl.BlockSpec((tm, r.shape[1]), lambda i, j: (i, 0)) for r in per_row]
                + [row_spec] * len(vecs))
    args = [a] + b_args + list(extras) + list(per_row) + [v.reshape(1, N) for v in vecs]
    aliases = {}
    if out_loc is None:
        out_specs = [mn_spec] * n_out + [row_spec] * row_outs
        out_shape = [jax.ShapeDtypeStruct((M, N), d) for d in outs] + [jax.ShapeDtypeStruct((1, N), F32)] * row_outs
    else:
        assert n_out == 1 and mode == "tn"
        out_specs = [out_loc.tile_spec(tm, tn, lambda i, j: (i, j))]
        out_shape = [jax.ShapeDtypeStruct(out_loc.slab.shape, out_loc.slab.dtype)]
        if out_loc.slab.arr is not None:
            in_specs.append(_ANY)
            args.append(out_loc.slab.arr)
            aliases = {len(args) - 1: 0}

    res = pl.pallas_call(
        body, name=name, grid=(M // tm, N // tn), in_specs=in_specs, out_specs=out_specs, out_shape=out_shape,
        input_output_aliases=aliases, compiler_params=_cparams(("arbitrary" if row_outs else "parallel", "parallel")),
    )(*args)
    if out_loc is not None:
        out_loc.slab.arr = res[0]
        return None
    return res[0] if len(res) == 1 else tuple(res)


def _epi_add(acc, r):
    return (acc + r,)


def _epi_add_norm(acc, r, g):
    x = acc + r
    return x, _rms(x, g)


def _epi_norm_bwd(acc, x, dx_in, g):
    r = lax.rsqrt(jnp.mean(x * x, axis=-1, keepdims=True) + EPS)
    xh = x * r
    dxh = acc * g
    dx = dx_in + r * (dxh - xh * jnp.mean(dxh * xh, axis=-1, keepdims=True))
    return dx, dx, jnp.sum(acc * xh, axis=0, keepdims=True)


_NORM_BWD_OUTS = (F32, BF16)


def residual_norm(a, w, xs, g, name, tm=1024):
    if g is None:
        return mm(a, w, "nn", name, epi=_epi_add, extras=(xs,), tm=tm), None
    return mm(a, w, "nn", name, outs=(F32, BF16), epi=_epi_add_norm, extras=(xs,), vecs=(g,), tm=tm)


def _epi_relu2(acc):
    r = jnp.maximum(acc, 0.0)
    return acc, r * r


def _epi_relu2_bwd(acc, h1):
    return (acc * (2.0 * jnp.maximum(h1.astype(F32), 0.0)),)


def _rms(x, g):
    return x * lax.rsqrt(jnp.mean(x * x, axis=-1, keepdims=True) + EPS) * g


def _row_spec(ts, cols):
    return pl.BlockSpec((ts, cols), lambda i: (i, 0))


def _par_spec(cols):
    return pl.BlockSpec((1, cols), lambda i: (0, 0))


def rmsnorm_fwd(x, g, name, ts=256):
    T, D = x.shape
    ts = min(ts, T)

    def body(x_ref, g_ref, o_ref):
        o_ref[...] = _rms(x_ref[...], g_ref[...]).astype(o_ref.dtype)

    return pl.pallas_call(
        body, name=name, grid=(T // ts,),
        in_specs=[_row_spec(ts, D), _par_spec(D)], out_specs=_row_spec(ts, D),
        out_shape=jax.ShapeDtypeStruct((T, D), BF16), compiler_params=_cparams(("parallel",)),
    )(x, g.reshape(1, D))


def rmsnorm_bwd(x, g, dy, dx_in, name, ts=256):
    T, D = x.shape
    ts = min(ts, T)

    def body(x_ref, g_ref, dy_ref, dxi_ref, dx_ref, dg_ref):
        xv = x_ref[...]
        r = lax.rsqrt(jnp.mean(xv * xv, axis=-1, keepdims=True) + EPS)
        xh = xv * r
        dyv = dy_ref[...].astype(F32)
        dxh = dyv * g_ref[...]
        dx_ref[...] = dxi_ref[...] + r * (dxh - xh * jnp.mean(dxh * xh, axis=-1, keepdims=True))
        dg = jnp.sum(dyv * xh, axis=0, keepdims=True)

        @pl.when(pl.program_id(0) == 0)
        def _():
            dg_ref[...] = jnp.zeros_like(dg_ref)

        dg_ref[...] += dg

    dx, dg = pl.pallas_call(
        body, name=name, grid=(T // ts,),
        in_specs=[_row_spec(ts, D), _par_spec(D), _row_spec(ts, D), _row_spec(ts, D)],
        out_specs=[_row_spec(ts, D), _par_spec(D)],
        out_shape=[jax.ShapeDtypeStruct((T, D), F32), jax.ShapeDtypeStruct((1, D), F32)],
        compiler_params=_cparams(("arbitrary",)),
    )(x, g.reshape(1, D), dy, dx_in)
    return dx, dg.reshape(D)


def rope_tables(pos, name="rope_tables"):
    T = pos.shape[0]
    half = MLA_ROPE // 2
    inv = ROPE_THETA ** (-jnp.arange(0, MLA_ROPE, 2, dtype=F32) / MLA_ROPE)
    inv_row = jnp.concatenate([inv, inv, jnp.zeros((LANES - MLA_ROPE,), F32)]).reshape(1, LANES)

    def body(p_ref, f_ref, c_ref, a_ref, b_ref):
        ang = p_ref[...].astype(F32) * f_ref[...]
        lane = lax.broadcasted_iota(jnp.int32, ang.shape, 1)
        c, s = jnp.cos(ang), jnp.sin(ang)
        c_ref[...] = jnp.where(lane < MLA_ROPE, c, 0.0)
        a_ref[...] = jnp.where(lane < half, -s, 0.0)
        b_ref[...] = jnp.where((lane >= half) & (lane < MLA_ROPE), s, 0.0)

    sh = jax.ShapeDtypeStruct((T, LANES), F32)
    return pl.pallas_call(body, name=name, out_shape=[sh, sh, sh], compiler_params=_cparams())(pos, inv_row)


def _roll_l(x):
    return pltpu.roll(x, LANES - MLA_ROPE // 2, 1)


def _roll_r(x):
    return pltpu.roll(x, MLA_ROPE // 2, 1)


def _rope(r, c, sa, sb):
    return r * c + _roll_l(r) * sa + _roll_r(r) * sb


def _rope_t(d, c, sa, sb):
    return d * c + _roll_r(d * sa) + _roll_l(d * sb)


def _epi_rope_q(acc, c, sa, sb):
    hw = 2 * LANES
    parts = []
    for h in range(acc.shape[1] // hw):
        parts += [acc[:, h * hw:h * hw + LANES], _rope(acc[:, h * hw + LANES:(h + 1) * hw], c, sa, sb)]
    return (jnp.concatenate(parts, axis=1),)


def mla_mid_fwd(z, qn, kvn, tabs, name, ts=256):
    T = z.shape[0]
    ts = min(ts, T)
    a0, a1 = MLA_QR, MLA_QR + MLA_KVR

    def body(z_ref, qn_ref, kvn_ref, c_ref, sa_ref, sb_ref, cq_ref, ckv_ref, kr_ref):
        cq_ref[...] = _rms(z_ref[:, 0:a0], qn_ref[...]).astype(BF16)
        ckv_ref[...] = _rms(z_ref[:, a0:a1], kvn_ref[...]).astype(BF16)
        kr_ref[...] = _rope(z_ref[:, a1:MLA_ZPAD], c_ref[...], sa_ref[...], sb_ref[...]).astype(BF16)

    return pl.pallas_call(
        body, name=name, grid=(T // ts,),
        in_specs=[_row_spec(ts, MLA_ZPAD), _par_spec(MLA_QR), _par_spec(MLA_KVR)] + [_row_spec(ts, LANES)] * 3,
        out_specs=[_row_spec(ts, MLA_QR), _row_spec(ts, MLA_KVR), _row_spec(ts, LANES)],
        out_shape=[jax.ShapeDtypeStruct((T, MLA_QR), BF16), jax.ShapeDtypeStruct((T, MLA_KVR), BF16),
                   jax.ShapeDtypeStruct((T, LANES), BF16)],
        compiler_params=_cparams(("parallel",)),
    )(z, qn.reshape(1, -1), kvn.reshape(1, -1), *tabs)


def mla_mid_bwd(z, qn, kvn, tabs, dcq, dckv, dkr, name, ts=256):
    T = z.shape[0]
    ts = min(ts, T)
    a0, a1 = MLA_QR, MLA_QR + MLA_KVR

    def body(z_ref, qn_ref, kvn_ref, c_ref, sa_ref, sb_ref, dcq_ref, dckv_ref, dkr_ref, dz_ref, dqn_ref, dkvn_ref):
        _, vq = jax.vjp(_rms, z_ref[:, 0:a0], qn_ref[...])
        dzq, dqn = vq(dcq_ref[...].astype(F32))
        _, vk = jax.vjp(_rms, z_ref[:, a0:a1], kvn_ref[...])
        dzk, dkvn = vk(dckv_ref[...].astype(F32))
        dz_ref[:, 0:a0] = dzq.astype(dz_ref.dtype)
        dz_ref[:, a0:a1] = dzk.astype(dz_ref.dtype)
        dz_ref[:, a1:MLA_ZPAD] = _rope_t(dkr_ref[...].astype(F32), c_ref[...], sa_ref[...], sb_ref[...]).astype(dz_ref.dtype)

        @pl.when(pl.program_id(0) == 0)
        def _():
            dqn_ref[...] = jnp.zeros_like(dqn_ref)
            dkvn_ref[...] = jnp.zeros_like(dkvn_ref)

        dqn_ref[...] += dqn
        dkvn_ref[...] += dkvn

    dz, dqn, dkvn = pl.pallas_call(
        body, name=name, grid=(T // ts,),
        in_specs=[_row_spec(ts, MLA_ZPAD), _par_spec(MLA_QR), _par_spec(MLA_KVR)] + [_row_spec(ts, LANES)] * 3
        + [_row_spec(ts, MLA_QR), _row_spec(ts, MLA_KVR), _row_spec(ts, LANES)],
        out_specs=[_row_spec(ts, MLA_ZPAD), _par_spec(MLA_QR), _par_spec(MLA_KVR)],
        out_shape=[jax.ShapeDtypeStruct((T, MLA_ZPAD), BF16), jax.ShapeDtypeStruct((1, MLA_QR), F32),
                   jax.ShapeDtypeStruct((1, MLA_KVR), F32)],
        compiler_params=_cparams(("arbitrary",)),
    )(z, qn.reshape(1, -1), kvn.reshape(1, -1), *tabs, dcq, dckv, dkr)
    return dz, dqn.reshape(-1), dkvn.reshape(-1)


def loss_head(x, g, target, name="loss_head", ts=256):
    T, D = x.shape
    ts = min(ts, T)

    def body(x_ref, g_ref, t_ref, se_ref, dx_ref, dg_ref):
        xv = x_ref[...]
        r = lax.rsqrt(jnp.mean(xv * xv, axis=-1, keepdims=True) + EPS)
        xh = xv * r
        err = xh * g_ref[...] - t_ref[...]
        dy = err * (1.0 / D)
        dxh = dy * g_ref[...]
        dx_ref[...] = r * (dxh - xh * jnp.mean(dxh * xh, axis=-1, keepdims=True))

        @pl.when(pl.program_id(0) == 0)
        def _():
            se_ref[...] = jnp.zeros_like(se_ref)
            dg_ref[...] = jnp.zeros_like(dg_ref)

        se_ref[...] += jnp.sum(err * err, axis=0, keepdims=True)
        dg_ref[...] += jnp.sum(dy * xh, axis=0, keepdims=True)

    se, dx, dg = pl.pallas_call(
        body, name=name, grid=(T // ts,),
        in_specs=[_row_spec(ts, D), _par_spec(D), _row_spec(ts, D)],
        out_specs=[_par_spec(D), _row_spec(ts, D), _par_spec(D)],
        out_shape=[jax.ShapeDtypeStruct((1, D), F32), jax.ShapeDtypeStruct((T, D), F32), jax.ShapeDtypeStruct((1, D), F32)],
        compiler_params=_cparams(("arbitrary",)),
    )(x, g.reshape(1, D), target)
    return se, dx, dg.reshape(D)


def _dot_nt(a, b):
    return lax.dot_general(a, b, (((1,), (1,)), ((), ())), preferred_element_type=F32)


def _dot_nn(a, b):
    return lax.dot_general(a, b, (((1,), (0,)), ((), ())), preferred_element_type=F32)


class _Attn:
    def __init__(self, H, dq, dk1, dv, causal, scale, hp, hp_kv, blk=256):
        self.H, self.dq, self.dk1, self.dv, self.causal, self.scale, self.blk = H, dq, dk1, dv, causal, scale, blk
        self.hp, self.hp_kv = hp, hp_kv


def _cols(ref, rows, hh, width):
    return ref[rows, hh * width:(hh + 1) * width]


def _keys(cfg, k1_ref, k2_ref, rows, hh):
    ks = _cols(k1_ref, rows, hh, cfg.dk1)
    if k2_ref is not None:
        ks = jnp.concatenate([ks, k2_ref[rows, :]], axis=1)
    return ks


_ONE_BUFFER = pl.Buffered(1)


def _blocks(cfg, Tq, Tk):
    tq, tk = min(cfg.blk, Tq), min(cfg.blk, Tk)
    assert tq == tk or not cfg.causal
    return tq, tk


def _attn_specs(cfg, hp, t, Tk, has_k2, by_q):
    g = cfg.H // hp
    if by_q:
        specs = [pl.BlockSpec((t, hp * cfg.dq), lambda h, i: (i, h)),
                 pl.BlockSpec((Tk, hp * cfg.dk1), lambda h, i: (0, h), pipeline_mode=_ONE_BUFFER),
                 pl.BlockSpec((Tk, hp * cfg.dv), lambda h, i: (0, g + h), pipeline_mode=_ONE_BUFFER)]
        if has_k2:
            specs.append(pl.BlockSpec((Tk, LANES), lambda h, i: (0, 0), pipeline_mode=_ONE_BUFFER))
    else:
        specs = [None,
                 pl.BlockSpec((t, hp * cfg.dk1), lambda j, h: (j, h)),
                 pl.BlockSpec((t, hp * cfg.dv), lambda j, h: (j, g + h))]
        if has_k2:
            specs.append(pl.BlockSpec((t, LANES), lambda j, h: (j, 0)))
    return specs


def _mask(s, diagonal):
    if not diagonal:
        return s
    return jnp.where(lax.broadcasted_iota(jnp.int32, s.shape, 0) >= lax.broadcasted_iota(jnp.int32, s.shape, 1), s, NEG)


def flash_fwd(cfg, q, k1, v, k2, name):
    Tq, Tk = q.shape[0], k1.shape[0]
    t, tk = _blocks(cfg, Tq, Tk)
    nkb = Tk // tk
    has_k2 = k2 is not None
    hp = cfg.hp

    def body(*refs):
        q_ref, k1_ref, v_ref = refs[:3]
        k2_ref = refs[3] if has_k2 else None
        o_ref, lse_ref = refs[-2], refs[-1]
        i = pl.program_id(1)
        qs = [_cols(q_ref, slice(None), hh, cfg.dq) for hh in range(hp)]

        def step(j, carry, diagonal=False):
            rows = pl.ds(pl.multiple_of(j * tk, tk), tk)
            out = []
            for hh in range(hp):
                m, l, acc = carry[hh]
                s = _mask(_dot_nt(qs[hh], _keys(cfg, k1_ref, k2_ref, rows, hh)) * cfg.scale, diagonal)
                m2 = jnp.maximum(m, jnp.max(s, axis=-1, keepdims=True))
                p = jnp.exp(s - m2)
                alpha = jnp.exp(m - m2)
                l2 = alpha * l + jnp.sum(p, axis=-1, keepdims=True)
                acc2 = alpha * acc + _dot_nn(p.astype(BF16), _cols(v_ref, rows, hh, cfg.dv))
                out.append((m2, l2, acc2))
            return tuple(out)

        init = tuple((jnp.full((t, 1), NEG, F32), jnp.zeros((t, 1), F32), jnp.zeros((t, cfg.dv), F32)) for _ in range(hp))
        res = lax.fori_loop(0, i if cfg.causal else nkb, step, init)
        if cfg.causal:
            res = step(i, res, True)
        for hh in range(hp):
            m, l, acc = res[hh]
            o_ref[:, hh * cfg.dv:(hh + 1) * cfg.dv] = (acc / l).astype(o_ref.dtype)
            lse_ref[hh] = m + jnp.log(l)

    args = [q, k1, v] + ([k2] if has_k2 else [])
    return pl.pallas_call(
        body, name=name, grid=(cfg.H // hp, Tq // t), in_specs=_attn_specs(cfg, hp, t, Tk, has_k2, True),
        out_specs=[pl.BlockSpec((t, hp * cfg.dv), lambda h, i: (i, h)), pl.BlockSpec((hp, t, 1), lambda h, i: (h, i, 0))],
        out_shape=[jax.ShapeDtypeStruct((Tq, cfg.H * cfg.dv), BF16), jax.ShapeDtypeStruct((cfg.H, Tq, 1), F32)],
        compiler_params=_cparams(("parallel", "parallel")),
    )(*args)


def flash_dq(cfg, q, k1, v, k2, o, do, lse, out_dtype, name, rope_tabs=None):
    Tq, Tk = q.shape[0], k1.shape[0]
    t, tk = _blocks(cfg, Tq, Tk)
    nkb = Tk // tk
    has_k2 = k2 is not None
    hp = cfg.hp
    n_tab = 0 if rope_tabs is None else len(rope_tabs)

    def body(*refs):
        q_ref, k1_ref, v_ref = refs[:3]
        k2_ref = refs[3] if has_k2 else None
        tab_refs = refs[len(refs) - 5 - n_tab:len(refs) - 5]
        o_ref, do_ref, lse_ref, dq_ref, dl_ref = refs[-5:]
        i = pl.program_id(1)
        qs = [_cols(q_ref, slice(None), hh, cfg.dq) for hh in range(hp)]
        dos = [_cols(do_ref, slice(None), hh, cfg.dv) for hh in range(hp)]
        lses = [lse_ref[hh] for hh in range(hp)]
        deltas = []
        for hh in range(hp):
            d = jnp.sum(dos[hh].astype(F32) * _cols(o_ref, slice(None), hh, cfg.dv).astype(F32), axis=-1, keepdims=True)
            dl_ref[hh] = d
            deltas.append(d)

        def step(j, dqs, diagonal=False):
            rows = pl.ds(pl.multiple_of(j * tk, tk), tk)
            out = []
            for hh in range(hp):
                ks = _keys(cfg, k1_ref, k2_ref, rows, hh)
                s = _mask(_dot_nt(qs[hh], ks) * cfg.scale, diagonal)
                p = jnp.exp(s - lses[hh])
                dp = _dot_nt(dos[hh], _cols(v_ref, rows, hh, cfg.dv))
                ds = p * (dp - deltas[hh]) * cfg.scale
                out.append(dqs[hh] + _dot_nn(ds.astype(BF16), ks))
            return tuple(out)

        dqs = lax.fori_loop(0, i if cfg.causal else nkb, step, tuple(jnp.zeros((t, cfg.dq), F32) for _ in range(hp)))
        if cfg.causal:
            dqs = step(i, dqs, True)
        tabs = [r[...] for r in tab_refs]
        for hh in range(hp):
            dq = dqs[hh]
            if tabs:
                dq = jnp.concatenate([dq[:, :LANES], _rope_t(dq[:, LANES:], *tabs)], axis=1)
            dq_ref[:, hh * cfg.dq:(hh + 1) * cfg.dq] = dq.astype(dq_ref.dtype)

    ov = pl.BlockSpec((t, hp * cfg.dv), lambda h, i: (i, h))
    row1 = pl.BlockSpec((hp, t, 1), lambda h, i: (h, i, 0))
    tab_specs = [pl.BlockSpec((t, LANES), lambda h, i: (i, 0))] * n_tab
    args = [q, k1, v] + ([k2] if has_k2 else []) + list(rope_tabs or ()) + [o, do, lse]
    return pl.pallas_call(
        body, name=name, grid=(cfg.H // hp, Tq // t),
        in_specs=_attn_specs(cfg, hp, t, Tk, has_k2, True) + tab_specs + [ov, ov, row1],
        out_specs=[pl.BlockSpec((t, hp * cfg.dq), lambda h, i: (i, h)), row1],
        out_shape=[jax.ShapeDtypeStruct((Tq, cfg.H * cfg.dq), out_dtype), jax.ShapeDtypeStruct((cfg.H, Tq, 1), F32)],
        compiler_params=_cparams(("parallel", "parallel")),
    )(*args)


def flash_dkv(cfg, q, k1, v, k2, do, lse, delta, out_dtype, name):
    Tq, Tk = q.shape[0], k1.shape[0]
    tq, t = _blocks(cfg, Tq, Tk)
    nqb = Tq // tq
    has_k2 = k2 is not None
    hp = cfg.hp_kv
    assert hp == cfg.H
    v0 = cfg.H * cfg.dk1

    def body(*refs):
        q_ref, k1_ref, v_ref = refs[:3]
        k2_ref = refs[3] if has_k2 else None
        n_in = 4 if has_k2 else 3
        do_ref, lse_ref, dl_ref = refs[n_in:n_in + 3]
        dkv_ref = refs[n_in + 3]
        j, h = pl.program_id(0), pl.program_id(1)
        kss = [_keys(cfg, k1_ref, k2_ref, slice(None), hh) for hh in range(hp)]
        vss = [_cols(v_ref, slice(None), hh, cfg.dv) for hh in range(hp)]

        def step(i, carry, diagonal=False):
            rows = pl.ds(pl.multiple_of(i * tq, tq), tq)
            out = []
            for hh in range(hp):
                dk, dv = carry[hh]
                qi, doi = _cols(q_ref, rows, hh, cfg.dq), _cols(do_ref, rows, hh, cfg.dv)
                s = _dot_nt(kss[hh], qi) * cfg.scale
                if diagonal:
                    s = jnp.where(lax.broadcasted_iota(jnp.int32, s.shape, 0) <= lax.broadcasted_iota(jnp.int32, s.shape, 1), s, NEG)
                p = jnp.exp(s - lse_ref[hh, :, rows])
                dv = dv + _dot_nn(p.astype(BF16), doi)
                ds = p * (_dot_nt(vss[hh], doi) - dl_ref[hh, :, rows]) * cfg.scale
                dk = dk + _dot_nn(ds.astype(BF16), qi)
                out.append((dk, dv))
            return tuple(out)

        init = tuple((jnp.zeros((t, cfg.dq), F32), jnp.zeros((t, cfg.dv), F32)) for _ in range(hp))
        if cfg.causal:
            res = lax.fori_loop(j + 1, nqb, step, step(j, init, True))
        else:
            res = lax.fori_loop(0, nqb, step, init)
        for hh in range(hp):
            dk, dv = res[hh]
            dkv_ref[:, hh * cfg.dk1:(hh + 1) * cfg.dk1] = dk[:, 0:cfg.dk1].astype(dkv_ref.dtype)
            dkv_ref[:, v0 + hh * cfg.dv:v0 + (hh + 1) * cfg.dv] = dv.astype(dkv_ref.dtype)
        if has_k2:
            dk2_ref = refs[n_in + 4]

            @pl.when(h == 0)
            def _():
                dk2_ref[...] = jnp.zeros_like(dk2_ref)

            for hh in range(hp):
                dk2_ref[...] += res[hh][0][:, cfg.dk1:]

    specs = _attn_specs(cfg, hp, t, Tk, has_k2, False)
    specs[0] = pl.BlockSpec((Tq, hp * cfg.dq), lambda j, h: (0, h), pipeline_mode=_ONE_BUFFER)
    rows_all = pl.BlockSpec((hp, 1, Tq), lambda j, h: (h, 0, 0), pipeline_mode=_ONE_BUFFER)
    specs += [pl.BlockSpec((Tq, hp * cfg.dv), lambda j, h: (0, h), pipeline_mode=_ONE_BUFFER), rows_all, rows_all]
    args = [q, k1, v] + ([k2] if has_k2 else []) + [do, lse.reshape(cfg.H, 1, Tq), delta.reshape(cfg.H, 1, Tq)]
    out_specs = [pl.BlockSpec((t, v0 + cfg.H * cfg.dv), lambda j, h: (j, 0))]
    out_shape = [jax.ShapeDtypeStruct((Tk, v0 + cfg.H * cfg.dv), out_dtype)]
    if has_k2:
        out_specs.append(pl.BlockSpec((t, LANES), lambda j, h: (j, 0)))
        out_shape.append(jax.ShapeDtypeStruct((Tk, LANES), F32))
    return pl.pallas_call(
        body, name=name, grid=(Tk // t, cfg.H // hp), in_specs=specs, out_specs=out_specs, out_shape=out_shape,
        compiler_params=_cparams(("parallel", "arbitrary")),
    )(*args)


def _shift_down(x, s):
    if s == 0:
        return x
    t = lax.broadcasted_iota(jnp.int32, x.shape, 0)
    return jnp.where(t >= s, pltpu.roll(x, s, 0), 0.0)


def _shift_up(x, s):
    if s == 0:
        return x
    n = x.shape[0]
    t = lax.broadcasted_iota(jnp.int32, x.shape, 0)
    return jnp.where(t < n - s, pltpu.roll(x, n - s, 0), 0.0)


def _conv(x, w_ref, kw):
    y = x * w_ref[kw - 1:kw, :]
    for j in range(kw - 1):
        y = y + _shift_down(x, kw - 1 - j) * w_ref[j:j + 1, :]
    return y


def _conv_t(d, w_ref, kw):
    y = d * w_ref[kw - 1:kw, :]
    for j in range(kw - 1):
        y = y + _shift_up(d, kw - 1 - j) * w_ref[j:j + 1, :]
    return y


def _conv_dw(d, x, kw):
    rows = lax.broadcasted_iota(jnp.int32, (kw, d.shape[1]), 0)
    dw = jnp.zeros((kw, d.shape[1]), F32)
    for j in range(kw):
        r = jnp.sum(d * _shift_down(x, kw - 1 - j), axis=0, keepdims=True)
        dw = jnp.where(rows == j, r, dw)
    return dw


def _silu(x):
    return x * jax.nn.sigmoid(x)


def _silu_grad(x):
    s = jax.nn.sigmoid(x)
    return s * (1.0 + x * (1.0 - s))


def gdn_conv_fwd(z, w, name, tc=256):
    T, C = z.shape[0], w.shape[1]
    kw = w.shape[0]

    def body(x_ref, w_ref, o_ref):
        o_ref[...] = _silu(_conv(x_ref[...], w_ref, kw))

    return pl.pallas_call(
        body, name=name, grid=(C // tc,),
        in_specs=[pl.BlockSpec((T, tc), lambda j: (0, j)), pl.BlockSpec((kw, tc), lambda j: (0, j))],
        out_specs=pl.BlockSpec((T, tc), lambda j: (0, j)),
        out_shape=jax.ShapeDtypeStruct((T, C), F32), compiler_params=_cparams(("parallel",)),
    )(z, w)


def gdn_conv_bwd(z, w, dy, name, tc=256):
    T, C = z.shape[0], w.shape[1]
    kw = w.shape[0]

    def body(x_ref, w_ref, dy_ref, dx_ref, dw_ref):
        xv = x_ref[...]
        dc = dy_ref[...] * _silu_grad(_conv(xv, w_ref, kw))
        dx_ref[...] = _conv_t(dc, w_ref, kw).astype(dx_ref.dtype)
        dw_ref[...] = _conv_dw(dc, xv, kw)

    col = lambda j: (0, j)
    return pl.pallas_call(
        body, name=name, grid=(C // tc,),
        in_specs=[pl.BlockSpec((T, tc), col), pl.BlockSpec((kw, tc), col), pl.BlockSpec((T, tc), col)],
        out_specs=[pl.BlockSpec((T, tc), col), pl.BlockSpec((kw, tc), col)],
        out_shape=[jax.ShapeDtypeStruct((T, C), BF16), jax.ShapeDtypeStruct((kw, C), F32)],
        compiler_params=_cparams(("parallel",)),
    )(z, w, dy)


def sc_fwd(z, w, name, tc=256):
    T, C = z.shape[0], w.shape[1]
    kw, nb = w.shape[0], C // tc

    def body(b_ref, c_ref, u_ref, w_ref, o_ref):
        o_ref[...] = (b_ref[...] * _conv(c_ref[...] * u_ref[...], w_ref, kw)).astype(o_ref.dtype)

    return pl.pallas_call(
        body, name=name, grid=(nb,),
        in_specs=[pl.BlockSpec((T, tc), lambda j: (0, j)), pl.BlockSpec((T, tc), lambda j: (0, nb + j)),
                  pl.BlockSpec((T, tc), lambda j: (0, 2 * nb + j)), pl.BlockSpec((kw, tc), lambda j: (0, j))],
        out_specs=pl.BlockSpec((T, tc), lambda j: (0, j)),
        out_shape=jax.ShapeDtypeStruct((T, C), BF16), compiler_params=_cparams(("parallel",)),
    )(z, z, z, w)


def sc_bwd(z, w, dy, name, tc=256):
    T, C = z.shape[0], w.shape[1]
    kw, nb = w.shape[0], C // tc

    def body(b_ref, c_ref, u_ref, w_ref, dy_ref, db_ref, dc_ref, du_ref, dw_ref):
        cv, uv, dyv = c_ref[...], u_ref[...], dy_ref[...]
        cu = cv * uv
        db_ref[...] = (dyv * _conv(cu, w_ref, kw)).astype(db_ref.dtype)
        dcv = dyv * b_ref[...]
        dcu = _conv_t(dcv, w_ref, kw)
        dc_ref[...] = (dcu * uv).astype(dc_ref.dtype)
        du_ref[...] = (dcu * cv).astype(du_ref.dtype)
        dw_ref[...] = _conv_dw(dcv, cu, kw)

    col = lambda j: (0, j)
    act = jax.ShapeDtypeStruct((T, C), BF16)
    return pl.pallas_call(
        body, name=name, grid=(nb,),
        in_specs=[pl.BlockSpec((T, tc), col), pl.BlockSpec((T, tc), lambda j: (0, nb + j)),
                  pl.BlockSpec((T, tc), lambda j: (0, 2 * nb + j)), pl.BlockSpec((kw, tc), col), pl.BlockSpec((T, tc), col)],
        out_specs=[pl.BlockSpec((T, tc), col)] * 3 + [pl.BlockSpec((kw, tc), col)],
        out_shape=[act, act, act, jax.ShapeDtypeStruct((kw, C), F32)],
        compiler_params=_cparams(("parallel",)),
    )(z, z, z, w, dy)


def _hdot(a, b, dims):
    a_hi, b_hi = a.astype(BF16), b.astype(BF16)
    a_lo, b_lo = (a - a_hi.astype(F32)).astype(BF16), (b - b_hi.astype(F32)).astype(BF16)
    dot = lambda x, y: lax.dot_general(x, y, (dims, ((), ())), preferred_element_type=F32)
    return dot(a_hi, b_hi) + (dot(a_hi, b_lo) + dot(a_lo, b_hi))


def _bdot(a, b, dims):
    return lax.dot_general(a.astype(BF16), b.astype(BF16), (dims, ((), ())), preferred_element_type=F32)


_NN, _NT, _TN = ((1,), (0,)), ((1,), (1,)), ((0,), (0,))


def _per_head_dots(dot2d):
    def stacked(a, b, dims):
        return jnp.stack([dot2d(a[h], b[h], dims) for h in range(a.shape[0])])

    @jax.custom_vjp
    def nn(a, b):
        return stacked(a, b, _NN)

    @jax.custom_vjp
    def nt(a, b):
        return stacked(a, b, _NT)

    @jax.custom_vjp
    def tn(a, b):
        return stacked(a, b, _TN)

    nn.defvjp(lambda a, b: (nn(a, b), (a, b)), lambda r, d: (stacked(d, r[1], _NT), stacked(r[0], d, _TN)))
    nt.defvjp(lambda a, b: (nt(a, b), (a, b)), lambda r, d: (stacked(d, r[1], _NN), stacked(d, r[0], _TN)))
    tn.defvjp(lambda a, b: (tn(a, b), (a, b)), lambda r, d: (stacked(r[1], d, _NT), stacked(r[0], d, _NN)))
    return nn, nt, tn


_hnn, _hnt, _htn = _per_head_dots(_hdot)
_bnn, _bnt, _btn = _per_head_dots(_bdot)


@jax.custom_vjp
def _unit_lower_inverse(m):
    c = m.shape[-1]
    eye = (lax.broadcasted_iota(jnp.int32, (c, c), 0) == lax.broadcasted_iota(jnp.int32, (c, c), 1)).astype(F32)
    t = eye - m
    p = _hnn(m, m)
    n = 2
    while n < c:
        t = t + _hnn(t, p)
        n *= 2
        if n < c:
            p = _hnn(p, p)
    return t


def _uli_fwd(m):
    t = _unit_lower_inverse(m)
    return t, t


def _uli_bwd(t, dt):
    return (-_htn(t, _hnt(dt, t)),)


_unit_lower_inverse.defvjp(_uli_fwd, _uli_bwd)


@jax.custom_vjp
def _known_inverse(m, t):
    return t


_known_inverse.defvjp(lambda m, t: (t, t), lambda t, dt: (_uli_bwd(t, dt)[0], jnp.zeros_like(t)))


def _gdn_chunk(q, k, v, gate, bl, al, a_log, dt_bias, o_norm, st, t_known=None):
    nh, c = q.shape[0], q.shape[1]
    ii = lax.broadcasted_iota(jnp.int32, (c, c), 0)
    jj = lax.broadcasted_iota(jnp.int32, (c, c), 1)
    tri, strict = ii >= jj, ii > jj
    q = q * lax.rsqrt(jnp.sum(q * q, -1, keepdims=True) + EPS) * (GDN_D ** -0.5)
    k = k * lax.rsqrt(jnp.sum(k * k, -1, keepdims=True) + EPS)
    beta = jax.nn.sigmoid(bl)
    g = -jnp.exp(a_log) * jax.nn.softplus(al + dt_bias)
    gc = _hnn(jnp.broadcast_to(tri.astype(F32), (nh, c, c)), g)
    gcol = _hnn(gc, jnp.full((nh, LANES, c), 1.0 / LANES, F32))
    grow = _hnt(jnp.full((nh, c, LANES), 1.0 / LANES, F32), gc)
    decay = jnp.where(tri, jnp.exp(jnp.where(tri, gcol - grow, 0.0)), 0.0)
    kb = k * beta
    m = jnp.where(strict, _bnt(kb, k) * decay, 0.0)
    t_inv = _unit_lower_inverse(m) if t_known is None else _known_inverse(m, t_known)
    eg = jnp.exp(gc)
    u = _bnn(t_inv, v * beta)
    w = _bnn(t_inv, kb * eg)
    attn = _bnt(q, k) * decay
    v_new = u - _bnn(w, st)
    o = _bnn(q * eg, st) + _bnn(attn, v_new)
    g_last = jnp.sum(g, axis=1, keepdims=True)
    st_new = st * jnp.exp(g_last) + _btn(k * jnp.exp(g_last - gc), v_new)
    o = o * lax.rsqrt(jnp.mean(o * o, -1, keepdims=True) + EPS) * o_norm
    return o * _silu(gate), st_new, t_inv


GDN_HP = 8
_GW = GDN_HP * GDN_D
_GB = GDN_H // GDN_HP


def _gdn_specs(n_chunks, rev):
    def tok(col):
        if rev:
            return pl.BlockSpec((GDN_C, _GW), lambda h, n: (n_chunks - 1 - n, col + h))
        return pl.BlockSpec((GDN_C, _GW), lambda h, n: (n, col + h))
    par = pl.BlockSpec((1, _GW), lambda h, n: (0, h))
    shared = pl.BlockSpec((1, GDN_D), lambda h, n: (0, 0))
    if rev:
        st = pl.BlockSpec((GDN_HP, None, GDN_D, GDN_D), lambda h, n: (h, n_chunks - 1 - n, 0, 0))
    else:
        st = pl.BlockSpec((GDN_HP, None, GDN_D, GDN_D), lambda h, n: (h, n, 0, 0))
    return tok, par, shared, st


def _heads(ref):
    return jnp.stack([ref[:, h * GDN_D:(h + 1) * GDN_D] for h in range(ref.shape[1] // GDN_D)])


def gdn_chunk_fwd(qkv, z, a_log_x, dt_bias_x, o_norm, name):
    T = qkv.shape[0]
    n_chunks = T // GDN_C
    H = GDN_H
    tok, par, shared, st_spec = _gdn_specs(n_chunks, False)

    def body(q_ref, k_ref, v_ref, g_ref, bl_ref, al_ref, a_ref, dt_ref, on_ref, o_ref, st_ref, ti_ref, state):
        @pl.when(pl.program_id(1) == 0)
        def _():
            state[...] = jnp.zeros_like(state)

        st = state[...]
        st_ref[...] = st
        o, st_new, t_inv = _gdn_chunk(_heads(q_ref), _heads(k_ref), _heads(v_ref), _heads(g_ref), _heads(bl_ref), _heads(al_ref),
                                      _heads(a_ref), _heads(dt_ref), on_ref[...], st)
        for hh in range(GDN_HP):
            o_ref[:, hh * GDN_D:(hh + 1) * GDN_D] = o[hh].astype(o_ref.dtype)
        ti_ref[...] = t_inv
        state[...] = st_new

    B = _GB
    return pl.pallas_call(
        body, name=name, grid=(B, n_chunks),
        in_specs=[tok(0), tok(B), tok(2 * B), tok(3 * B), tok(4 * B), tok(5 * B), par, par, shared],
        out_specs=[tok(0), st_spec, pl.BlockSpec((GDN_HP, None, GDN_C, GDN_C), lambda h, n: (h, n, 0, 0))],
        out_shape=[jax.ShapeDtypeStruct((T, H * GDN_D), BF16), jax.ShapeDtypeStruct((H, n_chunks, GDN_D, GDN_D), F32),
                   jax.ShapeDtypeStruct((H, n_chunks, GDN_C, GDN_C), F32)],
        scratch_shapes=[pltpu.VMEM((GDN_HP, GDN_D, GDN_D), F32)],
        compiler_params=_cparams(("parallel", "arbitrary")),
    )(qkv, qkv, qkv, z, z, z, a_log_x, dt_bias_x, o_norm)


def gdn_chunk_bwd(qkv, z, a_log_x, dt_bias_x, o_norm, states, t_invs, do, name):
    T = qkv.shape[0]
    n_chunks = T // GDN_C
    H = GDN_H
    tok, par, shared, st_spec = _gdn_specs(n_chunks, True)

    def body(q_ref, k_ref, v_ref, g_ref, bl_ref, al_ref, a_ref, dt_ref, on_ref, st_ref, ti_ref, do_ref,
             dqkv_ref, dg_ref, dba_ref, da_ref, ddt_ref, don_ref, dstate):
        h, n = pl.program_id(0), pl.program_id(1)

        @pl.when(n == 0)
        def _():
            dstate[...] = jnp.zeros_like(dstate)
            da_ref[...] = jnp.zeros_like(da_ref)
            ddt_ref[...] = jnp.zeros_like(ddt_ref)

        @pl.when((n == 0) & (h == 0))
        def _():
            don_ref[...] = jnp.zeros_like(don_ref)

        t_known = ti_ref[...]
        _, vjp = jax.vjp(lambda *ins: _gdn_chunk(*ins, t_known=t_known)[:2],
                         _heads(q_ref), _heads(k_ref), _heads(v_ref), _heads(g_ref), _heads(bl_ref), _heads(al_ref),
                         _heads(a_ref), _heads(dt_ref), on_ref[...], st_ref[...])
        dq, dk, dv, dg, dbl, dal, da, ddt, don, dst = vjp((_heads(do_ref).astype(F32), dstate[...]))
        lane = lax.broadcasted_iota(jnp.int32, (GDN_C, LANES), 1)
        dba = jnp.zeros((GDN_C, LANES), F32)
        for hh in range(GDN_HP):
            cols = slice(hh * GDN_D, (hh + 1) * GDN_D)
            for part, d in enumerate((dq, dk, dv)):
                dqkv_ref[:, part * H * GDN_D + hh * GDN_D:part * H * GDN_D + (hh + 1) * GDN_D] = d[hh]
            dg_ref[:, cols] = dg[hh].astype(dg_ref.dtype)
            dba = jnp.where(lane == hh, jnp.sum(dbl[hh], axis=-1, keepdims=True), dba)
            dba = jnp.where(lane == H + hh, jnp.sum(dal[hh], axis=-1, keepdims=True), dba)
            da_ref[:, cols] += da[hh]
            ddt_ref[:, cols] += ddt[hh]
        dba_ref[...] = dba.astype(dba_ref.dtype)
        don_ref[...] += don
        dstate[...] = dst

    tok0 = tok(0)
    B = _GB
    assert B == 1
    bf_tok = jax.ShapeDtypeStruct((T, H * GDN_D), BF16)
    par_sh = jax.ShapeDtypeStruct((1, H * GDN_D), F32)
    return pl.pallas_call(
        body, name=name, grid=(B, n_chunks),
        in_specs=[tok(0), tok(B), tok(2 * B), tok(3 * B), tok(4 * B), tok(5 * B), par, par, shared, st_spec,
                  pl.BlockSpec((GDN_HP, None, GDN_C, GDN_C), lambda h, n: (h, n_chunks - 1 - n, 0, 0)), tok0],
        out_specs=[pl.BlockSpec((GDN_C, 3 * H * GDN_D), lambda h, n: (n_chunks - 1 - n, 0)), tok0,
                   pl.BlockSpec((GDN_C, LANES), lambda h, n: (n_chunks - 1 - n, 0)), par, par, shared],
        out_shape=[jax.ShapeDtypeStruct((T, 3 * H * GDN_D), F32), bf_tok, jax.ShapeDtypeStruct((T, LANES), BF16), par_sh, par_sh,
                   jax.ShapeDtypeStruct((1, GDN_D), F32)],
        scratch_shapes=[pltpu.VMEM((GDN_HP, GDN_D, GDN_D), F32)],
        compiler_params=_cparams(("arbitrary", "arbitrary")),
    )(qkv, qkv, qkv, z, z, z, a_log_x, dt_bias_x, o_norm, states, t_invs, do)


def _prefetch_call(body, name, grid, in_specs, out_specs, out_shape, aliases=None):
    return pl.pallas_call(
        body, name=name,
        grid_spec=pltpu.PrefetchScalarGridSpec(num_scalar_prefetch=1, grid=grid, in_specs=in_specs, out_specs=out_specs),
        out_shape=out_shape, input_output_aliases=aliases or {},
        compiler_params=_cparams(("parallel",) * len(grid)))


def cast_into(src, src_row0, rows, slab, row0, me, name):
    width = src.shape[1]
    tr = _pick_rows(rows, 1024, row0, src_row0)
    assert rows % tr == 0 and row0 % tr == 0 and src_row0 % tr == 0

    def body(me_ref, s_ref, *refs):
        refs[-1][...] = s_ref[...].astype(refs[-1].dtype)

    in_specs = [pl.BlockSpec((tr, width), lambda r, me_ref: (src_row0 // tr + r, 0))]
    args = [src]
    aliases = {}
    if slab.arr is not None:
        in_specs.append(_ANY)
        args.append(slab.arr)
        aliases = {2: 0}
    slab.arr = _prefetch_call(
        body, name, (rows // tr,), in_specs,
        pl.BlockSpec((None, tr, width), lambda r, me_ref: (me_ref[0], row0 // tr + r, 0)),
        jax.ShapeDtypeStruct(slab.shape, slab.dtype), aliases)(me, *args)


def pair_add(g, b, c_idx, name):
    n, rh, w = b.shape
    tr = _pick_rows(rh, 1024)
    nb = rh // tr

    def body(c_ref, g_ref, b_ref, o_ref):
        o_ref[...] = (g_ref[...].astype(F32) + b_ref[...].astype(F32)).astype(o_ref.dtype)

    return _prefetch_call(
        body, name, (n, nb),
        [pl.BlockSpec((None, tr, w), lambda k, r, c: (k, c[0] * nb + r, 0)), pl.BlockSpec((None, tr, w), lambda k, r, c: (k, r, 0))],
        pl.BlockSpec((None, tr, w), lambda k, r, c: (k, r, 0)), jax.ShapeDtypeStruct(b.shape, BF16))(c_idx, g, b)


def chip_sum(p, rv, mc, name):
    n, rh, w = p.shape
    tr = _pick_rows(rh, 512)
    nb = rh // tr

    def body(mc_ref, p_ref, rv_ref, o_ref):
        me = mc_ref[0]
        acc = None
        for k in range(n):
            part = jnp.where(me == k, p_ref[...], rv_ref[k]).astype(F32)
            acc = part if acc is None else acc + part
        o_ref[...] = acc.astype(o_ref.dtype)

    return _prefetch_call(
        body, name, (nb,),
        [pl.BlockSpec((None, tr, w), lambda r, mc_ref: (mc_ref[0], r, 0)), pl.BlockSpec((n, tr, w), lambda r, mc_ref: (0, r, 0))],
        pl.BlockSpec((tr, w), lambda r, mc_ref: (mc_ref[1] * nb + r, 0)), jax.ShapeDtypeStruct((2 * rh, w), BF16))(mc, p, rv)


def adamw(red, row0, w, m, v, w_row0, rows, prev, name):
    cols = w.shape[1]
    tr = _pick_rows(rows, 512, row0, w_row0)
    assert rows % tr == 0 and row0 % tr == 0 and w_row0 % tr == 0

    def body(g_ref, w_ref, m_ref, v_ref, *refs):
        go_ref, d_ref, nm_ref, nv_ref = refs[-4:]
        gv = g_ref[...].astype(F32)
        nm = ADAM_B1 * m_ref[...] + (1.0 - ADAM_B1) * gv
        nv = ADAM_B2 * v_ref[...] + (1.0 - ADAM_B2) * (gv * gv)
        m_hat = nm / (1.0 - ADAM_B1 ** ADAM_STEP)
        v_hat = nv / (1.0 - ADAM_B2 ** ADAM_STEP)
        go_ref[...] = gv
        d_ref[...] = -ADAM_LR * (m_hat / (jnp.sqrt(v_hat) + ADAM_EPS) + ADAM_WD * w_ref[...])
        nm_ref[...] = nm
        nv_ref[...] = nv

    spec = pl.BlockSpec((tr, cols), lambda r: (w_row0 // tr + r, 0))
    sh = jax.ShapeDtypeStruct(w.shape, F32)
    in_specs = [pl.BlockSpec((tr, cols), lambda r: (row0 // tr + r, 0)), spec, spec, spec]
    args, aliases = [red, w, m, v], {}
    if prev is not None:
        in_specs += [_ANY] * 4
        args += list(prev)
        aliases = {4 + k: k for k in range(4)}
    return pl.pallas_call(
        body, name=name, grid=(rows // tr,), in_specs=in_specs, out_specs=[spec] * 4, out_shape=[sh] * 4,
        input_output_aliases=aliases, compiler_params=_cparams(("parallel",)),
    )(*args)


def _place():
    x, y, c = lax.axis_index("x"), lax.axis_index("y"), lax.axis_index("c")
    chips = [(1 - x, y), (x, 1 - y), (1 - x, 1 - y)]
    return x, y, c, chips


def _chip_index(cx, cy):
    return 2 * cx + cy


def _remote(src, dst, send_sem, recv_sem, to):
    return pltpu.make_async_remote_copy(src_ref=src, dst_ref=dst, send_sem=send_sem, recv_sem=recv_sem,
                                        device_id=to, device_id_type=MESH)


def _comm_call(body, name, ins, out_shapes, n_sems, aliases):
    return pl.pallas_call(
        body, name=name, in_specs=[_ANY] * len(ins), out_specs=[_ANY] * len(out_shapes), out_shape=out_shapes,
        scratch_shapes=[pltpu.SemaphoreType.DMA((n_sems,)), pltpu.SemaphoreType.DMA((n_sems,))],
        input_output_aliases=aliases,
    )(*ins)


def pair_swap_halves(slabs, name="grad_pair_swap"):
    n = len(slabs)

    def body(*refs):
        in_refs, out_refs, send_sems, recv_sems = refs[:n], refs[n:2 * n], refs[-2], refs[-1]
        x, y, c, _ = _place()
        cps = []
        for a in range(n):
            rh = in_refs[a].shape[1] // 2
            cp = _remote(in_refs[a].at[:, pl.ds((1 - c) * rh, rh), :], out_refs[a], send_sems.at[a], recv_sems.at[a], (x, y, 1 - c))
            cp.start()
            cps.append(cp)
        for cp in cps:
            cp.wait()

    outs = [jax.ShapeDtypeStruct((s.shape[0], s.shape[1] // 2, s.shape[2]), s.dtype) for s in slabs]
    return _comm_call(body, name, slabs, outs, n, {})


def pair_join_halves(reds, name="grad_pair_join"):
    n = len(reds)

    def body(*refs):
        in_refs, out_refs, send_sems, recv_sems = refs[:n], refs[n:2 * n], refs[-2], refs[-1]
        x, y, c, _ = _place()
        cps = []
        for a in range(n):
            rh = in_refs[a].shape[0] // 2
            mine = pl.ds(c * rh, rh)
            cp = _remote(in_refs[a].at[mine], out_refs[a].at[mine], send_sems.at[a], recv_sems.at[a], (x, y, 1 - c))
            cp.start()
            cps.append(cp)
        for a in range(n):
            rh = in_refs[a].shape[0] // 2
            got = out_refs[a].at[pl.ds((1 - c) * rh, rh)]
            _remote(got, got, send_sems.at[a], recv_sems.at[a], (x, y, 1 - c)).wait_recv()
        for cp in cps:
            cp.wait_send()

    return _comm_call(body, name, reds, [jax.ShapeDtypeStruct(r.shape, r.dtype) for r in reds], n, {a: a for a in range(n)})


_HBM = pl.BlockSpec(memory_space=pltpu.HBM)
_SEM = pl.BlockSpec(memory_space=pltpu.SEMAPHORE)
_EFFECT = pltpu.SideEffectType.DATAFLOW_SIDE_EFFECTING


def _in_hbm(a):
    return pltpu.with_memory_space_constraint(a, pltpu.HBM)


def _hbm_like(a):
    return pltpu.HBM(a.shape, a.dtype)


def _start_call(body, name, ins, n_sems, after):
    n = len(ins)
    res = pl.pallas_call(
        body, name=name, in_specs=[_HBM] * n + [_ANY],
        out_specs=[_SEM, _SEM] + [_HBM] * n + [pl.BlockSpec(memory_space=pltpu.VMEM)],
        out_shape=[pltpu.SemaphoreType.DMA((n_sems,)), pltpu.SemaphoreType.DMA((n_sems,))] + [_hbm_like(a) for a in ins]
        + [jax.ShapeDtypeStruct((8, LANES), F32)],
        input_output_aliases={a: 2 + a for a in range(n)},
        compiler_params=pltpu.CompilerParams(has_side_effects=_EFFECT),
    )(*[_in_hbm(a) for a in ins], after)
    return res[0], res[1], list(res[2:2 + n]), res[-1]


def _wait_call(body, name, thru, send_sems, recv_sems, after):
    n = len(thru)
    after = list(after) if isinstance(after, (list, tuple)) else [after]
    return pl.pallas_call(
        body, name=name, in_specs=[_HBM] * n + [_SEM, _SEM] + [_ANY] * len(after), out_specs=[_HBM] * n,
        out_shape=[_hbm_like(a) for a in thru], input_output_aliases={a: a for a in range(n)},
        compiler_params=pltpu.CompilerParams(has_side_effects=_EFFECT),
    )(*thru, send_sems, recv_sems, *after)


def gather_start(slabs, after, name="weight_gather_start"):
    n = len(slabs)

    def body(*refs):
        g_refs, send_sems, recv_sems, token = refs[:n], refs[n + 1], refs[n + 2], refs[-1]
        x, y, c, chips = _place()
        me = _chip_index(x, y)
        for a in range(n):
            rh = g_refs[a].shape[1] // 2
            mine = g_refs[a].at[me, pl.ds(c * rh, rh)]
            for j, chip in enumerate(chips):
                _remote(mine, mine, send_sems.at[3 * a + j], recv_sems.at[3 * a + j], (*chip, c)).start()
        token[...] = jnp.zeros_like(token)

    return _start_call(body, name, slabs, 3 * n, after)


def gather_wait(send_sems, recv_sems, thru, after, name="weight_gather_wait"):
    n = len(thru)

    def body(*refs):
        g_refs, send_sems, recv_sems = refs[:n], refs[n], refs[n + 1]
        x, y, c, chips = _place()
        me = _chip_index(x, y)
        for a in range(n):
            rh = g_refs[a].shape[1] // 2
            rows = pl.ds(c * rh, rh)
            for j, chip in enumerate(chips):
                mine, got = g_refs[a].at[me, rows], g_refs[a].at[_chip_index(*chip), rows]
                _remote(mine, mine, send_sems.at[3 * a + j], recv_sems.at[3 * a + j], (*chip, c)).wait_send()
                _remote(got, got, send_sems.at[3 * a + j], recv_sems.at[3 * a + j], (*chip, c)).wait_recv()

    return _wait_call(body, name, thru, send_sems, recv_sems, after)


def gather_forward(slabs, name="weight_gather_forward"):
    n = len(slabs)

    def body(*refs):
        in_refs, out_refs, send_sems, recv_sems = refs[:n], refs[n:2 * n], refs[-2], refs[-1]
        x, y, c, chips = _place()
        sib = (x, y, 1 - c)
        sends = []
        for a in range(n):
            rh = in_refs[a].shape[1] // 2
            for j, chip in enumerate(chips):
                k = _chip_index(*chip)
                cp = _remote(in_refs[a].at[k, pl.ds(c * rh, rh)], out_refs[a].at[k, pl.ds(c * rh, rh)], send_sems.at[3 * a + j],
                             recv_sems.at[3 * a + j], sib)
                cp.start()
                sends.append(cp)
        for a in range(n):
            rh = in_refs[a].shape[1] // 2
            for j, chip in enumerate(chips):
                got = out_refs[a].at[_chip_index(*chip), pl.ds((1 - c) * rh, rh)]
                _remote(got, got, send_sems.at[3 * a + j], recv_sems.at[3 * a + j], sib).wait_recv()
        for cp in sends:
            cp.wait_send()

    return _comm_call(body, name, slabs, [jax.ShapeDtypeStruct(s.shape, s.dtype) for s in slabs], 3 * n, {a: a for a in range(n)})


def exchange_start(parts, after, name="grad_exchange_start"):
    n = len(parts)

    def body(*refs):
        p_refs, land_refs, send_sems, recv_sems, token = refs[:n], refs[n:2 * n], refs[2 * n + 1], refs[2 * n + 2], refs[-1]
        x, y, c, chips = _place()
        me = _chip_index(x, y)
        for a in range(n):
            for j, chip in enumerate(chips):
                _remote(p_refs[a].at[_chip_index(*chip)], land_refs[a].at[me], send_sems.at[3 * a + j], recv_sems.at[3 * a + j],
                        (*chip, c)).start()
        token[...] = jnp.zeros_like(token)

    return _start_call(body, name, list(parts) + [lax.empty(p.shape, p.dtype) for p in parts], 3 * n, after)


def swap_start(slabs, after, name="grad_swap_start"):
    n = len(slabs)

    def body(*refs):
        g_refs, land_refs, send_sems, recv_sems, token = refs[:n], refs[n:2 * n], refs[2 * n + 1], refs[2 * n + 2], refs[-1]
        x, y, c, _ = _place()
        for a in range(n):
            rh = g_refs[a].shape[1] // 2
            _remote(g_refs[a].at[:, pl.ds((1 - c) * rh, rh), :], land_refs[a], send_sems.at[a], recv_sems.at[a], (x, y, 1 - c)).start()
        token[...] = jnp.zeros_like(token)

    lands = [lax.empty((s.shape[0], s.shape[1] // 2, s.shape[2]), s.dtype) for s in slabs]
    return _start_call(body, name, list(slabs) + lands, n, after)


def swap_wait(send_sems, recv_sems, thru, after, name="grad_swap_wait"):
    n = len(thru) // 2

    def body(*refs):
        g_refs, land_refs, send_sems, recv_sems = refs[:n], refs[n:2 * n], refs[2 * n], refs[2 * n + 1]
        x, y, c, _ = _place()
        for a in range(n):
            rh = g_refs[a].shape[1] // 2
            cp = _remote(g_refs[a].at[:, pl.ds((1 - c) * rh, rh), :], land_refs[a], send_sems.at[a], recv_sems.at[a], (x, y, 1 - c))
            cp.wait_send()
            cp.wait_recv()

    res = _wait_call(body, name, thru, send_sems, recv_sems, after)
    return res[:n], res[n:]


def exchange_wait(send_sems, recv_sems, thru, after, name="grad_exchange_wait"):
    n = len(thru) // 2

    def body(*refs):
        p_refs, land_refs, send_sems, recv_sems = refs[:n], refs[n:2 * n], refs[2 * n], refs[2 * n + 1]
        x, y, c, chips = _place()
        me = _chip_index(x, y)
        for a in range(n):
            for j, chip in enumerate(chips):
                k = _chip_index(*chip)
                _remote(p_refs[a].at[k], land_refs[a].at[me], send_sems.at[3 * a + j], recv_sems.at[3 * a + j], (*chip, c)).wait_send()
                _remote(land_refs[a].at[k], land_refs[a].at[k], send_sems.at[3 * a + j], recv_sems.at[3 * a + j], (*chip, c)).wait_recv()

    res = _wait_call(body, name, thru, send_sems, recv_sems, after)
    return res[:n], res[n:]


_SLABS = {
    "mla_w_in": [("mla_w_in", 1, 0, 2)], "mla_w_uq": [("mla_w_uq", 2, 0, 2)], "mla_w_ukv": [("mla_w_ukv", 2, 0, 2)],
    "l0_mla_w_o": [("mla_w_o", 1, 0, 1)],
    "l0_w1024": [("mlp_w1", 2, 0, 1), ("mlp_w2", 1, 0, 1), ("xa_w_q", 1, 0, 1), ("xa_w_o", 1, 0, 1)],
    "l0_xa_w_kv": [("xa_w_kv", 2, 0, 1)],
    "l1_w1024": [("mlp_w1", 2, 1, 2), ("mlp_w2", 1, 1, 2), ("xa_w_q", 1, 1, 2), ("xa_w_o", 1, 1, 2), ("gdn_w_o", 1, 0, 1)],
    "l1_xa_w_kv": [("xa_w_kv", 2, 1, 2)], "gdn_w_in": [("gdn_w_in", 2, 0, 1)],
    "l23_w1024": [("mlp_w1", 2, 2, 4), ("mlp_w2", 1, 2, 4), ("xa_w_q", 1, 2, 4), ("xa_w_o", 1, 2, 4), ("mla_w_o", 1, 1, 2),
                  ("sc_w_o", 1, 0, 1)],
    "l23_xa_w_kv": [("xa_w_kv", 2, 2, 4)], "sc_w_in": [("sc_w_in", 2, 0, 1)],
}
_GROUPS = [(["mla_w_in", "mla_w_uq", "mla_w_ukv", "l0_mla_w_o"], None),
           (["l0_w1024", "l0_xa_w_kv"], (0, "xa")),
           (["l1_w1024", "l1_xa_w_kv", "gdn_w_in"], (1, "mix")),
           (["l23_w1024", "l23_xa_w_kv", "sc_w_in"], (2, "mix"))]
_SWAP_DONE = {(2, "mix"): (1, "mlp"), (1, "mix"): (0, "mlp")}
_RELAID = ("mla_w_in", "mla_w_uq", "mla_w_ukv", "gdn_w_in")
_SMALL = [("mla_q_norm", 1), ("mla_kv_norm", 1), ("gdn_conv_w", 2), ("sc_conv_w", 2)]
_REPL = ["gdn_a_log", "gdn_dt_bias", "gdn_o_norm", "norm_mix", "norm_mem", "norm_mlp", "mem_norm", "final_norm"]
_WEIGHTS = ['mla_w_in', 'mla_q_norm', 'mla_kv_norm', 'mla_w_uq', 'mla_w_ukv', 'mla_w_o', 'gdn_w_in', 'gdn_conv_w',
            'gdn_a_log', 'gdn_dt_bias', 'gdn_o_norm', 'gdn_w_o', 'sc_w_in', 'sc_conv_w', 'sc_w_o', 'norm_mix',
            'norm_mem', 'norm_mlp', 'xa_w_q', 'xa_w_kv', 'xa_w_o', 'mlp_w1', 'mlp_w2', 'mem_norm', 'final_norm']


class Layout:
    def __init__(self, shard_shapes):
        self.members, self.where, self.slab_dims = {}, {}, {}
        for slab, members in _SLABS.items():
            off, rows = 0, []
            for name, axis, l0, l1 in members:
                _, rpl, width = shard_shapes[name]
                rows.append((name, off, l0, l1, rpl))
                for layer in range(l0, l1):
                    self.where[(name, layer)] = (slab, off + (layer - l0) * rpl, rpl, width, axis)
                off += (l1 - l0) * rpl
            self.members[slab], self.slab_dims[slab] = rows, (off, width)

    def new_slabs(self, dtype):
        return {s: Slab(rows, width, dtype) for s, (rows, width) in self.slab_dims.items()}

    def loc(self, slabs, name, layer):
        slab, row0, rpl, width, axis = self.where[(name, layer)]
        if axis == 1:
            return Loc(slabs[slab], row0, N_CHIPS * rpl, width, 0)
        return Loc(slabs[slab], row0, rpl, N_CHIPS * width, 1)

    def _whole(self, name):
        (member,) = self.members[name]
        _, off, l0, l1, rpl = member
        assert off == 0 and l0 == 0
        return l1, rpl, self.slab_dims[name][1], dict((n, a) for n, a, _, _ in _SLABS[name])[name]

    def full(self, slabs, name):
        layers, rpl, width, axis = self._whole(name)
        blocks = slabs[name].arr.reshape(N_CHIPS, layers, rpl, width)
        return jnp.concatenate([blocks[s] for s in range(N_CHIPS)], axis=axis)

    def put_full(self, slabs, name, grad):
        layers, rpl, width, axis = self._whole(name)
        parts = jnp.stack(jnp.split(grad, N_CHIPS, axis=axis)).reshape(N_CHIPS, layers * rpl, width)
        slabs[name].arr = parts.astype(slabs[name].dtype)


def _small_pack(vals, names):
    flat = jnp.concatenate([vals[n].astype(F32).reshape(-1) for n in names])
    return jnp.pad(flat, (0, SMALL_ROWS * SMALL_COLS - flat.shape[0])).reshape(SMALL_ROWS, SMALL_COLS)


def _small_unpack(flat, like, names):
    out, off = {}, 0
    flat = flat.reshape(-1)
    for n in names:
        out[n] = flat[off:off + like[n].size].reshape(like[n].shape)
        off += like[n].size
    return out


_MLA_CFG = _Attn(MLA_H, 2 * LANES, MLA_NOPE, MLA_V, True, (MLA_NOPE + MLA_ROPE) ** -0.5, hp=8, hp_kv=8, blk=512)
_XA_CFG = _Attn(XA_H, XA_D, XA_D, XA_D, False, XA_D ** -0.5, hp=4, hp_kv=4, blk=1024)


def _mla_weights(w_in, w_uq, w_ukv):
    w_in_p = jnp.pad(w_in, ((0, 0), (0, MLA_ZPAD - w_in.shape[1])))
    w_uq_p = jnp.pad(w_uq.reshape(MLA_QR, MLA_H, MLA_NOPE + MLA_ROPE), ((0, 0), (0, 0), (0, 2 * LANES - MLA_NOPE - MLA_ROPE)))
    w_uq_p = w_uq_p.reshape(MLA_QR, MLA_H * 2 * LANES)
    kv = w_ukv.reshape(MLA_KVR, MLA_H, MLA_NOPE + MLA_V)
    w_ukv_p = jnp.concatenate([kv[:, :, :MLA_NOPE].reshape(MLA_KVR, -1), kv[:, :, MLA_NOPE:].reshape(MLA_KVR, -1)], axis=1)
    return w_in_p, w_uq_p, w_ukv_p


def _mla_weight_grads(d_in_p, d_uq_p, d_ukv_p):
    d_in = d_in_p[:, :MLA_QR + MLA_KVR + MLA_ROPE]
    d_uq = d_uq_p.reshape(MLA_QR, MLA_H, 2 * LANES)[:, :, :MLA_NOPE + MLA_ROPE].reshape(MLA_QR, -1)
    half = MLA_H * MLA_NOPE
    d_ukv = jnp.concatenate([d_ukv_p[:, :half].reshape(MLA_KVR, MLA_H, MLA_NOPE),
                             d_ukv_p[:, half:].reshape(MLA_KVR, MLA_H, MLA_V)], axis=2).reshape(MLA_KVR, -1)
    return d_in, d_uq, d_ukv


def _mla_fwd(xs, h, wts, w_o, qn, kvn, tabs, g_next, tag):
    w_in_p, w_uq_p, w_ukv_p = wts
    z = mm(h, w_in_p, "nn", f"{tag}_in")
    cq, ckv, kr = mla_mid_fwd(z, qn, kvn, tabs, f"{tag}_mid")
    q = mm(cq, w_uq_p, "nn", f"{tag}_uq", outs=(BF16,), epi=_epi_rope_q, per_row=tabs, tm=512)
    kv = mm(ckv, w_ukv_p, "nn", f"{tag}_ukv", outs=(BF16,))
    o, lse = flash_fwd(_MLA_CFG, q, kv, kv, kr, f"{tag}_attn")
    xs, h_next = residual_norm(o, w_o, xs, g_next, f"{tag}_out")
    return xs, h_next, (z, cq, ckv, kr, q, kv, o, lse)


def _mla_bwd(dx, h, wts, w_o, g_wo, qn, kvn, tabs, saved, tag):
    w_in_p, w_uq_p, w_ukv_p = wts
    z, cq, ckv, kr, q, kv, o, lse = saved
    mm(o, dx, "tn", f"{tag}_dwo", outs=(BF16,), out_loc=g_wo)
    do = mm(dx, w_o, "nt", f"{tag}_do", outs=(BF16,))
    dqp, delta = flash_dq(_MLA_CFG, q, kv, kv, kr, o, do, lse, BF16, f"{tag}_attn_dq", rope_tabs=tabs)
    dkv, dkr = flash_dkv(_MLA_CFG, q, kv, kv, kr, do, lse, delta, BF16, f"{tag}_attn_dkv")
    d_uq_p = mm(cq, dqp, "tn", f"{tag}_duq")
    dcq = mm(dqp, w_uq_p, "nt", f"{tag}_dcq")
    d_ukv_p = mm(ckv, dkv, "tn", f"{tag}_dukv")
    dckv = mm(dkv, w_ukv_p, "nt", f"{tag}_dckv")
    dz, dqn, dkvn = mla_mid_bwd(z, qn, kvn, tabs, dcq, dckv, dkr, f"{tag}_mid_bwd")
    d_in_p = mm(h, dz, "tn", f"{tag}_din")
    dh = (dz, w_in_p)
    d_in, d_uq, d_ukv = _mla_weight_grads(d_in_p, d_uq_p, d_ukv_p)
    return dh, dict(mla_w_in=d_in, mla_w_uq=d_uq, mla_w_ukv=d_ukv, mla_q_norm=dqn, mla_kv_norm=dkvn)


_GDN_QKV = 3 * GDN_H * GDN_D
_GDN_GATE_END = _GDN_QKV + GDN_H * GDN_D


def _gdn_weights(w_in):
    rep = lambda cols: jnp.repeat(cols, GDN_D, axis=1)
    return jnp.concatenate([w_in[:, :_GDN_GATE_END], rep(w_in[:, _GDN_GATE_END:_GDN_GATE_END + GDN_H]),
                            rep(w_in[:, _GDN_GATE_END + GDN_H:])], axis=1)


def _fold(x):
    return x.reshape(x.shape[0], -1, GDN_D).sum(-1)


def _gdn_fwd(xs, h, w_in_x, conv_w, a_log, dt_bias, o_norm, w_o, g_next, tag):
    z = mm(h, w_in_x, "nn", f"{tag}_in")
    qkv = gdn_conv_fwd(z, conv_w, f"{tag}_conv")
    a_x, dt_x = jnp.repeat(a_log.reshape(1, -1), GDN_D, axis=1), jnp.repeat(dt_bias.reshape(1, -1), GDN_D, axis=1)
    og, states, t_invs = gdn_chunk_fwd(qkv, z, a_x, dt_x, o_norm.reshape(1, -1), f"{tag}_chunks")
    xs, h_next = residual_norm(og, w_o, xs, g_next, f"{tag}_out")
    return xs, h_next, (z, qkv, a_x, dt_x, og, states, t_invs)


def _gdn_weights_compact(w_in):
    return jnp.pad(w_in, ((0, 0), (0, LANES - 2 * GDN_H)))


def _gdn_bwd(dx, h, w_in_c, conv_w, o_norm, w_o, g_wo, saved, tag):
    z, qkv, a_x, dt_x, og, states, t_invs = saved
    mm(og, dx, "tn", f"{tag}_dwo", outs=(BF16,), out_loc=g_wo)
    dog = mm(dx, w_o, "nt", f"{tag}_dog")
    dqkv, dgate, dba, da_x, ddt_x, don = gdn_chunk_bwd(qkv, z, a_x, dt_x, o_norm.reshape(1, -1), states, t_invs, dog,
                                                       f"{tag}_chunks_bwd")
    dpre, dconv = gdn_conv_bwd(z, conv_w, dqkv, f"{tag}_conv_bwd")
    dz = jnp.concatenate([dpre, dgate, dba], axis=1)
    d_in_c = mm(h, dz, "tn", f"{tag}_din")
    dh = (dz, w_in_c)
    return dh, dict(gdn_w_in=d_in_c[:, :_GDN_GATE_END + 2 * GDN_H], gdn_conv_w=dconv, gdn_a_log=_fold(da_x).reshape(-1),
                    gdn_dt_bias=_fold(ddt_x).reshape(-1), gdn_o_norm=don.reshape(-1))


def _sc_fwd(xs, h, w_in, conv_w, w_o, g_next, tag):
    z = mm(h, w_in, "nn", f"{tag}_in")
    y = sc_fwd(z, conv_w, f"{tag}_conv")
    xs, h_next = residual_norm(y, w_o, xs, g_next, f"{tag}_out")
    return xs, h_next, (z, y)


def _sc_bwd(dx, h, w_in, g_win, conv_w, w_o, g_wo, saved, tag):
    z, y = saved
    mm(y, dx, "tn", f"{tag}_dwo", outs=(BF16,), out_loc=g_wo)
    dy = mm(dx, w_o, "nt", f"{tag}_dy")
    db, dc, du, dconv = sc_bwd(z, conv_w, dy, f"{tag}_conv_bwd")
    dz = jnp.concatenate([db, dc, du], axis=1)
    mm(h, dz, "tn", f"{tag}_din", outs=(BF16,), out_loc=g_win)
    dh = (dz, w_in)
    return dh, dict(sc_conv_w=dconv)


def local_step(x, mem, pos, target, lay, wslabs, gslabs, small, before=None, after_bwd=None):
    depth = small["norm_mix"].shape[0]
    W = lambda name, layer: lay.loc(wslabs, name, layer)
    G = lambda name, layer: lay.loc(gslabs, name, layer)
    tabs = rope_tables(pos)
    mem_n = rmsnorm_fwd(mem, small["mem_norm"], "mem_norm")
    full = {n: lay.full(wslabs, n) for n in ("mla_w_in", "mla_w_uq", "mla_w_ukv")}
    mla_w = [_mla_weights(full["mla_w_in"][j], full["mla_w_uq"][j], full["mla_w_ukv"][j]) for j in range(full["mla_w_in"].shape[0])]
    gdn_in_x, gdn_in_c = {}, {}

    xs, h_pre = x, None
    saved = []
    for i in range(depth):
        j, kind = i // 3, i % 3
        tag = f"l{i}"
        if before is not None:
            xs = before(i, "mix", xs)
        if kind == 1:
            gdn_full = lay.full(wslabs, "gdn_w_in")[j]
            gdn_in_x[j], gdn_in_c[j] = _gdn_weights(gdn_full), _gdn_weights_compact(gdn_full)
        x_a = xs
        h = h_pre if h_pre is not None else rmsnorm_fwd(xs, small["norm_mix"][i], f"{tag}_norm_mix")
        g_mem = small["norm_mem"][i]
        if kind == 0:
            xs, hn, mix = _mla_fwd(xs, h, mla_w[j], W("mla_w_o", j), small["mla_q_norm"][j], small["mla_kv_norm"][j], tabs, g_mem,
                                   f"{tag}_mla")
        elif kind == 1:
            xs, hn, mix = _gdn_fwd(xs, h, gdn_in_x[j], small["gdn_conv_w"][j], small["gdn_a_log"][j], small["gdn_dt_bias"][j],
                                   small["gdn_o_norm"][j], W("gdn_w_o", j), g_mem, f"{tag}_gdn")
        else:
            xs, hn, mix = _sc_fwd(xs, h, W("sc_w_in", j), small["sc_conv_w"][j], W("sc_w_o", j), g_mem, f"{tag}_sc")
        if before is not None:
            xs = before(i, "xa", xs)
        x_b = xs
        xq = mm(hn, W("xa_w_q", i), "nn", f"{tag}_xa_q", outs=(BF16,))
        xkv = mm(mem_n, W("xa_w_kv", i), "nn", f"{tag}_xa_kv", outs=(BF16,))
        xo, xlse = flash_fwd(_XA_CFG, xq, xkv, xkv, None, f"{tag}_xa_attn")
        xs, hm = residual_norm(xo, W("xa_w_o", i), xs, small["norm_mlp"][i], f"{tag}_xa_out")
        x_c = xs
        h1, act = mm(hm, W("mlp_w1", i), "nn", f"{tag}_mlp_up", outs=(BF16, BF16), epi=_epi_relu2)
        xs, h_pre = residual_norm(act, W("mlp_w2", i), xs, small["norm_mix"][i + 1] if i + 1 < depth else None,
                                  f"{tag}_mlp_down", tm=512)
        saved.append((x_a, h, mix, x_b, hn, xq, xkv, xo, xlse, x_c, hm, h1, act))

    se, dx, d_final = loss_head(xs, small["final_norm"], target)
    dxb = dx.astype(BF16)

    def hooked(i, stage, dx, dxb):
        new = dx if after_bwd is None else after_bwd(i, stage, dx)
        return (dx, dxb) if new is dx else (new, new.astype(BF16))

    per_layer = {n: [None] * depth for n in ("norm_mix", "norm_mem", "norm_mlp")}
    mixer = {}
    dmem_n = jnp.zeros(mem.shape, F32)
    for i in reversed(range(depth)):
        j, kind = i // 3, i % 3
        tag = f"l{i}"
        x_a, h, mix, x_b, hn, xq, xkv, xo, xlse, x_c, hm, h1, act = saved[i]
        mm(act, dxb, "tn", f"{tag}_mlp_dw2", outs=(BF16,), out_loc=G("mlp_w2", i))
        dh1 = mm(dxb, W("mlp_w2", i), "nt", f"{tag}_mlp_dh1", outs=(BF16,), epi=_epi_relu2_bwd, extras=(h1,))
        mm(hm, dh1, "tn", f"{tag}_mlp_dw1", outs=(BF16,), out_loc=G("mlp_w1", i))
        dx, dxb, dg = mm(dh1, W("mlp_w1", i), "nt", f"{tag}_mlp_dhm", outs=_NORM_BWD_OUTS, epi=_epi_norm_bwd, extras=(x_c, dx),
                         vecs=(small["norm_mlp"][i],), row_outs=1, tm=512)
        per_layer["norm_mlp"][i] = dg.reshape(-1)
        dx, dxb = hooked(i, "mlp", dx, dxb)
        mm(xo, dxb, "tn", f"{tag}_xa_dwo", outs=(BF16,), out_loc=G("xa_w_o", i))
        dxo = mm(dxb, W("xa_w_o", i), "nt", f"{tag}_xa_do", outs=(BF16,))
        dxq, xdelta = flash_dq(_XA_CFG, xq, xkv, xkv, None, xo, dxo, xlse, BF16, f"{tag}_xa_attn_dq")
        (dxkv,) = flash_dkv(_XA_CFG, xq, xkv, xkv, None, dxo, xlse, xdelta, BF16, f"{tag}_xa_attn_dkv")
        mm(hn, dxq, "tn", f"{tag}_xa_dwq", outs=(BF16,), out_loc=G("xa_w_q", i))
        dx, dxb, dg = mm(dxq, W("xa_w_q", i), "nt", f"{tag}_xa_dhn", outs=_NORM_BWD_OUTS, epi=_epi_norm_bwd, extras=(x_b, dx),
                         vecs=(small["norm_mem"][i],), row_outs=1, tm=512)
        per_layer["norm_mem"][i] = dg.reshape(-1)
        mm(mem_n, dxkv, "tn", f"{tag}_xa_dwkv", outs=(BF16,), out_loc=G("xa_w_kv", i))
        dmem_n = mm(dxkv, W("xa_w_kv", i), "nt", f"{tag}_xa_dmem", epi=_epi_add, extras=(dmem_n,))
        dx, dxb = hooked(i, "xa", dx, dxb)
        if kind == 0:
            dh, gr = _mla_bwd(dxb, h, mla_w[j], W("mla_w_o", j), G("mla_w_o", j), small["mla_q_norm"][j], small["mla_kv_norm"][j],
                              tabs, mix, f"{tag}_mla")
        elif kind == 1:
            dh, gr = _gdn_bwd(dxb, h, gdn_in_c[j], small["gdn_conv_w"][j], small["gdn_o_norm"][j], W("gdn_w_o", j), G("gdn_w_o", j),
                              mix, f"{tag}_gdn")
        else:
            dh, gr = _sc_bwd(dxb, h, W("sc_w_in", j), G("sc_w_in", j), small["sc_conv_w"][j], W("sc_w_o", j), G("sc_w_o", j),
                             mix, f"{tag}_sc")
        if kind == 1:
            lay.put_full(gslabs, "gdn_w_in", gr.pop("gdn_w_in")[None])
        for n, g in gr.items():
            mixer.setdefault(n, {})[j] = g
        dz_mix, w_mix = dh
        dx, dxb, dg = mm(dz_mix, w_mix, "nt", f"{tag}_mix_dh", outs=_NORM_BWD_OUTS, epi=_epi_norm_bwd, extras=(x_a, dx),
                         vecs=(small["norm_mix"][i],), row_outs=1, tm=256 if kind == 1 else 512)
        per_layer["norm_mix"][i] = dg.reshape(-1)
        dx, dxb = hooked(i, "mix", dx, dxb)

    _, d_mem_norm = rmsnorm_bwd(mem, small["mem_norm"], dmem_n, jnp.zeros(mem.shape, F32), "mem_norm_bwd")
    grads = {n: jnp.stack(v) for n, v in per_layer.items()}
    for n, by_j in mixer.items():
        grads[n] = jnp.stack([by_j[j] for j in sorted(by_j)])
    grads["mem_norm"] = d_mem_norm
    grads["final_norm"] = d_final
    for n in ("mla_w_in", "mla_w_uq", "mla_w_ukv"):
        lay.put_full(gslabs, n, grads.pop(n))
    return se, dx, grads


def kernel(x, mem, positions, mla_w_in, mla_q_norm, mla_kv_norm, mla_w_uq, mla_w_ukv, mla_w_o, gdn_w_in, gdn_conv_w, gdn_a_log, gdn_dt_bias, gdn_o_norm, gdn_w_o, sc_w_in, sc_conv_w, sc_w_o, norm_mix, norm_mem, norm_mlp, xa_w_q, xa_w_kv, xa_w_o, mlp_w1, mlp_w2, mem_norm, final_norm, loss_target, m_mla_w_in, m_mla_q_norm, m_mla_kv_norm, m_mla_w_uq, m_mla_w_ukv, m_mla_w_o, m_gdn_w_in, m_gdn_conv_w, m_gdn_a_log, m_gdn_dt_bias, m_gdn_o_norm, m_gdn_w_o, m_sc_w_in, m_sc_conv_w, m_sc_w_o, m_norm_mix, m_norm_mem, m_norm_mlp, m_xa_w_q, m_xa_w_kv, m_xa_w_o, m_mlp_w1, m_mlp_w2, m_mem_norm, m_final_norm, v_mla_w_in, v_mla_q_norm, v_mla_kv_norm, v_mla_w_uq, v_mla_w_ukv, v_mla_w_o, v_gdn_w_in, v_gdn_conv_w, v_gdn_a_log, v_gdn_dt_bias, v_gdn_o_norm, v_gdn_w_o, v_sc_w_in, v_sc_conv_w, v_sc_w_o, v_norm_mix, v_norm_mem, v_norm_mlp, v_xa_w_q, v_xa_w_kv, v_xa_w_o, v_mlp_w1, v_mlp_w2, v_mem_norm, v_final_norm):
    given = dict(locals())
    p = {n: given[n] for n in _WEIGHTS}
    mom = {n: given["m_" + n] for n in _WEIGHTS}
    var = {n: given["v_" + n] for n in _WEIGHTS}
    split = sorted({n for members in _SLABS.values() for n, _, _, _ in members})
    lay = Layout({n: p[n].shape for n in split})
    flat2d = lambda a: a.reshape(-1, a.shape[-1])

    me = (2 * lax.axis_index("x") + lax.axis_index("y")).astype(jnp.int32)
    core = lax.axis_index("c").astype(jnp.int32)
    me1, c1, mc = me.reshape(1), core.reshape(1), jnp.stack([me, core])

    wslabs = lay.new_slabs(BF16)

    def cast_group(slabs, chip):
        for slab in slabs:
            for name, off, l0, l1, rpl in lay.members[slab]:
                cast_into(flat2d(p[name]), l0 * rpl, (l1 - l0) * rpl, wslabs[slab], off, chip, f"cast_{slab}_{name}")

    first = _GROUPS[0][0]
    cast_group(first, me1)
    small_names = [n for n, _ in _SMALL]
    words = lax.bitcast_convert_type(jnp.concatenate([p[n].reshape(-1) for n in small_names]), BF16).reshape(-1)
    words = jnp.pad(words, (0, SMALL_ROWS * SMALL_COLS - words.shape[0])).reshape(1, SMALL_ROWS, SMALL_COLS)
    small_slab = lax.dynamic_update_slice(jnp.zeros((N_CHIPS, SMALL_ROWS, SMALL_COLS), BF16), words, (me, 0, 0))

    send0, recv0, thru0, token = gather_start([wslabs[s].arr for s in first] + [small_slab], me1, "weight_gather_start_first")
    in_flight = {}
    for slabs, point in _GROUPS[1:]:
        cast_group(slabs, me1 + token[0, 0].astype(jnp.int32))
        send, recv, thru, token = gather_start([wslabs[s].arr for s in slabs], token, f"weight_gather_start_{slabs[0]}")
        in_flight[point] = (send, recv, thru, slabs)
    started_token = token
    landed = gather_wait(send0, recv0, thru0, started_token, "weight_gather_wait_first")
    gathered = gather_forward(landed, "weight_gather_forward_first")
    for s, arr in zip(first, gathered):
        wslabs[s].arr = arr

    def before(i, stage, xs):
        if (i, stage) in in_flight:
            send, recv, thru, slabs = in_flight[(i, stage)]
            landed = gather_wait(send, recv, thru, xs, f"weight_gather_wait_{slabs[0]}")
            for s, arr in zip(slabs, gather_forward(landed, f"weight_gather_forward_{slabs[0]}")):
                wslabs[s].arr = arr
        return xs

    small = {n: p[n] for n in _REPL}
    got, off = gathered[-1].reshape(N_CHIPS, -1), 0
    for n, ax in _SMALL:
        vals = lax.bitcast_convert_type(got[:, off:off + 2 * p[n].size].reshape(N_CHIPS, p[n].size, 2), F32)
        vals = vals.reshape((N_CHIPS,) + p[n].shape)
        small[n] = jnp.concatenate([vals[s] for s in range(N_CHIPS)], axis=ax)
        off += 2 * p[n].size

    gslabs = lay.new_slabs(BF16)
    complete_at = {point: slabs for slabs, point in _GROUPS[1:]}
    swapping, exchanging = {}, []

    def after_bwd(i, stage, dx):
        if (i, stage) in swapping:
            slabs, send, recv, thru = swapping.pop((i, stage))
            g, swapped = swap_wait(send, recv, thru, dx, f"grad_swap_wait_{slabs[0]}")
        elif (i, stage) in complete_at:
            slabs = complete_at[(i, stage)]
            g = [gslabs[s].arr for s in slabs]
            if (i, stage) in _SWAP_DONE:
                send, recv, thru, token = swap_start(g, c1, f"grad_swap_start_{slabs[0]}")
                swapping[_SWAP_DONE[(i, stage)]] = (slabs, send, recv, thru)
                return dx + token[0, 0]
            swapped = pair_swap_halves(g, f"grad_pair_swap_{slabs[0]}")
        else:
            return dx
        part = [pair_add(a, b, c1, f"pair_add_{s}") for a, b, s in zip(g, swapped, slabs)]
        send, recv, thru, token = exchange_start(part, c1, f"grad_exchange_start_{slabs[0]}")
        exchanging.append((slabs, send, recv, thru))
        return dx + token[0, 0]

    se, dx, sgrads = local_step(x[0], mem[0], positions.reshape(-1, 1), loss_target[0], lay, wslabs, gslabs, small,
                                before, after_bwd)
    loss = lax.psum(0.5 * jnp.sum(se) / x.shape[-1], ("x", "y", "c"))
    names, parts, received = [], [], []
    for slabs, send, recv, thru in exchanging:
        part, got = exchange_wait(send, recv, thru, dx, f"grad_exchange_wait_{slabs[0]}")
        names, parts, received = names + slabs, parts + list(part), received + list(got)

    axes = dict(_SMALL)
    small_order = small_names + _REPL
    slots = []
    for s in range(N_CHIPS):
        vals = {n: (lax.slice_in_dim(g, s * p[n].shape[axes[n]], (s + 1) * p[n].shape[axes[n]], axis=axes[n]) if n in axes else g)
                for n, g in sgrads.items()}
        slots.append(_small_pack(vals, small_order))
    g_last = [gslabs[s].arr for s in first] + [jnp.stack(slots).astype(BF16)]
    names_last = first + ["small"]
    swapped_last = pair_swap_halves(g_last, "grad_pair_swap_last")
    part_last = [pair_add(g, b, c1, f"pair_add_{s}") for g, b, s in zip(g_last, swapped_last, names_last)]
    send, recv, thru, token = exchange_start(part_last, c1, "grad_exchange_start_last")
    mc_after = mc + token[0, 0].astype(jnp.int32)
    halves = [chip_sum(q, r, mc_after, f"chip_sum_{s}") for q, r, s in zip(parts, received, names)]
    part_last, got_last = exchange_wait(send, recv, thru, list(halves), "grad_exchange_wait_last")
    halves += [chip_sum(q, r, mc, f"chip_sum_{s}") for q, r, s in zip(part_last, got_last, names_last)]
    reduced = dict(zip(names + names_last, pair_join_halves(halves)))

    res = {}
    for slab in _SLABS:
        for name, off, l0, l1, rpl in lay.members[slab]:
            res[name] = adamw(reduced[slab], off, flat2d(p[name]), flat2d(mom[name]), flat2d(var[name]), l0 * rpl, (l1 - l0) * rpl,
                              res.get(name), f"adamw_{slab}_{name}")
    for name in split:
        res[name] = [o.reshape(p[name].shape) for o in res[name]]
    sp = {k: _small_pack(d, small_order) for k, d in (("w", p), ("m", mom), ("v", var))}
    outs = adamw(reduced["small"], 0, sp["w"], sp["m"], sp["v"], 0, SMALL_ROWS, None, "adamw_small")
    unpacked = [_small_unpack(o, p, small_order) for o in outs]
    for n in small_order:
        res[n] = [u[n] for u in unpacked]
    return (loss, dx[None], *[res[n][k] for k in range(4) for n in _WEIGHTS])
```
